```python
import math
import jax, jax.numpy as jnp
from jax import lax
import numpy as np

D_MODEL = 1024
BATCH = 8
SEQ = 4096
DEPTH = 1

RWKV_WIDTH = D_MODEL // 2
RWKV_HEAD = 64
RWKV_HEADS = RWKV_WIDTH // RWKV_HEAD
DECAY_RANK = 64
AAA_RANK = 64
GATE_RANK = 128
RWKV_SPLITS = (RWKV_WIDTH, 2 * RWKV_WIDTH, 3 * RWKV_WIDTH,
               3 * RWKV_WIDTH + DECAY_RANK, 3 * RWKV_WIDTH + DECAY_RANK + AAA_RANK)
N_RWKV_COLS = 3 * RWKV_WIDTH + DECAY_RANK + AAA_RANK + GATE_RANK
LNX_EPS = 64e-5
S5_WIDTH = D_MODEL // 2
S5_GROUP = 16
S5_GROUPS = S5_WIDTH // S5_GROUP
S5_STATE = 64
STEP_MIN = 1e-3
STEP_MAX = 1e-1
N_IN_COLS = N_RWKV_COLS + S5_WIDTH + 2 * D_MODEL
D_FF = 256 * ((8 * D_MODEL // 3 + 255) // 256)
CONV_WIDTH = 3
NORM_EPS = 1e-6

kernel_name = 'hybrid_rwkv7_s5_gated_merge_convffn'


def rms_norm(x, g):
    xf = x.astype(jnp.float32)
    y = xf * lax.rsqrt(jnp.mean(xf * xf, axis=-1, keepdims=True) + NORM_EPS)
    return y.astype(x.dtype) * g


def token_shift(p, mu):
    prev = jnp.pad(p, ((0, 0), (1, 0), (0, 0)))[:, :-1]
    return p + (prev - p) * mu


def wkv7_scan(r, decay, k, v, a_vec, b_vec):
    bsz, _, nh, n = r.shape

    def step(S, inp):
        r_t, w_t, k_t, v_t, a_t, b_t = inp
        sa = jnp.einsum('bhij,bhj->bhi', S, a_t)
        S = (S * w_t[:, :, None, :] + sa[..., None] * b_t[:, :, None, :]
             + v_t[..., None] * k_t[:, :, None, :])
        return S, jnp.einsum('bhij,bhj->bhi', S, r_t)

    xs = tuple(jnp.moveaxis(t, 1, 0) for t in (r, decay, k, v, a_vec, b_vec))
    S0 = jnp.zeros((bsz, nh, n, n), jnp.float32)
    _, ys = lax.scan(step, S0, xs)
    return jnp.moveaxis(ys, 0, 1)


def rwkv7_branch(p, mu, w0, w2, a0, a2, g2, k_k, k_a, r_k, lnx_w, lnx_b):
    dtype = p.dtype
    f32 = jnp.float32
    bsz, L, _ = p.shape
    p = token_shift(p.astype(f32), mu.astype(f32))
    r, k, v, wd, ad, gd = jnp.split(p, RWKV_SPLITS, axis=-1)
    w = -jax.nn.softplus(-(w0 + jnp.tanh(wd) @ w2)) - 0.5
    decay = jnp.exp(-jnp.exp(w))
    a = jax.nn.sigmoid(a0 + ad @ a2)
    g = jax.nn.sigmoid(gd) @ g2
    hs = (bsz, L, RWKV_HEADS, RWKV_HEAD)
    kk = (k * k_k).reshape(hs)
    kk = kk / jnp.maximum(jnp.linalg.norm(kk, axis=-1, keepdims=True), 1e-12)
    k = k * (1.0 + (a - 1.0) * k_a)
    rh, kh, vh = r.reshape(hs), k.reshape(hs), v.reshape(hs)
    ah = a.reshape(hs)
    y = wkv7_scan(rh, decay.reshape(hs), kh, vh, -kk, kk * ah)
    mean = jnp.mean(y, axis=-1, keepdims=True)
    var = jnp.mean(jnp.square(y - mean), axis=-1, keepdims=True)
    y = ((y - mean) * lax.rsqrt(var + LNX_EPS)).reshape(bsz, L, RWKV_WIDTH) * lnx_w + lnx_b
    bonus = jnp.sum(rh * kh * r_k, axis=-1, keepdims=True) * vh
    y = (y + bonus.reshape(bsz, L, RWKV_WIDTH)) * g
    return y.astype(dtype)


def s5_combine(left, right):
    a_i, b_i = left
    a_j, b_j = right
    return a_j * a_i, a_j * b_i + b_j


def s5_branch(u, a_re, a_im, b_re, b_im, c_re, c_im, d, log_step, w_glu, b_glu):
    f32 = jnp.float32
    bsz, L, _ = u.shape
    lam = lax.complex(a_re.astype(f32), a_im.astype(f32))
    dt = jnp.exp(log_step.astype(f32))[:, None]
    a_bar = jnp.exp(lam * dt)
    b_bar = ((a_bar - 1.0) / lam)[..., None] * lax.complex(b_re.astype(f32), b_im.astype(f32))
    c = lax.complex(c_re.astype(f32), c_im.astype(f32))
    ug = u.astype(f32).reshape(bsz, L, S5_GROUPS, S5_GROUP)
    bu = jnp.einsum('gpc,blgc->blgp', b_bar, ug.astype(jnp.complex64))
    a_elems = jnp.broadcast_to(a_bar, (1, L, S5_GROUPS, S5_STATE))
    _, states = lax.associative_scan(s5_combine, (a_elems, bu), axis=1)
    y = jnp.real(jnp.einsum('gcp,blgp->blgc', c, states)) + d.astype(f32).reshape(S5_GROUPS, S5_GROUP) * ug
    y = jax.nn.gelu(y.reshape(bsz, L, S5_WIDTH)).astype(u.dtype)
    return y * jax.nn.sigmoid(y @ w_glu + b_glu)


def conv_ffn(h, w_up, conv_w, conv_b, w_down):
    L = h.shape[1]
    z = h @ w_up
    zp = jnp.pad(z, ((0, 0), (CONV_WIDTH - 1, 0), (0, 0)))
    z = conv_b + sum(conv_w[j] * zp[:, j:j + L] for j in range(CONV_WIDTH))
    gate, val = jnp.split(z, 2, axis=-1)
    return (jax.nn.gelu(gate) * val) @ w_down


def _fwd_setup_inputs(seed: int = 0) -> dict:
    key = jax.random.key(seed)
    ks = jax.random.split(key, 40)
    f32 = jnp.float32
    Ld, W, G, P, C, F = DEPTH, RWKV_WIDTH, S5_GROUPS, S5_STATE, S5_GROUP, D_FF

    def nrm(k, shape, scale):
        return jax.random.normal(k, shape, f32) * scale

    def gain(k, shape):
        return 1.0 + 0.05 * jax.random.normal(k, shape, f32)

    n = jnp.arange(P, dtype=f32)
    return {
        'x': nrm(ks[0], (BATCH, SEQ, D_MODEL), 1.0),
        'norm_mix_pre': gain(ks[1], (Ld, D_MODEL)),
        'norm_mix_post': gain(ks[2], (Ld, D_MODEL)),
        'norm_ffn_pre': gain(ks[3], (Ld, D_MODEL)),
        'norm_ffn_post': gain(ks[4], (Ld, D_MODEL)),
        'w_in': nrm(ks[5], (Ld, D_MODEL, N_IN_COLS), D_MODEL ** -0.5),
        'b_gate': nrm(ks[6], (Ld, 2 * D_MODEL), 0.02),
        'rwkv_shift_mu': jax.random.uniform(ks[7], (Ld, N_RWKV_COLS), f32, 0.0, 1.0),
        'rwkv_w0': jax.random.uniform(ks[8], (Ld, W), f32, -6.0, -1.0),
        'rwkv_w2': nrm(ks[9], (Ld, DECAY_RANK, W), 0.1 * DECAY_RANK ** -0.5),
        'rwkv_a0': nrm(ks[10], (Ld, W), 0.1),
        'rwkv_a2': nrm(ks[11], (Ld, AAA_RANK, W), 0.1 * AAA_RANK ** -0.5),
        'rwkv_g2': nrm(ks[12], (Ld, GATE_RANK, W), GATE_RANK ** -0.5),
        'rwkv_k_k': 0.85 + 0.05 * jax.random.normal(ks[13], (Ld, W), f32),
        'rwkv_k_a': gain(ks[14], (Ld, W)),
        'rwkv_r_k': nrm(ks[15], (Ld, RWKV_HEADS, RWKV_HEAD), 0.3),
        'rwkv_lnx_w': gain(ks[16], (Ld, W)),
        'rwkv_lnx_b': nrm(ks[17], (Ld, W), 0.02),
        's5_a_re': -0.5 + 0.01 * jax.random.normal(ks[18], (Ld, G, P), f32),
        's5_a_im': math.pi * n + 0.01 * jax.random.normal(ks[19], (Ld, G, P), f32),
        's5_b_re': nrm(ks[20], (Ld, G, P, C), (2 * C) ** -0.5),
        's5_b_im': nrm(ks[21], (Ld, G, P, C), (2 * C) ** -0.5),
        's5_c_re': nrm(ks[22], (Ld, G, C, P), P ** -0.5),
        's5_c_im': nrm(ks[23], (Ld, G, C, P), P ** -0.5),
        's5_d': nrm(ks[24], (Ld, S5_WIDTH), 1.0),
        's5_log_step': jax.random.uniform(ks[25], (Ld, G), f32, math.log(STEP_MIN), math.log(STEP_MAX)),
        's5_w_glu': nrm(ks[26], (Ld, S5_WIDTH, S5_WIDTH), S5_WIDTH ** -0.5),
        's5_b_glu': nrm(ks[27], (Ld, S5_WIDTH), 0.02),
        'w_branch_rwkv': nrm(ks[28], (Ld, W, D_MODEL), W ** -0.5),
        'w_branch_s5': nrm(ks[29], (Ld, S5_WIDTH, D_MODEL), S5_WIDTH ** -0.5),
        'w_out': nrm(ks[30], (Ld, D_MODEL, D_MODEL), D_MODEL ** -0.5),
        'ffn_w_up': nrm(ks[31], (Ld, D_MODEL, 2 * F), D_MODEL ** -0.5),
        'ffn_conv_w': nrm(ks[32], (Ld, CONV_WIDTH, 2 * F), CONV_WIDTH ** -0.5),
        'ffn_conv_b': nrm(ks[33], (Ld, 2 * F), 0.02),
        'ffn_w_down': nrm(ks[34], (Ld, F, D_MODEL), F ** -0.5),
    }


def _fwd_reference(x, norm_mix_pre, norm_mix_post, norm_ffn_pre, norm_ffn_post, w_in, b_gate,
              rwkv_shift_mu, rwkv_w0, rwkv_w2, rwkv_a0, rwkv_a2, rwkv_g2, rwkv_k_k, rwkv_k_a,
              rwkv_r_k, rwkv_lnx_w, rwkv_lnx_b, s5_a_re, s5_a_im, s5_b_re, s5_b_im, s5_c_re,
              s5_c_im, s5_d, s5_log_step, s5_w_glu, s5_b_glu, w_branch_rwkv, w_branch_s5, w_out,
              ffn_w_up, ffn_conv_w, ffn_conv_b, ffn_w_down):
    for l in range(DEPTH):
        h = rms_norm(x, norm_mix_pre[l])
        proj = h @ w_in[l]
        p_rwkv = proj[..., :N_RWKV_COLS]
        u_s5 = proj[..., N_RWKV_COLS:N_RWKV_COLS + S5_WIDTH]
        gates = jax.nn.sigmoid(proj[..., N_RWKV_COLS + S5_WIDTH:] + b_gate[l])
        g_rwkv, g_s5 = jnp.split(gates, 2, axis=-1)
        o_rwkv = rwkv7_branch(p_rwkv, rwkv_shift_mu[l], rwkv_w0[l], rwkv_w2[l], rwkv_a0[l],
                              rwkv_a2[l], rwkv_g2[l], rwkv_k_k[l], rwkv_k_a[l], rwkv_r_k[l],
                              rwkv_lnx_w[l], rwkv_lnx_b[l]) @ w_branch_rwkv[l]
        o_s5 = s5_branch(u_s5, s5_a_re[l], s5_a_im[l], s5_b_re[l], s5_b_im[l], s5_c_re[l],
                         s5_c_im[l], s5_d[l], s5_log_step[l], s5_w_glu[l], s5_b_glu[l]) @ w_branch_s5[l]
        mixed = (g_rwkv * o_rwkv + g_s5 * o_s5) @ w_out[l]
        x = x + rms_norm(mixed, norm_mix_post[l])
        h = rms_norm(x, norm_ffn_pre[l])
        f = conv_ffn(h, ffn_w_up[l], ffn_conv_w[l], ffn_conv_b[l], ffn_w_down[l])
        x = x + rms_norm(f, norm_ffn_post[l])
    return x


import jax as _jax
import jax.numpy as _jnp

TWIN_FORMAT = 'train_step'
FWD_PARAMS = ['x', 'norm_mix_pre', 'norm_mix_post', 'norm_ffn_pre', 'norm_ffn_post', 'w_in', 'b_gate', 'rwkv_shift_mu', 'rwkv_w0', 'rwkv_w2', 'rwkv_a0', 'rwkv_a2', 'rwkv_g2', 'rwkv_k_k', 'rwkv_k_a', 'rwkv_r_k', 'rwkv_lnx_w', 'rwkv_lnx_b', 's5_a_re', 's5_a_im', 's5_b_re', 's5_b_im', 's5_c_re', 's5_c_im', 's5_d', 's5_log_step', 's5_w_glu', 's5_b_glu', 'w_branch_rwkv', 'w_branch_s5', 'w_out', 'ffn_w_up', 'ffn_conv_w', 'ffn_conv_b', 'ffn_w_down']
TWIN_WEIGHTS = ['norm_mix_pre', 'norm_mix_post', 'norm_ffn_pre', 'norm_ffn_post', 'w_in', 'b_gate', 'rwkv_shift_mu', 'rwkv_w0', 'rwkv_w2', 'rwkv_a0', 'rwkv_a2', 'rwkv_g2', 'rwkv_k_k', 'rwkv_k_a', 'rwkv_r_k', 'rwkv_lnx_w', 'rwkv_lnx_b', 's5_a_re', 's5_a_im', 's5_b_re', 's5_b_im', 's5_c_re', 's5_c_im', 's5_d', 's5_log_step', 's5_w_glu', 's5_b_glu', 'w_branch_rwkv', 'w_branch_s5', 'w_out', 'ffn_w_up', 'ffn_conv_w', 'ffn_conv_b', 'ffn_w_down']
TWIN_DIFF_INPUT = 'x'
TWIN_INPUTS = ['x', 'norm_mix_pre', 'norm_mix_post', 'norm_ffn_pre', 'norm_ffn_post', 'w_in', 'b_gate', 'rwkv_shift_mu', 'rwkv_w0', 'rwkv_w2', 'rwkv_a0', 'rwkv_a2', 'rwkv_g2', 'rwkv_k_k', 'rwkv_k_a', 'rwkv_r_k', 'rwkv_lnx_w', 'rwkv_lnx_b', 's5_a_re', 's5_a_im', 's5_b_re', 's5_b_im', 's5_c_re', 's5_c_im', 's5_d', 's5_log_step', 's5_w_glu', 's5_b_glu', 'w_branch_rwkv', 'w_branch_s5', 'w_out', 'ffn_w_up', 'ffn_conv_w', 'ffn_conv_b', 'ffn_w_down', 'loss_target', 'm_norm_mix_pre', 'm_norm_mix_post', 'm_norm_ffn_pre', 'm_norm_ffn_post', 'm_w_in', 'm_b_gate', 'm_rwkv_shift_mu', 'm_rwkv_w0', 'm_rwkv_w2', 'm_rwkv_a0', 'm_rwkv_a2', 'm_rwkv_g2', 'm_rwkv_k_k', 'm_rwkv_k_a', 'm_rwkv_r_k', 'm_rwkv_lnx_w', 'm_rwkv_lnx_b', 'm_s5_a_re', 'm_s5_a_im', 'm_s5_b_re', 'm_s5_b_im', 'm_s5_c_re', 'm_s5_c_im', 'm_s5_d', 'm_s5_log_step', 'm_s5_w_glu', 'm_s5_b_glu', 'm_w_branch_rwkv', 'm_w_branch_s5', 'm_w_out', 'm_ffn_w_up', 'm_ffn_conv_w', 'm_ffn_conv_b', 'm_ffn_w_down', 'v_norm_mix_pre', 'v_norm_mix_post', 'v_norm_ffn_pre', 'v_norm_ffn_post', 'v_w_in', 'v_b_gate', 'v_rwkv_shift_mu', 'v_rwkv_w0', 'v_rwkv_w2', 'v_rwkv_a0', 'v_rwkv_a2', 'v_rwkv_g2', 'v_rwkv_k_k', 'v_rwkv_k_a', 'v_rwkv_r_k', 'v_rwkv_lnx_w', 'v_rwkv_lnx_b', 'v_s5_a_re', 'v_s5_a_im', 'v_s5_b_re', 'v_s5_b_im', 'v_s5_c_re', 'v_s5_c_im', 'v_s5_d', 'v_s5_log_step', 'v_s5_w_glu', 'v_s5_b_glu', 'v_w_branch_rwkv', 'v_w_branch_s5', 'v_w_out', 'v_ffn_w_up', 'v_ffn_conv_w', 'v_ffn_conv_b', 'v_ffn_w_down']
TWIN_OUTPUTS = ['loss', 'grad_x', 'grad_norm_mix_pre', 'grad_norm_mix_post', 'grad_norm_ffn_pre', 'grad_norm_ffn_post', 'grad_w_in', 'grad_b_gate', 'grad_rwkv_shift_mu', 'grad_rwkv_w0', 'grad_rwkv_w2', 'grad_rwkv_a0', 'grad_rwkv_a2', 'grad_rwkv_g2', 'grad_rwkv_k_k', 'grad_rwkv_k_a', 'grad_rwkv_r_k', 'grad_rwkv_lnx_w', 'grad_rwkv_lnx_b', 'grad_s5_a_re', 'grad_s5_a_im', 'grad_s5_b_re', 'grad_s5_b_im', 'grad_s5_c_re', 'grad_s5_c_im', 'grad_s5_d', 'grad_s5_log_step', 'grad_s5_w_glu', 'grad_s5_b_glu', 'grad_w_branch_rwkv', 'grad_w_branch_s5', 'grad_w_out', 'grad_ffn_w_up', 'grad_ffn_conv_w', 'grad_ffn_conv_b', 'grad_ffn_w_down', 'delta_norm_mix_pre', 'delta_norm_mix_post', 'delta_norm_ffn_pre', 'delta_norm_ffn_post', 'delta_w_in', 'delta_b_gate', 'delta_rwkv_shift_mu', 'delta_rwkv_w0', 'delta_rwkv_w2', 'delta_rwkv_a0', 'delta_rwkv_a2', 'delta_rwkv_g2', 'delta_rwkv_k_k', 'delta_rwkv_k_a', 'delta_rwkv_r_k', 'delta_rwkv_lnx_w', 'delta_rwkv_lnx_b', 'delta_s5_a_re', 'delta_s5_a_im', 'delta_s5_b_re', 'delta_s5_b_im', 'delta_s5_c_re', 'delta_s5_c_im', 'delta_s5_d', 'delta_s5_log_step', 'delta_s5_w_glu', 'delta_s5_b_glu', 'delta_w_branch_rwkv', 'delta_w_branch_s5', 'delta_w_out', 'delta_ffn_w_up', 'delta_ffn_conv_w', 'delta_ffn_conv_b', 'delta_ffn_w_down', 'new_m_norm_mix_pre', 'new_m_norm_mix_post', 'new_m_norm_ffn_pre', 'new_m_norm_ffn_post', 'new_m_w_in', 'new_m_b_gate', 'new_m_rwkv_shift_mu', 'new_m_rwkv_w0', 'new_m_rwkv_w2', 'new_m_rwkv_a0', 'new_m_rwkv_a2', 'new_m_rwkv_g2', 'new_m_rwkv_k_k', 'new_m_rwkv_k_a', 'new_m_rwkv_r_k', 'new_m_rwkv_lnx_w', 'new_m_rwkv_lnx_b', 'new_m_s5_a_re', 'new_m_s5_a_im', 'new_m_s5_b_re', 'new_m_s5_b_im', 'new_m_s5_c_re', 'new_m_s5_c_im', 'new_m_s5_d', 'new_m_s5_log_step', 'new_m_s5_w_glu', 'new_m_s5_b_glu', 'new_m_w_branch_rwkv', 'new_m_w_branch_s5', 'new_m_w_out', 'new_m_ffn_w_up', 'new_m_ffn_conv_w', 'new_m_ffn_conv_b', 'new_m_ffn_w_down', 'new_v_norm_mix_pre', 'new_v_norm_mix_post', 'new_v_norm_ffn_pre', 'new_v_norm_ffn_post', 'new_v_w_in', 'new_v_b_gate', 'new_v_rwkv_shift_mu', 'new_v_rwkv_w0', 'new_v_rwkv_w2', 'new_v_rwkv_a0', 'new_v_rwkv_a2', 'new_v_rwkv_g2', 'new_v_rwkv_k_k', 'new_v_rwkv_k_a', 'new_v_rwkv_r_k', 'new_v_rwkv_lnx_w', 'new_v_rwkv_lnx_b', 'new_v_s5_a_re', 'new_v_s5_a_im', 'new_v_s5_b_re', 'new_v_s5_b_im', 'new_v_s5_c_re', 'new_v_s5_c_im', 'new_v_s5_d', 'new_v_s5_log_step', 'new_v_s5_w_glu', 'new_v_s5_b_glu', 'new_v_w_branch_rwkv', 'new_v_w_branch_s5', 'new_v_w_out', 'new_v_ffn_w_up', 'new_v_ffn_conv_w', 'new_v_ffn_conv_b', 'new_v_ffn_w_down']
TWIN_LEAF_KINDS = {'loss': 'loss', 'grad_x': 'grad_x', 'grad_norm_mix_pre': 'grad_w', 'grad_norm_mix_post': 'grad_w', 'grad_norm_ffn_pre': 'grad_w', 'grad_norm_ffn_post': 'grad_w', 'grad_w_in': 'grad_w', 'grad_b_gate': 'grad_w', 'grad_rwkv_shift_mu': 'grad_w', 'grad_rwkv_w0': 'grad_w', 'grad_rwkv_w2': 'grad_w', 'grad_rwkv_a0': 'grad_w', 'grad_rwkv_a2': 'grad_w', 'grad_rwkv_g2': 'grad_w', 'grad_rwkv_k_k': 'grad_w', 'grad_rwkv_k_a': 'grad_w', 'grad_rwkv_r_k': 'grad_w', 'grad_rwkv_lnx_w': 'grad_w', 'grad_rwkv_lnx_b': 'grad_w', 'grad_s5_a_re': 'grad_w', 'grad_s5_a_im': 'grad_w', 'grad_s5_b_re': 'grad_w', 'grad_s5_b_im': 'grad_w', 'grad_s5_c_re': 'grad_w', 'grad_s5_c_im': 'grad_w', 'grad_s5_d': 'grad_w', 'grad_s5_log_step': 'grad_w', 'grad_s5_w_glu': 'grad_w', 'grad_s5_b_glu': 'grad_w', 'grad_w_branch_rwkv': 'grad_w', 'grad_w_branch_s5': 'grad_w', 'grad_w_out': 'grad_w', 'grad_ffn_w_up': 'grad_w', 'grad_ffn_conv_w': 'grad_w', 'grad_ffn_conv_b': 'grad_w', 'grad_ffn_w_down': 'grad_w', 'delta_norm_mix_pre': 'delta_w', 'delta_norm_mix_post': 'delta_w', 'delta_norm_ffn_pre': 'delta_w', 'delta_norm_ffn_post': 'delta_w', 'delta_w_in': 'delta_w', 'delta_b_gate': 'delta_w', 'delta_rwkv_shift_mu': 'delta_w', 'delta_rwkv_w0': 'delta_w', 'delta_rwkv_w2': 'delta_w', 'delta_rwkv_a0': 'delta_w', 'delta_rwkv_a2': 'delta_w', 'delta_rwkv_g2': 'delta_w', 'delta_rwkv_k_k': 'delta_w', 'delta_rwkv_k_a': 'delta_w', 'delta_rwkv_r_k': 'delta_w', 'delta_rwkv_lnx_w': 'delta_w', 'delta_rwkv_lnx_b': 'delta_w', 'delta_s5_a_re': 'delta_w', 'delta_s5_a_im': 'delta_w', 'delta_s5_b_re': 'delta_w', 'delta_s5_b_im': 'delta_w', 'delta_s5_c_re': 'delta_w', 'delta_s5_c_im': 'delta_w', 'delta_s5_d': 'delta_w', 'delta_s5_log_step': 'delta_w', 'delta_s5_w_glu': 'delta_w', 'delta_s5_b_glu': 'delta_w', 'delta_w_branch_rwkv': 'delta_w', 'delta_w_branch_s5': 'delta_w', 'delta_w_out': 'delta_w', 'delta_ffn_w_up': 'delta_w', 'delta_ffn_conv_w': 'delta_w', 'delta_ffn_conv_b': 'delta_w', 'delta_ffn_w_down': 'delta_w', 'new_m_norm_mix_pre': 'new_m', 'new_m_norm_mix_post': 'new_m', 'new_m_norm_ffn_pre': 'new_m', 'new_m_norm_ffn_post': 'new_m', 'new_m_w_in': 'new_m', 'new_m_b_gate': 'new_m', 'new_m_rwkv_shift_mu': 'new_m', 'new_m_rwkv_w0': 'new_m', 'new_m_rwkv_w2': 'new_m', 'new_m_rwkv_a0': 'new_m', 'new_m_rwkv_a2': 'new_m', 'new_m_rwkv_g2': 'new_m', 'new_m_rwkv_k_k': 'new_m', 'new_m_rwkv_k_a': 'new_m', 'new_m_rwkv_r_k': 'new_m', 'new_m_rwkv_lnx_w': 'new_m', 'new_m_rwkv_lnx_b': 'new_m', 'new_m_s5_a_re': 'new_m', 'new_m_s5_a_im': 'new_m', 'new_m_s5_b_re': 'new_m', 'new_m_s5_b_im': 'new_m', 'new_m_s5_c_re': 'new_m', 'new_m_s5_c_im': 'new_m', 'new_m_s5_d': 'new_m', 'new_m_s5_log_step': 'new_m', 'new_m_s5_w_glu': 'new_m', 'new_m_s5_b_glu': 'new_m', 'new_m_w_branch_rwkv': 'new_m', 'new_m_w_branch_s5': 'new_m', 'new_m_w_out': 'new_m', 'new_m_ffn_w_up': 'new_m', 'new_m_ffn_conv_w': 'new_m', 'new_m_ffn_conv_b': 'new_m', 'new_m_ffn_w_down': 'new_m', 'new_v_norm_mix_pre': 'new_v', 'new_v_norm_mix_post': 'new_v', 'new_v_norm_ffn_pre': 'new_v', 'new_v_norm_ffn_post': 'new_v', 'new_v_w_in': 'new_v', 'new_v_b_gate': 'new_v', 'new_v_rwkv_shift_mu': 'new_v', 'new_v_rwkv_w0': 'new_v', 'new_v_rwkv_w2': 'new_v', 'new_v_rwkv_a0': 'new_v', 'new_v_rwkv_a2': 'new_v', 'new_v_rwkv_g2': 'new_v', 'new_v_rwkv_k_k': 'new_v', 'new_v_rwkv_k_a': 'new_v', 'new_v_rwkv_r_k': 'new_v', 'new_v_rwkv_lnx_w': 'new_v', 'new_v_rwkv_lnx_b': 'new_v', 'new_v_s5_a_re': 'new_v', 'new_v_s5_a_im': 'new_v', 'new_v_s5_b_re': 'new_v', 'new_v_s5_b_im': 'new_v', 'new_v_s5_c_re': 'new_v', 'new_v_s5_c_im': 'new_v', 'new_v_s5_d': 'new_v', 'new_v_s5_log_step': 'new_v', 'new_v_s5_w_glu': 'new_v', 'new_v_s5_b_glu': 'new_v', 'new_v_w_branch_rwkv': 'new_v', 'new_v_w_branch_s5': 'new_v', 'new_v_w_out': 'new_v', 'new_v_ffn_w_up': 'new_v', 'new_v_ffn_conv_w': 'new_v', 'new_v_ffn_conv_b': 'new_v', 'new_v_ffn_w_down': 'new_v'}


def _forward(args):
    return _fwd_reference(*[args[k] for k in FWD_PARAMS])


def _output_shape():
    out = _jax.eval_shape(lambda: _forward(_fwd_setup_inputs(0)))
    return out.shape, out.dtype

N_MICROBATCH = 1
ADAM_LR = 0.001
ADAM_B1 = 0.9
ADAM_B2 = 0.999
ADAM_EPS = 1e-08
ADAM_WD = 0.01
ADAM_STEP = 10
PER_EXAMPLE_BATCH_AXIS = {'x': 0, 'loss_target': 0}
SHARED_INPUTS = []
_WEIGHT_DTYPES = {'norm_mix_pre': _jnp.float32, 'norm_mix_post': _jnp.float32, 'norm_ffn_pre': _jnp.float32, 'norm_ffn_post': _jnp.float32, 'w_in': _jnp.float32, 'b_gate': _jnp.float32, 'rwkv_shift_mu': _jnp.float32, 'rwkv_w0': _jnp.float32, 'rwkv_w2': _jnp.float32, 'rwkv_a0': _jnp.float32, 'rwkv_a2': _jnp.float32, 'rwkv_g2': _jnp.float32, 'rwkv_k_k': _jnp.float32, 'rwkv_k_a': _jnp.float32, 'rwkv_r_k': _jnp.float32, 'rwkv_lnx_w': _jnp.float32, 'rwkv_lnx_b': _jnp.float32, 's5_a_re': _jnp.float32, 's5_a_im': _jnp.float32, 's5_b_re': _jnp.float32, 's5_b_im': _jnp.float32, 's5_c_re': _jnp.float32, 's5_c_im': _jnp.float32, 's5_d': _jnp.float32, 's5_log_step': _jnp.float32, 's5_w_glu': _jnp.float32, 's5_b_glu': _jnp.float32, 'w_branch_rwkv': _jnp.float32, 'w_branch_s5': _jnp.float32, 'w_out': _jnp.float32, 'ffn_w_up': _jnp.float32, 'ffn_conv_w': _jnp.float32, 'ffn_conv_b': _jnp.float32, 'ffn_w_down': _jnp.float32}
MOMENT_SCALE = {'norm_mix_pre': 9.092445e-01, 'norm_mix_post': 3.208196e+01, 'norm_ffn_pre': 6.724648e-01, 'norm_ffn_post': 3.189390e+01, 'w_in': 3.799523e-01, 'b_gate': 2.149067e-01, 'rwkv_shift_mu': 8.919104e-01, 'rwkv_w0': 2.288621e-01, 'rwkv_w2': 2.186712e-02, 'rwkv_a0': 2.655959e-01, 'rwkv_a2': 2.337208e-01, 'rwkv_g2': 5.882931e-01, 'rwkv_k_k': 7.769370e-01, 'rwkv_k_a': 6.053046e-01, 'rwkv_r_k': 1.003778e+00, 'rwkv_lnx_w': 7.327882e-01, 'rwkv_lnx_b': 2.334152e+00, 's5_a_re': 2.717278e-02, 's5_a_im': 2.351556e-02, 's5_b_re': 1.637155e-02, 's5_b_im': 1.606495e-02, 's5_c_re': 2.315897e-02, 's5_c_im': 2.294594e-02, 's5_d': 9.221251e-01, 's5_log_step': 3.084429e+01, 's5_w_glu': 1.258202e-01, 's5_b_glu': 3.248969e-01, 'w_branch_rwkv': 4.686269e-01, 'w_branch_s5': 5.545695e-01, 'w_out': 7.062190e-01, 'ffn_w_up': 2.646145e-01, 'ffn_conv_w': 2.952056e-01, 'ffn_conv_b': 1.266465e+00, 'ffn_w_down': 5.270672e-01}


def _to_microbatches(a, axis):
    t = _jnp.moveaxis(a, axis, 0)
    t = t.reshape((N_MICROBATCH, t.shape[0] // N_MICROBATCH) + t.shape[1:])
    return _jnp.moveaxis(t, 1, axis + 1)


def setup_inputs(seed: int = 0) -> dict:
    inp = _fwd_setup_inputs(seed)
    key = _jax.random.fold_in(_jax.random.key(seed), 7919)
    shape, _ = _output_shape()
    out = dict(inp)
    out["loss_target"] = _jax.random.normal(_jax.random.fold_in(key, 0), shape, _jnp.float32)
    for i, name in enumerate(TWIN_WEIGHTS):
        w = inp[name].astype(_jnp.float32)
        if MOMENT_SCALE is None:
            s = _jnp.sqrt(_jnp.mean(_jnp.square(w)) + 1e-30)
        else:
            s = MOMENT_SCALE[name]
        km, kv = _jax.random.split(_jax.random.fold_in(key, i + 1))
        out[name] = w
        out["m_" + name] = s * _jax.random.normal(km, w.shape, _jnp.float32)
        out["v_" + name] = (s * s) * _jax.random.uniform(kv, w.shape, _jnp.float32, 0.5, 1.5)
    if N_MICROBATCH > 1:
        for name, axis in PER_EXAMPLE_BATCH_AXIS.items():
            out[name] = _to_microbatches(out[name], axis)
    return {'x': out['x'], 'norm_mix_pre': out['norm_mix_pre'], 'norm_mix_post': out['norm_mix_post'], 'norm_ffn_pre': out['norm_ffn_pre'], 'norm_ffn_post': out['norm_ffn_post'], 'w_in': out['w_in'], 'b_gate': out['b_gate'], 'rwkv_shift_mu': out['rwkv_shift_mu'], 'rwkv_w0': out['rwkv_w0'], 'rwkv_w2': out['rwkv_w2'], 'rwkv_a0': out['rwkv_a0'], 'rwkv_a2': out['rwkv_a2'], 'rwkv_g2': out['rwkv_g2'], 'rwkv_k_k': out['rwkv_k_k'], 'rwkv_k_a': out['rwkv_k_a'], 'rwkv_r_k': out['rwkv_r_k'], 'rwkv_lnx_w': out['rwkv_lnx_w'], 'rwkv_lnx_b': out['rwkv_lnx_b'], 's5_a_re': out['s5_a_re'], 's5_a_im': out['s5_a_im'], 's5_b_re': out['s5_b_re'], 's5_b_im': out['s5_b_im'], 's5_c_re': out['s5_c_re'], 's5_c_im': out['s5_c_im'], 's5_d': out['s5_d'], 's5_log_step': out['s5_log_step'], 's5_w_glu': out['s5_w_glu'], 's5_b_glu': out['s5_b_glu'], 'w_branch_rwkv': out['w_branch_rwkv'], 'w_branch_s5': out['w_branch_s5'], 'w_out': out['w_out'], 'ffn_w_up': out['ffn_w_up'], 'ffn_conv_w': out['ffn_conv_w'], 'ffn_conv_b': out['ffn_conv_b'], 'ffn_w_down': out['ffn_w_down'], 'loss_target': out['loss_target'], 'm_norm_mix_pre': out['m_norm_mix_pre'], 'm_norm_mix_post': out['m_norm_mix_post'], 'm_norm_ffn_pre': out['m_norm_ffn_pre'], 'm_norm_ffn_post': out['m_norm_ffn_post'], 'm_w_in': out['m_w_in'], 'm_b_gate': out['m_b_gate'], 'm_rwkv_shift_mu': out['m_rwkv_shift_mu'], 'm_rwkv_w0': out['m_rwkv_w0'], 'm_rwkv_w2': out['m_rwkv_w2'], 'm_rwkv_a0': out['m_rwkv_a0'], 'm_rwkv_a2': out['m_rwkv_a2'], 'm_rwkv_g2': out['m_rwkv_g2'], 'm_rwkv_k_k': out['m_rwkv_k_k'], 'm_rwkv_k_a': out['m_rwkv_k_a'], 'm_rwkv_r_k': out['m_rwkv_r_k'], 'm_rwkv_lnx_w': out['m_rwkv_lnx_w'], 'm_rwkv_lnx_b': out['m_rwkv_lnx_b'], 'm_s5_a_re': out['m_s5_a_re'], 'm_s5_a_im': out['m_s5_a_im'], 'm_s5_b_re': out['m_s5_b_re'], 'm_s5_b_im': out['m_s5_b_im'], 'm_s5_c_re': out['m_s5_c_re'], 'm_s5_c_im': out['m_s5_c_im'], 'm_s5_d': out['m_s5_d'], 'm_s5_log_step': out['m_s5_log_step'], 'm_s5_w_glu': out['m_s5_w_glu'], 'm_s5_b_glu': out['m_s5_b_glu'], 'm_w_branch_rwkv': out['m_w_branch_rwkv'], 'm_w_branch_s5': out['m_w_branch_s5'], 'm_w_out': out['m_w_out'], 'm_ffn_w_up': out['m_ffn_w_up'], 'm_ffn_conv_w': out['m_ffn_conv_w'], 'm_ffn_conv_b': out['m_ffn_conv_b'], 'm_ffn_w_down': out['m_ffn_w_down'], 'v_norm_mix_pre': out['v_norm_mix_pre'], 'v_norm_mix_post': out['v_norm_mix_post'], 'v_norm_ffn_pre': out['v_norm_ffn_pre'], 'v_norm_ffn_post': out['v_norm_ffn_post'], 'v_w_in': out['v_w_in'], 'v_b_gate': out['v_b_gate'], 'v_rwkv_shift_mu': out['v_rwkv_shift_mu'], 'v_rwkv_w0': out['v_rwkv_w0'], 'v_rwkv_w2': out['v_rwkv_w2'], 'v_rwkv_a0': out['v_rwkv_a0'], 'v_rwkv_a2': out['v_rwkv_a2'], 'v_rwkv_g2': out['v_rwkv_g2'], 'v_rwkv_k_k': out['v_rwkv_k_k'], 'v_rwkv_k_a': out['v_rwkv_k_a'], 'v_rwkv_r_k': out['v_rwkv_r_k'], 'v_rwkv_lnx_w': out['v_rwkv_lnx_w'], 'v_rwkv_lnx_b': out['v_rwkv_lnx_b'], 'v_s5_a_re': out['v_s5_a_re'], 'v_s5_a_im': out['v_s5_a_im'], 'v_s5_b_re': out['v_s5_b_re'], 'v_s5_b_im': out['v_s5_b_im'], 'v_s5_c_re': out['v_s5_c_re'], 'v_s5_c_im': out['v_s5_c_im'], 'v_s5_d': out['v_s5_d'], 'v_s5_log_step': out['v_s5_log_step'], 'v_s5_w_glu': out['v_s5_w_glu'], 'v_s5_b_glu': out['v_s5_b_glu'], 'v_w_branch_rwkv': out['v_w_branch_rwkv'], 'v_w_branch_s5': out['v_w_branch_s5'], 'v_w_out': out['v_w_out'], 'v_ffn_w_up': out['v_ffn_w_up'], 'v_ffn_conv_w': out['v_ffn_conv_w'], 'v_ffn_conv_b': out['v_ffn_conv_b'], 'v_ffn_w_down': out['v_ffn_w_down']}


def _loss(weights, diff, rest, loss_target):
    with _jax.named_scope("forward"):
        args = {**rest, TWIN_DIFF_INPUT: diff, **{k: w.astype(_WEIGHT_DTYPES[k]) for k, w in weights.items()}}
        y = _forward(args)
    with _jax.named_scope("loss_head"):
        err = _jnp.square(y.astype(_jnp.float32) - loss_target)
        return 0.5 * _jnp.sum(_jnp.mean(err, axis=-1)) if err.ndim else 0.5 * err


def _adamw(w, g, m, v):
    m = ADAM_B1 * m + (1.0 - ADAM_B1) * g
    v = ADAM_B2 * v + (1.0 - ADAM_B2) * _jnp.square(g)
    m_hat = m / (1.0 - ADAM_B1 ** ADAM_STEP)
    v_hat = v / (1.0 - ADAM_B2 ** ADAM_STEP)
    delta = -ADAM_LR * (m_hat / (_jnp.sqrt(v_hat) + ADAM_EPS) + ADAM_WD * w)
    return delta, m, v


def reference(x, norm_mix_pre, norm_mix_post, norm_ffn_pre, norm_ffn_post, w_in, b_gate, rwkv_shift_mu, rwkv_w0, rwkv_w2, rwkv_a0, rwkv_a2, rwkv_g2, rwkv_k_k, rwkv_k_a, rwkv_r_k, rwkv_lnx_w, rwkv_lnx_b, s5_a_re, s5_a_im, s5_b_re, s5_b_im, s5_c_re, s5_c_im, s5_d, s5_log_step, s5_w_glu, s5_b_glu, w_branch_rwkv, w_branch_s5, w_out, ffn_w_up, ffn_conv_w, ffn_conv_b, ffn_w_down, loss_target, m_norm_mix_pre, m_norm_mix_post, m_norm_ffn_pre, m_norm_ffn_post, m_w_in, m_b_gate, m_rwkv_shift_mu, m_rwkv_w0, m_rwkv_w2, m_rwkv_a0, m_rwkv_a2, m_rwkv_g2, m_rwkv_k_k, m_rwkv_k_a, m_rwkv_r_k, m_rwkv_lnx_w, m_rwkv_lnx_b, m_s5_a_re, m_s5_a_im, m_s5_b_re, m_s5_b_im, m_s5_c_re, m_s5_c_im, m_s5_d, m_s5_log_step, m_s5_w_glu, m_s5_b_glu, m_w_branch_rwkv, m_w_branch_s5, m_w_out, m_ffn_w_up, m_ffn_conv_w, m_ffn_conv_b, m_ffn_w_down, v_norm_mix_pre, v_norm_mix_post, v_norm_ffn_pre, v_norm_ffn_post, v_w_in, v_b_gate, v_rwkv_shift_mu, v_rwkv_w0, v_rwkv_w2, v_rwkv_a0, v_rwkv_a2, v_rwkv_g2, v_rwkv_k_k, v_rwkv_k_a, v_rwkv_r_k, v_rwkv_lnx_w, v_rwkv_lnx_b, v_s5_a_re, v_s5_a_im, v_s5_b_re, v_s5_b_im, v_s5_c_re, v_s5_c_im, v_s5_d, v_s5_log_step, v_s5_w_glu, v_s5_b_glu, v_w_branch_rwkv, v_w_branch_s5, v_w_out, v_ffn_w_up, v_ffn_conv_w, v_ffn_conv_b, v_ffn_w_down):
    given = dict(x=x, norm_mix_pre=norm_mix_pre, norm_mix_post=norm_mix_post, norm_ffn_pre=norm_ffn_pre, norm_ffn_post=norm_ffn_post, w_in=w_in, b_gate=b_gate, rwkv_shift_mu=rwkv_shift_mu, rwkv_w0=rwkv_w0, rwkv_w2=rwkv_w2, rwkv_a0=rwkv_a0, rwkv_a2=rwkv_a2, rwkv_g2=rwkv_g2, rwkv_k_k=rwkv_k_k, rwkv_k_a=rwkv_k_a, rwkv_r_k=rwkv_r_k, rwkv_lnx_w=rwkv_lnx_w, rwkv_lnx_b=rwkv_lnx_b, s5_a_re=s5_a_re, s5_a_im=s5_a_im, s5_b_re=s5_b_re, s5_b_im=s5_b_im, s5_c_re=s5_c_re, s5_c_im=s5_c_im, s5_d=s5_d, s5_log_step=s5_log_step, s5_w_glu=s5_w_glu, s5_b_glu=s5_b_glu, w_branch_rwkv=w_branch_rwkv, w_branch_s5=w_branch_s5, w_out=w_out, ffn_w_up=ffn_w_up, ffn_conv_w=ffn_conv_w, ffn_conv_b=ffn_conv_b, ffn_w_down=ffn_w_down, loss_target=loss_target, m_norm_mix_pre=m_norm_mix_pre, m_norm_mix_post=m_norm_mix_post, m_norm_ffn_pre=m_norm_ffn_pre, m_norm_ffn_post=m_norm_ffn_post, m_w_in=m_w_in, m_b_gate=m_b_gate, m_rwkv_shift_mu=m_rwkv_shift_mu, m_rwkv_w0=m_rwkv_w0, m_rwkv_w2=m_rwkv_w2, m_rwkv_a0=m_rwkv_a0, m_rwkv_a2=m_rwkv_a2, m_rwkv_g2=m_rwkv_g2, m_rwkv_k_k=m_rwkv_k_k, m_rwkv_k_a=m_rwkv_k_a, m_rwkv_r_k=m_rwkv_r_k, m_rwkv_lnx_w=m_rwkv_lnx_w, m_rwkv_lnx_b=m_rwkv_lnx_b, m_s5_a_re=m_s5_a_re, m_s5_a_im=m_s5_a_im, m_s5_b_re=m_s5_b_re, m_s5_b_im=m_s5_b_im, m_s5_c_re=m_s5_c_re, m_s5_c_im=m_s5_c_im, m_s5_d=m_s5_d, m_s5_log_step=m_s5_log_step, m_s5_w_glu=m_s5_w_glu, m_s5_b_glu=m_s5_b_glu, m_w_branch_rwkv=m_w_branch_rwkv, m_w_branch_s5=m_w_branch_s5, m_w_out=m_w_out, m_ffn_w_up=m_ffn_w_up, m_ffn_conv_w=m_ffn_conv_w, m_ffn_conv_b=m_ffn_conv_b, m_ffn_w_down=m_ffn_w_down, v_norm_mix_pre=v_norm_mix_pre, v_norm_mix_post=v_norm_mix_post, v_norm_ffn_pre=v_norm_ffn_pre, v_norm_ffn_post=v_norm_ffn_post, v_w_in=v_w_in, v_b_gate=v_b_gate, v_rwkv_shift_mu=v_rwkv_shift_mu, v_rwkv_w0=v_rwkv_w0, v_rwkv_w2=v_rwkv_w2, v_rwkv_a0=v_rwkv_a0, v_rwkv_a2=v_rwkv_a2, v_rwkv_g2=v_rwkv_g2, v_rwkv_k_k=v_rwkv_k_k, v_rwkv_k_a=v_rwkv_k_a, v_rwkv_r_k=v_rwkv_r_k, v_rwkv_lnx_w=v_rwkv_lnx_w, v_rwkv_lnx_b=v_rwkv_lnx_b, v_s5_a_re=v_s5_a_re, v_s5_a_im=v_s5_a_im, v_s5_b_re=v_s5_b_re, v_s5_b_im=v_s5_b_im, v_s5_c_re=v_s5_c_re, v_s5_c_im=v_s5_c_im, v_s5_d=v_s5_d, v_s5_log_step=v_s5_log_step, v_s5_w_glu=v_s5_w_glu, v_s5_b_glu=v_s5_b_glu, v_w_branch_rwkv=v_w_branch_rwkv, v_w_branch_s5=v_w_branch_s5, v_w_out=v_w_out, v_ffn_w_up=v_ffn_w_up, v_ffn_conv_w=v_ffn_conv_w, v_ffn_conv_b=v_ffn_conv_b, v_ffn_w_down=v_ffn_w_down)
    weights = {n: given[n] for n in TWIN_WEIGHTS}
    shared = {n: given[n] for n in SHARED_INPUTS}
    per_example = {n: given[n] for n in ['x']}
    grad_fn = _jax.value_and_grad(_loss, argnums=(0, 1))

    def one_microbatch(ex, loss_target):
        ex = dict(ex)
        diff = ex.pop(TWIN_DIFF_INPUT)
        return grad_fn(weights, diff, {**shared, **ex}, loss_target)

    if N_MICROBATCH == 1:
        loss, (grad_w, grad_x) = one_microbatch(per_example, given["loss_target"])
    else:
        def body(carry, xs):
            loss_sum, grad_sum = carry
            l_k, (gw_k, gx_k) = one_microbatch(xs[0], xs[1])
            with _jax.named_scope("update"):
                return (loss_sum + l_k, _jax.tree.map(_jnp.add, grad_sum, gw_k)), gx_k

        init = (_jnp.zeros((), _jnp.float32), _jax.tree.map(_jnp.zeros_like, weights))
        (loss, grad_w), grad_x = _jax.lax.scan(body, init, (per_example, given["loss_target"]))
    with _jax.named_scope("update"):
        delta_w, new_m, new_v = {}, {}, {}
        for n in TWIN_WEIGHTS:
            delta_w[n], new_m[n], new_v[n] = _adamw(weights[n], grad_w[n], given["m_" + n], given["v_" + n])
    return (loss, grad_x, *[grad_w[n] for n in TWIN_WEIGHTS], *[delta_w[n] for n in TWIN_WEIGHTS],
            *[new_m[n] for n in TWIN_WEIGHTS], *[new_v[n] for n in TWIN_WEIGHTS])
```

```python
import functools

import jax
import jax.numpy as jnp
from jax import lax
from jax.experimental import pallas as pl
from jax.experimental.pallas import tpu as pltpu

f32, bf16 = jnp.float32, jnp.bfloat16
MESH = pl.DeviceIdType.MESH

D_MODEL = 1024
RWKV_W = 512
HEADS, HEAD = 8, 64
N_RWKV = 1792
S5_W = 512
S5_G, S5_P, S5_C = 32, 64, 16
S5_N = S5_G * S5_P
D_FF = 2816
NORM_EPS = 1e-6
LNX_EPS = 64e-5
ADAM_LR, ADAM_B1, ADAM_B2, ADAM_EPS, ADAM_WD, ADAM_STEP = 0.001, 0.9, 0.999, 1e-08, 0.01, 10

VMEM_LIMIT = 48 * 1024 * 1024
PACK_W = 1024
WKV_T = 32
S5_T = 256

WEIGHTS = ['norm_mix_pre', 'norm_mix_post', 'norm_ffn_pre', 'norm_ffn_post', 'w_in', 'b_gate', 'rwkv_shift_mu',
           'rwkv_w0', 'rwkv_w2', 'rwkv_a0', 'rwkv_a2', 'rwkv_g2', 'rwkv_k_k', 'rwkv_k_a', 'rwkv_r_k', 'rwkv_lnx_w',
           'rwkv_lnx_b', 's5_a_re', 's5_a_im', 's5_b_re', 's5_b_im', 's5_c_re', 's5_c_im', 's5_d', 's5_log_step',
           's5_w_glu', 's5_b_glu', 'w_branch_rwkv', 'w_branch_s5', 'w_out', 'ffn_w_up', 'ffn_conv_w', 'ffn_conv_b',
           'ffn_w_down']
SHARDED = {'w_in': ((1024, 4352), 1), 'rwkv_w2': ((64, 512), 1), 'rwkv_a2': ((64, 512), 1), 'rwkv_g2': ((128, 512), 1),
           's5_w_glu': ((512, 512), 0), 'w_branch_rwkv': ((512, 1024), 1), 'w_branch_s5': ((512, 1024), 1),
           'w_out': ((1024, 1024), 0), 'ffn_w_up': ((1024, 5632), 1), 'ffn_conv_w': ((3, 5632), 1),
           'ffn_w_down': ((2816, 1024), 0)}
BIG_F32 = ['rwkv_w2', 'rwkv_a2', 'rwkv_g2', 'ffn_conv_w']
BIG = BIG_F32 + [n for n in WEIGHTS if n in SHARDED and n not in BIG_F32]
SMALL = [n for n in WEIGHTS if n not in SHARDED]


def _ceil_to(n, m):
    return -(-n // m) * m


def _mesh_pos():
    return lax.axis_index("x"), lax.axis_index("y"), lax.axis_index("c")


def _pick(d, cap=4096):
    for c in (1024, 1408, 2176, 896, 512, 256, 128):
        if c <= cap and d % c == 0:
            return c
    raise ValueError(d)


def _mm(a, b, mode, name, out_dtype=f32):
    if mode == 'tn':
        (K, M), (K2, N) = a.shape, b.shape
    elif mode == 'nt':
        (M, K), (N, K2) = a.shape, b.shape
    else:
        (M, K), (K2, N) = a.shape, b.shape
    assert K == K2, (name, a.shape, b.shape)
    tm = _pick(M, 512)
    tn = _pick(N)
    tk = _pick(K, 512) if mode == 'tn' else _pick(K)
    nk = K // tk
    dims = {'nn': ((1,), (0,)), 'nt': ((1,), (1,)), 'tn': ((0,), (0,))}[mode]

    def body(a_ref, b_ref, o_ref, acc_ref):
        k = pl.program_id(2)

        @pl.when(k == 0)
        def _():
            acc_ref[...] = jnp.zeros_like(acc_ref)

        acc_ref[...] += lax.dot_general(a_ref[...].astype(bf16), b_ref[...].astype(bf16), (dims, ((), ())),
                                        preferred_element_type=f32)

        @pl.when(k == nk - 1)
        def _():
            o_ref[...] = acc_ref[...].astype(o_ref.dtype)

    a_spec = pl.BlockSpec((tk, tm), lambda i, j, k: (k, i)) if mode == 'tn' else pl.BlockSpec((tm, tk), lambda i, j, k: (i, k))
    b_spec = pl.BlockSpec((tn, tk), lambda i, j, k: (j, k)) if mode == 'nt' else pl.BlockSpec((tk, tn), lambda i, j, k: (k, j))
    return pl.pallas_call(
        body, name=name, grid=(M // tm, N // tn, nk),
        in_specs=[a_spec, b_spec], out_specs=pl.BlockSpec((tm, tn), lambda i, j, k: (i, j)),
        out_shape=jax.ShapeDtypeStruct((M, N), out_dtype),
        scratch_shapes=[pltpu.VMEM((tm, tn), f32)],
        compiler_params=pltpu.CompilerParams(dimension_semantics=("parallel", "parallel", "arbitrary"),
                                             vmem_limit_bytes=VMEM_LIMIT),
    )(a, b)


def _rowcall(name, fn, L, tm, rows, consts=(), out_rows=(), out_accs=(), prev=(), nxt=()):
    nsteps = L // tm
    nb8 = tm // 8
    last8 = L // 8 - 1
    n_r, n_p, n_x, n_c, n_or = len(rows), len(prev), len(nxt), len(consts), len(out_rows)

    def body(*refs):
        i = pl.program_id(0)
        vals = [r[...] for r in refs[:n_r + n_p + n_x + n_c]]
        R, P = vals[:n_r], vals[n_r:n_r + n_p]
        X, C = vals[n_r + n_p:n_r + n_p + n_x], vals[n_r + n_p + n_x:]
        o_refs = refs[n_r + n_p + n_x + n_c:]
        outs_r, outs_a = fn(i, nsteps, R, P, X, C)
        for ref, v in zip(o_refs[:n_or], outs_r, strict=True):
            ref[...] = v.astype(ref.dtype)
        if out_accs:
            @pl.when(i == 0)
            def _():
                for ref in o_refs[n_or:]:
                    ref[...] = jnp.zeros_like(ref)

            for ref, v in zip(o_refs[n_or:], outs_a, strict=True):
                ref[...] += v

    def const_spec(c):
        nd = c.ndim
        return pl.BlockSpec(c.shape, lambda i: (0,) * nd)

    in_specs = ([pl.BlockSpec((tm, a.shape[1]), lambda i: (i, 0)) for a in rows]
                + [pl.BlockSpec((8, rows[j].shape[1]), lambda i: (jnp.maximum(i * nb8 - 1, 0), 0)) for j in prev]
                + [pl.BlockSpec((8, rows[j].shape[1]), lambda i: (jnp.minimum((i + 1) * nb8, last8), 0)) for j in nxt]
                + [const_spec(c) for c in consts])
    out_specs = ([pl.BlockSpec((tm, c), lambda i: (i, 0)) for c, _ in out_rows]
                 + [pl.BlockSpec(s, lambda i: (0, 0)) for s in out_accs])
    out_shape = ([jax.ShapeDtypeStruct((L, c), dt) for c, dt in out_rows]
                 + [jax.ShapeDtypeStruct(s, f32) for s in out_accs])
    args = list(rows) + [rows[j] for j in prev] + [rows[j] for j in nxt] + list(consts)
    return pl.pallas_call(
        body, name=name, grid=(nsteps,), in_specs=in_specs, out_specs=out_specs, out_shape=out_shape,
        compiler_params=pltpu.CompilerParams(dimension_semantics=("arbitrary",), vmem_limit_bytes=VMEM_LIMIT),
    )(*args)


def _shift_down(x, prev8, i, k):
    rolled = pltpu.roll(x, k, axis=0)
    pfix = jnp.where(i > 0, pltpu.roll(prev8, k, axis=0), 0.0)
    row8 = lax.broadcasted_iota(jnp.int32, pfix.shape, 0)
    top = jnp.where(row8 < k, pfix, rolled[:8])
    return jnp.concatenate([top, rolled[8:]], axis=0)


def _shift_up(x, next8, i, nsteps, k):
    tm = x.shape[0]
    rolled = pltpu.roll(x, tm - k, axis=0)
    nfix = jnp.where(i < nsteps - 1, pltpu.roll(next8, 8 - k, axis=0), 0.0)
    row8 = lax.broadcasted_iota(jnp.int32, nfix.shape, 0)
    bot = jnp.where(row8 >= 8 - k, nfix, rolled[tm - 8:])
    return jnp.concatenate([rolled[:tm - 8], bot], axis=0)


def _sum0(x):
    return jnp.sum(x, axis=0, keepdims=True)


def _rms(x, g):
    return x * lax.rsqrt(jnp.mean(x * x, axis=-1, keepdims=True) + NORM_EPS) * g


def _softplus(x):
    return jnp.maximum(x, 0.0) + jnp.log(1.0 + jnp.exp(-jnp.abs(x)))


def _gelu(x):
    return 0.5 * x * (1.0 + jnp.tanh(0.7978845608028654 * (x + 0.044715 * x * x * x)))


def _dot32(a, b):
    return jnp.dot(a, b, preferred_element_type=f32, precision=lax.Precision.HIGHEST)


def _prep(q, w0, a0, k_k, k_a, w2p, a2p, g2, E):
    r, k, v = q[:, 0:512], q[:, 512:1024], q[:, 1024:1536]
    wa, gd = q[:, 1536:1664], q[:, 1664:1792]
    wlog = -_softplus(-(w0 + _dot32(jnp.tanh(wa), w2p))) - 0.5
    decay = jnp.exp(-jnp.exp(wlog))
    a = jax.nn.sigmoid(a0 + _dot32(wa, a2p))
    g = _dot32(jax.nn.sigmoid(gd), g2)
    kk = k * k_k
    kkn = kk / jnp.maximum(jnp.sqrt(_dot32(kk * kk, E)), 1e-12)
    k2 = k * (1.0 + (a - 1.0) * k_a)
    return r, decay, k2, v, -kkn, kkn * a, g


def _rwkv_out(y, r, k2, v, g, lnx_w, lnx_b, r_k, E):
    mean = _dot32(y, E) * (1.0 / HEAD)
    yc = y - mean
    var = _dot32(yc * yc, E) * (1.0 / HEAD)
    yn = yc * lax.rsqrt(var + LNX_EPS) * lnx_w + lnx_b
    bonus = _dot32(r * k2 * r_k, E) * v
    return (yn + bonus) * g


def _s5_mid(ysc, u, d):
    return _gelu(ysc + d * u)


def _s5_glu(yg, z2, b_glu):
    return yg * jax.nn.sigmoid(z2 + b_glu)


def _merge(gp, o_r, o_s, b_gate):
    gates = jax.nn.sigmoid(gp + b_gate)
    return gates[:, :D_MODEL] * o_r + gates[:, D_MODEL:] * o_s


def _act(zc):
    return _gelu(zc[:, :D_FF]) * zc[:, D_FF:]


def _s5_disc(a_re, a_im, ls, b_re, b_im):
    dt = jnp.exp(ls)
    er = jnp.exp(a_re * dt)
    ar, ai = er * jnp.cos(a_im * dt), er * jnp.sin(a_im * dt)
    x, y = ar - 1.0, ai
    den = a_re * a_re + a_im * a_im
    fr, fi = (x * a_re + y * a_im) / den, (y * a_re - x * a_im) / den
    return ar, ai, fr * b_re - fi * b_im, fr * b_im + fi * b_re


def _expand(x):
    return jnp.concatenate([jnp.broadcast_to(x[h:h + 1, :], (HEAD, HEAD)) for h in range(HEADS)], axis=0)


def _cols_to_rows(cols):
    lane = lax.broadcasted_iota(jnp.int32, (RWKV_W, 8), 1)
    y8 = jnp.zeros((RWKV_W, 8), f32)
    for s, c in enumerate(cols):
        y8 = jnp.where(lane == s, c, y8)
    return y8.T


def _head_rowsum(p):
    return [jnp.sum(p[h * HEAD:(h + 1) * HEAD, :], axis=0, keepdims=True) for h in range(HEADS)]


def _wkv7_fwd(r, w, k, a, b, v):
    L = v.shape[0]
    nc = L // WKV_T

    def body(r_ref, w_ref, k_ref, a_ref, b_ref, v_ref, y_ref, ck_ref, s_ref):
        @pl.when(pl.program_id(0) == 0)
        def _():
            s_ref[...] = jnp.zeros_like(s_ref)

        ck_ref[0] = s_ref[...]

        def group(g, S):
            t0 = pl.multiple_of(g * 8, 8)
            v8 = v_ref[pl.ds(t0, 8), :].T
            cols = []
            for s in range(8):
                t = t0 + s
                sa = jnp.sum(S * _expand(a_ref[t]), axis=1, keepdims=True)
                S = S * _expand(w_ref[t]) + sa * _expand(b_ref[t]) + v8[:, s:s + 1] * _expand(k_ref[t])
                cols.append(jnp.sum(S * _expand(r_ref[t]), axis=1, keepdims=True))
            y_ref[pl.ds(t0, 8), :] = _cols_to_rows(cols)
            return S

        s_ref[...] = lax.fori_loop(0, WKV_T // 8, group, s_ref[...])

    hspec = pl.BlockSpec((WKV_T, HEADS, HEAD), lambda c: (c, 0, 0))
    return pl.pallas_call(
        body, name="wkv7_fwd", grid=(nc,),
        in_specs=[hspec] * 5 + [pl.BlockSpec((WKV_T, RWKV_W), lambda c: (c, 0))],
        out_specs=[pl.BlockSpec((WKV_T, RWKV_W), lambda c: (c, 0)), pl.BlockSpec((1, RWKV_W, HEAD), lambda c: (c, 0, 0))],
        out_shape=[jax.ShapeDtypeStruct((L, RWKV_W), f32), jax.ShapeDtypeStruct((nc, RWKV_W, HEAD), f32)],
        scratch_shapes=[pltpu.VMEM((RWKV_W, HEAD), f32)],
        compiler_params=pltpu.CompilerParams(dimension_semantics=("arbitrary",)),
    )(r, w, k, a, b, v)


def _wkv7_bwd(r, w, k, a, b, v, ck, dy):
    L = v.shape[0]
    nc = L // WKV_T

    def body(r_ref, w_ref, k_ref, a_ref, b_ref, v_ref, ck_ref, dy_ref,
             dr_ref, dw_ref, dk_ref, da_ref, db_ref, dv_ref, sall_ref, ds_ref):
        @pl.when(pl.program_id(0) == 0)
        def _():
            ds_ref[...] = jnp.zeros_like(ds_ref)

        def fgroup(g, S):
            t0 = pl.multiple_of(g * 8, 8)
            v8 = v_ref[pl.ds(t0, 8), :].T
            for s in range(8):
                t = t0 + s
                sall_ref[t] = S
                sa = jnp.sum(S * _expand(a_ref[t]), axis=1, keepdims=True)
                S = S * _expand(w_ref[t]) + sa * _expand(b_ref[t]) + v8[:, s:s + 1] * _expand(k_ref[t])
            return S

        sall_ref[WKV_T] = lax.fori_loop(0, WKV_T // 8, fgroup, ck_ref[0])

        def bgroup(gi, dS):
            t0 = pl.multiple_of((WKV_T // 8 - 1 - gi) * 8, 8)
            v8 = v_ref[pl.ds(t0, 8), :].T
            dy8 = dy_ref[pl.ds(t0, 8), :].T
            dvcols = [None] * 8
            for s in range(7, -1, -1):
                t = t0 + s
                Sp, St = sall_ref[t], sall_ref[t + 1]
                R, Wd, K, A, B = (_expand(x[t]) for x in (r_ref, w_ref, k_ref, a_ref, b_ref))
                vc, dyc = v8[:, s:s + 1], dy8[:, s:s + 1]
                dS = dS + dyc * R
                dvcols[s] = jnp.sum(dS * K, axis=1, keepdims=True)
                dsa = jnp.sum(dS * B, axis=1, keepdims=True)
                sa = jnp.sum(Sp * A, axis=1, keepdims=True)
                for ref, prod in ((dr_ref, St * dyc), (dk_ref, dS * vc), (db_ref, dS * sa), (dw_ref, dS * Sp),
                                  (da_ref, Sp * dsa)):
                    for h, row in enumerate(_head_rowsum(prod)):
                        ref[t, h:h + 1, :] = row
                dS = dS * Wd + dsa * A
            dv_ref[pl.ds(t0, 8), :] = _cols_to_rows(dvcols)
            return dS

        ds_ref[...] = lax.fori_loop(0, WKV_T // 8, bgroup, ds_ref[...])

    hspec = pl.BlockSpec((WKV_T, HEADS, HEAD), lambda c: (nc - 1 - c, 0, 0))
    fspec = pl.BlockSpec((WKV_T, RWKV_W), lambda c: (nc - 1 - c, 0))
    return pl.pallas_call(
        body, name="wkv7_bwd", grid=(nc,),
        in_specs=[hspec] * 5 + [fspec, pl.BlockSpec((1, RWKV_W, HEAD), lambda c: (nc - 1 - c, 0, 0)), fspec],
        out_specs=[hspec] * 5 + [fspec],
        out_shape=[jax.ShapeDtypeStruct((L, HEADS, HEAD), f32)] * 5 + [jax.ShapeDtypeStruct((L, RWKV_W), f32)],
        scratch_shapes=[pltpu.VMEM((WKV_T + 1, RWKV_W, HEAD), f32), pltpu.VMEM((RWKV_W, HEAD), f32)],
        compiler_params=pltpu.CompilerParams(dimension_semantics=("arbitrary",), vmem_limit_bytes=VMEM_LIMIT),
    )(r, w, k, a, b, v, ck, dy)


def _cmul(ar, ai, xr, xi):
    return ar * xr - ai * xi, ar * xi + ai * xr


def _s5_scan(x, abar, reverse, name):
    L = x.shape[0]
    nt = L // S5_T
    ng = S5_T // 8

    def body(x_ref, a_ref, o_ref, car_ref, pw_ref):
        @pl.when(pl.program_id(0) == 0)
        def _():
            car_ref[...] = jnp.zeros_like(car_ref)
            ar = jnp.broadcast_to(a_ref[:, :S5_N], (8, S5_N))
            ai = jnp.broadcast_to(a_ref[:, S5_N:], (8, S5_N))
            if reverse:
                ai = -ai
            row = lax.broadcasted_iota(jnp.int32, (8, S5_N), 0)
            pr, pi = ar, ai
            qr, qi = jnp.zeros((8, S5_N), f32), jnp.zeros((8, S5_N), f32)
            for e in range(1, 9):
                sel = (row == 8 - e) if reverse else (row == e - 1)
                qr, qi = jnp.where(sel, pr, qr), jnp.where(sel, pi, qi)
                if e in (1, 2, 4):
                    j = (1, 2, 4).index(e)
                    pw_ref[j, :, :S5_N] = pr
                    pw_ref[j, :, S5_N:] = pi
                pr, pi = _cmul(pr, pi, ar, ai)
            pw_ref[3, :, :S5_N] = qr
            pw_ref[3, :, S5_N:] = qi

        row = lax.broadcasted_iota(jnp.int32, (8, S5_N), 0)

        def group(gi, carry):
            g = (ng - 1 - gi) if reverse else gi
            t0 = pl.multiple_of(g * 8, 8)
            xr, xi = x_ref[pl.ds(t0, 8), :S5_N], x_ref[pl.ds(t0, 8), S5_N:]
            for j, d in enumerate((1, 2, 4)):
                if reverse:
                    sr = jnp.where(row < 8 - d, pltpu.roll(xr, 8 - d, axis=0), 0.0)
                    si = jnp.where(row < 8 - d, pltpu.roll(xi, 8 - d, axis=0), 0.0)
                else:
                    sr = jnp.where(row >= d, pltpu.roll(xr, d, axis=0), 0.0)
                    si = jnp.where(row >= d, pltpu.roll(xi, d, axis=0), 0.0)
                mr, mi = _cmul(pw_ref[j, :, :S5_N], pw_ref[j, :, S5_N:], sr, si)
                xr, xi = xr + mr, xi + mi
            cr, ci = carry
            mr, mi = _cmul(pw_ref[3, :, :S5_N], pw_ref[3, :, S5_N:], cr, ci)
            xr, xi = xr + mr, xi + mi
            o_ref[pl.ds(t0, 8), :S5_N] = xr
            o_ref[pl.ds(t0, 8), S5_N:] = xi
            e = 0 if reverse else 7
            return (jnp.broadcast_to(xr[e:e + 1, :], (8, S5_N)), jnp.broadcast_to(xi[e:e + 1, :], (8, S5_N)))

        cr, ci = lax.fori_loop(0, ng, group, (car_ref[:, :S5_N], car_ref[:, S5_N:]))
        car_ref[:, :S5_N] = cr
        car_ref[:, S5_N:] = ci

    imap = (lambda i: (nt - 1 - i, 0)) if reverse else (lambda i: (i, 0))
    return pl.pallas_call(
        body, name=name, grid=(nt,),
        in_specs=[pl.BlockSpec((S5_T, 2 * S5_N), imap), pl.BlockSpec((1, 2 * S5_N), lambda i: (0, 0))],
        out_specs=pl.BlockSpec((S5_T, 2 * S5_N), imap),
        out_shape=jax.ShapeDtypeStruct((L, 2 * S5_N), f32),
        scratch_shapes=[pltpu.VMEM((8, 2 * S5_N), f32), pltpu.VMEM((4, 8, 2 * S5_N), f32)],
        compiler_params=pltpu.CompilerParams(dimension_semantics=("arbitrary",), vmem_limit_bytes=VMEM_LIMIT),
    )(x, abar)


def _s5_disc_fwd(a_re, a_im, ls, b_re, b_im):
    def body(a_re_ref, a_im_ref, ls_ref, b_re_ref, b_im_ref, ar_ref, ai_ref, br_ref, bi_ref):
        outs = _s5_disc(a_re_ref[...], a_im_ref[...], ls_ref[...], b_re_ref[...], b_im_ref[...])
        for ref, v in zip((ar_ref, ai_ref, br_ref, bi_ref), outs):
            ref[...] = v

    c1, c16 = jax.ShapeDtypeStruct((S5_N, 1), f32), jax.ShapeDtypeStruct((S5_N, S5_C), f32)
    return pl.pallas_call(body, name="s5_disc", out_shape=[c1, c1, c16, c16])(a_re, a_im, ls, b_re, b_im)


def _s5_disc_bwd(a_re, a_im, ls, b_re, b_im, d_ar, d_ai, d_br, d_bi, seg):
    def body(a_re_ref, a_im_ref, ls_ref, b_re_ref, b_im_ref, g1, g2, g3, g4, seg_ref, o1, o2, o3, o4, o5):
        _, vjp = jax.vjp(_s5_disc, a_re_ref[...], a_im_ref[...], ls_ref[...], b_re_ref[...], b_im_ref[...])
        da_re, da_im, dls, db_re, db_im = vjp((g1[...], g2[...], g3[...], g4[...]))
        o1[...] = da_re
        o2[...] = da_im
        o3[...] = _dot32(seg_ref[...], dls)
        o4[...] = db_re
        o5[...] = db_im

    c1, c16 = jax.ShapeDtypeStruct((S5_N, 1), f32), jax.ShapeDtypeStruct((S5_N, S5_C), f32)
    return pl.pallas_call(body, name="s5_disc_bwd", out_shape=[c1, c1, jax.ShapeDtypeStruct((S5_G, 1), f32), c16, c16])(
        a_re, a_im, ls, b_re, b_im, d_ar, d_ai, d_br, d_bi, seg)


ANY = pl.BlockSpec(memory_space=pl.ANY)


def _chip_exchange(x, same_src, name):
    blk = x.shape if same_src else x.shape[1:]

    def body(x_ref, o_ref, ssems, rsems, lsem):
        px, py, pc = _mesh_pos()
        me = 2 * px + py
        local = pltpu.make_async_copy(x_ref if same_src else x_ref.at[me], o_ref.at[me], lsem)
        local.start()
        copies = []
        for k in (1, 2, 3):
            qx = (1 - px) if (k >> 1) else px
            qy = (1 - py) if (k & 1) else py
            src = x_ref if same_src else x_ref.at[2 * qx + qy]
            cp = pltpu.make_async_remote_copy(src, o_ref.at[me], ssems.at[k - 1], rsems.at[k - 1],
                                              device_id=(qx, qy, pc), device_id_type=MESH)
            cp.start()
            copies.append(cp)
        for cp in copies:
            cp.wait()
        local.wait()

    return pl.pallas_call(
        body, name=name, in_specs=[ANY], out_specs=ANY,
        out_shape=jax.ShapeDtypeStruct((4,) + tuple(blk), x.dtype),
        scratch_shapes=[pltpu.SemaphoreType.DMA((3,)), pltpu.SemaphoreType.DMA((3,)), pltpu.SemaphoreType.DMA],
    )(x)


def _send_other_half(g, name):
    _, R, Wd = g.shape
    rh = R // 2

    def body(g_ref, o_ref, ssem, rsem):
        px, py, pc = _mesh_pos()
        off = pl.multiple_of((1 - pc) * rh, 8)
        cp = pltpu.make_async_remote_copy(g_ref.at[:, pl.ds(off, rh), :], o_ref, ssem, rsem,
                                          device_id=(px, py, 1 - pc), device_id_type=MESH)
        cp.start()
        cp.wait()

    return pl.pallas_call(
        body, name=name, in_specs=[ANY], out_specs=ANY, out_shape=jax.ShapeDtypeStruct((4, rh, Wd), g.dtype),
        scratch_shapes=[pltpu.SemaphoreType.DMA, pltpu.SemaphoreType.DMA],
    )(g)


def _pair_place(xh, name):
    def body(x_ref, o_ref, ssem, rsem, lsem):
        px, py, pc = _mesh_pos()
        local = pltpu.make_async_copy(x_ref, o_ref.at[pc], lsem)
        local.start()
        cp = pltpu.make_async_remote_copy(x_ref, o_ref.at[pc], ssem, rsem, device_id=(px, py, 1 - pc), device_id_type=MESH)
        cp.start()
        cp.wait()
        local.wait()

    return pl.pallas_call(
        body, name=name, in_specs=[ANY], out_specs=ANY, out_shape=jax.ShapeDtypeStruct((2,) + xh.shape, xh.dtype),
        scratch_shapes=[pltpu.SemaphoreType.DMA, pltpu.SemaphoreType.DMA, pltpu.SemaphoreType.DMA],
    )(xh)


def _add_my_half(g, recv, tm, name):
    _, R, Wd = g.shape
    rh = R // 2
    nh = rh // tm
    cidx = lax.axis_index("c").astype(jnp.int32).reshape(1)

    def body(c_ref, g_ref, r_ref, o_ref):
        o_ref[...] = g_ref[...] + r_ref[...]

    return pl.pallas_call(
        body, name=name,
        grid_spec=pltpu.PrefetchScalarGridSpec(
            num_scalar_prefetch=1, grid=(4, nh),
            in_specs=[pl.BlockSpec((1, tm, Wd), lambda j, i, c: (j, c[0] * nh + i, 0)),
                      pl.BlockSpec((1, tm, Wd), lambda j, i, c: (j, i, 0))],
            out_specs=pl.BlockSpec((1, tm, Wd), lambda j, i, c: (j, i, 0))),
        out_shape=jax.ShapeDtypeStruct((4, rh, Wd), f32),
    )(cidx, g, recv)


def _sum_slots(x, tm, name):
    _, rh, Wd = x.shape

    def body(x_ref, o_ref):
        o_ref[...] = ((x_ref[0] + x_ref[1]) + x_ref[2]) + x_ref[3]

    return pl.pallas_call(
        body, name=name, grid=(rh // tm,),
        in_specs=[pl.BlockSpec((4, tm, Wd), lambda i: (0, i, 0))], out_specs=pl.BlockSpec((tm, Wd), lambda i: (i, 0)),
        out_shape=jax.ShapeDtypeStruct((rh, Wd), f32),
        compiler_params=pltpu.CompilerParams(vmem_limit_bytes=VMEM_LIMIT),
    )(x)


def _shard_elems(name):
    shape, _ = SHARDED[name]
    return shape[0] * shape[1] // 4


BIG_ROWS = sum(_ceil_to(_shard_elems(n), PACK_W) // PACK_W for n in BIG)


def _flat_pad(v):
    v = v.reshape(-1)
    return jnp.pad(v, (0, _ceil_to(v.shape[0], PACK_W) - v.shape[0]))


def _pack(big_blocks, small_blocks, rows):
    flat = jnp.concatenate([_flat_pad(b) for b in big_blocks] + [_flat_pad(s) for s in small_blocks])
    return jnp.pad(flat, (0, rows * PACK_W - flat.shape[0])).reshape(rows, PACK_W)


def _split_shards(name, full):
    (r, c), axis = SHARDED[name]
    if axis == 1:
        return full.reshape(r, 4, c // 4).transpose(1, 0, 2).reshape(4, -1)
    return full.reshape(4, -1)


def _join_shards(name, parts):
    (r, c), axis = SHARDED[name]
    if axis == 1:
        return parts.reshape(4, r, c // 4).transpose(1, 0, 2).reshape(r, c)
    return parts.reshape(r, c)


def _unpack(buf, big_shapes, small_shapes):
    flat = buf.reshape(-1)
    out, off = [], 0
    for shp in list(big_shapes) + list(small_shapes):
        n = 1
        for d in shp:
            n *= d
        out.append(flat[off:off + n].reshape(shp))
        off += _ceil_to(n, PACK_W)
    return out


def _adamw(w, g, m, v, tm):
    R = w.shape[0]

    def fn(i, nsteps, Rv, P, X, C):
        w_, g_, m_, v_ = Rv
        m2 = ADAM_B1 * m_ + (1.0 - ADAM_B1) * g_
        v2 = ADAM_B2 * v_ + (1.0 - ADAM_B2) * (g_ * g_)
        m_hat = m2 / (1.0 - ADAM_B1 ** ADAM_STEP)
        v_hat = v2 / (1.0 - ADAM_B2 ** ADAM_STEP)
        delta = -ADAM_LR * (m_hat / (jnp.sqrt(v_hat) + ADAM_EPS) + ADAM_WD * w_)
        return (delta, m2, v2), ()

    return _rowcall("adamw", fn, R, tm, [w, g, m, v], out_rows=[(PACK_W, f32)] * 3)


def _forward_backward(x, tgt, W, S):
    L = x.shape[0]
    TM, TMW = 256, 128
    row = lambda c, dt=f32: (c, dt)
    hid = jnp.arange(RWKV_W) // HEAD
    E = (hid[:, None] == hid[None, :]).astype(f32)
    seg = (jnp.arange(S5_N)[None, :] // S5_P == jnp.arange(S5_G)[:, None]).astype(f32)

    w_in = W['w_in']
    w_p, w_u, w_g = w_in[:, :N_RWKV], w_in[:, N_RWKV:N_RWKV + S5_W], w_in[:, N_RWKV + S5_W:]
    zpad = jnp.zeros((64, RWKV_W), f32)
    w2p = jnp.concatenate([W['rwkv_w2'].astype(f32), zpad], axis=0)
    a2p = jnp.concatenate([zpad, W['rwkv_a2'].astype(f32)], axis=0)
    g2 = W['rwkv_g2'].astype(f32)
    prep_consts = [S['rwkv_shift_mu'], S['rwkv_w0'], S['rwkv_a0'], S['rwkv_k_k'], S['rwkv_k_a'], w2p, a2p, g2, E]
    out_consts = [S['rwkv_lnx_w'], S['rwkv_lnx_b'], S['rwkv_r_k'], E]
    cw, cb = S['ffn_conv_w_full'], S['ffn_conv_b']

    a_re, a_im = S['s5_a_re'].reshape(S5_N, 1), S['s5_a_im'].reshape(S5_N, 1)
    ls = jnp.repeat(S['s5_log_step'].reshape(S5_G, 1), S5_P, axis=0)
    b_re, b_im = S['s5_b_re'].reshape(S5_N, S5_C), S['s5_b_im'].reshape(S5_N, S5_C)
    ar, ai, bbr, bbi = _s5_disc_fwd(a_re, a_im, ls, b_re, b_im)
    abar = jnp.concatenate([ar.reshape(1, S5_N), ai.reshape(1, S5_N)], axis=1)
    eye = jnp.eye(S5_G, dtype=f32)

    def bdiag_in(bb):
        t = bb.reshape(S5_G, S5_P, S5_C).transpose(0, 2, 1)
        return (t[:, :, None, :] * eye[:, None, :, None]).reshape(S5_W, S5_N)

    def bdiag_out(cc):
        t = cc.transpose(0, 2, 1)
        return (t[:, :, None, :] * eye[:, None, :, None]).reshape(S5_N, S5_W)

    def undiag_in(m):
        t = m.reshape(S5_G, S5_C, S5_G, S5_P)
        t = jnp.sum(t * eye[:, None, :, None], axis=2)
        return t.transpose(0, 2, 1).reshape(S5_N, S5_C)

    def undiag_out(m):
        t = m.reshape(S5_G, S5_P, S5_G, S5_C)
        t = jnp.sum(t * eye[:, None, :, None], axis=2)
        return t.transpose(0, 2, 1)

    bmat = jnp.concatenate([bdiag_in(bbr), bdiag_in(bbi)], axis=1).astype(bf16)
    cmat = jnp.concatenate([bdiag_out(S['s5_c_re'].reshape(S5_G, S5_C, S5_P)),
                            -bdiag_out(S['s5_c_im'].reshape(S5_G, S5_C, S5_P))], axis=0).astype(bf16)

    g1, g2n, g3, g4 = S['norm_mix_pre'], S['norm_mix_post'], S['norm_ffn_pre'], S['norm_ffn_post']
    (h1,) = _rowcall("norm_pre", lambda i, n, R, P, X, C: ((_rms(R[0], C[0]),), ()), L, TM, [x], [g1],
                     out_rows=[row(D_MODEL, bf16)])
    p = _mm(h1, w_p, 'nn', "mm_p")
    u = _mm(h1, w_u, 'nn', "mm_u")
    gp = _mm(h1, w_g, 'nn', "mm_g")

    def prep_fn(i, n, R, P, X, C):
        q = R[0] + (_shift_down(R[0], P[0], i, 1) - R[0]) * C[0]
        return _prep(q, *C[1:]), ()

    r, decay, k2, v, an, bv, g = _rowcall("rwkv_prep", prep_fn, L, TM, [p], prep_consts,
                                          out_rows=[row(RWKV_W)] * 7, prev=[0])
    hd = lambda t: t.reshape(L, HEADS, HEAD)
    y, ck = _wkv7_fwd(hd(r), hd(decay), hd(k2), hd(an), hd(bv), v)
    (o_a,) = _rowcall("rwkv_out", lambda i, n, R, P, X, C: ((_rwkv_out(*R, *C),), ()), L, TM, [y, r, k2, v, g],
                      out_consts, out_rows=[row(RWKV_W)])
    o_r = _mm(o_a, W['w_branch_rwkv'], 'nn', "mm_br")

    bu = _mm(u, bmat, 'nn', "mm_bu")
    st = _s5_scan(bu, abar, False, "s5_scan")
    ysc = _mm(st, cmat, 'nn', "mm_cs")
    (yg,) = _rowcall("s5_mid", lambda i, n, R, P, X, C: ((_s5_mid(*R, *C),), ()), L, TM, [ysc, u], [S['s5_d']],
                     out_rows=[row(S5_W)])
    z2 = _mm(yg, W['s5_w_glu'], 'nn', "mm_glu")
    (o_b,) = _rowcall("s5_glu", lambda i, n, R, P, X, C: ((_s5_glu(*R, *C),), ()), L, TM, [yg, z2], [S['s5_b_glu']],
                      out_rows=[row(S5_W)])
    o_s = _mm(o_b, W['w_branch_s5'], 'nn', "mm_bs")

    (merged,) = _rowcall("merge", lambda i, n, R, P, X, C: ((_merge(*R, *C),), ()), L, TM, [gp, o_r, o_s],
                         [S['b_gate']], out_rows=[row(D_MODEL)])
    mixed = _mm(merged, W['w_out'], 'nn', "mm_out")

    def resid_fn(i, n, R, P, X, C):
        x1_ = R[0] + _rms(R[1], C[0])
        return (x1_, _rms(x1_, C[1])), ()

    x1, h2 = _rowcall("resid_norm", resid_fn, L, TM, [x, mixed], [g2n, g3], out_rows=[row(D_MODEL), row(D_MODEL, bf16)])

    z = _mm(h2, W['ffn_w_up'], 'nn', "mm_up")

    def conv(zt, zprev, i, cw_, cb_):
        z2s, z1s = _shift_down(zt, zprev, i, 2), _shift_down(zt, zprev, i, 1)
        return cb_ + cw_[0:1] * z2s + cw_[1:2] * z1s + cw_[2:3] * zt, z2s, z1s

    (act,) = _rowcall("conv_act", lambda i, n, R, P, X, C: ((_act(conv(R[0], P[0], i, C[0], C[1])[0]),), ()), L, TMW,
                      [z], [cw, cb], out_rows=[row(D_FF)], prev=[0])
    f = _mm(act, W['ffn_w_down'], 'nn', "mm_down")

    def final_fn(i, n, R, P, X, C):
        x1_, f_, t_ = R
        fn_, vjp = jax.vjp(_rms, f_, C[0])
        diff = x1_ + fn_ - t_
        loss = jnp.sum(diff * diff) * (0.5 / D_MODEL)
        dx2_ = diff * (1.0 / D_MODEL)
        df_, dg4_ = vjp(dx2_)
        return (df_, dx2_), (jnp.full((1, PACK_W), loss, f32), dg4_)

    df, dx2, loss, dg4 = _rowcall("loss_head", final_fn, L, TM, [x1, f, tgt], [g4],
                                  out_rows=[row(D_MODEL)] * 2, out_accs=[(1, PACK_W), (1, D_MODEL)])
    G = {'norm_ffn_post': dg4}

    dact = _mm(df, W['ffn_w_down'], 'nt', "mm_down_dx")
    G['ffn_w_down'] = _mm(act, df, 'tn', "mm_down_dw")

    def conv_bwd_fn(i, n, R, P, X, C):
        zc, z2s, z1s = conv(R[0], P[0], i, C[0], C[1])
        _, vjp = jax.vjp(_act, zc)
        (dzc_,) = vjp(R[1])
        return (dzc_,), (_sum0(dzc_), _sum0(dzc_ * z2s), _sum0(dzc_ * z1s), _sum0(dzc_ * R[0]))

    wide = (1, 2 * D_FF)
    dzc, dcb, dcw0, dcw1, dcw2 = _rowcall("conv_act_bwd", conv_bwd_fn, L, TMW, [z, dact], [cw, cb],
                                          out_rows=[row(2 * D_FF)], out_accs=[wide] * 4, prev=[0])
    G['ffn_conv_b'] = dcb
    G['ffn_conv_w'] = jnp.concatenate([dcw0, dcw1, dcw2], axis=0)

    def conv_shift_fn(i, n, R, P, X, C):
        d = R[0]
        return (C[0][2:3] * d + C[0][1:2] * _shift_up(d, X[0], i, n, 1) + C[0][0:1] * _shift_up(d, X[0], i, n, 2),), ()

    (dz,) = _rowcall("conv_shift_bwd", conv_shift_fn, L, TMW, [dzc], [cw], out_rows=[row(2 * D_FF)], nxt=[0])
    dh2 = _mm(dz, W['ffn_w_up'], 'nt', "mm_up_dx")
    G['ffn_w_up'] = _mm(h2, dz, 'tn', "mm_up_dw")

    def norm2_bwd_fn(i, n, R, P, X, C):
        x1_, mixed_, dx2_, dh2_ = R
        _, vjp3 = jax.vjp(_rms, x1_, C[1])
        dx1a, dg3_ = vjp3(dh2_)
        dx1_ = dx2_ + dx1a
        _, vjp2 = jax.vjp(_rms, mixed_, C[0])
        dmixed_, dg2_ = vjp2(dx1_)
        return (dx1_, dmixed_), (dg2_, dg3_)

    dx1, dmixed, dg2n, dg3 = _rowcall("norm_mid_bwd", norm2_bwd_fn, L, TM, [x1, mixed, dx2, dh2], [g2n, g3],
                                      out_rows=[row(D_MODEL)] * 2, out_accs=[(1, D_MODEL)] * 2)
    G['norm_mix_post'], G['norm_ffn_pre'] = dg2n, dg3

    dmerged = _mm(dmixed, W['w_out'], 'nt', "mm_out_dx")
    G['w_out'] = _mm(merged, dmixed, 'tn', "mm_out_dw")

    def merge_bwd_fn(i, n, R, P, X, C):
        _, vjp = jax.vjp(_merge, R[0], R[1], R[2], C[0])
        dgp_, do_r_, do_s_, dbg_ = vjp(R[3])
        return (dgp_, do_r_, do_s_), (dbg_,)

    dgp, do_r, do_s, G['b_gate'] = _rowcall("merge_bwd", merge_bwd_fn, L, TM, [gp, o_r, o_s, dmerged], [S['b_gate']],
                                            out_rows=[row(2 * D_MODEL), row(D_MODEL), row(D_MODEL)],
                                            out_accs=[(1, 2 * D_MODEL)])
    do_a = _mm(do_r, W['w_branch_rwkv'], 'nt', "mm_br_dx")
    G['w_branch_rwkv'] = _mm(o_a, do_r, 'tn', "mm_br_dw")
    do_b = _mm(do_s, W['w_branch_s5'], 'nt', "mm_bs_dx")
    G['w_branch_s5'] = _mm(o_b, do_s, 'tn', "mm_bs_dw")

    def glu_bwd_fn(i, n, R, P, X, C):
        _, vjp = jax.vjp(_s5_glu, R[0], R[1], C[0])
        dyg1_, dz2_, dbg_ = vjp(R[2])
        return (dyg1_, dz2_), (dbg_,)

    dyg1, dz2, G['s5_b_glu'] = _rowcall("s5_glu_bwd", glu_bwd_fn, L, TM, [yg, z2, do_b], [S['s5_b_glu']],
                                        out_rows=[row(S5_W)] * 2, out_accs=[(1, S5_W)])
    dyg2 = _mm(dz2, W['s5_w_glu'], 'nt', "mm_glu_dx")
    G['s5_w_glu'] = _mm(yg, dz2, 'tn', "mm_glu_dw")

    def mid_bwd_fn(i, n, R, P, X, C):
        _, vjp = jax.vjp(_s5_mid, R[0], R[1], C[0])
        dysc_, du_, dd_ = vjp(R[2] + R[3])
        return (dysc_, du_), (dd_,)

    dysc, du1, G['s5_d'] = _rowcall("s5_mid_bwd", mid_bwd_fn, L, TM, [ysc, u, dyg1, dyg2], [S['s5_d']],
                                    out_rows=[row(S5_W)] * 2, out_accs=[(1, S5_W)])
    dst = _mm(dysc, cmat, 'nt', "mm_cs_dx")
    dcmat = _mm(st, dysc, 'tn', "mm_cs_dw")
    lam = _s5_scan(dst, abar, True, "s5_scan_bwd")

    def s5_da_fn(i, n, R, P, X, C):
        lr, li = R[0][:, :S5_N], R[0][:, S5_N:]
        sp = _shift_down(R[1], P[0], i, 1)
        sr, si = sp[:, :S5_N], sp[:, S5_N:]
        return (), (jnp.concatenate([_sum0(lr * sr + li * si), _sum0(li * sr - lr * si)], axis=1),)

    (dabar,) = _rowcall("s5_da", s5_da_fn, L, TM, [lam, st], out_accs=[(1, 2 * S5_N)], prev=[1])
    du2 = _mm(lam, bmat, 'nt', "mm_bu_dx")
    dbmat = _mm(u, lam, 'tn', "mm_bu_dw")
    da_re, da_im, dls, db_re, db_im = _s5_disc_bwd(
        a_re, a_im, ls, b_re, b_im, dabar[:, :S5_N].reshape(S5_N, 1), dabar[:, S5_N:].reshape(S5_N, 1),
        undiag_in(dbmat[:, :S5_N]), undiag_in(dbmat[:, S5_N:]), seg)
    G['s5_a_re'], G['s5_a_im'], G['s5_log_step'] = da_re, da_im, dls
    G['s5_b_re'], G['s5_b_im'] = db_re, db_im
    G['s5_c_re'], G['s5_c_im'] = undiag_out(dcmat[:S5_N]), -undiag_out(dcmat[S5_N:])

    def out_bwd_fn(i, n, R, P, X, C):
        _, vjp = jax.vjp(_rwkv_out, *R[:5], *C)
        gs = vjp(R[5])
        return gs[:5], gs[5:8]

    dy, dr1, dk1, dv1, dg, dlw, dlb, drk = _rowcall("rwkv_out_bwd", out_bwd_fn, L, TM, [y, r, k2, v, g, do_a], out_consts,
                                                    out_rows=[row(RWKV_W)] * 5, out_accs=[(1, RWKV_W)] * 3)
    G['rwkv_lnx_w'], G['rwkv_lnx_b'], G['rwkv_r_k'] = dlw, dlb, drk
    dr2, ddec, dk2b, dan, dbv, dv2 = _wkv7_bwd(hd(r), hd(decay), hd(k2), hd(an), hd(bv), v, ck, dy)
    fl = lambda t: t.reshape(L, RWKV_W)

    def prep_bwd_fn(i, n, R, P, X, C):
        p_ = R[0]
        d1 = _shift_down(p_, P[0], i, 1) - p_
        q = p_ + d1 * C[0]
        _, vjp = jax.vjp(_prep, q, *C[1:])
        cots = (R[1] + R[2], R[3], R[4] + R[5], R[6] + R[7], R[8], R[9], R[10])
        gs = vjp(cots)
        return (gs[0],), (_sum0(gs[0] * d1),) + tuple(gs[1:8])

    small, lowr = (1, RWKV_W), (128, RWKV_W)
    dq, dmu, dw0, da0, dkk, dka, dw2p, da2p, dg2 = _rowcall(
        "rwkv_prep_bwd", prep_bwd_fn, L, TM, [p, dr1, fl(dr2), fl(ddec), dk1, fl(dk2b), dv1, dv2, fl(dan), fl(dbv), dg],
        prep_consts, out_rows=[row(N_RWKV)], out_accs=[(1, N_RWKV)] + [small] * 4 + [lowr] * 3, prev=[0])
    G['rwkv_shift_mu'], G['rwkv_w0'], G['rwkv_a0'], G['rwkv_k_k'], G['rwkv_k_a'] = dmu, dw0, da0, dkk, dka
    G['rwkv_w2'], G['rwkv_a2'], G['rwkv_g2'] = dw2p[:64], da2p[64:], dg2

    def shift_bwd_fn(i, n, R, P, X, C):
        dm = R[0] * C[0]
        return (R[0] - dm + _shift_up(dm, X[0] * C[0], i, n, 1),), ()

    (dp,) = _rowcall("shift_bwd", shift_bwd_fn, L, TM, [dq], [S['rwkv_shift_mu']], out_rows=[row(N_RWKV)], nxt=[0])

    (du,) = _rowcall("add_du", lambda i, n, R, P, X, C: ((R[0] + R[1],), ()), L, TM, [du1, du2], out_rows=[row(S5_W)])
    dproj = jnp.concatenate([dp, du, dgp], axis=1)
    dh1 = _mm(dproj, w_in, 'nt', "mm_in_dx")
    G['w_in'] = _mm(h1, dproj, 'tn', "mm_in_dw")

    def norm1_bwd_fn(i, n, R, P, X, C):
        _, vjp = jax.vjp(_rms, R[0], C[0])
        dxa, dg1_ = vjp(R[2])
        return (R[1] + dxa,), (dg1_,)

    dx, G['norm_mix_pre'] = _rowcall("norm_pre_bwd", norm1_bwd_fn, L, TM, [x, dx1, dh1], [g1],
                                     out_rows=[row(D_MODEL)], out_accs=[(1, D_MODEL)])
    return loss, dx, G


def kernel(x, norm_mix_pre, norm_mix_post, norm_ffn_pre, norm_ffn_post, w_in, b_gate, rwkv_shift_mu, rwkv_w0, rwkv_w2, rwkv_a0, rwkv_a2, rwkv_g2, rwkv_k_k, rwkv_k_a, rwkv_r_k, rwkv_lnx_w, rwkv_lnx_b, s5_a_re, s5_a_im, s5_b_re, s5_b_im, s5_c_re, s5_c_im, s5_d, s5_log_step, s5_w_glu, s5_b_glu, w_branch_rwkv, w_branch_s5, w_out, ffn_w_up, ffn_conv_w, ffn_conv_b, ffn_w_down, loss_target, m_norm_mix_pre, m_norm_mix_post, m_norm_ffn_pre, m_norm_ffn_post, m_w_in, m_b_gate, m_rwkv_shift_mu, m_rwkv_w0, m_rwkv_w2, m_rwkv_a0, m_rwkv_a2, m_rwkv_g2, m_rwkv_k_k, m_rwkv_k_a, m_rwkv_r_k, m_rwkv_lnx_w, m_rwkv_lnx_b, m_s5_a_re, m_s5_a_im, m_s5_b_re, m_s5_b_im, m_s5_c_re, m_s5_c_im, m_s5_d, m_s5_log_step, m_s5_w_glu, m_s5_b_glu, m_w_branch_rwkv, m_w_branch_s5, m_w_out, m_ffn_w_up, m_ffn_conv_w, m_ffn_conv_b, m_ffn_w_down, v_norm_mix_pre, v_norm_mix_post, v_norm_ffn_pre, v_norm_ffn_post, v_w_in, v_b_gate, v_rwkv_shift_mu, v_rwkv_w0, v_rwkv_w2, v_rwkv_a0, v_rwkv_a2, v_rwkv_g2, v_rwkv_k_k, v_rwkv_k_a, v_rwkv_r_k, v_rwkv_lnx_w, v_rwkv_lnx_b, v_s5_a_re, v_s5_a_im, v_s5_b_re, v_s5_b_im, v_s5_c_re, v_s5_c_im, v_s5_d, v_s5_log_step, v_s5_w_glu, v_s5_b_glu, v_w_branch_rwkv, v_w_branch_s5, v_w_out, v_ffn_w_up, v_ffn_conv_w, v_ffn_conv_b, v_ffn_w_down):
    A = dict(locals())
    L = x.shape[1]
    big_shapes = [A[n].shape for n in BIG]
    small_shapes = [A[n].shape for n in SMALL]
    small_rows = sum(_ceil_to(A[n].size, PACK_W) // PACK_W for n in SMALL)
    R = _ceil_to(BIG_ROWS + small_rows + 1, 32)

    wpack = _pack([A[n] for n in BIG], [A[n] for n in SMALL], R)
    wrows = _ceil_to(BIG_ROWS, 16)
    frows = _ceil_to(sum(_ceil_to(_shard_elems(n), PACK_W) // PACK_W for n in BIG_F32), 8)
    gathered = _chip_exchange(wpack[:wrows].astype(bf16), True, "gather_weights").reshape(4, -1)
    gathered32 = _chip_exchange(wpack[:frows], True, "gather_weights_f32").reshape(4, -1)
    W, off = {}, 0
    for n in BIG:
        ne = _shard_elems(n)
        W[n] = _join_shards(n, (gathered32 if n in BIG_F32 else gathered)[:, off:off + ne])
        off += _ceil_to(ne, PACK_W)
    S = {n: A[n].reshape(1, -1) for n in SMALL}
    S['ffn_conv_w_full'] = W['ffn_conv_w']

    loss, dx, G = _forward_backward(x[0], loss_target[0], W, S)

    big_parts = jnp.concatenate(
        [jnp.pad(s, ((0, 0), (0, _ceil_to(s.shape[1], PACK_W) - s.shape[1])))
         for s in (_split_shards(n, G[n]) for n in BIG)], axis=1)
    small_flat = jnp.concatenate([_flat_pad(G[n]) for n in SMALL] + [loss.reshape(-1)])
    small_flat = jnp.pad(small_flat, (0, (R - BIG_ROWS) * PACK_W - small_flat.shape[0]))
    gpack = jnp.concatenate([big_parts, jnp.broadcast_to(small_flat, (4, small_flat.shape[0]))], axis=1)
    gpack = gpack.reshape(4, R, PACK_W)
    tm_half = R // 2 // 5
    from_sibling = _send_other_half(gpack, "grads_to_sibling")
    chip_sum = _add_my_half(gpack, from_sibling, tm_half, "grads_pair_sum")
    from_chips = _chip_exchange(chip_sum, False, "grads_chip_exchange")
    my_half = _sum_slots(from_chips, tm_half, "grads_chip_sum")
    gsum = _pair_place(my_half, "grads_to_sibling_back").reshape(R, PACK_W)

    mpack = _pack([A['m_' + n] for n in BIG], [A['m_' + n] for n in SMALL], R)
    vpack = _pack([A['v_' + n] for n in BIG], [A['v_' + n] for n in SMALL], R)
    delta, new_m, new_v = _adamw(wpack, gsum, mpack, vpack, R // 10)

    def named(buf):
        vals = _unpack(buf, big_shapes, small_shapes)
        d = dict(zip(BIG + SMALL, vals))
        return [d[n] for n in WEIGHTS]

    loss_out = gsum[BIG_ROWS + small_rows, 0]
    return (loss_out, dx[None], *named(gsum), *named(delta), *named(new_m), *named(new_v))
```

```python
import functools

import jax
import jax.numpy as jnp
from jax import lax
from jax.experimental import pallas as pl
from jax.experimental.pallas import tpu as pltpu

f32, bf16 = jnp.float32, jnp.bfloat16
MESH = pl.DeviceIdType.MESH

D_MODEL = 1024
RWKV_W = 512
HEADS, HEAD = 8, 64
N_RWKV = 1792
S5_W = 512
S5_G, S5_P, S5_C = 32, 64, 16
S5_N = S5_G * S5_P
D_FF = 2816
NORM_EPS = 1e-6
LNX_EPS = 64e-5
ADAM_LR, ADAM_B1, ADAM_B2, ADAM_EPS, ADAM_WD, ADAM_STEP = 0.001, 0.9, 0.999, 1e-08, 0.01, 10

VMEM_LIMIT = 48 * 1024 * 1024
PACK_W = 1024
WKV_C = 64
S5_T = 256

WEIGHTS = ['norm_mix_pre', 'norm_mix_post', 'norm_ffn_pre', 'norm_ffn_post', 'w_in', 'b_gate', 'rwkv_shift_mu',
           'rwkv_w0', 'rwkv_w2', 'rwkv_a0', 'rwkv_a2', 'rwkv_g2', 'rwkv_k_k', 'rwkv_k_a', 'rwkv_r_k', 'rwkv_lnx_w',
           'rwkv_lnx_b', 's5_a_re', 's5_a_im', 's5_b_re', 's5_b_im', 's5_c_re', 's5_c_im', 's5_d', 's5_log_step',
           's5_w_glu', 's5_b_glu', 'w_branch_rwkv', 'w_branch_s5', 'w_out', 'ffn_w_up', 'ffn_conv_w', 'ffn_conv_b',
           'ffn_w_down']
SHARDED = {'w_in': ((1024, 4352), 1), 'rwkv_w2': ((64, 512), 1), 'rwkv_a2': ((64, 512), 1), 'rwkv_g2': ((128, 512), 1),
           's5_w_glu': ((512, 512), 0), 'w_branch_rwkv': ((512, 1024), 1), 'w_branch_s5': ((512, 1024), 1),
           'w_out': ((1024, 1024), 0), 'ffn_w_up': ((1024, 5632), 1), 'ffn_conv_w': ((3, 5632), 1),
           'ffn_w_down': ((2816, 1024), 0)}
BIG_F32 = ['rwkv_w2', 'rwkv_a2', 'rwkv_g2', 'ffn_conv_w']
BIG = BIG_F32 + [n for n in WEIGHTS if n in SHARDED and n not in BIG_F32]
SMALL = [n for n in WEIGHTS if n not in SHARDED]


def _ceil_to(n, m):
    return -(-n // m) * m


def _mesh_pos():
    return lax.axis_index("x"), lax.axis_index("y"), lax.axis_index("c")


def _pick(d, cap=4096):
    for c in (1024, 1408, 2176, 896, 512, 256, 128):
        if c <= cap and d % c == 0:
            return c
    raise ValueError(d)


def _mm(a, b, mode, name, out_dtype=f32):
    if mode == 'tn':
        (K, M), (K2, N) = a.shape, b.shape
    elif mode == 'nt':
        (M, K), (N, K2) = a.shape, b.shape
    else:
        (M, K), (K2, N) = a.shape, b.shape
    assert K == K2, (name, a.shape, b.shape)
    tm = _pick(M, 512)
    tn = _pick(N)
    tk = _pick(K, 512) if mode == 'tn' else _pick(K)
    nk = K // tk
    dims = {'nn': ((1,), (0,)), 'nt': ((1,), (1,)), 'tn': ((0,), (0,))}[mode]

    def body(a_ref, b_ref, o_ref, acc_ref):
        k = pl.program_id(2)

        @pl.when(k == 0)
        def _():
            acc_ref[...] = jnp.zeros_like(acc_ref)

        acc_ref[...] += lax.dot_general(a_ref[...].astype(bf16), b_ref[...].astype(bf16), (dims, ((), ())),
                                        preferred_element_type=f32)

        @pl.when(k == nk - 1)
        def _():
            o_ref[...] = acc_ref[...].astype(o_ref.dtype)

    a_spec = pl.BlockSpec((tk, tm), lambda i, j, k: (k, i)) if mode == 'tn' else pl.BlockSpec((tm, tk), lambda i, j, k: (i, k))
    b_spec = pl.BlockSpec((tn, tk), lambda i, j, k: (j, k)) if mode == 'nt' else pl.BlockSpec((tk, tn), lambda i, j, k: (k, j))
    return pl.pallas_call(
        body, name=name, grid=(M // tm, N // tn, nk),
        in_specs=[a_spec, b_spec], out_specs=pl.BlockSpec((tm, tn), lambda i, j, k: (i, j)),
        out_shape=jax.ShapeDtypeStruct((M, N), out_dtype),
        scratch_shapes=[pltpu.VMEM((tm, tn), f32)],
        compiler_params=pltpu.CompilerParams(dimension_semantics=("parallel", "parallel", "arbitrary"),
                                             vmem_limit_bytes=VMEM_LIMIT),
    )(a, b)


def _rowcall(name, fn, L, tm, rows, consts=(), out_rows=(), out_accs=(), prev=(), nxt=()):
    nsteps = L // tm
    nb8 = tm // 8
    last8 = L // 8 - 1
    n_r, n_p, n_x, n_c, n_or = len(rows), len(prev), len(nxt), len(consts), len(out_rows)

    def body(*refs):
        i = pl.program_id(0)
        vals = [r[...] for r in refs[:n_r + n_p + n_x + n_c]]
        R, P = vals[:n_r], vals[n_r:n_r + n_p]
        X, C = vals[n_r + n_p:n_r + n_p + n_x], vals[n_r + n_p + n_x:]
        o_refs = refs[n_r + n_p + n_x + n_c:]
        outs_r, outs_a = fn(i, nsteps, R, P, X, C)
        for ref, v in zip(o_refs[:n_or], outs_r, strict=True):
            ref[...] = v.astype(ref.dtype)
        if out_accs:
            @pl.when(i == 0)
            def _():
                for ref in o_refs[n_or:]:
                    ref[...] = jnp.zeros_like(ref)

            for ref, v in zip(o_refs[n_or:], outs_a, strict=True):
                ref[...] += v

    def const_spec(c):
        nd = c.ndim
        return pl.BlockSpec(c.shape, lambda i: (0,) * nd)

    in_specs = ([pl.BlockSpec((tm, a.shape[1]), lambda i: (i, 0)) for a in rows]
                + [pl.BlockSpec((8, rows[j].shape[1]), lambda i: (jnp.maximum(i * nb8 - 1, 0), 0)) for j in prev]
                + [pl.BlockSpec((8, rows[j].shape[1]), lambda i: (jnp.minimum((i + 1) * nb8, last8), 0)) for j in nxt]
                + [const_spec(c) for c in consts])
    out_specs = ([pl.BlockSpec((tm, c), lambda i: (i, 0)) for c, _ in out_rows]
                 + [pl.BlockSpec(s, lambda i: (0, 0)) for s in out_accs])
    out_shape = ([jax.ShapeDtypeStruct((L, c), dt) for c, dt in out_rows]
                 + [jax.ShapeDtypeStruct(s, f32) for s in out_accs])
    args = list(rows) + [rows[j] for j in prev] + [rows[j] for j in nxt] + list(consts)
    return pl.pallas_call(
        body, name=name, grid=(nsteps,), in_specs=in_specs, out_specs=out_specs, out_shape=out_shape,
        compiler_params=pltpu.CompilerParams(dimension_semantics=("arbitrary",), vmem_limit_bytes=VMEM_LIMIT),
    )(*args)


def _shift_down(x, prev8, i, k):
    rolled = pltpu.roll(x, k, axis=0)
    pfix = jnp.where(i > 0, pltpu.roll(prev8, k, axis=0), 0.0)
    row8 = lax.broadcasted_iota(jnp.int32, pfix.shape, 0)
    top = jnp.where(row8 < k, pfix, rolled[:8])
    return jnp.concatenate([top, rolled[8:]], axis=0)


def _shift_up(x, next8, i, nsteps, k):
    tm = x.shape[0]
    rolled = pltpu.roll(x, tm - k, axis=0)
    nfix = jnp.where(i < nsteps - 1, pltpu.roll(next8, 8 - k, axis=0), 0.0)
    row8 = lax.broadcasted_iota(jnp.int32, nfix.shape, 0)
    bot = jnp.where(row8 >= 8 - k, nfix, rolled[tm - 8:])
    return jnp.concatenate([rolled[:tm - 8], bot], axis=0)


def _sum0(x):
    return jnp.sum(x, axis=0, keepdims=True)


def _rms(x, g):
    return x * lax.rsqrt(jnp.mean(x * x, axis=-1, keepdims=True) + NORM_EPS) * g


def _softplus(x):
    return jnp.maximum(x, 0.0) + jnp.log(1.0 + jnp.exp(-jnp.abs(x)))


def _gelu(x):
    return 0.5 * x * (1.0 + jnp.tanh(0.7978845608028654 * (x + 0.044715 * x * x * x)))


def _dot32(a, b):
    return jnp.dot(a, b, preferred_element_type=f32, precision=lax.Precision.HIGHEST)


def _prep(q, w0, a0, k_k, k_a, w2p, a2p, g2, E):
    r, k, v = q[:, 0:512], q[:, 512:1024], q[:, 1024:1536]
    wa, gd = q[:, 1536:1664], q[:, 1664:1792]
    wlog = -_softplus(-(w0 + _dot32(jnp.tanh(wa), w2p))) - 0.5
    lw = -jnp.exp(wlog)
    a = jax.nn.sigmoid(a0 + _dot32(wa, a2p))
    g = _dot32(jax.nn.sigmoid(gd), g2)
    kk = k * k_k
    kkn = kk / jnp.maximum(jnp.sqrt(_dot32(kk * kk, E)), 1e-12)
    k2 = k * (1.0 + (a - 1.0) * k_a)
    return r, lw, k2, v, -kkn, kkn * a, g


def _rwkv_out(y, r, k2, v, g, lnx_w, lnx_b, r_k, E):
    mean = _dot32(y, E) * (1.0 / HEAD)
    yc = y - mean
    var = _dot32(yc * yc, E) * (1.0 / HEAD)
    yn = yc * lax.rsqrt(var + LNX_EPS) * lnx_w + lnx_b
    bonus = _dot32(r * k2 * r_k, E) * v
    return (yn + bonus) * g


def _s5_mid(ysc, u, d):
    return _gelu(ysc + d * u)


def _s5_glu(yg, z2, b_glu):
    return yg * jax.nn.sigmoid(z2 + b_glu)


def _merge(gp, o_r, o_s, b_gate):
    gates = jax.nn.sigmoid(gp + b_gate)
    return gates[:, :D_MODEL] * o_r + gates[:, D_MODEL:] * o_s


def _act(zc):
    return _gelu(zc[:, :D_FF]) * zc[:, D_FF:]


def _s5_disc(a_re, a_im, ls, b_re, b_im):
    dt = jnp.exp(ls)
    er = jnp.exp(a_re * dt)
    ar, ai = er * jnp.cos(a_im * dt), er * jnp.sin(a_im * dt)
    x, y = ar - 1.0, ai
    den = a_re * a_re + a_im * a_im
    fr, fi = (x * a_re + y * a_im) / den, (y * a_re - x * a_im) / den
    return ar, ai, fr * b_re - fi * b_im, fr * b_im + fi * b_re


_DIMS = {'nn': ((1,), (0,)), 'nt': ((1,), (1,)), 'tn': ((0,), (0,))}


def _raw_bdot(a, b, mode):
    return lax.dot_general(a.astype(bf16), b.astype(bf16), (_DIMS[mode], ((), ())), preferred_element_type=f32)


@functools.partial(jax.custom_vjp, nondiff_argnums=(2,))
def _bdot(a, b, mode):
    return _raw_bdot(a, b, mode)


def _bdot_fwd(a, b, mode):
    return _raw_bdot(a, b, mode), (a, b)


def _bdot_bwd(mode, res, g):
    a, b = res
    if mode == 'nn':
        return _raw_bdot(g, b, 'nt'), _raw_bdot(a, g, 'tn')
    if mode == 'nt':
        return _raw_bdot(g, b, 'nn'), _raw_bdot(g, a, 'tn')
    return _raw_bdot(b, g, 'nt'), _raw_bdot(a, g, 'nn')


_bdot.defvjp(_bdot_fwd, _bdot_bwd)


def _wkv_chunk(S0, r, lw, k, v, a, b, tri, bd):
    C = r[0].shape[0]
    P = range(len(r))
    lane = lax.broadcasted_iota(jnp.int32, (1, 2 * HEAD), 1)
    halves = [(lane < HEAD).astype(f32), (lane >= HEAD).astype(f32)]
    eye = (lax.broadcasted_iota(jnp.int32, (C, C), 0) == lax.broadcasted_iota(jnp.int32, (C, C), 1)).astype(f32)
    sl = tri - eye
    cum = [_dot32(tri, lw[p]) for p in P]
    g = [jnp.exp(cum[p]) for p in P]
    gi = [jnp.exp(-cum[p]) for p in P]
    at = [a[p] * jnp.exp(cum[p] - lw[p]) for p in P]
    rt = [r[p] * g[p] for p in P]
    kb = [k[p] * gi[p] for p in P]
    bb = [b[p] * gi[p] for p in P]
    PE = [(p, e) for p in P for e in range(2)]
    atm = {pe: at[pe[0]] * halves[pe[1]] for pe in PE}
    rtm = {pe: rt[pe[0]] * halves[pe[1]] for pe in PE}
    aab = {pe: _bdot(atm[pe], bb[pe[0]], 'nt') * sl for pe in PE}
    aak = {pe: _bdot(atm[pe], kb[pe[0]], 'nt') * sl for pe in PE}
    rk = {pe: _bdot(rtm[pe], kb[pe[0]], 'nt') * tri for pe in PE}
    rb = {pe: _bdot(rtm[pe], bb[pe[0]], 'nt') * tri for pe in PE}
    rhs = [_bdot(at[p], S0[p], 'nt') + sum(halves[e] * _bdot(aak[(p, e)], v[p], 'nn') for e in range(2)) for p in P]
    y0 = [_bdot(rt[p], S0[p], 'nt') + sum(halves[e] * _bdot(rk[(p, e)], v[p], 'nn') for e in range(2)) for p in P]
    x = {pe: eye + aab[pe] for pe in PE}
    pw = aab
    n = 1
    while 2 * n < C:
        pw = {pe: _bdot(pw[pe], pw[pe], 'nn') for pe in PE}
        x = {pe: x[pe] + _bdot(x[pe], pw[pe], 'nn') for pe in PE}
        n *= 2
    u = [sum(halves[e] * _bdot(x[(p, e)], rhs[p], 'nn') for e in range(2)) for p in P]
    y = [y0[p] + sum(halves[e] * _bdot(rb[(p, e)], u[p], 'nn') for e in range(2)) for p in P]
    S1 = [g[p][C - 1:C, :] * (S0[p] + bd * (_bdot(v[p], kb[p], 'tn') + _bdot(u[p], bb[p], 'tn'))) for p in P]
    return y, S1


def _pairs(x):
    return [x[:, 2 * HEAD * p:2 * HEAD * (p + 1)] for p in range(HEADS // 2)]


def _wkv_consts():
    tri = jnp.tril(jnp.ones((WKV_C, WKV_C), f32))
    hid = jnp.arange(2 * HEAD) // HEAD
    return tri, (hid[:, None] == hid[None, :]).astype(f32)


def _wkv7_fwd(r, lw, k, v, a, b):
    L = r.shape[0]
    nc, npair = L // WKV_C, HEADS // 2

    def body(r_ref, lw_ref, k_ref, v_ref, a_ref, b_ref, tri_ref, bd_ref, y_ref, ck_ref, s_ref):
        @pl.when(pl.program_id(0) == 0)
        def _():
            s_ref[...] = jnp.zeros_like(s_ref)

        s0 = [s_ref[p] for p in range(npair)]
        for p in range(npair):
            ck_ref[0, p] = s0[p]
        y, s1 = _wkv_chunk(s0, *(_pairs(x) for x in (r_ref, lw_ref, k_ref, v_ref, a_ref, b_ref)), tri_ref[...], bd_ref[...])
        for p in range(npair):
            y_ref[:, 2 * HEAD * p:2 * HEAD * (p + 1)] = y[p]
            s_ref[p] = s1[p]

    row = pl.BlockSpec((WKV_C, RWKV_W), lambda c: (c, 0))
    sspec = pl.BlockSpec((1, npair, 2 * HEAD, 2 * HEAD), lambda c: (c, 0, 0, 0))
    return pl.pallas_call(
        body, name="wkv7_fwd", grid=(nc,),
        in_specs=[row] * 6 + [pl.BlockSpec((WKV_C, WKV_C), lambda c: (0, 0)), pl.BlockSpec((2 * HEAD, 2 * HEAD), lambda c: (0, 0))],
        out_specs=[row, sspec],
        out_shape=[jax.ShapeDtypeStruct((L, RWKV_W), f32), jax.ShapeDtypeStruct((nc, npair, 2 * HEAD, 2 * HEAD), f32)],
        scratch_shapes=[pltpu.VMEM((npair, 2 * HEAD, 2 * HEAD), f32)],
        compiler_params=pltpu.CompilerParams(dimension_semantics=("arbitrary",), vmem_limit_bytes=VMEM_LIMIT),
    )(r, lw, k, v, a, b, *_wkv_consts())


def _wkv7_bwd(r, lw, k, v, a, b, ck, dy):
    L = r.shape[0]
    nc, npair = L // WKV_C, HEADS // 2

    def body(r_ref, lw_ref, k_ref, v_ref, a_ref, b_ref, ck_ref, dy_ref, tri_ref, bd_ref,
             dr_ref, dlw_ref, dk_ref, dv_ref, da_ref, db_ref, ds_ref):
        @pl.when(pl.program_id(0) == 0)
        def _():
            ds_ref[...] = jnp.zeros_like(ds_ref)

        tri, bd = tri_ref[...], bd_ref[...]
        ins = [[ck_ref[0, p] for p in range(npair)]] + [_pairs(x) for x in (r_ref, lw_ref, k_ref, v_ref, a_ref, b_ref)]
        _, vjp = jax.vjp(lambda *t: _wkv_chunk(*t, tri, bd), *ins)
        gs = vjp((_pairs(dy_ref), [ds_ref[p] for p in range(npair)]))
        for p in range(npair):
            ds_ref[p] = gs[0][p]
            for ref, gval in zip((dr_ref, dlw_ref, dk_ref, dv_ref, da_ref, db_ref), gs[1:]):
                ref[:, 2 * HEAD * p:2 * HEAD * (p + 1)] = gval[p]

    row = pl.BlockSpec((WKV_C, RWKV_W), lambda c: (nc - 1 - c, 0))
    sspec = pl.BlockSpec((1, npair, 2 * HEAD, 2 * HEAD), lambda c: (nc - 1 - c, 0, 0, 0))
    return pl.pallas_call(
        body, name="wkv7_bwd", grid=(nc,),
        in_specs=[row] * 6 + [sspec, row, pl.BlockSpec((WKV_C, WKV_C), lambda c: (0, 0)),
                              pl.BlockSpec((2 * HEAD, 2 * HEAD), lambda c: (0, 0))],
        out_specs=[row] * 6,
        out_shape=[jax.ShapeDtypeStruct((L, RWKV_W), f32)] * 6,
        scratch_shapes=[pltpu.VMEM((npair, 2 * HEAD, 2 * HEAD), f32)],
        compiler_params=pltpu.CompilerParams(dimension_semantics=("arbitrary",), vmem_limit_bytes=VMEM_LIMIT),
    )(r, lw, k, v, a, b, ck, dy, *_wkv_consts())


def _cmul(ar, ai, xr, xi):
    return ar * xr - ai * xi, ar * xi + ai * xr


def _s5_scan(x, abar, reverse, name):
    L = x.shape[0]
    nt = L // S5_T
    ng = S5_T // 8

    def body(x_ref, a_ref, o_ref, car_ref, pw_ref):
        @pl.when(pl.program_id(0) == 0)
        def _():
            car_ref[...] = jnp.zeros_like(car_ref)
            ar = jnp.broadcast_to(a_ref[:, :S5_N], (8, S5_N))
            ai = jnp.broadcast_to(a_ref[:, S5_N:], (8, S5_N))
            if reverse:
                ai = -ai
            row = lax.broadcasted_iota(jnp.int32, (8, S5_N), 0)
            pr, pi = ar, ai
            qr, qi = jnp.zeros((8, S5_N), f32), jnp.zeros((8, S5_N), f32)
            for e in range(1, 9):
                sel = (row == 8 - e) if reverse else (row == e - 1)
                qr, qi = jnp.where(sel, pr, qr), jnp.where(sel, pi, qi)
                if e in (1, 2, 4):
                    j = (1, 2, 4).index(e)
                    pw_ref[j, :, :S5_N] = pr
                    pw_ref[j, :, S5_N:] = pi
                pr, pi = _cmul(pr, pi, ar, ai)
            pw_ref[3, :, :S5_N] = qr
            pw_ref[3, :, S5_N:] = qi

        row = lax.broadcasted_iota(jnp.int32, (8, S5_N), 0)

        def group(gi, carry):
            g = (ng - 1 - gi) if reverse else gi
            t0 = pl.multiple_of(g * 8, 8)
            xr, xi = x_ref[pl.ds(t0, 8), :S5_N], x_ref[pl.ds(t0, 8), S5_N:]
            for j, d in enumerate((1, 2, 4)):
                if reverse:
                    sr = jnp.where(row < 8 - d, pltpu.roll(xr, 8 - d, axis=0), 0.0)
                    si = jnp.where(row < 8 - d, pltpu.roll(xi, 8 - d, axis=0), 0.0)
                else:
                    sr = jnp.where(row >= d, pltpu.roll(xr, d, axis=0), 0.0)
                    si = jnp.where(row >= d, pltpu.roll(xi, d, axis=0), 0.0)
                mr, mi = _cmul(pw_ref[j, :, :S5_N], pw_ref[j, :, S5_N:], sr, si)
                xr, xi = xr + mr, xi + mi
            cr, ci = carry
            mr, mi = _cmul(pw_ref[3, :, :S5_N], pw_ref[3, :, S5_N:], cr, ci)
            xr, xi = xr + mr, xi + mi
            o_ref[pl.ds(t0, 8), :S5_N] = xr
            o_ref[pl.ds(t0, 8), S5_N:] = xi
            e = 0 if reverse else 7
            return (jnp.broadcast_to(xr[e:e + 1, :], (8, S5_N)), jnp.broadcast_to(xi[e:e + 1, :], (8, S5_N)))

        cr, ci = lax.fori_loop(0, ng, group, (car_ref[:, :S5_N], car_ref[:, S5_N:]))
        car_ref[:, :S5_N] = cr
        car_ref[:, S5_N:] = ci

    imap = (lambda i: (nt - 1 - i, 0)) if reverse else (lambda i: (i, 0))
    return pl.pallas_call(
        body, name=name, grid=(nt,),
        in_specs=[pl.BlockSpec((S5_T, 2 * S5_N), imap), pl.BlockSpec((1, 2 * S5_N), lambda i: (0, 0))],
        out_specs=pl.BlockSpec((S5_T, 2 * S5_N), imap),
        out_shape=jax.ShapeDtypeStruct((L, 2 * S5_N), f32),
        scratch_shapes=[pltpu.VMEM((8, 2 * S5_N), f32), pltpu.VMEM((4, 8, 2 * S5_N), f32)],
        compiler_params=pltpu.CompilerParams(dimension_semantics=("arbitrary",), vmem_limit_bytes=VMEM_LIMIT),
    )(x, abar)


def _s5_disc_fwd(a_re, a_im, ls, b_re, b_im):
    def body(a_re_ref, a_im_ref, ls_ref, b_re_ref, b_im_ref, ar_ref, ai_ref, br_ref, bi_ref):
        outs = _s5_disc(a_re_ref[...], a_im_ref[...], ls_ref[...], b_re_ref[...], b_im_ref[...])
        for ref, v in zip((ar_ref, ai_ref, br_ref, bi_ref), outs):
            ref[...] = v

    c1, c16 = jax.ShapeDtypeStruct((S5_N, 1), f32), jax.ShapeDtypeStruct((S5_N, S5_C), f32)
    return pl.pallas_call(body, name="s5_disc", out_shape=[c1, c1, c16, c16])(a_re, a_im, ls, b_re, b_im)


def _s5_disc_bwd(a_re, a_im, ls, b_re, b_im, d_ar, d_ai, d_br, d_bi, seg):
    def body(a_re_ref, a_im_ref, ls_ref, b_re_ref, b_im_ref, g1, g2, g3, g4, seg_ref, o1, o2, o3, o4, o5):
        _, vjp = jax.vjp(_s5_disc, a_re_ref[...], a_im_ref[...], ls_ref[...], b_re_ref[...], b_im_ref[...])
        da_re, da_im, dls, db_re, db_im = vjp((g1[...], g2[...], g3[...], g4[...]))
        o1[...] = da_re
        o2[...] = da_im
        o3[...] = _dot32(seg_ref[...], dls)
        o4[...] = db_re
        o5[...] = db_im

    c1, c16 = jax.ShapeDtypeStruct((S5_N, 1), f32), jax.ShapeDtypeStruct((S5_N, S5_C), f32)
    return pl.pallas_call(body, name="s5_disc_bwd", out_shape=[c1, c1, jax.ShapeDtypeStruct((S5_G, 1), f32), c16, c16])(
        a_re, a_im, ls, b_re, b_im, d_ar, d_ai, d_br, d_bi, seg)


ANY = pl.BlockSpec(memory_space=pl.ANY)


def _chip_exchange(x, same_src, name):
    blk = x.shape if same_src else x.shape[1:]

    def body(x_ref, o_ref, ssems, rsems, lsem):
        px, py, pc = _mesh_pos()
        me = 2 * px + py
        local = pltpu.make_async_copy(x_ref if same_src else x_ref.at[me], o_ref.at[me], lsem)
        local.start()
        copies = []
        for k in (1, 2, 3):
            qx = (1 - px) if (k >> 1) else px
            qy = (1 - py) if (k & 1) else py
            src = x_ref if same_src else x_ref.at[2 * qx + qy]
            cp = pltpu.make_async_remote_copy(src, o_ref.at[me], ssems.at[k - 1], rsems.at[k - 1],
                                              device_id=(qx, qy, pc), device_id_type=MESH)
            cp.start()
            copies.append(cp)
        for cp in copies:
            cp.wait()
        local.wait()

    return pl.pallas_call(
        body, name=name, in_specs=[ANY], out_specs=ANY,
        out_shape=jax.ShapeDtypeStruct((4,) + tuple(blk), x.dtype),
        scratch_shapes=[pltpu.SemaphoreType.DMA((3,)), pltpu.SemaphoreType.DMA((3,)), pltpu.SemaphoreType.DMA],
    )(x)


def _send_other_half(g, name):
    _, R, Wd = g.shape
    rh = R // 2

    def body(g_ref, o_ref, ssem, rsem):
        px, py, pc = _mesh_pos()
        off = pl.multiple_of((1 - pc) * rh, 8)
        cp = pltpu.make_async_remote_copy(g_ref.at[:, pl.ds(off, rh), :], o_ref, ssem, rsem,
                                          device_id=(px, py, 1 - pc), device_id_type=MESH)
        cp.start()
        cp.wait()

    return pl.pallas_call(
        body, name=name, in_specs=[ANY], out_specs=ANY, out_shape=jax.ShapeDtypeStruct((4, rh, Wd), g.dtype),
        scratch_shapes=[pltpu.SemaphoreType.DMA, pltpu.SemaphoreType.DMA],
    )(g)


def _pair_place(xh, name):
    def body(x_ref, o_ref, ssem, rsem, lsem):
        px, py, pc = _mesh_pos()
        local = pltpu.make_async_copy(x_ref, o_ref.at[pc], lsem)
        local.start()
        cp = pltpu.make_async_remote_copy(x_ref, o_ref.at[pc], ssem, rsem, device_id=(px, py, 1 - pc), device_id_type=MESH)
        cp.start()
        cp.wait()
        local.wait()

    return pl.pallas_call(
        body, name=name, in_specs=[ANY], out_specs=ANY, out_shape=jax.ShapeDtypeStruct((2,) + xh.shape, xh.dtype),
        scratch_shapes=[pltpu.SemaphoreType.DMA, pltpu.SemaphoreType.DMA, pltpu.SemaphoreType.DMA],
    )(xh)


def _add_my_half(g, recv, tm, name):
    _, R, Wd = g.shape
    rh = R // 2
    nh = rh // tm
    cidx = lax.axis_index("c").astype(jnp.int32).reshape(1)

    def body(c_ref, g_ref, r_ref, o_ref):
        o_ref[...] = g_ref[...] + r_ref[...]

    return pl.pallas_call(
        body, name=name,
        grid_spec=pltpu.PrefetchScalarGridSpec(
            num_scalar_prefetch=1, grid=(4, nh),
            in_specs=[pl.BlockSpec((1, tm, Wd), lambda j, i, c: (j, c[0] * nh + i, 0)),
                      pl.BlockSpec((1, tm, Wd), lambda j, i, c: (j, i, 0))],
            out_specs=pl.BlockSpec((1, tm, Wd), lambda j, i, c: (j, i, 0))),
        out_shape=jax.ShapeDtypeStruct((4, rh, Wd), f32),
    )(cidx, g, recv)


def _sum_slots(x, tm, name):
    _, rh, Wd = x.shape

    def body(x_ref, o_ref):
        o_ref[...] = ((x_ref[0] + x_ref[1]) + x_ref[2]) + x_ref[3]

    return pl.pallas_call(
        body, name=name, grid=(rh // tm,),
        in_specs=[pl.BlockSpec((4, tm, Wd), lambda i: (0, i, 0))], out_specs=pl.BlockSpec((tm, Wd), lambda i: (i, 0)),
        out_shape=jax.ShapeDtypeStruct((rh, Wd), f32),
        compiler_params=pltpu.CompilerParams(vmem_limit_bytes=VMEM_LIMIT),
    )(x)


def _shard_elems(name):
    shape, _ = SHARDED[name]
    return shape[0] * shape[1] // 4


BIG_ROWS = sum(_ceil_to(_shard_elems(n), PACK_W) // PACK_W for n in BIG)


def _flat_pad(v):
    v = v.reshape(-1)
    return jnp.pad(v, (0, _ceil_to(v.shape[0], PACK_W) - v.shape[0]))


def _pack(big_blocks, small_blocks, rows):
    flat = jnp.concatenate([_flat_pad(b) for b in big_blocks] + [_flat_pad(s) for s in small_blocks])
    return jnp.pad(flat, (0, rows * PACK_W - flat.shape[0])).reshape(rows, PACK_W)


def _split_shards(name, full):
    (r, c), axis = SHARDED[name]
    if axis == 1:
        return full.reshape(r, 4, c // 4).transpose(1, 0, 2).reshape(4, -1)
    return full.reshape(4, -1)


def _join_shards(name, parts):
    (r, c), axis = SHARDED[name]
    if axis == 1:
        return parts.reshape(4, r, c // 4).transpose(1, 0, 2).reshape(r, c)
    return parts.reshape(r, c)


def _unpack(buf, big_shapes, small_shapes):
    flat = buf.reshape(-1)
    out, off = [], 0
    for shp in list(big_shapes) + list(small_shapes):
        n = 1
        for d in shp:
            n *= d
        out.append(flat[off:off + n].reshape(shp))
        off += _ceil_to(n, PACK_W)
    return out


def _adamw(w, g, m, v, tm):
    R = w.shape[0]

    def fn(i, nsteps, Rv, P, X, C):
        w_, g_, m_, v_ = Rv
        m2 = ADAM_B1 * m_ + (1.0 - ADAM_B1) * g_
        v2 = ADAM_B2 * v_ + (1.0 - ADAM_B2) * (g_ * g_)
        m_hat = m2 / (1.0 - ADAM_B1 ** ADAM_STEP)
        v_hat = v2 / (1.0 - ADAM_B2 ** ADAM_STEP)
        delta = -ADAM_LR * (m_hat / (jnp.sqrt(v_hat) + ADAM_EPS) + ADAM_WD * w_)
        return (delta, m2, v2), ()

    return _rowcall("adamw", fn, R, tm, [w, g, m, v], out_rows=[(PACK_W, f32)] * 3)


def _forward_backward(x, tgt, W, S):
    L = x.shape[0]
    TM, TMW = 256, 128
    row = lambda c, dt=f32: (c, dt)
    hid = jnp.arange(RWKV_W) // HEAD
    E = (hid[:, None] == hid[None, :]).astype(f32)
    seg = (jnp.arange(S5_N)[None, :] // S5_P == jnp.arange(S5_G)[:, None]).astype(f32)

    w_in = W['w_in']
    w_p, w_u, w_g = w_in[:, :N_RWKV], w_in[:, N_RWKV:N_RWKV + S5_W], w_in[:, N_RWKV + S5_W:]
    zpad = jnp.zeros((64, RWKV_W), f32)
    w2p = jnp.concatenate([W['rwkv_w2'].astype(f32), zpad], axis=0)
    a2p = jnp.concatenate([zpad, W['rwkv_a2'].astype(f32)], axis=0)
    g2 = W['rwkv_g2'].astype(f32)
    prep_consts = [S['rwkv_shift_mu'], S['rwkv_w0'], S['rwkv_a0'], S['rwkv_k_k'], S['rwkv_k_a'], w2p, a2p, g2, E]
    out_consts = [S['rwkv_lnx_w'], S['rwkv_lnx_b'], S['rwkv_r_k'], E]
    cw, cb = S['ffn_conv_w_full'], S['ffn_conv_b']

    a_re, a_im = S['s5_a_re'].reshape(S5_N, 1), S['s5_a_im'].reshape(S5_N, 1)
    ls = jnp.repeat(S['s5_log_step'].reshape(S5_G, 1), S5_P, axis=0)
    b_re, b_im = S['s5_b_re'].reshape(S5_N, S5_C), S['s5_b_im'].reshape(S5_N, S5_C)
    ar, ai, bbr, bbi = _s5_disc_fwd(a_re, a_im, ls, b_re, b_im)
    abar = jnp.concatenate([ar.reshape(1, S5_N), ai.reshape(1, S5_N)], axis=1)
    eye = jnp.eye(S5_G, dtype=f32)

    def bdiag_in(bb):
        t = bb.reshape(S5_G, S5_P, S5_C).transpose(0, 2, 1)
        return (t[:, :, None, :] * eye[:, None, :, None]).reshape(S5_W, S5_N)

    def bdiag_out(cc):
        t = cc.transpose(0, 2, 1)
        return (t[:, :, None, :] * eye[:, None, :, None]).reshape(S5_N, S5_W)

    def undiag_in(m):
        t = m.reshape(S5_G, S5_C, S5_G, S5_P)
        t = jnp.sum(t * eye[:, None, :, None], axis=2)
        return t.transpose(0, 2, 1).reshape(S5_N, S5_C)

    def undiag_out(m):
        t = m.reshape(S5_G, S5_P, S5_G, S5_C)
        t = jnp.sum(t * eye[:, None, :, None], axis=2)
        return t.transpose(0, 2, 1)

    bmat = jnp.concatenate([bdiag_in(bbr), bdiag_in(bbi)], axis=1).astype(bf16)
    cmat = jnp.concatenate([bdiag_out(S['s5_c_re'].reshape(S5_G, S5_C, S5_P)),
                            -bdiag_out(S['s5_c_im'].reshape(S5_G, S5_C, S5_P))], axis=0).astype(bf16)

    g1, g2n, g3, g4 = S['norm_mix_pre'], S['norm_mix_post'], S['norm_ffn_pre'], S['norm_ffn_post']
    (h1,) = _rowcall("norm_pre", lambda i, n, R, P, X, C: ((_rms(R[0], C[0]),), ()), L, TM, [x], [g1],
                     out_rows=[row(D_MODEL, bf16)])
    p = _mm(h1, w_p, 'nn', "mm_p")
    u = _mm(h1, w_u, 'nn', "mm_u")
    gp = _mm(h1, w_g, 'nn', "mm_g")

    def prep_fn(i, n, R, P, X, C):
        q = R[0] + (_shift_down(R[0], P[0], i, 1) - R[0]) * C[0]
        return _prep(q, *C[1:]), ()

    r, lw, k2, v, an, bv, g = _rowcall("rwkv_prep", prep_fn, L, TM, [p], prep_consts,
                                       out_rows=[row(RWKV_W)] * 7, prev=[0])
    y, ck = _wkv7_fwd(r, lw, k2, v, an, bv)
    (o_a,) = _rowcall("rwkv_out", lambda i, n, R, P, X, C: ((_rwkv_out(*R, *C),), ()), L, TM, [y, r, k2, v, g],
                      out_consts, out_rows=[row(RWKV_W)])
    o_r = _mm(o_a, W['w_branch_rwkv'], 'nn', "mm_br")

    bu = _mm(u, bmat, 'nn', "mm_bu")
    st = _s5_scan(bu, abar, False, "s5_scan")
    ysc = _mm(st, cmat, 'nn', "mm_cs")
    (yg,) = _rowcall("s5_mid", lambda i, n, R, P, X, C: ((_s5_mid(*R, *C),), ()), L, TM, [ysc, u], [S['s5_d']],
                     out_rows=[row(S5_W)])
    z2 = _mm(yg, W['s5_w_glu'], 'nn', "mm_glu")
    (o_b,) = _rowcall("s5_glu", lambda i, n, R, P, X, C: ((_s5_glu(*R, *C),), ()), L, TM, [yg, z2], [S['s5_b_glu']],
                      out_rows=[row(S5_W)])
    o_s = _mm(o_b, W['w_branch_s5'], 'nn', "mm_bs")

    (merged,) = _rowcall("merge", lambda i, n, R, P, X, C: ((_merge(*R, *C),), ()), L, TM, [gp, o_r, o_s],
                         [S['b_gate']], out_rows=[row(D_MODEL)])
    mixed = _mm(merged, W['w_out'], 'nn', "mm_out")

    def resid_fn(i, n, R, P, X, C):
        x1_ = R[0] + _rms(R[1], C[0])
        return (x1_, _rms(x1_, C[1])), ()

    x1, h2 = _rowcall("resid_norm", resid_fn, L, TM, [x, mixed], [g2n, g3], out_rows=[row(D_MODEL), row(D_MODEL, bf16)])

    z = _mm(h2, W['ffn_w_up'], 'nn', "mm_up")

    def conv(zt, zprev, i, cw_, cb_):
        z2s, z1s = _shift_down(zt, zprev, i, 2), _shift_down(zt, zprev, i, 1)
        return cb_ + cw_[0:1] * z2s + cw_[1:2] * z1s + cw_[2:3] * zt, z2s, z1s

    (act,) = _rowcall("conv_act", lambda i, n, R, P, X, C: ((_act(conv(R[0], P[0], i, C[0], C[1])[0]),), ()), L, TMW,
                      [z], [cw, cb], out_rows=[row(D_FF)], prev=[0])
    f = _mm(act, W['ffn_w_down'], 'nn', "mm_down")

    def final_fn(i, n, R, P, X, C):
        x1_, f_, t_ = R
        fn_, vjp = jax.vjp(_rms, f_, C[0])
        diff = x1_ + fn_ - t_
        loss = jnp.sum(diff * diff) * (0.5 / D_MODEL)
        dx2_ = diff * (1.0 / D_MODEL)
        df_, dg4_ = vjp(dx2_)
        return (df_, dx2_), (jnp.full((1, PACK_W), loss, f32), dg4_)

    df, dx2, loss, dg4 = _rowcall("loss_head", final_fn, L, TM, [x1, f, tgt], [g4],
                                  out_rows=[row(D_MODEL)] * 2, out_accs=[(1, PACK_W), (1, D_MODEL)])
    G = {'norm_ffn_post': dg4}

    dact = _mm(df, W['ffn_w_down'], 'nt', "mm_down_dx")
    G['ffn_w_down'] = _mm(act, df, 'tn', "mm_down_dw")

    def conv_bwd_fn(i, n, R, P, X, C):
        zc, z2s, z1s = conv(R[0], P[0], i, C[0], C[1])
        _, vjp = jax.vjp(_act, zc)
        (dzc_,) = vjp(R[1])
        return (dzc_,), (_sum0(dzc_), _sum0(dzc_ * z2s), _sum0(dzc_ * z1s), _sum0(dzc_ * R[0]))

    wide = (1, 2 * D_FF)
    dzc, dcb, dcw0, dcw1, dcw2 = _rowcall("conv_act_bwd", conv_bwd_fn, L, TMW, [z, dact], [cw, cb],
                                          out_rows=[row(2 * D_FF)], out_accs=[wide] * 4, prev=[0])
    G['ffn_conv_b'] = dcb
    G['ffn_conv_w'] = jnp.concatenate([dcw0, dcw1, dcw2], axis=0)

    def conv_shift_fn(i, n, R, P, X, C):
        d = R[0]
        return (C[0][2:3] * d + C[0][1:2] * _shift_up(d, X[0], i, n, 1) + C[0][0:1] * _shift_up(d, X[0], i, n, 2),), ()

    (dz,) = _rowcall("conv_shift_bwd", conv_shift_fn, L, TMW, [dzc], [cw], out_rows=[row(2 * D_FF)], nxt=[0])
    dh2 = _mm(dz, W['ffn_w_up'], 'nt', "mm_up_dx")
    G['ffn_w_up'] = _mm(h2, dz, 'tn', "mm_up_dw")

    def norm2_bwd_fn(i, n, R, P, X, C):
        x1_, mixed_, dx2_, dh2_ = R
        _, vjp3 = jax.vjp(_rms, x1_, C[1])
        dx1a, dg3_ = vjp3(dh2_)
        dx1_ = dx2_ + dx1a
        _, vjp2 = jax.vjp(_rms, mixed_, C[0])
        dmixed_, dg2_ = vjp2(dx1_)
        return (dx1_, dmixed_), (dg2_, dg3_)

    dx1, dmixed, dg2n, dg3 = _rowcall("norm_mid_bwd", norm2_bwd_fn, L, TM, [x1, mixed, dx2, dh2], [g2n, g3],
                                      out_rows=[row(D_MODEL)] * 2, out_accs=[(1, D_MODEL)] * 2)
    G['norm_mix_post'], G['norm_ffn_pre'] = dg2n, dg3

    dmerged = _mm(dmixed, W['w_out'], 'nt', "mm_out_dx")
    G['w_out'] = _mm(merged, dmixed, 'tn', "mm_out_dw")

    def merge_bwd_fn(i, n, R, P, X, C):
        _, vjp = jax.vjp(_merge, R[0], R[1], R[2], C[0])
        dgp_, do_r_, do_s_, dbg_ = vjp(R[3])
        return (dgp_, do_r_, do_s_), (dbg_,)

    dgp, do_r, do_s, G['b_gate'] = _rowcall("merge_bwd", merge_bwd_fn, L, TM, [gp, o_r, o_s, dmerged], [S['b_gate']],
                                            out_rows=[row(2 * D_MODEL), row(D_MODEL), row(D_MODEL)],
                                            out_accs=[(1, 2 * D_MODEL)])
    do_a = _mm(do_r, W['w_branch_rwkv'], 'nt', "mm_br_dx")
    G['w_branch_rwkv'] = _mm(o_a, do_r, 'tn', "mm_br_dw")
    do_b = _mm(do_s, W['w_branch_s5'], 'nt', "mm_bs_dx")
    G['w_branch_s5'] = _mm(o_b, do_s, 'tn', "mm_bs_dw")

    def glu_bwd_fn(i, n, R, P, X, C):
        _, vjp = jax.vjp(_s5_glu, R[0], R[1], C[0])
        dyg1_, dz2_, dbg_ = vjp(R[2])
        return (dyg1_, dz2_), (dbg_,)

    dyg1, dz2, G['s5_b_glu'] = _rowcall("s5_glu_bwd", glu_bwd_fn, L, TM, [yg, z2, do_b], [S['s5_b_glu']],
                                        out_rows=[row(S5_W)] * 2, out_accs=[(1, S5_W)])
    dyg2 = _mm(dz2, W['s5_w_glu'], 'nt', "mm_glu_dx")
    G['s5_w_glu'] = _mm(yg, dz2, 'tn', "mm_glu_dw")

    def mid_bwd_fn(i, n, R, P, X, C):
        _, vjp = jax.vjp(_s5_mid, R[0], R[1], C[0])
        dysc_, du_, dd_ = vjp(R[2] + R[3])
        return (dysc_, du_), (dd_,)

    dysc, du1, G['s5_d'] = _rowcall("s5_mid_bwd", mid_bwd_fn, L, TM, [ysc, u, dyg1, dyg2], [S['s5_d']],
                                    out_rows=[row(S5_W)] * 2, out_accs=[(1, S5_W)])
    dst = _mm(dysc, cmat, 'nt', "mm_cs_dx")
    dcmat = _mm(st, dysc, 'tn', "mm_cs_dw")
    lam = _s5_scan(dst, abar, True, "s5_scan_bwd")

    def s5_da_fn(i, n, R, P, X, C):
        lr, li = R[0][:, :S5_N], R[0][:, S5_N:]
        sp = _shift_down(R[1], P[0], i, 1)
        sr, si = sp[:, :S5_N], sp[:, S5_N:]
        return (), (jnp.concatenate([_sum0(lr * sr + li * si), _sum0(li * sr - lr * si)], axis=1),)

    (dabar,) = _rowcall("s5_da", s5_da_fn, L, TM, [lam, st], out_accs=[(1, 2 * S5_N)], prev=[1])
    du2 = _mm(lam, bmat, 'nt', "mm_bu_dx")
    dbmat = _mm(u, lam, 'tn', "mm_bu_dw")
    da_re, da_im, dls, db_re, db_im = _s5_disc_bwd(
        a_re, a_im, ls, b_re, b_im, dabar[:, :S5_N].reshape(S5_N, 1), dabar[:, S5_N:].reshape(S5_N, 1),
        undiag_in(dbmat[:, :S5_N]), undiag_in(dbmat[:, S5_N:]), seg)
    G['s5_a_re'], G['s5_a_im'], G['s5_log_step'] = da_re, da_im, dls
    G['s5_b_re'], G['s5_b_im'] = db_re, db_im
    G['s5_c_re'], G['s5_c_im'] = undiag_out(dcmat[:S5_N]), -undiag_out(dcmat[S5_N:])

    def out_bwd_fn(i, n, R, P, X, C):
        _, vjp = jax.vjp(_rwkv_out, *R[:5], *C)
        gs = vjp(R[5])
        return gs[:5], gs[5:8]

    dy, dr1, dk1, dv1, dg, dlw, dlb, drk = _rowcall("rwkv_out_bwd", out_bwd_fn, L, TM, [y, r, k2, v, g, do_a], out_consts,
                                                    out_rows=[row(RWKV_W)] * 5, out_accs=[(1, RWKV_W)] * 3)
    G['rwkv_lnx_w'], G['rwkv_lnx_b'], G['rwkv_r_k'] = dlw, dlb, drk
    dr2, dlwk, dk2b, dv2, dan, dbv = _wkv7_bwd(r, lw, k2, v, an, bv, ck, dy)

    def prep_bwd_fn(i, n, R, P, X, C):
        p_ = R[0]
        d1 = _shift_down(p_, P[0], i, 1) - p_
        q = p_ + d1 * C[0]
        _, vjp = jax.vjp(_prep, q, *C[1:])
        cots = (R[1] + R[2], R[3], R[4] + R[5], R[6] + R[7], R[8], R[9], R[10])
        gs = vjp(cots)
        return (gs[0],), (_sum0(gs[0] * d1),) + tuple(gs[1:8])

    small, lowr = (1, RWKV_W), (128, RWKV_W)
    dq, dmu, dw0, da0, dkk, dka, dw2p, da2p, dg2 = _rowcall(
        "rwkv_prep_bwd", prep_bwd_fn, L, TM, [p, dr1, dr2, dlwk, dk1, dk2b, dv1, dv2, dan, dbv, dg],
        prep_consts, out_rows=[row(N_RWKV)], out_accs=[(1, N_RWKV)] + [small] * 4 + [lowr] * 3, prev=[0])
    G['rwkv_shift_mu'], G['rwkv_w0'], G['rwkv_a0'], G['rwkv_k_k'], G['rwkv_k_a'] = dmu, dw0, da0, dkk, dka
    G['rwkv_w2'], G['rwkv_a2'], G['rwkv_g2'] = dw2p[:64], da2p[64:], dg2

    def shift_bwd_fn(i, n, R, P, X, C):
        dm = R[0] * C[0]
        return (R[0] - dm + _shift_up(dm, X[0] * C[0], i, n, 1),), ()

    (dp,) = _rowcall("shift_bwd", shift_bwd_fn, L, TM, [dq], [S['rwkv_shift_mu']], out_rows=[row(N_RWKV)], nxt=[0])

    (du,) = _rowcall("add_du", lambda i, n, R, P, X, C: ((R[0] + R[1],), ()), L, TM, [du1, du2], out_rows=[row(S5_W)])
    dproj = jnp.concatenate([dp, du, dgp], axis=1)
    dh1 = _mm(dproj, w_in, 'nt', "mm_in_dx")
    G['w_in'] = _mm(h1, dproj, 'tn', "mm_in_dw")

    def norm1_bwd_fn(i, n, R, P, X, C):
        _, vjp = jax.vjp(_rms, R[0], C[0])
        dxa, dg1_ = vjp(R[2])
        return (R[1] + dxa,), (dg1_,)

    dx, G['norm_mix_pre'] = _rowcall("norm_pre_bwd", norm1_bwd_fn, L, TM, [x, dx1, dh1], [g1],
                                     out_rows=[row(D_MODEL)], out_accs=[(1, D_MODEL)])
    return loss, dx, G


def kernel(x, norm_mix_pre, norm_mix_post, norm_ffn_pre, norm_ffn_post, w_in, b_gate, rwkv_shift_mu, rwkv_w0, rwkv_w2, rwkv_a0, rwkv_a2, rwkv_g2, rwkv_k_k, rwkv_k_a, rwkv_r_k, rwkv_lnx_w, rwkv_lnx_b, s5_a_re, s5_a_im, s5_b_re, s5_b_im, s5_c_re, s5_c_im, s5_d, s5_log_step, s5_w_glu, s5_b_glu, w_branch_rwkv, w_branch_s5, w_out, ffn_w_up, ffn_conv_w, ffn_conv_b, ffn_w_down, loss_target, m_norm_mix_pre, m_norm_mix_post, m_norm_ffn_pre, m_norm_ffn_post, m_w_in, m_b_gate, m_rwkv_shift_mu, m_rwkv_w0, m_rwkv_w2, m_rwkv_a0, m_rwkv_a2, m_rwkv_g2, m_rwkv_k_k, m_rwkv_k_a, m_rwkv_r_k, m_rwkv_lnx_w, m_rwkv_lnx_b, m_s5_a_re, m_s5_a_im, m_s5_b_re, m_s5_b_im, m_s5_c_re, m_s5_c_im, m_s5_d, m_s5_log_step, m_s5_w_glu, m_s5_b_glu, m_w_branch_rwkv, m_w_branch_s5, m_w_out, m_ffn_w_up, m_ffn_conv_w, m_ffn_conv_b, m_ffn_w_down, v_norm_mix_pre, v_norm_mix_post, v_norm_ffn_pre, v_norm_ffn_post, v_w_in, v_b_gate, v_rwkv_shift_mu, v_rwkv_w0, v_rwkv_w2, v_rwkv_a0, v_rwkv_a2, v_rwkv_g2, v_rwkv_k_k, v_rwkv_k_a, v_rwkv_r_k, v_rwkv_lnx_w, v_rwkv_lnx_b, v_s5_a_re, v_s5_a_im, v_s5_b_re, v_s5_b_im, v_s5_c_re, v_s5_c_im, v_s5_d, v_s5_log_step, v_s5_w_glu, v_s5_b_glu, v_w_branch_rwkv, v_w_branch_s5, v_w_out, v_ffn_w_up, v_ffn_conv_w, v_ffn_conv_b, v_ffn_w_down):
    A = dict(locals())
    L = x.shape[1]
    big_shapes = [A[n].shape for n in BIG]
    small_shapes = [A[n].shape for n in SMALL]
    small_rows = sum(_ceil_to(A[n].size, PACK_W) // PACK_W for n in SMALL)
    R = _ceil_to(BIG_ROWS + small_rows + 1, 32)

    wpack = _pack([A[n] for n in BIG], [A[n] for n in SMALL], R)
    wrows = _ceil_to(BIG_ROWS, 16)
    frows = _ceil_to(sum(_ceil_to(_shard_elems(n), PACK_W) // PACK_W for n in BIG_F32), 8)
    gathered = _chip_exchange(wpack[:wrows].astype(bf16), True, "gather_weights").reshape(4, -1)
    gathered32 = _chip_exchange(wpack[:frows], True, "gather_weights_f32").reshape(4, -1)
    W, off = {}, 0
    for n in BIG:
        ne = _shard_elems(n)
        W[n] = _join_shards(n, (gathered32 if n in BIG_F32 else gathered)[:, off:off + ne])
        off += _ceil_to(ne, PACK_W)
    S = {n: A[n].reshape(1, -1) for n in SMALL}
    S['ffn_conv_w_full'] = W['ffn_conv_w']

    loss, dx, G = _forward_backward(x[0], loss_target[0], W, S)

    big_parts = jnp.concatenate(
        [jnp.pad(s, ((0, 0), (0, _ceil_to(s.shape[1], PACK_W) - s.shape[1])))
         for s in (_split_shards(n, G[n]) for n in BIG)], axis=1)
    small_flat = jnp.concatenate([_flat_pad(G[n]) for n in SMALL] + [loss.reshape(-1)])
    small_flat = jnp.pad(small_flat, (0, (R - BIG_ROWS) * PACK_W - small_flat.shape[0]))
    gpack = jnp.concatenate([big_parts, jnp.broadcast_to(small_flat, (4, small_flat.shape[0]))], axis=1)
    gpack = gpack.reshape(4, R, PACK_W)
    tm_half = R // 2 // 5
    from_sibling = _send_other_half(gpack, "grads_to_sibling")
    chip_sum = _add_my_half(gpack, from_sibling, tm_half, "grads_pair_sum")
    from_chips = _chip_exchange(chip_sum, False, "grads_chip_exchange")
    my_half = _sum_slots(from_chips, tm_half, "grads_chip_sum")
    gsum = _pair_place(my_half, "grads_to_sibling_back").reshape(R, PACK_W)

    mpack = _pack([A['m_' + n] for n in BIG], [A['m_' + n] for n in SMALL], R)
    vpack = _pack([A['v_' + n] for n in BIG], [A['v_' + n] for n in SMALL], R)
    delta, new_m, new_v = _adamw(wpack, gsum, mpack, vpack, R // 10)

    def named(buf):
        vals = _unpack(buf, big_shapes, small_shapes)
        d = dict(zip(BIG + SMALL, vals))
        return [d[n] for n in WEIGHTS]

    loss_out = gsum[BIG_ROWS + small_rows, 0]
    return (loss_out, dx[None], *named(gsum), *named(delta), *named(new_m), *named(new_v))
```

```python
import functools

import jax
import jax.numpy as jnp
from jax import lax
from jax.experimental import pallas as pl
from jax.experimental.pallas import tpu as pltpu

f32, bf16 = jnp.float32, jnp.bfloat16
MESH = pl.DeviceIdType.MESH

D_MODEL = 1024
RWKV_W = 512
HEADS, HEAD = 8, 64
N_RWKV = 1792
S5_W = 512
S5_G, S5_P, S5_C = 32, 64, 16
S5_N = S5_G * S5_P
D_FF = 2816
NORM_EPS = 1e-6
LNX_EPS = 64e-5
ADAM_LR, ADAM_B1, ADAM_B2, ADAM_EPS, ADAM_WD, ADAM_STEP = 0.001, 0.9, 0.999, 1e-08, 0.01, 10

VMEM_LIMIT = 48 * 1024 * 1024
PACK_W = 1024
WKV_C = 64
S5_T = 256

WEIGHTS = ['norm_mix_pre', 'norm_mix_post', 'norm_ffn_pre', 'norm_ffn_post', 'w_in', 'b_gate', 'rwkv_shift_mu',
           'rwkv_w0', 'rwkv_w2', 'rwkv_a0', 'rwkv_a2', 'rwkv_g2', 'rwkv_k_k', 'rwkv_k_a', 'rwkv_r_k', 'rwkv_lnx_w',
           'rwkv_lnx_b', 's5_a_re', 's5_a_im', 's5_b_re', 's5_b_im', 's5_c_re', 's5_c_im', 's5_d', 's5_log_step',
           's5_w_glu', 's5_b_glu', 'w_branch_rwkv', 'w_branch_s5', 'w_out', 'ffn_w_up', 'ffn_conv_w', 'ffn_conv_b',
           'ffn_w_down']


def _ceil_to(n, m):
    return -(-n // m) * m


def _mesh_pos():
    return lax.axis_index("x"), lax.axis_index("y"), lax.axis_index("c")


def _pick(d, cap=4096):
    for c in (1024, 1408, 2176, 896, 512, 256, 128):
        if c <= cap and d % c == 0:
            return c
    raise ValueError(d)


def _mm(a, b, mode, name, out_dtype=f32):
    if mode == 'tn':
        (K, M), (K2, N) = a.shape, b.shape
    elif mode == 'nt':
        (M, K), (N, K2) = a.shape, b.shape
    else:
        (M, K), (K2, N) = a.shape, b.shape
    assert K == K2, (name, a.shape, b.shape)
    tm = _pick(M, 512)
    tn = _pick(N)
    tk = _pick(K, 512) if mode == 'tn' else _pick(K)
    nk = K // tk
    dims = {'nn': ((1,), (0,)), 'nt': ((1,), (1,)), 'tn': ((0,), (0,))}[mode]

    def body(a_ref, b_ref, o_ref, acc_ref):
        k = pl.program_id(2)

        @pl.when(k == 0)
        def _():
            acc_ref[...] = jnp.zeros_like(acc_ref)

        acc_ref[...] += lax.dot_general(a_ref[...].astype(bf16), b_ref[...].astype(bf16), (dims, ((), ())),
                                        preferred_element_type=f32)

        @pl.when(k == nk - 1)
        def _():
            o_ref[...] = acc_ref[...].astype(o_ref.dtype)

    a_spec = pl.BlockSpec((tk, tm), lambda i, j, k: (k, i)) if mode == 'tn' else pl.BlockSpec((tm, tk), lambda i, j, k: (i, k))
    b_spec = pl.BlockSpec((tn, tk), lambda i, j, k: (j, k)) if mode == 'nt' else pl.BlockSpec((tk, tn), lambda i, j, k: (k, j))
    return pl.pallas_call(
        body, name=name, grid=(M // tm, N // tn, nk),
        in_specs=[a_spec, b_spec], out_specs=pl.BlockSpec((tm, tn), lambda i, j, k: (i, j)),
        out_shape=jax.ShapeDtypeStruct((M, N), out_dtype),
        scratch_shapes=[pltpu.VMEM((tm, tn), f32)],
        compiler_params=pltpu.CompilerParams(dimension_semantics=("parallel", "parallel", "arbitrary"),
                                             vmem_limit_bytes=VMEM_LIMIT),
    )(a, b)


def _rowcall(name, fn, L, tm, rows, consts=(), out_rows=(), out_accs=(), prev=(), nxt=()):
    nsteps = L // tm
    nb8 = tm // 8
    last8 = L // 8 - 1
    n_r, n_p, n_x, n_c, n_or = len(rows), len(prev), len(nxt), len(consts), len(out_rows)

    def body(*refs):
        i = pl.program_id(0)
        vals = [r[...] for r in refs[:n_r + n_p + n_x + n_c]]
        R, P = vals[:n_r], vals[n_r:n_r + n_p]
        X, C = vals[n_r + n_p:n_r + n_p + n_x], vals[n_r + n_p + n_x:]
        o_refs = refs[n_r + n_p + n_x + n_c:]
        outs_r, outs_a = fn(i, nsteps, R, P, X, C)
        for ref, v in zip(o_refs[:n_or], outs_r, strict=True):
            ref[...] = v.astype(ref.dtype)
        if out_accs:
            @pl.when(i == 0)
            def _():
                for ref in o_refs[n_or:]:
                    ref[...] = jnp.zeros_like(ref)

            for ref, v in zip(o_refs[n_or:], outs_a, strict=True):
                ref[...] += v

    def const_spec(c):
        nd = c.ndim
        return pl.BlockSpec(c.shape, lambda i: (0,) * nd)

    in_specs = ([pl.BlockSpec((tm, a.shape[1]), lambda i: (i, 0)) for a in rows]
                + [pl.BlockSpec((8, rows[j].shape[1]), lambda i: (jnp.maximum(i * nb8 - 1, 0), 0)) for j in prev]
                + [pl.BlockSpec((8, rows[j].shape[1]), lambda i: (jnp.minimum((i + 1) * nb8, last8), 0)) for j in nxt]
                + [const_spec(c) for c in consts])
    out_specs = ([pl.BlockSpec((tm, c), lambda i: (i, 0)) for c, _ in out_rows]
                 + [pl.BlockSpec(s, lambda i: (0, 0)) for s in out_accs])
    out_shape = ([jax.ShapeDtypeStruct((L, c), dt) for c, dt in out_rows]
                 + [jax.ShapeDtypeStruct(s, f32) for s in out_accs])
    args = list(rows) + [rows[j] for j in prev] + [rows[j] for j in nxt] + list(consts)
    return pl.pallas_call(
        body, name=name, grid=(nsteps,), in_specs=in_specs, out_specs=out_specs, out_shape=out_shape,
        compiler_params=pltpu.CompilerParams(dimension_semantics=("arbitrary",), vmem_limit_bytes=VMEM_LIMIT),
    )(*args)


def _shift_down(x, prev8, i, k):
    rolled = pltpu.roll(x, k, axis=0)
    pfix = jnp.where(i > 0, pltpu.roll(prev8, k, axis=0), 0.0)
    row8 = lax.broadcasted_iota(jnp.int32, pfix.shape, 0)
    top = jnp.where(row8 < k, pfix, rolled[:8])
    return jnp.concatenate([top, rolled[8:]], axis=0)


def _shift_up(x, next8, i, nsteps, k):
    tm = x.shape[0]
    rolled = pltpu.roll(x, tm - k, axis=0)
    nfix = jnp.where(i < nsteps - 1, pltpu.roll(next8, 8 - k, axis=0), 0.0)
    row8 = lax.broadcasted_iota(jnp.int32, nfix.shape, 0)
    bot = jnp.where(row8 >= 8 - k, nfix, rolled[tm - 8:])
    return jnp.concatenate([rolled[:tm - 8], bot], axis=0)


def _sum0(x):
    return jnp.sum(x, axis=0, keepdims=True)


def _rms(x, g):
    return x * lax.rsqrt(jnp.mean(x * x, axis=-1, keepdims=True) + NORM_EPS) * g


def _softplus(x):
    return jnp.maximum(x, 0.0) + jnp.log(1.0 + jnp.exp(-jnp.abs(x)))


def _gelu(x):
    return 0.5 * x * (1.0 + jnp.tanh(0.7978845608028654 * (x + 0.044715 * x * x * x)))


def _dot32(a, b):
    return jnp.dot(a, b, preferred_element_type=f32, precision=lax.Precision.HIGHEST)


def _prep(q, w0, a0, k_k, k_a, w2p, a2p, g2, E):
    r, k, v = q[:, 0:512], q[:, 512:1024], q[:, 1024:1536]
    wa, gd = q[:, 1536:1664], q[:, 1664:1792]
    wlog = -_softplus(-(w0 + _dot32(jnp.tanh(wa), w2p))) - 0.5
    lw = -jnp.exp(wlog)
    a = jax.nn.sigmoid(a0 + _dot32(wa, a2p))
    g = _dot32(jax.nn.sigmoid(gd), g2)
    kk = k * k_k
    kkn = kk / jnp.maximum(jnp.sqrt(_dot32(kk * kk, E)), 1e-12)
    k2 = k * (1.0 + (a - 1.0) * k_a)
    return r, lw, k2, v, -kkn, kkn * a, g


def _rwkv_out(y, r, k2, v, g, lnx_w, lnx_b, r_k, E):
    mean = _dot32(y, E) * (1.0 / HEAD)
    yc = y - mean
    var = _dot32(yc * yc, E) * (1.0 / HEAD)
    yn = yc * lax.rsqrt(var + LNX_EPS) * lnx_w + lnx_b
    bonus = _dot32(r * k2 * r_k, E) * v
    return (yn + bonus) * g


def _s5_mid(ysc, u, d):
    return _gelu(ysc + d * u)


def _s5_glu(yg, z2, b_glu):
    return yg * jax.nn.sigmoid(z2 + b_glu)


def _merge(gp, o_r, o_s, b_gate):
    gates = jax.nn.sigmoid(gp + b_gate)
    return gates[:, :D_MODEL] * o_r + gates[:, D_MODEL:] * o_s


def _act(zc):
    return _gelu(zc[:, :D_FF]) * zc[:, D_FF:]


def _s5_disc(a_re, a_im, ls, b_re, b_im):
    dt = jnp.exp(ls)
    er = jnp.exp(a_re * dt)
    ar, ai = er * jnp.cos(a_im * dt), er * jnp.sin(a_im * dt)
    x, y = ar - 1.0, ai
    den = a_re * a_re + a_im * a_im
    fr, fi = (x * a_re + y * a_im) / den, (y * a_re - x * a_im) / den
    return ar, ai, fr * b_re - fi * b_im, fr * b_im + fi * b_re


_DIMS = {'nn': ((1,), (0,)), 'nt': ((1,), (1,)), 'tn': ((0,), (0,))}


def _raw_bdot(a, b, mode):
    return lax.dot_general(a.astype(bf16), b.astype(bf16), (_DIMS[mode], ((), ())), preferred_element_type=f32)


@functools.partial(jax.custom_vjp, nondiff_argnums=(2,))
def _bdot(a, b, mode):
    return _raw_bdot(a, b, mode)


def _bdot_fwd(a, b, mode):
    return _raw_bdot(a, b, mode), (a, b)


def _bdot_bwd(mode, res, g):
    a, b = res
    if mode == 'nn':
        return _raw_bdot(g, b, 'nt'), _raw_bdot(a, g, 'tn')
    if mode == 'nt':
        return _raw_bdot(g, b, 'nn'), _raw_bdot(g, a, 'tn')
    return _raw_bdot(b, g, 'nt'), _raw_bdot(a, g, 'nn')


_bdot.defvjp(_bdot_fwd, _bdot_bwd)


def _wkv_chunk(S0, r, lw, k, v, a, b, tri, bd):
    C = r[0].shape[0]
    P = range(len(r))
    lane = lax.broadcasted_iota(jnp.int32, (1, 2 * HEAD), 1)
    halves = [(lane < HEAD).astype(f32), (lane >= HEAD).astype(f32)]
    eye = (lax.broadcasted_iota(jnp.int32, (C, C), 0) == lax.broadcasted_iota(jnp.int32, (C, C), 1)).astype(f32)
    sl = tri - eye
    cum = [_dot32(tri, lw[p]) for p in P]
    g = [jnp.exp(cum[p]) for p in P]
    gi = [jnp.exp(-cum[p]) for p in P]
    at = [a[p] * jnp.exp(cum[p] - lw[p]) for p in P]
    rt = [r[p] * g[p] for p in P]
    kb = [k[p] * gi[p] for p in P]
    bb = [b[p] * gi[p] for p in P]
    PE = [(p, e) for p in P for e in range(2)]
    atm = {pe: at[pe[0]] * halves[pe[1]] for pe in PE}
    rtm = {pe: rt[pe[0]] * halves[pe[1]] for pe in PE}
    aab = {pe: _bdot(atm[pe], bb[pe[0]], 'nt') * sl for pe in PE}
    aak = {pe: _bdot(atm[pe], kb[pe[0]], 'nt') * sl for pe in PE}
    rk = {pe: _bdot(rtm[pe], kb[pe[0]], 'nt') * tri for pe in PE}
    rb = {pe: _bdot(rtm[pe], bb[pe[0]], 'nt') * tri for pe in PE}
    rhs = [_bdot(at[p], S0[p], 'nt') + sum(halves[e] * _bdot(aak[(p, e)], v[p], 'nn') for e in range(2)) for p in P]
    y0 = [_bdot(rt[p], S0[p], 'nt') + sum(halves[e] * _bdot(rk[(p, e)], v[p], 'nn') for e in range(2)) for p in P]
    x = {pe: eye + aab[pe] for pe in PE}
    pw = aab
    n = 1
    while 2 * n < C:
        pw = {pe: _bdot(pw[pe], pw[pe], 'nn') for pe in PE}
        x = {pe: x[pe] + _bdot(x[pe], pw[pe], 'nn') for pe in PE}
        n *= 2
    u = [sum(halves[e] * _bdot(x[(p, e)], rhs[p], 'nn') for e in range(2)) for p in P]
    y = [y0[p] + sum(halves[e] * _bdot(rb[(p, e)], u[p], 'nn') for e in range(2)) for p in P]
    S1 = [g[p][C - 1:C, :] * (S0[p] + bd * (_bdot(v[p], kb[p], 'tn') + _bdot(u[p], bb[p], 'tn'))) for p in P]
    return y, S1


def _pairs(x):
    return [x[:, 2 * HEAD * p:2 * HEAD * (p + 1)] for p in range(HEADS // 2)]


def _wkv_consts():
    tri = jnp.tril(jnp.ones((WKV_C, WKV_C), f32))
    hid = jnp.arange(2 * HEAD) // HEAD
    return tri, (hid[:, None] == hid[None, :]).astype(f32)


def _wkv7_fwd(r, lw, k, v, a, b):
    L = r.shape[0]
    nc, npair = L // WKV_C, HEADS // 2

    def body(r_ref, lw_ref, k_ref, v_ref, a_ref, b_ref, tri_ref, bd_ref, y_ref, ck_ref, s_ref):
        @pl.when(pl.program_id(0) == 0)
        def _():
            s_ref[...] = jnp.zeros_like(s_ref)

        s0 = [s_ref[p] for p in range(npair)]
        for p in range(npair):
            ck_ref[0, p] = s0[p]
        y, s1 = _wkv_chunk(s0, *(_pairs(x) for x in (r_ref, lw_ref, k_ref, v_ref, a_ref, b_ref)), tri_ref[...], bd_ref[...])
        for p in range(npair):
            y_ref[:, 2 * HEAD * p:2 * HEAD * (p + 1)] = y[p]
            s_ref[p] = s1[p]

    row = pl.BlockSpec((WKV_C, RWKV_W), lambda c: (c, 0))
    sspec = pl.BlockSpec((1, npair, 2 * HEAD, 2 * HEAD), lambda c: (c, 0, 0, 0))
    return pl.pallas_call(
        body, name="wkv7_fwd", grid=(nc,),
        in_specs=[row] * 6 + [pl.BlockSpec((WKV_C, WKV_C), lambda c: (0, 0)), pl.BlockSpec((2 * HEAD, 2 * HEAD), lambda c: (0, 0))],
        out_specs=[row, sspec],
        out_shape=[jax.ShapeDtypeStruct((L, RWKV_W), f32), jax.ShapeDtypeStruct((nc, npair, 2 * HEAD, 2 * HEAD), f32)],
        scratch_shapes=[pltpu.VMEM((npair, 2 * HEAD, 2 * HEAD), f32)],
        compiler_params=pltpu.CompilerParams(dimension_semantics=("arbitrary",), vmem_limit_bytes=VMEM_LIMIT),
    )(r, lw, k, v, a, b, *_wkv_consts())


def _wkv7_bwd(r, lw, k, v, a, b, ck, dy):
    L = r.shape[0]
    nc, npair = L // WKV_C, HEADS // 2

    def body(r_ref, lw_ref, k_ref, v_ref, a_ref, b_ref, ck_ref, dy_ref, tri_ref, bd_ref,
             dr_ref, dlw_ref, dk_ref, dv_ref, da_ref, db_ref, ds_ref):
        @pl.when(pl.program_id(0) == 0)
        def _():
            ds_ref[...] = jnp.zeros_like(ds_ref)

        tri, bd = tri_ref[...], bd_ref[...]
        ins = [[ck_ref[0, p] for p in range(npair)]] + [_pairs(x) for x in (r_ref, lw_ref, k_ref, v_ref, a_ref, b_ref)]
        _, vjp = jax.vjp(lambda *t: _wkv_chunk(*t, tri, bd), *ins)
        gs = vjp((_pairs(dy_ref), [ds_ref[p] for p in range(npair)]))
        for p in range(npair):
            ds_ref[p] = gs[0][p]
            for ref, gval in zip((dr_ref, dlw_ref, dk_ref, dv_ref, da_ref, db_ref), gs[1:]):
                ref[:, 2 * HEAD * p:2 * HEAD * (p + 1)] = gval[p]

    row = pl.BlockSpec((WKV_C, RWKV_W), lambda c: (nc - 1 - c, 0))
    sspec = pl.BlockSpec((1, npair, 2 * HEAD, 2 * HEAD), lambda c: (nc - 1 - c, 0, 0, 0))
    return pl.pallas_call(
        body, name="wkv7_bwd", grid=(nc,),
        in_specs=[row] * 6 + [sspec, row, pl.BlockSpec((WKV_C, WKV_C), lambda c: (0, 0)),
                              pl.BlockSpec((2 * HEAD, 2 * HEAD), lambda c: (0, 0))],
        out_specs=[row] * 6,
        out_shape=[jax.ShapeDtypeStruct((L, RWKV_W), f32)] * 6,
        scratch_shapes=[pltpu.VMEM((npair, 2 * HEAD, 2 * HEAD), f32)],
        compiler_params=pltpu.CompilerParams(dimension_semantics=("arbitrary",), vmem_limit_bytes=VMEM_LIMIT),
    )(r, lw, k, v, a, b, ck, dy, *_wkv_consts())


def _cmul(ar, ai, xr, xi):
    return ar * xr - ai * xi, ar * xi + ai * xr


def _s5_scan(x, abar, reverse, name):
    L = x.shape[0]
    nt = L // S5_T
    ng = S5_T // 8

    def body(x_ref, a_ref, o_ref, car_ref, pw_ref):
        @pl.when(pl.program_id(0) == 0)
        def _():
            car_ref[...] = jnp.zeros_like(car_ref)
            ar = jnp.broadcast_to(a_ref[:, :S5_N], (8, S5_N))
            ai = jnp.broadcast_to(a_ref[:, S5_N:], (8, S5_N))
            if reverse:
                ai = -ai
            row = lax.broadcasted_iota(jnp.int32, (8, S5_N), 0)
            pr, pi = ar, ai
            qr, qi = jnp.zeros((8, S5_N), f32), jnp.zeros((8, S5_N), f32)
            for e in range(1, 9):
                sel = (row == 8 - e) if reverse else (row == e - 1)
                qr, qi = jnp.where(sel, pr, qr), jnp.where(sel, pi, qi)
                if e in (1, 2, 4):
                    j = (1, 2, 4).index(e)
                    pw_ref[j, :, :S5_N] = pr
                    pw_ref[j, :, S5_N:] = pi
                pr, pi = _cmul(pr, pi, ar, ai)
            pw_ref[3, :, :S5_N] = qr
            pw_ref[3, :, S5_N:] = qi

        row = lax.broadcasted_iota(jnp.int32, (8, S5_N), 0)

        def group(gi, carry):
            g = (ng - 1 - gi) if reverse else gi
            t0 = pl.multiple_of(g * 8, 8)
            xr, xi = x_ref[pl.ds(t0, 8), :S5_N], x_ref[pl.ds(t0, 8), S5_N:]
            for j, d in enumerate((1, 2, 4)):
                if reverse:
                    sr = jnp.where(row < 8 - d, pltpu.roll(xr, 8 - d, axis=0), 0.0)
                    si = jnp.where(row < 8 - d, pltpu.roll(xi, 8 - d, axis=0), 0.0)
                else:
                    sr = jnp.where(row >= d, pltpu.roll(xr, d, axis=0), 0.0)
                    si = jnp.where(row >= d, pltpu.roll(xi, d, axis=0), 0.0)
                mr, mi = _cmul(pw_ref[j, :, :S5_N], pw_ref[j, :, S5_N:], sr, si)
                xr, xi = xr + mr, xi + mi
            cr, ci = carry
            mr, mi = _cmul(pw_ref[3, :, :S5_N], pw_ref[3, :, S5_N:], cr, ci)
            xr, xi = xr + mr, xi + mi
            o_ref[pl.ds(t0, 8), :S5_N] = xr
            o_ref[pl.ds(t0, 8), S5_N:] = xi
            e = 0 if reverse else 7
            return (jnp.broadcast_to(xr[e:e + 1, :], (8, S5_N)), jnp.broadcast_to(xi[e:e + 1, :], (8, S5_N)))

        cr, ci = lax.fori_loop(0, ng, group, (car_ref[:, :S5_N], car_ref[:, S5_N:]))
        car_ref[:, :S5_N] = cr
        car_ref[:, S5_N:] = ci

    imap = (lambda i: (nt - 1 - i, 0)) if reverse else (lambda i: (i, 0))
    return pl.pallas_call(
        body, name=name, grid=(nt,),
        in_specs=[pl.BlockSpec((S5_T, 2 * S5_N), imap), pl.BlockSpec((1, 2 * S5_N), lambda i: (0, 0))],
        out_specs=pl.BlockSpec((S5_T, 2 * S5_N), imap),
        out_shape=jax.ShapeDtypeStruct((L, 2 * S5_N), f32),
        scratch_shapes=[pltpu.VMEM((8, 2 * S5_N), f32), pltpu.VMEM((4, 8, 2 * S5_N), f32)],
        compiler_params=pltpu.CompilerParams(dimension_semantics=("arbitrary",), vmem_limit_bytes=VMEM_LIMIT),
    )(x, abar)


def _s5_disc_fwd(a_re, a_im, ls, b_re, b_im):
    def body(a_re_ref, a_im_ref, ls_ref, b_re_ref, b_im_ref, ar_ref, ai_ref, br_ref, bi_ref):
        outs = _s5_disc(a_re_ref[...], a_im_ref[...], ls_ref[...], b_re_ref[...], b_im_ref[...])
        for ref, v in zip((ar_ref, ai_ref, br_ref, bi_ref), outs):
            ref[...] = v

    c1, c16 = jax.ShapeDtypeStruct((S5_N, 1), f32), jax.ShapeDtypeStruct((S5_N, S5_C), f32)
    return pl.pallas_call(body, name="s5_disc", out_shape=[c1, c1, c16, c16])(a_re, a_im, ls, b_re, b_im)


def _s5_disc_bwd(a_re, a_im, ls, b_re, b_im, d_ar, d_ai, d_br, d_bi, seg):
    def body(a_re_ref, a_im_ref, ls_ref, b_re_ref, b_im_ref, g1, g2, g3, g4, seg_ref, o1, o2, o3, o4, o5):
        _, vjp = jax.vjp(_s5_disc, a_re_ref[...], a_im_ref[...], ls_ref[...], b_re_ref[...], b_im_ref[...])
        da_re, da_im, dls, db_re, db_im = vjp((g1[...], g2[...], g3[...], g4[...]))
        o1[...] = da_re
        o2[...] = da_im
        o3[...] = _dot32(seg_ref[...], dls)
        o4[...] = db_re
        o5[...] = db_im

    c1, c16 = jax.ShapeDtypeStruct((S5_N, 1), f32), jax.ShapeDtypeStruct((S5_N, S5_C), f32)
    return pl.pallas_call(body, name="s5_disc_bwd", out_shape=[c1, c1, jax.ShapeDtypeStruct((S5_G, 1), f32), c16, c16])(
        a_re, a_im, ls, b_re, b_im, d_ar, d_ai, d_br, d_bi, seg)


ANY = pl.BlockSpec(memory_space=pl.ANY)

GATHER = {'w_in': ((4352, 1024), 0), 'ffn_w_up': ((1024, 5632), 1), 'w_branch_rwkv': ((512, 1024), 1),
          'w_branch_s5': ((512, 1024), 1), 'w_out': ((1024, 1024), 0), 's5_w_glu': ((512, 512), 0),
          'ffn_w_down': ((2816, 1024), 0), 'rwkv_w2': ((64, 512), 1), 'rwkv_a2': ((64, 512), 1),
          'rwkv_g2': ((128, 512), 1), 'ffn_conv_w': ((8, 5632), 1)}
BIG = ['w_in', 'ffn_w_up', 'w_branch_rwkv', 'w_branch_s5', 'w_out', 's5_w_glu', 'ffn_w_down']
TINY = ['rwkv_w2', 'rwkv_a2', 'rwkv_g2', 'ffn_conv_w']
SMALL = [n for n in WEIGHTS if n not in GATHER]
SMALL_ROWS = 320
ADAM_ROWS = 256


def _mo(v, m):
    return v if isinstance(v, int) else pl.multiple_of(v, m)


def _slab(ref, name, j, h=None):
    (R, Cn), axis = GATHER[name]
    if axis == 0:
        rs = R // 4
        if h is None:
            return ref.at[pl.ds(_mo(j * rs, 16), rs), :]
        return ref.at[pl.ds(_mo(j * rs + h * (rs // 2), 8), rs // 2), :]
    cols = pl.ds(_mo(j * (Cn // 4), 128), Cn // 4)
    if h is None:
        return ref.at[:, cols]
    return ref.at[pl.ds(_mo(h * (R // 2), 8), R // 2), cols]


def _half_shape(name):
    (R, Cn), axis = GATHER[name]
    return (R // 8, Cn) if axis == 0 else (R // 2, Cn // 4)


def _chip_peers(px, py):
    return [((1 - px) if (k >> 1) else px, (1 - py) if (k & 1) else py) for k in (1, 2, 3)]


def _run_copies(copies):
    for cp in copies:
        cp.start()
    for cp in copies:
        cp.wait()


def _gather_weights(blocks):
    names = list(blocks)
    n = len(names)

    def body(*refs):
        ins, outs = refs[:n], refs[n:2 * n]
        ssem, rsem, lsem = refs[2 * n:]
        px, py, pc = _mesh_pos()
        me = 2 * px + py
        copies = []
        for i, nm in enumerate(names):
            dst = _slab(outs[i], nm, me)
            copies.append(pltpu.make_async_copy(ins[i], dst, lsem.at[i]))
            for k, (qx, qy) in enumerate(_chip_peers(px, py)):
                copies.append(pltpu.make_async_remote_copy(ins[i], dst, ssem.at[3 * i + k], rsem.at[3 * i + k],
                                                           device_id=(qx, qy, pc), device_id_type=MESH))
        _run_copies(copies)

    outs = pl.pallas_call(
        body, name="gather_weights", in_specs=[ANY] * n, out_specs=[ANY] * n,
        out_shape=[jax.ShapeDtypeStruct(GATHER[nm][0], blocks[nm].dtype) for nm in names],
        scratch_shapes=[pltpu.SemaphoreType.DMA((3 * n,)), pltpu.SemaphoreType.DMA((3 * n,)), pltpu.SemaphoreType.DMA((n,))],
    )(*[blocks[nm] for nm in names])
    return dict(zip(names, outs))


def _grads_to_sibling(G, small):
    n = len(BIG)

    def body(*refs):
        g_refs, small_ref = refs[:n], refs[n]
        o_refs, small_o = refs[n + 1:2 * n + 1], refs[2 * n + 1]
        ssem, rsem = refs[2 * n + 2:]
        px, py, pc = _mesh_pos()
        sib = (px, py, 1 - pc)
        copies = []
        for i, nm in enumerate(BIG):
            for j in range(4):
                copies.append(pltpu.make_async_remote_copy(_slab(g_refs[i], nm, j, 1 - pc), o_refs[i].at[j],
                                                           ssem.at[4 * i + j], rsem.at[4 * i + j],
                                                           device_id=sib, device_id_type=MESH))
        copies.append(pltpu.make_async_remote_copy(small_ref, small_o, ssem.at[4 * n], rsem.at[4 * n],
                                                   device_id=sib, device_id_type=MESH))
        _run_copies(copies)

    outs = pl.pallas_call(
        body, name="grads_to_sibling", in_specs=[ANY] * (n + 1), out_specs=[ANY] * (n + 1),
        out_shape=[jax.ShapeDtypeStruct((4,) + _half_shape(nm), f32) for nm in BIG] + [jax.ShapeDtypeStruct(small.shape, f32)],
        scratch_shapes=[pltpu.SemaphoreType.DMA((4 * n + 1,)), pltpu.SemaphoreType.DMA((4 * n + 1,))],
    )(*[G[nm] for nm in BIG], small)
    return dict(zip(BIG, outs[:n])), outs[n]


def _pair_add(G, recv, small, small_recv):
    n = len(BIG)
    cidx = lax.axis_index("c").astype(jnp.int32).reshape(1)

    def body(c_ref, *refs):
        ins, outs = refs[:2 * n + 2], refs[2 * n + 2:]
        for i in range(n):
            outs[i][...] = ins[i][...] + ins[n + i][...]
        outs[n][...] = ins[2 * n][...] + ins[2 * n + 1][...]

    g_specs, r_specs = [], []
    for nm in BIG:
        hr, hc = _half_shape(nm)
        if GATHER[nm][1] == 0:
            g_specs.append(pl.BlockSpec((hr // 2, hc), lambda j, i, c: ((2 * j + c[0]) * 2 + i, 0)))
        else:
            g_specs.append(pl.BlockSpec((hr // 2, hc), lambda j, i, c: (2 * c[0] + i, j)))
        r_specs.append(pl.BlockSpec((1, hr // 2, hc), lambda j, i, c: (j, i, 0)))
    sm = pl.BlockSpec((SMALL_ROWS // 8, PACK_W), lambda j, i, c: (2 * j + i, 0))
    outs = pl.pallas_call(
        body, name="grads_pair_sum",
        grid_spec=pltpu.PrefetchScalarGridSpec(num_scalar_prefetch=1, grid=(4, 2), in_specs=g_specs + r_specs + [sm, sm],
                                               out_specs=r_specs + [sm]),
        out_shape=[jax.ShapeDtypeStruct((4,) + _half_shape(nm), f32) for nm in BIG] + [jax.ShapeDtypeStruct(small.shape, f32)],
        compiler_params=pltpu.CompilerParams(vmem_limit_bytes=VMEM_LIMIT),
    )(cidx, *[G[nm] for nm in BIG], *[recv[nm] for nm in BIG], small, small_recv)
    return dict(zip(BIG, outs[:n])), outs[n]


def _grads_chip_exchange(chip_sum, small):
    n = len(BIG)

    def body(*refs):
        ins, outs = refs[:n + 1], refs[n + 1:2 * n + 2]
        ssem, rsem, lsem = refs[2 * n + 2:]
        px, py, pc = _mesh_pos()
        me = 2 * px + py
        copies = []
        for i in range(n + 1):
            pick = (lambda ref, j: ref.at[j]) if i < n else (lambda ref, j: ref)
            copies.append(pltpu.make_async_copy(pick(ins[i], me), outs[i].at[me], lsem.at[i]))
            for k, (qx, qy) in enumerate(_chip_peers(px, py)):
                copies.append(pltpu.make_async_remote_copy(pick(ins[i], 2 * qx + qy), outs[i].at[me],
                                                           ssem.at[3 * i + k], rsem.at[3 * i + k],
                                                           device_id=(qx, qy, pc), device_id_type=MESH))
        _run_copies(copies)

    outs = pl.pallas_call(
        body, name="grads_chip_exchange", in_specs=[ANY] * (n + 1), out_specs=[ANY] * (n + 1),
        out_shape=[jax.ShapeDtypeStruct((4,) + _half_shape(nm), f32) for nm in BIG] + [jax.ShapeDtypeStruct((4,) + small.shape, f32)],
        scratch_shapes=[pltpu.SemaphoreType.DMA((3 * n + 3,)), pltpu.SemaphoreType.DMA((3 * n + 3,)),
                        pltpu.SemaphoreType.DMA((n + 1,))],
    )(*[chip_sum[nm] for nm in BIG], small)
    return dict(zip(BIG, outs[:n])), outs[n]


def _sum_slots(slots, small4):
    n = len(BIG)

    def body(*refs):
        for i in range(n + 1):
            x = refs[i]
            refs[n + 1 + i][...] = ((x[0] + x[1]) + x[2]) + x[3]

    specs_in, specs_out, shapes = [], [], []
    for nm in BIG:
        hr, hc = _half_shape(nm)
        specs_in.append(pl.BlockSpec((4, hr // 4, hc), lambda i: (0, i, 0)))
        specs_out.append(pl.BlockSpec((hr // 4, hc), lambda i: (i, 0)))
        shapes.append(jax.ShapeDtypeStruct((hr, hc), f32))
    specs_in.append(pl.BlockSpec((4, SMALL_ROWS // 4, PACK_W), lambda i: (0, i, 0)))
    specs_out.append(pl.BlockSpec((SMALL_ROWS // 4, PACK_W), lambda i: (i, 0)))
    shapes.append(jax.ShapeDtypeStruct((SMALL_ROWS, PACK_W), f32))
    outs = pl.pallas_call(
        body, name="grads_chip_sum", grid=(4,), in_specs=specs_in, out_specs=specs_out, out_shape=shapes,
        compiler_params=pltpu.CompilerParams(vmem_limit_bytes=VMEM_LIMIT),
    )(*[slots[nm] for nm in BIG], small4)
    return dict(zip(BIG, outs[:n])), outs[n]


def _halves_to_sibling(half):
    n = len(BIG)

    def body(*refs):
        ins, outs = refs[:n], refs[n:2 * n]
        ssem, rsem, lsem = refs[2 * n:]
        px, py, pc = _mesh_pos()
        copies = []
        for i in range(n):
            copies.append(pltpu.make_async_copy(ins[i], outs[i].at[pc], lsem.at[i]))
            copies.append(pltpu.make_async_remote_copy(ins[i], outs[i].at[pc], ssem.at[i], rsem.at[i],
                                                       device_id=(px, py, 1 - pc), device_id_type=MESH))
        _run_copies(copies)

    outs = pl.pallas_call(
        body, name="grads_halves_to_sibling", in_specs=[ANY] * n, out_specs=[ANY] * n,
        out_shape=[jax.ShapeDtypeStruct((2,) + _half_shape(nm), f32) for nm in BIG],
        scratch_shapes=[pltpu.SemaphoreType.DMA((n,)), pltpu.SemaphoreType.DMA((n,)), pltpu.SemaphoreType.DMA((n,))],
    )(*[half[nm] for nm in BIG])
    return dict(zip(BIG, outs))


def _flat_pad(v):
    v = v.reshape(-1)
    return jnp.pad(v, (0, _ceil_to(v.shape[0], PACK_W) - v.shape[0]))


def _pack_rows(parts, rows):
    flat = jnp.concatenate([_flat_pad(p) for p in parts])
    return jnp.pad(flat, (0, rows * PACK_W - flat.shape[0])).reshape(rows, PACK_W)


def _unpack_rows(buf, shapes):
    flat = buf.reshape(-1)
    out, off = [], 0
    for shp in shapes:
        n = 1
        for d in shp:
            n *= d
        out.append(flat[off:off + n].reshape(shp))
        off += _ceil_to(n, PACK_W)
    return out


def _adamw_math(w_, g_, m_, v_):
    m2 = ADAM_B1 * m_ + (1.0 - ADAM_B1) * g_
    v2 = ADAM_B2 * v_ + (1.0 - ADAM_B2) * (g_ * g_)
    m_hat = m2 / (1.0 - ADAM_B1 ** ADAM_STEP)
    v_hat = v2 / (1.0 - ADAM_B2 ** ADAM_STEP)
    return -ADAM_LR * (m_hat / (jnp.sqrt(v_hat) + ADAM_EPS) + ADAM_WD * w_), m2, v2


def _adamw(groups):
    ng = len(groups)

    def body(*refs):
        ins, outs = refs[:4 * ng], refs[4 * ng:]
        for i in range(ng):
            res = _adamw_math(*(r[...] for r in ins[4 * i:4 * i + 4]))
            for ref, val in zip(outs[3 * i:3 * i + 3], res):
                ref[...] = val

    in_specs, out_specs, out_shape = [], [], []
    for grp in groups:
        R, Cn = grp[0].shape
        spec = pl.BlockSpec((R // 8, Cn), lambda i: (i, 0))
        in_specs += [spec] * 4
        out_specs += [spec] * 3
        out_shape += [jax.ShapeDtypeStruct((R, Cn), f32)] * 3
    outs = pl.pallas_call(
        body, name="adamw", grid=(8,), in_specs=in_specs, out_specs=out_specs, out_shape=out_shape,
        compiler_params=pltpu.CompilerParams(vmem_limit_bytes=VMEM_LIMIT),
    )(*[a for grp in groups for a in grp])
    return [tuple(outs[3 * i:3 * i + 3]) for i in range(ng)]


def _forward_backward(x, tgt, W, S):
    L = x.shape[0]
    TM, TMW = 256, 128
    row = lambda c, dt=f32: (c, dt)
    hid = jnp.arange(RWKV_W) // HEAD
    E = (hid[:, None] == hid[None, :]).astype(f32)
    seg = (jnp.arange(S5_N)[None, :] // S5_P == jnp.arange(S5_G)[:, None]).astype(f32)

    w_in_t = W['w_in']
    w_p, w_u, w_g = w_in_t[:N_RWKV], w_in_t[N_RWKV:N_RWKV + S5_W], w_in_t[N_RWKV + S5_W:]
    zpad = jnp.zeros((64, RWKV_W), f32)
    w2p = jnp.concatenate([W['rwkv_w2'], zpad], axis=0)
    a2p = jnp.concatenate([zpad, W['rwkv_a2']], axis=0)
    g2 = W['rwkv_g2']
    prep_consts = [S['rwkv_shift_mu'], S['rwkv_w0'], S['rwkv_a0'], S['rwkv_k_k'], S['rwkv_k_a'], w2p, a2p, g2, E]
    out_consts = [S['rwkv_lnx_w'], S['rwkv_lnx_b'], S['rwkv_r_k'], E]
    cw, cb = W['ffn_conv_w'][:3], S['ffn_conv_b']

    a_re, a_im = S['s5_a_re'].reshape(S5_N, 1), S['s5_a_im'].reshape(S5_N, 1)
    ls = jnp.repeat(S['s5_log_step'].reshape(S5_G, 1), S5_P, axis=0)
    b_re, b_im = S['s5_b_re'].reshape(S5_N, S5_C), S['s5_b_im'].reshape(S5_N, S5_C)
    ar, ai, bbr, bbi = _s5_disc_fwd(a_re, a_im, ls, b_re, b_im)
    abar = jnp.concatenate([ar.reshape(1, S5_N), ai.reshape(1, S5_N)], axis=1)
    eye = jnp.eye(S5_G, dtype=f32)

    def bdiag_in(bb):
        t = bb.reshape(S5_G, S5_P, S5_C).transpose(0, 2, 1)
        return (t[:, :, None, :] * eye[:, None, :, None]).reshape(S5_W, S5_N)

    def bdiag_out(cc):
        t = cc.transpose(0, 2, 1)
        return (t[:, :, None, :] * eye[:, None, :, None]).reshape(S5_N, S5_W)

    def undiag_in(m):
        t = m.reshape(S5_G, S5_C, S5_G, S5_P)
        t = jnp.sum(t * eye[:, None, :, None], axis=2)
        return t.transpose(0, 2, 1).reshape(S5_N, S5_C)

    def undiag_out(m):
        t = m.reshape(S5_G, S5_P, S5_G, S5_C)
        t = jnp.sum(t * eye[:, None, :, None], axis=2)
        return t.transpose(0, 2, 1)

    bmat = jnp.concatenate([bdiag_in(bbr), bdiag_in(bbi)], axis=1).astype(bf16)
    cmat = jnp.concatenate([bdiag_out(S['s5_c_re'].reshape(S5_G, S5_C, S5_P)),
                            -bdiag_out(S['s5_c_im'].reshape(S5_G, S5_C, S5_P))], axis=0).astype(bf16)

    g1, g2n, g3, g4 = S['norm_mix_pre'], S['norm_mix_post'], S['norm_ffn_pre'], S['norm_ffn_post']
    (h1,) = _rowcall("norm_pre", lambda i, n, R, P, X, C: ((_rms(R[0], C[0]),), ()), L, TM, [x], [g1],
                     out_rows=[row(D_MODEL, bf16)])
    p = _mm(h1, w_p, 'nt', "mm_p")
    u = _mm(h1, w_u, 'nt', "mm_u")
    gp = _mm(h1, w_g, 'nt', "mm_g")

    def prep_fn(i, n, R, P, X, C):
        q = R[0] + (_shift_down(R[0], P[0], i, 1) - R[0]) * C[0]
        return _prep(q, *C[1:]), ()

    r, lw, k2, v, an, bv, g = _rowcall("rwkv_prep", prep_fn, L, TM, [p], prep_consts,
                                       out_rows=[row(RWKV_W)] * 7, prev=[0])
    y, ck = _wkv7_fwd(r, lw, k2, v, an, bv)
    (o_a,) = _rowcall("rwkv_out", lambda i, n, R, P, X, C: ((_rwkv_out(*R, *C),), ()), L, TM, [y, r, k2, v, g],
                      out_consts, out_rows=[row(RWKV_W)])
    o_r = _mm(o_a, W['w_branch_rwkv'], 'nn', "mm_br")

    bu = _mm(u, bmat, 'nn', "mm_bu")
    st = _s5_scan(bu, abar, False, "s5_scan")
    ysc = _mm(st, cmat, 'nn', "mm_cs")
    (yg,) = _rowcall("s5_mid", lambda i, n, R, P, X, C: ((_s5_mid(*R, *C),), ()), L, TM, [ysc, u], [S['s5_d']],
                     out_rows=[row(S5_W)])
    z2 = _mm(yg, W['s5_w_glu'], 'nn', "mm_glu")
    (o_b,) = _rowcall("s5_glu", lambda i, n, R, P, X, C: ((_s5_glu(*R, *C),), ()), L, TM, [yg, z2], [S['s5_b_glu']],
                      out_rows=[row(S5_W)])
    o_s = _mm(o_b, W['w_branch_s5'], 'nn', "mm_bs")

    (merged,) = _rowcall("merge", lambda i, n, R, P, X, C: ((_merge(*R, *C),), ()), L, TM, [gp, o_r, o_s],
                         [S['b_gate']], out_rows=[row(D_MODEL)])
    mixed = _mm(merged, W['w_out'], 'nn', "mm_out")

    def resid_fn(i, n, R, P, X, C):
        x1_ = R[0] + _rms(R[1], C[0])
        return (x1_, _rms(x1_, C[1])), ()

    x1, h2 = _rowcall("resid_norm", resid_fn, L, TM, [x, mixed], [g2n, g3], out_rows=[row(D_MODEL), row(D_MODEL, bf16)])

    z = _mm(h2, W['ffn_w_up'], 'nn', "mm_up")

    def conv(zt, zprev, i, cw_, cb_):
        z2s, z1s = _shift_down(zt, zprev, i, 2), _shift_down(zt, zprev, i, 1)
        return cb_ + cw_[0:1] * z2s + cw_[1:2] * z1s + cw_[2:3] * zt, z2s, z1s

    (act,) = _rowcall("conv_act", lambda i, n, R, P, X, C: ((_act(conv(R[0], P[0], i, C[0], C[1])[0]),), ()), L, TMW,
                      [z], [cw, cb], out_rows=[row(D_FF)], prev=[0])
    f = _mm(act, W['ffn_w_down'], 'nn', "mm_down")

    def final_fn(i, n, R, P, X, C):
        x1_, f_, t_ = R
        fn_, vjp = jax.vjp(_rms, f_, C[0])
        diff = x1_ + fn_ - t_
        loss = jnp.sum(diff * diff) * (0.5 / D_MODEL)
        dx2_ = diff * (1.0 / D_MODEL)
        df_, dg4_ = vjp(dx2_)
        return (df_, dx2_), (jnp.full((1, PACK_W), loss, f32), dg4_)

    df, dx2, loss, dg4 = _rowcall("loss_head", final_fn, L, TM, [x1, f, tgt], [g4],
                                  out_rows=[row(D_MODEL)] * 2, out_accs=[(1, PACK_W), (1, D_MODEL)])
    G = {'norm_ffn_post': dg4}

    dact = _mm(df, W['ffn_w_down'], 'nt', "mm_down_dx")
    G['ffn_w_down'] = _mm(act, df, 'tn', "mm_down_dw")

    def conv_bwd_fn(i, n, R, P, X, C):
        zc, z2s, z1s = conv(R[0], P[0], i, C[0], C[1])
        _, vjp = jax.vjp(_act, zc)
        (dzc_,) = vjp(R[1])
        return (dzc_,), (_sum0(dzc_), _sum0(dzc_ * z2s), _sum0(dzc_ * z1s), _sum0(dzc_ * R[0]))

    wide = (1, 2 * D_FF)
    dzc, dcb, dcw0, dcw1, dcw2 = _rowcall("conv_act_bwd", conv_bwd_fn, L, TMW, [z, dact], [cw, cb],
                                          out_rows=[row(2 * D_FF)], out_accs=[wide] * 4, prev=[0])
    G['ffn_conv_b'] = dcb
    G['ffn_conv_w'] = jnp.concatenate([dcw0, dcw1, dcw2], axis=0)

    def conv_shift_fn(i, n, R, P, X, C):
        d = R[0]
        return (C[0][2:3] * d + C[0][1:2] * _shift_up(d, X[0], i, n, 1) + C[0][0:1] * _shift_up(d, X[0], i, n, 2),), ()

    (dz,) = _rowcall("conv_shift_bwd", conv_shift_fn, L, TMW, [dzc], [cw], out_rows=[row(2 * D_FF)], nxt=[0])
    dh2 = _mm(dz, W['ffn_w_up'], 'nt', "mm_up_dx")
    G['ffn_w_up'] = _mm(h2, dz, 'tn', "mm_up_dw")

    def norm2_bwd_fn(i, n, R, P, X, C):
        x1_, mixed_, dx2_, dh2_ = R
        _, vjp3 = jax.vjp(_rms, x1_, C[1])
        dx1a, dg3_ = vjp3(dh2_)
        dx1_ = dx2_ + dx1a
        _, vjp2 = jax.vjp(_rms, mixed_, C[0])
        dmixed_, dg2_ = vjp2(dx1_)
        return (dx1_, dmixed_), (dg2_, dg3_)

    dx1, dmixed, dg2n, dg3 = _rowcall("norm_mid_bwd", norm2_bwd_fn, L, TM, [x1, mixed, dx2, dh2], [g2n, g3],
                                      out_rows=[row(D_MODEL)] * 2, out_accs=[(1, D_MODEL)] * 2)
    G['norm_mix_post'], G['norm_ffn_pre'] = dg2n, dg3

    dmerged = _mm(dmixed, W['w_out'], 'nt', "mm_out_dx")
    G['w_out'] = _mm(merged, dmixed, 'tn', "mm_out_dw")

    def merge_bwd_fn(i, n, R, P, X, C):
        _, vjp = jax.vjp(_merge, R[0], R[1], R[2], C[0])
        dgp_, do_r_, do_s_, dbg_ = vjp(R[3])
        return (dgp_, do_r_, do_s_), (dbg_,)

    dgp, do_r, do_s, G['b_gate'] = _rowcall("merge_bwd", merge_bwd_fn, L, TM, [gp, o_r, o_s, dmerged], [S['b_gate']],
                                            out_rows=[row(2 * D_MODEL), row(D_MODEL), row(D_MODEL)],
                                            out_accs=[(1, 2 * D_MODEL)])
    do_a = _mm(do_r, W['w_branch_rwkv'], 'nt', "mm_br_dx")
    G['w_branch_rwkv'] = _mm(o_a, do_r, 'tn', "mm_br_dw")
    do_b = _mm(do_s, W['w_branch_s5'], 'nt', "mm_bs_dx")
    G['w_branch_s5'] = _mm(o_b, do_s, 'tn', "mm_bs_dw")

    def glu_bwd_fn(i, n, R, P, X, C):
        _, vjp = jax.vjp(_s5_glu, R[0], R[1], C[0])
        dyg1_, dz2_, dbg_ = vjp(R[2])
        return (dyg1_, dz2_), (dbg_,)

    dyg1, dz2, G['s5_b_glu'] = _rowcall("s5_glu_bwd", glu_bwd_fn, L, TM, [yg, z2, do_b], [S['s5_b_glu']],
                                        out_rows=[row(S5_W)] * 2, out_accs=[(1, S5_W)])
    dyg2 = _mm(dz2, W['s5_w_glu'], 'nt', "mm_glu_dx")
    G['s5_w_glu'] = _mm(yg, dz2, 'tn', "mm_glu_dw")

    def mid_bwd_fn(i, n, R, P, X, C):
        _, vjp = jax.vjp(_s5_mid, R[0], R[1], C[0])
        dysc_, du_, dd_ = vjp(R[2] + R[3])
        return (dysc_, du_), (dd_,)

    dysc, du1, G['s5_d'] = _rowcall("s5_mid_bwd", mid_bwd_fn, L, TM, [ysc, u, dyg1, dyg2], [S['s5_d']],
                                    out_rows=[row(S5_W)] * 2, out_accs=[(1, S5_W)])
    dst = _mm(dysc, cmat, 'nt', "mm_cs_dx")
    dcmat = _mm(st, dysc, 'tn', "mm_cs_dw")
    lam = _s5_scan(dst, abar, True, "s5_scan_bwd")

    def s5_da_fn(i, n, R, P, X, C):
        lr, li = R[0][:, :S5_N], R[0][:, S5_N:]
        sp = _shift_down(R[1], P[0], i, 1)
        sr, si = sp[:, :S5_N], sp[:, S5_N:]
        return (), (jnp.concatenate([_sum0(lr * sr + li * si), _sum0(li * sr - lr * si)], axis=1),)

    (dabar,) = _rowcall("s5_da", s5_da_fn, L, TM, [lam, st], out_accs=[(1, 2 * S5_N)], prev=[1])
    du2 = _mm(lam, bmat, 'nt', "mm_bu_dx")
    dbmat = _mm(u, lam, 'tn', "mm_bu_dw")
    da_re, da_im, dls, db_re, db_im = _s5_disc_bwd(
        a_re, a_im, ls, b_re, b_im, dabar[:, :S5_N].reshape(S5_N, 1), dabar[:, S5_N:].reshape(S5_N, 1),
        undiag_in(dbmat[:, :S5_N]), undiag_in(dbmat[:, S5_N:]), seg)
    G['s5_a_re'], G['s5_a_im'], G['s5_log_step'] = da_re, da_im, dls
    G['s5_b_re'], G['s5_b_im'] = db_re, db_im
    G['s5_c_re'], G['s5_c_im'] = undiag_out(dcmat[:S5_N]), -undiag_out(dcmat[S5_N:])

    def out_bwd_fn(i, n, R, P, X, C):
        _, vjp = jax.vjp(_rwkv_out, *R[:5], *C)
        gs = vjp(R[5])
        return gs[:5], gs[5:8]

    dy, dr1, dk1, dv1, dg, dlw, dlb, drk = _rowcall("rwkv_out_bwd", out_bwd_fn, L, TM, [y, r, k2, v, g, do_a], out_consts,
                                                    out_rows=[row(RWKV_W)] * 5, out_accs=[(1, RWKV_W)] * 3)
    G['rwkv_lnx_w'], G['rwkv_lnx_b'], G['rwkv_r_k'] = dlw, dlb, drk
    dr2, dlwk, dk2b, dv2, dan, dbv = _wkv7_bwd(r, lw, k2, v, an, bv, ck, dy)

    def prep_bwd_fn(i, n, R, P, X, C):
        p_ = R[0]
        d1 = _shift_down(p_, P[0], i, 1) - p_
        q = p_ + d1 * C[0]
        _, vjp = jax.vjp(_prep, q, *C[1:])
        cots = (R[1] + R[2], R[3], R[4] + R[5], R[6] + R[7], R[8], R[9], R[10])
        gs = vjp(cots)
        return (gs[0],), (_sum0(gs[0] * d1),) + tuple(gs[1:8])

    small, lowr = (1, RWKV_W), (128, RWKV_W)
    dq, dmu, dw0, da0, dkk, dka, dw2p, da2p, dg2 = _rowcall(
        "rwkv_prep_bwd", prep_bwd_fn, L, TM, [p, dr1, dr2, dlwk, dk1, dk2b, dv1, dv2, dan, dbv, dg],
        prep_consts, out_rows=[row(N_RWKV)], out_accs=[(1, N_RWKV)] + [small] * 4 + [lowr] * 3, prev=[0])
    G['rwkv_shift_mu'], G['rwkv_w0'], G['rwkv_a0'], G['rwkv_k_k'], G['rwkv_k_a'] = dmu, dw0, da0, dkk, dka
    G['rwkv_w2'], G['rwkv_a2'], G['rwkv_g2'] = dw2p[:64], da2p[64:], dg2

    def shift_bwd_fn(i, n, R, P, X, C):
        dm = R[0] * C[0]
        return (R[0] - dm + _shift_up(dm, X[0] * C[0], i, n, 1),), ()

    (dp,) = _rowcall("shift_bwd", shift_bwd_fn, L, TM, [dq], [S['rwkv_shift_mu']], out_rows=[row(N_RWKV)], nxt=[0])

    (du,) = _rowcall("add_du", lambda i, n, R, P, X, C: ((R[0] + R[1],), ()), L, TM, [du1, du2], out_rows=[row(S5_W)])
    dproj = jnp.concatenate([dp, du, dgp], axis=1)
    dh1 = _mm(dproj, w_in_t, 'nn', "mm_in_dx")
    G['w_in'] = _mm(dproj, h1, 'tn', "mm_in_dw")

    def norm1_bwd_fn(i, n, R, P, X, C):
        _, vjp = jax.vjp(_rms, R[0], C[0])
        dxa, dg1_ = vjp(R[2])
        return (R[1] + dxa,), (dg1_,)

    dx, G['norm_mix_pre'] = _rowcall("norm_pre_bwd", norm1_bwd_fn, L, TM, [x, dx1, dh1], [g1],
                                     out_rows=[row(D_MODEL)], out_accs=[(1, D_MODEL)])
    return loss, dx, G


def kernel(x, norm_mix_pre, norm_mix_post, norm_ffn_pre, norm_ffn_post, w_in, b_gate, rwkv_shift_mu, rwkv_w0, rwkv_w2, rwkv_a0, rwkv_a2, rwkv_g2, rwkv_k_k, rwkv_k_a, rwkv_r_k, rwkv_lnx_w, rwkv_lnx_b, s5_a_re, s5_a_im, s5_b_re, s5_b_im, s5_c_re, s5_c_im, s5_d, s5_log_step, s5_w_glu, s5_b_glu, w_branch_rwkv, w_branch_s5, w_out, ffn_w_up, ffn_conv_w, ffn_conv_b, ffn_w_down, loss_target, m_norm_mix_pre, m_norm_mix_post, m_norm_ffn_pre, m_norm_ffn_post, m_w_in, m_b_gate, m_rwkv_shift_mu, m_rwkv_w0, m_rwkv_w2, m_rwkv_a0, m_rwkv_a2, m_rwkv_g2, m_rwkv_k_k, m_rwkv_k_a, m_rwkv_r_k, m_rwkv_lnx_w, m_rwkv_lnx_b, m_s5_a_re, m_s5_a_im, m_s5_b_re, m_s5_b_im, m_s5_c_re, m_s5_c_im, m_s5_d, m_s5_log_step, m_s5_w_glu, m_s5_b_glu, m_w_branch_rwkv, m_w_branch_s5, m_w_out, m_ffn_w_up, m_ffn_conv_w, m_ffn_conv_b, m_ffn_w_down, v_norm_mix_pre, v_norm_mix_post, v_norm_ffn_pre, v_norm_ffn_post, v_w_in, v_b_gate, v_rwkv_shift_mu, v_rwkv_w0, v_rwkv_w2, v_rwkv_a0, v_rwkv_a2, v_rwkv_g2, v_rwkv_k_k, v_rwkv_k_a, v_rwkv_r_k, v_rwkv_lnx_w, v_rwkv_lnx_b, v_s5_a_re, v_s5_a_im, v_s5_b_re, v_s5_b_im, v_s5_c_re, v_s5_c_im, v_s5_d, v_s5_log_step, v_s5_w_glu, v_s5_b_glu, v_w_branch_rwkv, v_w_branch_s5, v_w_out, v_ffn_w_up, v_ffn_conv_w, v_ffn_conv_b, v_ffn_w_down):
    A = dict(locals())
    me = 2 * lax.axis_index("x") + lax.axis_index("y")
    blk = lambda n: A[n][0]

    mine = {n: (blk(n).T if n == 'w_in' else blk(n)).astype(bf16) for n in BIG}
    mine.update({n: blk(n) for n in TINY})
    mine['ffn_conv_w'] = jnp.pad(blk('ffn_conv_w'), ((0, 5), (0, 0)))
    W = _gather_weights(mine)
    S = {n: A[n].reshape(1, -1) for n in SMALL}

    loss, dx, G = _forward_backward(x[0], loss_target[0], W, S)

    tiny_shapes = [G[n].shape for n in TINY]
    small_buf = _pack_rows([G[n] for n in SMALL] + [G[n] for n in TINY] + [loss], SMALL_ROWS)
    recv, small_recv = _grads_to_sibling(G, small_buf)
    chip_sum, small_sum = _pair_add(G, recv, small_buf, small_recv)
    slots, small4 = _grads_chip_exchange(chip_sum, small_sum)
    half, small_tot = _sum_slots(slots, small4)
    both = _halves_to_sibling(half)
    grad = {}
    for n in BIG:
        (R, Cn), axis = GATHER[n]
        grad[n] = both[n].reshape(R // 4, Cn) if axis == 0 else both[n].reshape(R, Cn // 4)
    grad['w_in'] = grad['w_in'].T
    vals = _unpack_rows(small_tot, [A[n].shape for n in SMALL] + tiny_shapes + [(1, PACK_W)])
    grad.update(zip(SMALL, vals))
    for n, full in zip(TINY, vals[len(SMALL):]):
        cs = A[n].shape[2]
        grad[n] = lax.dynamic_slice_in_dim(full, me * cs, cs, axis=1)
    loss_out = vals[-1][0, 0]

    packed = SMALL + TINY
    groups = [(blk(n), grad[n], blk('m_' + n), blk('v_' + n)) for n in BIG]
    groups.append(tuple(_pack_rows([src(n) for n in packed], ADAM_ROWS)
                        for src in (lambda n: A[n], lambda n: grad[n], lambda n: A['m_' + n], lambda n: A['v_' + n])))
    res = _adamw(groups)
    outs = [dict(), dict(), dict()]
    for n, r3 in zip(BIG, res[:-1]):
        for d, val in zip(outs, r3):
            d[n] = val
    for d, buf in zip(outs, res[-1]):
        d.update(zip(packed, _unpack_rows(buf, [A[n].shape for n in packed])))
    full = lambda d: [d[n].reshape(A[n].shape) for n in WEIGHTS]
    return (loss_out, dx[None], *full(grad), *full(outs[0]), *full(outs[1]), *full(outs[2]))
```

```python
import functools

import jax
import jax.numpy as jnp
from jax import lax
from jax.experimental import pallas as pl
from jax.experimental.pallas import tpu as pltpu

f32, bf16 = jnp.float32, jnp.bfloat16
MESH = pl.DeviceIdType.MESH

D_MODEL = 1024
RWKV_W = 512
HEADS, HEAD = 8, 64
N_RWKV = 1792
S5_W = 512
S5_G, S5_P, S5_C = 32, 64, 16
S5_N = S5_G * S5_P
D_FF = 2816
NORM_EPS = 1e-6
LNX_EPS = 64e-5
ADAM_LR, ADAM_B1, ADAM_B2, ADAM_EPS, ADAM_WD, ADAM_STEP = 0.001, 0.9, 0.999, 1e-08, 0.01, 10

VMEM_LIMIT = 48 * 1024 * 1024
PACK_W = 1024
WKV_C = 64
S5_T = 256

WEIGHTS = ['norm_mix_pre', 'norm_mix_post', 'norm_ffn_pre', 'norm_ffn_post', 'w_in', 'b_gate', 'rwkv_shift_mu',
           'rwkv_w0', 'rwkv_w2', 'rwkv_a0', 'rwkv_a2', 'rwkv_g2', 'rwkv_k_k', 'rwkv_k_a', 'rwkv_r_k', 'rwkv_lnx_w',
           'rwkv_lnx_b', 's5_a_re', 's5_a_im', 's5_b_re', 's5_b_im', 's5_c_re', 's5_c_im', 's5_d', 's5_log_step',
           's5_w_glu', 's5_b_glu', 'w_branch_rwkv', 'w_branch_s5', 'w_out', 'ffn_w_up', 'ffn_conv_w', 'ffn_conv_b',
           'ffn_w_down']


def _ceil_to(n, m):
    return -(-n // m) * m


def _mesh_pos():
    return lax.axis_index("x"), lax.axis_index("y"), lax.axis_index("c")


def _pick(d, cap=4096):
    for c in (1024, 1408, 2176, 896, 512, 256, 128):
        if c <= cap and d % c == 0:
            return c
    raise ValueError(d)


def _mm(a, b, mode, name, out_dtype=f32):
    if mode == 'tn':
        (K, M), (K2, N) = a.shape, b.shape
    elif mode == 'nt':
        (M, K), (N, K2) = a.shape, b.shape
    else:
        (M, K), (K2, N) = a.shape, b.shape
    assert K == K2, (name, a.shape, b.shape)
    if mode == 'tn':
        tm = _pick(M, 2176)
        tn = _pick(N, 512 if tm > 1408 else (1024 if tm > 1024 else 1408))
        tk = _pick(K, 512)
    else:
        tm, tn, tk = _pick(M, 512), _pick(N), _pick(K)
    nk = K // tk
    dims = {'nn': ((1,), (0,)), 'nt': ((1,), (1,)), 'tn': ((0,), (0,))}[mode]

    def body(a_ref, b_ref, o_ref, acc_ref):
        k = pl.program_id(2)

        @pl.when(k == 0)
        def _():
            acc_ref[...] = jnp.zeros_like(acc_ref)

        acc_ref[...] += lax.dot_general(a_ref[...].astype(bf16), b_ref[...].astype(bf16), (dims, ((), ())),
                                        preferred_element_type=f32)

        @pl.when(k == nk - 1)
        def _():
            o_ref[...] = acc_ref[...].astype(o_ref.dtype)

    a_spec = pl.BlockSpec((tk, tm), lambda i, j, k: (k, i)) if mode == 'tn' else pl.BlockSpec((tm, tk), lambda i, j, k: (i, k))
    b_spec = pl.BlockSpec((tn, tk), lambda i, j, k: (j, k)) if mode == 'nt' else pl.BlockSpec((tk, tn), lambda i, j, k: (k, j))
    return pl.pallas_call(
        body, name=name, grid=(M // tm, N // tn, nk),
        in_specs=[a_spec, b_spec], out_specs=pl.BlockSpec((tm, tn), lambda i, j, k: (i, j)),
        out_shape=jax.ShapeDtypeStruct((M, N), out_dtype),
        scratch_shapes=[pltpu.VMEM((tm, tn), f32)],
        compiler_params=pltpu.CompilerParams(dimension_semantics=("parallel", "parallel", "arbitrary"),
                                             vmem_limit_bytes=VMEM_LIMIT),
    )(a, b)


def _rowcall(name, fn, L, tm, rows, consts=(), out_rows=(), out_accs=(), prev=(), nxt=()):
    nsteps = L // tm
    nb8 = tm // 8
    last8 = L // 8 - 1
    n_r, n_p, n_x, n_c, n_or = len(rows), len(prev), len(nxt), len(consts), len(out_rows)

    def body(*refs):
        i = pl.program_id(0)
        vals = [r[...] for r in refs[:n_r + n_p + n_x + n_c]]
        R, P = vals[:n_r], vals[n_r:n_r + n_p]
        X, C = vals[n_r + n_p:n_r + n_p + n_x], vals[n_r + n_p + n_x:]
        o_refs = refs[n_r + n_p + n_x + n_c:]
        outs_r, outs_a = fn(i, nsteps, R, P, X, C)
        for ref, v in zip(o_refs[:n_or], outs_r, strict=True):
            ref[...] = v.astype(ref.dtype)
        if out_accs:
            @pl.when(i == 0)
            def _():
                for ref in o_refs[n_or:]:
                    ref[...] = jnp.zeros_like(ref)

            for ref, v in zip(o_refs[n_or:], outs_a, strict=True):
                ref[...] += v

    def const_spec(c):
        nd = c.ndim
        return pl.BlockSpec(c.shape, lambda i: (0,) * nd)

    in_specs = ([pl.BlockSpec((tm, a.shape[1]), lambda i: (i, 0)) for a in rows]
                + [pl.BlockSpec((8, rows[j].shape[1]), lambda i: (jnp.maximum(i * nb8 - 1, 0), 0)) for j in prev]
                + [pl.BlockSpec((8, rows[j].shape[1]), lambda i: (jnp.minimum((i + 1) * nb8, last8), 0)) for j in nxt]
                + [const_spec(c) for c in consts])
    out_specs = ([pl.BlockSpec((tm, c), lambda i: (i, 0)) for c, _ in out_rows]
                 + [pl.BlockSpec(s, lambda i: (0, 0)) for s in out_accs])
    out_shape = ([jax.ShapeDtypeStruct((L, c), dt) for c, dt in out_rows]
                 + [jax.ShapeDtypeStruct(s, f32) for s in out_accs])
    args = list(rows) + [rows[j] for j in prev] + [rows[j] for j in nxt] + list(consts)
    return pl.pallas_call(
        body, name=name, grid=(nsteps,), in_specs=in_specs, out_specs=out_specs, out_shape=out_shape,
        compiler_params=pltpu.CompilerParams(dimension_semantics=("arbitrary",), vmem_limit_bytes=VMEM_LIMIT),
    )(*args)


def _shift_down(x, prev8, i, k):
    rolled = pltpu.roll(x, k, axis=0)
    pfix = jnp.where(i > 0, pltpu.roll(prev8, k, axis=0), 0.0)
    row8 = lax.broadcasted_iota(jnp.int32, pfix.shape, 0)
    top = jnp.where(row8 < k, pfix, rolled[:8])
    return jnp.concatenate([top, rolled[8:]], axis=0)


def _shift_up(x, next8, i, nsteps, k):
    tm = x.shape[0]
    rolled = pltpu.roll(x, tm - k, axis=0)
    nfix = jnp.where(i < nsteps - 1, pltpu.roll(next8, 8 - k, axis=0), 0.0)
    row8 = lax.broadcasted_iota(jnp.int32, nfix.shape, 0)
    bot = jnp.where(row8 >= 8 - k, nfix, rolled[tm - 8:])
    return jnp.concatenate([rolled[:tm - 8], bot], axis=0)


def _sum0(x):
    return jnp.sum(x, axis=0, keepdims=True)


def _rms(x, g):
    return x * lax.rsqrt(jnp.mean(x * x, axis=-1, keepdims=True) + NORM_EPS) * g


def _softplus(x):
    return jnp.maximum(x, 0.0) + jnp.log(1.0 + jnp.exp(-jnp.abs(x)))


def _gelu(x):
    return 0.5 * x * (1.0 + jnp.tanh(0.7978845608028654 * (x + 0.044715 * x * x * x)))


def _dot32(a, b):
    return jnp.dot(a, b, preferred_element_type=f32, precision=lax.Precision.HIGHEST)


def _prep(q, w0, a0, k_k, k_a, w2p, a2p, g2, E):
    r, k, v = q[:, 0:512], q[:, 512:1024], q[:, 1024:1536]
    wa, gd = q[:, 1536:1664], q[:, 1664:1792]
    wlog = -_softplus(-(w0 + _dot32(jnp.tanh(wa), w2p))) - 0.5
    lw = -jnp.exp(wlog)
    a = jax.nn.sigmoid(a0 + _dot32(wa, a2p))
    g = _dot32(jax.nn.sigmoid(gd), g2)
    kk = k * k_k
    kkn = kk / jnp.maximum(jnp.sqrt(_dot32(kk * kk, E)), 1e-12)
    k2 = k * (1.0 + (a - 1.0) * k_a)
    return r, lw, k2, v, -kkn, kkn * a, g


def _rwkv_out(y, r, k2, v, g, lnx_w, lnx_b, r_k, E):
    mean = _dot32(y, E) * (1.0 / HEAD)
    yc = y - mean
    var = _dot32(yc * yc, E) * (1.0 / HEAD)
    yn = yc * lax.rsqrt(var + LNX_EPS) * lnx_w + lnx_b
    bonus = _dot32(r * k2 * r_k, E) * v
    return (yn + bonus) * g


def _s5_mid(ysc, u, d):
    return _gelu(ysc + d * u)


def _s5_glu(yg, z2, b_glu):
    return yg * jax.nn.sigmoid(z2 + b_glu)


def _merge(gp, o_r, o_s, b_gate):
    gates = jax.nn.sigmoid(gp + b_gate)
    return gates[:, :D_MODEL] * o_r + gates[:, D_MODEL:] * o_s


def _act(zc):
    return _gelu(zc[:, :D_FF]) * zc[:, D_FF:]


def _s5_disc(a_re, a_im, ls, b_re, b_im):
    dt = jnp.exp(ls)
    er = jnp.exp(a_re * dt)
    ar, ai = er * jnp.cos(a_im * dt), er * jnp.sin(a_im * dt)
    x, y = ar - 1.0, ai
    den = a_re * a_re + a_im * a_im
    fr, fi = (x * a_re + y * a_im) / den, (y * a_re - x * a_im) / den
    return ar, ai, fr * b_re - fi * b_im, fr * b_im + fi * b_re


_DIMS = {'nn': ((1,), (0,)), 'nt': ((1,), (1,)), 'tn': ((0,), (0,))}


def _raw_bdot(a, b, mode):
    return lax.dot_general(a.astype(bf16), b.astype(bf16), (_DIMS[mode], ((), ())), preferred_element_type=f32)


@functools.partial(jax.custom_vjp, nondiff_argnums=(2,))
def _bdot(a, b, mode):
    return _raw_bdot(a, b, mode)


def _bdot_fwd(a, b, mode):
    return _raw_bdot(a, b, mode), (a, b)


def _bdot_bwd(mode, res, g):
    a, b = res
    if mode == 'nn':
        return _raw_bdot(g, b, 'nt'), _raw_bdot(a, g, 'tn')
    if mode == 'nt':
        return _raw_bdot(g, b, 'nn'), _raw_bdot(g, a, 'tn')
    return _raw_bdot(b, g, 'nt'), _raw_bdot(a, g, 'nn')


_bdot.defvjp(_bdot_fwd, _bdot_bwd)


def _wkv_chunk(S0, r, lw, k, v, a, b, tri, bd):
    C = r[0].shape[0]
    P = range(len(r))
    lane = lax.broadcasted_iota(jnp.int32, (1, 2 * HEAD), 1)
    halves = [(lane < HEAD).astype(f32), (lane >= HEAD).astype(f32)]
    eye = (lax.broadcasted_iota(jnp.int32, (C, C), 0) == lax.broadcasted_iota(jnp.int32, (C, C), 1)).astype(f32)
    sl = tri - eye
    cum = [_dot32(tri, lw[p]) for p in P]
    g = [jnp.exp(cum[p]) for p in P]
    gi = [jnp.exp(-cum[p]) for p in P]
    at = [a[p] * jnp.exp(cum[p] - lw[p]) for p in P]
    rt = [r[p] * g[p] for p in P]
    kb = [k[p] * gi[p] for p in P]
    bb = [b[p] * gi[p] for p in P]
    PE = [(p, e) for p in P for e in range(2)]
    atm = {pe: at[pe[0]] * halves[pe[1]] for pe in PE}
    rtm = {pe: rt[pe[0]] * halves[pe[1]] for pe in PE}
    aab = {pe: _bdot(atm[pe], bb[pe[0]], 'nt') * sl for pe in PE}
    aak = {pe: _bdot(atm[pe], kb[pe[0]], 'nt') * sl for pe in PE}
    rk = {pe: _bdot(rtm[pe], kb[pe[0]], 'nt') * tri for pe in PE}
    rb = {pe: _bdot(rtm[pe], bb[pe[0]], 'nt') * tri for pe in PE}
    rhs = [_bdot(at[p], S0[p], 'nt') + sum(halves[e] * _bdot(aak[(p, e)], v[p], 'nn') for e in range(2)) for p in P]
    y0 = [_bdot(rt[p], S0[p], 'nt') + sum(halves[e] * _bdot(rk[(p, e)], v[p], 'nn') for e in range(2)) for p in P]
    x = {pe: eye + aab[pe] for pe in PE}
    pw = aab
    n = 1
    while 2 * n < C:
        pw = {pe: _bdot(pw[pe], pw[pe], 'nn') for pe in PE}
        x = {pe: x[pe] + _bdot(x[pe], pw[pe], 'nn') for pe in PE}
        n *= 2
    u = [sum(halves[e] * _bdot(x[(p, e)], rhs[p], 'nn') for e in range(2)) for p in P]
    y = [y0[p] + sum(halves[e] * _bdot(rb[(p, e)], u[p], 'nn') for e in range(2)) for p in P]
    S1 = [g[p][C - 1:C, :] * (S0[p] + bd * (_bdot(v[p], kb[p], 'tn') + _bdot(u[p], bb[p], 'tn'))) for p in P]
    return y, S1


def _pairs(x):
    return [x[:, 2 * HEAD * p:2 * HEAD * (p + 1)] for p in range(HEADS // 2)]


def _wkv_consts():
    tri = jnp.tril(jnp.ones((WKV_C, WKV_C), f32))
    hid = jnp.arange(2 * HEAD) // HEAD
    return tri, (hid[:, None] == hid[None, :]).astype(f32)


def _wkv7_fwd(r, lw, k, v, a, b):
    L = r.shape[0]
    nc, npair = L // WKV_C, HEADS // 2

    def body(r_ref, lw_ref, k_ref, v_ref, a_ref, b_ref, tri_ref, bd_ref, y_ref, ck_ref, s_ref):
        @pl.when(pl.program_id(0) == 0)
        def _():
            s_ref[...] = jnp.zeros_like(s_ref)

        s0 = [s_ref[p] for p in range(npair)]
        for p in range(npair):
            ck_ref[0, p] = s0[p]
        y, s1 = _wkv_chunk(s0, *(_pairs(x) for x in (r_ref, lw_ref, k_ref, v_ref, a_ref, b_ref)), tri_ref[...], bd_ref[...])
        for p in range(npair):
            y_ref[:, 2 * HEAD * p:2 * HEAD * (p + 1)] = y[p]
            s_ref[p] = s1[p]

    row = pl.BlockSpec((WKV_C, RWKV_W), lambda c: (c, 0))
    sspec = pl.BlockSpec((1, npair, 2 * HEAD, 2 * HEAD), lambda c: (c, 0, 0, 0))
    return pl.pallas_call(
        body, name="wkv7_fwd", grid=(nc,),
        in_specs=[row] * 6 + [pl.BlockSpec((WKV_C, WKV_C), lambda c: (0, 0)), pl.BlockSpec((2 * HEAD, 2 * HEAD), lambda c: (0, 0))],
        out_specs=[row, sspec],
        out_shape=[jax.ShapeDtypeStruct((L, RWKV_W), f32), jax.ShapeDtypeStruct((nc, npair, 2 * HEAD, 2 * HEAD), f32)],
        scratch_shapes=[pltpu.VMEM((npair, 2 * HEAD, 2 * HEAD), f32)],
        compiler_params=pltpu.CompilerParams(dimension_semantics=("arbitrary",), vmem_limit_bytes=VMEM_LIMIT),
    )(r, lw, k, v, a, b, *_wkv_consts())


def _wkv7_bwd(r, lw, k, v, a, b, ck, dy):
    L = r.shape[0]
    nc, npair = L // WKV_C, HEADS // 2

    def body(r_ref, lw_ref, k_ref, v_ref, a_ref, b_ref, ck_ref, dy_ref, tri_ref, bd_ref,
             dr_ref, dlw_ref, dk_ref, dv_ref, da_ref, db_ref, ds_ref):
        @pl.when(pl.program_id(0) == 0)
        def _():
            ds_ref[...] = jnp.zeros_like(ds_ref)

        tri, bd = tri_ref[...], bd_ref[...]
        ins = [[ck_ref[0, p] for p in range(npair)]] + [_pairs(x) for x in (r_ref, lw_ref, k_ref, v_ref, a_ref, b_ref)]
        _, vjp = jax.vjp(lambda *t: _wkv_chunk(*t, tri, bd), *ins)
        gs = vjp((_pairs(dy_ref), [ds_ref[p] for p in range(npair)]))
        for p in range(npair):
            ds_ref[p] = gs[0][p]
            for ref, gval in zip((dr_ref, dlw_ref, dk_ref, dv_ref, da_ref, db_ref), gs[1:]):
                ref[:, 2 * HEAD * p:2 * HEAD * (p + 1)] = gval[p]

    row = pl.BlockSpec((WKV_C, RWKV_W), lambda c: (nc - 1 - c, 0))
    sspec = pl.BlockSpec((1, npair, 2 * HEAD, 2 * HEAD), lambda c: (nc - 1 - c, 0, 0, 0))
    return pl.pallas_call(
        body, name="wkv7_bwd", grid=(nc,),
        in_specs=[row] * 6 + [sspec, row, pl.BlockSpec((WKV_C, WKV_C), lambda c: (0, 0)),
                              pl.BlockSpec((2 * HEAD, 2 * HEAD), lambda c: (0, 0))],
        out_specs=[row] * 6,
        out_shape=[jax.ShapeDtypeStruct((L, RWKV_W), f32)] * 6,
        scratch_shapes=[pltpu.VMEM((npair, 2 * HEAD, 2 * HEAD), f32)],
        compiler_params=pltpu.CompilerParams(dimension_semantics=("arbitrary",), vmem_limit_bytes=VMEM_LIMIT),
    )(r, lw, k, v, a, b, ck, dy, *_wkv_consts())


def _cmul(ar, ai, xr, xi):
    return ar * xr - ai * xi, ar * xi + ai * xr


def _s5_scan(x, abar, reverse, name):
    L = x.shape[0]
    nt = L // S5_T
    ng = S5_T // 8

    def body(x_ref, a_ref, o_ref, car_ref, pw_ref):
        @pl.when(pl.program_id(0) == 0)
        def _():
            car_ref[...] = jnp.zeros_like(car_ref)
            ar = jnp.broadcast_to(a_ref[:, :S5_N], (8, S5_N))
            ai = jnp.broadcast_to(a_ref[:, S5_N:], (8, S5_N))
            if reverse:
                ai = -ai
            row = lax.broadcasted_iota(jnp.int32, (8, S5_N), 0)
            pr, pi = ar, ai
            qr, qi = jnp.zeros((8, S5_N), f32), jnp.zeros((8, S5_N), f32)
            for e in range(1, 9):
                sel = (row == 8 - e) if reverse else (row == e - 1)
                qr, qi = jnp.where(sel, pr, qr), jnp.where(sel, pi, qi)
                if e in (1, 2, 4):
                    j = (1, 2, 4).index(e)
                    pw_ref[j, :, :S5_N] = pr
                    pw_ref[j, :, S5_N:] = pi
                pr, pi = _cmul(pr, pi, ar, ai)
            pw_ref[3, :, :S5_N] = qr
            pw_ref[3, :, S5_N:] = qi

        row = lax.broadcasted_iota(jnp.int32, (8, S5_N), 0)

        def group(gi, carry):
            g = (ng - 1 - gi) if reverse else gi
            t0 = pl.multiple_of(g * 8, 8)
            xr, xi = x_ref[pl.ds(t0, 8), :S5_N], x_ref[pl.ds(t0, 8), S5_N:]
            for j, d in enumerate((1, 2, 4)):
                if reverse:
                    sr = jnp.where(row < 8 - d, pltpu.roll(xr, 8 - d, axis=0), 0.0)
                    si = jnp.where(row < 8 - d, pltpu.roll(xi, 8 - d, axis=0), 0.0)
                else:
                    sr = jnp.where(row >= d, pltpu.roll(xr, d, axis=0), 0.0)
                    si = jnp.where(row >= d, pltpu.roll(xi, d, axis=0), 0.0)
                mr, mi = _cmul(pw_ref[j, :, :S5_N], pw_ref[j, :, S5_N:], sr, si)
                xr, xi = xr + mr, xi + mi
            cr, ci = carry
            mr, mi = _cmul(pw_ref[3, :, :S5_N], pw_ref[3, :, S5_N:], cr, ci)
            xr, xi = xr + mr, xi + mi
            o_ref[pl.ds(t0, 8), :S5_N] = xr
            o_ref[pl.ds(t0, 8), S5_N:] = xi
            e = 0 if reverse else 7
            return (jnp.broadcast_to(xr[e:e + 1, :], (8, S5_N)), jnp.broadcast_to(xi[e:e + 1, :], (8, S5_N)))

        cr, ci = lax.fori_loop(0, ng, group, (car_ref[:, :S5_N], car_ref[:, S5_N:]))
        car_ref[:, :S5_N] = cr
        car_ref[:, S5_N:] = ci

    imap = (lambda i: (nt - 1 - i, 0)) if reverse else (lambda i: (i, 0))
    return pl.pallas_call(
        body, name=name, grid=(nt,),
        in_specs=[pl.BlockSpec((S5_T, 2 * S5_N), imap), pl.BlockSpec((1, 2 * S5_N), lambda i: (0, 0))],
        out_specs=pl.BlockSpec((S5_T, 2 * S5_N), imap),
        out_shape=jax.ShapeDtypeStruct((L, 2 * S5_N), f32),
        scratch_shapes=[pltpu.VMEM((8, 2 * S5_N), f32), pltpu.VMEM((4, 8, 2 * S5_N), f32)],
        compiler_params=pltpu.CompilerParams(dimension_semantics=("arbitrary",), vmem_limit_bytes=VMEM_LIMIT),
    )(x, abar)


def _s5_disc_fwd(a_re, a_im, ls, b_re, b_im):
    def body(a_re_ref, a_im_ref, ls_ref, b_re_ref, b_im_ref, ar_ref, ai_ref, br_ref, bi_ref):
        outs = _s5_disc(a_re_ref[...], a_im_ref[...], ls_ref[...], b_re_ref[...], b_im_ref[...])
        for ref, v in zip((ar_ref, ai_ref, br_ref, bi_ref), outs):
            ref[...] = v

    c1, c16 = jax.ShapeDtypeStruct((S5_N, 1), f32), jax.ShapeDtypeStruct((S5_N, S5_C), f32)
    return pl.pallas_call(body, name="s5_disc", out_shape=[c1, c1, c16, c16])(a_re, a_im, ls, b_re, b_im)


def _s5_disc_bwd(a_re, a_im, ls, b_re, b_im, d_ar, d_ai, d_br, d_bi, seg):
    def body(a_re_ref, a_im_ref, ls_ref, b_re_ref, b_im_ref, g1, g2, g3, g4, seg_ref, o1, o2, o3, o4, o5):
        _, vjp = jax.vjp(_s5_disc, a_re_ref[...], a_im_ref[...], ls_ref[...], b_re_ref[...], b_im_ref[...])
        da_re, da_im, dls, db_re, db_im = vjp((g1[...], g2[...], g3[...], g4[...]))
        o1[...] = da_re
        o2[...] = da_im
        o3[...] = _dot32(seg_ref[...], dls)
        o4[...] = db_re
        o5[...] = db_im

    c1, c16 = jax.ShapeDtypeStruct((S5_N, 1), f32), jax.ShapeDtypeStruct((S5_N, S5_C), f32)
    return pl.pallas_call(body, name="s5_disc_bwd", out_shape=[c1, c1, jax.ShapeDtypeStruct((S5_G, 1), f32), c16, c16])(
        a_re, a_im, ls, b_re, b_im, d_ar, d_ai, d_br, d_bi, seg)


ANY = pl.BlockSpec(memory_space=pl.ANY)

GATHER = {'w_in': ((4352, 1024), 0), 'ffn_w_up': ((1024, 5632), 1), 'w_branch_rwkv': ((512, 1024), 1),
          'w_branch_s5': ((512, 1024), 1), 'w_out': ((1024, 1024), 0), 's5_w_glu': ((512, 512), 0),
          'ffn_w_down': ((2816, 1024), 0), 'rwkv_w2': ((64, 512), 1), 'rwkv_a2': ((64, 512), 1),
          'rwkv_g2': ((128, 512), 1), 'ffn_conv_w': ((8, 5632), 1)}
BIG = ['w_in', 'ffn_w_up', 'w_branch_rwkv', 'w_branch_s5', 'w_out', 's5_w_glu', 'ffn_w_down']
TINY = ['rwkv_w2', 'rwkv_a2', 'rwkv_g2', 'ffn_conv_w']
SMALL = [n for n in WEIGHTS if n not in GATHER]
SMALL_ROWS = 320
ADAM_ROWS = 256


def _mo(v, m):
    return v if isinstance(v, int) else pl.multiple_of(v, m)


def _slab(ref, name, j, h=None):
    (R, Cn), axis = GATHER[name]
    if axis == 0:
        rs = R // 4
        if h is None:
            return ref.at[pl.ds(_mo(j * rs, 16), rs), :]
        return ref.at[pl.ds(_mo(j * rs + h * (rs // 2), 8), rs // 2), :]
    cols = pl.ds(_mo(j * (Cn // 4), 128), Cn // 4)
    if h is None:
        return ref.at[:, cols]
    return ref.at[pl.ds(_mo(h * (R // 2), 8), R // 2), cols]


def _half_shape(name):
    (R, Cn), axis = GATHER[name]
    return (R // 8, Cn) if axis == 0 else (R // 2, Cn // 4)


def _chip_peers(px, py):
    return [((1 - px) if (k >> 1) else px, (1 - py) if (k & 1) else py) for k in (1, 2, 3)]


def _run_copies(copies):
    for cp in copies:
        cp.start()
    for cp in copies:
        cp.wait()


def _gather_weights(blocks):
    names = list(blocks)
    n = len(names)

    def body(*refs):
        ins, outs = refs[:n], refs[n:2 * n]
        ssem, rsem, lsem = refs[2 * n:]
        px, py, pc = _mesh_pos()
        me = 2 * px + py
        copies = []
        for i, nm in enumerate(names):
            if nm in BIG:
                hr = blocks[nm].shape[0] // 2
                src, dst = ins[i].at[pl.ds(pl.multiple_of(pc * hr, 16), hr), :], _slab(outs[i], nm, me, pc)
            else:
                src, dst = ins[i], _slab(outs[i], nm, me)
            copies.append(pltpu.make_async_copy(src, dst, lsem.at[i]))
            for k, (qx, qy) in enumerate(_chip_peers(px, py)):
                copies.append(pltpu.make_async_remote_copy(src, dst, ssem.at[3 * i + k], rsem.at[3 * i + k],
                                                           device_id=(qx, qy, pc), device_id_type=MESH))
        _run_copies(copies)

    outs = pl.pallas_call(
        body, name="gather_weights", in_specs=[ANY] * n, out_specs=[ANY] * n,
        out_shape=[jax.ShapeDtypeStruct(GATHER[nm][0], blocks[nm].dtype) for nm in names],
        scratch_shapes=[pltpu.SemaphoreType.DMA((3 * n,)), pltpu.SemaphoreType.DMA((3 * n,)), pltpu.SemaphoreType.DMA((n,))],
    )(*[blocks[nm] for nm in names])
    return dict(zip(names, outs))


def _gather_pair(full):
    n = len(BIG)

    def body(*refs):
        ins, outs = refs[:n], refs[n:2 * n]
        ssem, rsem = refs[2 * n:]
        px, py, pc = _mesh_pos()
        copies = []
        for i, nm in enumerate(BIG):
            for j in range(4):
                copies.append(pltpu.make_async_remote_copy(_slab(ins[i], nm, j, pc), _slab(outs[i], nm, j, pc),
                                                           ssem.at[4 * i + j], rsem.at[4 * i + j],
                                                           device_id=(px, py, 1 - pc), device_id_type=MESH))
        _run_copies(copies)

    outs = pl.pallas_call(
        body, name="gather_weights_pair", in_specs=[ANY] * n, out_specs=[ANY] * n,
        out_shape=[jax.ShapeDtypeStruct(full[nm].shape, full[nm].dtype) for nm in BIG],
        input_output_aliases={i: i for i in range(n)},
        scratch_shapes=[pltpu.SemaphoreType.DMA((4 * n,)), pltpu.SemaphoreType.DMA((4 * n,))],
    )(*[full[nm] for nm in BIG])
    return dict(zip(BIG, outs))


def _grads_to_sibling(G, small):
    n = len(BIG)

    def body(*refs):
        g_refs, small_ref = refs[:n], refs[n]
        o_refs, small_o = refs[n + 1:2 * n + 1], refs[2 * n + 1]
        ssem, rsem = refs[2 * n + 2:]
        px, py, pc = _mesh_pos()
        sib = (px, py, 1 - pc)
        copies = []
        for i, nm in enumerate(BIG):
            for j in range(4):
                copies.append(pltpu.make_async_remote_copy(_slab(g_refs[i], nm, j, 1 - pc), o_refs[i].at[j],
                                                           ssem.at[4 * i + j], rsem.at[4 * i + j],
                                                           device_id=sib, device_id_type=MESH))
        copies.append(pltpu.make_async_remote_copy(small_ref, small_o, ssem.at[4 * n], rsem.at[4 * n],
                                                   device_id=sib, device_id_type=MESH))
        _run_copies(copies)

    outs = pl.pallas_call(
        body, name="grads_to_sibling", in_specs=[ANY] * (n + 1), out_specs=[ANY] * (n + 1),
        out_shape=[jax.ShapeDtypeStruct((4,) + _half_shape(nm), f32) for nm in BIG] + [jax.ShapeDtypeStruct(small.shape, f32)],
        scratch_shapes=[pltpu.SemaphoreType.DMA((4 * n + 1,)), pltpu.SemaphoreType.DMA((4 * n + 1,))],
    )(*[G[nm] for nm in BIG], small)
    return dict(zip(BIG, outs[:n])), outs[n]


def _pair_add(G, recv, small, small_recv):
    n = len(BIG)
    cidx = lax.axis_index("c").astype(jnp.int32).reshape(1)

    def body(c_ref, *refs):
        ins, outs = refs[:2 * n + 2], refs[2 * n + 2:]
        for i in range(n):
            outs[i][...] = (ins[i][...] + ins[n + i][...]).astype(bf16)
        outs[n][...] = ins[2 * n][...] + ins[2 * n + 1][...]

    g_specs, r_specs = [], []
    for nm in BIG:
        hr, hc = _half_shape(nm)
        if GATHER[nm][1] == 0:
            g_specs.append(pl.BlockSpec((hr // 2, hc), lambda j, i, c: ((2 * j + c[0]) * 2 + i, 0)))
        else:
            g_specs.append(pl.BlockSpec((hr // 2, hc), lambda j, i, c: (2 * c[0] + i, j)))
        r_specs.append(pl.BlockSpec((1, hr // 2, hc), lambda j, i, c: (j, i, 0)))
    sm = pl.BlockSpec((SMALL_ROWS // 8, PACK_W), lambda j, i, c: (2 * j + i, 0))
    outs = pl.pallas_call(
        body, name="grads_pair_sum",
        grid_spec=pltpu.PrefetchScalarGridSpec(num_scalar_prefetch=1, grid=(4, 2), in_specs=g_specs + r_specs + [sm, sm],
                                               out_specs=r_specs + [sm]),
        out_shape=[jax.ShapeDtypeStruct((4,) + _half_shape(nm), bf16) for nm in BIG] + [jax.ShapeDtypeStruct(small.shape, f32)],
        compiler_params=pltpu.CompilerParams(vmem_limit_bytes=VMEM_LIMIT),
    )(cidx, *[G[nm] for nm in BIG], *[recv[nm] for nm in BIG], small, small_recv)
    return dict(zip(BIG, outs[:n])), outs[n]


def _grads_chip_exchange(chip_sum, small):
    n = len(BIG)

    def body(*refs):
        ins, outs = refs[:n + 1], refs[n + 1:2 * n + 2]
        ssem, rsem, lsem = refs[2 * n + 2:]
        px, py, pc = _mesh_pos()
        me = 2 * px + py
        copies = []
        for i in range(n + 1):
            pick = (lambda ref, j: ref.at[j]) if i < n else (lambda ref, j: ref)
            copies.append(pltpu.make_async_copy(pick(ins[i], me), outs[i].at[me], lsem.at[i]))
            for k, (qx, qy) in enumerate(_chip_peers(px, py)):
                copies.append(pltpu.make_async_remote_copy(pick(ins[i], 2 * qx + qy), outs[i].at[me],
                                                           ssem.at[3 * i + k], rsem.at[3 * i + k],
                                                           device_id=(qx, qy, pc), device_id_type=MESH))
        _run_copies(copies)

    outs = pl.pallas_call(
        body, name="grads_chip_exchange", in_specs=[ANY] * (n + 1), out_specs=[ANY] * (n + 1),
        out_shape=[jax.ShapeDtypeStruct(chip_sum[nm].shape, chip_sum[nm].dtype) for nm in BIG]
        + [jax.ShapeDtypeStruct((4,) + small.shape, f32)],
        scratch_shapes=[pltpu.SemaphoreType.DMA((3 * n + 3,)), pltpu.SemaphoreType.DMA((3 * n + 3,)),
                        pltpu.SemaphoreType.DMA((n + 1,))],
    )(*[chip_sum[nm] for nm in BIG], small)
    return dict(zip(BIG, outs[:n])), outs[n]


def _sum_slots(slots, small4):
    n = len(BIG)

    def body(*refs):
        for i in range(n + 1):
            x = refs[i]
            refs[n + 1 + i][...] = ((x[0].astype(f32) + x[1].astype(f32)) + x[2].astype(f32)) + x[3].astype(f32)

    specs_in, specs_out, shapes = [], [], []
    for nm in BIG:
        hr, hc = _half_shape(nm)
        specs_in.append(pl.BlockSpec((4, hr // 2, hc), lambda i: (0, i, 0)))
        specs_out.append(pl.BlockSpec((hr // 2, hc), lambda i: (i, 0)))
        shapes.append(jax.ShapeDtypeStruct((hr, hc), f32))
    specs_in.append(pl.BlockSpec((4, SMALL_ROWS // 2, PACK_W), lambda i: (0, i, 0)))
    specs_out.append(pl.BlockSpec((SMALL_ROWS // 2, PACK_W), lambda i: (i, 0)))
    shapes.append(jax.ShapeDtypeStruct((SMALL_ROWS, PACK_W), f32))
    outs = pl.pallas_call(
        body, name="grads_chip_sum", grid=(2,), in_specs=specs_in, out_specs=specs_out, out_shape=shapes,
        compiler_params=pltpu.CompilerParams(vmem_limit_bytes=VMEM_LIMIT),
    )(*[slots[nm] for nm in BIG], small4)
    return dict(zip(BIG, outs[:n])), outs[n]


def _halves_to_sibling(half):
    n = len(BIG)

    def body(*refs):
        ins, outs = refs[:n], refs[n:2 * n]
        ssem, rsem, lsem = refs[2 * n:]
        px, py, pc = _mesh_pos()
        copies = []
        for i in range(n):
            copies.append(pltpu.make_async_copy(ins[i], outs[i].at[pc], lsem.at[i]))
            copies.append(pltpu.make_async_remote_copy(ins[i], outs[i].at[pc], ssem.at[i], rsem.at[i],
                                                       device_id=(px, py, 1 - pc), device_id_type=MESH))
        _run_copies(copies)

    outs = pl.pallas_call(
        body, name="grads_halves_to_sibling", in_specs=[ANY] * n, out_specs=[ANY] * n,
        out_shape=[jax.ShapeDtypeStruct((2,) + _half_shape(nm), f32) for nm in BIG],
        scratch_shapes=[pltpu.SemaphoreType.DMA((n,)), pltpu.SemaphoreType.DMA((n,)), pltpu.SemaphoreType.DMA((n,))],
    )(*[half[nm] for nm in BIG])
    return dict(zip(BIG, outs))


def _flat_pad(v):
    v = v.reshape(-1)
    return jnp.pad(v, (0, _ceil_to(v.shape[0], PACK_W) - v.shape[0]))


def _pack_rows(parts, rows):
    flat = jnp.concatenate([_flat_pad(p) for p in parts])
    return jnp.pad(flat, (0, rows * PACK_W - flat.shape[0])).reshape(rows, PACK_W)


def _unpack_rows(buf, shapes):
    flat = buf.reshape(-1)
    out, off = [], 0
    for shp in shapes:
        n = 1
        for d in shp:
            n *= d
        out.append(flat[off:off + n].reshape(shp))
        off += _ceil_to(n, PACK_W)
    return out


def _adamw_math(w_, g_, m_, v_):
    m2 = ADAM_B1 * m_ + (1.0 - ADAM_B1) * g_
    v2 = ADAM_B2 * v_ + (1.0 - ADAM_B2) * (g_ * g_)
    m_hat = m2 / (1.0 - ADAM_B1 ** ADAM_STEP)
    v_hat = v2 / (1.0 - ADAM_B2 ** ADAM_STEP)
    return -ADAM_LR * (m_hat / (jnp.sqrt(v_hat) + ADAM_EPS) + ADAM_WD * w_), m2, v2


def _adamw(groups):
    ng = len(groups)

    def body(*refs):
        ins, outs = refs[:4 * ng], refs[4 * ng:]
        for i in range(ng):
            res = _adamw_math(*(r[...] for r in ins[4 * i:4 * i + 4]))
            for ref, val in zip(outs[3 * i:3 * i + 3], res):
                ref[...] = val

    in_specs, out_specs, out_shape = [], [], []
    for grp in groups:
        R, Cn = grp[0].shape
        spec = pl.BlockSpec((R // 8, Cn), lambda i: (i, 0))
        in_specs += [spec] * 4
        out_specs += [spec] * 3
        out_shape += [jax.ShapeDtypeStruct((R, Cn), f32)] * 3
    outs = pl.pallas_call(
        body, name="adamw", grid=(8,), in_specs=in_specs, out_specs=out_specs, out_shape=out_shape,
        compiler_params=pltpu.CompilerParams(vmem_limit_bytes=VMEM_LIMIT),
    )(*[a for grp in groups for a in grp])
    return [tuple(outs[3 * i:3 * i + 3]) for i in range(ng)]


def _forward_backward(x, tgt, W, S):
    L = x.shape[0]
    TM, TMW = 256, 128
    row = lambda c, dt=f32: (c, dt)
    hid = jnp.arange(RWKV_W) // HEAD
    E = (hid[:, None] == hid[None, :]).astype(f32)
    seg = (jnp.arange(S5_N)[None, :] // S5_P == jnp.arange(S5_G)[:, None]).astype(f32)

    w_in_t = W['w_in']
    w_p, w_u, w_g = w_in_t[:N_RWKV], w_in_t[N_RWKV:N_RWKV + S5_W], w_in_t[N_RWKV + S5_W:]
    zpad = jnp.zeros((64, RWKV_W), f32)
    w2p = jnp.concatenate([W['rwkv_w2'], zpad], axis=0)
    a2p = jnp.concatenate([zpad, W['rwkv_a2']], axis=0)
    g2 = W['rwkv_g2']
    prep_consts = [S['rwkv_shift_mu'], S['rwkv_w0'], S['rwkv_a0'], S['rwkv_k_k'], S['rwkv_k_a'], w2p, a2p, g2, E]
    out_consts = [S['rwkv_lnx_w'], S['rwkv_lnx_b'], S['rwkv_r_k'], E]
    cw, cb = W['ffn_conv_w'][:3], S['ffn_conv_b']

    a_re, a_im = S['s5_a_re'].reshape(S5_N, 1), S['s5_a_im'].reshape(S5_N, 1)
    ls = jnp.repeat(S['s5_log_step'].reshape(S5_G, 1), S5_P, axis=0)
    b_re, b_im = S['s5_b_re'].reshape(S5_N, S5_C), S['s5_b_im'].reshape(S5_N, S5_C)
    ar, ai, bbr, bbi = _s5_disc_fwd(a_re, a_im, ls, b_re, b_im)
    abar = jnp.concatenate([ar.reshape(1, S5_N), ai.reshape(1, S5_N)], axis=1)
    eye = jnp.eye(S5_G, dtype=f32)

    def bdiag_in(bb):
        t = bb.reshape(S5_G, S5_P, S5_C).transpose(0, 2, 1)
        return (t[:, :, None, :] * eye[:, None, :, None]).reshape(S5_W, S5_N)

    def bdiag_out(cc):
        t = cc.transpose(0, 2, 1)
        return (t[:, :, None, :] * eye[:, None, :, None]).reshape(S5_N, S5_W)

    def undiag_in(m):
        t = m.reshape(S5_G, S5_C, S5_G, S5_P)
        t = jnp.sum(t * eye[:, None, :, None], axis=2)
        return t.transpose(0, 2, 1).reshape(S5_N, S5_C)

    def undiag_out(m):
        t = m.reshape(S5_G, S5_P, S5_G, S5_C)
        t = jnp.sum(t * eye[:, None, :, None], axis=2)
        return t.transpose(0, 2, 1)

    bmat = jnp.concatenate([bdiag_in(bbr), bdiag_in(bbi)], axis=1).astype(bf16)
    cmat = jnp.concatenate([bdiag_out(S['s5_c_re'].reshape(S5_G, S5_C, S5_P)),
                            -bdiag_out(S['s5_c_im'].reshape(S5_G, S5_C, S5_P))], axis=0).astype(bf16)

    g1, g2n, g3, g4 = S['norm_mix_pre'], S['norm_mix_post'], S['norm_ffn_pre'], S['norm_ffn_post']
    (h1,) = _rowcall("norm_pre", lambda i, n, R, P, X, C: ((_rms(R[0], C[0]),), ()), L, TM, [x], [g1],
                     out_rows=[row(D_MODEL, bf16)])
    p = _mm(h1, w_p, 'nt', "mm_p")
    u = _mm(h1, w_u, 'nt', "mm_u")
    gp = _mm(h1, w_g, 'nt', "mm_g")

    def prep_fn(i, n, R, P, X, C):
        q = R[0] + (_shift_down(R[0], P[0], i, 1) - R[0]) * C[0]
        return _prep(q, *C[1:]), ()

    r, lw, k2, v, an, bv, g = _rowcall("rwkv_prep", prep_fn, L, TM, [p], prep_consts,
                                       out_rows=[row(RWKV_W)] * 7, prev=[0])
    y, ck = _wkv7_fwd(r, lw, k2, v, an, bv)
    (o_a,) = _rowcall("rwkv_out", lambda i, n, R, P, X, C: ((_rwkv_out(*R, *C),), ()), L, TM, [y, r, k2, v, g],
                      out_consts, out_rows=[row(RWKV_W, bf16)])
    o_r = _mm(o_a, W['w_branch_rwkv'], 'nn', "mm_br")

    bu = _mm(u, bmat, 'nn', "mm_bu")
    st = _s5_scan(bu, abar, False, "s5_scan")
    ysc = _mm(st, cmat, 'nn', "mm_cs")
    (yg,) = _rowcall("s5_mid", lambda i, n, R, P, X, C: ((_s5_mid(*R, *C),), ()), L, TM, [ysc, u], [S['s5_d']],
                     out_rows=[row(S5_W)])
    z2 = _mm(yg, W['s5_w_glu'], 'nn', "mm_glu")
    (o_b,) = _rowcall("s5_glu", lambda i, n, R, P, X, C: ((_s5_glu(*R, *C),), ()), L, TM, [yg, z2], [S['s5_b_glu']],
                      out_rows=[row(S5_W, bf16)])
    o_s = _mm(o_b, W['w_branch_s5'], 'nn', "mm_bs")

    (merged,) = _rowcall("merge", lambda i, n, R, P, X, C: ((_merge(*R, *C),), ()), L, TM, [gp, o_r, o_s],
                         [S['b_gate']], out_rows=[row(D_MODEL, bf16)])
    mixed = _mm(merged, W['w_out'], 'nn', "mm_out")

    def resid_fn(i, n, R, P, X, C):
        x1_ = R[0] + _rms(R[1], C[0])
        return (x1_, _rms(x1_, C[1])), ()

    x1, h2 = _rowcall("resid_norm", resid_fn, L, TM, [x, mixed], [g2n, g3], out_rows=[row(D_MODEL), row(D_MODEL, bf16)])

    z = _mm(h2, W['ffn_w_up'], 'nn', "mm_up")

    def conv(zt, zprev, i, cw_, cb_):
        z2s, z1s = _shift_down(zt, zprev, i, 2), _shift_down(zt, zprev, i, 1)
        return cb_ + cw_[0:1] * z2s + cw_[1:2] * z1s + cw_[2:3] * zt, z2s, z1s

    (act,) = _rowcall("conv_act", lambda i, n, R, P, X, C: ((_act(conv(R[0], P[0], i, C[0], C[1])[0]),), ()), L, TMW,
                      [z], [cw, cb], out_rows=[row(D_FF, bf16)], prev=[0])
    f = _mm(act, W['ffn_w_down'], 'nn', "mm_down")

    def final_fn(i, n, R, P, X, C):
        x1_, f_, t_ = R
        fn_, vjp = jax.vjp(_rms, f_, C[0])
        diff = x1_ + fn_ - t_
        loss = jnp.sum(diff * diff) * (0.5 / D_MODEL)
        dx2_ = diff * (1.0 / D_MODEL)
        df_, dg4_ = vjp(dx2_)
        return (df_, dx2_), (jnp.full((1, PACK_W), loss, f32), dg4_)

    df, dx2, loss, dg4 = _rowcall("loss_head", final_fn, L, TM, [x1, f, tgt], [g4],
                                  out_rows=[row(D_MODEL, bf16), row(D_MODEL)], out_accs=[(1, PACK_W), (1, D_MODEL)])
    G = {'norm_ffn_post': dg4}

    dact = _mm(df, W['ffn_w_down'], 'nt', "mm_down_dx")
    G['ffn_w_down'] = _mm(act, df, 'tn', "mm_down_dw")

    def conv_bwd_fn(i, n, R, P, X, C):
        zc, z2s, z1s = conv(R[0], P[0], i, C[0], C[1])
        _, vjp = jax.vjp(_act, zc)
        (dzc_,) = vjp(R[1])
        return (dzc_,), (_sum0(dzc_), _sum0(dzc_ * z2s), _sum0(dzc_ * z1s), _sum0(dzc_ * R[0]))

    wide = (1, 2 * D_FF)
    dzc, dcb, dcw0, dcw1, dcw2 = _rowcall("conv_act_bwd", conv_bwd_fn, L, TMW, [z, dact], [cw, cb],
                                          out_rows=[row(2 * D_FF)], out_accs=[wide] * 4, prev=[0])
    G['ffn_conv_b'] = dcb
    G['ffn_conv_w'] = jnp.concatenate([dcw0, dcw1, dcw2], axis=0)

    def conv_shift_fn(i, n, R, P, X, C):
        d = R[0]
        return (C[0][2:3] * d + C[0][1:2] * _shift_up(d, X[0], i, n, 1) + C[0][0:1] * _shift_up(d, X[0], i, n, 2),), ()

    (dz,) = _rowcall("conv_shift_bwd", conv_shift_fn, L, TMW, [dzc], [cw], out_rows=[row(2 * D_FF, bf16)], nxt=[0])
    dh2 = _mm(dz, W['ffn_w_up'], 'nt', "mm_up_dx")
    G['ffn_w_up'] = _mm(h2, dz, 'tn', "mm_up_dw")

    def norm2_bwd_fn(i, n, R, P, X, C):
        x1_, mixed_, dx2_, dh2_ = R
        _, vjp3 = jax.vjp(_rms, x1_, C[1])
        dx1a, dg3_ = vjp3(dh2_)
        dx1_ = dx2_ + dx1a
        _, vjp2 = jax.vjp(_rms, mixed_, C[0])
        dmixed_, dg2_ = vjp2(dx1_)
        return (dx1_, dmixed_), (dg2_, dg3_)

    dx1, dmixed, dg2n, dg3 = _rowcall("norm_mid_bwd", norm2_bwd_fn, L, TM, [x1, mixed, dx2, dh2], [g2n, g3],
                                      out_rows=[row(D_MODEL), row(D_MODEL, bf16)], out_accs=[(1, D_MODEL)] * 2)
    G['norm_mix_post'], G['norm_ffn_pre'] = dg2n, dg3

    dmerged = _mm(dmixed, W['w_out'], 'nt', "mm_out_dx")
    G['w_out'] = _mm(merged, dmixed, 'tn', "mm_out_dw")

    def merge_bwd_fn(i, n, R, P, X, C):
        _, vjp = jax.vjp(_merge, R[0], R[1], R[2], C[0])
        dgp_, do_r_, do_s_, dbg_ = vjp(R[3])
        return (dgp_, do_r_, do_s_), (dbg_,)

    dgp, do_r, do_s, G['b_gate'] = _rowcall("merge_bwd", merge_bwd_fn, L, TM, [gp, o_r, o_s, dmerged], [S['b_gate']],
                                            out_rows=[row(2 * D_MODEL, bf16), row(D_MODEL, bf16), row(D_MODEL, bf16)],
                                            out_accs=[(1, 2 * D_MODEL)])
    do_a = _mm(do_r, W['w_branch_rwkv'], 'nt', "mm_br_dx")
    G['w_branch_rwkv'] = _mm(o_a, do_r, 'tn', "mm_br_dw")
    do_b = _mm(do_s, W['w_branch_s5'], 'nt', "mm_bs_dx")
    G['w_branch_s5'] = _mm(o_b, do_s, 'tn', "mm_bs_dw")

    def glu_bwd_fn(i, n, R, P, X, C):
        _, vjp = jax.vjp(_s5_glu, R[0], R[1], C[0])
        dyg1_, dz2_, dbg_ = vjp(R[2])
        return (dyg1_, dz2_), (dbg_,)

    dyg1, dz2, G['s5_b_glu'] = _rowcall("s5_glu_bwd", glu_bwd_fn, L, TM, [yg, z2, do_b], [S['s5_b_glu']],
                                        out_rows=[row(S5_W), row(S5_W, bf16)], out_accs=[(1, S5_W)])
    dyg2 = _mm(dz2, W['s5_w_glu'], 'nt', "mm_glu_dx")
    G['s5_w_glu'] = _mm(yg, dz2, 'tn', "mm_glu_dw")

    def mid_bwd_fn(i, n, R, P, X, C):
        _, vjp = jax.vjp(_s5_mid, R[0], R[1], C[0])
        dysc_, du_, dd_ = vjp(R[2] + R[3])
        return (dysc_, du_), (dd_,)

    dysc, du1, G['s5_d'] = _rowcall("s5_mid_bwd", mid_bwd_fn, L, TM, [ysc, u, dyg1, dyg2], [S['s5_d']],
                                    out_rows=[row(S5_W, bf16), row(S5_W)], out_accs=[(1, S5_W)])
    dst = _mm(dysc, cmat, 'nt', "mm_cs_dx")
    dcmat = _mm(st, dysc, 'tn', "mm_cs_dw")
    lam = _s5_scan(dst, abar, True, "s5_scan_bwd")

    def s5_da_fn(i, n, R, P, X, C):
        lr, li = R[0][:, :S5_N], R[0][:, S5_N:]
        sp = _shift_down(R[1], P[0], i, 1)
        sr, si = sp[:, :S5_N], sp[:, S5_N:]
        return (), (jnp.concatenate([_sum0(lr * sr + li * si), _sum0(li * sr - lr * si)], axis=1),)

    (dabar,) = _rowcall("s5_da", s5_da_fn, L, TM, [lam, st], out_accs=[(1, 2 * S5_N)], prev=[1])
    du2 = _mm(lam, bmat, 'nt', "mm_bu_dx")
    dbmat = _mm(u, lam, 'tn', "mm_bu_dw")
    da_re, da_im, dls, db_re, db_im = _s5_disc_bwd(
        a_re, a_im, ls, b_re, b_im, dabar[:, :S5_N].reshape(S5_N, 1), dabar[:, S5_N:].reshape(S5_N, 1),
        undiag_in(dbmat[:, :S5_N]), undiag_in(dbmat[:, S5_N:]), seg)
    G['s5_a_re'], G['s5_a_im'], G['s5_log_step'] = da_re, da_im, dls
    G['s5_b_re'], G['s5_b_im'] = db_re, db_im
    G['s5_c_re'], G['s5_c_im'] = undiag_out(dcmat[:S5_N]), -undiag_out(dcmat[S5_N:])

    def out_bwd_fn(i, n, R, P, X, C):
        _, vjp = jax.vjp(_rwkv_out, *R[:5], *C)
        gs = vjp(R[5])
        return gs[:5], gs[5:8]

    dy, dr1, dk1, dv1, dg, dlw, dlb, drk = _rowcall("rwkv_out_bwd", out_bwd_fn, L, TM, [y, r, k2, v, g, do_a], out_consts,
                                                    out_rows=[row(RWKV_W)] * 5, out_accs=[(1, RWKV_W)] * 3)
    G['rwkv_lnx_w'], G['rwkv_lnx_b'], G['rwkv_r_k'] = dlw, dlb, drk
    dr2, dlwk, dk2b, dv2, dan, dbv = _wkv7_bwd(r, lw, k2, v, an, bv, ck, dy)

    def prep_bwd_fn(i, n, R, P, X, C):
        p_ = R[0]
        d1 = _shift_down(p_, P[0], i, 1) - p_
        q = p_ + d1 * C[0]
        _, vjp = jax.vjp(_prep, q, *C[1:])
        cots = (R[1] + R[2], R[3], R[4] + R[5], R[6] + R[7], R[8], R[9], R[10])
        gs = vjp(cots)
        return (gs[0],), (_sum0(gs[0] * d1),) + tuple(gs[1:8])

    small, lowr = (1, RWKV_W), (128, RWKV_W)
    dq, dmu, dw0, da0, dkk, dka, dw2p, da2p, dg2 = _rowcall(
        "rwkv_prep_bwd", prep_bwd_fn, L, TM, [p, dr1, dr2, dlwk, dk1, dk2b, dv1, dv2, dan, dbv, dg],
        prep_consts, out_rows=[row(N_RWKV)], out_accs=[(1, N_RWKV)] + [small] * 4 + [lowr] * 3, prev=[0])
    G['rwkv_shift_mu'], G['rwkv_w0'], G['rwkv_a0'], G['rwkv_k_k'], G['rwkv_k_a'] = dmu, dw0, da0, dkk, dka
    G['rwkv_w2'], G['rwkv_a2'], G['rwkv_g2'] = dw2p[:64], da2p[64:], dg2

    def shift_bwd_fn(i, n, R, P, X, C):
        dm = R[0] * C[0]
        return (R[0] - dm + _shift_up(dm, X[0] * C[0], i, n, 1),), ()

    (dp,) = _rowcall("shift_bwd", shift_bwd_fn, L, TM, [dq], [S['rwkv_shift_mu']], out_rows=[row(N_RWKV, bf16)], nxt=[0])

    (du,) = _rowcall("add_du", lambda i, n, R, P, X, C: ((R[0] + R[1],), ()), L, TM, [du1, du2], out_rows=[row(S5_W, bf16)])
    dproj = jnp.concatenate([dp, du, dgp], axis=1)
    dh1 = _mm(dproj, w_in_t, 'nn', "mm_in_dx")
    G['w_in'] = _mm(dproj, h1, 'tn', "mm_in_dw")

    def norm1_bwd_fn(i, n, R, P, X, C):
        _, vjp = jax.vjp(_rms, R[0], C[0])
        dxa, dg1_ = vjp(R[2])
        return (R[1] + dxa,), (dg1_,)

    dx, G['norm_mix_pre'] = _rowcall("norm_pre_bwd", norm1_bwd_fn, L, TM, [x, dx1, dh1], [g1],
                                     out_rows=[row(D_MODEL)], out_accs=[(1, D_MODEL)])
    return loss, dx, G


def kernel(x, norm_mix_pre, norm_mix_post, norm_ffn_pre, norm_ffn_post, w_in, b_gate, rwkv_shift_mu, rwkv_w0, rwkv_w2, rwkv_a0, rwkv_a2, rwkv_g2, rwkv_k_k, rwkv_k_a, rwkv_r_k, rwkv_lnx_w, rwkv_lnx_b, s5_a_re, s5_a_im, s5_b_re, s5_b_im, s5_c_re, s5_c_im, s5_d, s5_log_step, s5_w_glu, s5_b_glu, w_branch_rwkv, w_branch_s5, w_out, ffn_w_up, ffn_conv_w, ffn_conv_b, ffn_w_down, loss_target, m_norm_mix_pre, m_norm_mix_post, m_norm_ffn_pre, m_norm_ffn_post, m_w_in, m_b_gate, m_rwkv_shift_mu, m_rwkv_w0, m_rwkv_w2, m_rwkv_a0, m_rwkv_a2, m_rwkv_g2, m_rwkv_k_k, m_rwkv_k_a, m_rwkv_r_k, m_rwkv_lnx_w, m_rwkv_lnx_b, m_s5_a_re, m_s5_a_im, m_s5_b_re, m_s5_b_im, m_s5_c_re, m_s5_c_im, m_s5_d, m_s5_log_step, m_s5_w_glu, m_s5_b_glu, m_w_branch_rwkv, m_w_branch_s5, m_w_out, m_ffn_w_up, m_ffn_conv_w, m_ffn_conv_b, m_ffn_w_down, v_norm_mix_pre, v_norm_mix_post, v_norm_ffn_pre, v_norm_ffn_post, v_w_in, v_b_gate, v_rwkv_shift_mu, v_rwkv_w0, v_rwkv_w2, v_rwkv_a0, v_rwkv_a2, v_rwkv_g2, v_rwkv_k_k, v_rwkv_k_a, v_rwkv_r_k, v_rwkv_lnx_w, v_rwkv_lnx_b, v_s5_a_re, v_s5_a_im, v_s5_b_re, v_s5_b_im, v_s5_c_re, v_s5_c_im, v_s5_d, v_s5_log_step, v_s5_w_glu, v_s5_b_glu, v_w_branch_rwkv, v_w_branch_s5, v_w_out, v_ffn_w_up, v_ffn_conv_w, v_ffn_conv_b, v_ffn_w_down):
    A = dict(locals())
    me = 2 * lax.axis_index("x") + lax.axis_index("y")
    blk = lambda n: A[n][0]

    mine = {n: (blk(n).T if n == 'w_in' else blk(n)).astype(bf16) for n in BIG}
    mine.update({n: blk(n) for n in TINY})
    mine['ffn_conv_w'] = jnp.pad(blk('ffn_conv_w'), ((0, 5), (0, 0)))
    W = _gather_weights(mine)
    W.update(_gather_pair(W))
    S = {n: A[n].reshape(1, -1) for n in SMALL}

    loss, dx, G = _forward_backward(x[0], loss_target[0], W, S)

    tiny_shapes = [G[n].shape for n in TINY]
    small_buf = _pack_rows([G[n] for n in SMALL] + [G[n] for n in TINY] + [loss], SMALL_ROWS)
    recv, small_recv = _grads_to_sibling(G, small_buf)
    chip_sum, small_sum = _pair_add(G, recv, small_buf, small_recv)
    slots, small4 = _grads_chip_exchange(chip_sum, small_sum)
    half, small_tot = _sum_slots(slots, small4)
    both = _halves_to_sibling(half)
    grad = {}
    for n in BIG:
        (R, Cn), axis = GATHER[n]
        grad[n] = both[n].reshape(R // 4, Cn) if axis == 0 else both[n].reshape(R, Cn // 4)
    grad['w_in'] = grad['w_in'].T
    vals = _unpack_rows(small_tot, [A[n].shape for n in SMALL] + tiny_shapes + [(1, PACK_W)])
    grad.update(zip(SMALL, vals))
    for n, full in zip(TINY, vals[len(SMALL):]):
        cs = A[n].shape[2]
        grad[n] = lax.dynamic_slice_in_dim(full, me * cs, cs, axis=1)
    loss_out = vals[-1][0, 0]

    packed = SMALL + TINY
    groups = [(blk(n), grad[n], blk('m_' + n), blk('v_' + n)) for n in BIG]
    groups.append(tuple(_pack_rows([src(n) for n in packed], ADAM_ROWS)
                        for src in (lambda n: A[n], lambda n: grad[n], lambda n: A['m_' + n], lambda n: A['v_' + n])))
    res = _adamw(groups)
    outs = [dict(), dict(), dict()]
    for n, r3 in zip(BIG, res[:-1]):
        for d, val in zip(outs, r3):
            d[n] = val
    for d, buf in zip(outs, res[-1]):
        d.update(zip(packed, _unpack_rows(buf, [A[n].shape for n in packed])))
    full = lambda d: [d[n].reshape(A[n].shape) for n in WEIGHTS]
    return (loss_out, dx[None], *full(grad), *full(outs[0]), *full(outs[1]), *full(outs[2]))
```

```python
import functools

import jax
import jax.numpy as jnp
from jax import lax
from jax.experimental import pallas as pl
from jax.experimental.pallas import tpu as pltpu

f32, bf16 = jnp.float32, jnp.bfloat16
MESH = pl.DeviceIdType.MESH

D_MODEL = 1024
RWKV_W = 512
HEADS, HEAD = 8, 64
N_RWKV = 1792
S5_W = 512
S5_G, S5_P, S5_C = 32, 64, 16
S5_N = S5_G * S5_P
D_FF = 2816
NORM_EPS = 1e-6
LNX_EPS = 64e-5
ADAM_LR, ADAM_B1, ADAM_B2, ADAM_EPS, ADAM_WD, ADAM_STEP = 0.001, 0.9, 0.999, 1e-08, 0.01, 10

VMEM_LIMIT = 48 * 1024 * 1024
PACK_W = 1024
WKV_C = 64
S5_T = 256

WEIGHTS = ['norm_mix_pre', 'norm_mix_post', 'norm_ffn_pre', 'norm_ffn_post', 'w_in', 'b_gate', 'rwkv_shift_mu',
           'rwkv_w0', 'rwkv_w2', 'rwkv_a0', 'rwkv_a2', 'rwkv_g2', 'rwkv_k_k', 'rwkv_k_a', 'rwkv_r_k', 'rwkv_lnx_w',
           'rwkv_lnx_b', 's5_a_re', 's5_a_im', 's5_b_re', 's5_b_im', 's5_c_re', 's5_c_im', 's5_d', 's5_log_step',
           's5_w_glu', 's5_b_glu', 'w_branch_rwkv', 'w_branch_s5', 'w_out', 'ffn_w_up', 'ffn_conv_w', 'ffn_conv_b',
           'ffn_w_down']


def _ceil_to(n, m):
    return -(-n // m) * m


def _mesh_pos():
    return lax.axis_index("x"), lax.axis_index("y"), lax.axis_index("c")


def _pick(d, cap=4096):
    for c in (1024, 1408, 2176, 896, 512, 256, 128):
        if c <= cap and d % c == 0:
            return c
    raise ValueError(d)


def _mm(a, b, mode, name, out_dtype=f32):
    if mode == 'tn':
        (K, M), (K2, N) = a.shape, b.shape
    elif mode == 'nt':
        (M, K), (N, K2) = a.shape, b.shape
    else:
        (M, K), (K2, N) = a.shape, b.shape
    assert K == K2, (name, a.shape, b.shape)
    if mode == 'tn':
        tm = _pick(M, 2176)
        tn = _pick(N, 512 if tm > 1408 else (1024 if tm > 1024 else 1408))
        tk = _pick(K, 512)
    else:
        tm, tn, tk = _pick(M, 512), _pick(N), _pick(K)
    nk = K // tk
    dims = {'nn': ((1,), (0,)), 'nt': ((1,), (1,)), 'tn': ((0,), (0,))}[mode]

    def body(a_ref, b_ref, o_ref, acc_ref):
        k = pl.program_id(2)

        @pl.when(k == 0)
        def _():
            acc_ref[...] = jnp.zeros_like(acc_ref)

        acc_ref[...] += lax.dot_general(a_ref[...].astype(bf16), b_ref[...].astype(bf16), (dims, ((), ())),
                                        preferred_element_type=f32)

        @pl.when(k == nk - 1)
        def _():
            o_ref[...] = acc_ref[...].astype(o_ref.dtype)

    a_spec = pl.BlockSpec((tk, tm), lambda i, j, k: (k, i)) if mode == 'tn' else pl.BlockSpec((tm, tk), lambda i, j, k: (i, k))
    b_spec = pl.BlockSpec((tn, tk), lambda i, j, k: (j, k)) if mode == 'nt' else pl.BlockSpec((tk, tn), lambda i, j, k: (k, j))
    return pl.pallas_call(
        body, name=name, grid=(M // tm, N // tn, nk),
        in_specs=[a_spec, b_spec], out_specs=pl.BlockSpec((tm, tn), lambda i, j, k: (i, j)),
        out_shape=jax.ShapeDtypeStruct((M, N), out_dtype),
        scratch_shapes=[pltpu.VMEM((tm, tn), f32)],
        compiler_params=pltpu.CompilerParams(dimension_semantics=("parallel", "parallel", "arbitrary"),
                                             vmem_limit_bytes=VMEM_LIMIT),
    )(a, b)


def _rowcall(name, fn, L, tm, rows, consts=(), out_rows=(), out_accs=(), prev=(), nxt=()):
    nsteps = L // tm
    nb8 = tm // 8
    last8 = L // 8 - 1
    n_r, n_p, n_x, n_c, n_or = len(rows), len(prev), len(nxt), len(consts), len(out_rows)

    def body(*refs):
        i = pl.program_id(0)
        vals = [r[...] for r in refs[:n_r + n_p + n_x + n_c]]
        R, P = vals[:n_r], vals[n_r:n_r + n_p]
        X, C = vals[n_r + n_p:n_r + n_p + n_x], vals[n_r + n_p + n_x:]
        o_refs = refs[n_r + n_p + n_x + n_c:]
        outs_r, outs_a = fn(i, nsteps, R, P, X, C)
        for ref, v in zip(o_refs[:n_or], outs_r, strict=True):
            ref[...] = v.astype(ref.dtype)
        if out_accs:
            @pl.when(i == 0)
            def _():
                for ref in o_refs[n_or:]:
                    ref[...] = jnp.zeros_like(ref)

            for ref, v in zip(o_refs[n_or:], outs_a, strict=True):
                ref[...] += v

    def const_spec(c):
        nd = c.ndim
        return pl.BlockSpec(c.shape, lambda i: (0,) * nd)

    in_specs = ([pl.BlockSpec((tm, a.shape[1]), lambda i: (i, 0)) for a in rows]
                + [pl.BlockSpec((8, rows[j].shape[1]), lambda i: (jnp.maximum(i * nb8 - 1, 0), 0)) for j in prev]
                + [pl.BlockSpec((8, rows[j].shape[1]), lambda i: (jnp.minimum((i + 1) * nb8, last8), 0)) for j in nxt]
                + [const_spec(c) for c in consts])
    out_specs = ([pl.BlockSpec((tm, c), lambda i: (i, 0)) for c, _ in out_rows]
                 + [pl.BlockSpec(s, lambda i: (0, 0)) for s in out_accs])
    out_shape = ([jax.ShapeDtypeStruct((L, c), dt) for c, dt in out_rows]
                 + [jax.ShapeDtypeStruct(s, f32) for s in out_accs])
    args = list(rows) + [rows[j] for j in prev] + [rows[j] for j in nxt] + list(consts)
    return pl.pallas_call(
        body, name=name, grid=(nsteps,), in_specs=in_specs, out_specs=out_specs, out_shape=out_shape,
        compiler_params=pltpu.CompilerParams(dimension_semantics=("arbitrary",), vmem_limit_bytes=VMEM_LIMIT),
    )(*args)


def _shift_down(x, prev8, i, k):
    rolled = pltpu.roll(x, k, axis=0)
    pfix = jnp.where(i > 0, pltpu.roll(prev8, k, axis=0), 0.0)
    row8 = lax.broadcasted_iota(jnp.int32, pfix.shape, 0)
    top = jnp.where(row8 < k, pfix, rolled[:8])
    return jnp.concatenate([top, rolled[8:]], axis=0)


def _shift_up(x, next8, i, nsteps, k):
    tm = x.shape[0]
    rolled = pltpu.roll(x, tm - k, axis=0)
    nfix = jnp.where(i < nsteps - 1, pltpu.roll(next8, 8 - k, axis=0), 0.0)
    row8 = lax.broadcasted_iota(jnp.int32, nfix.shape, 0)
    bot = jnp.where(row8 >= 8 - k, nfix, rolled[tm - 8:])
    return jnp.concatenate([rolled[:tm - 8], bot], axis=0)


def _sum0(x):
    return jnp.sum(x, axis=0, keepdims=True)


def _rms(x, g):
    return x * lax.rsqrt(jnp.mean(x * x, axis=-1, keepdims=True) + NORM_EPS) * g


def _softplus(x):
    return jnp.maximum(x, 0.0) + jnp.log(1.0 + jnp.exp(-jnp.abs(x)))


def _gelu(x):
    return 0.5 * x * (1.0 + jnp.tanh(0.7978845608028654 * (x + 0.044715 * x * x * x)))


def _dot32(a, b):
    return jnp.dot(a, b, preferred_element_type=f32, precision=lax.Precision.HIGHEST)


def _prep(q, w0, a0, k_k, k_a, w2p, a2p, g2, E):
    r, k, v = q[:, 0:512], q[:, 512:1024], q[:, 1024:1536]
    wa, gd = q[:, 1536:1664], q[:, 1664:1792]
    wlog = -_softplus(-(w0 + _dot32(jnp.tanh(wa), w2p))) - 0.5
    lw = -jnp.exp(wlog)
    a = jax.nn.sigmoid(a0 + _dot32(wa, a2p))
    g = _dot32(jax.nn.sigmoid(gd), g2)
    kk = k * k_k
    kkn = kk / jnp.maximum(jnp.sqrt(_dot32(kk * kk, E)), 1e-12)
    k2 = k * (1.0 + (a - 1.0) * k_a)
    return r, lw, k2, v, -kkn, kkn * a, g


def _rwkv_out(y, r, k2, v, g, lnx_w, lnx_b, r_k, E):
    mean = _dot32(y, E) * (1.0 / HEAD)
    yc = y - mean
    var = _dot32(yc * yc, E) * (1.0 / HEAD)
    yn = yc * lax.rsqrt(var + LNX_EPS) * lnx_w + lnx_b
    bonus = _dot32(r * k2 * r_k, E) * v
    return (yn + bonus) * g


def _s5_mid(ysc, u, d):
    return _gelu(ysc + d * u)


def _s5_glu(yg, z2, b_glu):
    return yg * jax.nn.sigmoid(z2 + b_glu)


def _merge(gp, o_r, o_s, b_gate):
    gates = jax.nn.sigmoid(gp + b_gate)
    return gates[:, :D_MODEL] * o_r + gates[:, D_MODEL:] * o_s


def _act(zc):
    return _gelu(zc[:, :D_FF]) * zc[:, D_FF:]


def _s5_disc(a_re, a_im, ls, b_re, b_im):
    dt = jnp.exp(ls)
    er = jnp.exp(a_re * dt)
    ar, ai = er * jnp.cos(a_im * dt), er * jnp.sin(a_im * dt)
    x, y = ar - 1.0, ai
    den = a_re * a_re + a_im * a_im
    fr, fi = (x * a_re + y * a_im) / den, (y * a_re - x * a_im) / den
    return ar, ai, fr * b_re - fi * b_im, fr * b_im + fi * b_re


_DIMS = {'nn': ((1,), (0,)), 'nt': ((1,), (1,)), 'tn': ((0,), (0,))}


def _raw_bdot(a, b, mode):
    return lax.dot_general(a.astype(bf16), b.astype(bf16), (_DIMS[mode], ((), ())), preferred_element_type=f32)


@functools.partial(jax.custom_vjp, nondiff_argnums=(2,))
def _bdot(a, b, mode):
    return _raw_bdot(a, b, mode)


def _bdot_fwd(a, b, mode):
    return _raw_bdot(a, b, mode), (a, b)


def _bdot_bwd(mode, res, g):
    a, b = res
    if mode == 'nn':
        return _raw_bdot(g, b, 'nt'), _raw_bdot(a, g, 'tn')
    if mode == 'nt':
        return _raw_bdot(g, b, 'nn'), _raw_bdot(g, a, 'tn')
    return _raw_bdot(b, g, 'nt'), _raw_bdot(a, g, 'nn')


_bdot.defvjp(_bdot_fwd, _bdot_bwd)


def _wkv_chunk(S0, r, lw, k, v, a, b, tri, bd):
    C = r[0].shape[0]
    P = range(len(r))
    lane = lax.broadcasted_iota(jnp.int32, (1, 2 * HEAD), 1)
    halves = [(lane < HEAD).astype(f32), (lane >= HEAD).astype(f32)]
    eye = (lax.broadcasted_iota(jnp.int32, (C, C), 0) == lax.broadcasted_iota(jnp.int32, (C, C), 1)).astype(f32)
    sl = tri - eye
    cum = [_dot32(tri, lw[p]) for p in P]
    g = [jnp.exp(cum[p]) for p in P]
    gi = [jnp.exp(-cum[p]) for p in P]
    at = [a[p] * jnp.exp(cum[p] - lw[p]) for p in P]
    rt = [r[p] * g[p] for p in P]
    kb = [k[p] * gi[p] for p in P]
    bb = [b[p] * gi[p] for p in P]
    PE = [(p, e) for p in P for e in range(2)]
    atm = {pe: at[pe[0]] * halves[pe[1]] for pe in PE}
    rtm = {pe: rt[pe[0]] * halves[pe[1]] for pe in PE}
    aab = {pe: _bdot(atm[pe], bb[pe[0]], 'nt') * sl for pe in PE}
    aak = {pe: _bdot(atm[pe], kb[pe[0]], 'nt') * sl for pe in PE}
    rk = {pe: _bdot(rtm[pe], kb[pe[0]], 'nt') * tri for pe in PE}
    rb = {pe: _bdot(rtm[pe], bb[pe[0]], 'nt') * tri for pe in PE}
    rhs = [_bdot(at[p], S0[p], 'nt') + sum(halves[e] * _bdot(aak[(p, e)], v[p], 'nn') for e in range(2)) for p in P]
    y0 = [_bdot(rt[p], S0[p], 'nt') + sum(halves[e] * _bdot(rk[(p, e)], v[p], 'nn') for e in range(2)) for p in P]
    x = {pe: eye + aab[pe] for pe in PE}
    pw = aab
    n = 1
    while 2 * n < C:
        pw = {pe: _bdot(pw[pe], pw[pe], 'nn') for pe in PE}
        x = {pe: x[pe] + _bdot(x[pe], pw[pe], 'nn') for pe in PE}
        n *= 2
    u = [sum(halves[e] * _bdot(x[(p, e)], rhs[p], 'nn') for e in range(2)) for p in P]
    y = [y0[p] + sum(halves[e] * _bdot(rb[(p, e)], u[p], 'nn') for e in range(2)) for p in P]
    S1 = [g[p][C - 1:C, :] * (S0[p] + bd * (_bdot(v[p], kb[p], 'tn') + _bdot(u[p], bb[p], 'tn'))) for p in P]
    return y, S1


def _pairs(x):
    return [x[:, 2 * HEAD * p:2 * HEAD * (p + 1)] for p in range(HEADS // 2)]


def _wkv_consts():
    tri = jnp.tril(jnp.ones((WKV_C, WKV_C), f32))
    hid = jnp.arange(2 * HEAD) // HEAD
    return tri, (hid[:, None] == hid[None, :]).astype(f32)


def _wkv7_fwd(r, lw, k, v, a, b):
    L = r.shape[0]
    nc, npair = L // WKV_C, HEADS // 2

    def body(r_ref, lw_ref, k_ref, v_ref, a_ref, b_ref, tri_ref, bd_ref, y_ref, ck_ref, s_ref):
        @pl.when(pl.program_id(0) == 0)
        def _():
            s_ref[...] = jnp.zeros_like(s_ref)

        s0 = [s_ref[p] for p in range(npair)]
        for p in range(npair):
            ck_ref[0, p] = s0[p]
        y, s1 = _wkv_chunk(s0, *(_pairs(x) for x in (r_ref, lw_ref, k_ref, v_ref, a_ref, b_ref)), tri_ref[...], bd_ref[...])
        for p in range(npair):
            y_ref[:, 2 * HEAD * p:2 * HEAD * (p + 1)] = y[p]
            s_ref[p] = s1[p]

    row = pl.BlockSpec((WKV_C, RWKV_W), lambda c: (c, 0))
    sspec = pl.BlockSpec((1, npair, 2 * HEAD, 2 * HEAD), lambda c: (c, 0, 0, 0))
    return pl.pallas_call(
        body, name="wkv7_fwd", grid=(nc,),
        in_specs=[row] * 6 + [pl.BlockSpec((WKV_C, WKV_C), lambda c: (0, 0)), pl.BlockSpec((2 * HEAD, 2 * HEAD), lambda c: (0, 0))],
        out_specs=[row, sspec],
        out_shape=[jax.ShapeDtypeStruct((L, RWKV_W), f32), jax.ShapeDtypeStruct((nc, npair, 2 * HEAD, 2 * HEAD), f32)],
        scratch_shapes=[pltpu.VMEM((npair, 2 * HEAD, 2 * HEAD), f32)],
        compiler_params=pltpu.CompilerParams(dimension_semantics=("arbitrary",), vmem_limit_bytes=VMEM_LIMIT),
    )(r, lw, k, v, a, b, *_wkv_consts())


def _wkv7_bwd(r, lw, k, v, a, b, ck, dy):
    L = r.shape[0]
    nc, npair = L // WKV_C, HEADS // 2

    def body(r_ref, lw_ref, k_ref, v_ref, a_ref, b_ref, ck_ref, dy_ref, tri_ref, bd_ref,
             dr_ref, dlw_ref, dk_ref, dv_ref, da_ref, db_ref, ds_ref):
        @pl.when(pl.program_id(0) == 0)
        def _():
            ds_ref[...] = jnp.zeros_like(ds_ref)

        tri, bd = tri_ref[...], bd_ref[...]
        ins = [[ck_ref[0, p] for p in range(npair)]] + [_pairs(x) for x in (r_ref, lw_ref, k_ref, v_ref, a_ref, b_ref)]
        _, vjp = jax.vjp(lambda *t: _wkv_chunk(*t, tri, bd), *ins)
        gs = vjp((_pairs(dy_ref), [ds_ref[p] for p in range(npair)]))
        for p in range(npair):
            ds_ref[p] = gs[0][p]
            for ref, gval in zip((dr_ref, dlw_ref, dk_ref, dv_ref, da_ref, db_ref), gs[1:]):
                ref[:, 2 * HEAD * p:2 * HEAD * (p + 1)] = gval[p]

    row = pl.BlockSpec((WKV_C, RWKV_W), lambda c: (nc - 1 - c, 0))
    sspec = pl.BlockSpec((1, npair, 2 * HEAD, 2 * HEAD), lambda c: (nc - 1 - c, 0, 0, 0))
    return pl.pallas_call(
        body, name="wkv7_bwd", grid=(nc,),
        in_specs=[row] * 6 + [sspec, row, pl.BlockSpec((WKV_C, WKV_C), lambda c: (0, 0)),
                              pl.BlockSpec((2 * HEAD, 2 * HEAD), lambda c: (0, 0))],
        out_specs=[row] * 6,
        out_shape=[jax.ShapeDtypeStruct((L, RWKV_W), f32)] * 6,
        scratch_shapes=[pltpu.VMEM((npair, 2 * HEAD, 2 * HEAD), f32)],
        compiler_params=pltpu.CompilerParams(dimension_semantics=("arbitrary",), vmem_limit_bytes=VMEM_LIMIT),
    )(r, lw, k, v, a, b, ck, dy, *_wkv_consts())


def _cmul(ar, ai, xr, xi):
    return ar * xr - ai * xi, ar * xi + ai * xr


def _s5_scan(x, abar, reverse, name):
    L = x.shape[0]
    nt = L // S5_T
    ng = S5_T // 8

    def body(x_ref, a_ref, o_ref, car_ref, pw_ref):
        @pl.when(pl.program_id(0) == 0)
        def _():
            car_ref[...] = jnp.zeros_like(car_ref)
            ar = jnp.broadcast_to(a_ref[:, :S5_N], (8, S5_N))
            ai = jnp.broadcast_to(a_ref[:, S5_N:], (8, S5_N))
            if reverse:
                ai = -ai
            row = lax.broadcasted_iota(jnp.int32, (8, S5_N), 0)
            pr, pi = ar, ai
            qr, qi = jnp.zeros((8, S5_N), f32), jnp.zeros((8, S5_N), f32)
            for e in range(1, 9):
                sel = (row == 8 - e) if reverse else (row == e - 1)
                qr, qi = jnp.where(sel, pr, qr), jnp.where(sel, pi, qi)
                if e in (1, 2, 4):
                    j = (1, 2, 4).index(e)
                    pw_ref[j, :, :S5_N] = pr
                    pw_ref[j, :, S5_N:] = pi
                pr, pi = _cmul(pr, pi, ar, ai)
            pw_ref[3, :, :S5_N] = qr
            pw_ref[3, :, S5_N:] = qi

        row = lax.broadcasted_iota(jnp.int32, (8, S5_N), 0)

        def group(gi, carry):
            g = (ng - 1 - gi) if reverse else gi
            t0 = pl.multiple_of(g * 8, 8)
            xr, xi = x_ref[pl.ds(t0, 8), :S5_N], x_ref[pl.ds(t0, 8), S5_N:]
            for j, d in enumerate((1, 2, 4)):
                if reverse:
                    sr = jnp.where(row < 8 - d, pltpu.roll(xr, 8 - d, axis=0), 0.0)
                    si = jnp.where(row < 8 - d, pltpu.roll(xi, 8 - d, axis=0), 0.0)
                else:
                    sr = jnp.where(row >= d, pltpu.roll(xr, d, axis=0), 0.0)
                    si = jnp.where(row >= d, pltpu.roll(xi, d, axis=0), 0.0)
                mr, mi = _cmul(pw_ref[j, :, :S5_N], pw_ref[j, :, S5_N:], sr, si)
                xr, xi = xr + mr, xi + mi
            cr, ci = carry
            mr, mi = _cmul(pw_ref[3, :, :S5_N], pw_ref[3, :, S5_N:], cr, ci)
            xr, xi = xr + mr, xi + mi
            o_ref[pl.ds(t0, 8), :S5_N] = xr
            o_ref[pl.ds(t0, 8), S5_N:] = xi
            e = 0 if reverse else 7
            return (jnp.broadcast_to(xr[e:e + 1, :], (8, S5_N)), jnp.broadcast_to(xi[e:e + 1, :], (8, S5_N)))

        cr, ci = lax.fori_loop(0, ng, group, (car_ref[:, :S5_N], car_ref[:, S5_N:]))
        car_ref[:, :S5_N] = cr
        car_ref[:, S5_N:] = ci

    imap = (lambda i: (nt - 1 - i, 0)) if reverse else (lambda i: (i, 0))
    return pl.pallas_call(
        body, name=name, grid=(nt,),
        in_specs=[pl.BlockSpec((S5_T, 2 * S5_N), imap), pl.BlockSpec((1, 2 * S5_N), lambda i: (0, 0))],
        out_specs=pl.BlockSpec((S5_T, 2 * S5_N), imap),
        out_shape=jax.ShapeDtypeStruct((L, 2 * S5_N), f32),
        scratch_shapes=[pltpu.VMEM((8, 2 * S5_N), f32), pltpu.VMEM((4, 8, 2 * S5_N), f32)],
        compiler_params=pltpu.CompilerParams(dimension_semantics=("arbitrary",), vmem_limit_bytes=VMEM_LIMIT),
    )(x, abar)


def _s5_disc_fwd(a_re, a_im, ls, b_re, b_im):
    def body(a_re_ref, a_im_ref, ls_ref, b_re_ref, b_im_ref, ar_ref, ai_ref, br_ref, bi_ref):
        outs = _s5_disc(a_re_ref[...], a_im_ref[...], ls_ref[...], b_re_ref[...], b_im_ref[...])
        for ref, v in zip((ar_ref, ai_ref, br_ref, bi_ref), outs):
            ref[...] = v

    c1, c16 = jax.ShapeDtypeStruct((S5_N, 1), f32), jax.ShapeDtypeStruct((S5_N, S5_C), f32)
    return pl.pallas_call(body, name="s5_disc", out_shape=[c1, c1, c16, c16])(a_re, a_im, ls, b_re, b_im)


def _s5_disc_bwd(a_re, a_im, ls, b_re, b_im, d_ar, d_ai, d_br, d_bi, seg):
    def body(a_re_ref, a_im_ref, ls_ref, b_re_ref, b_im_ref, g1, g2, g3, g4, seg_ref, o1, o2, o3, o4, o5):
        _, vjp = jax.vjp(_s5_disc, a_re_ref[...], a_im_ref[...], ls_ref[...], b_re_ref[...], b_im_ref[...])
        da_re, da_im, dls, db_re, db_im = vjp((g1[...], g2[...], g3[...], g4[...]))
        o1[...] = da_re
        o2[...] = da_im
        o3[...] = _dot32(seg_ref[...], dls)
        o4[...] = db_re
        o5[...] = db_im

    c1, c16 = jax.ShapeDtypeStruct((S5_N, 1), f32), jax.ShapeDtypeStruct((S5_N, S5_C), f32)
    return pl.pallas_call(body, name="s5_disc_bwd", out_shape=[c1, c1, jax.ShapeDtypeStruct((S5_G, 1), f32), c16, c16])(
        a_re, a_im, ls, b_re, b_im, d_ar, d_ai, d_br, d_bi, seg)


ANY = pl.BlockSpec(memory_space=pl.ANY)

GATHER = {'w_in': ((4352, 1024), 0), 'ffn_w_up': ((1024, 5632), 1), 'w_branch_rwkv': ((512, 1024), 1),
          'w_branch_s5': ((512, 1024), 1), 'w_out': ((1024, 1024), 0), 's5_w_glu': ((512, 512), 0),
          'ffn_w_down': ((2816, 1024), 0), 'rwkv_w2': ((64, 512), 1), 'rwkv_a2': ((64, 512), 1),
          'rwkv_g2': ((128, 512), 1), 'ffn_conv_w': ((8, 5632), 1)}
BIG = ['w_in', 'ffn_w_up', 'w_branch_rwkv', 'w_branch_s5', 'w_out', 's5_w_glu', 'ffn_w_down']
TINY = ['rwkv_w2', 'rwkv_a2', 'rwkv_g2', 'ffn_conv_w']
SMALL = [n for n in WEIGHTS if n not in GATHER]
SMALL_ROWS = 320
ADAM_ROWS = 256


def _mo(v, m):
    return v if isinstance(v, int) else pl.multiple_of(v, m)


def _slab(ref, name, j, h=None):
    (R, Cn), axis = GATHER[name]
    if axis == 0:
        rs = R // 4
        if h is None:
            return ref.at[pl.ds(_mo(j * rs, 16), rs), :]
        return ref.at[pl.ds(_mo(j * rs + h * (rs // 2), 8), rs // 2), :]
    cols = pl.ds(_mo(j * (Cn // 4), 128), Cn // 4)
    if h is None:
        return ref.at[:, cols]
    return ref.at[pl.ds(_mo(h * (R // 2), 8), R // 2), cols]


def _half_shape(name):
    (R, Cn), axis = GATHER[name]
    return (R // 8, Cn) if axis == 0 else (R // 2, Cn // 4)


def _chip_peers(px, py):
    return [((1 - px) if (k >> 1) else px, (1 - py) if (k & 1) else py) for k in (1, 2, 3)]


def _run_copies(copies):
    for cp in copies:
        cp.start()
    for cp in copies:
        cp.wait()


def _gather_weights(blocks):
    names = list(blocks)
    n = len(names)

    def body(*refs):
        ins, outs = refs[:n], refs[n:2 * n]
        ssem, rsem, lsem = refs[2 * n:]
        px, py, pc = _mesh_pos()
        me = 2 * px + py
        copies = []
        for i, nm in enumerate(names):
            if nm in BIG:
                hr = blocks[nm].shape[0] // 2
                src, dst = ins[i].at[pl.ds(pl.multiple_of(pc * hr, 16), hr), :], _slab(outs[i], nm, me, pc)
            else:
                src, dst = ins[i], _slab(outs[i], nm, me)
            copies.append(pltpu.make_async_copy(src, dst, lsem.at[i]))
            for k, (qx, qy) in enumerate(_chip_peers(px, py)):
                copies.append(pltpu.make_async_remote_copy(src, dst, ssem.at[3 * i + k], rsem.at[3 * i + k],
                                                           device_id=(qx, qy, pc), device_id_type=MESH))
        _run_copies(copies)

    outs = pl.pallas_call(
        body, name="gather_weights", in_specs=[ANY] * n, out_specs=[ANY] * n,
        out_shape=[jax.ShapeDtypeStruct(GATHER[nm][0], blocks[nm].dtype) for nm in names],
        scratch_shapes=[pltpu.SemaphoreType.DMA((3 * n,)), pltpu.SemaphoreType.DMA((3 * n,)), pltpu.SemaphoreType.DMA((n,))],
    )(*[blocks[nm] for nm in names])
    return dict(zip(names, outs))


def _gather_pair(full):
    n = len(BIG)

    def body(*refs):
        ins, outs = refs[:n], refs[n:2 * n]
        ssem, rsem = refs[2 * n:]
        px, py, pc = _mesh_pos()
        copies = []
        for i, nm in enumerate(BIG):
            for j in range(4):
                copies.append(pltpu.make_async_remote_copy(_slab(ins[i], nm, j, pc), _slab(outs[i], nm, j, pc),
                                                           ssem.at[4 * i + j], rsem.at[4 * i + j],
                                                           device_id=(px, py, 1 - pc), device_id_type=MESH))
        _run_copies(copies)

    outs = pl.pallas_call(
        body, name="gather_weights_pair", in_specs=[ANY] * n, out_specs=[ANY] * n,
        out_shape=[jax.ShapeDtypeStruct(full[nm].shape, full[nm].dtype) for nm in BIG],
        input_output_aliases={i: i for i in range(n)},
        scratch_shapes=[pltpu.SemaphoreType.DMA((4 * n,)), pltpu.SemaphoreType.DMA((4 * n,))],
    )(*[full[nm] for nm in BIG])
    return dict(zip(BIG, outs))


def _grads_to_sibling(G, small):
    n = len(BIG)

    def body(*refs):
        g_refs, small_ref = refs[:n], refs[n]
        o_refs, small_o = refs[n + 1:2 * n + 1], refs[2 * n + 1]
        ssem, rsem = refs[2 * n + 2:]
        px, py, pc = _mesh_pos()
        sib = (px, py, 1 - pc)
        copies = []
        for i, nm in enumerate(BIG):
            for j in range(4):
                copies.append(pltpu.make_async_remote_copy(_slab(g_refs[i], nm, j, 1 - pc), o_refs[i].at[j],
                                                           ssem.at[4 * i + j], rsem.at[4 * i + j],
                                                           device_id=sib, device_id_type=MESH))
        copies.append(pltpu.make_async_remote_copy(small_ref, small_o, ssem.at[4 * n], rsem.at[4 * n],
                                                   device_id=sib, device_id_type=MESH))
        _run_copies(copies)

    outs = pl.pallas_call(
        body, name="grads_to_sibling", in_specs=[ANY] * (n + 1), out_specs=[ANY] * (n + 1),
        out_shape=[jax.ShapeDtypeStruct((4,) + _half_shape(nm), f32) for nm in BIG] + [jax.ShapeDtypeStruct(small.shape, f32)],
        scratch_shapes=[pltpu.SemaphoreType.DMA((4 * n + 1,)), pltpu.SemaphoreType.DMA((4 * n + 1,))],
    )(*[G[nm] for nm in BIG], small)
    return dict(zip(BIG, outs[:n])), outs[n]


def _pair_add(G, recv, small, small_recv):
    n = len(BIG)
    cidx = lax.axis_index("c").astype(jnp.int32).reshape(1)

    def body(c_ref, *refs):
        ins, outs = refs[:2 * n + 2], refs[2 * n + 2:]
        for i in range(n):
            outs[i][...] = (ins[i][...] + ins[n + i][...]).astype(bf16)
        outs[n][...] = ins[2 * n][...] + ins[2 * n + 1][...]

    g_specs, r_specs = [], []
    for nm in BIG:
        hr, hc = _half_shape(nm)
        if GATHER[nm][1] == 0:
            g_specs.append(pl.BlockSpec((hr // 2, hc), lambda j, i, c: ((2 * j + c[0]) * 2 + i, 0)))
        else:
            g_specs.append(pl.BlockSpec((hr // 2, hc), lambda j, i, c: (2 * c[0] + i, j)))
        r_specs.append(pl.BlockSpec((1, hr // 2, hc), lambda j, i, c: (j, i, 0)))
    sm = pl.BlockSpec((SMALL_ROWS // 8, PACK_W), lambda j, i, c: (2 * j + i, 0))
    outs = pl.pallas_call(
        body, name="grads_pair_sum",
        grid_spec=pltpu.PrefetchScalarGridSpec(num_scalar_prefetch=1, grid=(4, 2), in_specs=g_specs + r_specs + [sm, sm],
                                               out_specs=r_specs + [sm]),
        out_shape=[jax.ShapeDtypeStruct((4,) + _half_shape(nm), bf16) for nm in BIG] + [jax.ShapeDtypeStruct(small.shape, f32)],
        compiler_params=pltpu.CompilerParams(vmem_limit_bytes=VMEM_LIMIT),
    )(cidx, *[G[nm] for nm in BIG], *[recv[nm] for nm in BIG], small, small_recv)
    return dict(zip(BIG, outs[:n])), outs[n]


def _grads_chip_exchange(chip_sum, small):
    n = len(BIG)

    def body(*refs):
        ins, outs = refs[:n + 1], refs[n + 1:2 * n + 2]
        ssem, rsem, lsem = refs[2 * n + 2:]
        px, py, pc = _mesh_pos()
        me = 2 * px + py
        copies = []
        copies.append(pltpu.make_async_copy(ins[n], outs[n].at[me], lsem))
        for i in range(n + 1):
            pick = (lambda ref, j: ref.at[j]) if i < n else (lambda ref, j: ref)
            for k, (qx, qy) in enumerate(_chip_peers(px, py)):
                copies.append(pltpu.make_async_remote_copy(pick(ins[i], 2 * qx + qy), outs[i].at[me],
                                                           ssem.at[3 * i + k], rsem.at[3 * i + k],
                                                           device_id=(qx, qy, pc), device_id_type=MESH))
        _run_copies(copies)

    outs = pl.pallas_call(
        body, name="grads_chip_exchange", in_specs=[ANY] * (n + 1), out_specs=[ANY] * (n + 1),
        out_shape=[jax.ShapeDtypeStruct(chip_sum[nm].shape, chip_sum[nm].dtype) for nm in BIG]
        + [jax.ShapeDtypeStruct((4,) + small.shape, f32)],
        scratch_shapes=[pltpu.SemaphoreType.DMA((3 * n + 3,)), pltpu.SemaphoreType.DMA((3 * n + 3,)),
                        pltpu.SemaphoreType.DMA],
    )(*[chip_sum[nm] for nm in BIG], small)
    return dict(zip(BIG, outs[:n])), outs[n]


def _sum_slots(slots, chip_sum, small4):
    n = len(BIG)
    me = (2 * lax.axis_index("x") + lax.axis_index("y")).astype(jnp.int32).reshape(1)

    def body(me_ref, *refs):
        for i in range(n):
            own = refs[5 * i + 4][0].astype(f32)
            term = [jnp.where(me_ref[0] == k, own, refs[5 * i + k][0].astype(f32)) for k in range(4)]
            refs[5 * n + 1 + i][...] = ((term[0] + term[1]) + term[2]) + term[3]
        x = refs[5 * n]
        refs[6 * n + 1][...] = ((x[0] + x[1]) + x[2]) + x[3]

    in_specs, args, specs_out, shapes = [], [], [], []
    for nm in BIG:
        hr, hc = _half_shape(nm)
        for k in range(4):
            in_specs.append(pl.BlockSpec((1, hr // 2, hc), lambda i, m, k=k: (jnp.where(m[0] == k, (k + 1) % 4, k), i, 0)))
        in_specs.append(pl.BlockSpec((1, hr // 2, hc), lambda i, m: (m[0], i, 0)))
        args += [slots[nm]] * 4 + [chip_sum[nm]]
        specs_out.append(pl.BlockSpec((hr // 2, hc), lambda i, m: (i, 0)))
        shapes.append(jax.ShapeDtypeStruct((hr, hc), f32))
    in_specs.append(pl.BlockSpec((4, SMALL_ROWS // 2, PACK_W), lambda i, m: (0, i, 0)))
    specs_out.append(pl.BlockSpec((SMALL_ROWS // 2, PACK_W), lambda i, m: (i, 0)))
    shapes.append(jax.ShapeDtypeStruct((SMALL_ROWS, PACK_W), f32))
    outs = pl.pallas_call(
        body, name="grads_chip_sum",
        grid_spec=pltpu.PrefetchScalarGridSpec(num_scalar_prefetch=1, grid=(2,), in_specs=in_specs, out_specs=specs_out),
        out_shape=shapes, compiler_params=pltpu.CompilerParams(vmem_limit_bytes=VMEM_LIMIT),
    )(me, *args, small4)
    return dict(zip(BIG, outs[:n])), outs[n]


def _halves_to_sibling(half):
    n = len(BIG)

    def body(*refs):
        ins, outs = refs[:n], refs[n:2 * n]
        ssem, rsem = refs[2 * n:]
        px, py, pc = _mesh_pos()
        _run_copies([pltpu.make_async_remote_copy(ins[i], outs[i], ssem.at[i], rsem.at[i],
                                                  device_id=(px, py, 1 - pc), device_id_type=MESH) for i in range(n)])

    outs = pl.pallas_call(
        body, name="grads_halves_to_sibling", in_specs=[ANY] * n, out_specs=[ANY] * n,
        out_shape=[jax.ShapeDtypeStruct(_half_shape(nm), f32) for nm in BIG],
        scratch_shapes=[pltpu.SemaphoreType.DMA((n,)), pltpu.SemaphoreType.DMA((n,))],
    )(*[half[nm] for nm in BIG])
    return dict(zip(BIG, outs))


def _join_halves(mine, other, pc):
    hr = mine.shape[0]
    return lax.dynamic_slice_in_dim(jnp.concatenate([other, mine, other], axis=0), (1 - pc) * hr, 2 * hr, axis=0)


def _flat_pad(v):
    v = v.reshape(-1)
    return jnp.pad(v, (0, _ceil_to(v.shape[0], PACK_W) - v.shape[0]))


def _pack_rows(parts, rows):
    flat = jnp.concatenate([_flat_pad(p) for p in parts])
    return jnp.pad(flat, (0, rows * PACK_W - flat.shape[0])).reshape(rows, PACK_W)


def _unpack_rows(buf, shapes):
    flat = buf.reshape(-1)
    out, off = [], 0
    for shp in shapes:
        n = 1
        for d in shp:
            n *= d
        out.append(flat[off:off + n].reshape(shp))
        off += _ceil_to(n, PACK_W)
    return out


def _adamw_math(w_, g_, m_, v_):
    m2 = ADAM_B1 * m_ + (1.0 - ADAM_B1) * g_
    v2 = ADAM_B2 * v_ + (1.0 - ADAM_B2) * (g_ * g_)
    m_hat = m2 / (1.0 - ADAM_B1 ** ADAM_STEP)
    v_hat = v2 / (1.0 - ADAM_B2 ** ADAM_STEP)
    return -ADAM_LR * (m_hat / (jnp.sqrt(v_hat) + ADAM_EPS) + ADAM_WD * w_), m2, v2


def _adamw(groups):
    ng = len(groups)

    def body(*refs):
        ins, outs = refs[:4 * ng], refs[4 * ng:]
        for i in range(ng):
            res = _adamw_math(*(r[...] for r in ins[4 * i:4 * i + 4]))
            for ref, val in zip(outs[3 * i:3 * i + 3], res):
                ref[...] = val

    in_specs, out_specs, out_shape = [], [], []
    for grp in groups:
        R, Cn = grp[0].shape
        spec = pl.BlockSpec((R // 8, Cn), lambda i: (i, 0))
        in_specs += [spec] * 4
        out_specs += [spec] * 3
        out_shape += [jax.ShapeDtypeStruct((R, Cn), f32)] * 3
    outs = pl.pallas_call(
        body, name="adamw", grid=(8,), in_specs=in_specs, out_specs=out_specs, out_shape=out_shape,
        compiler_params=pltpu.CompilerParams(vmem_limit_bytes=VMEM_LIMIT),
    )(*[a for grp in groups for a in grp])
    return [tuple(outs[3 * i:3 * i + 3]) for i in range(ng)]


def _forward_backward(x, tgt, W, S):
    L = x.shape[0]
    TM, TMW = 256, 128
    row = lambda c, dt=f32: (c, dt)
    hid = jnp.arange(RWKV_W) // HEAD
    E = (hid[:, None] == hid[None, :]).astype(f32)
    seg = (jnp.arange(S5_N)[None, :] // S5_P == jnp.arange(S5_G)[:, None]).astype(f32)

    w_in_t = W['w_in']
    w_p, w_u, w_g = w_in_t[:N_RWKV], w_in_t[N_RWKV:N_RWKV + S5_W], w_in_t[N_RWKV + S5_W:]
    zpad = jnp.zeros((64, RWKV_W), f32)
    w2p = jnp.concatenate([W['rwkv_w2'], zpad], axis=0)
    a2p = jnp.concatenate([zpad, W['rwkv_a2']], axis=0)
    g2 = W['rwkv_g2']
    prep_consts = [S['rwkv_shift_mu'], S['rwkv_w0'], S['rwkv_a0'], S['rwkv_k_k'], S['rwkv_k_a'], w2p, a2p, g2, E]
    out_consts = [S['rwkv_lnx_w'], S['rwkv_lnx_b'], S['rwkv_r_k'], E]
    cw, cb = W['ffn_conv_w'][:3], S['ffn_conv_b']

    a_re, a_im = S['s5_a_re'].reshape(S5_N, 1), S['s5_a_im'].reshape(S5_N, 1)
    ls = jnp.repeat(S['s5_log_step'].reshape(S5_G, 1), S5_P, axis=0)
    b_re, b_im = S['s5_b_re'].reshape(S5_N, S5_C), S['s5_b_im'].reshape(S5_N, S5_C)
    ar, ai, bbr, bbi = _s5_disc_fwd(a_re, a_im, ls, b_re, b_im)
    abar = jnp.concatenate([ar.reshape(1, S5_N), ai.reshape(1, S5_N)], axis=1)
    eye = jnp.eye(S5_G, dtype=f32)

    def bdiag_in(bb):
        t = bb.reshape(S5_G, S5_P, S5_C).transpose(0, 2, 1)
        return (t[:, :, None, :] * eye[:, None, :, None]).reshape(S5_W, S5_N)

    def bdiag_out(cc):
        t = cc.transpose(0, 2, 1)
        return (t[:, :, None, :] * eye[:, None, :, None]).reshape(S5_N, S5_W)

    def undiag_in(m):
        t = m.reshape(S5_G, S5_C, S5_G, S5_P)
        t = jnp.sum(t * eye[:, None, :, None], axis=2)
        return t.transpose(0, 2, 1).reshape(S5_N, S5_C)

    def undiag_out(m):
        t = m.reshape(S5_G, S5_P, S5_G, S5_C)
        t = jnp.sum(t * eye[:, None, :, None], axis=2)
        return t.transpose(0, 2, 1)

    bmat = jnp.concatenate([bdiag_in(bbr), bdiag_in(bbi)], axis=1).astype(bf16)
    cmat = jnp.concatenate([bdiag_out(S['s5_c_re'].reshape(S5_G, S5_C, S5_P)),
                            -bdiag_out(S['s5_c_im'].reshape(S5_G, S5_C, S5_P))], axis=0).astype(bf16)

    g1, g2n, g3, g4 = S['norm_mix_pre'], S['norm_mix_post'], S['norm_ffn_pre'], S['norm_ffn_post']
    (h1,) = _rowcall("norm_pre", lambda i, n, R, P, X, C: ((_rms(R[0], C[0]),), ()), L, TM, [x], [g1],
                     out_rows=[row(D_MODEL, bf16)])
    p = _mm(h1, w_p, 'nt', "mm_p")
    u = _mm(h1, w_u, 'nt', "mm_u")
    gp = _mm(h1, w_g, 'nt', "mm_g")

    def prep_fn(i, n, R, P, X, C):
        q = R[0] + (_shift_down(R[0], P[0], i, 1) - R[0]) * C[0]
        return _prep(q, *C[1:]), ()

    r, lw, k2, v, an, bv, g = _rowcall("rwkv_prep", prep_fn, L, TM, [p], prep_consts,
                                       out_rows=[row(RWKV_W)] * 7, prev=[0])
    y, ck = _wkv7_fwd(r, lw, k2, v, an, bv)
    (o_a,) = _rowcall("rwkv_out", lambda i, n, R, P, X, C: ((_rwkv_out(*R, *C),), ()), L, TM, [y, r, k2, v, g],
                      out_consts, out_rows=[row(RWKV_W, bf16)])
    o_r = _mm(o_a, W['w_branch_rwkv'], 'nn', "mm_br")

    bu = _mm(u, bmat, 'nn', "mm_bu")
    st = _s5_scan(bu, abar, False, "s5_scan")
    ysc = _mm(st, cmat, 'nn', "mm_cs")
    (yg,) = _rowcall("s5_mid", lambda i, n, R, P, X, C: ((_s5_mid(*R, *C),), ()), L, TM, [ysc, u], [S['s5_d']],
                     out_rows=[row(S5_W)])
    z2 = _mm(yg, W['s5_w_glu'], 'nn', "mm_glu")
    (o_b,) = _rowcall("s5_glu", lambda i, n, R, P, X, C: ((_s5_glu(*R, *C),), ()), L, TM, [yg, z2], [S['s5_b_glu']],
                      out_rows=[row(S5_W, bf16)])
    o_s = _mm(o_b, W['w_branch_s5'], 'nn', "mm_bs")

    (merged,) = _rowcall("merge", lambda i, n, R, P, X, C: ((_merge(*R, *C),), ()), L, TM, [gp, o_r, o_s],
                         [S['b_gate']], out_rows=[row(D_MODEL, bf16)])
    mixed = _mm(merged, W['w_out'], 'nn', "mm_out")

    def resid_fn(i, n, R, P, X, C):
        x1_ = R[0] + _rms(R[1], C[0])
        return (x1_, _rms(x1_, C[1])), ()

    x1, h2 = _rowcall("resid_norm", resid_fn, L, TM, [x, mixed], [g2n, g3], out_rows=[row(D_MODEL), row(D_MODEL, bf16)])

    z = _mm(h2, W['ffn_w_up'], 'nn', "mm_up")

    def conv(zt, zprev, i, cw_, cb_):
        z2s, z1s = _shift_down(zt, zprev, i, 2), _shift_down(zt, zprev, i, 1)
        return cb_ + cw_[0:1] * z2s + cw_[1:2] * z1s + cw_[2:3] * zt, z2s, z1s

    (act,) = _rowcall("conv_act", lambda i, n, R, P, X, C: ((_act(conv(R[0], P[0], i, C[0], C[1])[0]),), ()), L, TMW,
                      [z], [cw, cb], out_rows=[row(D_FF, bf16)], prev=[0])
    f = _mm(act, W['ffn_w_down'], 'nn', "mm_down")

    def final_fn(i, n, R, P, X, C):
        x1_, f_, t_ = R
        fn_, vjp = jax.vjp(_rms, f_, C[0])
        diff = x1_ + fn_ - t_
        loss = jnp.sum(diff * diff) * (0.5 / D_MODEL)
        dx2_ = diff * (1.0 / D_MODEL)
        df_, dg4_ = vjp(dx2_)
        return (df_, dx2_), (jnp.full((1, PACK_W), loss, f32), dg4_)

    df, dx2, loss, dg4 = _rowcall("loss_head", final_fn, L, TM, [x1, f, tgt], [g4],
                                  out_rows=[row(D_MODEL, bf16), row(D_MODEL)], out_accs=[(1, PACK_W), (1, D_MODEL)])
    G = {'norm_ffn_post': dg4}

    dact = _mm(df, W['ffn_w_down'], 'nt', "mm_down_dx")
    G['ffn_w_down'] = _mm(act, df, 'tn', "mm_down_dw")

    def conv_bwd_fn(i, n, R, P, X, C):
        zc, z2s, z1s = conv(R[0], P[0], i, C[0], C[1])
        _, vjp = jax.vjp(_act, zc)
        (dzc_,) = vjp(R[1])
        return (dzc_,), (_sum0(dzc_), _sum0(dzc_ * z2s), _sum0(dzc_ * z1s), _sum0(dzc_ * R[0]))

    wide = (1, 2 * D_FF)
    dzc, dcb, dcw0, dcw1, dcw2 = _rowcall("conv_act_bwd", conv_bwd_fn, L, TMW, [z, dact], [cw, cb],
                                          out_rows=[row(2 * D_FF)], out_accs=[wide] * 4, prev=[0])
    G['ffn_conv_b'] = dcb
    G['ffn_conv_w'] = jnp.concatenate([dcw0, dcw1, dcw2], axis=0)

    def conv_shift_fn(i, n, R, P, X, C):
        d = R[0]
        return (C[0][2:3] * d + C[0][1:2] * _shift_up(d, X[0], i, n, 1) + C[0][0:1] * _shift_up(d, X[0], i, n, 2),), ()

    (dz,) = _rowcall("conv_shift_bwd", conv_shift_fn, L, TMW, [dzc], [cw], out_rows=[row(2 * D_FF, bf16)], nxt=[0])
    dh2 = _mm(dz, W['ffn_w_up'], 'nt', "mm_up_dx")
    G['ffn_w_up'] = _mm(h2, dz, 'tn', "mm_up_dw")

    def norm2_bwd_fn(i, n, R, P, X, C):
        x1_, mixed_, dx2_, dh2_ = R
        _, vjp3 = jax.vjp(_rms, x1_, C[1])
        dx1a, dg3_ = vjp3(dh2_)
        dx1_ = dx2_ + dx1a
        _, vjp2 = jax.vjp(_rms, mixed_, C[0])
        dmixed_, dg2_ = vjp2(dx1_)
        return (dx1_, dmixed_), (dg2_, dg3_)

    dx1, dmixed, dg2n, dg3 = _rowcall("norm_mid_bwd", norm2_bwd_fn, L, TM, [x1, mixed, dx2, dh2], [g2n, g3],
                                      out_rows=[row(D_MODEL), row(D_MODEL, bf16)], out_accs=[(1, D_MODEL)] * 2)
    G['norm_mix_post'], G['norm_ffn_pre'] = dg2n, dg3

    dmerged = _mm(dmixed, W['w_out'], 'nt', "mm_out_dx")
    G['w_out'] = _mm(merged, dmixed, 'tn', "mm_out_dw")

    def merge_bwd_fn(i, n, R, P, X, C):
        _, vjp = jax.vjp(_merge, R[0], R[1], R[2], C[0])
        dgp_, do_r_, do_s_, dbg_ = vjp(R[3])
        return (dgp_, do_r_, do_s_), (dbg_,)

    dgp, do_r, do_s, G['b_gate'] = _rowcall("merge_bwd", merge_bwd_fn, L, TM, [gp, o_r, o_s, dmerged], [S['b_gate']],
                                            out_rows=[row(2 * D_MODEL, bf16), row(D_MODEL, bf16), row(D_MODEL, bf16)],
                                            out_accs=[(1, 2 * D_MODEL)])
    do_a = _mm(do_r, W['w_branch_rwkv'], 'nt', "mm_br_dx")
    G['w_branch_rwkv'] = _mm(o_a, do_r, 'tn', "mm_br_dw")
    do_b = _mm(do_s, W['w_branch_s5'], 'nt', "mm_bs_dx")
    G['w_branch_s5'] = _mm(o_b, do_s, 'tn', "mm_bs_dw")

    def glu_bwd_fn(i, n, R, P, X, C):
        _, vjp = jax.vjp(_s5_glu, R[0], R[1], C[0])
        dyg1_, dz2_, dbg_ = vjp(R[2])
        return (dyg1_, dz2_), (dbg_,)

    dyg1, dz2, G['s5_b_glu'] = _rowcall("s5_glu_bwd", glu_bwd_fn, L, TM, [yg, z2, do_b], [S['s5_b_glu']],
                                        out_rows=[row(S5_W), row(S5_W, bf16)], out_accs=[(1, S5_W)])
    dyg2 = _mm(dz2, W['s5_w_glu'], 'nt', "mm_glu_dx")
    G['s5_w_glu'] = _mm(yg, dz2, 'tn', "mm_glu_dw")

    def mid_bwd_fn(i, n, R, P, X, C):
        _, vjp = jax.vjp(_s5_mid, R[0], R[1], C[0])
        dysc_, du_, dd_ = vjp(R[2] + R[3])
        return (dysc_, du_), (dd_,)

    dysc, du1, G['s5_d'] = _rowcall("s5_mid_bwd", mid_bwd_fn, L, TM, [ysc, u, dyg1, dyg2], [S['s5_d']],
                                    out_rows=[row(S5_W, bf16), row(S5_W)], out_accs=[(1, S5_W)])
    dst = _mm(dysc, cmat, 'nt', "mm_cs_dx")
    dcmat = _mm(st, dysc, 'tn', "mm_cs_dw")
    lam = _s5_scan(dst, abar, True, "s5_scan_bwd")

    def s5_da_fn(i, n, R, P, X, C):
        lr, li = R[0][:, :S5_N], R[0][:, S5_N:]
        sp = _shift_down(R[1], P[0], i, 1)
        sr, si = sp[:, :S5_N], sp[:, S5_N:]
        return (), (jnp.concatenate([_sum0(lr * sr + li * si), _sum0(li * sr - lr * si)], axis=1),)

    (dabar,) = _rowcall("s5_da", s5_da_fn, L, TM, [lam, st], out_accs=[(1, 2 * S5_N)], prev=[1])
    du2 = _mm(lam, bmat, 'nt', "mm_bu_dx")
    dbmat = _mm(u, lam, 'tn', "mm_bu_dw")
    da_re, da_im, dls, db_re, db_im = _s5_disc_bwd(
        a_re, a_im, ls, b_re, b_im, dabar[:, :S5_N].reshape(S5_N, 1), dabar[:, S5_N:].reshape(S5_N, 1),
        undiag_in(dbmat[:, :S5_N]), undiag_in(dbmat[:, S5_N:]), seg)
    G['s5_a_re'], G['s5_a_im'], G['s5_log_step'] = da_re, da_im, dls
    G['s5_b_re'], G['s5_b_im'] = db_re, db_im
    G['s5_c_re'], G['s5_c_im'] = undiag_out(dcmat[:S5_N]), -undiag_out(dcmat[S5_N:])

    def out_bwd_fn(i, n, R, P, X, C):
        _, vjp = jax.vjp(_rwkv_out, *R[:5], *C)
        gs = vjp(R[5])
        return gs[:5], gs[5:8]

    dy, dr1, dk1, dv1, dg, dlw, dlb, drk = _rowcall("rwkv_out_bwd", out_bwd_fn, L, TM, [y, r, k2, v, g, do_a], out_consts,
                                                    out_rows=[row(RWKV_W)] * 5, out_accs=[(1, RWKV_W)] * 3)
    G['rwkv_lnx_w'], G['rwkv_lnx_b'], G['rwkv_r_k'] = dlw, dlb, drk
    dr2, dlwk, dk2b, dv2, dan, dbv = _wkv7_bwd(r, lw, k2, v, an, bv, ck, dy)

    def prep_bwd_fn(i, n, R, P, X, C):
        p_ = R[0]
        d1 = _shift_down(p_, P[0], i, 1) - p_
        q = p_ + d1 * C[0]
        _, vjp = jax.vjp(_prep, q, *C[1:])
        cots = (R[1] + R[2], R[3], R[4] + R[5], R[6] + R[7], R[8], R[9], R[10])
        gs = vjp(cots)
        return (gs[0],), (_sum0(gs[0] * d1),) + tuple(gs[1:8])

    small, lowr = (1, RWKV_W), (128, RWKV_W)
    dq, dmu, dw0, da0, dkk, dka, dw2p, da2p, dg2 = _rowcall(
        "rwkv_prep_bwd", prep_bwd_fn, L, TM, [p, dr1, dr2, dlwk, dk1, dk2b, dv1, dv2, dan, dbv, dg],
        prep_consts, out_rows=[row(N_RWKV)], out_accs=[(1, N_RWKV)] + [small] * 4 + [lowr] * 3, prev=[0])
    G['rwkv_shift_mu'], G['rwkv_w0'], G['rwkv_a0'], G['rwkv_k_k'], G['rwkv_k_a'] = dmu, dw0, da0, dkk, dka
    G['rwkv_w2'], G['rwkv_a2'], G['rwkv_g2'] = dw2p[:64], da2p[64:], dg2

    def shift_bwd_fn(i, n, R, P, X, C):
        dm = R[0] * C[0]
        return (R[0] - dm + _shift_up(dm, X[0] * C[0], i, n, 1),), ()

    (dp,) = _rowcall("shift_bwd", shift_bwd_fn, L, TM, [dq], [S['rwkv_shift_mu']], out_rows=[row(N_RWKV, bf16)], nxt=[0])

    (du,) = _rowcall("add_du", lambda i, n, R, P, X, C: ((R[0] + R[1],), ()), L, TM, [du1, du2], out_rows=[row(S5_W, bf16)])
    dproj = jnp.concatenate([dp, du, dgp], axis=1)
    dh1 = _mm(dproj, w_in_t, 'nn', "mm_in_dx")
    G['w_in'] = _mm(dproj, h1, 'tn', "mm_in_dw")

    def norm1_bwd_fn(i, n, R, P, X, C):
        _, vjp = jax.vjp(_rms, R[0], C[0])
        dxa, dg1_ = vjp(R[2])
        return (R[1] + dxa,), (dg1_,)

    dx, G['norm_mix_pre'] = _rowcall("norm_pre_bwd", norm1_bwd_fn, L, TM, [x, dx1, dh1], [g1],
                                     out_rows=[row(D_MODEL)], out_accs=[(1, D_MODEL)])
    return loss, dx, G


def kernel(x, norm_mix_pre, norm_mix_post, norm_ffn_pre, norm_ffn_post, w_in, b_gate, rwkv_shift_mu, rwkv_w0, rwkv_w2, rwkv_a0, rwkv_a2, rwkv_g2, rwkv_k_k, rwkv_k_a, rwkv_r_k, rwkv_lnx_w, rwkv_lnx_b, s5_a_re, s5_a_im, s5_b_re, s5_b_im, s5_c_re, s5_c_im, s5_d, s5_log_step, s5_w_glu, s5_b_glu, w_branch_rwkv, w_branch_s5, w_out, ffn_w_up, ffn_conv_w, ffn_conv_b, ffn_w_down, loss_target, m_norm_mix_pre, m_norm_mix_post, m_norm_ffn_pre, m_norm_ffn_post, m_w_in, m_b_gate, m_rwkv_shift_mu, m_rwkv_w0, m_rwkv_w2, m_rwkv_a0, m_rwkv_a2, m_rwkv_g2, m_rwkv_k_k, m_rwkv_k_a, m_rwkv_r_k, m_rwkv_lnx_w, m_rwkv_lnx_b, m_s5_a_re, m_s5_a_im, m_s5_b_re, m_s5_b_im, m_s5_c_re, m_s5_c_im, m_s5_d, m_s5_log_step, m_s5_w_glu, m_s5_b_glu, m_w_branch_rwkv, m_w_branch_s5, m_w_out, m_ffn_w_up, m_ffn_conv_w, m_ffn_conv_b, m_ffn_w_down, v_norm_mix_pre, v_norm_mix_post, v_norm_ffn_pre, v_norm_ffn_post, v_w_in, v_b_gate, v_rwkv_shift_mu, v_rwkv_w0, v_rwkv_w2, v_rwkv_a0, v_rwkv_a2, v_rwkv_g2, v_rwkv_k_k, v_rwkv_k_a, v_rwkv_r_k, v_rwkv_lnx_w, v_rwkv_lnx_b, v_s5_a_re, v_s5_a_im, v_s5_b_re, v_s5_b_im, v_s5_c_re, v_s5_c_im, v_s5_d, v_s5_log_step, v_s5_w_glu, v_s5_b_glu, v_w_branch_rwkv, v_w_branch_s5, v_w_out, v_ffn_w_up, v_ffn_conv_w, v_ffn_conv_b, v_ffn_w_down):
    A = dict(locals())
    me = 2 * lax.axis_index("x") + lax.axis_index("y")
    blk = lambda n: A[n][0]

    mine = {n: (blk(n).T if n == 'w_in' else blk(n)).astype(bf16) for n in BIG}
    mine.update({n: blk(n) for n in TINY})
    mine['ffn_conv_w'] = jnp.pad(blk('ffn_conv_w'), ((0, 5), (0, 0)))
    W = _gather_weights(mine)
    W.update(_gather_pair(W))
    S = {n: A[n].reshape(1, -1) for n in SMALL}

    loss, dx, G = _forward_backward(x[0], loss_target[0], W, S)

    tiny_shapes = [G[n].shape for n in TINY]
    small_buf = _pack_rows([G[n] for n in SMALL] + [G[n] for n in TINY] + [loss], SMALL_ROWS)
    recv, small_recv = _grads_to_sibling(G, small_buf)
    chip_sum, small_sum = _pair_add(G, recv, small_buf, small_recv)
    slots, small4 = _grads_chip_exchange(chip_sum, small_sum)
    half, small_tot = _sum_slots(slots, chip_sum, small4)
    other = _halves_to_sibling(half)
    pc = lax.axis_index("c")
    grad = {n: _join_halves(half[n], other[n], pc) for n in BIG}
    grad['w_in'] = grad['w_in'].T
    vals = _unpack_rows(small_tot, [A[n].shape for n in SMALL] + tiny_shapes + [(1, PACK_W)])
    grad.update(zip(SMALL, vals))
    for n, full in zip(TINY, vals[len(SMALL):]):
        cs = A[n].shape[2]
        grad[n] = lax.dynamic_slice_in_dim(full, me * cs, cs, axis=1)
    loss_out = vals[-1][0, 0]

    packed = SMALL + TINY
    groups = [(blk(n), grad[n], blk('m_' + n), blk('v_' + n)) for n in BIG]
    groups.append(tuple(_pack_rows([src(n) for n in packed], ADAM_ROWS)
                        for src in (lambda n: A[n], lambda n: grad[n], lambda n: A['m_' + n], lambda n: A['v_' + n])))
    res = _adamw(groups)
    outs = [dict(), dict(), dict()]
    for n, r3 in zip(BIG, res[:-1]):
        for d, val in zip(outs, r3):
            d[n] = val
    for d, buf in zip(outs, res[-1]):
        d.update(zip(packed, _unpack_rows(buf, [A[n].shape for n in packed])))
    full = lambda d: [d[n].reshape(A[n].shape) for n in WEIGHTS]
    return (loss_out, dx[None], *full(grad), *full(outs[0]), *full(outs[1]), *full(outs[2]))
```

```python
import functools

import jax
import jax.numpy as jnp
from jax import lax
from jax.experimental import pallas as pl
from jax.experimental.pallas import tpu as pltpu

f32, bf16 = jnp.float32, jnp.bfloat16
MESH = pl.DeviceIdType.MESH

D_MODEL = 1024
RWKV_W = 512
HEADS, HEAD = 8, 64
N_RWKV = 1792
S5_W = 512
S5_G, S5_P, S5_C = 32, 64, 16
S5_N = S5_G * S5_P
D_FF = 2816
NORM_EPS = 1e-6
LNX_EPS = 64e-5
ADAM_LR, ADAM_B1, ADAM_B2, ADAM_EPS, ADAM_WD, ADAM_STEP = 0.001, 0.9, 0.999, 1e-08, 0.01, 10

VMEM_LIMIT = 48 * 1024 * 1024
PACK_W = 1024
WKV_C = 64
S5_T = 256

WEIGHTS = ['norm_mix_pre', 'norm_mix_post', 'norm_ffn_pre', 'norm_ffn_post', 'w_in', 'b_gate', 'rwkv_shift_mu',
           'rwkv_w0', 'rwkv_w2', 'rwkv_a0', 'rwkv_a2', 'rwkv_g2', 'rwkv_k_k', 'rwkv_k_a', 'rwkv_r_k', 'rwkv_lnx_w',
           'rwkv_lnx_b', 's5_a_re', 's5_a_im', 's5_b_re', 's5_b_im', 's5_c_re', 's5_c_im', 's5_d', 's5_log_step',
           's5_w_glu', 's5_b_glu', 'w_branch_rwkv', 'w_branch_s5', 'w_out', 'ffn_w_up', 'ffn_conv_w', 'ffn_conv_b',
           'ffn_w_down']


def _ceil_to(n, m):
    return -(-n // m) * m


def _mesh_pos():
    return lax.axis_index("x"), lax.axis_index("y"), lax.axis_index("c")


def _pick(d, cap=4096):
    for c in (1024, 1408, 2176, 896, 512, 256, 128):
        if c <= cap and d % c == 0:
            return c
    raise ValueError(d)


def _mm(a, b, mode, name, out_dtype=f32):
    if mode == 'tn':
        (K, M), (K2, N) = a.shape, b.shape
    elif mode == 'nt':
        (M, K), (N, K2) = a.shape, b.shape
    else:
        (M, K), (K2, N) = a.shape, b.shape
    assert K == K2, (name, a.shape, b.shape)
    if mode == 'tn':
        tm = _pick(M, 2176)
        tn = _pick(N, 512 if tm > 1408 else (1024 if tm > 1024 else 1408))
        tk = _pick(K, 512)
    else:
        tm, tn, tk = _pick(M, 512), _pick(N), _pick(K)
    nk = K // tk
    dims = {'nn': ((1,), (0,)), 'nt': ((1,), (1,)), 'tn': ((0,), (0,))}[mode]

    def body(a_ref, b_ref, o_ref, acc_ref):
        k = pl.program_id(2)

        @pl.when(k == 0)
        def _():
            acc_ref[...] = jnp.zeros_like(acc_ref)

        acc_ref[...] += lax.dot_general(a_ref[...].astype(bf16), b_ref[...].astype(bf16), (dims, ((), ())),
                                        preferred_element_type=f32)

        @pl.when(k == nk - 1)
        def _():
            o_ref[...] = acc_ref[...].astype(o_ref.dtype)

    a_spec = pl.BlockSpec((tk, tm), lambda i, j, k: (k, i)) if mode == 'tn' else pl.BlockSpec((tm, tk), lambda i, j, k: (i, k))
    b_spec = pl.BlockSpec((tn, tk), lambda i, j, k: (j, k)) if mode == 'nt' else pl.BlockSpec((tk, tn), lambda i, j, k: (k, j))
    return pl.pallas_call(
        body, name=name, grid=(M // tm, N // tn, nk),
        in_specs=[a_spec, b_spec], out_specs=pl.BlockSpec((tm, tn), lambda i, j, k: (i, j)),
        out_shape=jax.ShapeDtypeStruct((M, N), out_dtype),
        scratch_shapes=[pltpu.VMEM((tm, tn), f32)],
        compiler_params=pltpu.CompilerParams(dimension_semantics=("parallel", "parallel", "arbitrary"),
                                             vmem_limit_bytes=VMEM_LIMIT),
    )(a, b)


def _mm_s5(a, b, kind, name):
    L = a.shape[0]
    CB, SB = 128, 512
    tm = 512
    nt = L // tm
    two = kind in ('in_dx', 'out')
    wgrad = kind in ('in_dw', 'out_dw')
    dims = {'in': 'nn', 'in_dx': 'nt', 'in_dw': 'tn', 'out': 'nn', 'out_dx': 'nt', 'out_dw': 'tn'}[kind]

    def body(a_ref, b_ref, o_ref, acc_ref):
        s = pl.program_id(2 if two else 1)
        prod = lax.dot_general(a_ref[...].astype(bf16), b_ref[...].astype(bf16), (_DIMS[dims], ((), ())),
                               preferred_element_type=f32)
        if not (two or wgrad):
            o_ref[...] = prod.astype(o_ref.dtype)
            return

        @pl.when(s == 0)
        def _():
            acc_ref[...] = jnp.zeros_like(acc_ref)

        acc_ref[...] += prod

        @pl.when(s == (1 if two else nt - 1))
        def _():
            o_ref[...] = acc_ref[...].reshape(o_ref.shape).astype(o_ref.dtype)

    if kind == 'in':
        grid, oshape, ob = (nt, 8), (L, 8 * SB), (tm, SB)
        specs = [pl.BlockSpec((tm, CB), lambda i, j: (i, j % 4)), pl.BlockSpec((CB, SB), lambda i, j: (j % 4, j))]
        ospec = pl.BlockSpec(ob, lambda i, j: (i, j))
    elif kind == 'out_dx':
        grid, oshape, ob = (nt, 8), (L, 8 * SB), (tm, SB)
        specs = [pl.BlockSpec((tm, CB), lambda i, j: (i, j % 4)), pl.BlockSpec((SB, CB), lambda i, j: (j, j % 4))]
        ospec = pl.BlockSpec(ob, lambda i, j: (i, j))
    elif kind == 'in_dx':
        grid, oshape, ob = (nt, 4, 2), (L, 4 * CB), (tm, CB)
        specs = [pl.BlockSpec((tm, SB), lambda i, k, p: (i, 4 * p + k)), pl.BlockSpec((CB, SB), lambda i, k, p: (k, 4 * p + k))]
        ospec = pl.BlockSpec(ob, lambda i, k, p: (i, k))
    elif kind == 'out':
        grid, oshape, ob = (nt, 4, 2), (L, 4 * CB), (tm, CB)
        specs = [pl.BlockSpec((tm, SB), lambda i, k, p: (i, 4 * p + k)), pl.BlockSpec((SB, CB), lambda i, k, p: (4 * p + k, k))]
        ospec = pl.BlockSpec(ob, lambda i, k, p: (i, k))
    elif kind == 'in_dw':
        grid, oshape, ob = (8, nt), (8, CB, SB), (CB, SB)
        specs = [pl.BlockSpec((tm, CB), lambda j, t: (t, j % 4)), pl.BlockSpec((tm, SB), lambda j, t: (t, j))]
        ospec = pl.BlockSpec((1, CB, SB), lambda j, t: (j, 0, 0))
    else:
        grid, oshape, ob = (8, nt), (8, SB, CB), (SB, CB)
        specs = [pl.BlockSpec((tm, SB), lambda j, t: (t, j)), pl.BlockSpec((tm, CB), lambda j, t: (t, j % 4))]
        ospec = pl.BlockSpec((1, SB, CB), lambda j, t: (j, 0, 0))
    return pl.pallas_call(
        body, name=name, grid=grid, in_specs=specs, out_specs=ospec, out_shape=jax.ShapeDtypeStruct(oshape, f32),
        scratch_shapes=[pltpu.VMEM(ob, f32)],
        compiler_params=pltpu.CompilerParams(vmem_limit_bytes=VMEM_LIMIT),
    )(a, b)


def _rowcall(name, fn, L, tm, rows, consts=(), out_rows=(), out_accs=(), prev=(), nxt=()):
    nsteps = L // tm
    nb8 = tm // 8
    last8 = L // 8 - 1
    n_r, n_p, n_x, n_c, n_or = len(rows), len(prev), len(nxt), len(consts), len(out_rows)

    def body(*refs):
        i = pl.program_id(0)
        vals = [r[...] for r in refs[:n_r + n_p + n_x + n_c]]
        R, P = vals[:n_r], vals[n_r:n_r + n_p]
        X, C = vals[n_r + n_p:n_r + n_p + n_x], vals[n_r + n_p + n_x:]
        o_refs = refs[n_r + n_p + n_x + n_c:]
        outs_r, outs_a = fn(i, nsteps, R, P, X, C)
        for ref, v in zip(o_refs[:n_or], outs_r, strict=True):
            ref[...] = v.astype(ref.dtype)
        if out_accs:
            @pl.when(i == 0)
            def _():
                for ref in o_refs[n_or:]:
                    ref[...] = jnp.zeros_like(ref)

            for ref, v in zip(o_refs[n_or:], outs_a, strict=True):
                ref[...] += v

    def const_spec(c):
        nd = c.ndim
        return pl.BlockSpec(c.shape, lambda i: (0,) * nd)

    in_specs = ([pl.BlockSpec((tm, a.shape[1]), lambda i: (i, 0)) for a in rows]
                + [pl.BlockSpec((8, rows[j].shape[1]), lambda i: (jnp.maximum(i * nb8 - 1, 0), 0)) for j in prev]
                + [pl.BlockSpec((8, rows[j].shape[1]), lambda i: (jnp.minimum((i + 1) * nb8, last8), 0)) for j in nxt]
                + [const_spec(c) for c in consts])
    out_specs = ([pl.BlockSpec((tm, c), lambda i: (i, 0)) for c, _ in out_rows]
                 + [pl.BlockSpec(s, lambda i: (0, 0)) for s in out_accs])
    out_shape = ([jax.ShapeDtypeStruct((L, c), dt) for c, dt in out_rows]
                 + [jax.ShapeDtypeStruct(s, f32) for s in out_accs])
    args = list(rows) + [rows[j] for j in prev] + [rows[j] for j in nxt] + list(consts)
    return pl.pallas_call(
        body, name=name, grid=(nsteps,), in_specs=in_specs, out_specs=out_specs, out_shape=out_shape,
        compiler_params=pltpu.CompilerParams(dimension_semantics=("arbitrary",), vmem_limit_bytes=VMEM_LIMIT),
    )(*args)


def _shift_down(x, prev8, i, k):
    rolled = pltpu.roll(x, k, axis=0)
    pfix = jnp.where(i > 0, pltpu.roll(prev8, k, axis=0), 0.0)
    row8 = lax.broadcasted_iota(jnp.int32, pfix.shape, 0)
    top = jnp.where(row8 < k, pfix, rolled[:8])
    return jnp.concatenate([top, rolled[8:]], axis=0)


def _shift_up(x, next8, i, nsteps, k):
    tm = x.shape[0]
    rolled = pltpu.roll(x, tm - k, axis=0)
    nfix = jnp.where(i < nsteps - 1, pltpu.roll(next8, 8 - k, axis=0), 0.0)
    row8 = lax.broadcasted_iota(jnp.int32, nfix.shape, 0)
    bot = jnp.where(row8 >= 8 - k, nfix, rolled[tm - 8:])
    return jnp.concatenate([rolled[:tm - 8], bot], axis=0)


def _sum0(x):
    return jnp.sum(x, axis=0, keepdims=True)


def _rms(x, g):
    return x * lax.rsqrt(jnp.mean(x * x, axis=-1, keepdims=True) + NORM_EPS) * g


def _softplus(x):
    return jnp.maximum(x, 0.0) + jnp.log(1.0 + jnp.exp(-jnp.abs(x)))


def _gelu(x):
    return 0.5 * x * (1.0 + jnp.tanh(0.7978845608028654 * (x + 0.044715 * x * x * x)))


def _dot32(a, b):
    return jnp.dot(a, b, preferred_element_type=f32, precision=lax.Precision.HIGHEST)


def _prep(q, w0, a0, k_k, k_a, w2p, a2p, g2, E):
    r, k, v = q[:, 0:512], q[:, 512:1024], q[:, 1024:1536]
    wa, gd = q[:, 1536:1664], q[:, 1664:1792]
    wlog = -_softplus(-(w0 + _dot32(jnp.tanh(wa), w2p))) - 0.5
    lw = -jnp.exp(wlog)
    a = jax.nn.sigmoid(a0 + _dot32(wa, a2p))
    g = _dot32(jax.nn.sigmoid(gd), g2)
    kk = k * k_k
    kkn = kk / jnp.maximum(jnp.sqrt(_dot32(kk * kk, E)), 1e-12)
    k2 = k * (1.0 + (a - 1.0) * k_a)
    return r, lw, k2, v, -kkn, kkn * a, g


def _rwkv_out(y, r, k2, v, g, lnx_w, lnx_b, r_k, E):
    mean = _dot32(y, E) * (1.0 / HEAD)
    yc = y - mean
    var = _dot32(yc * yc, E) * (1.0 / HEAD)
    yn = yc * lax.rsqrt(var + LNX_EPS) * lnx_w + lnx_b
    bonus = _dot32(r * k2 * r_k, E) * v
    return (yn + bonus) * g


def _s5_mid(ysc, u, d):
    return _gelu(ysc + d * u)


def _s5_glu(yg, z2, b_glu):
    return yg * jax.nn.sigmoid(z2 + b_glu)


def _merge(gp, o_r, o_s, b_gate):
    gates = jax.nn.sigmoid(gp + b_gate)
    return gates[:, :D_MODEL] * o_r + gates[:, D_MODEL:] * o_s


def _act(zc):
    return _gelu(zc[:, :D_FF]) * zc[:, D_FF:]


def _s5_disc(a_re, a_im, ls, b_re, b_im):
    dt = jnp.exp(ls)
    er = jnp.exp(a_re * dt)
    ar, ai = er * jnp.cos(a_im * dt), er * jnp.sin(a_im * dt)
    x, y = ar - 1.0, ai
    den = a_re * a_re + a_im * a_im
    fr, fi = (x * a_re + y * a_im) / den, (y * a_re - x * a_im) / den
    return ar, ai, fr * b_re - fi * b_im, fr * b_im + fi * b_re


_DIMS = {'nn': ((1,), (0,)), 'nt': ((1,), (1,)), 'tn': ((0,), (0,))}


def _raw_bdot(a, b, mode):
    return lax.dot_general(a.astype(bf16), b.astype(bf16), (_DIMS[mode], ((), ())), preferred_element_type=f32)


@functools.partial(jax.custom_vjp, nondiff_argnums=(2,))
def _bdot(a, b, mode):
    return _raw_bdot(a, b, mode)


def _bdot_fwd(a, b, mode):
    return _raw_bdot(a, b, mode), (a, b)


def _bdot_bwd(mode, res, g):
    a, b = res
    if mode == 'nn':
        return _raw_bdot(g, b, 'nt'), _raw_bdot(a, g, 'tn')
    if mode == 'nt':
        return _raw_bdot(g, b, 'nn'), _raw_bdot(g, a, 'tn')
    return _raw_bdot(b, g, 'nt'), _raw_bdot(a, g, 'nn')


_bdot.defvjp(_bdot_fwd, _bdot_bwd)


def _wkv_chunk(S0, r, lw, k, v, a, b, tri, bd):
    C = r[0].shape[0]
    P = range(len(r))
    lane = lax.broadcasted_iota(jnp.int32, (1, 2 * HEAD), 1)
    halves = [(lane < HEAD).astype(f32), (lane >= HEAD).astype(f32)]
    eye = (lax.broadcasted_iota(jnp.int32, (C, C), 0) == lax.broadcasted_iota(jnp.int32, (C, C), 1)).astype(f32)
    sl = tri - eye
    cum = [_dot32(tri, lw[p]) for p in P]
    g = [jnp.exp(cum[p]) for p in P]
    gi = [jnp.exp(-cum[p]) for p in P]
    at = [a[p] * jnp.exp(cum[p] - lw[p]) for p in P]
    rt = [r[p] * g[p] for p in P]
    kb = [k[p] * gi[p] for p in P]
    bb = [b[p] * gi[p] for p in P]
    PE = [(p, e) for p in P for e in range(2)]
    atm = {pe: at[pe[0]] * halves[pe[1]] for pe in PE}
    rtm = {pe: rt[pe[0]] * halves[pe[1]] for pe in PE}
    aab = {pe: _bdot(atm[pe], bb[pe[0]], 'nt') * sl for pe in PE}
    aak = {pe: _bdot(atm[pe], kb[pe[0]], 'nt') * sl for pe in PE}
    rk = {pe: _bdot(rtm[pe], kb[pe[0]], 'nt') * tri for pe in PE}
    rb = {pe: _bdot(rtm[pe], bb[pe[0]], 'nt') * tri for pe in PE}
    rhs = [_bdot(at[p], S0[p], 'nt') + sum(halves[e] * _bdot(aak[(p, e)], v[p], 'nn') for e in range(2)) for p in P]
    y0 = [_bdot(rt[p], S0[p], 'nt') + sum(halves[e] * _bdot(rk[(p, e)], v[p], 'nn') for e in range(2)) for p in P]
    x = {pe: eye + aab[pe] for pe in PE}
    pw = aab
    n = 1
    while 2 * n < C:
        pw = {pe: _bdot(pw[pe], pw[pe], 'nn') for pe in PE}
        x = {pe: x[pe] + _bdot(x[pe], pw[pe], 'nn') for pe in PE}
        n *= 2
    u = [sum(halves[e] * _bdot(x[(p, e)], rhs[p], 'nn') for e in range(2)) for p in P]
    y = [y0[p] + sum(halves[e] * _bdot(rb[(p, e)], u[p], 'nn') for e in range(2)) for p in P]
    S1 = [g[p][C - 1:C, :] * (S0[p] + bd * (_bdot(v[p], kb[p], 'tn') + _bdot(u[p], bb[p], 'tn'))) for p in P]
    return y, S1


def _pairs(x):
    return [x[:, 2 * HEAD * p:2 * HEAD * (p + 1)] for p in range(HEADS // 2)]


def _wkv_consts():
    tri = jnp.tril(jnp.ones((WKV_C, WKV_C), f32))
    hid = jnp.arange(2 * HEAD) // HEAD
    return tri, (hid[:, None] == hid[None, :]).astype(f32)


def _wkv7_fwd(r, lw, k, v, a, b):
    L = r.shape[0]
    nc, npair = L // WKV_C, HEADS // 2

    def body(r_ref, lw_ref, k_ref, v_ref, a_ref, b_ref, tri_ref, bd_ref, y_ref, ck_ref, s_ref):
        @pl.when(pl.program_id(0) == 0)
        def _():
            s_ref[...] = jnp.zeros_like(s_ref)

        s0 = [s_ref[p] for p in range(npair)]
        for p in range(npair):
            ck_ref[0, p] = s0[p]
        y, s1 = _wkv_chunk(s0, *(_pairs(x) for x in (r_ref, lw_ref, k_ref, v_ref, a_ref, b_ref)), tri_ref[...], bd_ref[...])
        for p in range(npair):
            y_ref[:, 2 * HEAD * p:2 * HEAD * (p + 1)] = y[p]
            s_ref[p] = s1[p]

    row = pl.BlockSpec((WKV_C, RWKV_W), lambda c: (c, 0))
    sspec = pl.BlockSpec((1, npair, 2 * HEAD, 2 * HEAD), lambda c: (c, 0, 0, 0))
    return pl.pallas_call(
        body, name="wkv7_fwd", grid=(nc,),
        in_specs=[row] * 6 + [pl.BlockSpec((WKV_C, WKV_C), lambda c: (0, 0)), pl.BlockSpec((2 * HEAD, 2 * HEAD), lambda c: (0, 0))],
        out_specs=[row, sspec],
        out_shape=[jax.ShapeDtypeStruct((L, RWKV_W), f32), jax.ShapeDtypeStruct((nc, npair, 2 * HEAD, 2 * HEAD), f32)],
        scratch_shapes=[pltpu.VMEM((npair, 2 * HEAD, 2 * HEAD), f32)],
        compiler_params=pltpu.CompilerParams(dimension_semantics=("arbitrary",), vmem_limit_bytes=VMEM_LIMIT),
    )(r, lw, k, v, a, b, *_wkv_consts())


def _wkv7_bwd(r, lw, k, v, a, b, ck, dy):
    L = r.shape[0]
    nc, npair = L // WKV_C, HEADS // 2

    def body(r_ref, lw_ref, k_ref, v_ref, a_ref, b_ref, ck_ref, dy_ref, tri_ref, bd_ref,
             dr_ref, dlw_ref, dk_ref, dv_ref, da_ref, db_ref, ds_ref):
        @pl.when(pl.program_id(0) == 0)
        def _():
            ds_ref[...] = jnp.zeros_like(ds_ref)

        tri, bd = tri_ref[...], bd_ref[...]
        ins = [[ck_ref[0, p] for p in range(npair)]] + [_pairs(x) for x in (r_ref, lw_ref, k_ref, v_ref, a_ref, b_ref)]
        _, vjp = jax.vjp(lambda *t: _wkv_chunk(*t, tri, bd), *ins)
        gs = vjp((_pairs(dy_ref), [ds_ref[p] for p in range(npair)]))
        for p in range(npair):
            ds_ref[p] = gs[0][p]
            for ref, gval in zip((dr_ref, dlw_ref, dk_ref, dv_ref, da_ref, db_ref), gs[1:]):
                ref[:, 2 * HEAD * p:2 * HEAD * (p + 1)] = gval[p]

    row = pl.BlockSpec((WKV_C, RWKV_W), lambda c: (nc - 1 - c, 0))
    sspec = pl.BlockSpec((1, npair, 2 * HEAD, 2 * HEAD), lambda c: (nc - 1 - c, 0, 0, 0))
    return pl.pallas_call(
        body, name="wkv7_bwd", grid=(nc,),
        in_specs=[row] * 6 + [sspec, row, pl.BlockSpec((WKV_C, WKV_C), lambda c: (0, 0)),
                              pl.BlockSpec((2 * HEAD, 2 * HEAD), lambda c: (0, 0))],
        out_specs=[row] * 6,
        out_shape=[jax.ShapeDtypeStruct((L, RWKV_W), f32)] * 6,
        scratch_shapes=[pltpu.VMEM((npair, 2 * HEAD, 2 * HEAD), f32)],
        compiler_params=pltpu.CompilerParams(dimension_semantics=("arbitrary",), vmem_limit_bytes=VMEM_LIMIT),
    )(r, lw, k, v, a, b, ck, dy, *_wkv_consts())


def _cmul(ar, ai, xr, xi):
    return ar * xr - ai * xi, ar * xi + ai * xr


def _s5_scan(x, abar, reverse, name):
    L = x.shape[0]
    nt = L // S5_T
    ng = S5_T // 8

    def body(x_ref, a_ref, o_ref, car_ref, pw_ref):
        @pl.when(pl.program_id(0) == 0)
        def _():
            car_ref[...] = jnp.zeros_like(car_ref)
            ar = jnp.broadcast_to(a_ref[:, :S5_N], (8, S5_N))
            ai = jnp.broadcast_to(a_ref[:, S5_N:], (8, S5_N))
            if reverse:
                ai = -ai
            row = lax.broadcasted_iota(jnp.int32, (8, S5_N), 0)
            pr, pi = ar, ai
            qr, qi = jnp.zeros((8, S5_N), f32), jnp.zeros((8, S5_N), f32)
            for e in range(1, 9):
                sel = (row == 8 - e) if reverse else (row == e - 1)
                qr, qi = jnp.where(sel, pr, qr), jnp.where(sel, pi, qi)
                if e in (1, 2, 4):
                    j = (1, 2, 4).index(e)
                    pw_ref[j, :, :S5_N] = pr
                    pw_ref[j, :, S5_N:] = pi
                pr, pi = _cmul(pr, pi, ar, ai)
            pw_ref[3, :, :S5_N] = qr
            pw_ref[3, :, S5_N:] = qi

        row = lax.broadcasted_iota(jnp.int32, (8, S5_N), 0)

        def group(gi, carry):
            g = (ng - 1 - gi) if reverse else gi
            t0 = pl.multiple_of(g * 8, 8)
            xr, xi = x_ref[pl.ds(t0, 8), :S5_N], x_ref[pl.ds(t0, 8), S5_N:]
            for j, d in enumerate((1, 2, 4)):
                if reverse:
                    sr = jnp.where(row < 8 - d, pltpu.roll(xr, 8 - d, axis=0), 0.0)
                    si = jnp.where(row < 8 - d, pltpu.roll(xi, 8 - d, axis=0), 0.0)
                else:
                    sr = jnp.where(row >= d, pltpu.roll(xr, d, axis=0), 0.0)
                    si = jnp.where(row >= d, pltpu.roll(xi, d, axis=0), 0.0)
                mr, mi = _cmul(pw_ref[j, :, :S5_N], pw_ref[j, :, S5_N:], sr, si)
                xr, xi = xr + mr, xi + mi
            cr, ci = carry
            mr, mi = _cmul(pw_ref[3, :, :S5_N], pw_ref[3, :, S5_N:], cr, ci)
            xr, xi = xr + mr, xi + mi
            o_ref[pl.ds(t0, 8), :S5_N] = xr
            o_ref[pl.ds(t0, 8), S5_N:] = xi
            e = 0 if reverse else 7
            return (jnp.broadcast_to(xr[e:e + 1, :], (8, S5_N)), jnp.broadcast_to(xi[e:e + 1, :], (8, S5_N)))

        cr, ci = lax.fori_loop(0, ng, group, (car_ref[:, :S5_N], car_ref[:, S5_N:]))
        car_ref[:, :S5_N] = cr
        car_ref[:, S5_N:] = ci

    imap = (lambda i: (nt - 1 - i, 0)) if reverse else (lambda i: (i, 0))
    return pl.pallas_call(
        body, name=name, grid=(nt,),
        in_specs=[pl.BlockSpec((S5_T, 2 * S5_N), imap), pl.BlockSpec((1, 2 * S5_N), lambda i: (0, 0))],
        out_specs=pl.BlockSpec((S5_T, 2 * S5_N), imap),
        out_shape=jax.ShapeDtypeStruct((L, 2 * S5_N), f32),
        scratch_shapes=[pltpu.VMEM((8, 2 * S5_N), f32), pltpu.VMEM((4, 8, 2 * S5_N), f32)],
        compiler_params=pltpu.CompilerParams(dimension_semantics=("arbitrary",), vmem_limit_bytes=VMEM_LIMIT),
    )(x, abar)


def _s5_disc_fwd(a_re, a_im, ls, b_re, b_im):
    def body(a_re_ref, a_im_ref, ls_ref, b_re_ref, b_im_ref, ar_ref, ai_ref, br_ref, bi_ref):
        outs = _s5_disc(a_re_ref[...], a_im_ref[...], ls_ref[...], b_re_ref[...], b_im_ref[...])
        for ref, v in zip((ar_ref, ai_ref, br_ref, bi_ref), outs):
            ref[...] = v

    c1, c16 = jax.ShapeDtypeStruct((S5_N, 1), f32), jax.ShapeDtypeStruct((S5_N, S5_C), f32)
    return pl.pallas_call(body, name="s5_disc", out_shape=[c1, c1, c16, c16])(a_re, a_im, ls, b_re, b_im)


def _s5_disc_bwd(a_re, a_im, ls, b_re, b_im, d_ar, d_ai, d_br, d_bi, seg):
    def body(a_re_ref, a_im_ref, ls_ref, b_re_ref, b_im_ref, g1, g2, g3, g4, seg_ref, o1, o2, o3, o4, o5):
        _, vjp = jax.vjp(_s5_disc, a_re_ref[...], a_im_ref[...], ls_ref[...], b_re_ref[...], b_im_ref[...])
        da_re, da_im, dls, db_re, db_im = vjp((g1[...], g2[...], g3[...], g4[...]))
        o1[...] = da_re
        o2[...] = da_im
        o3[...] = _dot32(seg_ref[...], dls)
        o4[...] = db_re
        o5[...] = db_im

    c1, c16 = jax.ShapeDtypeStruct((S5_N, 1), f32), jax.ShapeDtypeStruct((S5_N, S5_C), f32)
    return pl.pallas_call(body, name="s5_disc_bwd", out_shape=[c1, c1, jax.ShapeDtypeStruct((S5_G, 1), f32), c16, c16])(
        a_re, a_im, ls, b_re, b_im, d_ar, d_ai, d_br, d_bi, seg)


ANY = pl.BlockSpec(memory_space=pl.ANY)

GATHER = {'w_in': ((4352, 1024), 0), 'ffn_w_up': ((1024, 5632), 1), 'w_branch_rwkv': ((512, 1024), 1),
          'w_branch_s5': ((512, 1024), 1), 'w_out': ((1024, 1024), 0), 's5_w_glu': ((512, 512), 0),
          'ffn_w_down': ((2816, 1024), 0), 'rwkv_w2': ((64, 512), 1), 'rwkv_a2': ((64, 512), 1),
          'rwkv_g2': ((128, 512), 1), 'ffn_conv_w': ((8, 5632), 1)}
BIG = ['w_in', 'ffn_w_up', 'w_branch_rwkv', 'w_branch_s5', 'w_out', 's5_w_glu', 'ffn_w_down']
TINY = ['rwkv_w2', 'rwkv_a2', 'rwkv_g2', 'ffn_conv_w']
SMALL = [n for n in WEIGHTS if n not in GATHER]
SMALL_ROWS = 320
ADAM_ROWS = 256


def _mo(v, m):
    return v if isinstance(v, int) else pl.multiple_of(v, m)


def _slab(ref, name, j, h=None):
    (R, Cn), axis = GATHER[name]
    if axis == 0:
        rs = R // 4
        if h is None:
            return ref.at[pl.ds(_mo(j * rs, 16), rs), :]
        return ref.at[pl.ds(_mo(j * rs + h * (rs // 2), 8), rs // 2), :]
    cols = pl.ds(_mo(j * (Cn // 4), 128), Cn // 4)
    if h is None:
        return ref.at[:, cols]
    return ref.at[pl.ds(_mo(h * (R // 2), 8), R // 2), cols]


def _half_shape(name):
    (R, Cn), axis = GATHER[name]
    return (R // 8, Cn) if axis == 0 else (R // 2, Cn // 4)


def _chip_peers(px, py):
    return [((1 - px) if (k >> 1) else px, (1 - py) if (k & 1) else py) for k in (1, 2, 3)]


def _run_copies(copies):
    for cp in copies:
        cp.start()
    for cp in copies:
        cp.wait()


def _gather_weights(blocks):
    names = list(blocks)
    n = len(names)

    def body(*refs):
        ins, outs = refs[:n], refs[n:2 * n]
        ssem, rsem, lsem = refs[2 * n:]
        px, py, pc = _mesh_pos()
        me = 2 * px + py
        copies = []
        for i, nm in enumerate(names):
            if nm in BIG:
                hr = blocks[nm].shape[0] // 2
                src, dst = ins[i].at[pl.ds(pl.multiple_of(pc * hr, 16), hr), :], _slab(outs[i], nm, me, pc)
            else:
                src, dst = ins[i], _slab(outs[i], nm, me)
            copies.append(pltpu.make_async_copy(src, dst, lsem.at[i]))
            for k, (qx, qy) in enumerate(_chip_peers(px, py)):
                copies.append(pltpu.make_async_remote_copy(src, dst, ssem.at[3 * i + k], rsem.at[3 * i + k],
                                                           device_id=(qx, qy, pc), device_id_type=MESH))
        _run_copies(copies)

    outs = pl.pallas_call(
        body, name="gather_weights", in_specs=[ANY] * n, out_specs=[ANY] * n,
        out_shape=[jax.ShapeDtypeStruct(GATHER[nm][0], blocks[nm].dtype) for nm in names],
        scratch_shapes=[pltpu.SemaphoreType.DMA((3 * n,)), pltpu.SemaphoreType.DMA((3 * n,)), pltpu.SemaphoreType.DMA((n,))],
    )(*[blocks[nm] for nm in names])
    return dict(zip(names, outs))


def _gather_pair(full):
    n = len(BIG)

    def body(*refs):
        ins, outs = refs[:n], refs[n:2 * n]
        ssem, rsem = refs[2 * n:]
        px, py, pc = _mesh_pos()
        copies = []
        for i, nm in enumerate(BIG):
            for j in range(4):
                copies.append(pltpu.make_async_remote_copy(_slab(ins[i], nm, j, pc), _slab(outs[i], nm, j, pc),
                                                           ssem.at[4 * i + j], rsem.at[4 * i + j],
                                                           device_id=(px, py, 1 - pc), device_id_type=MESH))
        _run_copies(copies)

    outs = pl.pallas_call(
        body, name="gather_weights_pair", in_specs=[ANY] * n, out_specs=[ANY] * n,
        out_shape=[jax.ShapeDtypeStruct(full[nm].shape, full[nm].dtype) for nm in BIG],
        input_output_aliases={i: i for i in range(n)},
        scratch_shapes=[pltpu.SemaphoreType.DMA((4 * n,)), pltpu.SemaphoreType.DMA((4 * n,))],
    )(*[full[nm] for nm in BIG])
    return dict(zip(BIG, outs))


def _grads_to_sibling(G, small):
    n = len(BIG)

    def body(*refs):
        g_refs, small_ref = refs[:n], refs[n]
        o_refs, small_o = refs[n + 1:2 * n + 1], refs[2 * n + 1]
        ssem, rsem = refs[2 * n + 2:]
        px, py, pc = _mesh_pos()
        sib = (px, py, 1 - pc)
        copies = []
        for i, nm in enumerate(BIG):
            for j in range(4):
                copies.append(pltpu.make_async_remote_copy(_slab(g_refs[i], nm, j, 1 - pc), o_refs[i].at[j],
                                                           ssem.at[4 * i + j], rsem.at[4 * i + j],
                                                           device_id=sib, device_id_type=MESH))
        copies.append(pltpu.make_async_remote_copy(small_ref, small_o, ssem.at[4 * n], rsem.at[4 * n],
                                                   device_id=sib, device_id_type=MESH))
        _run_copies(copies)

    outs = pl.pallas_call(
        body, name="grads_to_sibling", in_specs=[ANY] * (n + 1), out_specs=[ANY] * (n + 1),
        out_shape=[jax.ShapeDtypeStruct((4,) + _half_shape(nm), f32) for nm in BIG] + [jax.ShapeDtypeStruct(small.shape, f32)],
        scratch_shapes=[pltpu.SemaphoreType.DMA((4 * n + 1,)), pltpu.SemaphoreType.DMA((4 * n + 1,))],
    )(*[G[nm] for nm in BIG], small)
    return dict(zip(BIG, outs[:n])), outs[n]


def _pair_add(G, recv, small, small_recv):
    n = len(BIG)
    cidx = lax.axis_index("c").astype(jnp.int32).reshape(1)

    def body(c_ref, *refs):
        ins, outs = refs[:2 * n + 2], refs[2 * n + 2:]
        for i in range(n):
            outs[i][...] = (ins[i][...] + ins[n + i][...]).astype(bf16)
        outs[n][...] = ins[2 * n][...] + ins[2 * n + 1][...]

    g_specs, r_specs = [], []
    for nm in BIG:
        hr, hc = _half_shape(nm)
        if GATHER[nm][1] == 0:
            g_specs.append(pl.BlockSpec((hr // 2, hc), lambda j, i, c: ((2 * j + c[0]) * 2 + i, 0)))
        else:
            g_specs.append(pl.BlockSpec((hr // 2, hc), lambda j, i, c: (2 * c[0] + i, j)))
        r_specs.append(pl.BlockSpec((1, hr // 2, hc), lambda j, i, c: (j, i, 0)))
    sm = pl.BlockSpec((SMALL_ROWS // 8, PACK_W), lambda j, i, c: (2 * j + i, 0))
    outs = pl.pallas_call(
        body, name="grads_pair_sum",
        grid_spec=pltpu.PrefetchScalarGridSpec(num_scalar_prefetch=1, grid=(4, 2), in_specs=g_specs + r_specs + [sm, sm],
                                               out_specs=r_specs + [sm]),
        out_shape=[jax.ShapeDtypeStruct((4,) + _half_shape(nm), bf16) for nm in BIG] + [jax.ShapeDtypeStruct(small.shape, f32)],
        compiler_params=pltpu.CompilerParams(vmem_limit_bytes=VMEM_LIMIT),
    )(cidx, *[G[nm] for nm in BIG], *[recv[nm] for nm in BIG], small, small_recv)
    return dict(zip(BIG, outs[:n])), outs[n]


def _grads_chip_exchange(chip_sum, small):
    n = len(BIG)

    def body(*refs):
        ins, outs = refs[:n + 1], refs[n + 1:2 * n + 2]
        ssem, rsem, lsem = refs[2 * n + 2:]
        px, py, pc = _mesh_pos()
        me = 2 * px + py
        copies = []
        copies.append(pltpu.make_async_copy(ins[n], outs[n].at[me], lsem))
        for i in range(n + 1):
            pick = (lambda ref, j: ref.at[j]) if i < n else (lambda ref, j: ref)
            for k, (qx, qy) in enumerate(_chip_peers(px, py)):
                copies.append(pltpu.make_async_remote_copy(pick(ins[i], 2 * qx + qy), outs[i].at[me],
                                                           ssem.at[3 * i + k], rsem.at[3 * i + k],
                                                           device_id=(qx, qy, pc), device_id_type=MESH))
        _run_copies(copies)

    outs = pl.pallas_call(
        body, name="grads_chip_exchange", in_specs=[ANY] * (n + 1), out_specs=[ANY] * (n + 1),
        out_shape=[jax.ShapeDtypeStruct(chip_sum[nm].shape, chip_sum[nm].dtype) for nm in BIG]
        + [jax.ShapeDtypeStruct((4,) + small.shape, f32)],
        scratch_shapes=[pltpu.SemaphoreType.DMA((3 * n + 3,)), pltpu.SemaphoreType.DMA((3 * n + 3,)),
                        pltpu.SemaphoreType.DMA],
    )(*[chip_sum[nm] for nm in BIG], small)
    return dict(zip(BIG, outs[:n])), outs[n]


def _sum_slots(slots, chip_sum, small4):
    n = len(BIG)
    me = (2 * lax.axis_index("x") + lax.axis_index("y")).astype(jnp.int32).reshape(1)

    def body(me_ref, *refs):
        for i in range(n):
            own = refs[5 * i + 4][0].astype(f32)
            term = [jnp.where(me_ref[0] == k, own, refs[5 * i + k][0].astype(f32)) for k in range(4)]
            refs[5 * n + 1 + i][...] = ((term[0] + term[1]) + term[2]) + term[3]
        x = refs[5 * n]
        refs[6 * n + 1][...] = ((x[0] + x[1]) + x[2]) + x[3]

    in_specs, args, specs_out, shapes = [], [], [], []
    for nm in BIG:
        hr, hc = _half_shape(nm)
        for k in range(4):
            in_specs.append(pl.BlockSpec((1, hr // 2, hc), lambda i, m, k=k: (jnp.where(m[0] == k, (k + 1) % 4, k), i, 0)))
        in_specs.append(pl.BlockSpec((1, hr // 2, hc), lambda i, m: (m[0], i, 0)))
        args += [slots[nm]] * 4 + [chip_sum[nm]]
        specs_out.append(pl.BlockSpec((hr // 2, hc), lambda i, m: (i, 0)))
        shapes.append(jax.ShapeDtypeStruct((hr, hc), f32))
    in_specs.append(pl.BlockSpec((4, SMALL_ROWS // 2, PACK_W), lambda i, m: (0, i, 0)))
    specs_out.append(pl.BlockSpec((SMALL_ROWS // 2, PACK_W), lambda i, m: (i, 0)))
    shapes.append(jax.ShapeDtypeStruct((SMALL_ROWS, PACK_W), f32))
    outs = pl.pallas_call(
        body, name="grads_chip_sum",
        grid_spec=pltpu.PrefetchScalarGridSpec(num_scalar_prefetch=1, grid=(2,), in_specs=in_specs, out_specs=specs_out),
        out_shape=shapes, compiler_params=pltpu.CompilerParams(vmem_limit_bytes=VMEM_LIMIT),
    )(me, *args, small4)
    return dict(zip(BIG, outs[:n])), outs[n]


def _halves_to_sibling(half):
    n = len(BIG)

    def body(*refs):
        ins, outs = refs[:n], refs[n:2 * n]
        ssem, rsem = refs[2 * n:]
        px, py, pc = _mesh_pos()
        _run_copies([pltpu.make_async_remote_copy(ins[i], outs[i], ssem.at[i], rsem.at[i],
                                                  device_id=(px, py, 1 - pc), device_id_type=MESH) for i in range(n)])

    outs = pl.pallas_call(
        body, name="grads_halves_to_sibling", in_specs=[ANY] * n, out_specs=[ANY] * n,
        out_shape=[jax.ShapeDtypeStruct(_half_shape(nm), f32) for nm in BIG],
        scratch_shapes=[pltpu.SemaphoreType.DMA((n,)), pltpu.SemaphoreType.DMA((n,))],
    )(*[half[nm] for nm in BIG])
    return dict(zip(BIG, outs))


def _join_halves(mine, other, pc):
    hr = mine.shape[0]
    return lax.dynamic_slice_in_dim(jnp.concatenate([other, mine, other], axis=0), (1 - pc) * hr, 2 * hr, axis=0)


def _flat_pad(v):
    v = v.reshape(-1)
    return jnp.pad(v, (0, _ceil_to(v.shape[0], PACK_W) - v.shape[0]))


def _pack_rows(parts, rows):
    flat = jnp.concatenate([_flat_pad(p) for p in parts])
    return jnp.pad(flat, (0, rows * PACK_W - flat.shape[0])).reshape(rows, PACK_W)


def _unpack_rows(buf, shapes):
    flat = buf.reshape(-1)
    out, off = [], 0
    for shp in shapes:
        n = 1
        for d in shp:
            n *= d
        out.append(flat[off:off + n].reshape(shp))
        off += _ceil_to(n, PACK_W)
    return out


def _adamw_math(w_, g_, m_, v_):
    m2 = ADAM_B1 * m_ + (1.0 - ADAM_B1) * g_
    v2 = ADAM_B2 * v_ + (1.0 - ADAM_B2) * (g_ * g_)
    m_hat = m2 / (1.0 - ADAM_B1 ** ADAM_STEP)
    v_hat = v2 / (1.0 - ADAM_B2 ** ADAM_STEP)
    return -ADAM_LR * (m_hat / (jnp.sqrt(v_hat) + ADAM_EPS) + ADAM_WD * w_), m2, v2


def _adamw(groups):
    ng = len(groups)

    def body(*refs):
        ins, outs = refs[:4 * ng], refs[4 * ng:]
        for i in range(ng):
            res = _adamw_math(*(r[...] for r in ins[4 * i:4 * i + 4]))
            for ref, val in zip(outs[3 * i:3 * i + 3], res):
                ref[...] = val

    in_specs, out_specs, out_shape = [], [], []
    for grp in groups:
        R, Cn = grp[0].shape
        spec = pl.BlockSpec((R // 8, Cn), lambda i: (i, 0))
        in_specs += [spec] * 4
        out_specs += [spec] * 3
        out_shape += [jax.ShapeDtypeStruct((R, Cn), f32)] * 3
    outs = pl.pallas_call(
        body, name="adamw", grid=(8,), in_specs=in_specs, out_specs=out_specs, out_shape=out_shape,
        compiler_params=pltpu.CompilerParams(vmem_limit_bytes=VMEM_LIMIT),
    )(*[a for grp in groups for a in grp])
    return [tuple(outs[3 * i:3 * i + 3]) for i in range(ng)]


def _forward_backward(x, tgt, W, S):
    L = x.shape[0]
    TM, TMW = 256, 128
    row = lambda c, dt=f32: (c, dt)
    hid = jnp.arange(RWKV_W) // HEAD
    E = (hid[:, None] == hid[None, :]).astype(f32)
    seg = (jnp.arange(S5_N)[None, :] // S5_P == jnp.arange(S5_G)[:, None]).astype(f32)

    w_in_t = W['w_in']
    w_p, w_u, w_g = w_in_t[:N_RWKV], w_in_t[N_RWKV:N_RWKV + S5_W], w_in_t[N_RWKV + S5_W:]
    zpad = jnp.zeros((64, RWKV_W), f32)
    w2p = jnp.concatenate([W['rwkv_w2'], zpad], axis=0)
    a2p = jnp.concatenate([zpad, W['rwkv_a2']], axis=0)
    g2 = W['rwkv_g2']
    prep_consts = [S['rwkv_shift_mu'], S['rwkv_w0'], S['rwkv_a0'], S['rwkv_k_k'], S['rwkv_k_a'], w2p, a2p, g2, E]
    out_consts = [S['rwkv_lnx_w'], S['rwkv_lnx_b'], S['rwkv_r_k'], E]
    cw, cb = W['ffn_conv_w'][:3], S['ffn_conv_b']

    a_re, a_im = S['s5_a_re'].reshape(S5_N, 1), S['s5_a_im'].reshape(S5_N, 1)
    ls = jnp.repeat(S['s5_log_step'].reshape(S5_G, 1), S5_P, axis=0)
    b_re, b_im = S['s5_b_re'].reshape(S5_N, S5_C), S['s5_b_im'].reshape(S5_N, S5_C)
    ar, ai, bbr, bbi = _s5_disc_fwd(a_re, a_im, ls, b_re, b_im)
    abar = jnp.concatenate([ar.reshape(1, S5_N), ai.reshape(1, S5_N)], axis=1)
    eye = jnp.eye(S5_G, dtype=f32)

    def bdiag_in(bb):
        t = bb.reshape(S5_G, S5_P, S5_C).transpose(0, 2, 1)
        return (t[:, :, None, :] * eye[:, None, :, None]).reshape(S5_W, S5_N)

    def bdiag_out(cc):
        t = cc.transpose(0, 2, 1)
        return (t[:, :, None, :] * eye[:, None, :, None]).reshape(S5_N, S5_W)

    eye8 = jnp.eye(8, dtype=f32)

    def undiag_in(blocks):
        t = blocks.reshape(4, 8, S5_C, 8, S5_P)
        t = jnp.sum(t * eye8[None, :, None, :, None], axis=3)
        return t.reshape(S5_G, S5_C, S5_P).transpose(0, 2, 1).reshape(S5_N, S5_C)

    def undiag_out(blocks):
        t = blocks.reshape(4, 8, S5_P, 8, S5_C)
        t = jnp.sum(t * eye8[None, :, None, :, None], axis=3)
        return t.reshape(S5_G, S5_P, S5_C).transpose(0, 2, 1)

    bmat = jnp.concatenate([bdiag_in(bbr), bdiag_in(bbi)], axis=1).astype(bf16)
    cmat = jnp.concatenate([bdiag_out(S['s5_c_re'].reshape(S5_G, S5_C, S5_P)),
                            -bdiag_out(S['s5_c_im'].reshape(S5_G, S5_C, S5_P))], axis=0).astype(bf16)

    g1, g2n, g3, g4 = S['norm_mix_pre'], S['norm_mix_post'], S['norm_ffn_pre'], S['norm_ffn_post']
    (h1,) = _rowcall("norm_pre", lambda i, n, R, P, X, C: ((_rms(R[0], C[0]),), ()), L, TM, [x], [g1],
                     out_rows=[row(D_MODEL, bf16)])
    p = _mm(h1, w_p, 'nt', "mm_p")
    u = _mm(h1, w_u, 'nt', "mm_u")
    gp = _mm(h1, w_g, 'nt', "mm_g")

    def prep_fn(i, n, R, P, X, C):
        q = R[0] + (_shift_down(R[0], P[0], i, 1) - R[0]) * C[0]
        return _prep(q, *C[1:]), ()

    r, lw, k2, v, an, bv, g = _rowcall("rwkv_prep", prep_fn, L, TM, [p], prep_consts,
                                       out_rows=[row(RWKV_W)] * 7, prev=[0])
    y, ck = _wkv7_fwd(r, lw, k2, v, an, bv)
    (o_a,) = _rowcall("rwkv_out", lambda i, n, R, P, X, C: ((_rwkv_out(*R, *C),), ()), L, TM, [y, r, k2, v, g],
                      out_consts, out_rows=[row(RWKV_W, bf16)])
    o_r = _mm(o_a, W['w_branch_rwkv'], 'nn', "mm_br")

    bu = _mm_s5(u, bmat, 'in', "mm_bu")
    st = _s5_scan(bu, abar, False, "s5_scan")
    ysc = _mm_s5(st, cmat, 'out', "mm_cs")
    (yg,) = _rowcall("s5_mid", lambda i, n, R, P, X, C: ((_s5_mid(*R, *C),), ()), L, TM, [ysc, u], [S['s5_d']],
                     out_rows=[row(S5_W)])
    z2 = _mm(yg, W['s5_w_glu'], 'nn', "mm_glu")
    (o_b,) = _rowcall("s5_glu", lambda i, n, R, P, X, C: ((_s5_glu(*R, *C),), ()), L, TM, [yg, z2], [S['s5_b_glu']],
                      out_rows=[row(S5_W, bf16)])
    o_s = _mm(o_b, W['w_branch_s5'], 'nn', "mm_bs")

    (merged,) = _rowcall("merge", lambda i, n, R, P, X, C: ((_merge(*R, *C),), ()), L, TM, [gp, o_r, o_s],
                         [S['b_gate']], out_rows=[row(D_MODEL, bf16)])
    mixed = _mm(merged, W['w_out'], 'nn', "mm_out")

    def resid_fn(i, n, R, P, X, C):
        x1_ = R[0] + _rms(R[1], C[0])
        return (x1_, _rms(x1_, C[1])), ()

    x1, h2 = _rowcall("resid_norm", resid_fn, L, TM, [x, mixed], [g2n, g3], out_rows=[row(D_MODEL), row(D_MODEL, bf16)])

    z = _mm(h2, W['ffn_w_up'], 'nn', "mm_up")

    def conv(zt, zprev, i, cw_, cb_):
        z2s, z1s = _shift_down(zt, zprev, i, 2), _shift_down(zt, zprev, i, 1)
        return cb_ + cw_[0:1] * z2s + cw_[1:2] * z1s + cw_[2:3] * zt, z2s, z1s

    (act,) = _rowcall("conv_act", lambda i, n, R, P, X, C: ((_act(conv(R[0], P[0], i, C[0], C[1])[0]),), ()), L, TMW,
                      [z], [cw, cb], out_rows=[row(D_FF, bf16)], prev=[0])
    f = _mm(act, W['ffn_w_down'], 'nn', "mm_down")

    def final_fn(i, n, R, P, X, C):
        x1_, f_, t_ = R
        fn_, vjp = jax.vjp(_rms, f_, C[0])
        diff = x1_ + fn_ - t_
        loss = jnp.sum(diff * diff) * (0.5 / D_MODEL)
        dx2_ = diff * (1.0 / D_MODEL)
        df_, dg4_ = vjp(dx2_)
        return (df_, dx2_), (jnp.full((1, PACK_W), loss, f32), dg4_)

    df, dx2, loss, dg4 = _rowcall("loss_head", final_fn, L, TM, [x1, f, tgt], [g4],
                                  out_rows=[row(D_MODEL, bf16), row(D_MODEL)], out_accs=[(1, PACK_W), (1, D_MODEL)])
    G = {'norm_ffn_post': dg4}

    dact = _mm(df, W['ffn_w_down'], 'nt', "mm_down_dx")
    G['ffn_w_down'] = _mm(act, df, 'tn', "mm_down_dw")

    def conv_bwd_fn(i, n, R, P, X, C):
        zc, z2s, z1s = conv(R[0], P[0], i, C[0], C[1])
        _, vjp = jax.vjp(_act, zc)
        (dzc_,) = vjp(R[1])
        return (dzc_,), (_sum0(dzc_), _sum0(dzc_ * z2s), _sum0(dzc_ * z1s), _sum0(dzc_ * R[0]))

    wide = (1, 2 * D_FF)
    dzc, dcb, dcw0, dcw1, dcw2 = _rowcall("conv_act_bwd", conv_bwd_fn, L, TMW, [z, dact], [cw, cb],
                                          out_rows=[row(2 * D_FF)], out_accs=[wide] * 4, prev=[0])
    G['ffn_conv_b'] = dcb
    G['ffn_conv_w'] = jnp.concatenate([dcw0, dcw1, dcw2], axis=0)

    def conv_shift_fn(i, n, R, P, X, C):
        d = R[0]
        return (C[0][2:3] * d + C[0][1:2] * _shift_up(d, X[0], i, n, 1) + C[0][0:1] * _shift_up(d, X[0], i, n, 2),), ()

    (dz,) = _rowcall("conv_shift_bwd", conv_shift_fn, L, TMW, [dzc], [cw], out_rows=[row(2 * D_FF, bf16)], nxt=[0])
    dh2 = _mm(dz, W['ffn_w_up'], 'nt', "mm_up_dx")
    G['ffn_w_up'] = _mm(h2, dz, 'tn', "mm_up_dw")

    def norm2_bwd_fn(i, n, R, P, X, C):
        x1_, mixed_, dx2_, dh2_ = R
        _, vjp3 = jax.vjp(_rms, x1_, C[1])
        dx1a, dg3_ = vjp3(dh2_)
        dx1_ = dx2_ + dx1a
        _, vjp2 = jax.vjp(_rms, mixed_, C[0])
        dmixed_, dg2_ = vjp2(dx1_)
        return (dx1_, dmixed_), (dg2_, dg3_)

    dx1, dmixed, dg2n, dg3 = _rowcall("norm_mid_bwd", norm2_bwd_fn, L, TM, [x1, mixed, dx2, dh2], [g2n, g3],
                                      out_rows=[row(D_MODEL), row(D_MODEL, bf16)], out_accs=[(1, D_MODEL)] * 2)
    G['norm_mix_post'], G['norm_ffn_pre'] = dg2n, dg3

    dmerged = _mm(dmixed, W['w_out'], 'nt', "mm_out_dx")
    G['w_out'] = _mm(merged, dmixed, 'tn', "mm_out_dw")

    def merge_bwd_fn(i, n, R, P, X, C):
        _, vjp = jax.vjp(_merge, R[0], R[1], R[2], C[0])
        dgp_, do_r_, do_s_, dbg_ = vjp(R[3])
        return (dgp_, do_r_, do_s_), (dbg_,)

    dgp, do_r, do_s, G['b_gate'] = _rowcall("merge_bwd", merge_bwd_fn, L, TM, [gp, o_r, o_s, dmerged], [S['b_gate']],
                                            out_rows=[row(2 * D_MODEL, bf16), row(D_MODEL, bf16), row(D_MODEL, bf16)],
                                            out_accs=[(1, 2 * D_MODEL)])
    do_a = _mm(do_r, W['w_branch_rwkv'], 'nt', "mm_br_dx")
    G['w_branch_rwkv'] = _mm(o_a, do_r, 'tn', "mm_br_dw")
    do_b = _mm(do_s, W['w_branch_s5'], 'nt', "mm_bs_dx")
    G['w_branch_s5'] = _mm(o_b, do_s, 'tn', "mm_bs_dw")

    def glu_bwd_fn(i, n, R, P, X, C):
        _, vjp = jax.vjp(_s5_glu, R[0], R[1], C[0])
        dyg1_, dz2_, dbg_ = vjp(R[2])
        return (dyg1_, dz2_), (dbg_,)

    dyg1, dz2, G['s5_b_glu'] = _rowcall("s5_glu_bwd", glu_bwd_fn, L, TM, [yg, z2, do_b], [S['s5_b_glu']],
                                        out_rows=[row(S5_W), row(S5_W, bf16)], out_accs=[(1, S5_W)])
    dyg2 = _mm(dz2, W['s5_w_glu'], 'nt', "mm_glu_dx")
    G['s5_w_glu'] = _mm(yg, dz2, 'tn', "mm_glu_dw")

    def mid_bwd_fn(i, n, R, P, X, C):
        _, vjp = jax.vjp(_s5_mid, R[0], R[1], C[0])
        dysc_, du_, dd_ = vjp(R[2] + R[3])
        return (dysc_, du_), (dd_,)

    dysc, du1, G['s5_d'] = _rowcall("s5_mid_bwd", mid_bwd_fn, L, TM, [ysc, u, dyg1, dyg2], [S['s5_d']],
                                    out_rows=[row(S5_W, bf16), row(S5_W)], out_accs=[(1, S5_W)])
    dst = _mm_s5(dysc, cmat, 'out_dx', "mm_cs_dx")
    dcmat = _mm_s5(st, dysc, 'out_dw', "mm_cs_dw")
    lam = _s5_scan(dst, abar, True, "s5_scan_bwd")

    def s5_da_fn(i, n, R, P, X, C):
        lr, li = R[0][:, :S5_N], R[0][:, S5_N:]
        sp = _shift_down(R[1], P[0], i, 1)
        sr, si = sp[:, :S5_N], sp[:, S5_N:]
        return (), (jnp.concatenate([_sum0(lr * sr + li * si), _sum0(li * sr - lr * si)], axis=1),)

    (dabar,) = _rowcall("s5_da", s5_da_fn, L, TM, [lam, st], out_accs=[(1, 2 * S5_N)], prev=[1])
    du2 = _mm_s5(lam, bmat, 'in_dx', "mm_bu_dx")
    dbmat = _mm_s5(u, lam, 'in_dw', "mm_bu_dw")
    da_re, da_im, dls, db_re, db_im = _s5_disc_bwd(
        a_re, a_im, ls, b_re, b_im, dabar[:, :S5_N].reshape(S5_N, 1), dabar[:, S5_N:].reshape(S5_N, 1),
        undiag_in(dbmat[:4]), undiag_in(dbmat[4:]), seg)
    G['s5_a_re'], G['s5_a_im'], G['s5_log_step'] = da_re, da_im, dls
    G['s5_b_re'], G['s5_b_im'] = db_re, db_im
    G['s5_c_re'], G['s5_c_im'] = undiag_out(dcmat[:4]), -undiag_out(dcmat[4:])

    def out_bwd_fn(i, n, R, P, X, C):
        _, vjp = jax.vjp(_rwkv_out, *R[:5], *C)
        gs = vjp(R[5])
        return gs[:5], gs[5:8]

    dy, dr1, dk1, dv1, dg, dlw, dlb, drk = _rowcall("rwkv_out_bwd", out_bwd_fn, L, TM, [y, r, k2, v, g, do_a], out_consts,
                                                    out_rows=[row(RWKV_W)] * 5, out_accs=[(1, RWKV_W)] * 3)
    G['rwkv_lnx_w'], G['rwkv_lnx_b'], G['rwkv_r_k'] = dlw, dlb, drk
    dr2, dlwk, dk2b, dv2, dan, dbv = _wkv7_bwd(r, lw, k2, v, an, bv, ck, dy)

    def prep_bwd_fn(i, n, R, P, X, C):
        p_ = R[0]
        d1 = _shift_down(p_, P[0], i, 1) - p_
        q = p_ + d1 * C[0]
        _, vjp = jax.vjp(_prep, q, *C[1:])
        cots = (R[1] + R[2], R[3], R[4] + R[5], R[6] + R[7], R[8], R[9], R[10])
        gs = vjp(cots)
        return (gs[0],), (_sum0(gs[0] * d1),) + tuple(gs[1:8])

    small, lowr = (1, RWKV_W), (128, RWKV_W)
    dq, dmu, dw0, da0, dkk, dka, dw2p, da2p, dg2 = _rowcall(
        "rwkv_prep_bwd", prep_bwd_fn, L, TM, [p, dr1, dr2, dlwk, dk1, dk2b, dv1, dv2, dan, dbv, dg],
        prep_consts, out_rows=[row(N_RWKV)], out_accs=[(1, N_RWKV)] + [small] * 4 + [lowr] * 3, prev=[0])
    G['rwkv_shift_mu'], G['rwkv_w0'], G['rwkv_a0'], G['rwkv_k_k'], G['rwkv_k_a'] = dmu, dw0, da0, dkk, dka
    G['rwkv_w2'], G['rwkv_a2'], G['rwkv_g2'] = dw2p[:64], da2p[64:], dg2

    def shift_bwd_fn(i, n, R, P, X, C):
        dm = R[0] * C[0]
        return (R[0] - dm + _shift_up(dm, X[0] * C[0], i, n, 1),), ()

    (dp,) = _rowcall("shift_bwd", shift_bwd_fn, L, TM, [dq], [S['rwkv_shift_mu']], out_rows=[row(N_RWKV, bf16)], nxt=[0])

    (du,) = _rowcall("add_du", lambda i, n, R, P, X, C: ((R[0] + R[1],), ()), L, TM, [du1, du2], out_rows=[row(S5_W, bf16)])
    dproj = jnp.concatenate([dp, du, dgp], axis=1)
    dh1 = _mm(dproj, w_in_t, 'nn', "mm_in_dx")
    G['w_in'] = _mm(dproj, h1, 'tn', "mm_in_dw")

    def norm1_bwd_fn(i, n, R, P, X, C):
        _, vjp = jax.vjp(_rms, R[0], C[0])
        dxa, dg1_ = vjp(R[2])
        return (R[1] + dxa,), (dg1_,)

    dx, G['norm_mix_pre'] = _rowcall("norm_pre_bwd", norm1_bwd_fn, L, TM, [x, dx1, dh1], [g1],
                                     out_rows=[row(D_MODEL)], out_accs=[(1, D_MODEL)])
    return loss, dx, G


def kernel(x, norm_mix_pre, norm_mix_post, norm_ffn_pre, norm_ffn_post, w_in, b_gate, rwkv_shift_mu, rwkv_w0, rwkv_w2, rwkv_a0, rwkv_a2, rwkv_g2, rwkv_k_k, rwkv_k_a, rwkv_r_k, rwkv_lnx_w, rwkv_lnx_b, s5_a_re, s5_a_im, s5_b_re, s5_b_im, s5_c_re, s5_c_im, s5_d, s5_log_step, s5_w_glu, s5_b_glu, w_branch_rwkv, w_branch_s5, w_out, ffn_w_up, ffn_conv_w, ffn_conv_b, ffn_w_down, loss_target, m_norm_mix_pre, m_norm_mix_post, m_norm_ffn_pre, m_norm_ffn_post, m_w_in, m_b_gate, m_rwkv_shift_mu, m_rwkv_w0, m_rwkv_w2, m_rwkv_a0, m_rwkv_a2, m_rwkv_g2, m_rwkv_k_k, m_rwkv_k_a, m_rwkv_r_k, m_rwkv_lnx_w, m_rwkv_lnx_b, m_s5_a_re, m_s5_a_im, m_s5_b_re, m_s5_b_im, m_s5_c_re, m_s5_c_im, m_s5_d, m_s5_log_step, m_s5_w_glu, m_s5_b_glu, m_w_branch_rwkv, m_w_branch_s5, m_w_out, m_ffn_w_up, m_ffn_conv_w, m_ffn_conv_b, m_ffn_w_down, v_norm_mix_pre, v_norm_mix_post, v_norm_ffn_pre, v_norm_ffn_post, v_w_in, v_b_gate, v_rwkv_shift_mu, v_rwkv_w0, v_rwkv_w2, v_rwkv_a0, v_rwkv_a2, v_rwkv_g2, v_rwkv_k_k, v_rwkv_k_a, v_rwkv_r_k, v_rwkv_lnx_w, v_rwkv_lnx_b, v_s5_a_re, v_s5_a_im, v_s5_b_re, v_s5_b_im, v_s5_c_re, v_s5_c_im, v_s5_d, v_s5_log_step, v_s5_w_glu, v_s5_b_glu, v_w_branch_rwkv, v_w_branch_s5, v_w_out, v_ffn_w_up, v_ffn_conv_w, v_ffn_conv_b, v_ffn_w_down):
    A = dict(locals())
    me = 2 * lax.axis_index("x") + lax.axis_index("y")
    blk = lambda n: A[n][0]

    mine = {n: (blk(n).T if n == 'w_in' else blk(n)).astype(bf16) for n in BIG}
    mine.update({n: blk(n) for n in TINY})
    mine['ffn_conv_w'] = jnp.pad(blk('ffn_conv_w'), ((0, 5), (0, 0)))
    W = _gather_weights(mine)
    W.update(_gather_pair(W))
    S = {n: A[n].reshape(1, -1) for n in SMALL}

    loss, dx, G = _forward_backward(x[0], loss_target[0], W, S)

    tiny_shapes = [G[n].shape for n in TINY]
    small_buf = _pack_rows([G[n] for n in SMALL] + [G[n] for n in TINY] + [loss], SMALL_ROWS)
    recv, small_recv = _grads_to_sibling(G, small_buf)
    chip_sum, small_sum = _pair_add(G, recv, small_buf, small_recv)
    slots, small4 = _grads_chip_exchange(chip_sum, small_sum)
    half, small_tot = _sum_slots(slots, chip_sum, small4)
    other = _halves_to_sibling(half)
    pc = lax.axis_index("c")
    grad = {n: _join_halves(half[n], other[n], pc) for n in BIG}
    grad['w_in'] = grad['w_in'].T
    vals = _unpack_rows(small_tot, [A[n].shape for n in SMALL] + tiny_shapes + [(1, PACK_W)])
    grad.update(zip(SMALL, vals))
    for n, full in zip(TINY, vals[len(SMALL):]):
        cs = A[n].shape[2]
        grad[n] = lax.dynamic_slice_in_dim(full, me * cs, cs, axis=1)
    loss_out = vals[-1][0, 0]

    packed = SMALL + TINY
    groups = [(blk(n), grad[n], blk('m_' + n), blk('v_' + n)) for n in BIG]
    groups.append(tuple(_pack_rows([src(n) for n in packed], ADAM_ROWS)
                        for src in (lambda n: A[n], lambda n: grad[n], lambda n: A['m_' + n], lambda n: A['v_' + n])))
    res = _adamw(groups)
    outs = [dict(), dict(), dict()]
    for n, r3 in zip(BIG, res[:-1]):
        for d, val in zip(outs, r3):
            d[n] = val
    for d, buf in zip(outs, res[-1]):
        d.update(zip(packed, _unpack_rows(buf, [A[n].shape for n in packed])))
    full = lambda d: [d[n].reshape(A[n].shape) for n in WEIGHTS]
    return (loss_out, dx[None], *full(grad), *full(outs[0]), *full(outs[1]), *full(outs[2]))
```

```python
import functools

import jax
import jax.numpy as jnp
from jax import lax
from jax.experimental import pallas as pl
from jax.experimental.pallas import tpu as pltpu

f32, bf16 = jnp.float32, jnp.bfloat16
MESH = pl.DeviceIdType.MESH

D_MODEL = 1024
RWKV_W = 512
HEADS, HEAD = 8, 64
N_RWKV = 1792
S5_W = 512
S5_G, S5_P, S5_C = 32, 64, 16
S5_N = S5_G * S5_P
D_FF = 2816
NORM_EPS = 1e-6
LNX_EPS = 64e-5
ADAM_LR, ADAM_B1, ADAM_B2, ADAM_EPS, ADAM_WD, ADAM_STEP = 0.001, 0.9, 0.999, 1e-08, 0.01, 10

VMEM_LIMIT = 48 * 1024 * 1024
PACK_W = 1024
WKV_C = 64
S5_T = 256

WEIGHTS = ['norm_mix_pre', 'norm_mix_post', 'norm_ffn_pre', 'norm_ffn_post', 'w_in', 'b_gate', 'rwkv_shift_mu',
           'rwkv_w0', 'rwkv_w2', 'rwkv_a0', 'rwkv_a2', 'rwkv_g2', 'rwkv_k_k', 'rwkv_k_a', 'rwkv_r_k', 'rwkv_lnx_w',
           'rwkv_lnx_b', 's5_a_re', 's5_a_im', 's5_b_re', 's5_b_im', 's5_c_re', 's5_c_im', 's5_d', 's5_log_step',
           's5_w_glu', 's5_b_glu', 'w_branch_rwkv', 'w_branch_s5', 'w_out', 'ffn_w_up', 'ffn_conv_w', 'ffn_conv_b',
           'ffn_w_down']


def _ceil_to(n, m):
    return -(-n // m) * m


def _mesh_pos():
    return lax.axis_index("x"), lax.axis_index("y"), lax.axis_index("c")


def _pick(d, cap=4096):
    for c in (1024, 1408, 2176, 896, 512, 256, 128):
        if c <= cap and d % c == 0:
            return c
    raise ValueError(d)


def _mm(a, b, mode, name, out_dtype=f32):
    if mode == 'tn':
        (K, M), (K2, N) = a.shape, b.shape
    elif mode == 'nt':
        (M, K), (N, K2) = a.shape, b.shape
    else:
        (M, K), (K2, N) = a.shape, b.shape
    assert K == K2, (name, a.shape, b.shape)
    if mode == 'tn':
        tm = _pick(M, 2176)
        tn = _pick(N, 512 if tm > 1408 else (1024 if tm > 1024 else 1408))
        tk = _pick(K, 512)
    else:
        tm, tn, tk = _pick(M, 512), _pick(N), _pick(K)
    nk = K // tk
    dims = {'nn': ((1,), (0,)), 'nt': ((1,), (1,)), 'tn': ((0,), (0,))}[mode]

    def body(a_ref, b_ref, o_ref, acc_ref):
        k = pl.program_id(2)

        @pl.when(k == 0)
        def _():
            acc_ref[...] = jnp.zeros_like(acc_ref)

        acc_ref[...] += lax.dot_general(a_ref[...].astype(bf16), b_ref[...].astype(bf16), (dims, ((), ())),
                                        preferred_element_type=f32)

        @pl.when(k == nk - 1)
        def _():
            o_ref[...] = acc_ref[...].astype(o_ref.dtype)

    a_spec = pl.BlockSpec((tk, tm), lambda i, j, k: (k, i)) if mode == 'tn' else pl.BlockSpec((tm, tk), lambda i, j, k: (i, k))
    b_spec = pl.BlockSpec((tn, tk), lambda i, j, k: (j, k)) if mode == 'nt' else pl.BlockSpec((tk, tn), lambda i, j, k: (k, j))
    return pl.pallas_call(
        body, name=name, grid=(M // tm, N // tn, nk),
        in_specs=[a_spec, b_spec], out_specs=pl.BlockSpec((tm, tn), lambda i, j, k: (i, j)),
        out_shape=jax.ShapeDtypeStruct((M, N), out_dtype),
        scratch_shapes=[pltpu.VMEM((tm, tn), f32)],
        compiler_params=pltpu.CompilerParams(dimension_semantics=("parallel", "parallel", "arbitrary"),
                                             vmem_limit_bytes=VMEM_LIMIT),
    )(a, b)


def _rowcall(name, fn, L, tm, rows, consts=(), out_rows=(), out_accs=(), prev=(), nxt=()):
    nsteps = L // tm
    nb8 = tm // 8
    last8 = L // 8 - 1
    n_r, n_p, n_x, n_c, n_or = len(rows), len(prev), len(nxt), len(consts), len(out_rows)

    def body(*refs):
        i = pl.program_id(0)
        vals = [r[...] for r in refs[:n_r + n_p + n_x + n_c]]
        R, P = vals[:n_r], vals[n_r:n_r + n_p]
        X, C = vals[n_r + n_p:n_r + n_p + n_x], vals[n_r + n_p + n_x:]
        o_refs = refs[n_r + n_p + n_x + n_c:]
        outs_r, outs_a = fn(i, nsteps, R, P, X, C)
        for ref, v in zip(o_refs[:n_or], outs_r, strict=True):
            ref[...] = v.astype(ref.dtype)
        if out_accs:
            @pl.when(i == 0)
            def _():
                for ref in o_refs[n_or:]:
                    ref[...] = jnp.zeros_like(ref)

            for ref, v in zip(o_refs[n_or:], outs_a, strict=True):
                ref[...] += v

    def const_spec(c):
        nd = c.ndim
        return pl.BlockSpec(c.shape, lambda i: (0,) * nd)

    in_specs = ([pl.BlockSpec((tm, a.shape[1]), lambda i: (i, 0)) for a in rows]
                + [pl.BlockSpec((8, rows[j].shape[1]), lambda i: (jnp.maximum(i * nb8 - 1, 0), 0)) for j in prev]
                + [pl.BlockSpec((8, rows[j].shape[1]), lambda i: (jnp.minimum((i + 1) * nb8, last8), 0)) for j in nxt]
                + [const_spec(c) for c in consts])
    out_specs = ([pl.BlockSpec((tm, c), lambda i: (i, 0)) for c, _ in out_rows]
                 + [pl.BlockSpec(s, lambda i: (0, 0)) for s in out_accs])
    out_shape = ([jax.ShapeDtypeStruct((L, c), dt) for c, dt in out_rows]
                 + [jax.ShapeDtypeStruct(s, f32) for s in out_accs])
    args = list(rows) + [rows[j] for j in prev] + [rows[j] for j in nxt] + list(consts)
    return pl.pallas_call(
        body, name=name, grid=(nsteps,), in_specs=in_specs, out_specs=out_specs, out_shape=out_shape,
        compiler_params=pltpu.CompilerParams(dimension_semantics=("arbitrary",), vmem_limit_bytes=VMEM_LIMIT),
    )(*args)


def _shift_down(x, prev8, i, k):
    rolled = pltpu.roll(x, k, axis=0)
    pfix = jnp.where(i > 0, pltpu.roll(prev8, k, axis=0), 0.0)
    row8 = lax.broadcasted_iota(jnp.int32, pfix.shape, 0)
    top = jnp.where(row8 < k, pfix, rolled[:8])
    return jnp.concatenate([top, rolled[8:]], axis=0)


def _shift_up(x, next8, i, nsteps, k):
    tm = x.shape[0]
    rolled = pltpu.roll(x, tm - k, axis=0)
    nfix = jnp.where(i < nsteps - 1, pltpu.roll(next8, 8 - k, axis=0), 0.0)
    row8 = lax.broadcasted_iota(jnp.int32, nfix.shape, 0)
    bot = jnp.where(row8 >= 8 - k, nfix, rolled[tm - 8:])
    return jnp.concatenate([rolled[:tm - 8], bot], axis=0)


def _sum0(x):
    return jnp.sum(x, axis=0, keepdims=True)


def _rms(x, g):
    return x * lax.rsqrt(jnp.mean(x * x, axis=-1, keepdims=True) + NORM_EPS) * g


def _softplus(x):
    return jnp.maximum(x, 0.0) + jnp.log(1.0 + jnp.exp(-jnp.abs(x)))


def _gelu(x):
    return 0.5 * x * (1.0 + jnp.tanh(0.7978845608028654 * (x + 0.044715 * x * x * x)))


def _dot32(a, b):
    return jnp.dot(a, b, preferred_element_type=f32, precision=lax.Precision.HIGHEST)


def _prep(q, w0, a0, k_k, k_a, w2p, a2p, g2, E):
    r, k, v = q[:, 0:512], q[:, 512:1024], q[:, 1024:1536]
    wa, gd = q[:, 1536:1664], q[:, 1664:1792]
    wlog = -_softplus(-(w0 + _dot32(jnp.tanh(wa), w2p))) - 0.5
    lw = -jnp.exp(wlog)
    a = jax.nn.sigmoid(a0 + _dot32(wa, a2p))
    g = _dot32(jax.nn.sigmoid(gd), g2)
    kk = k * k_k
    kkn = kk / jnp.maximum(jnp.sqrt(_dot32(kk * kk, E)), 1e-12)
    k2 = k * (1.0 + (a - 1.0) * k_a)
    return r, lw, k2, v, -kkn, kkn * a, g


def _rwkv_out(y, r, k2, v, g, lnx_w, lnx_b, r_k, E):
    mean = _dot32(y, E) * (1.0 / HEAD)
    yc = y - mean
    var = _dot32(yc * yc, E) * (1.0 / HEAD)
    yn = yc * lax.rsqrt(var + LNX_EPS) * lnx_w + lnx_b
    bonus = _dot32(r * k2 * r_k, E) * v
    return (yn + bonus) * g


def _s5_mid(ysc, u, d):
    return _gelu(ysc + d * u)


def _s5_glu(yg, z2, b_glu):
    return yg * jax.nn.sigmoid(z2 + b_glu)


def _merge(gp, o_r, o_s, b_gate):
    gates = jax.nn.sigmoid(gp + b_gate)
    return gates[:, :D_MODEL] * o_r + gates[:, D_MODEL:] * o_s


def _act(zc):
    return _gelu(zc[:, :D_FF]) * zc[:, D_FF:]


def _s5_disc(a_re, a_im, ls, b_re, b_im):
    dt = jnp.exp(ls)
    er = jnp.exp(a_re * dt)
    ar, ai = er * jnp.cos(a_im * dt), er * jnp.sin(a_im * dt)
    x, y = ar - 1.0, ai
    den = a_re * a_re + a_im * a_im
    fr, fi = (x * a_re + y * a_im) / den, (y * a_re - x * a_im) / den
    return ar, ai, fr * b_re - fi * b_im, fr * b_im + fi * b_re


_DIMS = {'nn': ((1,), (0,)), 'nt': ((1,), (1,)), 'tn': ((0,), (0,))}


def _raw_bdot(a, b, mode):
    return lax.dot_general(a.astype(bf16), b.astype(bf16), (_DIMS[mode], ((), ())), preferred_element_type=f32)


@functools.partial(jax.custom_vjp, nondiff_argnums=(2,))
def _bdot(a, b, mode):
    return _raw_bdot(a, b, mode)


def _bdot_fwd(a, b, mode):
    return _raw_bdot(a, b, mode), (a, b)


def _bdot_bwd(mode, res, g):
    a, b = res
    if mode == 'nn':
        return _raw_bdot(g, b, 'nt'), _raw_bdot(a, g, 'tn')
    if mode == 'nt':
        return _raw_bdot(g, b, 'nn'), _raw_bdot(g, a, 'tn')
    return _raw_bdot(b, g, 'nt'), _raw_bdot(a, g, 'nn')


_bdot.defvjp(_bdot_fwd, _bdot_bwd)


def _wkv_chunk(S0, r, lw, k, v, a, b, tri, bd):
    C = r[0].shape[0]
    P = range(len(r))
    lane = lax.broadcasted_iota(jnp.int32, (1, 2 * HEAD), 1)
    halves = [(lane < HEAD).astype(f32), (lane >= HEAD).astype(f32)]
    eye = (lax.broadcasted_iota(jnp.int32, (C, C), 0) == lax.broadcasted_iota(jnp.int32, (C, C), 1)).astype(f32)
    sl = tri - eye
    cum = [_dot32(tri, lw[p]) for p in P]
    g = [jnp.exp(cum[p]) for p in P]
    gi = [jnp.exp(-cum[p]) for p in P]
    at = [a[p] * jnp.exp(cum[p] - lw[p]) for p in P]
    rt = [r[p] * g[p] for p in P]
    kb = [k[p] * gi[p] for p in P]
    bb = [b[p] * gi[p] for p in P]
    PE = [(p, e) for p in P for e in range(2)]
    atm = {pe: at[pe[0]] * halves[pe[1]] for pe in PE}
    rtm = {pe: rt[pe[0]] * halves[pe[1]] for pe in PE}
    aab = {pe: _bdot(atm[pe], bb[pe[0]], 'nt') * sl for pe in PE}
    aak = {pe: _bdot(atm[pe], kb[pe[0]], 'nt') * sl for pe in PE}
    rk = {pe: _bdot(rtm[pe], kb[pe[0]], 'nt') * tri for pe in PE}
    rb = {pe: _bdot(rtm[pe], bb[pe[0]], 'nt') * tri for pe in PE}
    rhs = [_bdot(at[p], S0[p], 'nt') + sum(halves[e] * _bdot(aak[(p, e)], v[p], 'nn') for e in range(2)) for p in P]
    y0 = [_bdot(rt[p], S0[p], 'nt') + sum(halves[e] * _bdot(rk[(p, e)], v[p], 'nn') for e in range(2)) for p in P]
    x = {pe: eye + aab[pe] for pe in PE}
    pw = aab
    n = 1
    while 2 * n < C:
        pw = {pe: _bdot(pw[pe], pw[pe], 'nn') for pe in PE}
        x = {pe: x[pe] + _bdot(x[pe], pw[pe], 'nn') for pe in PE}
        n *= 2
    u = [sum(halves[e] * _bdot(x[(p, e)], rhs[p], 'nn') for e in range(2)) for p in P]
    y = [y0[p] + sum(halves[e] * _bdot(rb[(p, e)], u[p], 'nn') for e in range(2)) for p in P]
    S1 = [g[p][C - 1:C, :] * (S0[p] + bd * (_bdot(v[p], kb[p], 'tn') + _bdot(u[p], bb[p], 'tn'))) for p in P]
    return y, S1


def _pairs(x):
    return [x[:, 2 * HEAD * p:2 * HEAD * (p + 1)] for p in range(HEADS // 2)]


def _wkv_consts():
    tri = jnp.tril(jnp.ones((WKV_C, WKV_C), f32))
    hid = jnp.arange(2 * HEAD) // HEAD
    return tri, (hid[:, None] == hid[None, :]).astype(f32)


def _wkv7_fwd(r, lw, k, v, a, b):
    L = r.shape[0]
    nc, npair = L // WKV_C, HEADS // 2

    def body(r_ref, lw_ref, k_ref, v_ref, a_ref, b_ref, tri_ref, bd_ref, y_ref, ck_ref, s_ref):
        @pl.when(pl.program_id(0) == 0)
        def _():
            s_ref[...] = jnp.zeros_like(s_ref)

        s0 = [s_ref[p] for p in range(npair)]
        for p in range(npair):
            ck_ref[0, p] = s0[p]
        y, s1 = _wkv_chunk(s0, *(_pairs(x) for x in (r_ref, lw_ref, k_ref, v_ref, a_ref, b_ref)), tri_ref[...], bd_ref[...])
        for p in range(npair):
            y_ref[:, 2 * HEAD * p:2 * HEAD * (p + 1)] = y[p]
            s_ref[p] = s1[p]

    row = pl.BlockSpec((WKV_C, RWKV_W), lambda c: (c, 0))
    sspec = pl.BlockSpec((1, npair, 2 * HEAD, 2 * HEAD), lambda c: (c, 0, 0, 0))
    return pl.pallas_call(
        body, name="wkv7_fwd", grid=(nc,),
        in_specs=[row] * 6 + [pl.BlockSpec((WKV_C, WKV_C), lambda c: (0, 0)), pl.BlockSpec((2 * HEAD, 2 * HEAD), lambda c: (0, 0))],
        out_specs=[row, sspec],
        out_shape=[jax.ShapeDtypeStruct((L, RWKV_W), f32), jax.ShapeDtypeStruct((nc, npair, 2 * HEAD, 2 * HEAD), f32)],
        scratch_shapes=[pltpu.VMEM((npair, 2 * HEAD, 2 * HEAD), f32)],
        compiler_params=pltpu.CompilerParams(dimension_semantics=("arbitrary",), vmem_limit_bytes=VMEM_LIMIT),
    )(r, lw, k, v, a, b, *_wkv_consts())


def _wkv7_bwd(r, lw, k, v, a, b, ck, dy):
    L = r.shape[0]
    nc, npair = L // WKV_C, HEADS // 2

    def body(r_ref, lw_ref, k_ref, v_ref, a_ref, b_ref, ck_ref, dy_ref, tri_ref, bd_ref,
             dr_ref, dlw_ref, dk_ref, dv_ref, da_ref, db_ref, ds_ref):
        @pl.when(pl.program_id(0) == 0)
        def _():
            ds_ref[...] = jnp.zeros_like(ds_ref)

        tri, bd = tri_ref[...], bd_ref[...]
        ins = [[ck_ref[0, p] for p in range(npair)]] + [_pairs(x) for x in (r_ref, lw_ref, k_ref, v_ref, a_ref, b_ref)]
        _, vjp = jax.vjp(lambda *t: _wkv_chunk(*t, tri, bd), *ins)
        gs = vjp((_pairs(dy_ref), [ds_ref[p] for p in range(npair)]))
        for p in range(npair):
            ds_ref[p] = gs[0][p]
            for ref, gval in zip((dr_ref, dlw_ref, dk_ref, dv_ref, da_ref, db_ref), gs[1:]):
                ref[:, 2 * HEAD * p:2 * HEAD * (p + 1)] = gval[p]

    row = pl.BlockSpec((WKV_C, RWKV_W), lambda c: (nc - 1 - c, 0))
    sspec = pl.BlockSpec((1, npair, 2 * HEAD, 2 * HEAD), lambda c: (nc - 1 - c, 0, 0, 0))
    return pl.pallas_call(
        body, name="wkv7_bwd", grid=(nc,),
        in_specs=[row] * 6 + [sspec, row, pl.BlockSpec((WKV_C, WKV_C), lambda c: (0, 0)),
                              pl.BlockSpec((2 * HEAD, 2 * HEAD), lambda c: (0, 0))],
        out_specs=[row] * 6,
        out_shape=[jax.ShapeDtypeStruct((L, RWKV_W), f32)] * 6,
        scratch_shapes=[pltpu.VMEM((npair, 2 * HEAD, 2 * HEAD), f32)],
        compiler_params=pltpu.CompilerParams(dimension_semantics=("arbitrary",), vmem_limit_bytes=VMEM_LIMIT),
    )(r, lw, k, v, a, b, ck, dy, *_wkv_consts())


def _cmul(ar, ai, xr, xi):
    return ar * xr - ai * xi, ar * xi + ai * xr


def _scan_init(a_ref, car_ref, pw_ref, reverse):
    car_ref[...] = jnp.zeros_like(car_ref)
    ar = jnp.broadcast_to(a_ref[:, :S5_N], (8, S5_N))
    ai = jnp.broadcast_to(a_ref[:, S5_N:], (8, S5_N))
    if reverse:
        ai = -ai
    row = lax.broadcasted_iota(jnp.int32, (8, S5_N), 0)
    pr, pi = ar, ai
    qr, qi = jnp.zeros((8, S5_N), f32), jnp.zeros((8, S5_N), f32)
    for e in range(1, 9):
        sel = (row == 8 - e) if reverse else (row == e - 1)
        qr, qi = jnp.where(sel, pr, qr), jnp.where(sel, pi, qi)
        if e in (1, 2, 4):
            j = (1, 2, 4).index(e)
            pw_ref[j, :, :S5_N] = pr
            pw_ref[j, :, S5_N:] = pi
        pr, pi = _cmul(pr, pi, ar, ai)
    pw_ref[3, :, :S5_N] = qr
    pw_ref[3, :, S5_N:] = qi


def _scan_tile(x_ref, o_ref, car_ref, pw_ref, reverse):
    ng = x_ref.shape[0] // 8
    row = lax.broadcasted_iota(jnp.int32, (8, S5_N), 0)

    def group(gi, carry):
        g = (ng - 1 - gi) if reverse else gi
        t0 = pl.multiple_of(g * 8, 8)
        xr, xi = x_ref[pl.ds(t0, 8), :S5_N], x_ref[pl.ds(t0, 8), S5_N:]
        for j, d in enumerate((1, 2, 4)):
            if reverse:
                sr = jnp.where(row < 8 - d, pltpu.roll(xr, 8 - d, axis=0), 0.0)
                si = jnp.where(row < 8 - d, pltpu.roll(xi, 8 - d, axis=0), 0.0)
            else:
                sr = jnp.where(row >= d, pltpu.roll(xr, d, axis=0), 0.0)
                si = jnp.where(row >= d, pltpu.roll(xi, d, axis=0), 0.0)
            mr, mi = _cmul(pw_ref[j, :, :S5_N], pw_ref[j, :, S5_N:], sr, si)
            xr, xi = xr + mr, xi + mi
        cr, ci = carry
        mr, mi = _cmul(pw_ref[3, :, :S5_N], pw_ref[3, :, S5_N:], cr, ci)
        xr, xi = xr + mr, xi + mi
        o_ref[pl.ds(t0, 8), :S5_N] = xr
        o_ref[pl.ds(t0, 8), S5_N:] = xi
        e = 0 if reverse else 7
        return (jnp.broadcast_to(xr[e:e + 1, :], (8, S5_N)), jnp.broadcast_to(xi[e:e + 1, :], (8, S5_N)))

    cr, ci = lax.fori_loop(0, ng, group, (car_ref[:, :S5_N], car_ref[:, S5_N:]))
    car_ref[:, :S5_N] = cr
    car_ref[:, S5_N:] = ci


_CB, _SB = 128, 512


def _cblk(k):
    return slice(_CB * k, _CB * (k + 1))


def _sblk(j):
    return slice(_SB * j, _SB * (j + 1))


def _s5_fwd(u, bmat, cmat, abar):
    L = u.shape[0]
    nt = L // S5_T

    def body(u_ref, b_ref, c_ref, a_ref, st_ref, y_ref, bu_ref, car_ref, pw_ref):
        @pl.when(pl.program_id(0) == 0)
        def _():
            _scan_init(a_ref, car_ref, pw_ref, False)

        for j in range(8):
            bu_ref[:, _sblk(j)] = _raw_bdot(u_ref[:, _cblk(j % 4)], b_ref[_cblk(j % 4), _sblk(j)], 'nn')
        _scan_tile(bu_ref, st_ref, car_ref, pw_ref, False)
        for k in range(4):
            y_ref[:, _cblk(k)] = (_raw_bdot(st_ref[:, _sblk(k)], c_ref[_sblk(k), _cblk(k)], 'nn')
                                  + _raw_bdot(st_ref[:, _sblk(4 + k)], c_ref[_sblk(4 + k), _cblk(k)], 'nn'))

    whole = lambda shape: pl.BlockSpec(shape, lambda i: (0, 0))
    return pl.pallas_call(
        body, name="s5_fwd", grid=(nt,),
        in_specs=[pl.BlockSpec((S5_T, S5_W), lambda i: (i, 0)), whole(bmat.shape), whole(cmat.shape), whole(abar.shape)],
        out_specs=[pl.BlockSpec((S5_T, 2 * S5_N), lambda i: (i, 0)), pl.BlockSpec((S5_T, S5_W), lambda i: (i, 0))],
        out_shape=[jax.ShapeDtypeStruct((L, 2 * S5_N), f32), jax.ShapeDtypeStruct((L, S5_W), f32)],
        scratch_shapes=[pltpu.VMEM((S5_T, 2 * S5_N), f32), pltpu.VMEM((8, 2 * S5_N), f32), pltpu.VMEM((4, 8, 2 * S5_N), f32)],
        compiler_params=pltpu.CompilerParams(dimension_semantics=("arbitrary",), vmem_limit_bytes=VMEM_LIMIT),
    )(u, bmat, cmat, abar)


def _s5_bwd(dy, st, u, bmat, cmat, abar):
    L = u.shape[0]
    nt = L // S5_T
    nb8 = S5_T // 8

    def body(dy_ref, st_ref, sp_ref, u_ref, b_ref, c_ref, a_ref, du_ref, db_ref, dc_ref, da_ref, lam_ref, car_ref, pw_ref):
        i = pl.program_id(0)

        @pl.when(i == 0)
        def _():
            _scan_init(a_ref, car_ref, pw_ref, True)
            db_ref[...] = jnp.zeros_like(db_ref)
            dc_ref[...] = jnp.zeros_like(dc_ref)
            da_ref[...] = jnp.zeros_like(da_ref)

        for j in range(8):
            lam_ref[:, _sblk(j)] = _raw_bdot(dy_ref[:, _cblk(j % 4)], c_ref[_sblk(j), _cblk(j % 4)], 'nt')
        _scan_tile(lam_ref, lam_ref, car_ref, pw_ref, True)
        for k in range(4):
            du_ref[:, _cblk(k)] = (_raw_bdot(lam_ref[:, _sblk(k)], b_ref[_cblk(k), _sblk(k)], 'nt')
                                   + _raw_bdot(lam_ref[:, _sblk(4 + k)], b_ref[_cblk(k), _sblk(4 + k)], 'nt'))
            sr = _shift_down(st_ref[:, _sblk(k)], sp_ref[:, _sblk(k)], nt - 1 - i, 1)
            si = _shift_down(st_ref[:, _sblk(4 + k)], sp_ref[:, _sblk(4 + k)], nt - 1 - i, 1)
            lr, li = lam_ref[:, _sblk(k)], lam_ref[:, _sblk(4 + k)]
            da_ref[:, _sblk(k)] += _sum0(lr * sr + li * si)
            da_ref[:, _sblk(4 + k)] += _sum0(li * sr - lr * si)
        for j in range(8):
            db_ref[j] += _raw_bdot(u_ref[:, _cblk(j % 4)], lam_ref[:, _sblk(j)], 'tn')
            dc_ref[j] += _raw_bdot(st_ref[:, _sblk(j)], dy_ref[:, _cblk(j % 4)], 'tn')

    whole = lambda shape: pl.BlockSpec(shape, lambda i: (0,) * len(shape))
    rev = lambda i: (nt - 1 - i, 0)
    return pl.pallas_call(
        body, name="s5_bwd", grid=(nt,),
        in_specs=[pl.BlockSpec((S5_T, S5_W), rev), pl.BlockSpec((S5_T, 2 * S5_N), rev),
                  pl.BlockSpec((8, 2 * S5_N), lambda i: (jnp.maximum((nt - 1 - i) * nb8 - 1, 0), 0)),
                  pl.BlockSpec((S5_T, S5_W), rev), whole(bmat.shape), whole(cmat.shape), whole(abar.shape)],
        out_specs=[pl.BlockSpec((S5_T, S5_W), rev), whole((8, _CB, _SB)), whole((8, _SB, _CB)), whole((1, 2 * S5_N))],
        out_shape=[jax.ShapeDtypeStruct((L, S5_W), f32), jax.ShapeDtypeStruct((8, _CB, _SB), f32),
                   jax.ShapeDtypeStruct((8, _SB, _CB), f32), jax.ShapeDtypeStruct((1, 2 * S5_N), f32)],
        scratch_shapes=[pltpu.VMEM((S5_T, 2 * S5_N), f32), pltpu.VMEM((8, 2 * S5_N), f32), pltpu.VMEM((4, 8, 2 * S5_N), f32)],
        compiler_params=pltpu.CompilerParams(dimension_semantics=("arbitrary",), vmem_limit_bytes=VMEM_LIMIT),
    )(dy, st, st, u, bmat, cmat, abar)


def _s5_disc_fwd(a_re, a_im, ls, b_re, b_im):
    def body(a_re_ref, a_im_ref, ls_ref, b_re_ref, b_im_ref, ar_ref, ai_ref, br_ref, bi_ref):
        outs = _s5_disc(a_re_ref[...], a_im_ref[...], ls_ref[...], b_re_ref[...], b_im_ref[...])
        for ref, v in zip((ar_ref, ai_ref, br_ref, bi_ref), outs):
            ref[...] = v

    c1, c16 = jax.ShapeDtypeStruct((S5_N, 1), f32), jax.ShapeDtypeStruct((S5_N, S5_C), f32)
    return pl.pallas_call(body, name="s5_disc", out_shape=[c1, c1, c16, c16])(a_re, a_im, ls, b_re, b_im)


def _s5_disc_bwd(a_re, a_im, ls, b_re, b_im, d_ar, d_ai, d_br, d_bi, seg):
    def body(a_re_ref, a_im_ref, ls_ref, b_re_ref, b_im_ref, g1, g2, g3, g4, seg_ref, o1, o2, o3, o4, o5):
        _, vjp = jax.vjp(_s5_disc, a_re_ref[...], a_im_ref[...], ls_ref[...], b_re_ref[...], b_im_ref[...])
        da_re, da_im, dls, db_re, db_im = vjp((g1[...], g2[...], g3[...], g4[...]))
        o1[...] = da_re
        o2[...] = da_im
        o3[...] = _dot32(seg_ref[...], dls)
        o4[...] = db_re
        o5[...] = db_im

    c1, c16 = jax.ShapeDtypeStruct((S5_N, 1), f32), jax.ShapeDtypeStruct((S5_N, S5_C), f32)
    return pl.pallas_call(body, name="s5_disc_bwd", out_shape=[c1, c1, jax.ShapeDtypeStruct((S5_G, 1), f32), c16, c16])(
        a_re, a_im, ls, b_re, b_im, d_ar, d_ai, d_br, d_bi, seg)


ANY = pl.BlockSpec(memory_space=pl.ANY)

GATHER = {'w_in': ((4352, 1024), 0), 'ffn_w_up': ((1024, 5632), 1), 'w_branch_rwkv': ((512, 1024), 1),
          'w_branch_s5': ((512, 1024), 1), 'w_out': ((1024, 1024), 0), 's5_w_glu': ((512, 512), 0),
          'ffn_w_down': ((2816, 1024), 0), 'rwkv_w2': ((64, 512), 1), 'rwkv_a2': ((64, 512), 1),
          'rwkv_g2': ((128, 512), 1), 'ffn_conv_w': ((8, 5632), 1)}
BIG = ['w_in', 'ffn_w_up', 'w_branch_rwkv', 'w_branch_s5', 'w_out', 's5_w_glu', 'ffn_w_down']
TINY = ['rwkv_w2', 'rwkv_a2', 'rwkv_g2', 'ffn_conv_w']
SMALL = [n for n in WEIGHTS if n not in GATHER]
SMALL_ROWS = 320
ADAM_ROWS = 256


def _mo(v, m):
    return v if isinstance(v, int) else pl.multiple_of(v, m)


def _slab(ref, name, j, h=None):
    (R, Cn), axis = GATHER[name]
    if axis == 0:
        rs = R // 4
        if h is None:
            return ref.at[pl.ds(_mo(j * rs, 16), rs), :]
        return ref.at[pl.ds(_mo(j * rs + h * (rs // 2), 8), rs // 2), :]
    cols = pl.ds(_mo(j * (Cn // 4), 128), Cn // 4)
    if h is None:
        return ref.at[:, cols]
    return ref.at[pl.ds(_mo(h * (R // 2), 8), R // 2), cols]


def _half_shape(name):
    (R, Cn), axis = GATHER[name]
    return (R // 8, Cn) if axis == 0 else (R // 2, Cn // 4)


def _chip_peers(px, py):
    return [((1 - px) if (k >> 1) else px, (1 - py) if (k & 1) else py) for k in (1, 2, 3)]


def _run_copies(copies):
    for cp in copies:
        cp.start()
    for cp in copies:
        cp.wait()


def _gather_weights(blocks):
    names = list(blocks)
    n = len(names)

    def body(*refs):
        ins, outs = refs[:n], refs[n:2 * n]
        ssem, rsem, lsem = refs[2 * n:]
        px, py, pc = _mesh_pos()
        me = 2 * px + py
        copies = []
        for i, nm in enumerate(names):
            if nm in BIG:
                hr = blocks[nm].shape[0] // 2
                src, dst = ins[i].at[pl.ds(pl.multiple_of(pc * hr, 16), hr), :], _slab(outs[i], nm, me, pc)
            else:
                src, dst = ins[i], _slab(outs[i], nm, me)
            copies.append(pltpu.make_async_copy(src, dst, lsem.at[i]))
            for k, (qx, qy) in enumerate(_chip_peers(px, py)):
                copies.append(pltpu.make_async_remote_copy(src, dst, ssem.at[3 * i + k], rsem.at[3 * i + k],
                                                           device_id=(qx, qy, pc), device_id_type=MESH))
        _run_copies(copies)

    outs = pl.pallas_call(
        body, name="gather_weights", in_specs=[ANY] * n, out_specs=[ANY] * n,
        out_shape=[jax.ShapeDtypeStruct(GATHER[nm][0], blocks[nm].dtype) for nm in names],
        scratch_shapes=[pltpu.SemaphoreType.DMA((3 * n,)), pltpu.SemaphoreType.DMA((3 * n,)), pltpu.SemaphoreType.DMA((n,))],
    )(*[blocks[nm] for nm in names])
    return dict(zip(names, outs))


def _gather_pair(full):
    n = len(BIG)

    def body(*refs):
        ins, outs = refs[:n], refs[n:2 * n]
        ssem, rsem = refs[2 * n:]
        px, py, pc = _mesh_pos()
        copies = []
        for i, nm in enumerate(BIG):
            for j in range(4):
                copies.append(pltpu.make_async_remote_copy(_slab(ins[i], nm, j, pc), _slab(outs[i], nm, j, pc),
                                                           ssem.at[4 * i + j], rsem.at[4 * i + j],
                                                           device_id=(px, py, 1 - pc), device_id_type=MESH))
        _run_copies(copies)

    outs = pl.pallas_call(
        body, name="gather_weights_pair", in_specs=[ANY] * n, out_specs=[ANY] * n,
        out_shape=[jax.ShapeDtypeStruct(full[nm].shape, full[nm].dtype) for nm in BIG],
        input_output_aliases={i: i for i in range(n)},
        scratch_shapes=[pltpu.SemaphoreType.DMA((4 * n,)), pltpu.SemaphoreType.DMA((4 * n,))],
    )(*[full[nm] for nm in BIG])
    return dict(zip(BIG, outs))


def _grads_to_sibling(G, small):
    n = len(BIG)

    def body(*refs):
        g_refs, small_ref = refs[:n], refs[n]
        o_refs, small_o = refs[n + 1:2 * n + 1], refs[2 * n + 1]
        ssem, rsem = refs[2 * n + 2:]
        px, py, pc = _mesh_pos()
        sib = (px, py, 1 - pc)
        copies = []
        for i, nm in enumerate(BIG):
            for j in range(4):
                copies.append(pltpu.make_async_remote_copy(_slab(g_refs[i], nm, j, 1 - pc), o_refs[i].at[j],
                                                           ssem.at[4 * i + j], rsem.at[4 * i + j],
                                                           device_id=sib, device_id_type=MESH))
        copies.append(pltpu.make_async_remote_copy(small_ref, small_o, ssem.at[4 * n], rsem.at[4 * n],
                                                   device_id=sib, device_id_type=MESH))
        _run_copies(copies)

    outs = pl.pallas_call(
        body, name="grads_to_sibling", in_specs=[ANY] * (n + 1), out_specs=[ANY] * (n + 1),
        out_shape=[jax.ShapeDtypeStruct((4,) + _half_shape(nm), f32) for nm in BIG] + [jax.ShapeDtypeStruct(small.shape, f32)],
        scratch_shapes=[pltpu.SemaphoreType.DMA((4 * n + 1,)), pltpu.SemaphoreType.DMA((4 * n + 1,))],
    )(*[G[nm] for nm in BIG], small)
    return dict(zip(BIG, outs[:n])), outs[n]


def _pair_add(G, recv, small, small_recv):
    n = len(BIG)
    cidx = lax.axis_index("c").astype(jnp.int32).reshape(1)

    def body(c_ref, *refs):
        ins, outs = refs[:2 * n + 2], refs[2 * n + 2:]
        for i in range(n):
            outs[i][...] = (ins[i][...] + ins[n + i][...]).astype(bf16)
        outs[n][...] = ins[2 * n][...] + ins[2 * n + 1][...]

    g_specs, r_specs = [], []
    for nm in BIG:
        hr, hc = _half_shape(nm)
        if GATHER[nm][1] == 0:
            g_specs.append(pl.BlockSpec((hr // 2, hc), lambda j, i, c: ((2 * j + c[0]) * 2 + i, 0)))
        else:
            g_specs.append(pl.BlockSpec((hr // 2, hc), lambda j, i, c: (2 * c[0] + i, j)))
        r_specs.append(pl.BlockSpec((1, hr // 2, hc), lambda j, i, c: (j, i, 0)))
    sm = pl.BlockSpec((SMALL_ROWS // 8, PACK_W), lambda j, i, c: (2 * j + i, 0))
    outs = pl.pallas_call(
        body, name="grads_pair_sum",
        grid_spec=pltpu.PrefetchScalarGridSpec(num_scalar_prefetch=1, grid=(4, 2), in_specs=g_specs + r_specs + [sm, sm],
                                               out_specs=r_specs + [sm]),
        out_shape=[jax.ShapeDtypeStruct((4,) + _half_shape(nm), bf16) for nm in BIG] + [jax.ShapeDtypeStruct(small.shape, f32)],
        compiler_params=pltpu.CompilerParams(vmem_limit_bytes=VMEM_LIMIT),
    )(cidx, *[G[nm] for nm in BIG], *[recv[nm] for nm in BIG], small, small_recv)
    return dict(zip(BIG, outs[:n])), outs[n]


def _grads_chip_exchange(chip_sum, small):
    n = len(BIG)

    def body(*refs):
        ins, outs = refs[:n + 1], refs[n + 1:2 * n + 2]
        ssem, rsem, lsem = refs[2 * n + 2:]
        px, py, pc = _mesh_pos()
        me = 2 * px + py
        copies = []
        copies.append(pltpu.make_async_copy(ins[n], outs[n].at[me], lsem))
        for i in range(n + 1):
            pick = (lambda ref, j: ref.at[j]) if i < n else (lambda ref, j: ref)
            for k, (qx, qy) in enumerate(_chip_peers(px, py)):
                copies.append(pltpu.make_async_remote_copy(pick(ins[i], 2 * qx + qy), outs[i].at[me],
                                                           ssem.at[3 * i + k], rsem.at[3 * i + k],
                                                           device_id=(qx, qy, pc), device_id_type=MESH))
        _run_copies(copies)

    outs = pl.pallas_call(
        body, name="grads_chip_exchange", in_specs=[ANY] * (n + 1), out_specs=[ANY] * (n + 1),
        out_shape=[jax.ShapeDtypeStruct(chip_sum[nm].shape, chip_sum[nm].dtype) for nm in BIG]
        + [jax.ShapeDtypeStruct((4,) + small.shape, f32)],
        scratch_shapes=[pltpu.SemaphoreType.DMA((3 * n + 3,)), pltpu.SemaphoreType.DMA((3 * n + 3,)),
                        pltpu.SemaphoreType.DMA],
    )(*[chip_sum[nm] for nm in BIG], small)
    return dict(zip(BIG, outs[:n])), outs[n]


def _sum_slots(slots, chip_sum, small4):
    n = len(BIG)
    me = (2 * lax.axis_index("x") + lax.axis_index("y")).astype(jnp.int32).reshape(1)

    def body(me_ref, *refs):
        for i in range(n):
            own = refs[5 * i + 4][0].astype(f32)
            term = [jnp.where(me_ref[0] == k, own, refs[5 * i + k][0].astype(f32)) for k in range(4)]
            refs[5 * n + 1 + i][...] = ((term[0] + term[1]) + term[2]) + term[3]
        x = refs[5 * n]
        refs[6 * n + 1][...] = ((x[0] + x[1]) + x[2]) + x[3]

    in_specs, args, specs_out, shapes = [], [], [], []
    for nm in BIG:
        hr, hc = _half_shape(nm)
        for k in range(4):
            in_specs.append(pl.BlockSpec((1, hr // 2, hc), lambda i, m, k=k: (jnp.where(m[0] == k, (k + 1) % 4, k), i, 0)))
        in_specs.append(pl.BlockSpec((1, hr // 2, hc), lambda i, m: (m[0], i, 0)))
        args += [slots[nm]] * 4 + [chip_sum[nm]]
        specs_out.append(pl.BlockSpec((hr // 2, hc), lambda i, m: (i, 0)))
        shapes.append(jax.ShapeDtypeStruct((hr, hc), f32))
    in_specs.append(pl.BlockSpec((4, SMALL_ROWS // 2, PACK_W), lambda i, m: (0, i, 0)))
    specs_out.append(pl.BlockSpec((SMALL_ROWS // 2, PACK_W), lambda i, m: (i, 0)))
    shapes.append(jax.ShapeDtypeStruct((SMALL_ROWS, PACK_W), f32))
    outs = pl.pallas_call(
        body, name="grads_chip_sum",
        grid_spec=pltpu.PrefetchScalarGridSpec(num_scalar_prefetch=1, grid=(2,), in_specs=in_specs, out_specs=specs_out),
        out_shape=shapes, compiler_params=pltpu.CompilerParams(vmem_limit_bytes=VMEM_LIMIT),
    )(me, *args, small4)
    return dict(zip(BIG, outs[:n])), outs[n]


def _halves_to_sibling(half):
    n = len(BIG)

    def body(*refs):
        ins, outs = refs[:n], refs[n:2 * n]
        ssem, rsem = refs[2 * n:]
        px, py, pc = _mesh_pos()
        _run_copies([pltpu.make_async_remote_copy(ins[i], outs[i], ssem.at[i], rsem.at[i],
                                                  device_id=(px, py, 1 - pc), device_id_type=MESH) for i in range(n)])

    outs = pl.pallas_call(
        body, name="grads_halves_to_sibling", in_specs=[ANY] * n, out_specs=[ANY] * n,
        out_shape=[jax.ShapeDtypeStruct(_half_shape(nm), f32) for nm in BIG],
        scratch_shapes=[pltpu.SemaphoreType.DMA((n,)), pltpu.SemaphoreType.DMA((n,))],
    )(*[half[nm] for nm in BIG])
    return dict(zip(BIG, outs))


def _join_halves(mine, other, pc):
    hr = mine.shape[0]
    return lax.dynamic_slice_in_dim(jnp.concatenate([other, mine, other], axis=0), (1 - pc) * hr, 2 * hr, axis=0)


def _flat_pad(v):
    v = v.reshape(-1)
    return jnp.pad(v, (0, _ceil_to(v.shape[0], PACK_W) - v.shape[0]))


def _pack_rows(parts, rows):
    flat = jnp.concatenate([_flat_pad(p) for p in parts])
    return jnp.pad(flat, (0, rows * PACK_W - flat.shape[0])).reshape(rows, PACK_W)


def _unpack_rows(buf, shapes):
    flat = buf.reshape(-1)
    out, off = [], 0
    for shp in shapes:
        n = 1
        for d in shp:
            n *= d
        out.append(flat[off:off + n].reshape(shp))
        off += _ceil_to(n, PACK_W)
    return out


def _adamw_math(w_, g_, m_, v_):
    m2 = ADAM_B1 * m_ + (1.0 - ADAM_B1) * g_
    v2 = ADAM_B2 * v_ + (1.0 - ADAM_B2) * (g_ * g_)
    m_hat = m2 / (1.0 - ADAM_B1 ** ADAM_STEP)
    v_hat = v2 / (1.0 - ADAM_B2 ** ADAM_STEP)
    return -ADAM_LR * (m_hat / (jnp.sqrt(v_hat) + ADAM_EPS) + ADAM_WD * w_), m2, v2


def _adamw(groups):
    ng = len(groups)

    def body(*refs):
        ins, outs = refs[:4 * ng], refs[4 * ng:]
        for i in range(ng):
            res = _adamw_math(*(r[...] for r in ins[4 * i:4 * i + 4]))
            for ref, val in zip(outs[3 * i:3 * i + 3], res):
                ref[...] = val

    in_specs, out_specs, out_shape = [], [], []
    for grp in groups:
        R, Cn = grp[0].shape
        spec = pl.BlockSpec((R // 8, Cn), lambda i: (i, 0))
        in_specs += [spec] * 4
        out_specs += [spec] * 3
        out_shape += [jax.ShapeDtypeStruct((R, Cn), f32)] * 3
    outs = pl.pallas_call(
        body, name="adamw", grid=(8,), in_specs=in_specs, out_specs=out_specs, out_shape=out_shape,
        compiler_params=pltpu.CompilerParams(vmem_limit_bytes=VMEM_LIMIT),
    )(*[a for grp in groups for a in grp])
    return [tuple(outs[3 * i:3 * i + 3]) for i in range(ng)]


def _forward_backward(x, tgt, W, S):
    L = x.shape[0]
    TM, TMW = 256, 128
    row = lambda c, dt=f32: (c, dt)
    hid = jnp.arange(RWKV_W) // HEAD
    E = (hid[:, None] == hid[None, :]).astype(f32)
    seg = (jnp.arange(S5_N)[None, :] // S5_P == jnp.arange(S5_G)[:, None]).astype(f32)

    w_in_t = W['w_in']
    w_p, w_u, w_g = w_in_t[:N_RWKV], w_in_t[N_RWKV:N_RWKV + S5_W], w_in_t[N_RWKV + S5_W:]
    zpad = jnp.zeros((64, RWKV_W), f32)
    w2p = jnp.concatenate([W['rwkv_w2'], zpad], axis=0)
    a2p = jnp.concatenate([zpad, W['rwkv_a2']], axis=0)
    g2 = W['rwkv_g2']
    prep_consts = [S['rwkv_shift_mu'], S['rwkv_w0'], S['rwkv_a0'], S['rwkv_k_k'], S['rwkv_k_a'], w2p, a2p, g2, E]
    out_consts = [S['rwkv_lnx_w'], S['rwkv_lnx_b'], S['rwkv_r_k'], E]
    cw, cb = W['ffn_conv_w'][:3], S['ffn_conv_b']

    a_re, a_im = S['s5_a_re'].reshape(S5_N, 1), S['s5_a_im'].reshape(S5_N, 1)
    ls = jnp.repeat(S['s5_log_step'].reshape(S5_G, 1), S5_P, axis=0)
    b_re, b_im = S['s5_b_re'].reshape(S5_N, S5_C), S['s5_b_im'].reshape(S5_N, S5_C)
    ar, ai, bbr, bbi = _s5_disc_fwd(a_re, a_im, ls, b_re, b_im)
    abar = jnp.concatenate([ar.reshape(1, S5_N), ai.reshape(1, S5_N)], axis=1)
    eye = jnp.eye(S5_G, dtype=f32)

    def bdiag_in(bb):
        t = bb.reshape(S5_G, S5_P, S5_C).transpose(0, 2, 1)
        return (t[:, :, None, :] * eye[:, None, :, None]).reshape(S5_W, S5_N)

    def bdiag_out(cc):
        t = cc.transpose(0, 2, 1)
        return (t[:, :, None, :] * eye[:, None, :, None]).reshape(S5_N, S5_W)

    eye8 = jnp.eye(8, dtype=f32)

    def undiag_in(blocks):
        t = blocks.reshape(4, 8, S5_C, 8, S5_P)
        t = jnp.sum(t * eye8[None, :, None, :, None], axis=3)
        return t.reshape(S5_G, S5_C, S5_P).transpose(0, 2, 1).reshape(S5_N, S5_C)

    def undiag_out(blocks):
        t = blocks.reshape(4, 8, S5_P, 8, S5_C)
        t = jnp.sum(t * eye8[None, :, None, :, None], axis=3)
        return t.reshape(S5_G, S5_P, S5_C).transpose(0, 2, 1)

    bmat = jnp.concatenate([bdiag_in(bbr), bdiag_in(bbi)], axis=1).astype(bf16)
    cmat = jnp.concatenate([bdiag_out(S['s5_c_re'].reshape(S5_G, S5_C, S5_P)),
                            -bdiag_out(S['s5_c_im'].reshape(S5_G, S5_C, S5_P))], axis=0).astype(bf16)

    g1, g2n, g3, g4 = S['norm_mix_pre'], S['norm_mix_post'], S['norm_ffn_pre'], S['norm_ffn_post']
    (h1,) = _rowcall("norm_pre", lambda i, n, R, P, X, C: ((_rms(R[0], C[0]),), ()), L, TM, [x], [g1],
                     out_rows=[row(D_MODEL, bf16)])
    p = _mm(h1, w_p, 'nt', "mm_p")
    u = _mm(h1, w_u, 'nt', "mm_u")
    gp = _mm(h1, w_g, 'nt', "mm_g")

    def prep_fn(i, n, R, P, X, C):
        q = R[0] + (_shift_down(R[0], P[0], i, 1) - R[0]) * C[0]
        return _prep(q, *C[1:]), ()

    r, lw, k2, v, an, bv, g = _rowcall("rwkv_prep", prep_fn, L, TM, [p], prep_consts,
                                       out_rows=[row(RWKV_W)] * 7, prev=[0])
    y, ck = _wkv7_fwd(r, lw, k2, v, an, bv)
    (o_a,) = _rowcall("rwkv_out", lambda i, n, R, P, X, C: ((_rwkv_out(*R, *C),), ()), L, TM, [y, r, k2, v, g],
                      out_consts, out_rows=[row(RWKV_W, bf16)])
    o_r = _mm(o_a, W['w_branch_rwkv'], 'nn', "mm_br")

    st, ysc = _s5_fwd(u, bmat, cmat, abar)
    (yg,) = _rowcall("s5_mid", lambda i, n, R, P, X, C: ((_s5_mid(*R, *C),), ()), L, TM, [ysc, u], [S['s5_d']],
                     out_rows=[row(S5_W)])
    z2 = _mm(yg, W['s5_w_glu'], 'nn', "mm_glu")
    (o_b,) = _rowcall("s5_glu", lambda i, n, R, P, X, C: ((_s5_glu(*R, *C),), ()), L, TM, [yg, z2], [S['s5_b_glu']],
                      out_rows=[row(S5_W, bf16)])
    o_s = _mm(o_b, W['w_branch_s5'], 'nn', "mm_bs")

    (merged,) = _rowcall("merge", lambda i, n, R, P, X, C: ((_merge(*R, *C),), ()), L, TM, [gp, o_r, o_s],
                         [S['b_gate']], out_rows=[row(D_MODEL, bf16)])
    mixed = _mm(merged, W['w_out'], 'nn', "mm_out")

    def resid_fn(i, n, R, P, X, C):
        x1_ = R[0] + _rms(R[1], C[0])
        return (x1_, _rms(x1_, C[1])), ()

    x1, h2 = _rowcall("resid_norm", resid_fn, L, TM, [x, mixed], [g2n, g3], out_rows=[row(D_MODEL), row(D_MODEL, bf16)])

    z = _mm(h2, W['ffn_w_up'], 'nn', "mm_up")

    def conv(zt, zprev, i, cw_, cb_):
        z2s, z1s = _shift_down(zt, zprev, i, 2), _shift_down(zt, zprev, i, 1)
        return cb_ + cw_[0:1] * z2s + cw_[1:2] * z1s + cw_[2:3] * zt, z2s, z1s

    (act,) = _rowcall("conv_act", lambda i, n, R, P, X, C: ((_act(conv(R[0], P[0], i, C[0], C[1])[0]),), ()), L, TMW,
                      [z], [cw, cb], out_rows=[row(D_FF, bf16)], prev=[0])
    f = _mm(act, W['ffn_w_down'], 'nn', "mm_down")

    def final_fn(i, n, R, P, X, C):
        x1_, f_, t_ = R
        fn_, vjp = jax.vjp(_rms, f_, C[0])
        diff = x1_ + fn_ - t_
        loss = jnp.sum(diff * diff) * (0.5 / D_MODEL)
        dx2_ = diff * (1.0 / D_MODEL)
        df_, dg4_ = vjp(dx2_)
        return (df_, dx2_), (jnp.full((1, PACK_W), loss, f32), dg4_)

    df, dx2, loss, dg4 = _rowcall("loss_head", final_fn, L, TM, [x1, f, tgt], [g4],
                                  out_rows=[row(D_MODEL, bf16), row(D_MODEL)], out_accs=[(1, PACK_W), (1, D_MODEL)])
    G = {'norm_ffn_post': dg4}

    dact = _mm(df, W['ffn_w_down'], 'nt', "mm_down_dx")
    G['ffn_w_down'] = _mm(act, df, 'tn', "mm_down_dw")

    def conv_bwd_fn(i, n, R, P, X, C):
        zc, z2s, z1s = conv(R[0], P[0], i, C[0], C[1])
        _, vjp = jax.vjp(_act, zc)
        (dzc_,) = vjp(R[1])
        return (dzc_,), (_sum0(dzc_), _sum0(dzc_ * z2s), _sum0(dzc_ * z1s), _sum0(dzc_ * R[0]))

    wide = (1, 2 * D_FF)
    dzc, dcb, dcw0, dcw1, dcw2 = _rowcall("conv_act_bwd", conv_bwd_fn, L, TMW, [z, dact], [cw, cb],
                                          out_rows=[row(2 * D_FF)], out_accs=[wide] * 4, prev=[0])
    G['ffn_conv_b'] = dcb
    G['ffn_conv_w'] = jnp.concatenate([dcw0, dcw1, dcw2], axis=0)

    def conv_shift_fn(i, n, R, P, X, C):
        d = R[0]
        return (C[0][2:3] * d + C[0][1:2] * _shift_up(d, X[0], i, n, 1) + C[0][0:1] * _shift_up(d, X[0], i, n, 2),), ()

    (dz,) = _rowcall("conv_shift_bwd", conv_shift_fn, L, TMW, [dzc], [cw], out_rows=[row(2 * D_FF, bf16)], nxt=[0])
    dh2 = _mm(dz, W['ffn_w_up'], 'nt', "mm_up_dx")
    G['ffn_w_up'] = _mm(h2, dz, 'tn', "mm_up_dw")

    def norm2_bwd_fn(i, n, R, P, X, C):
        x1_, mixed_, dx2_, dh2_ = R
        _, vjp3 = jax.vjp(_rms, x1_, C[1])
        dx1a, dg3_ = vjp3(dh2_)
        dx1_ = dx2_ + dx1a
        _, vjp2 = jax.vjp(_rms, mixed_, C[0])
        dmixed_, dg2_ = vjp2(dx1_)
        return (dx1_, dmixed_), (dg2_, dg3_)

    dx1, dmixed, dg2n, dg3 = _rowcall("norm_mid_bwd", norm2_bwd_fn, L, TM, [x1, mixed, dx2, dh2], [g2n, g3],
                                      out_rows=[row(D_MODEL), row(D_MODEL, bf16)], out_accs=[(1, D_MODEL)] * 2)
    G['norm_mix_post'], G['norm_ffn_pre'] = dg2n, dg3

    dmerged = _mm(dmixed, W['w_out'], 'nt', "mm_out_dx")
    G['w_out'] = _mm(merged, dmixed, 'tn', "mm_out_dw")

    def merge_bwd_fn(i, n, R, P, X, C):
        _, vjp = jax.vjp(_merge, R[0], R[1], R[2], C[0])
        dgp_, do_r_, do_s_, dbg_ = vjp(R[3])
        return (dgp_, do_r_, do_s_), (dbg_,)

    dgp, do_r, do_s, G['b_gate'] = _rowcall("merge_bwd", merge_bwd_fn, L, TM, [gp, o_r, o_s, dmerged], [S['b_gate']],
                                            out_rows=[row(2 * D_MODEL, bf16), row(D_MODEL, bf16), row(D_MODEL, bf16)],
                                            out_accs=[(1, 2 * D_MODEL)])
    do_a = _mm(do_r, W['w_branch_rwkv'], 'nt', "mm_br_dx")
    G['w_branch_rwkv'] = _mm(o_a, do_r, 'tn', "mm_br_dw")
    do_b = _mm(do_s, W['w_branch_s5'], 'nt', "mm_bs_dx")
    G['w_branch_s5'] = _mm(o_b, do_s, 'tn', "mm_bs_dw")

    def glu_bwd_fn(i, n, R, P, X, C):
        _, vjp = jax.vjp(_s5_glu, R[0], R[1], C[0])
        dyg1_, dz2_, dbg_ = vjp(R[2])
        return (dyg1_, dz2_), (dbg_,)

    dyg1, dz2, G['s5_b_glu'] = _rowcall("s5_glu_bwd", glu_bwd_fn, L, TM, [yg, z2, do_b], [S['s5_b_glu']],
                                        out_rows=[row(S5_W), row(S5_W, bf16)], out_accs=[(1, S5_W)])
    dyg2 = _mm(dz2, W['s5_w_glu'], 'nt', "mm_glu_dx")
    G['s5_w_glu'] = _mm(yg, dz2, 'tn', "mm_glu_dw")

    def mid_bwd_fn(i, n, R, P, X, C):
        _, vjp = jax.vjp(_s5_mid, R[0], R[1], C[0])
        dysc_, du_, dd_ = vjp(R[2] + R[3])
        return (dysc_, du_), (dd_,)

    dysc, du1, G['s5_d'] = _rowcall("s5_mid_bwd", mid_bwd_fn, L, TM, [ysc, u, dyg1, dyg2], [S['s5_d']],
                                    out_rows=[row(S5_W, bf16), row(S5_W)], out_accs=[(1, S5_W)])
    du2, dbmat, dcmat, dabar = _s5_bwd(dysc, st, u, bmat, cmat, abar)
    da_re, da_im, dls, db_re, db_im = _s5_disc_bwd(
        a_re, a_im, ls, b_re, b_im, dabar[:, :S5_N].reshape(S5_N, 1), dabar[:, S5_N:].reshape(S5_N, 1),
        undiag_in(dbmat[:4]), undiag_in(dbmat[4:]), seg)
    G['s5_a_re'], G['s5_a_im'], G['s5_log_step'] = da_re, da_im, dls
    G['s5_b_re'], G['s5_b_im'] = db_re, db_im
    G['s5_c_re'], G['s5_c_im'] = undiag_out(dcmat[:4]), -undiag_out(dcmat[4:])

    def out_bwd_fn(i, n, R, P, X, C):
        _, vjp = jax.vjp(_rwkv_out, *R[:5], *C)
        gs = vjp(R[5])
        return gs[:5], gs[5:8]

    dy, dr1, dk1, dv1, dg, dlw, dlb, drk = _rowcall("rwkv_out_bwd", out_bwd_fn, L, TM, [y, r, k2, v, g, do_a], out_consts,
                                                    out_rows=[row(RWKV_W)] * 5, out_accs=[(1, RWKV_W)] * 3)
    G['rwkv_lnx_w'], G['rwkv_lnx_b'], G['rwkv_r_k'] = dlw, dlb, drk
    dr2, dlwk, dk2b, dv2, dan, dbv = _wkv7_bwd(r, lw, k2, v, an, bv, ck, dy)

    def prep_bwd_fn(i, n, R, P, X, C):
        p_ = R[0]
        d1 = _shift_down(p_, P[0], i, 1) - p_
        q = p_ + d1 * C[0]
        _, vjp = jax.vjp(_prep, q, *C[1:])
        cots = (R[1] + R[2], R[3], R[4] + R[5], R[6] + R[7], R[8], R[9], R[10])
        gs = vjp(cots)
        return (gs[0],), (_sum0(gs[0] * d1),) + tuple(gs[1:8])

    small, lowr = (1, RWKV_W), (128, RWKV_W)
    dq, dmu, dw0, da0, dkk, dka, dw2p, da2p, dg2 = _rowcall(
        "rwkv_prep_bwd", prep_bwd_fn, L, TM, [p, dr1, dr2, dlwk, dk1, dk2b, dv1, dv2, dan, dbv, dg],
        prep_consts, out_rows=[row(N_RWKV)], out_accs=[(1, N_RWKV)] + [small] * 4 + [lowr] * 3, prev=[0])
    G['rwkv_shift_mu'], G['rwkv_w0'], G['rwkv_a0'], G['rwkv_k_k'], G['rwkv_k_a'] = dmu, dw0, da0, dkk, dka
    G['rwkv_w2'], G['rwkv_a2'], G['rwkv_g2'] = dw2p[:64], da2p[64:], dg2

    def shift_bwd_fn(i, n, R, P, X, C):
        dm = R[0] * C[0]
        return (R[0] - dm + _shift_up(dm, X[0] * C[0], i, n, 1),), ()

    (dp,) = _rowcall("shift_bwd", shift_bwd_fn, L, TM, [dq], [S['rwkv_shift_mu']], out_rows=[row(N_RWKV, bf16)], nxt=[0])

    (du,) = _rowcall("add_du", lambda i, n, R, P, X, C: ((R[0] + R[1],), ()), L, TM, [du1, du2], out_rows=[row(S5_W, bf16)])
    dproj = jnp.concatenate([dp, du, dgp], axis=1)
    dh1 = _mm(dproj, w_in_t, 'nn', "mm_in_dx")
    G['w_in'] = _mm(dproj, h1, 'tn', "mm_in_dw")

    def norm1_bwd_fn(i, n, R, P, X, C):
        _, vjp = jax.vjp(_rms, R[0], C[0])
        dxa, dg1_ = vjp(R[2])
        return (R[1] + dxa,), (dg1_,)

    dx, G['norm_mix_pre'] = _rowcall("norm_pre_bwd", norm1_bwd_fn, L, TM, [x, dx1, dh1], [g1],
                                     out_rows=[row(D_MODEL)], out_accs=[(1, D_MODEL)])
    return loss, dx, G


def kernel(x, norm_mix_pre, norm_mix_post, norm_ffn_pre, norm_ffn_post, w_in, b_gate, rwkv_shift_mu, rwkv_w0, rwkv_w2, rwkv_a0, rwkv_a2, rwkv_g2, rwkv_k_k, rwkv_k_a, rwkv_r_k, rwkv_lnx_w, rwkv_lnx_b, s5_a_re, s5_a_im, s5_b_re, s5_b_im, s5_c_re, s5_c_im, s5_d, s5_log_step, s5_w_glu, s5_b_glu, w_branch_rwkv, w_branch_s5, w_out, ffn_w_up, ffn_conv_w, ffn_conv_b, ffn_w_down, loss_target, m_norm_mix_pre, m_norm_mix_post, m_norm_ffn_pre, m_norm_ffn_post, m_w_in, m_b_gate, m_rwkv_shift_mu, m_rwkv_w0, m_rwkv_w2, m_rwkv_a0, m_rwkv_a2, m_rwkv_g2, m_rwkv_k_k, m_rwkv_k_a, m_rwkv_r_k, m_rwkv_lnx_w, m_rwkv_lnx_b, m_s5_a_re, m_s5_a_im, m_s5_b_re, m_s5_b_im, m_s5_c_re, m_s5_c_im, m_s5_d, m_s5_log_step, m_s5_w_glu, m_s5_b_glu, m_w_branch_rwkv, m_w_branch_s5, m_w_out, m_ffn_w_up, m_ffn_conv_w, m_ffn_conv_b, m_ffn_w_down, v_norm_mix_pre, v_norm_mix_post, v_norm_ffn_pre, v_norm_ffn_post, v_w_in, v_b_gate, v_rwkv_shift_mu, v_rwkv_w0, v_rwkv_w2, v_rwkv_a0, v_rwkv_a2, v_rwkv_g2, v_rwkv_k_k, v_rwkv_k_a, v_rwkv_r_k, v_rwkv_lnx_w, v_rwkv_lnx_b, v_s5_a_re, v_s5_a_im, v_s5_b_re, v_s5_b_im, v_s5_c_re, v_s5_c_im, v_s5_d, v_s5_log_step, v_s5_w_glu, v_s5_b_glu, v_w_branch_rwkv, v_w_branch_s5, v_w_out, v_ffn_w_up, v_ffn_conv_w, v_ffn_conv_b, v_ffn_w_down):
    A = dict(locals())
    me = 2 * lax.axis_index("x") + lax.axis_index("y")
    blk = lambda n: A[n][0]

    mine = {n: (blk(n).T if n == 'w_in' else blk(n)).astype(bf16) for n in BIG}
    mine.update({n: blk(n) for n in TINY})
    mine['ffn_conv_w'] = jnp.pad(blk('ffn_conv_w'), ((0, 5), (0, 0)))
    W = _gather_weights(mine)
    W.update(_gather_pair(W))
    S = {n: A[n].reshape(1, -1) for n in SMALL}

    loss, dx, G = _forward_backward(x[0], loss_target[0], W, S)

    tiny_shapes = [G[n].shape for n in TINY]
    small_buf = _pack_rows([G[n] for n in SMALL] + [G[n] for n in TINY] + [loss], SMALL_ROWS)
    recv, small_recv = _grads_to_sibling(G, small_buf)
    chip_sum, small_sum = _pair_add(G, recv, small_buf, small_recv)
    slots, small4 = _grads_chip_exchange(chip_sum, small_sum)
    half, small_tot = _sum_slots(slots, chip_sum, small4)
    other = _halves_to_sibling(half)
    pc = lax.axis_index("c")
    grad = {n: _join_halves(half[n], other[n], pc) for n in BIG}
    grad['w_in'] = grad['w_in'].T
    vals = _unpack_rows(small_tot, [A[n].shape for n in SMALL] + tiny_shapes + [(1, PACK_W)])
    grad.update(zip(SMALL, vals))
    for n, full in zip(TINY, vals[len(SMALL):]):
        cs = A[n].shape[2]
        grad[n] = lax.dynamic_slice_in_dim(full, me * cs, cs, axis=1)
    loss_out = vals[-1][0, 0]

    packed = SMALL + TINY
    groups = [(blk(n), grad[n], blk('m_' + n), blk('v_' + n)) for n in BIG]
    groups.append(tuple(_pack_rows([src(n) for n in packed], ADAM_ROWS)
                        for src in (lambda n: A[n], lambda n: grad[n], lambda n: A['m_' + n], lambda n: A['v_' + n])))
    res = _adamw(groups)
    outs = [dict(), dict(), dict()]
    for n, r3 in zip(BIG, res[:-1]):
        for d, val in zip(outs, r3):
            d[n] = val
    for d, buf in zip(outs, res[-1]):
        d.update(zip(packed, _unpack_rows(buf, [A[n].shape for n in packed])))
    full = lambda d: [d[n].reshape(A[n].shape) for n in WEIGHTS]
    return (loss_out, dx[None], *full(grad), *full(outs[0]), *full(outs[1]), *full(outs[2]))
```

```python
import functools

import jax
import jax.numpy as jnp
from jax import lax
from jax.experimental import pallas as pl
from jax.experimental.pallas import tpu as pltpu

f32, bf16 = jnp.float32, jnp.bfloat16
MESH = pl.DeviceIdType.MESH

D_MODEL = 1024
RWKV_W = 512
HEADS, HEAD = 8, 64
N_RWKV = 1792
S5_W = 512
S5_G, S5_P, S5_C = 32, 64, 16
S5_N = S5_G * S5_P
D_FF = 2816
NORM_EPS = 1e-6
LNX_EPS = 64e-5
ADAM_LR, ADAM_B1, ADAM_B2, ADAM_EPS, ADAM_WD, ADAM_STEP = 0.001, 0.9, 0.999, 1e-08, 0.01, 10

VMEM_LIMIT = 48 * 1024 * 1024
PACK_W = 1024
WKV_C = 64
S5_T = 256

WEIGHTS = ['norm_mix_pre', 'norm_mix_post', 'norm_ffn_pre', 'norm_ffn_post', 'w_in', 'b_gate', 'rwkv_shift_mu',
           'rwkv_w0', 'rwkv_w2', 'rwkv_a0', 'rwkv_a2', 'rwkv_g2', 'rwkv_k_k', 'rwkv_k_a', 'rwkv_r_k', 'rwkv_lnx_w',
           'rwkv_lnx_b', 's5_a_re', 's5_a_im', 's5_b_re', 's5_b_im', 's5_c_re', 's5_c_im', 's5_d', 's5_log_step',
           's5_w_glu', 's5_b_glu', 'w_branch_rwkv', 'w_branch_s5', 'w_out', 'ffn_w_up', 'ffn_conv_w', 'ffn_conv_b',
           'ffn_w_down']


def _ceil_to(n, m):
    return -(-n // m) * m


def _mesh_pos():
    return lax.axis_index("x"), lax.axis_index("y"), lax.axis_index("c")


def _pick(d, cap=4096):
    for c in (1024, 1408, 2176, 896, 512, 256, 128):
        if c <= cap and d % c == 0:
            return c
    raise ValueError(d)


def _mm_resident(a, w, mode, name, M, N, K, out_dtype):
    budget = 40 * 1024 * 1024 - 2 * K * N
    tm = next(t for t in (512, 256, 128) if 2 * t * (K * a.dtype.itemsize + 4 * N) <= budget)
    dims = _DIMS[mode]

    def body(a_ref, w_ref, o_ref):
        o_ref[...] = lax.dot_general(a_ref[...].astype(bf16), w_ref[...], (dims, ((), ())),
                                     preferred_element_type=f32).astype(o_ref.dtype)

    return pl.pallas_call(
        body, name=name, grid=(M // tm,),
        in_specs=[pl.BlockSpec((tm, K), lambda i: (i, 0)),
                  pl.BlockSpec(w.shape, lambda i: (0, 0), pipeline_mode=pl.Buffered(1))],
        out_specs=pl.BlockSpec((tm, N), lambda i: (i, 0)), out_shape=jax.ShapeDtypeStruct((M, N), out_dtype),
        compiler_params=pltpu.CompilerParams(dimension_semantics=("parallel",), vmem_limit_bytes=VMEM_LIMIT),
    )(a, w)


def _mm(a, b, mode, name, out_dtype=f32):
    if mode == 'tn':
        (K, M), (K2, N) = a.shape, b.shape
    elif mode == 'nt':
        (M, K), (N, K2) = a.shape, b.shape
    else:
        (M, K), (K2, N) = a.shape, b.shape
    assert K == K2, (name, a.shape, b.shape)
    if mode != 'tn' and b.dtype == bf16:
        return _mm_resident(a, b, mode, name, M, N, K, out_dtype)
    if mode == 'tn':
        tm = _pick(M, 2176)
        tn = _pick(N, 512 if tm > 1408 else (1024 if tm > 1024 else 1408))
        tk = _pick(K, 512)
    else:
        tm, tn, tk = _pick(M, 512), _pick(N), _pick(K)
    nk = K // tk
    dims = {'nn': ((1,), (0,)), 'nt': ((1,), (1,)), 'tn': ((0,), (0,))}[mode]

    def body(a_ref, b_ref, o_ref, acc_ref):
        k = pl.program_id(2)

        @pl.when(k == 0)
        def _():
            acc_ref[...] = jnp.zeros_like(acc_ref)

        acc_ref[...] += lax.dot_general(a_ref[...].astype(bf16), b_ref[...].astype(bf16), (dims, ((), ())),
                                        preferred_element_type=f32)

        @pl.when(k == nk - 1)
        def _():
            o_ref[...] = acc_ref[...].astype(o_ref.dtype)

    a_spec = pl.BlockSpec((tk, tm), lambda i, j, k: (k, i)) if mode == 'tn' else pl.BlockSpec((tm, tk), lambda i, j, k: (i, k))
    b_spec = pl.BlockSpec((tn, tk), lambda i, j, k: (j, k)) if mode == 'nt' else pl.BlockSpec((tk, tn), lambda i, j, k: (k, j))
    return pl.pallas_call(
        body, name=name, grid=(M // tm, N // tn, nk),
        in_specs=[a_spec, b_spec], out_specs=pl.BlockSpec((tm, tn), lambda i, j, k: (i, j)),
        out_shape=jax.ShapeDtypeStruct((M, N), out_dtype),
        scratch_shapes=[pltpu.VMEM((tm, tn), f32)],
        compiler_params=pltpu.CompilerParams(dimension_semantics=("parallel", "parallel", "arbitrary"),
                                             vmem_limit_bytes=VMEM_LIMIT),
    )(a, b)


def _rowcall(name, fn, L, tm, rows, consts=(), out_rows=(), out_accs=(), prev=(), nxt=()):
    nsteps = L // tm
    nb8 = tm // 8
    last8 = L // 8 - 1
    n_r, n_p, n_x, n_c, n_or = len(rows), len(prev), len(nxt), len(consts), len(out_rows)

    def body(*refs):
        i = pl.program_id(0)
        vals = [r[...] for r in refs[:n_r + n_p + n_x + n_c]]
        R, P = vals[:n_r], vals[n_r:n_r + n_p]
        X, C = vals[n_r + n_p:n_r + n_p + n_x], vals[n_r + n_p + n_x:]
        o_refs = refs[n_r + n_p + n_x + n_c:]
        outs_r, outs_a = fn(i, nsteps, R, P, X, C)
        for ref, v in zip(o_refs[:n_or], outs_r, strict=True):
            ref[...] = v.astype(ref.dtype)
        if out_accs:
            @pl.when(i == 0)
            def _():
                for ref in o_refs[n_or:]:
                    ref[...] = jnp.zeros_like(ref)

            for ref, v in zip(o_refs[n_or:], outs_a, strict=True):
                ref[...] += v

    def const_spec(c):
        nd = c.ndim
        return pl.BlockSpec(c.shape, lambda i: (0,) * nd)

    in_specs = ([pl.BlockSpec((tm, a.shape[1]), lambda i: (i, 0)) for a in rows]
                + [pl.BlockSpec((8, rows[j].shape[1]), lambda i: (jnp.maximum(i * nb8 - 1, 0), 0)) for j in prev]
                + [pl.BlockSpec((8, rows[j].shape[1]), lambda i: (jnp.minimum((i + 1) * nb8, last8), 0)) for j in nxt]
                + [const_spec(c) for c in consts])
    out_specs = ([pl.BlockSpec((tm, c), lambda i: (i, 0)) for c, _ in out_rows]
                 + [pl.BlockSpec(s, lambda i: (0, 0)) for s in out_accs])
    out_shape = ([jax.ShapeDtypeStruct((L, c), dt) for c, dt in out_rows]
                 + [jax.ShapeDtypeStruct(s, f32) for s in out_accs])
    args = list(rows) + [rows[j] for j in prev] + [rows[j] for j in nxt] + list(consts)
    return pl.pallas_call(
        body, name=name, grid=(nsteps,), in_specs=in_specs, out_specs=out_specs, out_shape=out_shape,
        compiler_params=pltpu.CompilerParams(dimension_semantics=("arbitrary",), vmem_limit_bytes=VMEM_LIMIT),
    )(*args)


def _shift_down(x, prev8, i, k):
    rolled = pltpu.roll(x, k, axis=0)
    pfix = jnp.where(i > 0, pltpu.roll(prev8, k, axis=0), 0.0)
    row8 = lax.broadcasted_iota(jnp.int32, pfix.shape, 0)
    top = jnp.where(row8 < k, pfix, rolled[:8])
    return top if x.shape[0] == 8 else jnp.concatenate([top, rolled[8:]], axis=0)


def _shift_up(x, next8, i, nsteps, k):
    tm = x.shape[0]
    rolled = pltpu.roll(x, tm - k, axis=0)
    nfix = jnp.where(i < nsteps - 1, pltpu.roll(next8, 8 - k, axis=0), 0.0)
    row8 = lax.broadcasted_iota(jnp.int32, nfix.shape, 0)
    bot = jnp.where(row8 >= 8 - k, nfix, rolled[tm - 8:])
    return jnp.concatenate([rolled[:tm - 8], bot], axis=0)


def _sum0(x):
    return jnp.sum(x, axis=0, keepdims=True)


def _rms(x, g):
    return x * lax.rsqrt(jnp.mean(x * x, axis=-1, keepdims=True) + NORM_EPS) * g


def _softplus(x):
    return jnp.maximum(x, 0.0) + jnp.log(1.0 + jnp.exp(-jnp.abs(x)))


def _gelu(x):
    return 0.5 * x * (1.0 + jnp.tanh(0.7978845608028654 * (x + 0.044715 * x * x * x)))


def _dot32(a, b):
    return jnp.dot(a, b, preferred_element_type=f32, precision=lax.Precision.HIGHEST)


def _seg_raw(x, E):
    hi = x.astype(bf16)
    r1 = x - hi.astype(f32)
    mid = r1.astype(bf16)
    lo = (r1 - mid.astype(f32)).astype(bf16)
    Eb = E.astype(bf16)
    dot = lambda t: jnp.dot(t, Eb, preferred_element_type=f32)
    return (dot(lo) + dot(mid)) + dot(hi)


@jax.custom_vjp
def _seg(x, E):
    return _seg_raw(x, E)


_seg.defvjp(lambda x, E: (_seg_raw(x, E), E), lambda E, g: (_seg_raw(g, E), jnp.zeros_like(E)))


def _prep(q, w0, a0, k_k, k_a, w2p, a2p, g2, E):
    r, k, v = q[:, 0:512], q[:, 512:1024], q[:, 1024:1536]
    wa, gd = q[:, 1536:1664], q[:, 1664:1792]
    wlog = -_softplus(-(w0 + _dot32(jnp.tanh(wa), w2p))) - 0.5
    lw = -jnp.exp(wlog)
    a = jax.nn.sigmoid(a0 + _dot32(wa, a2p))
    g = _dot32(jax.nn.sigmoid(gd), g2)
    kk = k * k_k
    kkn = kk / jnp.maximum(jnp.sqrt(_seg(kk * kk, E)), 1e-12)
    k2 = k * (1.0 + (a - 1.0) * k_a)
    return r, lw, k2, v, -kkn, kkn * a, g


def _rwkv_out(y, r, k2, v, g, lnx_w, lnx_b, r_k, E):
    mean = _seg(y, E) * (1.0 / HEAD)
    yc = y - mean
    var = _seg(yc * yc, E) * (1.0 / HEAD)
    yn = yc * lax.rsqrt(var + LNX_EPS) * lnx_w + lnx_b
    bonus = _seg(r * k2 * r_k, E) * v
    return (yn + bonus) * g


def _s5_mid(ysc, u, d):
    return _gelu(ysc + d * u)


def _s5_glu(yg, z2, b_glu):
    return yg * jax.nn.sigmoid(z2 + b_glu)


def _merge(gp, o_r, o_s, b_gate):
    gates = jax.nn.sigmoid(gp + b_gate)
    return gates[:, :D_MODEL] * o_r + gates[:, D_MODEL:] * o_s


def _act(zc):
    return _gelu(zc[:, :D_FF]) * zc[:, D_FF:]


def _s5_disc(a_re, a_im, ls, b_re, b_im):
    dt = jnp.exp(ls)
    er = jnp.exp(a_re * dt)
    ar, ai = er * jnp.cos(a_im * dt), er * jnp.sin(a_im * dt)
    x, y = ar - 1.0, ai
    den = a_re * a_re + a_im * a_im
    fr, fi = (x * a_re + y * a_im) / den, (y * a_re - x * a_im) / den
    return ar, ai, fr * b_re - fi * b_im, fr * b_im + fi * b_re


_DIMS = {'nn': ((1,), (0,)), 'nt': ((1,), (1,)), 'tn': ((0,), (0,))}


def _raw_bdot(a, b, mode):
    return lax.dot_general(a.astype(bf16), b.astype(bf16), (_DIMS[mode], ((), ())), preferred_element_type=f32)


@functools.partial(jax.custom_vjp, nondiff_argnums=(2,))
def _bdot(a, b, mode):
    return _raw_bdot(a, b, mode)


def _bdot_fwd(a, b, mode):
    return _raw_bdot(a, b, mode), (a, b)


def _bdot_bwd(mode, res, g):
    a, b = res
    if mode == 'nn':
        return _raw_bdot(g, b, 'nt'), _raw_bdot(a, g, 'tn')
    if mode == 'nt':
        return _raw_bdot(g, b, 'nn'), _raw_bdot(g, a, 'tn')
    return _raw_bdot(b, g, 'nt'), _raw_bdot(a, g, 'nn')


_bdot.defvjp(_bdot_fwd, _bdot_bwd)


def _wkv_chunk(S0, r, lw, k, v, a, b, tri, bd):
    C = r[0].shape[0]
    P = range(len(r))
    lane = lax.broadcasted_iota(jnp.int32, (1, 2 * HEAD), 1)
    halves = [(lane < HEAD).astype(f32), (lane >= HEAD).astype(f32)]
    eye = (lax.broadcasted_iota(jnp.int32, (C, C), 0) == lax.broadcasted_iota(jnp.int32, (C, C), 1)).astype(f32)
    sl = tri - eye
    cum = [_dot32(tri, lw[p]) for p in P]
    g = [jnp.exp(cum[p]) for p in P]
    gi = [jnp.exp(-cum[p]) for p in P]
    at = [a[p] * jnp.exp(cum[p] - lw[p]) for p in P]
    rt = [r[p] * g[p] for p in P]
    kb = [k[p] * gi[p] for p in P]
    bb = [b[p] * gi[p] for p in P]
    PE = [(p, e) for p in P for e in range(2)]
    atm = {pe: at[pe[0]] * halves[pe[1]] for pe in PE}
    rtm = {pe: rt[pe[0]] * halves[pe[1]] for pe in PE}
    aab = {pe: _bdot(atm[pe], bb[pe[0]], 'nt') * sl for pe in PE}
    aak = {pe: _bdot(atm[pe], kb[pe[0]], 'nt') * sl for pe in PE}
    rk = {pe: _bdot(rtm[pe], kb[pe[0]], 'nt') * tri for pe in PE}
    rb = {pe: _bdot(rtm[pe], bb[pe[0]], 'nt') * tri for pe in PE}
    rhs = [_bdot(at[p], S0[p], 'nt') + sum(halves[e] * _bdot(aak[(p, e)], v[p], 'nn') for e in range(2)) for p in P]
    y0 = [_bdot(rt[p], S0[p], 'nt') + sum(halves[e] * _bdot(rk[(p, e)], v[p], 'nn') for e in range(2)) for p in P]
    x = {pe: eye + aab[pe] for pe in PE}
    pw = aab
    n = 1
    while 2 * n < C:
        pw = {pe: _bdot(pw[pe], pw[pe], 'nn') for pe in PE}
        x = {pe: x[pe] + _bdot(x[pe], pw[pe], 'nn') for pe in PE}
        n *= 2
    u = [sum(halves[e] * _bdot(x[(p, e)], rhs[p], 'nn') for e in range(2)) for p in P]
    y = [y0[p] + sum(halves[e] * _bdot(rb[(p, e)], u[p], 'nn') for e in range(2)) for p in P]
    S1 = [g[p][C - 1:C, :] * (S0[p] + bd * (_bdot(v[p], kb[p], 'tn') + _bdot(u[p], bb[p], 'tn'))) for p in P]
    return y, S1


def _pairs(x):
    return [x[:, 2 * HEAD * p:2 * HEAD * (p + 1)] for p in range(HEADS // 2)]


def _wkv_consts():
    tri = jnp.tril(jnp.ones((WKV_C, WKV_C), f32))
    hid = jnp.arange(2 * HEAD) // HEAD
    return tri, (hid[:, None] == hid[None, :]).astype(f32)


def _wkv7_fwd(r, lw, k, v, a, b):
    L = r.shape[0]
    nc, npair = L // WKV_C, HEADS // 2

    def body(r_ref, lw_ref, k_ref, v_ref, a_ref, b_ref, tri_ref, bd_ref, y_ref, ck_ref, s_ref):
        @pl.when(pl.program_id(0) == 0)
        def _():
            s_ref[...] = jnp.zeros_like(s_ref)

        s0 = [s_ref[p] for p in range(npair)]
        for p in range(npair):
            ck_ref[0, p] = s0[p]
        y, s1 = _wkv_chunk(s0, *(_pairs(x) for x in (r_ref, lw_ref, k_ref, v_ref, a_ref, b_ref)), tri_ref[...], bd_ref[...])
        for p in range(npair):
            y_ref[:, 2 * HEAD * p:2 * HEAD * (p + 1)] = y[p]
            s_ref[p] = s1[p]

    row = pl.BlockSpec((WKV_C, RWKV_W), lambda c: (c, 0))
    sspec = pl.BlockSpec((1, npair, 2 * HEAD, 2 * HEAD), lambda c: (c, 0, 0, 0))
    return pl.pallas_call(
        body, name="wkv7_fwd", grid=(nc,),
        in_specs=[row] * 6 + [pl.BlockSpec((WKV_C, WKV_C), lambda c: (0, 0)), pl.BlockSpec((2 * HEAD, 2 * HEAD), lambda c: (0, 0))],
        out_specs=[row, sspec],
        out_shape=[jax.ShapeDtypeStruct((L, RWKV_W), f32), jax.ShapeDtypeStruct((nc, npair, 2 * HEAD, 2 * HEAD), f32)],
        scratch_shapes=[pltpu.VMEM((npair, 2 * HEAD, 2 * HEAD), f32)],
        compiler_params=pltpu.CompilerParams(dimension_semantics=("arbitrary",), vmem_limit_bytes=VMEM_LIMIT),
    )(r, lw, k, v, a, b, *_wkv_consts())


def _wkv7_bwd(r, lw, k, v, a, b, ck, dy):
    L = r.shape[0]
    nc, npair = L // WKV_C, HEADS // 2

    def body(r_ref, lw_ref, k_ref, v_ref, a_ref, b_ref, ck_ref, dy_ref, tri_ref, bd_ref,
             dr_ref, dlw_ref, dk_ref, dv_ref, da_ref, db_ref, ds_ref):
        @pl.when(pl.program_id(0) == 0)
        def _():
            ds_ref[...] = jnp.zeros_like(ds_ref)

        tri, bd = tri_ref[...], bd_ref[...]
        ins = [[ck_ref[0, p] for p in range(npair)]] + [_pairs(x) for x in (r_ref, lw_ref, k_ref, v_ref, a_ref, b_ref)]
        _, vjp = jax.vjp(lambda *t: _wkv_chunk(*t, tri, bd), *ins)
        gs = vjp((_pairs(dy_ref), [ds_ref[p] for p in range(npair)]))
        for p in range(npair):
            ds_ref[p] = gs[0][p]
            for ref, gval in zip((dr_ref, dlw_ref, dk_ref, dv_ref, da_ref, db_ref), gs[1:]):
                ref[:, 2 * HEAD * p:2 * HEAD * (p + 1)] = gval[p]

    row = pl.BlockSpec((WKV_C, RWKV_W), lambda c: (nc - 1 - c, 0))
    sspec = pl.BlockSpec((1, npair, 2 * HEAD, 2 * HEAD), lambda c: (nc - 1 - c, 0, 0, 0))
    return pl.pallas_call(
        body, name="wkv7_bwd", grid=(nc,),
        in_specs=[row] * 6 + [sspec, row, pl.BlockSpec((WKV_C, WKV_C), lambda c: (0, 0)),
                              pl.BlockSpec((2 * HEAD, 2 * HEAD), lambda c: (0, 0))],
        out_specs=[row] * 6,
        out_shape=[jax.ShapeDtypeStruct((L, RWKV_W), f32)] * 6,
        scratch_shapes=[pltpu.VMEM((npair, 2 * HEAD, 2 * HEAD), f32)],
        compiler_params=pltpu.CompilerParams(dimension_semantics=("arbitrary",), vmem_limit_bytes=VMEM_LIMIT),
    )(r, lw, k, v, a, b, ck, dy, *_wkv_consts())


def _cmul(ar, ai, xr, xi):
    return ar * xr - ai * xi, ar * xi + ai * xr


def _scan_init(a_ref, car_ref, pw_ref, reverse):
    car_ref[...] = jnp.zeros_like(car_ref)
    ar = jnp.broadcast_to(a_ref[:, :S5_N], (8, S5_N))
    ai = jnp.broadcast_to(a_ref[:, S5_N:], (8, S5_N))
    if reverse:
        ai = -ai
    row = lax.broadcasted_iota(jnp.int32, (8, S5_N), 0)
    pr, pi = ar, ai
    qr, qi = jnp.zeros((8, S5_N), f32), jnp.zeros((8, S5_N), f32)
    for e in range(1, 9):
        sel = (row == 8 - e) if reverse else (row == e - 1)
        qr, qi = jnp.where(sel, pr, qr), jnp.where(sel, pi, qi)
        if e in (1, 2, 4):
            j = (1, 2, 4).index(e)
            pw_ref[j, :, :S5_N] = pr
            pw_ref[j, :, S5_N:] = pi
        pr, pi = _cmul(pr, pi, ar, ai)
    pw_ref[3, :, :S5_N] = qr
    pw_ref[3, :, S5_N:] = qi


def _scan_tile(x_ref, o_ref, car_ref, pw_ref, reverse):
    ng = x_ref.shape[0] // 8
    row = lax.broadcasted_iota(jnp.int32, (8, S5_N), 0)

    def group(gi, carry):
        g = (ng - 1 - gi) if reverse else gi
        t0 = pl.multiple_of(g * 8, 8)
        xr, xi = x_ref[pl.ds(t0, 8), :S5_N], x_ref[pl.ds(t0, 8), S5_N:]
        for j, d in enumerate((1, 2, 4)):
            if reverse:
                sr = jnp.where(row < 8 - d, pltpu.roll(xr, 8 - d, axis=0), 0.0)
                si = jnp.where(row < 8 - d, pltpu.roll(xi, 8 - d, axis=0), 0.0)
            else:
                sr = jnp.where(row >= d, pltpu.roll(xr, d, axis=0), 0.0)
                si = jnp.where(row >= d, pltpu.roll(xi, d, axis=0), 0.0)
            mr, mi = _cmul(pw_ref[j, :, :S5_N], pw_ref[j, :, S5_N:], sr, si)
            xr, xi = xr + mr, xi + mi
        cr, ci = carry
        mr, mi = _cmul(pw_ref[3, :, :S5_N], pw_ref[3, :, S5_N:], cr, ci)
        xr, xi = xr + mr, xi + mi
        o_ref[pl.ds(t0, 8), :S5_N] = xr
        o_ref[pl.ds(t0, 8), S5_N:] = xi
        e = 0 if reverse else 7
        return (jnp.broadcast_to(xr[e:e + 1, :], (8, S5_N)), jnp.broadcast_to(xi[e:e + 1, :], (8, S5_N)))

    cr, ci = lax.fori_loop(0, ng, group, (car_ref[:, :S5_N], car_ref[:, S5_N:]))
    car_ref[:, :S5_N] = cr
    car_ref[:, S5_N:] = ci


_CB, _SB = 128, 512


def _cblk(k):
    return slice(_CB * k, _CB * (k + 1))


def _sblk(j):
    return slice(_SB * j, _SB * (j + 1))


def _s5_fwd(u, bmat, cmat, abar):
    L = u.shape[0]
    nt = L // S5_T

    def body(u_ref, b_ref, c_ref, a_ref, st_ref, y_ref, bu_ref, car_ref, pw_ref):
        @pl.when(pl.program_id(0) == 0)
        def _():
            _scan_init(a_ref, car_ref, pw_ref, False)

        for j in range(8):
            bu_ref[:, _sblk(j)] = _raw_bdot(u_ref[:, _cblk(j % 4)], b_ref[_cblk(j % 4), _sblk(j)], 'nn')
        _scan_tile(bu_ref, st_ref, car_ref, pw_ref, False)
        for k in range(4):
            y_ref[:, _cblk(k)] = (_raw_bdot(st_ref[:, _sblk(k)], c_ref[_sblk(k), _cblk(k)], 'nn')
                                  + _raw_bdot(st_ref[:, _sblk(4 + k)], c_ref[_sblk(4 + k), _cblk(k)], 'nn'))

    whole = lambda shape: pl.BlockSpec(shape, lambda i: (0, 0))
    return pl.pallas_call(
        body, name="s5_fwd", grid=(nt,),
        in_specs=[pl.BlockSpec((S5_T, S5_W), lambda i: (i, 0)), whole(bmat.shape), whole(cmat.shape), whole(abar.shape)],
        out_specs=[pl.BlockSpec((S5_T, 2 * S5_N), lambda i: (i, 0)), pl.BlockSpec((S5_T, S5_W), lambda i: (i, 0))],
        out_shape=[jax.ShapeDtypeStruct((L, 2 * S5_N), f32), jax.ShapeDtypeStruct((L, S5_W), f32)],
        scratch_shapes=[pltpu.VMEM((S5_T, 2 * S5_N), f32), pltpu.VMEM((8, 2 * S5_N), f32), pltpu.VMEM((4, 8, 2 * S5_N), f32)],
        compiler_params=pltpu.CompilerParams(dimension_semantics=("arbitrary",), vmem_limit_bytes=VMEM_LIMIT),
    )(u, bmat, cmat, abar)


def _s5_bwd(dy, st, u, bmat, cmat, abar):
    L = u.shape[0]
    nt = L // S5_T
    nb8 = S5_T // 8

    def body(dy_ref, st_ref, sp_ref, u_ref, b_ref, c_ref, a_ref, du_ref, db_ref, dc_ref, da_ref, lam_ref, car_ref, pw_ref):
        i = pl.program_id(0)

        @pl.when(i == 0)
        def _():
            _scan_init(a_ref, car_ref, pw_ref, True)
            db_ref[...] = jnp.zeros_like(db_ref)
            dc_ref[...] = jnp.zeros_like(dc_ref)
            da_ref[...] = jnp.zeros_like(da_ref)

        for j in range(8):
            lam_ref[:, _sblk(j)] = _raw_bdot(dy_ref[:, _cblk(j % 4)], c_ref[_sblk(j), _cblk(j % 4)], 'nt')
        _scan_tile(lam_ref, lam_ref, car_ref, pw_ref, True)
        for k in range(4):
            du_ref[:, _cblk(k)] = (_raw_bdot(lam_ref[:, _sblk(k)], b_ref[_cblk(k), _sblk(k)], 'nt')
                                   + _raw_bdot(lam_ref[:, _sblk(4 + k)], b_ref[_cblk(k), _sblk(4 + k)], 'nt'))
            sr = _shift_down(st_ref[:, _sblk(k)], sp_ref[:, _sblk(k)], nt - 1 - i, 1)
            si = _shift_down(st_ref[:, _sblk(4 + k)], sp_ref[:, _sblk(4 + k)], nt - 1 - i, 1)
            lr, li = lam_ref[:, _sblk(k)], lam_ref[:, _sblk(4 + k)]
            da_ref[:, _sblk(k)] += _sum0(lr * sr + li * si)
            da_ref[:, _sblk(4 + k)] += _sum0(li * sr - lr * si)
        for j in range(8):
            db_ref[j] += _raw_bdot(u_ref[:, _cblk(j % 4)], lam_ref[:, _sblk(j)], 'tn')
            dc_ref[j] += _raw_bdot(st_ref[:, _sblk(j)], dy_ref[:, _cblk(j % 4)], 'tn')

    whole = lambda shape: pl.BlockSpec(shape, lambda i: (0,) * len(shape))
    rev = lambda i: (nt - 1 - i, 0)
    return pl.pallas_call(
        body, name="s5_bwd", grid=(nt,),
        in_specs=[pl.BlockSpec((S5_T, S5_W), rev), pl.BlockSpec((S5_T, 2 * S5_N), rev),
                  pl.BlockSpec((8, 2 * S5_N), lambda i: (jnp.maximum((nt - 1 - i) * nb8 - 1, 0), 0)),
                  pl.BlockSpec((S5_T, S5_W), rev), whole(bmat.shape), whole(cmat.shape), whole(abar.shape)],
        out_specs=[pl.BlockSpec((S5_T, S5_W), rev), whole((8, _CB, _SB)), whole((8, _SB, _CB)), whole((1, 2 * S5_N))],
        out_shape=[jax.ShapeDtypeStruct((L, S5_W), f32), jax.ShapeDtypeStruct((8, _CB, _SB), f32),
                   jax.ShapeDtypeStruct((8, _SB, _CB), f32), jax.ShapeDtypeStruct((1, 2 * S5_N), f32)],
        scratch_shapes=[pltpu.VMEM((S5_T, 2 * S5_N), f32), pltpu.VMEM((8, 2 * S5_N), f32), pltpu.VMEM((4, 8, 2 * S5_N), f32)],
        compiler_params=pltpu.CompilerParams(dimension_semantics=("arbitrary",), vmem_limit_bytes=VMEM_LIMIT),
    )(dy, st, st, u, bmat, cmat, abar)


def _s5_disc_fwd(a_re, a_im, ls, b_re, b_im):
    def body(a_re_ref, a_im_ref, ls_ref, b_re_ref, b_im_ref, ar_ref, ai_ref, br_ref, bi_ref):
        outs = _s5_disc(a_re_ref[...], a_im_ref[...], ls_ref[...], b_re_ref[...], b_im_ref[...])
        for ref, v in zip((ar_ref, ai_ref, br_ref, bi_ref), outs):
            ref[...] = v

    c1, c16 = jax.ShapeDtypeStruct((S5_N, 1), f32), jax.ShapeDtypeStruct((S5_N, S5_C), f32)
    return pl.pallas_call(body, name="s5_disc", out_shape=[c1, c1, c16, c16])(a_re, a_im, ls, b_re, b_im)


def _s5_disc_bwd(a_re, a_im, ls, b_re, b_im, d_ar, d_ai, d_br, d_bi, seg):
    def body(a_re_ref, a_im_ref, ls_ref, b_re_ref, b_im_ref, g1, g2, g3, g4, seg_ref, o1, o2, o3, o4, o5):
        _, vjp = jax.vjp(_s5_disc, a_re_ref[...], a_im_ref[...], ls_ref[...], b_re_ref[...], b_im_ref[...])
        da_re, da_im, dls, db_re, db_im = vjp((g1[...], g2[...], g3[...], g4[...]))
        o1[...] = da_re
        o2[...] = da_im
        o3[...] = _dot32(seg_ref[...], dls)
        o4[...] = db_re
        o5[...] = db_im

    c1, c16 = jax.ShapeDtypeStruct((S5_N, 1), f32), jax.ShapeDtypeStruct((S5_N, S5_C), f32)
    return pl.pallas_call(body, name="s5_disc_bwd", out_shape=[c1, c1, jax.ShapeDtypeStruct((S5_G, 1), f32), c16, c16])(
        a_re, a_im, ls, b_re, b_im, d_ar, d_ai, d_br, d_bi, seg)


ANY = pl.BlockSpec(memory_space=pl.ANY)

GATHER = {'w_in': ((4352, 1024), 0), 'ffn_w_up': ((1024, 5632), 1), 'w_branch_rwkv': ((512, 1024), 1),
          'w_branch_s5': ((512, 1024), 1), 'w_out': ((1024, 1024), 0), 's5_w_glu': ((512, 512), 0),
          'ffn_w_down': ((2816, 1024), 0), 'rwkv_w2': ((64, 512), 1), 'rwkv_a2': ((64, 512), 1),
          'rwkv_g2': ((128, 512), 1), 'ffn_conv_w': ((8, 5632), 1)}
BIG = ['w_in', 'ffn_w_up', 'w_branch_rwkv', 'w_branch_s5', 'w_out', 's5_w_glu', 'ffn_w_down']
TINY = ['rwkv_w2', 'rwkv_a2', 'rwkv_g2', 'ffn_conv_w']
SMALL = [n for n in WEIGHTS if n not in GATHER]
SMALL_ROWS = 320
ADAM_ROWS = 256


def _mo(v, m):
    return v if isinstance(v, int) else pl.multiple_of(v, m)


def _slab(ref, name, j, h=None):
    (R, Cn), axis = GATHER[name]
    if axis == 0:
        rs = R // 4
        if h is None:
            return ref.at[pl.ds(_mo(j * rs, 16), rs), :]
        return ref.at[pl.ds(_mo(j * rs + h * (rs // 2), 8), rs // 2), :]
    cols = pl.ds(_mo(j * (Cn // 4), 128), Cn // 4)
    if h is None:
        return ref.at[:, cols]
    return ref.at[pl.ds(_mo(h * (R // 2), 8), R // 2), cols]


def _half_shape(name):
    (R, Cn), axis = GATHER[name]
    return (R // 8, Cn) if axis == 0 else (R // 2, Cn // 4)


def _chip_peers(px, py):
    return [((1 - px) if (k >> 1) else px, (1 - py) if (k & 1) else py) for k in (1, 2, 3)]


def _run_copies(copies):
    for cp in copies:
        cp.start()
    for cp in copies:
        cp.wait()


def _gather_weights(blocks):
    names = list(blocks)
    n = len(names)

    def body(*refs):
        ins, outs = refs[:n], refs[n:2 * n]
        ssem, rsem, lsem = refs[2 * n:]
        px, py, pc = _mesh_pos()
        me = 2 * px + py
        copies = []
        for i, nm in enumerate(names):
            if nm in BIG:
                hr = blocks[nm].shape[0] // 2
                src, dst = ins[i].at[pl.ds(pl.multiple_of(pc * hr, 16), hr), :], _slab(outs[i], nm, me, pc)
            else:
                src, dst = ins[i], _slab(outs[i], nm, me)
            copies.append(pltpu.make_async_copy(src, dst, lsem.at[i]))
            for k, (qx, qy) in enumerate(_chip_peers(px, py)):
                copies.append(pltpu.make_async_remote_copy(src, dst, ssem.at[3 * i + k], rsem.at[3 * i + k],
                                                           device_id=(qx, qy, pc), device_id_type=MESH))
        _run_copies(copies)

    outs = pl.pallas_call(
        body, name="gather_weights", in_specs=[ANY] * n, out_specs=[ANY] * n,
        out_shape=[jax.ShapeDtypeStruct(GATHER[nm][0], blocks[nm].dtype) for nm in names],
        scratch_shapes=[pltpu.SemaphoreType.DMA((3 * n,)), pltpu.SemaphoreType.DMA((3 * n,)), pltpu.SemaphoreType.DMA((n,))],
    )(*[blocks[nm] for nm in names])
    return dict(zip(names, outs))


def _gather_pair(full):
    n = len(BIG)

    def body(*refs):
        ins, outs = refs[:n], refs[n:2 * n]
        ssem, rsem = refs[2 * n:]
        px, py, pc = _mesh_pos()
        copies = []
        for i, nm in enumerate(BIG):
            for j in range(4):
                copies.append(pltpu.make_async_remote_copy(_slab(ins[i], nm, j, pc), _slab(outs[i], nm, j, pc),
                                                           ssem.at[4 * i + j], rsem.at[4 * i + j],
                                                           device_id=(px, py, 1 - pc), device_id_type=MESH))
        _run_copies(copies)

    outs = pl.pallas_call(
        body, name="gather_weights_pair", in_specs=[ANY] * n, out_specs=[ANY] * n,
        out_shape=[jax.ShapeDtypeStruct(full[nm].shape, full[nm].dtype) for nm in BIG],
        input_output_aliases={i: i for i in range(n)},
        scratch_shapes=[pltpu.SemaphoreType.DMA((4 * n,)), pltpu.SemaphoreType.DMA((4 * n,))],
    )(*[full[nm] for nm in BIG])
    return dict(zip(BIG, outs))


def _grads_to_sibling(G, small):
    n = len(BIG)

    def body(*refs):
        g_refs, small_ref = refs[:n], refs[n]
        o_refs, small_o = refs[n + 1:2 * n + 1], refs[2 * n + 1]
        ssem, rsem = refs[2 * n + 2:]
        px, py, pc = _mesh_pos()
        sib = (px, py, 1 - pc)
        copies = []
        for i, nm in enumerate(BIG):
            for j in range(4):
                copies.append(pltpu.make_async_remote_copy(_slab(g_refs[i], nm, j, 1 - pc), o_refs[i].at[j],
                                                           ssem.at[4 * i + j], rsem.at[4 * i + j],
                                                           device_id=sib, device_id_type=MESH))
        copies.append(pltpu.make_async_remote_copy(small_ref, small_o, ssem.at[4 * n], rsem.at[4 * n],
                                                   device_id=sib, device_id_type=MESH))
        _run_copies(copies)

    outs = pl.pallas_call(
        body, name="grads_to_sibling", in_specs=[ANY] * (n + 1), out_specs=[ANY] * (n + 1),
        out_shape=[jax.ShapeDtypeStruct((4,) + _half_shape(nm), f32) for nm in BIG] + [jax.ShapeDtypeStruct(small.shape, f32)],
        scratch_shapes=[pltpu.SemaphoreType.DMA((4 * n + 1,)), pltpu.SemaphoreType.DMA((4 * n + 1,))],
    )(*[G[nm] for nm in BIG], small)
    return dict(zip(BIG, outs[:n])), outs[n]


def _pair_add(G, recv, small, small_recv):
    n = len(BIG)
    cidx = lax.axis_index("c").astype(jnp.int32).reshape(1)

    def body(c_ref, *refs):
        ins, outs = refs[:2 * n + 2], refs[2 * n + 2:]
        for i in range(n):
            outs[i][...] = (ins[i][...] + ins[n + i][...]).astype(bf16)
        outs[n][...] = ins[2 * n][...] + ins[2 * n + 1][...]

    g_specs, r_specs = [], []
    for nm in BIG:
        hr, hc = _half_shape(nm)
        if GATHER[nm][1] == 0:
            g_specs.append(pl.BlockSpec((hr // 2, hc), lambda j, i, c: ((2 * j + c[0]) * 2 + i, 0)))
        else:
            g_specs.append(pl.BlockSpec((hr // 2, hc), lambda j, i, c: (2 * c[0] + i, j)))
        r_specs.append(pl.BlockSpec((1, hr // 2, hc), lambda j, i, c: (j, i, 0)))
    sm = pl.BlockSpec((SMALL_ROWS // 8, PACK_W), lambda j, i, c: (2 * j + i, 0))
    outs = pl.pallas_call(
        body, name="grads_pair_sum",
        grid_spec=pltpu.PrefetchScalarGridSpec(num_scalar_prefetch=1, grid=(4, 2), in_specs=g_specs + r_specs + [sm, sm],
                                               out_specs=r_specs + [sm]),
        out_shape=[jax.ShapeDtypeStruct((4,) + _half_shape(nm), bf16) for nm in BIG] + [jax.ShapeDtypeStruct(small.shape, f32)],
        compiler_params=pltpu.CompilerParams(vmem_limit_bytes=VMEM_LIMIT),
    )(cidx, *[G[nm] for nm in BIG], *[recv[nm] for nm in BIG], small, small_recv)
    return dict(zip(BIG, outs[:n])), outs[n]


def _grads_chip_exchange(chip_sum, small):
    n = len(BIG)

    def body(*refs):
        ins, outs = refs[:n + 1], refs[n + 1:2 * n + 2]
        ssem, rsem, lsem = refs[2 * n + 2:]
        px, py, pc = _mesh_pos()
        me = 2 * px + py
        copies = []
        copies.append(pltpu.make_async_copy(ins[n], outs[n].at[me], lsem))
        for i in range(n + 1):
            pick = (lambda ref, j: ref.at[j]) if i < n else (lambda ref, j: ref)
            for k, (qx, qy) in enumerate(_chip_peers(px, py)):
                copies.append(pltpu.make_async_remote_copy(pick(ins[i], 2 * qx + qy), outs[i].at[me],
                                                           ssem.at[3 * i + k], rsem.at[3 * i + k],
                                                           device_id=(qx, qy, pc), device_id_type=MESH))
        _run_copies(copies)

    outs = pl.pallas_call(
        body, name="grads_chip_exchange", in_specs=[ANY] * (n + 1), out_specs=[ANY] * (n + 1),
        out_shape=[jax.ShapeDtypeStruct(chip_sum[nm].shape, chip_sum[nm].dtype) for nm in BIG]
        + [jax.ShapeDtypeStruct((4,) + small.shape, f32)],
        scratch_shapes=[pltpu.SemaphoreType.DMA((3 * n + 3,)), pltpu.SemaphoreType.DMA((3 * n + 3,)),
                        pltpu.SemaphoreType.DMA],
    )(*[chip_sum[nm] for nm in BIG], small)
    return dict(zip(BIG, outs[:n])), outs[n]


def _sum_slots(slots, chip_sum, small4):
    n = len(BIG)
    me = (2 * lax.axis_index("x") + lax.axis_index("y")).astype(jnp.int32).reshape(1)

    def body(me_ref, *refs):
        for i in range(n):
            own = refs[5 * i + 4][0].astype(f32)
            term = [jnp.where(me_ref[0] == k, own, refs[5 * i + k][0].astype(f32)) for k in range(4)]
            refs[5 * n + 1 + i][...] = ((term[0] + term[1]) + term[2]) + term[3]
        x = refs[5 * n]
        refs[6 * n + 1][...] = ((x[0] + x[1]) + x[2]) + x[3]

    in_specs, args, specs_out, shapes = [], [], [], []
    for nm in BIG:
        hr, hc = _half_shape(nm)
        for k in range(4):
            in_specs.append(pl.BlockSpec((1, hr // 2, hc), lambda i, m, k=k: (jnp.where(m[0] == k, (k + 1) % 4, k), i, 0)))
        in_specs.append(pl.BlockSpec((1, hr // 2, hc), lambda i, m: (m[0], i, 0)))
        args += [slots[nm]] * 4 + [chip_sum[nm]]
        specs_out.append(pl.BlockSpec((hr // 2, hc), lambda i, m: (i, 0)))
        shapes.append(jax.ShapeDtypeStruct((hr, hc), f32))
    in_specs.append(pl.BlockSpec((4, SMALL_ROWS // 2, PACK_W), lambda i, m: (0, i, 0)))
    specs_out.append(pl.BlockSpec((SMALL_ROWS // 2, PACK_W), lambda i, m: (i, 0)))
    shapes.append(jax.ShapeDtypeStruct((SMALL_ROWS, PACK_W), f32))
    outs = pl.pallas_call(
        body, name="grads_chip_sum",
        grid_spec=pltpu.PrefetchScalarGridSpec(num_scalar_prefetch=1, grid=(2,), in_specs=in_specs, out_specs=specs_out),
        out_shape=shapes, compiler_params=pltpu.CompilerParams(vmem_limit_bytes=VMEM_LIMIT),
    )(me, *args, small4)
    return dict(zip(BIG, outs[:n])), outs[n]


def _halves_to_sibling(half):
    n = len(BIG)

    def body(*refs):
        ins, outs = refs[:n], refs[n:2 * n]
        ssem, rsem = refs[2 * n:]
        px, py, pc = _mesh_pos()
        _run_copies([pltpu.make_async_remote_copy(ins[i], outs[i], ssem.at[i], rsem.at[i],
                                                  device_id=(px, py, 1 - pc), device_id_type=MESH) for i in range(n)])

    outs = pl.pallas_call(
        body, name="grads_halves_to_sibling", in_specs=[ANY] * n, out_specs=[ANY] * n,
        out_shape=[jax.ShapeDtypeStruct(_half_shape(nm), f32) for nm in BIG],
        scratch_shapes=[pltpu.SemaphoreType.DMA((n,)), pltpu.SemaphoreType.DMA((n,))],
    )(*[half[nm] for nm in BIG])
    return dict(zip(BIG, outs))


def _join_halves(mine, other, pc):
    hr = mine.shape[0]
    return lax.dynamic_slice_in_dim(jnp.concatenate([other, mine, other], axis=0), (1 - pc) * hr, 2 * hr, axis=0)


def _flat_pad(v):
    v = v.reshape(-1)
    return jnp.pad(v, (0, _ceil_to(v.shape[0], PACK_W) - v.shape[0]))


def _pack_rows(parts, rows):
    flat = jnp.concatenate([_flat_pad(p) for p in parts])
    return jnp.pad(flat, (0, rows * PACK_W - flat.shape[0])).reshape(rows, PACK_W)


def _unpack_rows(buf, shapes):
    flat = buf.reshape(-1)
    out, off = [], 0
    for shp in shapes:
        n = 1
        for d in shp:
            n *= d
        out.append(flat[off:off + n].reshape(shp))
        off += _ceil_to(n, PACK_W)
    return out


def _adamw_math(w_, g_, m_, v_):
    m2 = ADAM_B1 * m_ + (1.0 - ADAM_B1) * g_
    v2 = ADAM_B2 * v_ + (1.0 - ADAM_B2) * (g_ * g_)
    m_hat = m2 / (1.0 - ADAM_B1 ** ADAM_STEP)
    v_hat = v2 / (1.0 - ADAM_B2 ** ADAM_STEP)
    return -ADAM_LR * (m_hat / (jnp.sqrt(v_hat) + ADAM_EPS) + ADAM_WD * w_), m2, v2


def _adamw(groups):
    ng = len(groups)

    def body(*refs):
        ins, outs = refs[:4 * ng], refs[4 * ng:]
        for i in range(ng):
            res = _adamw_math(*(r[...] for r in ins[4 * i:4 * i + 4]))
            for ref, val in zip(outs[3 * i:3 * i + 3], res):
                ref[...] = val

    in_specs, out_specs, out_shape = [], [], []
    for grp in groups:
        R, Cn = grp[0].shape
        spec = pl.BlockSpec((R // 8, Cn), lambda i: (i, 0))
        in_specs += [spec] * 4
        out_specs += [spec] * 3
        out_shape += [jax.ShapeDtypeStruct((R, Cn), f32)] * 3
    outs = pl.pallas_call(
        body, name="adamw", grid=(8,), in_specs=in_specs, out_specs=out_specs, out_shape=out_shape,
        compiler_params=pltpu.CompilerParams(vmem_limit_bytes=VMEM_LIMIT),
    )(*[a for grp in groups for a in grp])
    return [tuple(outs[3 * i:3 * i + 3]) for i in range(ng)]


def _forward_backward(x, tgt, W, S):
    L = x.shape[0]
    TM, TMW = 256, 128
    row = lambda c, dt=f32: (c, dt)
    hid = jnp.arange(RWKV_W) // HEAD
    E = (hid[:, None] == hid[None, :]).astype(f32)
    seg = (jnp.arange(S5_N)[None, :] // S5_P == jnp.arange(S5_G)[:, None]).astype(f32)

    w_in_t = W['w_in']
    w_p, w_u, w_g = w_in_t[:N_RWKV], w_in_t[N_RWKV:N_RWKV + S5_W], w_in_t[N_RWKV + S5_W:]
    zpad = jnp.zeros((64, RWKV_W), f32)
    w2p = jnp.concatenate([W['rwkv_w2'], zpad], axis=0)
    a2p = jnp.concatenate([zpad, W['rwkv_a2']], axis=0)
    g2 = W['rwkv_g2']
    prep_consts = [S['rwkv_shift_mu'], S['rwkv_w0'], S['rwkv_a0'], S['rwkv_k_k'], S['rwkv_k_a'], w2p, a2p, g2, E]
    out_consts = [S['rwkv_lnx_w'], S['rwkv_lnx_b'], S['rwkv_r_k'], E]
    cw, cb = W['ffn_conv_w'][:3], S['ffn_conv_b']

    a_re, a_im = S['s5_a_re'].reshape(S5_N, 1), S['s5_a_im'].reshape(S5_N, 1)
    ls = jnp.repeat(S['s5_log_step'].reshape(S5_G, 1), S5_P, axis=0)
    b_re, b_im = S['s5_b_re'].reshape(S5_N, S5_C), S['s5_b_im'].reshape(S5_N, S5_C)
    ar, ai, bbr, bbi = _s5_disc_fwd(a_re, a_im, ls, b_re, b_im)
    abar = jnp.concatenate([ar.reshape(1, S5_N), ai.reshape(1, S5_N)], axis=1)
    eye = jnp.eye(S5_G, dtype=f32)

    def bdiag_in(bb):
        t = bb.reshape(S5_G, S5_P, S5_C).transpose(0, 2, 1)
        return (t[:, :, None, :] * eye[:, None, :, None]).reshape(S5_W, S5_N)

    def bdiag_out(cc):
        t = cc.transpose(0, 2, 1)
        return (t[:, :, None, :] * eye[:, None, :, None]).reshape(S5_N, S5_W)

    eye8 = jnp.eye(8, dtype=f32)

    def undiag_in(blocks):
        t = blocks.reshape(4, 8, S5_C, 8, S5_P)
        t = jnp.sum(t * eye8[None, :, None, :, None], axis=3)
        return t.reshape(S5_G, S5_C, S5_P).transpose(0, 2, 1).reshape(S5_N, S5_C)

    def undiag_out(blocks):
        t = blocks.reshape(4, 8, S5_P, 8, S5_C)
        t = jnp.sum(t * eye8[None, :, None, :, None], axis=3)
        return t.reshape(S5_G, S5_P, S5_C).transpose(0, 2, 1)

    bmat = jnp.concatenate([bdiag_in(bbr), bdiag_in(bbi)], axis=1).astype(bf16)
    cmat = jnp.concatenate([bdiag_out(S['s5_c_re'].reshape(S5_G, S5_C, S5_P)),
                            -bdiag_out(S['s5_c_im'].reshape(S5_G, S5_C, S5_P))], axis=0).astype(bf16)

    g1, g2n, g3, g4 = S['norm_mix_pre'], S['norm_mix_post'], S['norm_ffn_pre'], S['norm_ffn_post']
    (h1,) = _rowcall("norm_pre", lambda i, n, R, P, X, C: ((_rms(R[0], C[0]),), ()), L, TM, [x], [g1],
                     out_rows=[row(D_MODEL, bf16)])
    p = _mm(h1, w_p, 'nt', "mm_p")
    u = _mm(h1, w_u, 'nt', "mm_u")
    gp = _mm(h1, w_g, 'nt', "mm_g")

    def prep_fn(i, n, R, P, X, C):
        q = R[0] + (_shift_down(R[0], P[0], i, 1) - R[0]) * C[0]
        return _prep(q, *C[1:]), ()

    r, lw, k2, v, an, bv, g = _rowcall("rwkv_prep", prep_fn, L, TM, [p], prep_consts,
                                       out_rows=[row(RWKV_W)] * 7, prev=[0])
    y, ck = _wkv7_fwd(r, lw, k2, v, an, bv)
    (o_a,) = _rowcall("rwkv_out", lambda i, n, R, P, X, C: ((_rwkv_out(*R, *C),), ()), L, TM, [y, r, k2, v, g],
                      out_consts, out_rows=[row(RWKV_W, bf16)])
    o_r = _mm(o_a, W['w_branch_rwkv'], 'nn', "mm_br")

    st, ysc = _s5_fwd(u, bmat, cmat, abar)
    (yg,) = _rowcall("s5_mid", lambda i, n, R, P, X, C: ((_s5_mid(*R, *C),), ()), L, TM, [ysc, u], [S['s5_d']],
                     out_rows=[row(S5_W)])
    z2 = _mm(yg, W['s5_w_glu'], 'nn', "mm_glu")
    (o_b,) = _rowcall("s5_glu", lambda i, n, R, P, X, C: ((_s5_glu(*R, *C),), ()), L, TM, [yg, z2], [S['s5_b_glu']],
                      out_rows=[row(S5_W, bf16)])
    o_s = _mm(o_b, W['w_branch_s5'], 'nn', "mm_bs")

    (merged,) = _rowcall("merge", lambda i, n, R, P, X, C: ((_merge(*R, *C),), ()), L, TM, [gp, o_r, o_s],
                         [S['b_gate']], out_rows=[row(D_MODEL, bf16)])
    mixed = _mm(merged, W['w_out'], 'nn', "mm_out")

    def resid_fn(i, n, R, P, X, C):
        x1_ = R[0] + _rms(R[1], C[0])
        return (x1_, _rms(x1_, C[1])), ()

    x1, h2 = _rowcall("resid_norm", resid_fn, L, TM, [x, mixed], [g2n, g3], out_rows=[row(D_MODEL), row(D_MODEL, bf16)])

    z = _mm(h2, W['ffn_w_up'], 'nn', "mm_up")

    def conv(zt, zprev, i, cw_, cb_):
        z2s, z1s = _shift_down(zt, zprev, i, 2), _shift_down(zt, zprev, i, 1)
        return cb_ + cw_[0:1] * z2s + cw_[1:2] * z1s + cw_[2:3] * zt, z2s, z1s

    (act,) = _rowcall("conv_act", lambda i, n, R, P, X, C: ((_act(conv(R[0], P[0], i, C[0], C[1])[0]),), ()), L, TMW,
                      [z], [cw, cb], out_rows=[row(D_FF, bf16)], prev=[0])
    f = _mm(act, W['ffn_w_down'], 'nn', "mm_down")

    def final_fn(i, n, R, P, X, C):
        x1_, f_, t_ = R
        fn_, vjp = jax.vjp(_rms, f_, C[0])
        diff = x1_ + fn_ - t_
        loss = jnp.sum(diff * diff) * (0.5 / D_MODEL)
        dx2_ = diff * (1.0 / D_MODEL)
        df_, dg4_ = vjp(dx2_)
        return (df_, dx2_), (jnp.full((1, PACK_W), loss, f32), dg4_)

    df, dx2, loss, dg4 = _rowcall("loss_head", final_fn, L, TM, [x1, f, tgt], [g4],
                                  out_rows=[row(D_MODEL, bf16), row(D_MODEL)], out_accs=[(1, PACK_W), (1, D_MODEL)])
    G = {'norm_ffn_post': dg4}

    dact = _mm(df, W['ffn_w_down'], 'nt', "mm_down_dx")
    G['ffn_w_down'] = _mm(act, df, 'tn', "mm_down_dw")

    def conv_bwd_fn(i, n, R, P, X, C):
        z_, dact_ = R
        cw_, cb_ = C
        zc, z2s, z1s = conv(z_, P[0], i, cw_, cb_)
        _, vjp = jax.vjp(_act, zc)
        (dzc_,) = vjp(dact_)
        last8 = z_[z_.shape[0] - 8:]
        zcn = cb_ + cw_[0:1] * _shift_down(X[0], last8, 1, 2) + cw_[1:2] * _shift_down(X[0], last8, 1, 1) + cw_[2:3] * X[0]
        _, vjpn = jax.vjp(_act, zcn)
        (dzcn,) = vjpn(X[1])
        dz_ = (cw_[2:3] * dzc_ + cw_[1:2] * _shift_up(dzc_, dzcn, i, n, 1) + cw_[0:1] * _shift_up(dzc_, dzcn, i, n, 2))
        return (dz_,), (_sum0(dzc_), _sum0(dzc_ * z2s), _sum0(dzc_ * z1s), _sum0(dzc_ * z_))

    wide = (1, 2 * D_FF)
    dz, dcb, dcw0, dcw1, dcw2 = _rowcall("conv_act_bwd", conv_bwd_fn, L, TMW, [z, dact], [cw, cb],
                                         out_rows=[row(2 * D_FF, bf16)], out_accs=[wide] * 4, prev=[0], nxt=[0, 1])
    G['ffn_conv_b'] = dcb
    G['ffn_conv_w'] = jnp.concatenate([dcw0, dcw1, dcw2], axis=0)
    dh2 = _mm(dz, W['ffn_w_up'], 'nt', "mm_up_dx")
    G['ffn_w_up'] = _mm(h2, dz, 'tn', "mm_up_dw")

    def norm2_bwd_fn(i, n, R, P, X, C):
        x1_, mixed_, dx2_, dh2_ = R
        _, vjp3 = jax.vjp(_rms, x1_, C[1])
        dx1a, dg3_ = vjp3(dh2_)
        dx1_ = dx2_ + dx1a
        _, vjp2 = jax.vjp(_rms, mixed_, C[0])
        dmixed_, dg2_ = vjp2(dx1_)
        return (dx1_, dmixed_), (dg2_, dg3_)

    dx1, dmixed, dg2n, dg3 = _rowcall("norm_mid_bwd", norm2_bwd_fn, L, TM, [x1, mixed, dx2, dh2], [g2n, g3],
                                      out_rows=[row(D_MODEL), row(D_MODEL, bf16)], out_accs=[(1, D_MODEL)] * 2)
    G['norm_mix_post'], G['norm_ffn_pre'] = dg2n, dg3

    dmerged = _mm(dmixed, W['w_out'], 'nt', "mm_out_dx")
    G['w_out'] = _mm(merged, dmixed, 'tn', "mm_out_dw")

    def merge_bwd_fn(i, n, R, P, X, C):
        _, vjp = jax.vjp(_merge, R[0], R[1], R[2], C[0])
        dgp_, do_r_, do_s_, dbg_ = vjp(R[3])
        return (dgp_, do_r_, do_s_), (dbg_,)

    dgp, do_r, do_s, G['b_gate'] = _rowcall("merge_bwd", merge_bwd_fn, L, TM, [gp, o_r, o_s, dmerged], [S['b_gate']],
                                            out_rows=[row(2 * D_MODEL, bf16), row(D_MODEL, bf16), row(D_MODEL, bf16)],
                                            out_accs=[(1, 2 * D_MODEL)])
    do_a = _mm(do_r, W['w_branch_rwkv'], 'nt', "mm_br_dx")
    G['w_branch_rwkv'] = _mm(o_a, do_r, 'tn', "mm_br_dw")
    do_b = _mm(do_s, W['w_branch_s5'], 'nt', "mm_bs_dx")
    G['w_branch_s5'] = _mm(o_b, do_s, 'tn', "mm_bs_dw")

    def glu_bwd_fn(i, n, R, P, X, C):
        _, vjp = jax.vjp(_s5_glu, R[0], R[1], C[0])
        dyg1_, dz2_, dbg_ = vjp(R[2])
        return (dyg1_, dz2_), (dbg_,)

    dyg1, dz2, G['s5_b_glu'] = _rowcall("s5_glu_bwd", glu_bwd_fn, L, TM, [yg, z2, do_b], [S['s5_b_glu']],
                                        out_rows=[row(S5_W), row(S5_W, bf16)], out_accs=[(1, S5_W)])
    dyg2 = _mm(dz2, W['s5_w_glu'], 'nt', "mm_glu_dx")
    G['s5_w_glu'] = _mm(yg, dz2, 'tn', "mm_glu_dw")

    def mid_bwd_fn(i, n, R, P, X, C):
        _, vjp = jax.vjp(_s5_mid, R[0], R[1], C[0])
        dysc_, du_, dd_ = vjp(R[2] + R[3])
        return (dysc_, du_), (dd_,)

    dysc, du1, G['s5_d'] = _rowcall("s5_mid_bwd", mid_bwd_fn, L, TM, [ysc, u, dyg1, dyg2], [S['s5_d']],
                                    out_rows=[row(S5_W, bf16), row(S5_W)], out_accs=[(1, S5_W)])
    du2, dbmat, dcmat, dabar = _s5_bwd(dysc, st, u, bmat, cmat, abar)
    da_re, da_im, dls, db_re, db_im = _s5_disc_bwd(
        a_re, a_im, ls, b_re, b_im, dabar[:, :S5_N].reshape(S5_N, 1), dabar[:, S5_N:].reshape(S5_N, 1),
        undiag_in(dbmat[:4]), undiag_in(dbmat[4:]), seg)
    G['s5_a_re'], G['s5_a_im'], G['s5_log_step'] = da_re, da_im, dls
    G['s5_b_re'], G['s5_b_im'] = db_re, db_im
    G['s5_c_re'], G['s5_c_im'] = undiag_out(dcmat[:4]), -undiag_out(dcmat[4:])

    def out_bwd_fn(i, n, R, P, X, C):
        _, vjp = jax.vjp(_rwkv_out, *R[:5], *C)
        gs = vjp(R[5])
        return gs[:5], gs[5:8]

    dy, dr1, dk1, dv1, dg, dlw, dlb, drk = _rowcall("rwkv_out_bwd", out_bwd_fn, L, TM, [y, r, k2, v, g, do_a], out_consts,
                                                    out_rows=[row(RWKV_W)] * 5, out_accs=[(1, RWKV_W)] * 3)
    G['rwkv_lnx_w'], G['rwkv_lnx_b'], G['rwkv_r_k'] = dlw, dlb, drk
    dr2, dlwk, dk2b, dv2, dan, dbv = _wkv7_bwd(r, lw, k2, v, an, bv, ck, dy)

    def prep_bwd_fn(i, n, R, P, X, C):
        p_ = R[0]
        d1 = _shift_down(p_, P[0], i, 1) - p_
        q = p_ + d1 * C[0]
        _, vjp = jax.vjp(_prep, q, *C[1:])
        cots = (R[1] + R[2], R[3], R[4] + R[5], R[6] + R[7], R[8], R[9], R[10])
        gs = vjp(cots)
        return (gs[0],), (_sum0(gs[0] * d1),) + tuple(gs[1:8])

    small, lowr = (1, RWKV_W), (128, RWKV_W)
    dq, dmu, dw0, da0, dkk, dka, dw2p, da2p, dg2 = _rowcall(
        "rwkv_prep_bwd", prep_bwd_fn, L, TM, [p, dr1, dr2, dlwk, dk1, dk2b, dv1, dv2, dan, dbv, dg],
        prep_consts, out_rows=[row(N_RWKV)], out_accs=[(1, N_RWKV)] + [small] * 4 + [lowr] * 3, prev=[0])
    G['rwkv_shift_mu'], G['rwkv_w0'], G['rwkv_a0'], G['rwkv_k_k'], G['rwkv_k_a'] = dmu, dw0, da0, dkk, dka
    G['rwkv_w2'], G['rwkv_a2'], G['rwkv_g2'] = dw2p[:64], da2p[64:], dg2

    def shift_bwd_fn(i, n, R, P, X, C):
        dm = R[0] * C[0]
        return (R[0] - dm + _shift_up(dm, X[0] * C[0], i, n, 1),), ()

    (dp,) = _rowcall("shift_bwd", shift_bwd_fn, L, TM, [dq], [S['rwkv_shift_mu']], out_rows=[row(N_RWKV, bf16)], nxt=[0])

    (du,) = _rowcall("add_du", lambda i, n, R, P, X, C: ((R[0] + R[1],), ()), L, TM, [du1, du2], out_rows=[row(S5_W, bf16)])
    dproj = jnp.concatenate([dp, du, dgp], axis=1)
    dh1 = _mm(dproj, w_in_t, 'nn', "mm_in_dx")
    G['w_in'] = _mm(dproj, h1, 'tn', "mm_in_dw")

    def norm1_bwd_fn(i, n, R, P, X, C):
        _, vjp = jax.vjp(_rms, R[0], C[0])
        dxa, dg1_ = vjp(R[2])
        return (R[1] + dxa,), (dg1_,)

    dx, G['norm_mix_pre'] = _rowcall("norm_pre_bwd", norm1_bwd_fn, L, TM, [x, dx1, dh1], [g1],
                                     out_rows=[row(D_MODEL)], out_accs=[(1, D_MODEL)])
    return loss, dx, G


def kernel(x, norm_mix_pre, norm_mix_post, norm_ffn_pre, norm_ffn_post, w_in, b_gate, rwkv_shift_mu, rwkv_w0, rwkv_w2, rwkv_a0, rwkv_a2, rwkv_g2, rwkv_k_k, rwkv_k_a, rwkv_r_k, rwkv_lnx_w, rwkv_lnx_b, s5_a_re, s5_a_im, s5_b_re, s5_b_im, s5_c_re, s5_c_im, s5_d, s5_log_step, s5_w_glu, s5_b_glu, w_branch_rwkv, w_branch_s5, w_out, ffn_w_up, ffn_conv_w, ffn_conv_b, ffn_w_down, loss_target, m_norm_mix_pre, m_norm_mix_post, m_norm_ffn_pre, m_norm_ffn_post, m_w_in, m_b_gate, m_rwkv_shift_mu, m_rwkv_w0, m_rwkv_w2, m_rwkv_a0, m_rwkv_a2, m_rwkv_g2, m_rwkv_k_k, m_rwkv_k_a, m_rwkv_r_k, m_rwkv_lnx_w, m_rwkv_lnx_b, m_s5_a_re, m_s5_a_im, m_s5_b_re, m_s5_b_im, m_s5_c_re, m_s5_c_im, m_s5_d, m_s5_log_step, m_s5_w_glu, m_s5_b_glu, m_w_branch_rwkv, m_w_branch_s5, m_w_out, m_ffn_w_up, m_ffn_conv_w, m_ffn_conv_b, m_ffn_w_down, v_norm_mix_pre, v_norm_mix_post, v_norm_ffn_pre, v_norm_ffn_post, v_w_in, v_b_gate, v_rwkv_shift_mu, v_rwkv_w0, v_rwkv_w2, v_rwkv_a0, v_rwkv_a2, v_rwkv_g2, v_rwkv_k_k, v_rwkv_k_a, v_rwkv_r_k, v_rwkv_lnx_w, v_rwkv_lnx_b, v_s5_a_re, v_s5_a_im, v_s5_b_re, v_s5_b_im, v_s5_c_re, v_s5_c_im, v_s5_d, v_s5_log_step, v_s5_w_glu, v_s5_b_glu, v_w_branch_rwkv, v_w_branch_s5, v_w_out, v_ffn_w_up, v_ffn_conv_w, v_ffn_conv_b, v_ffn_w_down):
    A = dict(locals())
    me = 2 * lax.axis_index("x") + lax.axis_index("y")
    blk = lambda n: A[n][0]

    mine = {n: (blk(n).T if n == 'w_in' else blk(n)).astype(bf16) for n in BIG}
    mine.update({n: blk(n) for n in TINY})
    mine['ffn_conv_w'] = jnp.pad(blk('ffn_conv_w'), ((0, 5), (0, 0)))
    W = _gather_weights(mine)
    W.update(_gather_pair(W))
    S = {n: A[n].reshape(1, -1) for n in SMALL}

    loss, dx, G = _forward_backward(x[0], loss_target[0], W, S)

    tiny_shapes = [G[n].shape for n in TINY]
    small_buf = _pack_rows([G[n] for n in SMALL] + [G[n] for n in TINY] + [loss], SMALL_ROWS)
    recv, small_recv = _grads_to_sibling(G, small_buf)
    chip_sum, small_sum = _pair_add(G, recv, small_buf, small_recv)
    slots, small4 = _grads_chip_exchange(chip_sum, small_sum)
    half, small_tot = _sum_slots(slots, chip_sum, small4)
    other = _halves_to_sibling(half)
    pc = lax.axis_index("c")
    grad = {n: _join_halves(half[n], other[n], pc) for n in BIG}
    grad['w_in'] = grad['w_in'].T
    vals = _unpack_rows(small_tot, [A[n].shape for n in SMALL] + tiny_shapes + [(1, PACK_W)])
    grad.update(zip(SMALL, vals))
    for n, full in zip(TINY, vals[len(SMALL):]):
        cs = A[n].shape[2]
        grad[n] = lax.dynamic_slice_in_dim(full, me * cs, cs, axis=1)
    loss_out = vals[-1][0, 0]

    packed = SMALL + TINY
    groups = [(blk(n), grad[n], blk('m_' + n), blk('v_' + n)) for n in BIG]
    groups.append(tuple(_pack_rows([src(n) for n in packed], ADAM_ROWS)
                        for src in (lambda n: A[n], lambda n: grad[n], lambda n: A['m_' + n], lambda n: A['v_' + n])))
    res = _adamw(groups)
    outs = [dict(), dict(), dict()]
    for n, r3 in zip(BIG, res[:-1]):
        for d, val in zip(outs, r3):
            d[n] = val
    for d, buf in zip(outs, res[-1]):
        d.update(zip(packed, _unpack_rows(buf, [A[n].shape for n in packed])))
    full = lambda d: [d[n].reshape(A[n].shape) for n in WEIGHTS]
    return (loss_out, dx[None], *full(grad), *full(outs[0]), *full(outs[1]), *full(outs[2]))
```

```python
import functools

import jax
import jax.numpy as jnp
from jax import lax
from jax.experimental import pallas as pl
from jax.experimental.pallas import tpu as pltpu

f32, bf16 = jnp.float32, jnp.bfloat16
MESH = pl.DeviceIdType.MESH

D_MODEL = 1024
RWKV_W = 512
HEADS, HEAD = 8, 64
N_RWKV = 1792
S5_W = 512
S5_G, S5_P, S5_C = 32, 64, 16
S5_N = S5_G * S5_P
D_FF = 2816
NORM_EPS = 1e-6
LNX_EPS = 64e-5
ADAM_LR, ADAM_B1, ADAM_B2, ADAM_EPS, ADAM_WD, ADAM_STEP = 0.001, 0.9, 0.999, 1e-08, 0.01, 10

VMEM_LIMIT = 48 * 1024 * 1024
PACK_W = 1024
WKV_C = 64
S5_T = 256

WEIGHTS = ['norm_mix_pre', 'norm_mix_post', 'norm_ffn_pre', 'norm_ffn_post', 'w_in', 'b_gate', 'rwkv_shift_mu',
           'rwkv_w0', 'rwkv_w2', 'rwkv_a0', 'rwkv_a2', 'rwkv_g2', 'rwkv_k_k', 'rwkv_k_a', 'rwkv_r_k', 'rwkv_lnx_w',
           'rwkv_lnx_b', 's5_a_re', 's5_a_im', 's5_b_re', 's5_b_im', 's5_c_re', 's5_c_im', 's5_d', 's5_log_step',
           's5_w_glu', 's5_b_glu', 'w_branch_rwkv', 'w_branch_s5', 'w_out', 'ffn_w_up', 'ffn_conv_w', 'ffn_conv_b',
           'ffn_w_down']


def _ceil_to(n, m):
    return -(-n // m) * m


def _mesh_pos():
    return lax.axis_index("x"), lax.axis_index("y"), lax.axis_index("c")


def _pick(d, cap=4096):
    for c in (1024, 1408, 2176, 896, 512, 256, 128):
        if c <= cap and d % c == 0:
            return c
    raise ValueError(d)


def _mm_resident(a, w, mode, name, M, N, K, out_dtype):
    budget = 40 * 1024 * 1024 - 2 * K * N
    tm = next(t for t in (512, 256, 128) if 2 * t * (K * a.dtype.itemsize + 4 * N) <= budget)
    dims = _DIMS[mode]

    def body(a_ref, w_ref, o_ref):
        o_ref[...] = lax.dot_general(a_ref[...].astype(bf16), w_ref[...], (dims, ((), ())),
                                     preferred_element_type=f32).astype(o_ref.dtype)

    return pl.pallas_call(
        body, name=name, grid=(M // tm,),
        in_specs=[pl.BlockSpec((tm, K), lambda i: (i, 0)),
                  pl.BlockSpec(w.shape, lambda i: (0, 0), pipeline_mode=pl.Buffered(1))],
        out_specs=pl.BlockSpec((tm, N), lambda i: (i, 0)), out_shape=jax.ShapeDtypeStruct((M, N), out_dtype),
        compiler_params=pltpu.CompilerParams(dimension_semantics=("parallel",), vmem_limit_bytes=VMEM_LIMIT),
    )(a, w)


def _mm(a, b, mode, name, out_dtype=f32):
    if mode == 'tn':
        (K, M), (K2, N) = a.shape, b.shape
    elif mode == 'nt':
        (M, K), (N, K2) = a.shape, b.shape
    else:
        (M, K), (K2, N) = a.shape, b.shape
    assert K == K2, (name, a.shape, b.shape)
    if mode != 'tn' and b.dtype == bf16:
        return _mm_resident(a, b, mode, name, M, N, K, out_dtype)
    if mode == 'tn':
        tm = _pick(M, 2176)
        tn = _pick(N, 512 if tm > 1408 else (1024 if tm > 1024 else 1408))
        tk = _pick(K, 512)
    else:
        tm, tn, tk = _pick(M, 512), _pick(N), _pick(K)
    nk = K // tk
    dims = {'nn': ((1,), (0,)), 'nt': ((1,), (1,)), 'tn': ((0,), (0,))}[mode]

    def body(a_ref, b_ref, o_ref, acc_ref):
        k = pl.program_id(2)

        @pl.when(k == 0)
        def _():
            acc_ref[...] = jnp.zeros_like(acc_ref)

        acc_ref[...] += lax.dot_general(a_ref[...].astype(bf16), b_ref[...].astype(bf16), (dims, ((), ())),
                                        preferred_element_type=f32)

        @pl.when(k == nk - 1)
        def _():
            o_ref[...] = acc_ref[...].astype(o_ref.dtype)

    a_spec = pl.BlockSpec((tk, tm), lambda i, j, k: (k, i)) if mode == 'tn' else pl.BlockSpec((tm, tk), lambda i, j, k: (i, k))
    b_spec = pl.BlockSpec((tn, tk), lambda i, j, k: (j, k)) if mode == 'nt' else pl.BlockSpec((tk, tn), lambda i, j, k: (k, j))
    return pl.pallas_call(
        body, name=name, grid=(M // tm, N // tn, nk),
        in_specs=[a_spec, b_spec], out_specs=pl.BlockSpec((tm, tn), lambda i, j, k: (i, j)),
        out_shape=jax.ShapeDtypeStruct((M, N), out_dtype),
        scratch_shapes=[pltpu.VMEM((tm, tn), f32)],
        compiler_params=pltpu.CompilerParams(dimension_semantics=("parallel", "parallel", "arbitrary"),
                                             vmem_limit_bytes=VMEM_LIMIT),
    )(a, b)


def _rowcall(name, fn, L, tm, rows, consts=(), out_rows=(), out_accs=(), prev=(), nxt=()):
    nsteps = L // tm
    nb8 = tm // 8
    last8 = L // 8 - 1
    n_r, n_p, n_x, n_c, n_or = len(rows), len(prev), len(nxt), len(consts), len(out_rows)

    def body(*refs):
        i = pl.program_id(0)
        vals = [r[...] for r in refs[:n_r + n_p + n_x + n_c]]
        R, P = vals[:n_r], vals[n_r:n_r + n_p]
        X, C = vals[n_r + n_p:n_r + n_p + n_x], vals[n_r + n_p + n_x:]
        o_refs = refs[n_r + n_p + n_x + n_c:]
        outs_r, outs_a = fn(i, nsteps, R, P, X, C)
        for ref, v in zip(o_refs[:n_or], outs_r, strict=True):
            ref[...] = v.astype(ref.dtype)
        if out_accs:
            @pl.when(i == 0)
            def _():
                for ref in o_refs[n_or:]:
                    ref[...] = jnp.zeros_like(ref)

            for ref, v in zip(o_refs[n_or:], outs_a, strict=True):
                ref[...] += v

    def const_spec(c):
        nd = c.ndim
        return pl.BlockSpec(c.shape, lambda i: (0,) * nd)

    in_specs = ([pl.BlockSpec((tm, a.shape[1]), lambda i: (i, 0)) for a in rows]
                + [pl.BlockSpec((8, rows[j].shape[1]), lambda i: (jnp.maximum(i * nb8 - 1, 0), 0)) for j in prev]
                + [pl.BlockSpec((8, rows[j].shape[1]), lambda i: (jnp.minimum((i + 1) * nb8, last8), 0)) for j in nxt]
                + [const_spec(c) for c in consts])
    out_specs = ([pl.BlockSpec((tm, c), lambda i: (i, 0)) for c, _ in out_rows]
                 + [pl.BlockSpec(s, lambda i: (0, 0)) for s in out_accs])
    out_shape = ([jax.ShapeDtypeStruct((L, c), dt) for c, dt in out_rows]
                 + [jax.ShapeDtypeStruct(s, f32) for s in out_accs])
    args = list(rows) + [rows[j] for j in prev] + [rows[j] for j in nxt] + list(consts)
    return pl.pallas_call(
        body, name=name, grid=(nsteps,), in_specs=in_specs, out_specs=out_specs, out_shape=out_shape,
        compiler_params=pltpu.CompilerParams(dimension_semantics=("arbitrary",), vmem_limit_bytes=VMEM_LIMIT),
    )(*args)


def _shift_down(x, prev8, i, k):
    rolled = pltpu.roll(x, k, axis=0)
    pfix = jnp.where(i > 0, pltpu.roll(prev8, k, axis=0), 0.0)
    row8 = lax.broadcasted_iota(jnp.int32, pfix.shape, 0)
    top = jnp.where(row8 < k, pfix, rolled[:8])
    return top if x.shape[0] == 8 else jnp.concatenate([top, rolled[8:]], axis=0)


def _shift_up(x, next8, i, nsteps, k):
    tm = x.shape[0]
    rolled = pltpu.roll(x, tm - k, axis=0)
    nfix = jnp.where(i < nsteps - 1, pltpu.roll(next8, 8 - k, axis=0), 0.0)
    row8 = lax.broadcasted_iota(jnp.int32, nfix.shape, 0)
    bot = jnp.where(row8 >= 8 - k, nfix, rolled[tm - 8:])
    return jnp.concatenate([rolled[:tm - 8], bot], axis=0)


def _sum0(x):
    return jnp.sum(x, axis=0, keepdims=True)


def _rms(x, g):
    return x * lax.rsqrt(jnp.mean(x * x, axis=-1, keepdims=True) + NORM_EPS) * g


def _softplus(x):
    return jnp.maximum(x, 0.0) + jnp.log(1.0 + jnp.exp(-jnp.abs(x)))


def _gelu(x):
    return 0.5 * x * (1.0 + jnp.tanh(0.7978845608028654 * (x + 0.044715 * x * x * x)))


def _dot32(a, b):
    return jnp.dot(a, b, preferred_element_type=f32, precision=lax.Precision.HIGHEST)


def _seg_raw(x, E):
    hi = x.astype(bf16)
    r1 = x - hi.astype(f32)
    mid = r1.astype(bf16)
    lo = (r1 - mid.astype(f32)).astype(bf16)
    Eb = E.astype(bf16)
    dot = lambda t: jnp.dot(t, Eb, preferred_element_type=f32)
    return (dot(lo) + dot(mid)) + dot(hi)


@jax.custom_vjp
def _seg(x, E):
    return _seg_raw(x, E)


_seg.defvjp(lambda x, E: (_seg_raw(x, E), E), lambda E, g: (_seg_raw(g, E), jnp.zeros_like(E)))


def _prep(q, w0, a0, k_k, k_a, w2p, a2p, g2, E):
    r, k, v = q[:, 0:512], q[:, 512:1024], q[:, 1024:1536]
    wa, gd = q[:, 1536:1664], q[:, 1664:1792]
    wlog = -_softplus(-(w0 + _dot32(jnp.tanh(wa), w2p))) - 0.5
    lw = -jnp.exp(wlog)
    a = jax.nn.sigmoid(a0 + _dot32(wa, a2p))
    g = _dot32(jax.nn.sigmoid(gd), g2)
    kk = k * k_k
    kkn = kk / jnp.maximum(jnp.sqrt(_seg(kk * kk, E)), 1e-12)
    k2 = k * (1.0 + (a - 1.0) * k_a)
    return r, lw, k2, v, -kkn, kkn * a, g


def _rwkv_out(y, r, k2, v, g, lnx_w, lnx_b, r_k, E):
    mean = _seg(y, E) * (1.0 / HEAD)
    yc = y - mean
    var = _seg(yc * yc, E) * (1.0 / HEAD)
    yn = yc * lax.rsqrt(var + LNX_EPS) * lnx_w + lnx_b
    bonus = _seg(r * k2 * r_k, E) * v
    return (yn + bonus) * g


def _s5_mid(ysc, u, d):
    return _gelu(ysc + d * u)


def _s5_glu(yg, z2, b_glu):
    return yg * jax.nn.sigmoid(z2 + b_glu)


def _merge(gp, o_r, o_s, b_gate):
    gates = jax.nn.sigmoid(gp + b_gate)
    return gates[:, :D_MODEL] * o_r + gates[:, D_MODEL:] * o_s


def _act(zc):
    return _gelu(zc[:, :D_FF]) * zc[:, D_FF:]


def _s5_disc(a_re, a_im, ls, b_re, b_im):
    dt = jnp.exp(ls)
    er = jnp.exp(a_re * dt)
    ar, ai = er * jnp.cos(a_im * dt), er * jnp.sin(a_im * dt)
    x, y = ar - 1.0, ai
    den = a_re * a_re + a_im * a_im
    fr, fi = (x * a_re + y * a_im) / den, (y * a_re - x * a_im) / den
    return ar, ai, fr * b_re - fi * b_im, fr * b_im + fi * b_re


_DIMS = {'nn': ((1,), (0,)), 'nt': ((1,), (1,)), 'tn': ((0,), (0,))}


def _raw_bdot(a, b, mode):
    return lax.dot_general(a.astype(bf16), b.astype(bf16), (_DIMS[mode], ((), ())), preferred_element_type=f32)


@functools.partial(jax.custom_vjp, nondiff_argnums=(2,))
def _bdot(a, b, mode):
    return _raw_bdot(a, b, mode)


def _bdot_fwd(a, b, mode):
    return _raw_bdot(a, b, mode), (a, b)


def _bdot_bwd(mode, res, g):
    a, b = res
    if mode == 'nn':
        return _raw_bdot(g, b, 'nt'), _raw_bdot(a, g, 'tn')
    if mode == 'nt':
        return _raw_bdot(g, b, 'nn'), _raw_bdot(g, a, 'tn')
    return _raw_bdot(b, g, 'nt'), _raw_bdot(a, g, 'nn')


_bdot.defvjp(_bdot_fwd, _bdot_bwd)


def _wkv_chunk(S0, r, lw, k, v, a, b, tri, bd):
    C = r[0].shape[0]
    P = range(len(r))
    lane = lax.broadcasted_iota(jnp.int32, (1, 2 * HEAD), 1)
    m0, m1 = (lane < HEAD).astype(f32), (lane >= HEAD).astype(f32)
    cat = lambda *xs: jnp.concatenate(xs, axis=0)
    stack = lambda x: cat(x * m0, x * m1)
    unstack = lambda x2: m0 * x2[:C] + m1 * x2[C:]
    rid = lax.broadcasted_iota(jnp.int32, (2 * C, 2 * C), 0)
    cid = lax.broadcasted_iota(jnp.int32, (2 * C, 2 * C), 1)
    same = (rid < C) == (cid < C)
    eye2 = (rid == cid).astype(f32)
    tri2 = (same & (rid >= cid)).astype(f32)
    sl2 = tri2 - eye2
    cum = [_dot32(tri, lw[p]) for p in P]
    g = [jnp.exp(cum[p]) for p in P]
    gi = [jnp.exp(-cum[p]) for p in P]
    at = [a[p] * jnp.exp(cum[p] - lw[p]) for p in P]
    rt = [r[p] * g[p] for p in P]
    kb = [k[p] * gi[p] for p in P]
    bb = [b[p] * gi[p] for p in P]
    lhs = [cat(stack(at[p]), stack(rt[p])) for p in P]
    pb = [_bdot(lhs[p], stack(bb[p]), 'nt') for p in P]
    pk = [_bdot(lhs[p], stack(kb[p]), 'nt') for p in P]
    aab = [pb[p][:2 * C] * sl2 for p in P]
    base = [_bdot(cat(at[p], rt[p]), S0[p], 'nt') for p in P]
    t = [_bdot(cat(pk[p][:2 * C] * sl2, pk[p][2 * C:] * tri2), cat(v[p], v[p]), 'nn') for p in P]
    rhs = [cat(base[p][:C], base[p][:C]) + t[p][:2 * C] for p in P]
    x = [eye2 + aab[p] for p in P]
    pw = aab
    n = 1
    while 2 * n < C:
        pw = [_bdot(pw[p], pw[p], 'nn') for p in P]
        x = [x[p] + _bdot(x[p], pw[p], 'nn') for p in P]
        n *= 2
    u = [unstack(_bdot(x[p], rhs[p], 'nn')) for p in P]
    w2 = [_bdot(pb[p][2 * C:] * tri2, cat(u[p], u[p]), 'nn') for p in P]
    y = [base[p][C:] + unstack(t[p][2 * C:]) + unstack(w2[p]) for p in P]
    S1 = [g[p][C - 1:C, :] * (S0[p] + bd * _bdot(cat(v[p], u[p]), cat(kb[p], bb[p]), 'tn')) for p in P]
    return y, S1


def _pairs(x):
    return [x[:, 2 * HEAD * p:2 * HEAD * (p + 1)] for p in range(HEADS // 2)]


def _wkv_consts():
    tri = jnp.tril(jnp.ones((WKV_C, WKV_C), f32))
    hid = jnp.arange(2 * HEAD) // HEAD
    return tri, (hid[:, None] == hid[None, :]).astype(f32)


def _wkv7_fwd(r, lw, k, v, a, b):
    L = r.shape[0]
    nc, npair = L // WKV_C, HEADS // 2

    def body(r_ref, lw_ref, k_ref, v_ref, a_ref, b_ref, tri_ref, bd_ref, y_ref, ck_ref, s_ref):
        @pl.when(pl.program_id(0) == 0)
        def _():
            s_ref[...] = jnp.zeros_like(s_ref)

        s0 = [s_ref[p] for p in range(npair)]
        for p in range(npair):
            ck_ref[0, p] = s0[p]
        y, s1 = _wkv_chunk(s0, *(_pairs(x) for x in (r_ref, lw_ref, k_ref, v_ref, a_ref, b_ref)), tri_ref[...], bd_ref[...])
        for p in range(npair):
            y_ref[:, 2 * HEAD * p:2 * HEAD * (p + 1)] = y[p]
            s_ref[p] = s1[p]

    row = pl.BlockSpec((WKV_C, RWKV_W), lambda c: (c, 0))
    sspec = pl.BlockSpec((1, npair, 2 * HEAD, 2 * HEAD), lambda c: (c, 0, 0, 0))
    return pl.pallas_call(
        body, name="wkv7_fwd", grid=(nc,),
        in_specs=[row] * 6 + [pl.BlockSpec((WKV_C, WKV_C), lambda c: (0, 0)), pl.BlockSpec((2 * HEAD, 2 * HEAD), lambda c: (0, 0))],
        out_specs=[row, sspec],
        out_shape=[jax.ShapeDtypeStruct((L, RWKV_W), f32), jax.ShapeDtypeStruct((nc, npair, 2 * HEAD, 2 * HEAD), f32)],
        scratch_shapes=[pltpu.VMEM((npair, 2 * HEAD, 2 * HEAD), f32)],
        compiler_params=pltpu.CompilerParams(dimension_semantics=("arbitrary",), vmem_limit_bytes=VMEM_LIMIT),
    )(r, lw, k, v, a, b, *_wkv_consts())


def _wkv7_bwd(r, lw, k, v, a, b, ck, dy):
    L = r.shape[0]
    nc, npair = L // WKV_C, HEADS // 2

    def body(r_ref, lw_ref, k_ref, v_ref, a_ref, b_ref, ck_ref, dy_ref, tri_ref, bd_ref,
             dr_ref, dlw_ref, dk_ref, dv_ref, da_ref, db_ref, ds_ref):
        @pl.when(pl.program_id(0) == 0)
        def _():
            ds_ref[...] = jnp.zeros_like(ds_ref)

        tri, bd = tri_ref[...], bd_ref[...]
        ins = [[ck_ref[0, p] for p in range(npair)]] + [_pairs(x) for x in (r_ref, lw_ref, k_ref, v_ref, a_ref, b_ref)]
        _, vjp = jax.vjp(lambda *t: _wkv_chunk(*t, tri, bd), *ins)
        gs = vjp((_pairs(dy_ref), [ds_ref[p] for p in range(npair)]))
        for p in range(npair):
            ds_ref[p] = gs[0][p]
            for ref, gval in zip((dr_ref, dlw_ref, dk_ref, dv_ref, da_ref, db_ref), gs[1:]):
                ref[:, 2 * HEAD * p:2 * HEAD * (p + 1)] = gval[p]

    row = pl.BlockSpec((WKV_C, RWKV_W), lambda c: (nc - 1 - c, 0))
    sspec = pl.BlockSpec((1, npair, 2 * HEAD, 2 * HEAD), lambda c: (nc - 1 - c, 0, 0, 0))
    return pl.pallas_call(
        body, name="wkv7_bwd", grid=(nc,),
        in_specs=[row] * 6 + [sspec, row, pl.BlockSpec((WKV_C, WKV_C), lambda c: (0, 0)),
                              pl.BlockSpec((2 * HEAD, 2 * HEAD), lambda c: (0, 0))],
        out_specs=[row] * 6,
        out_shape=[jax.ShapeDtypeStruct((L, RWKV_W), f32)] * 6,
        scratch_shapes=[pltpu.VMEM((npair, 2 * HEAD, 2 * HEAD), f32)],
        compiler_params=pltpu.CompilerParams(dimension_semantics=("arbitrary",), vmem_limit_bytes=VMEM_LIMIT),
    )(r, lw, k, v, a, b, ck, dy, *_wkv_consts())


def _cmul(ar, ai, xr, xi):
    return ar * xr - ai * xi, ar * xi + ai * xr


def _scan_init(a_ref, car_ref, pw_ref, reverse):
    car_ref[...] = jnp.zeros_like(car_ref)
    ar = jnp.broadcast_to(a_ref[:, :S5_N], (8, S5_N))
    ai = jnp.broadcast_to(a_ref[:, S5_N:], (8, S5_N))
    if reverse:
        ai = -ai
    row = lax.broadcasted_iota(jnp.int32, (8, S5_N), 0)
    pr, pi = ar, ai
    qr, qi = jnp.zeros((8, S5_N), f32), jnp.zeros((8, S5_N), f32)
    for e in range(1, 9):
        sel = (row == 8 - e) if reverse else (row == e - 1)
        qr, qi = jnp.where(sel, pr, qr), jnp.where(sel, pi, qi)
        if e in (1, 2, 4):
            j = (1, 2, 4).index(e)
            pw_ref[j, :, :S5_N] = pr
            pw_ref[j, :, S5_N:] = pi
        pr, pi = _cmul(pr, pi, ar, ai)
    pw_ref[3, :, :S5_N] = qr
    pw_ref[3, :, S5_N:] = qi


def _scan_tile(x_ref, o_ref, car_ref, pw_ref, reverse):
    ng = x_ref.shape[0] // 8
    row = lax.broadcasted_iota(jnp.int32, (8, S5_N), 0)

    def group(gi, carry):
        g = (ng - 1 - gi) if reverse else gi
        t0 = pl.multiple_of(g * 8, 8)
        xr, xi = x_ref[pl.ds(t0, 8), :S5_N], x_ref[pl.ds(t0, 8), S5_N:]
        for j, d in enumerate((1, 2, 4)):
            if reverse:
                sr = jnp.where(row < 8 - d, pltpu.roll(xr, 8 - d, axis=0), 0.0)
                si = jnp.where(row < 8 - d, pltpu.roll(xi, 8 - d, axis=0), 0.0)
            else:
                sr = jnp.where(row >= d, pltpu.roll(xr, d, axis=0), 0.0)
                si = jnp.where(row >= d, pltpu.roll(xi, d, axis=0), 0.0)
            mr, mi = _cmul(pw_ref[j, :, :S5_N], pw_ref[j, :, S5_N:], sr, si)
            xr, xi = xr + mr, xi + mi
        cr, ci = carry
        mr, mi = _cmul(pw_ref[3, :, :S5_N], pw_ref[3, :, S5_N:], cr, ci)
        xr, xi = xr + mr, xi + mi
        o_ref[pl.ds(t0, 8), :S5_N] = xr
        o_ref[pl.ds(t0, 8), S5_N:] = xi
        e = 0 if reverse else 7
        return (jnp.broadcast_to(xr[e:e + 1, :], (8, S5_N)), jnp.broadcast_to(xi[e:e + 1, :], (8, S5_N)))

    cr, ci = lax.fori_loop(0, ng, group, (car_ref[:, :S5_N], car_ref[:, S5_N:]))
    car_ref[:, :S5_N] = cr
    car_ref[:, S5_N:] = ci


_CB, _SB = 128, 512


def _cblk(k):
    return slice(_CB * k, _CB * (k + 1))


def _sblk(j):
    return slice(_SB * j, _SB * (j + 1))


def _s5_fwd(u, bmat, cmat, abar):
    L = u.shape[0]
    nt = L // S5_T

    def body(u_ref, b_ref, c_ref, a_ref, st_ref, y_ref, bu_ref, car_ref, pw_ref):
        @pl.when(pl.program_id(0) == 0)
        def _():
            _scan_init(a_ref, car_ref, pw_ref, False)

        for j in range(8):
            bu_ref[:, _sblk(j)] = _raw_bdot(u_ref[:, _cblk(j % 4)], b_ref[_cblk(j % 4), _sblk(j)], 'nn')
        _scan_tile(bu_ref, st_ref, car_ref, pw_ref, False)
        for k in range(4):
            y_ref[:, _cblk(k)] = (_raw_bdot(st_ref[:, _sblk(k)], c_ref[_sblk(k), _cblk(k)], 'nn')
                                  + _raw_bdot(st_ref[:, _sblk(4 + k)], c_ref[_sblk(4 + k), _cblk(k)], 'nn'))

    whole = lambda shape: pl.BlockSpec(shape, lambda i: (0, 0))
    return pl.pallas_call(
        body, name="s5_fwd", grid=(nt,),
        in_specs=[pl.BlockSpec((S5_T, S5_W), lambda i: (i, 0)), whole(bmat.shape), whole(cmat.shape), whole(abar.shape)],
        out_specs=[pl.BlockSpec((S5_T, 2 * S5_N), lambda i: (i, 0)), pl.BlockSpec((S5_T, S5_W), lambda i: (i, 0))],
        out_shape=[jax.ShapeDtypeStruct((L, 2 * S5_N), f32), jax.ShapeDtypeStruct((L, S5_W), f32)],
        scratch_shapes=[pltpu.VMEM((S5_T, 2 * S5_N), f32), pltpu.VMEM((8, 2 * S5_N), f32), pltpu.VMEM((4, 8, 2 * S5_N), f32)],
        compiler_params=pltpu.CompilerParams(dimension_semantics=("arbitrary",), vmem_limit_bytes=VMEM_LIMIT),
    )(u, bmat, cmat, abar)


def _s5_bwd(dy, st, u, du_direct, bmat, cmat, abar):
    L = u.shape[0]
    nt = L // S5_T
    nb8 = S5_T // 8

    def body(dy_ref, st_ref, sp_ref, u_ref, dud_ref, b_ref, c_ref, a_ref, du_ref, db_ref, dc_ref, da_ref, lam_ref, car_ref,
             pw_ref):
        i = pl.program_id(0)

        @pl.when(i == 0)
        def _():
            _scan_init(a_ref, car_ref, pw_ref, True)
            db_ref[...] = jnp.zeros_like(db_ref)
            dc_ref[...] = jnp.zeros_like(dc_ref)
            da_ref[...] = jnp.zeros_like(da_ref)

        for j in range(8):
            lam_ref[:, _sblk(j)] = _raw_bdot(dy_ref[:, _cblk(j % 4)], c_ref[_sblk(j), _cblk(j % 4)], 'nt')
        _scan_tile(lam_ref, lam_ref, car_ref, pw_ref, True)
        for k in range(4):
            du_ref[:, _cblk(k)] = (dud_ref[:, _cblk(k)] + _raw_bdot(lam_ref[:, _sblk(k)], b_ref[_cblk(k), _sblk(k)], 'nt')
                                   + _raw_bdot(lam_ref[:, _sblk(4 + k)], b_ref[_cblk(k), _sblk(4 + k)], 'nt')
                                   ).astype(du_ref.dtype)
            sr = _shift_down(st_ref[:, _sblk(k)], sp_ref[:, _sblk(k)], nt - 1 - i, 1)
            si = _shift_down(st_ref[:, _sblk(4 + k)], sp_ref[:, _sblk(4 + k)], nt - 1 - i, 1)
            lr, li = lam_ref[:, _sblk(k)], lam_ref[:, _sblk(4 + k)]
            da_ref[:, _sblk(k)] += _sum0(lr * sr + li * si)
            da_ref[:, _sblk(4 + k)] += _sum0(li * sr - lr * si)
        for j in range(8):
            db_ref[j] += _raw_bdot(u_ref[:, _cblk(j % 4)], lam_ref[:, _sblk(j)], 'tn')
            dc_ref[j] += _raw_bdot(st_ref[:, _sblk(j)], dy_ref[:, _cblk(j % 4)], 'tn')

    whole = lambda shape: pl.BlockSpec(shape, lambda i: (0,) * len(shape))
    rev = lambda i: (nt - 1 - i, 0)
    return pl.pallas_call(
        body, name="s5_bwd", grid=(nt,),
        in_specs=[pl.BlockSpec((S5_T, S5_W), rev), pl.BlockSpec((S5_T, 2 * S5_N), rev),
                  pl.BlockSpec((8, 2 * S5_N), lambda i: (jnp.maximum((nt - 1 - i) * nb8 - 1, 0), 0)),
                  pl.BlockSpec((S5_T, S5_W), rev), pl.BlockSpec((S5_T, S5_W), rev), whole(bmat.shape), whole(cmat.shape),
                  whole(abar.shape)],
        out_specs=[pl.BlockSpec((S5_T, S5_W), rev), whole((8, _CB, _SB)), whole((8, _SB, _CB)), whole((1, 2 * S5_N))],
        out_shape=[jax.ShapeDtypeStruct((L, S5_W), bf16), jax.ShapeDtypeStruct((8, _CB, _SB), f32),
                   jax.ShapeDtypeStruct((8, _SB, _CB), f32), jax.ShapeDtypeStruct((1, 2 * S5_N), f32)],
        scratch_shapes=[pltpu.VMEM((S5_T, 2 * S5_N), f32), pltpu.VMEM((8, 2 * S5_N), f32), pltpu.VMEM((4, 8, 2 * S5_N), f32)],
        compiler_params=pltpu.CompilerParams(dimension_semantics=("arbitrary",), vmem_limit_bytes=VMEM_LIMIT),
    )(dy, st, st, u, du_direct, bmat, cmat, abar)


def _s5_disc_fwd(a_re, a_im, ls, b_re, b_im):
    def body(a_re_ref, a_im_ref, ls_ref, b_re_ref, b_im_ref, ar_ref, ai_ref, br_ref, bi_ref):
        outs = _s5_disc(a_re_ref[...], a_im_ref[...], ls_ref[...], b_re_ref[...], b_im_ref[...])
        for ref, v in zip((ar_ref, ai_ref, br_ref, bi_ref), outs):
            ref[...] = v

    c1, c16 = jax.ShapeDtypeStruct((S5_N, 1), f32), jax.ShapeDtypeStruct((S5_N, S5_C), f32)
    return pl.pallas_call(body, name="s5_disc", out_shape=[c1, c1, c16, c16])(a_re, a_im, ls, b_re, b_im)


def _s5_disc_bwd(a_re, a_im, ls, b_re, b_im, d_ar, d_ai, d_br, d_bi, seg):
    def body(a_re_ref, a_im_ref, ls_ref, b_re_ref, b_im_ref, g1, g2, g3, g4, seg_ref, o1, o2, o3, o4, o5):
        _, vjp = jax.vjp(_s5_disc, a_re_ref[...], a_im_ref[...], ls_ref[...], b_re_ref[...], b_im_ref[...])
        da_re, da_im, dls, db_re, db_im = vjp((g1[...], g2[...], g3[...], g4[...]))
        o1[...] = da_re
        o2[...] = da_im
        o3[...] = _dot32(seg_ref[...], dls)
        o4[...] = db_re
        o5[...] = db_im

    c1, c16 = jax.ShapeDtypeStruct((S5_N, 1), f32), jax.ShapeDtypeStruct((S5_N, S5_C), f32)
    return pl.pallas_call(body, name="s5_disc_bwd", out_shape=[c1, c1, jax.ShapeDtypeStruct((S5_G, 1), f32), c16, c16])(
        a_re, a_im, ls, b_re, b_im, d_ar, d_ai, d_br, d_bi, seg)


ANY = pl.BlockSpec(memory_space=pl.ANY)

GATHER = {'w_in': ((4352, 1024), 0), 'ffn_w_up': ((1024, 5632), 1), 'w_branch_rwkv': ((512, 1024), 1),
          'w_branch_s5': ((512, 1024), 1), 'w_out': ((1024, 1024), 0), 's5_w_glu': ((512, 512), 0),
          'ffn_w_down': ((2816, 1024), 0), 'rwkv_w2': ((64, 512), 1), 'rwkv_a2': ((64, 512), 1),
          'rwkv_g2': ((128, 512), 1), 'ffn_conv_w': ((8, 5632), 1)}
BIG = ['w_in', 'ffn_w_up', 'w_branch_rwkv', 'w_branch_s5', 'w_out', 's5_w_glu', 'ffn_w_down']
TINY = ['rwkv_w2', 'rwkv_a2', 'rwkv_g2', 'ffn_conv_w']
SMALL = [n for n in WEIGHTS if n not in GATHER]
SMALL_ROWS = 320
ADAM_ROWS = 256


def _mo(v, m):
    return v if isinstance(v, int) else pl.multiple_of(v, m)


def _slab(ref, name, j, h=None):
    (R, Cn), axis = GATHER[name]
    if axis == 0:
        rs = R // 4
        if h is None:
            return ref.at[pl.ds(_mo(j * rs, 16), rs), :]
        return ref.at[pl.ds(_mo(j * rs + h * (rs // 2), 8), rs // 2), :]
    cols = pl.ds(_mo(j * (Cn // 4), 128), Cn // 4)
    if h is None:
        return ref.at[:, cols]
    return ref.at[pl.ds(_mo(h * (R // 2), 8), R // 2), cols]


def _half_shape(name):
    (R, Cn), axis = GATHER[name]
    return (R // 8, Cn) if axis == 0 else (R // 2, Cn // 4)


def _chip_peers(px, py):
    return [((1 - px) if (k >> 1) else px, (1 - py) if (k & 1) else py) for k in (1, 2, 3)]


def _run_copies(copies):
    for cp in copies:
        cp.start()
    for cp in copies:
        cp.wait()


def _gather_weights(blocks):
    names = list(blocks)
    n = len(names)

    def body(*refs):
        ins, outs = refs[:n], refs[n:2 * n]
        ssem, rsem, lsem = refs[2 * n:]
        px, py, pc = _mesh_pos()
        me = 2 * px + py
        copies = []
        for i, nm in enumerate(names):
            if nm in BIG:
                hr = blocks[nm].shape[0] // 2
                src, dst = ins[i].at[pl.ds(pl.multiple_of(pc * hr, 16), hr), :], _slab(outs[i], nm, me, pc)
            else:
                src, dst = ins[i], _slab(outs[i], nm, me)
            copies.append(pltpu.make_async_copy(src, dst, lsem.at[i]))
            for k, (qx, qy) in enumerate(_chip_peers(px, py)):
                copies.append(pltpu.make_async_remote_copy(src, dst, ssem.at[3 * i + k], rsem.at[3 * i + k],
                                                           device_id=(qx, qy, pc), device_id_type=MESH))
        _run_copies(copies)

    outs = pl.pallas_call(
        body, name="gather_weights", in_specs=[ANY] * n, out_specs=[ANY] * n,
        out_shape=[jax.ShapeDtypeStruct(GATHER[nm][0], blocks[nm].dtype) for nm in names],
        scratch_shapes=[pltpu.SemaphoreType.DMA((3 * n,)), pltpu.SemaphoreType.DMA((3 * n,)), pltpu.SemaphoreType.DMA((n,))],
    )(*[blocks[nm] for nm in names])
    return dict(zip(names, outs))


def _gather_pair(full):
    n = len(BIG)

    def body(*refs):
        ins, outs = refs[:n], refs[n:2 * n]
        ssem, rsem = refs[2 * n:]
        px, py, pc = _mesh_pos()
        copies = []
        for i, nm in enumerate(BIG):
            for j in range(4):
                copies.append(pltpu.make_async_remote_copy(_slab(ins[i], nm, j, pc), _slab(outs[i], nm, j, pc),
                                                           ssem.at[4 * i + j], rsem.at[4 * i + j],
                                                           device_id=(px, py, 1 - pc), device_id_type=MESH))
        _run_copies(copies)

    outs = pl.pallas_call(
        body, name="gather_weights_pair", in_specs=[ANY] * n, out_specs=[ANY] * n,
        out_shape=[jax.ShapeDtypeStruct(full[nm].shape, full[nm].dtype) for nm in BIG],
        input_output_aliases={i: i for i in range(n)},
        scratch_shapes=[pltpu.SemaphoreType.DMA((4 * n,)), pltpu.SemaphoreType.DMA((4 * n,))],
    )(*[full[nm] for nm in BIG])
    return dict(zip(BIG, outs))


def _grads_to_sibling(G, small):
    n = len(BIG)

    def body(*refs):
        g_refs, small_ref = refs[:n], refs[n]
        o_refs, small_o = refs[n + 1:2 * n + 1], refs[2 * n + 1]
        ssem, rsem = refs[2 * n + 2:]
        px, py, pc = _mesh_pos()
        sib = (px, py, 1 - pc)
        copies = []
        for i, nm in enumerate(BIG):
            for j in range(4):
                copies.append(pltpu.make_async_remote_copy(_slab(g_refs[i], nm, j, 1 - pc), o_refs[i].at[j],
                                                           ssem.at[4 * i + j], rsem.at[4 * i + j],
                                                           device_id=sib, device_id_type=MESH))
        copies.append(pltpu.make_async_remote_copy(small_ref, small_o, ssem.at[4 * n], rsem.at[4 * n],
                                                   device_id=sib, device_id_type=MESH))
        _run_copies(copies)

    outs = pl.pallas_call(
        body, name="grads_to_sibling", in_specs=[ANY] * (n + 1), out_specs=[ANY] * (n + 1),
        out_shape=[jax.ShapeDtypeStruct((4,) + _half_shape(nm), f32) for nm in BIG] + [jax.ShapeDtypeStruct(small.shape, f32)],
        scratch_shapes=[pltpu.SemaphoreType.DMA((4 * n + 1,)), pltpu.SemaphoreType.DMA((4 * n + 1,))],
    )(*[G[nm] for nm in BIG], small)
    return dict(zip(BIG, outs[:n])), outs[n]


def _pair_add(G, recv, small, small_recv):
    n = len(BIG)
    cidx = lax.axis_index("c").astype(jnp.int32).reshape(1)

    def body(c_ref, *refs):
        ins, outs = refs[:2 * n + 2], refs[2 * n + 2:]
        for i in range(n):
            outs[i][...] = (ins[i][...] + ins[n + i][...]).astype(bf16)
        outs[n][...] = ins[2 * n][...] + ins[2 * n + 1][...]

    g_specs, r_specs = [], []
    for nm in BIG:
        hr, hc = _half_shape(nm)
        if GATHER[nm][1] == 0:
            g_specs.append(pl.BlockSpec((hr // 2, hc), lambda j, i, c: ((2 * j + c[0]) * 2 + i, 0)))
        else:
            g_specs.append(pl.BlockSpec((hr // 2, hc), lambda j, i, c: (2 * c[0] + i, j)))
        r_specs.append(pl.BlockSpec((1, hr // 2, hc), lambda j, i, c: (j, i, 0)))
    sm = pl.BlockSpec((SMALL_ROWS // 8, PACK_W), lambda j, i, c: (2 * j + i, 0))
    outs = pl.pallas_call(
        body, name="grads_pair_sum",
        grid_spec=pltpu.PrefetchScalarGridSpec(num_scalar_prefetch=1, grid=(4, 2), in_specs=g_specs + r_specs + [sm, sm],
                                               out_specs=r_specs + [sm]),
        out_shape=[jax.ShapeDtypeStruct((4,) + _half_shape(nm), bf16) for nm in BIG] + [jax.ShapeDtypeStruct(small.shape, f32)],
        compiler_params=pltpu.CompilerParams(vmem_limit_bytes=VMEM_LIMIT),
    )(cidx, *[G[nm] for nm in BIG], *[recv[nm] for nm in BIG], small, small_recv)
    return dict(zip(BIG, outs[:n])), outs[n]


def _grads_chip_exchange(chip_sum, small):
    n = len(BIG)

    def body(*refs):
        ins, outs = refs[:n + 1], refs[n + 1:2 * n + 2]
        ssem, rsem, lsem = refs[2 * n + 2:]
        px, py, pc = _mesh_pos()
        me = 2 * px + py
        copies = []
        copies.append(pltpu.make_async_copy(ins[n], outs[n].at[me], lsem))
        for i in range(n + 1):
            pick = (lambda ref, j: ref.at[j]) if i < n else (lambda ref, j: ref)
            for k, (qx, qy) in enumerate(_chip_peers(px, py)):
                copies.append(pltpu.make_async_remote_copy(pick(ins[i], 2 * qx + qy), outs[i].at[me],
                                                           ssem.at[3 * i + k], rsem.at[3 * i + k],
                                                           device_id=(qx, qy, pc), device_id_type=MESH))
        _run_copies(copies)

    outs = pl.pallas_call(
        body, name="grads_chip_exchange", in_specs=[ANY] * (n + 1), out_specs=[ANY] * (n + 1),
        out_shape=[jax.ShapeDtypeStruct(chip_sum[nm].shape, chip_sum[nm].dtype) for nm in BIG]
        + [jax.ShapeDtypeStruct((4,) + small.shape, f32)],
        scratch_shapes=[pltpu.SemaphoreType.DMA((3 * n + 3,)), pltpu.SemaphoreType.DMA((3 * n + 3,)),
                        pltpu.SemaphoreType.DMA],
    )(*[chip_sum[nm] for nm in BIG], small)
    return dict(zip(BIG, outs[:n])), outs[n]


def _sum_slots(slots, chip_sum, small4):
    n = len(BIG)
    me = (2 * lax.axis_index("x") + lax.axis_index("y")).astype(jnp.int32).reshape(1)

    def body(me_ref, *refs):
        for i in range(n):
            own = refs[5 * i + 4][0].astype(f32)
            term = [jnp.where(me_ref[0] == k, own, refs[5 * i + k][0].astype(f32)) for k in range(4)]
            refs[5 * n + 1 + i][...] = ((term[0] + term[1]) + term[2]) + term[3]
        x = refs[5 * n]
        refs[6 * n + 1][...] = ((x[0] + x[1]) + x[2]) + x[3]

    in_specs, args, specs_out, shapes = [], [], [], []
    for nm in BIG:
        hr, hc = _half_shape(nm)
        for k in range(4):
            in_specs.append(pl.BlockSpec((1, hr // 2, hc), lambda i, m, k=k: (jnp.where(m[0] == k, (k + 1) % 4, k), i, 0)))
        in_specs.append(pl.BlockSpec((1, hr // 2, hc), lambda i, m: (m[0], i, 0)))
        args += [slots[nm]] * 4 + [chip_sum[nm]]
        specs_out.append(pl.BlockSpec((hr // 2, hc), lambda i, m: (i, 0)))
        shapes.append(jax.ShapeDtypeStruct((hr, hc), f32))
    in_specs.append(pl.BlockSpec((4, SMALL_ROWS // 2, PACK_W), lambda i, m: (0, i, 0)))
    specs_out.append(pl.BlockSpec((SMALL_ROWS // 2, PACK_W), lambda i, m: (i, 0)))
    shapes.append(jax.ShapeDtypeStruct((SMALL_ROWS, PACK_W), f32))
    outs = pl.pallas_call(
        body, name="grads_chip_sum",
        grid_spec=pltpu.PrefetchScalarGridSpec(num_scalar_prefetch=1, grid=(2,), in_specs=in_specs, out_specs=specs_out),
        out_shape=shapes, compiler_params=pltpu.CompilerParams(vmem_limit_bytes=VMEM_LIMIT),
    )(me, *args, small4)
    return dict(zip(BIG, outs[:n])), outs[n]


def _halves_to_sibling(half):
    n = len(BIG)

    def body(*refs):
        ins, outs = refs[:n], refs[n:2 * n]
        ssem, rsem = refs[2 * n:]
        px, py, pc = _mesh_pos()
        _run_copies([pltpu.make_async_remote_copy(ins[i], outs[i], ssem.at[i], rsem.at[i],
                                                  device_id=(px, py, 1 - pc), device_id_type=MESH) for i in range(n)])

    outs = pl.pallas_call(
        body, name="grads_halves_to_sibling", in_specs=[ANY] * n, out_specs=[ANY] * n,
        out_shape=[jax.ShapeDtypeStruct(_half_shape(nm), f32) for nm in BIG],
        scratch_shapes=[pltpu.SemaphoreType.DMA((n,)), pltpu.SemaphoreType.DMA((n,))],
    )(*[half[nm] for nm in BIG])
    return dict(zip(BIG, outs))


def _join_halves(mine, other, pc):
    hr = mine.shape[0]
    return lax.dynamic_slice_in_dim(jnp.concatenate([other, mine, other], axis=0), (1 - pc) * hr, 2 * hr, axis=0)


def _flat_pad(v):
    v = v.reshape(-1)
    return jnp.pad(v, (0, _ceil_to(v.shape[0], PACK_W) - v.shape[0]))


def _pack_rows(parts, rows):
    flat = jnp.concatenate([_flat_pad(p) for p in parts])
    return jnp.pad(flat, (0, rows * PACK_W - flat.shape[0])).reshape(rows, PACK_W)


def _unpack_rows(buf, shapes):
    flat = buf.reshape(-1)
    out, off = [], 0
    for shp in shapes:
        n = 1
        for d in shp:
            n *= d
        out.append(flat[off:off + n].reshape(shp))
        off += _ceil_to(n, PACK_W)
    return out


def _adamw_math(w_, g_, m_, v_):
    m2 = ADAM_B1 * m_ + (1.0 - ADAM_B1) * g_
    v2 = ADAM_B2 * v_ + (1.0 - ADAM_B2) * (g_ * g_)
    m_hat = m2 / (1.0 - ADAM_B1 ** ADAM_STEP)
    v_hat = v2 / (1.0 - ADAM_B2 ** ADAM_STEP)
    return -ADAM_LR * (m_hat / (jnp.sqrt(v_hat) + ADAM_EPS) + ADAM_WD * w_), m2, v2


def _adamw(groups):
    ng = len(groups)

    def body(*refs):
        ins, outs = refs[:4 * ng], refs[4 * ng:]
        for i in range(ng):
            res = _adamw_math(*(r[...] for r in ins[4 * i:4 * i + 4]))
            for ref, val in zip(outs[3 * i:3 * i + 3], res):
                ref[...] = val

    in_specs, out_specs, out_shape = [], [], []
    for grp in groups:
        R, Cn = grp[0].shape
        spec = pl.BlockSpec((R // 8, Cn), lambda i: (i, 0))
        in_specs += [spec] * 4
        out_specs += [spec] * 3
        out_shape += [jax.ShapeDtypeStruct((R, Cn), f32)] * 3
    outs = pl.pallas_call(
        body, name="adamw", grid=(8,), in_specs=in_specs, out_specs=out_specs, out_shape=out_shape,
        compiler_params=pltpu.CompilerParams(vmem_limit_bytes=VMEM_LIMIT),
    )(*[a for grp in groups for a in grp])
    return [tuple(outs[3 * i:3 * i + 3]) for i in range(ng)]


def _forward_backward(x, tgt, W, S):
    L = x.shape[0]
    TM, TMW, TS = 256, 128, 512
    row = lambda c, dt=f32: (c, dt)
    hid = jnp.arange(RWKV_W) // HEAD
    E = (hid[:, None] == hid[None, :]).astype(f32)
    seg = (jnp.arange(S5_N)[None, :] // S5_P == jnp.arange(S5_G)[:, None]).astype(f32)

    w_in_t = W['w_in']
    w_p, w_u, w_g = w_in_t[:N_RWKV], w_in_t[N_RWKV:N_RWKV + S5_W], w_in_t[N_RWKV + S5_W:]
    zpad = jnp.zeros((64, RWKV_W), f32)
    w2p = jnp.concatenate([W['rwkv_w2'], zpad], axis=0)
    a2p = jnp.concatenate([zpad, W['rwkv_a2']], axis=0)
    g2 = W['rwkv_g2']
    prep_consts = [S['rwkv_shift_mu'], S['rwkv_w0'], S['rwkv_a0'], S['rwkv_k_k'], S['rwkv_k_a'], w2p, a2p, g2, E]
    out_consts = [S['rwkv_lnx_w'], S['rwkv_lnx_b'], S['rwkv_r_k'], E]
    cw, cb = W['ffn_conv_w'][:3], S['ffn_conv_b']

    a_re, a_im = S['s5_a_re'].reshape(S5_N, 1), S['s5_a_im'].reshape(S5_N, 1)
    ls = jnp.repeat(S['s5_log_step'].reshape(S5_G, 1), S5_P, axis=0)
    b_re, b_im = S['s5_b_re'].reshape(S5_N, S5_C), S['s5_b_im'].reshape(S5_N, S5_C)
    ar, ai, bbr, bbi = _s5_disc_fwd(a_re, a_im, ls, b_re, b_im)
    abar = jnp.concatenate([ar.reshape(1, S5_N), ai.reshape(1, S5_N)], axis=1)
    eye = jnp.eye(S5_G, dtype=f32)

    def bdiag_in(bb):
        t = bb.reshape(S5_G, S5_P, S5_C).transpose(0, 2, 1)
        return (t[:, :, None, :] * eye[:, None, :, None]).reshape(S5_W, S5_N)

    def bdiag_out(cc):
        t = cc.transpose(0, 2, 1)
        return (t[:, :, None, :] * eye[:, None, :, None]).reshape(S5_N, S5_W)

    eye8 = jnp.eye(8, dtype=f32)

    def undiag_in(blocks):
        t = blocks.reshape(4, 8, S5_C, 8, S5_P)
        t = jnp.sum(t * eye8[None, :, None, :, None], axis=3)
        return t.reshape(S5_G, S5_C, S5_P).transpose(0, 2, 1).reshape(S5_N, S5_C)

    def undiag_out(blocks):
        t = blocks.reshape(4, 8, S5_P, 8, S5_C)
        t = jnp.sum(t * eye8[None, :, None, :, None], axis=3)
        return t.reshape(S5_G, S5_P, S5_C).transpose(0, 2, 1)

    bmat = jnp.concatenate([bdiag_in(bbr), bdiag_in(bbi)], axis=1).astype(bf16)
    cmat = jnp.concatenate([bdiag_out(S['s5_c_re'].reshape(S5_G, S5_C, S5_P)),
                            -bdiag_out(S['s5_c_im'].reshape(S5_G, S5_C, S5_P))], axis=0).astype(bf16)

    g1, g2n, g3, g4 = S['norm_mix_pre'], S['norm_mix_post'], S['norm_ffn_pre'], S['norm_ffn_post']
    (h1,) = _rowcall("norm_pre", lambda i, n, R, P, X, C: ((_rms(R[0], C[0]),), ()), L, TS, [x], [g1],
                     out_rows=[row(D_MODEL, bf16)])
    p = _mm(h1, w_p, 'nt', "mm_p")
    u = _mm(h1, w_u, 'nt', "mm_u")
    gp = _mm(h1, w_g, 'nt', "mm_g")

    def prep_fn(i, n, R, P, X, C):
        q = R[0] + (_shift_down(R[0], P[0], i, 1) - R[0]) * C[0]
        return _prep(q, *C[1:]), ()

    r, lw, k2, v, an, bv, g = _rowcall("rwkv_prep", prep_fn, L, TM, [p], prep_consts,
                                       out_rows=[row(RWKV_W)] * 7, prev=[0])
    y, ck = _wkv7_fwd(r, lw, k2, v, an, bv)
    (o_a,) = _rowcall("rwkv_out", lambda i, n, R, P, X, C: ((_rwkv_out(*R, *C),), ()), L, TM, [y, r, k2, v, g],
                      out_consts, out_rows=[row(RWKV_W, bf16)])
    o_r = _mm(o_a, W['w_branch_rwkv'], 'nn', "mm_br")

    st, ysc = _s5_fwd(u, bmat, cmat, abar)
    (yg,) = _rowcall("s5_mid", lambda i, n, R, P, X, C: ((_s5_mid(*R, *C),), ()), L, TS, [ysc, u], [S['s5_d']],
                     out_rows=[row(S5_W)])
    z2 = _mm(yg, W['s5_w_glu'], 'nn', "mm_glu")
    (o_b,) = _rowcall("s5_glu", lambda i, n, R, P, X, C: ((_s5_glu(*R, *C),), ()), L, TS, [yg, z2], [S['s5_b_glu']],
                      out_rows=[row(S5_W, bf16)])
    o_s = _mm(o_b, W['w_branch_s5'], 'nn', "mm_bs")

    (merged,) = _rowcall("merge", lambda i, n, R, P, X, C: ((_merge(*R, *C),), ()), L, TS, [gp, o_r, o_s],
                         [S['b_gate']], out_rows=[row(D_MODEL, bf16)])
    mixed = _mm(merged, W['w_out'], 'nn', "mm_out")

    def resid_fn(i, n, R, P, X, C):
        x1_ = R[0] + _rms(R[1], C[0])
        return (x1_, _rms(x1_, C[1])), ()

    x1, h2 = _rowcall("resid_norm", resid_fn, L, TS, [x, mixed], [g2n, g3], out_rows=[row(D_MODEL), row(D_MODEL, bf16)])

    z = _mm(h2, W['ffn_w_up'], 'nn', "mm_up")

    def conv(zt, zprev, i, cw_, cb_):
        z2s, z1s = _shift_down(zt, zprev, i, 2), _shift_down(zt, zprev, i, 1)
        return cb_ + cw_[0:1] * z2s + cw_[1:2] * z1s + cw_[2:3] * zt, z2s, z1s

    (act,) = _rowcall("conv_act", lambda i, n, R, P, X, C: ((_act(conv(R[0], P[0], i, C[0], C[1])[0]),), ()), L, TMW,
                      [z], [cw, cb], out_rows=[row(D_FF, bf16)], prev=[0])
    f = _mm(act, W['ffn_w_down'], 'nn', "mm_down")

    def final_fn(i, n, R, P, X, C):
        x1_, f_, t_ = R
        fn_, vjp = jax.vjp(_rms, f_, C[0])
        diff = x1_ + fn_ - t_
        loss = jnp.sum(diff * diff) * (0.5 / D_MODEL)
        dx2_ = diff * (1.0 / D_MODEL)
        df_, dg4_ = vjp(dx2_)
        return (df_, dx2_), (jnp.full((1, PACK_W), loss, f32), dg4_)

    df, dx2, loss, dg4 = _rowcall("loss_head", final_fn, L, TS, [x1, f, tgt], [g4],
                                  out_rows=[row(D_MODEL, bf16), row(D_MODEL)], out_accs=[(1, PACK_W), (1, D_MODEL)])
    G = {'norm_ffn_post': dg4}

    dact = _mm(df, W['ffn_w_down'], 'nt', "mm_down_dx")
    G['ffn_w_down'] = _mm(act, df, 'tn', "mm_down_dw")

    def conv_bwd_fn(i, n, R, P, X, C):
        z_, dact_ = R
        cw_, cb_ = C
        zc, z2s, z1s = conv(z_, P[0], i, cw_, cb_)
        _, vjp = jax.vjp(_act, zc)
        (dzc_,) = vjp(dact_)
        last8 = z_[z_.shape[0] - 8:]
        zcn = cb_ + cw_[0:1] * _shift_down(X[0], last8, 1, 2) + cw_[1:2] * _shift_down(X[0], last8, 1, 1) + cw_[2:3] * X[0]
        _, vjpn = jax.vjp(_act, zcn)
        (dzcn,) = vjpn(X[1])
        dz_ = (cw_[2:3] * dzc_ + cw_[1:2] * _shift_up(dzc_, dzcn, i, n, 1) + cw_[0:1] * _shift_up(dzc_, dzcn, i, n, 2))
        return (dz_,), (_sum0(dzc_), _sum0(dzc_ * z2s), _sum0(dzc_ * z1s), _sum0(dzc_ * z_))

    wide = (1, 2 * D_FF)
    dz, dcb, dcw0, dcw1, dcw2 = _rowcall("conv_act_bwd", conv_bwd_fn, L, TMW, [z, dact], [cw, cb],
                                         out_rows=[row(2 * D_FF, bf16)], out_accs=[wide] * 4, prev=[0], nxt=[0, 1])
    G['ffn_conv_b'] = dcb
    G['ffn_conv_w'] = jnp.concatenate([dcw0, dcw1, dcw2], axis=0)
    dh2 = _mm(dz, W['ffn_w_up'], 'nt', "mm_up_dx")
    G['ffn_w_up'] = _mm(h2, dz, 'tn', "mm_up_dw")

    def norm2_bwd_fn(i, n, R, P, X, C):
        x1_, mixed_, dx2_, dh2_ = R
        _, vjp3 = jax.vjp(_rms, x1_, C[1])
        dx1a, dg3_ = vjp3(dh2_)
        dx1_ = dx2_ + dx1a
        _, vjp2 = jax.vjp(_rms, mixed_, C[0])
        dmixed_, dg2_ = vjp2(dx1_)
        return (dx1_, dmixed_), (dg2_, dg3_)

    dx1, dmixed, dg2n, dg3 = _rowcall("norm_mid_bwd", norm2_bwd_fn, L, TS, [x1, mixed, dx2, dh2], [g2n, g3],
                                      out_rows=[row(D_MODEL), row(D_MODEL, bf16)], out_accs=[(1, D_MODEL)] * 2)
    G['norm_mix_post'], G['norm_ffn_pre'] = dg2n, dg3

    dmerged = _mm(dmixed, W['w_out'], 'nt', "mm_out_dx")
    G['w_out'] = _mm(merged, dmixed, 'tn', "mm_out_dw")

    def merge_bwd_fn(i, n, R, P, X, C):
        _, vjp = jax.vjp(_merge, R[0], R[1], R[2], C[0])
        dgp_, do_r_, do_s_, dbg_ = vjp(R[3])
        return (dgp_, do_r_, do_s_), (dbg_,)

    dgp, do_r, do_s, G['b_gate'] = _rowcall("merge_bwd", merge_bwd_fn, L, TS, [gp, o_r, o_s, dmerged], [S['b_gate']],
                                            out_rows=[row(2 * D_MODEL, bf16), row(D_MODEL, bf16), row(D_MODEL, bf16)],
                                            out_accs=[(1, 2 * D_MODEL)])
    do_a = _mm(do_r, W['w_branch_rwkv'], 'nt', "mm_br_dx")
    G['w_branch_rwkv'] = _mm(o_a, do_r, 'tn', "mm_br_dw")
    do_b = _mm(do_s, W['w_branch_s5'], 'nt', "mm_bs_dx")
    G['w_branch_s5'] = _mm(o_b, do_s, 'tn', "mm_bs_dw")

    def glu_bwd_fn(i, n, R, P, X, C):
        _, vjp = jax.vjp(_s5_glu, R[0], R[1], C[0])
        dyg1_, dz2_, dbg_ = vjp(R[2])
        return (dyg1_, dz2_), (dbg_,)

    dyg1, dz2, G['s5_b_glu'] = _rowcall("s5_glu_bwd", glu_bwd_fn, L, TS, [yg, z2, do_b], [S['s5_b_glu']],
                                        out_rows=[row(S5_W), row(S5_W, bf16)], out_accs=[(1, S5_W)])
    dyg2 = _mm(dz2, W['s5_w_glu'], 'nt', "mm_glu_dx")
    G['s5_w_glu'] = _mm(yg, dz2, 'tn', "mm_glu_dw")

    def mid_bwd_fn(i, n, R, P, X, C):
        _, vjp = jax.vjp(_s5_mid, R[0], R[1], C[0])
        dysc_, du_, dd_ = vjp(R[2] + R[3])
        return (dysc_, du_), (dd_,)

    dysc, du1, G['s5_d'] = _rowcall("s5_mid_bwd", mid_bwd_fn, L, TS, [ysc, u, dyg1, dyg2], [S['s5_d']],
                                    out_rows=[row(S5_W, bf16), row(S5_W)], out_accs=[(1, S5_W)])
    du, dbmat, dcmat, dabar = _s5_bwd(dysc, st, u, du1, bmat, cmat, abar)
    da_re, da_im, dls, db_re, db_im = _s5_disc_bwd(
        a_re, a_im, ls, b_re, b_im, dabar[:, :S5_N].reshape(S5_N, 1), dabar[:, S5_N:].reshape(S5_N, 1),
        undiag_in(dbmat[:4]), undiag_in(dbmat[4:]), seg)
    G['s5_a_re'], G['s5_a_im'], G['s5_log_step'] = da_re, da_im, dls
    G['s5_b_re'], G['s5_b_im'] = db_re, db_im
    G['s5_c_re'], G['s5_c_im'] = undiag_out(dcmat[:4]), -undiag_out(dcmat[4:])

    def out_bwd_fn(i, n, R, P, X, C):
        _, vjp = jax.vjp(_rwkv_out, *R[:5], *C)
        gs = vjp(R[5])
        return gs[:5], gs[5:8]

    dy, dr1, dk1, dv1, dg, dlw, dlb, drk = _rowcall("rwkv_out_bwd", out_bwd_fn, L, TM, [y, r, k2, v, g, do_a], out_consts,
                                                    out_rows=[row(RWKV_W)] * 5, out_accs=[(1, RWKV_W)] * 3)
    G['rwkv_lnx_w'], G['rwkv_lnx_b'], G['rwkv_r_k'] = dlw, dlb, drk
    dr2, dlwk, dk2b, dv2, dan, dbv = _wkv7_bwd(r, lw, k2, v, an, bv, ck, dy)

    def prep_bwd_fn(i, n, R, P, X, C):
        p_ = R[0]
        d1 = _shift_down(p_, P[0], i, 1) - p_
        q = p_ + d1 * C[0]
        _, vjp = jax.vjp(_prep, q, *C[1:])
        cots = (R[1] + R[2], R[3], R[4] + R[5], R[6] + R[7], R[8], R[9], R[10])
        gs = vjp(cots)
        return (gs[0],), (_sum0(gs[0] * d1),) + tuple(gs[1:8])

    small, lowr = (1, RWKV_W), (128, RWKV_W)
    dq, dmu, dw0, da0, dkk, dka, dw2p, da2p, dg2 = _rowcall(
        "rwkv_prep_bwd", prep_bwd_fn, L, TM, [p, dr1, dr2, dlwk, dk1, dk2b, dv1, dv2, dan, dbv, dg],
        prep_consts, out_rows=[row(N_RWKV)], out_accs=[(1, N_RWKV)] + [small] * 4 + [lowr] * 3, prev=[0])
    G['rwkv_shift_mu'], G['rwkv_w0'], G['rwkv_a0'], G['rwkv_k_k'], G['rwkv_k_a'] = dmu, dw0, da0, dkk, dka
    G['rwkv_w2'], G['rwkv_a2'], G['rwkv_g2'] = dw2p[:64], da2p[64:], dg2

    def shift_bwd_fn(i, n, R, P, X, C):
        dm = R[0] * C[0]
        return (R[0] - dm + _shift_up(dm, X[0] * C[0], i, n, 1),), ()

    (dp,) = _rowcall("shift_bwd", shift_bwd_fn, L, TS, [dq], [S['rwkv_shift_mu']], out_rows=[row(N_RWKV, bf16)], nxt=[0])

    dproj = jnp.concatenate([dp, du, dgp], axis=1)
    dh1 = _mm(dproj, w_in_t, 'nn', "mm_in_dx")
    G['w_in'] = _mm(dproj, h1, 'tn', "mm_in_dw")

    def norm1_bwd_fn(i, n, R, P, X, C):
        _, vjp = jax.vjp(_rms, R[0], C[0])
        dxa, dg1_ = vjp(R[2])
        return (R[1] + dxa,), (dg1_,)

    dx, G['norm_mix_pre'] = _rowcall("norm_pre_bwd", norm1_bwd_fn, L, TS, [x, dx1, dh1], [g1],
                                     out_rows=[row(D_MODEL)], out_accs=[(1, D_MODEL)])
    return loss, dx, G


def kernel(x, norm_mix_pre, norm_mix_post, norm_ffn_pre, norm_ffn_post, w_in, b_gate, rwkv_shift_mu, rwkv_w0, rwkv_w2, rwkv_a0, rwkv_a2, rwkv_g2, rwkv_k_k, rwkv_k_a, rwkv_r_k, rwkv_lnx_w, rwkv_lnx_b, s5_a_re, s5_a_im, s5_b_re, s5_b_im, s5_c_re, s5_c_im, s5_d, s5_log_step, s5_w_glu, s5_b_glu, w_branch_rwkv, w_branch_s5, w_out, ffn_w_up, ffn_conv_w, ffn_conv_b, ffn_w_down, loss_target, m_norm_mix_pre, m_norm_mix_post, m_norm_ffn_pre, m_norm_ffn_post, m_w_in, m_b_gate, m_rwkv_shift_mu, m_rwkv_w0, m_rwkv_w2, m_rwkv_a0, m_rwkv_a2, m_rwkv_g2, m_rwkv_k_k, m_rwkv_k_a, m_rwkv_r_k, m_rwkv_lnx_w, m_rwkv_lnx_b, m_s5_a_re, m_s5_a_im, m_s5_b_re, m_s5_b_im, m_s5_c_re, m_s5_c_im, m_s5_d, m_s5_log_step, m_s5_w_glu, m_s5_b_glu, m_w_branch_rwkv, m_w_branch_s5, m_w_out, m_ffn_w_up, m_ffn_conv_w, m_ffn_conv_b, m_ffn_w_down, v_norm_mix_pre, v_norm_mix_post, v_norm_ffn_pre, v_norm_ffn_post, v_w_in, v_b_gate, v_rwkv_shift_mu, v_rwkv_w0, v_rwkv_w2, v_rwkv_a0, v_rwkv_a2, v_rwkv_g2, v_rwkv_k_k, v_rwkv_k_a, v_rwkv_r_k, v_rwkv_lnx_w, v_rwkv_lnx_b, v_s5_a_re, v_s5_a_im, v_s5_b_re, v_s5_b_im, v_s5_c_re, v_s5_c_im, v_s5_d, v_s5_log_step, v_s5_w_glu, v_s5_b_glu, v_w_branch_rwkv, v_w_branch_s5, v_w_out, v_ffn_w_up, v_ffn_conv_w, v_ffn_conv_b, v_ffn_w_down):
    A = dict(locals())
    me = 2 * lax.axis_index("x") + lax.axis_index("y")
    blk = lambda n: A[n][0]

    mine = {n: (blk(n).T if n == 'w_in' else blk(n)).astype(bf16) for n in BIG}
    mine.update({n: blk(n) for n in TINY})
    mine['ffn_conv_w'] = jnp.pad(blk('ffn_conv_w'), ((0, 5), (0, 0)))
    W = _gather_weights(mine)
    W.update(_gather_pair(W))
    S = {n: A[n].reshape(1, -1) for n in SMALL}

    loss, dx, G = _forward_backward(x[0], loss_target[0], W, S)

    tiny_shapes = [G[n].shape for n in TINY]
    small_buf = _pack_rows([G[n] for n in SMALL] + [G[n] for n in TINY] + [loss], SMALL_ROWS)
    recv, small_recv = _grads_to_sibling(G, small_buf)
    chip_sum, small_sum = _pair_add(G, recv, small_buf, small_recv)
    slots, small4 = _grads_chip_exchange(chip_sum, small_sum)
    half, small_tot = _sum_slots(slots, chip_sum, small4)
    other = _halves_to_sibling(half)
    pc = lax.axis_index("c")
    grad = {n: _join_halves(half[n], other[n], pc) for n in BIG}
    grad['w_in'] = grad['w_in'].T
    vals = _unpack_rows(small_tot, [A[n].shape for n in SMALL] + tiny_shapes + [(1, PACK_W)])
    grad.update(zip(SMALL, vals))
    for n, full in zip(TINY, vals[len(SMALL):]):
        cs = A[n].shape[2]
        grad[n] = lax.dynamic_slice_in_dim(full, me * cs, cs, axis=1)
    loss_out = vals[-1][0, 0]

    packed = SMALL + TINY
    groups = [(blk(n), grad[n], blk('m_' + n), blk('v_' + n)) for n in BIG]
    groups.append(tuple(_pack_rows([src(n) for n in packed], ADAM_ROWS)
                        for src in (lambda n: A[n], lambda n: grad[n], lambda n: A['m_' + n], lambda n: A['v_' + n])))
    res = _adamw(groups)
    outs = [dict(), dict(), dict()]
    for n, r3 in zip(BIG, res[:-1]):
        for d, val in zip(outs, r3):
            d[n] = val
    for d, buf in zip(outs, res[-1]):
        d.update(zip(packed, _unpack_rows(buf, [A[n].shape for n in packed])))
    full = lambda d: [d[n].reshape(A[n].shape) for n in WEIGHTS]
    return (loss_out, dx[None], *full(grad), *full(outs[0]), *full(outs[1]), *full(outs[2]))
```

```python
import functools

import jax
import jax.numpy as jnp
from jax import lax
from jax.experimental import pallas as pl
from jax.experimental.pallas import tpu as pltpu

f32, bf16 = jnp.float32, jnp.bfloat16
MESH = pl.DeviceIdType.MESH

D_MODEL = 1024
RWKV_W = 512
HEADS, HEAD = 8, 64
N_RWKV = 1792
S5_W = 512
S5_G, S5_P, S5_C = 32, 64, 16
S5_N = S5_G * S5_P
D_FF = 2816
NORM_EPS = 1e-6
LNX_EPS = 64e-5
ADAM_LR, ADAM_B1, ADAM_B2, ADAM_EPS, ADAM_WD, ADAM_STEP = 0.001, 0.9, 0.999, 1e-08, 0.01, 10

VMEM_LIMIT = 48 * 1024 * 1024
PACK_W = 1024
WKV_C = 64
S5_T = 256

WEIGHTS = ['norm_mix_pre', 'norm_mix_post', 'norm_ffn_pre', 'norm_ffn_post', 'w_in', 'b_gate', 'rwkv_shift_mu',
           'rwkv_w0', 'rwkv_w2', 'rwkv_a0', 'rwkv_a2', 'rwkv_g2', 'rwkv_k_k', 'rwkv_k_a', 'rwkv_r_k', 'rwkv_lnx_w',
           'rwkv_lnx_b', 's5_a_re', 's5_a_im', 's5_b_re', 's5_b_im', 's5_c_re', 's5_c_im', 's5_d', 's5_log_step',
           's5_w_glu', 's5_b_glu', 'w_branch_rwkv', 'w_branch_s5', 'w_out', 'ffn_w_up', 'ffn_conv_w', 'ffn_conv_b',
           'ffn_w_down']


def _ceil_to(n, m):
    return -(-n // m) * m


def _mesh_pos():
    return lax.axis_index("x"), lax.axis_index("y"), lax.axis_index("c")


def _pick(d, cap=4096):
    for c in (1024, 1408, 2176, 896, 512, 256, 128):
        if c <= cap and d % c == 0:
            return c
    raise ValueError(d)


def _mm_resident(a, w, mode, name, M, N, K, out_dtype):
    budget = 40 * 1024 * 1024 - 2 * K * N
    tm = next(t for t in (512, 256, 128) if 2 * t * (K * a.dtype.itemsize + 4 * N) <= budget)
    dims = _DIMS[mode]

    def body(a_ref, w_ref, o_ref):
        o_ref[...] = lax.dot_general(a_ref[...].astype(bf16), w_ref[...], (dims, ((), ())),
                                     preferred_element_type=f32).astype(o_ref.dtype)

    return pl.pallas_call(
        body, name=name, grid=(M // tm,),
        in_specs=[pl.BlockSpec((tm, K), lambda i: (i, 0)),
                  pl.BlockSpec(w.shape, lambda i: (0, 0), pipeline_mode=pl.Buffered(1))],
        out_specs=pl.BlockSpec((tm, N), lambda i: (i, 0)), out_shape=jax.ShapeDtypeStruct((M, N), out_dtype),
        compiler_params=pltpu.CompilerParams(dimension_semantics=("parallel",), vmem_limit_bytes=VMEM_LIMIT),
    )(a, w)


def _mm(a, b, mode, name, out_dtype=f32):
    if mode == 'tn':
        (K, M), (K2, N) = a.shape, b.shape
    elif mode == 'nt':
        (M, K), (N, K2) = a.shape, b.shape
    else:
        (M, K), (K2, N) = a.shape, b.shape
    assert K == K2, (name, a.shape, b.shape)
    if mode != 'tn' and b.dtype == bf16:
        return _mm_resident(a, b, mode, name, M, N, K, out_dtype)
    if mode == 'tn':
        tm = _pick(M, 2176)
        tn = _pick(N, 512 if tm > 1408 else (1024 if tm > 1024 else 1408))
        tk = _pick(K, 512)
    else:
        tm, tn, tk = _pick(M, 512), _pick(N), _pick(K)
    nk = K // tk
    dims = {'nn': ((1,), (0,)), 'nt': ((1,), (1,)), 'tn': ((0,), (0,))}[mode]

    def body(a_ref, b_ref, o_ref, acc_ref):
        k = pl.program_id(2)

        @pl.when(k == 0)
        def _():
            acc_ref[...] = jnp.zeros_like(acc_ref)

        acc_ref[...] += lax.dot_general(a_ref[...].astype(bf16), b_ref[...].astype(bf16), (dims, ((), ())),
                                        preferred_element_type=f32)

        @pl.when(k == nk - 1)
        def _():
            o_ref[...] = acc_ref[...].astype(o_ref.dtype)

    a_spec = pl.BlockSpec((tk, tm), lambda i, j, k: (k, i)) if mode == 'tn' else pl.BlockSpec((tm, tk), lambda i, j, k: (i, k))
    b_spec = pl.BlockSpec((tn, tk), lambda i, j, k: (j, k)) if mode == 'nt' else pl.BlockSpec((tk, tn), lambda i, j, k: (k, j))
    return pl.pallas_call(
        body, name=name, grid=(M // tm, N // tn, nk),
        in_specs=[a_spec, b_spec], out_specs=pl.BlockSpec((tm, tn), lambda i, j, k: (i, j)),
        out_shape=jax.ShapeDtypeStruct((M, N), out_dtype),
        scratch_shapes=[pltpu.VMEM((tm, tn), f32)],
        compiler_params=pltpu.CompilerParams(dimension_semantics=("parallel", "parallel", "arbitrary"),
                                             vmem_limit_bytes=VMEM_LIMIT),
    )(a, b)


def _rowcall(name, fn, L, tm, rows, consts=(), out_rows=(), out_accs=(), prev=(), nxt=()):
    nsteps = L // tm
    nb8 = tm // 8
    last8 = L // 8 - 1
    n_r, n_p, n_x, n_c, n_or = len(rows), len(prev), len(nxt), len(consts), len(out_rows)

    def body(*refs):
        i = pl.program_id(0)
        vals = [r[...] for r in refs[:n_r + n_p + n_x + n_c]]
        R, P = vals[:n_r], vals[n_r:n_r + n_p]
        X, C = vals[n_r + n_p:n_r + n_p + n_x], vals[n_r + n_p + n_x:]
        o_refs = refs[n_r + n_p + n_x + n_c:]
        outs_r, outs_a = fn(i, nsteps, R, P, X, C)
        for ref, v in zip(o_refs[:n_or], outs_r, strict=True):
            ref[...] = v.astype(ref.dtype)
        if out_accs:
            @pl.when(i == 0)
            def _():
                for ref in o_refs[n_or:]:
                    ref[...] = jnp.zeros_like(ref)

            for ref, v in zip(o_refs[n_or:], outs_a, strict=True):
                ref[...] += v

    def const_spec(c):
        nd = c.ndim
        return pl.BlockSpec(c.shape, lambda i: (0,) * nd)

    in_specs = ([pl.BlockSpec((tm, a.shape[1]), lambda i: (i, 0)) for a in rows]
                + [pl.BlockSpec((8, rows[j].shape[1]), lambda i: (jnp.maximum(i * nb8 - 1, 0), 0)) for j in prev]
                + [pl.BlockSpec((8, rows[j].shape[1]), lambda i: (jnp.minimum((i + 1) * nb8, last8), 0)) for j in nxt]
                + [const_spec(c) for c in consts])
    out_specs = ([pl.BlockSpec((tm, c), lambda i: (i, 0)) for c, _ in out_rows]
                 + [pl.BlockSpec(s, lambda i: (0, 0)) for s in out_accs])
    out_shape = ([jax.ShapeDtypeStruct((L, c), dt) for c, dt in out_rows]
                 + [jax.ShapeDtypeStruct(s, f32) for s in out_accs])
    args = list(rows) + [rows[j] for j in prev] + [rows[j] for j in nxt] + list(consts)
    return pl.pallas_call(
        body, name=name, grid=(nsteps,), in_specs=in_specs, out_specs=out_specs, out_shape=out_shape,
        compiler_params=pltpu.CompilerParams(dimension_semantics=("arbitrary",), vmem_limit_bytes=VMEM_LIMIT),
    )(*args)


def _shift_down(x, prev8, i, k):
    rolled = pltpu.roll(x, k, axis=0)
    pfix = jnp.where(i > 0, pltpu.roll(prev8, k, axis=0), 0.0)
    row8 = lax.broadcasted_iota(jnp.int32, pfix.shape, 0)
    top = jnp.where(row8 < k, pfix, rolled[:8])
    return top if x.shape[0] == 8 else jnp.concatenate([top, rolled[8:]], axis=0)


def _shift_up(x, next8, i, nsteps, k):
    tm = x.shape[0]
    rolled = pltpu.roll(x, tm - k, axis=0)
    nfix = jnp.where(i < nsteps - 1, pltpu.roll(next8, 8 - k, axis=0), 0.0)
    row8 = lax.broadcasted_iota(jnp.int32, nfix.shape, 0)
    bot = jnp.where(row8 >= 8 - k, nfix, rolled[tm - 8:])
    return jnp.concatenate([rolled[:tm - 8], bot], axis=0)


def _sum0(x):
    return jnp.sum(x, axis=0, keepdims=True)


def _rms(x, g):
    return x * lax.rsqrt(jnp.mean(x * x, axis=-1, keepdims=True) + NORM_EPS) * g


def _softplus(x):
    return jnp.maximum(x, 0.0) + jnp.log(1.0 + jnp.exp(-jnp.abs(x)))


def _gelu(x):
    return 0.5 * x * (1.0 + jnp.tanh(0.7978845608028654 * (x + 0.044715 * x * x * x)))


def _dot32(a, b):
    return jnp.dot(a, b, preferred_element_type=f32, precision=lax.Precision.HIGHEST)


def _seg_raw(x, E):
    hi = x.astype(bf16)
    r1 = x - hi.astype(f32)
    mid = r1.astype(bf16)
    lo = (r1 - mid.astype(f32)).astype(bf16)
    Eb = E.astype(bf16)
    dot = lambda t: jnp.dot(t, Eb, preferred_element_type=f32)
    return (dot(lo) + dot(mid)) + dot(hi)


@jax.custom_vjp
def _seg(x, E):
    return _seg_raw(x, E)


_seg.defvjp(lambda x, E: (_seg_raw(x, E), E), lambda E, g: (_seg_raw(g, E), jnp.zeros_like(E)))


def _prep(q, w0, a0, k_k, k_a, w2p, a2p, g2, E):
    r, k, v = q[:, 0:512], q[:, 512:1024], q[:, 1024:1536]
    wa, gd = q[:, 1536:1664], q[:, 1664:1792]
    wlog = -_softplus(-(w0 + _dot32(jnp.tanh(wa), w2p))) - 0.5
    lw = -jnp.exp(wlog)
    a = jax.nn.sigmoid(a0 + _dot32(wa, a2p))
    g = _dot32(jax.nn.sigmoid(gd), g2)
    kk = k * k_k
    kkn = kk / jnp.maximum(jnp.sqrt(_seg(kk * kk, E)), 1e-12)
    k2 = k * (1.0 + (a - 1.0) * k_a)
    return r, lw, k2, v, -kkn, kkn * a, g


def _rwkv_out(y, r, k2, v, g, lnx_w, lnx_b, r_k, E):
    mean = _seg(y, E) * (1.0 / HEAD)
    yc = y - mean
    var = _seg(yc * yc, E) * (1.0 / HEAD)
    yn = yc * lax.rsqrt(var + LNX_EPS) * lnx_w + lnx_b
    bonus = _seg(r * k2 * r_k, E) * v
    return (yn + bonus) * g


def _s5_mid(ysc, u, d):
    return _gelu(ysc + d * u)


def _s5_glu(yg, z2, b_glu):
    return yg * jax.nn.sigmoid(z2 + b_glu)


def _merge(gp, o_r, o_s, b_gate):
    gates = jax.nn.sigmoid(gp + b_gate)
    return gates[:, :D_MODEL] * o_r + gates[:, D_MODEL:] * o_s


def _act(zc):
    return _gelu(zc[:, :D_FF]) * zc[:, D_FF:]


def _s5_disc(a_re, a_im, ls, b_re, b_im):
    dt = jnp.exp(ls)
    er = jnp.exp(a_re * dt)
    ar, ai = er * jnp.cos(a_im * dt), er * jnp.sin(a_im * dt)
    x, y = ar - 1.0, ai
    den = a_re * a_re + a_im * a_im
    fr, fi = (x * a_re + y * a_im) / den, (y * a_re - x * a_im) / den
    return ar, ai, fr * b_re - fi * b_im, fr * b_im + fi * b_re


_DIMS = {'nn': ((1,), (0,)), 'nt': ((1,), (1,)), 'tn': ((0,), (0,))}


def _raw_bdot(a, b, mode):
    return lax.dot_general(a.astype(bf16), b.astype(bf16), (_DIMS[mode], ((), ())), preferred_element_type=f32)


@functools.partial(jax.custom_vjp, nondiff_argnums=(2,))
def _bdot(a, b, mode):
    return _raw_bdot(a, b, mode)


def _bdot_fwd(a, b, mode):
    return _raw_bdot(a, b, mode), (a, b)


def _bdot_bwd(mode, res, g):
    a, b = res
    if mode == 'nn':
        return _raw_bdot(g, b, 'nt'), _raw_bdot(a, g, 'tn')
    if mode == 'nt':
        return _raw_bdot(g, b, 'nn'), _raw_bdot(g, a, 'tn')
    return _raw_bdot(b, g, 'nt'), _raw_bdot(a, g, 'nn')


_bdot.defvjp(_bdot_fwd, _bdot_bwd)


def _wkv_chunk(S0, r, lw, k, v, a, b, tri, bd):
    C = r[0].shape[0]
    P = range(len(r))
    lane = lax.broadcasted_iota(jnp.int32, (1, 2 * HEAD), 1)
    m0, m1 = (lane < HEAD).astype(f32), (lane >= HEAD).astype(f32)
    cat = lambda *xs: jnp.concatenate(xs, axis=0)
    stack = lambda x: cat(x * m0, x * m1)
    unstack = lambda x2: m0 * x2[:C] + m1 * x2[C:]
    rid = lax.broadcasted_iota(jnp.int32, (2 * C, 2 * C), 0)
    cid = lax.broadcasted_iota(jnp.int32, (2 * C, 2 * C), 1)
    same = (rid < C) == (cid < C)
    eye2 = (rid == cid).astype(f32)
    tri2 = (same & (rid >= cid)).astype(f32)
    sl2 = tri2 - eye2
    cum = [_dot32(tri, lw[p]) for p in P]
    g = [jnp.exp(cum[p]) for p in P]
    gi = [jnp.exp(-cum[p]) for p in P]
    at = [a[p] * jnp.exp(cum[p] - lw[p]) for p in P]
    rt = [r[p] * g[p] for p in P]
    kb = [k[p] * gi[p] for p in P]
    bb = [b[p] * gi[p] for p in P]
    lhs = [cat(stack(at[p]), stack(rt[p])) for p in P]
    pb = [_bdot(lhs[p], stack(bb[p]), 'nt') for p in P]
    pk = [_bdot(lhs[p], stack(kb[p]), 'nt') for p in P]
    aab = [pb[p][:2 * C] * sl2 for p in P]
    base = [_bdot(cat(at[p], rt[p]), S0[p], 'nt') for p in P]
    t = [_bdot(cat(pk[p][:2 * C] * sl2, pk[p][2 * C:] * tri2), cat(v[p], v[p]), 'nn') for p in P]
    rhs = [cat(base[p][:C], base[p][:C]) + t[p][:2 * C] for p in P]
    x = [eye2 + aab[p] for p in P]
    pw = aab
    n = 1
    while 2 * n < C:
        pw = [_bdot(pw[p], pw[p], 'nn') for p in P]
        x = [x[p] + _bdot(x[p], pw[p], 'nn') for p in P]
        n *= 2
    u = [unstack(_bdot(x[p], rhs[p], 'nn')) for p in P]
    w2 = [_bdot(pb[p][2 * C:] * tri2, cat(u[p], u[p]), 'nn') for p in P]
    y = [base[p][C:] + unstack(t[p][2 * C:]) + unstack(w2[p]) for p in P]
    S1 = [g[p][C - 1:C, :] * (S0[p] + bd * _bdot(cat(v[p], u[p]), cat(kb[p], bb[p]), 'tn')) for p in P]
    return y, S1


def _pairs(x):
    return [x[:, 2 * HEAD * p:2 * HEAD * (p + 1)] for p in range(HEADS // 2)]


def _wkv_consts():
    tri = jnp.tril(jnp.ones((WKV_C, WKV_C), f32))
    hid = jnp.arange(2 * HEAD) // HEAD
    return tri, (hid[:, None] == hid[None, :]).astype(f32)


def _wkv7_fwd(r, lw, k, v, a, b):
    L = r.shape[0]
    nc, npair = L // WKV_C, HEADS // 2

    def body(r_ref, lw_ref, k_ref, v_ref, a_ref, b_ref, tri_ref, bd_ref, y_ref, ck_ref, s_ref):
        @pl.when(pl.program_id(0) == 0)
        def _():
            s_ref[...] = jnp.zeros_like(s_ref)

        s0 = [s_ref[p] for p in range(npair)]
        for p in range(npair):
            ck_ref[0, p] = s0[p]
        y, s1 = _wkv_chunk(s0, *(_pairs(x) for x in (r_ref, lw_ref, k_ref, v_ref, a_ref, b_ref)), tri_ref[...], bd_ref[...])
        for p in range(npair):
            y_ref[:, 2 * HEAD * p:2 * HEAD * (p + 1)] = y[p]
            s_ref[p] = s1[p]

    row = pl.BlockSpec((WKV_C, RWKV_W), lambda c: (c, 0))
    sspec = pl.BlockSpec((1, npair, 2 * HEAD, 2 * HEAD), lambda c: (c, 0, 0, 0))
    return pl.pallas_call(
        body, name="wkv7_fwd", grid=(nc,),
        in_specs=[row] * 6 + [pl.BlockSpec((WKV_C, WKV_C), lambda c: (0, 0)), pl.BlockSpec((2 * HEAD, 2 * HEAD), lambda c: (0, 0))],
        out_specs=[row, sspec],
        out_shape=[jax.ShapeDtypeStruct((L, RWKV_W), f32), jax.ShapeDtypeStruct((nc, npair, 2 * HEAD, 2 * HEAD), f32)],
        scratch_shapes=[pltpu.VMEM((npair, 2 * HEAD, 2 * HEAD), f32)],
        compiler_params=pltpu.CompilerParams(dimension_semantics=("arbitrary",), vmem_limit_bytes=VMEM_LIMIT),
    )(r, lw, k, v, a, b, *_wkv_consts())


def _wkv7_bwd(r, lw, k, v, a, b, ck, dy):
    L = r.shape[0]
    nc, npair = L // WKV_C, HEADS // 2

    def body(r_ref, lw_ref, k_ref, v_ref, a_ref, b_ref, ck_ref, dy_ref, tri_ref, bd_ref,
             dr_ref, dlw_ref, dk_ref, dv_ref, da_ref, db_ref, ds_ref):
        @pl.when(pl.program_id(0) == 0)
        def _():
            ds_ref[...] = jnp.zeros_like(ds_ref)

        tri, bd = tri_ref[...], bd_ref[...]
        ins = [[ck_ref[0, p] for p in range(npair)]] + [_pairs(x) for x in (r_ref, lw_ref, k_ref, v_ref, a_ref, b_ref)]
        _, vjp = jax.vjp(lambda *t: _wkv_chunk(*t, tri, bd), *ins)
        gs = vjp((_pairs(dy_ref), [ds_ref[p] for p in range(npair)]))
        for p in range(npair):
            ds_ref[p] = gs[0][p]
            for ref, gval in zip((dr_ref, dlw_ref, dk_ref, dv_ref, da_ref, db_ref), gs[1:]):
                ref[:, 2 * HEAD * p:2 * HEAD * (p + 1)] = gval[p]

    row = pl.BlockSpec((WKV_C, RWKV_W), lambda c: (nc - 1 - c, 0))
    sspec = pl.BlockSpec((1, npair, 2 * HEAD, 2 * HEAD), lambda c: (nc - 1 - c, 0, 0, 0))
    return pl.pallas_call(
        body, name="wkv7_bwd", grid=(nc,),
        in_specs=[row] * 6 + [sspec, row, pl.BlockSpec((WKV_C, WKV_C), lambda c: (0, 0)),
                              pl.BlockSpec((2 * HEAD, 2 * HEAD), lambda c: (0, 0))],
        out_specs=[row] * 6,
        out_shape=[jax.ShapeDtypeStruct((L, RWKV_W), f32)] * 6,
        scratch_shapes=[pltpu.VMEM((npair, 2 * HEAD, 2 * HEAD), f32)],
        compiler_params=pltpu.CompilerParams(dimension_semantics=("arbitrary",), vmem_limit_bytes=VMEM_LIMIT),
    )(r, lw, k, v, a, b, ck, dy, *_wkv_consts())


def _cmul(ar, ai, xr, xi):
    return ar * xr - ai * xi, ar * xi + ai * xr


def _scan_init(a_ref, car_ref, pw_ref, reverse):
    car_ref[...] = jnp.zeros_like(car_ref)
    ar = jnp.broadcast_to(a_ref[:, :S5_N], (8, S5_N))
    ai = jnp.broadcast_to(a_ref[:, S5_N:], (8, S5_N))
    if reverse:
        ai = -ai
    row = lax.broadcasted_iota(jnp.int32, (8, S5_N), 0)
    pr, pi = ar, ai
    qr, qi = jnp.zeros((8, S5_N), f32), jnp.zeros((8, S5_N), f32)
    for e in range(1, 9):
        sel = (row == 8 - e) if reverse else (row == e - 1)
        qr, qi = jnp.where(sel, pr, qr), jnp.where(sel, pi, qi)
        if e in (1, 2, 4):
            j = (1, 2, 4).index(e)
            pw_ref[j, :, :S5_N] = pr
            pw_ref[j, :, S5_N:] = pi
        pr, pi = _cmul(pr, pi, ar, ai)
    pw_ref[3, :, :S5_N] = qr
    pw_ref[3, :, S5_N:] = qi


def _scan_tile(x_ref, o_ref, car_ref, pw_ref, reverse):
    ng = x_ref.shape[0] // 8
    row = lax.broadcasted_iota(jnp.int32, (8, S5_N), 0)

    def group(gi, carry):
        g = (ng - 1 - gi) if reverse else gi
        t0 = pl.multiple_of(g * 8, 8)
        xr, xi = x_ref[pl.ds(t0, 8), :S5_N], x_ref[pl.ds(t0, 8), S5_N:]
        for j, d in enumerate((1, 2, 4)):
            if reverse:
                sr = jnp.where(row < 8 - d, pltpu.roll(xr, 8 - d, axis=0), 0.0)
                si = jnp.where(row < 8 - d, pltpu.roll(xi, 8 - d, axis=0), 0.0)
            else:
                sr = jnp.where(row >= d, pltpu.roll(xr, d, axis=0), 0.0)
                si = jnp.where(row >= d, pltpu.roll(xi, d, axis=0), 0.0)
            mr, mi = _cmul(pw_ref[j, :, :S5_N], pw_ref[j, :, S5_N:], sr, si)
            xr, xi = xr + mr, xi + mi
        cr, ci = carry
        mr, mi = _cmul(pw_ref[3, :, :S5_N], pw_ref[3, :, S5_N:], cr, ci)
        xr, xi = xr + mr, xi + mi
        o_ref[pl.ds(t0, 8), :S5_N] = xr
        o_ref[pl.ds(t0, 8), S5_N:] = xi
        e = 0 if reverse else 7
        return (jnp.broadcast_to(xr[e:e + 1, :], (8, S5_N)), jnp.broadcast_to(xi[e:e + 1, :], (8, S5_N)))

    cr, ci = lax.fori_loop(0, ng, group, (car_ref[:, :S5_N], car_ref[:, S5_N:]))
    car_ref[:, :S5_N] = cr
    car_ref[:, S5_N:] = ci


_CB, _SB = 128, 512


def _cblk(k):
    return slice(_CB * k, _CB * (k + 1))


def _sblk(j):
    return slice(_SB * j, _SB * (j + 1))


def _s5_fwd(u, bmat, cmat, abar, late):
    L = u.shape[0]
    nt = L // S5_T
    names = list(late)
    nh = len(names)

    def body(u_ref, b_ref, c_ref, a_ref, *rest):
        h_in, (st_ref, y_ref), h_out = rest[:nh], rest[nh:nh + 2], rest[nh + 2:2 * nh + 2]
        bu_ref, car_ref, pw_ref, ssem, rsem, lsem = rest[2 * nh + 2:]
        i = pl.program_id(0)

        def copies():
            px, py, pc = _mesh_pos()
            me = 2 * px + py
            out = []
            for a, nm in enumerate(names):
                hr = late[nm].shape[0] // 2
                src, dst = h_in[a].at[pl.ds(pl.multiple_of(pc * hr, 16), hr), :], _slab(h_out[a], nm, me, pc)
                out.append(pltpu.make_async_copy(src, dst, lsem.at[a]))
                out += [pltpu.make_async_remote_copy(src, dst, ssem.at[3 * a + k], rsem.at[3 * a + k],
                                                     device_id=(qx, qy, pc), device_id_type=MESH)
                        for k, (qx, qy) in enumerate(_chip_peers(px, py))]
            return out

        @pl.when(i == 0)
        def _():
            _scan_init(a_ref, car_ref, pw_ref, False)
            for cp in copies():
                cp.start()

        for j in range(8):
            bu_ref[:, _sblk(j)] = _raw_bdot(u_ref[:, _cblk(j % 4)], b_ref[_cblk(j % 4), _sblk(j)], 'nn')
        _scan_tile(bu_ref, st_ref, car_ref, pw_ref, False)
        for k in range(4):
            y_ref[:, _cblk(k)] = (_raw_bdot(st_ref[:, _sblk(k)], c_ref[_sblk(k), _cblk(k)], 'nn')
                                  + _raw_bdot(st_ref[:, _sblk(4 + k)], c_ref[_sblk(4 + k), _cblk(k)], 'nn'))

        @pl.when(i == nt - 1)
        def _():
            for cp in copies():
                cp.wait()

    whole = lambda shape: pl.BlockSpec(shape, lambda i: (0, 0))
    outs = pl.pallas_call(
        body, name="s5_fwd", grid=(nt,),
        in_specs=[pl.BlockSpec((S5_T, S5_W), lambda i: (i, 0)), whole(bmat.shape), whole(cmat.shape), whole(abar.shape)]
        + [ANY] * nh,
        out_specs=[pl.BlockSpec((S5_T, 2 * S5_N), lambda i: (i, 0)), pl.BlockSpec((S5_T, S5_W), lambda i: (i, 0))] + [ANY] * nh,
        out_shape=[jax.ShapeDtypeStruct((L, 2 * S5_N), f32), jax.ShapeDtypeStruct((L, S5_W), f32)]
        + [jax.ShapeDtypeStruct(GATHER[nm][0], late[nm].dtype) for nm in names],
        scratch_shapes=[pltpu.VMEM((S5_T, 2 * S5_N), f32), pltpu.VMEM((8, 2 * S5_N), f32), pltpu.VMEM((4, 8, 2 * S5_N), f32),
                        pltpu.SemaphoreType.DMA((3 * nh,)), pltpu.SemaphoreType.DMA((3 * nh,)), pltpu.SemaphoreType.DMA((nh,))],
        compiler_params=pltpu.CompilerParams(dimension_semantics=("arbitrary",), vmem_limit_bytes=VMEM_LIMIT),
    )(u, bmat, cmat, abar, *[late[nm] for nm in names])
    return outs[0], outs[1], dict(zip(names, outs[2:]))


def _s5_bwd(dy, st, u, du_direct, bmat, cmat, abar, chip_sum):
    L = u.shape[0]
    nt = L // S5_T
    nb8 = S5_T // 8
    names = list(chip_sum)
    nh = len(names)

    def body(dy_ref, st_ref, sp_ref, u_ref, dud_ref, b_ref, c_ref, a_ref, *rest):
        x_in, (du_ref, db_ref, dc_ref, da_ref), x_out = rest[:nh], rest[nh:nh + 4], rest[nh + 4:2 * nh + 4]
        lam_ref, car_ref, pw_ref, ssem, rsem = rest[2 * nh + 4:]
        i = pl.program_id(0)

        @pl.when(i == 0)
        def _():
            _scan_init(a_ref, car_ref, pw_ref, True)
            db_ref[...] = jnp.zeros_like(db_ref)
            dc_ref[...] = jnp.zeros_like(dc_ref)
            da_ref[...] = jnp.zeros_like(da_ref)
            for cp in _exchange_copies(x_in, x_out, ssem, rsem):
                cp.start()

        for j in range(8):
            lam_ref[:, _sblk(j)] = _raw_bdot(dy_ref[:, _cblk(j % 4)], c_ref[_sblk(j), _cblk(j % 4)], 'nt')
        _scan_tile(lam_ref, lam_ref, car_ref, pw_ref, True)
        for k in range(4):
            du_ref[:, _cblk(k)] = (dud_ref[:, _cblk(k)] + _raw_bdot(lam_ref[:, _sblk(k)], b_ref[_cblk(k), _sblk(k)], 'nt')
                                   + _raw_bdot(lam_ref[:, _sblk(4 + k)], b_ref[_cblk(k), _sblk(4 + k)], 'nt')
                                   ).astype(du_ref.dtype)
            sr = _shift_down(st_ref[:, _sblk(k)], sp_ref[:, _sblk(k)], nt - 1 - i, 1)
            si = _shift_down(st_ref[:, _sblk(4 + k)], sp_ref[:, _sblk(4 + k)], nt - 1 - i, 1)
            lr, li = lam_ref[:, _sblk(k)], lam_ref[:, _sblk(4 + k)]
            da_ref[:, _sblk(k)] += _sum0(lr * sr + li * si)
            da_ref[:, _sblk(4 + k)] += _sum0(li * sr - lr * si)
        for j in range(8):
            db_ref[j] += _raw_bdot(u_ref[:, _cblk(j % 4)], lam_ref[:, _sblk(j)], 'tn')
            dc_ref[j] += _raw_bdot(st_ref[:, _sblk(j)], dy_ref[:, _cblk(j % 4)], 'tn')

        @pl.when(i == nt - 1)
        def _():
            for cp in _exchange_copies(x_in, x_out, ssem, rsem):
                cp.wait()

    whole = lambda shape: pl.BlockSpec(shape, lambda i: (0,) * len(shape))
    rev = lambda i: (nt - 1 - i, 0)
    outs = pl.pallas_call(
        body, name="s5_bwd", grid=(nt,),
        in_specs=[pl.BlockSpec((S5_T, S5_W), rev), pl.BlockSpec((S5_T, 2 * S5_N), rev),
                  pl.BlockSpec((8, 2 * S5_N), lambda i: (jnp.maximum((nt - 1 - i) * nb8 - 1, 0), 0)),
                  pl.BlockSpec((S5_T, S5_W), rev), pl.BlockSpec((S5_T, S5_W), rev), whole(bmat.shape), whole(cmat.shape),
                  whole(abar.shape)] + [ANY] * nh,
        out_specs=[pl.BlockSpec((S5_T, S5_W), rev), whole((8, _CB, _SB)), whole((8, _SB, _CB)), whole((1, 2 * S5_N))]
        + [ANY] * nh,
        out_shape=[jax.ShapeDtypeStruct((L, S5_W), bf16), jax.ShapeDtypeStruct((8, _CB, _SB), f32),
                   jax.ShapeDtypeStruct((8, _SB, _CB), f32), jax.ShapeDtypeStruct((1, 2 * S5_N), f32)]
        + [jax.ShapeDtypeStruct(chip_sum[nm].shape, chip_sum[nm].dtype) for nm in names],
        scratch_shapes=[pltpu.VMEM((S5_T, 2 * S5_N), f32), pltpu.VMEM((8, 2 * S5_N), f32), pltpu.VMEM((4, 8, 2 * S5_N), f32),
                        pltpu.SemaphoreType.DMA((3 * nh,)), pltpu.SemaphoreType.DMA((3 * nh,))],
        compiler_params=pltpu.CompilerParams(dimension_semantics=("arbitrary",), vmem_limit_bytes=VMEM_LIMIT),
    )(dy, st, st, u, du_direct, bmat, cmat, abar, *[chip_sum[nm] for nm in names])
    return outs[0], outs[1], outs[2], outs[3], dict(zip(names, outs[4:]))


def _s5_disc_fwd(a_re, a_im, ls, b_re, b_im):
    def body(a_re_ref, a_im_ref, ls_ref, b_re_ref, b_im_ref, ar_ref, ai_ref, br_ref, bi_ref):
        outs = _s5_disc(a_re_ref[...], a_im_ref[...], ls_ref[...], b_re_ref[...], b_im_ref[...])
        for ref, v in zip((ar_ref, ai_ref, br_ref, bi_ref), outs):
            ref[...] = v

    c1, c16 = jax.ShapeDtypeStruct((S5_N, 1), f32), jax.ShapeDtypeStruct((S5_N, S5_C), f32)
    return pl.pallas_call(body, name="s5_disc", out_shape=[c1, c1, c16, c16])(a_re, a_im, ls, b_re, b_im)


def _s5_disc_bwd(a_re, a_im, ls, b_re, b_im, d_ar, d_ai, d_br, d_bi, seg):
    def body(a_re_ref, a_im_ref, ls_ref, b_re_ref, b_im_ref, g1, g2, g3, g4, seg_ref, o1, o2, o3, o4, o5):
        _, vjp = jax.vjp(_s5_disc, a_re_ref[...], a_im_ref[...], ls_ref[...], b_re_ref[...], b_im_ref[...])
        da_re, da_im, dls, db_re, db_im = vjp((g1[...], g2[...], g3[...], g4[...]))
        o1[...] = da_re
        o2[...] = da_im
        o3[...] = _dot32(seg_ref[...], dls)
        o4[...] = db_re
        o5[...] = db_im

    c1, c16 = jax.ShapeDtypeStruct((S5_N, 1), f32), jax.ShapeDtypeStruct((S5_N, S5_C), f32)
    return pl.pallas_call(body, name="s5_disc_bwd", out_shape=[c1, c1, jax.ShapeDtypeStruct((S5_G, 1), f32), c16, c16])(
        a_re, a_im, ls, b_re, b_im, d_ar, d_ai, d_br, d_bi, seg)


ANY = pl.BlockSpec(memory_space=pl.ANY)

GATHER = {'w_in': ((4352, 1024), 0), 'ffn_w_up': ((1024, 5632), 1), 'w_branch_rwkv': ((512, 1024), 1),
          'w_branch_s5': ((512, 1024), 1), 'w_out': ((1024, 1024), 0), 's5_w_glu': ((512, 512), 0),
          'ffn_w_down': ((2816, 1024), 0), 'rwkv_w2': ((64, 512), 1), 'rwkv_a2': ((64, 512), 1),
          'rwkv_g2': ((128, 512), 1), 'ffn_conv_w': ((8, 5632), 1)}
BIG = ['w_in', 'ffn_w_up', 'w_branch_rwkv', 'w_branch_s5', 'w_out', 's5_w_glu', 'ffn_w_down']
TINY = ['rwkv_w2', 'rwkv_a2', 'rwkv_g2', 'ffn_conv_w']
SMALL = [n for n in WEIGHTS if n not in GATHER]
SMALL_ROWS = 320
ADAM_ROWS = 256


def _mo(v, m):
    return v if isinstance(v, int) else pl.multiple_of(v, m)


def _slab(ref, name, j, h=None):
    (R, Cn), axis = GATHER[name]
    if axis == 0:
        rs = R // 4
        if h is None:
            return ref.at[pl.ds(_mo(j * rs, 16), rs), :]
        return ref.at[pl.ds(_mo(j * rs + h * (rs // 2), 8), rs // 2), :]
    cols = pl.ds(_mo(j * (Cn // 4), 128), Cn // 4)
    if h is None:
        return ref.at[:, cols]
    return ref.at[pl.ds(_mo(h * (R // 2), 8), R // 2), cols]


def _half_shape(name):
    (R, Cn), axis = GATHER[name]
    return (R // 8, Cn) if axis == 0 else (R // 2, Cn // 4)


def _chip_peers(px, py):
    return [((1 - px) if (k >> 1) else px, (1 - py) if (k & 1) else py) for k in (1, 2, 3)]


def _run_copies(copies):
    for cp in copies:
        cp.start()
    for cp in copies:
        cp.wait()


def _gather_weights(blocks):
    names = list(blocks)
    n = len(names)

    def body(*refs):
        ins, outs = refs[:n], refs[n:2 * n]
        ssem, rsem, lsem = refs[2 * n:]
        px, py, pc = _mesh_pos()
        me = 2 * px + py
        copies = []
        for i, nm in enumerate(names):
            if nm in BIG:
                hr = blocks[nm].shape[0] // 2
                src, dst = ins[i].at[pl.ds(pl.multiple_of(pc * hr, 16), hr), :], _slab(outs[i], nm, me, pc)
            else:
                src, dst = ins[i], _slab(outs[i], nm, me)
            copies.append(pltpu.make_async_copy(src, dst, lsem.at[i]))
            for k, (qx, qy) in enumerate(_chip_peers(px, py)):
                copies.append(pltpu.make_async_remote_copy(src, dst, ssem.at[3 * i + k], rsem.at[3 * i + k],
                                                           device_id=(qx, qy, pc), device_id_type=MESH))
        _run_copies(copies)

    outs = pl.pallas_call(
        body, name="gather_weights", in_specs=[ANY] * n, out_specs=[ANY] * n,
        out_shape=[jax.ShapeDtypeStruct(GATHER[nm][0], blocks[nm].dtype) for nm in names],
        scratch_shapes=[pltpu.SemaphoreType.DMA((3 * n,)), pltpu.SemaphoreType.DMA((3 * n,)), pltpu.SemaphoreType.DMA((n,))],
    )(*[blocks[nm] for nm in names])
    return dict(zip(names, outs))


def _gather_pair(full, names, call_name):
    n = len(names)

    def body(*refs):
        ins, outs = refs[:n], refs[n:2 * n]
        ssem, rsem = refs[2 * n:]
        px, py, pc = _mesh_pos()
        copies = []
        for i, nm in enumerate(names):
            for j in range(4):
                copies.append(pltpu.make_async_remote_copy(_slab(ins[i], nm, j, pc), _slab(outs[i], nm, j, pc),
                                                           ssem.at[4 * i + j], rsem.at[4 * i + j],
                                                           device_id=(px, py, 1 - pc), device_id_type=MESH))
        _run_copies(copies)

    outs = pl.pallas_call(
        body, name=call_name, in_specs=[ANY] * n, out_specs=[ANY] * n,
        out_shape=[jax.ShapeDtypeStruct(full[nm].shape, full[nm].dtype) for nm in names],
        input_output_aliases={i: i for i in range(n)},
        scratch_shapes=[pltpu.SemaphoreType.DMA((4 * n,)), pltpu.SemaphoreType.DMA((4 * n,))],
    )(*[full[nm] for nm in names])
    return dict(zip(names, outs))


def _grads_to_sibling(G, names, call_name, small=None):
    n = len(names)
    ns = 0 if small is None else 1

    def body(*refs):
        g_refs, o_refs = refs[:n + ns], refs[n + ns:2 * (n + ns)]
        ssem, rsem = refs[2 * (n + ns):]
        px, py, pc = _mesh_pos()
        sib = (px, py, 1 - pc)
        copies = []
        for i, nm in enumerate(names):
            for j in range(4):
                copies.append(pltpu.make_async_remote_copy(_slab(g_refs[i], nm, j, 1 - pc), o_refs[i].at[j],
                                                           ssem.at[4 * i + j], rsem.at[4 * i + j],
                                                           device_id=sib, device_id_type=MESH))
        if ns:
            copies.append(pltpu.make_async_remote_copy(g_refs[n], o_refs[n], ssem.at[4 * n], rsem.at[4 * n],
                                                       device_id=sib, device_id_type=MESH))
        _run_copies(copies)

    outs = pl.pallas_call(
        body, name=call_name, in_specs=[ANY] * (n + ns), out_specs=[ANY] * (n + ns),
        out_shape=[jax.ShapeDtypeStruct((4,) + _half_shape(nm), f32) for nm in names]
        + [jax.ShapeDtypeStruct((SMALL_ROWS, PACK_W), f32)] * ns,
        scratch_shapes=[pltpu.SemaphoreType.DMA((4 * n + ns,)), pltpu.SemaphoreType.DMA((4 * n + ns,))],
    )(*[G[nm] for nm in names], *([small] * ns))
    return dict(zip(names, outs[:n])), (outs[n] if ns else None)


def _pair_add(G, recv, names, call_name, small=None, small_recv=None):
    n = len(names)
    ns = 0 if small is None else 1
    cidx = lax.axis_index("c").astype(jnp.int32).reshape(1)

    def body(c_ref, *refs):
        ins, outs = refs[:2 * (n + ns)], refs[2 * (n + ns):]
        for i in range(n):
            outs[i][...] = (ins[i][...] + ins[n + ns + i][...]).astype(bf16)
        if ns:
            outs[n][...] = ins[n][...] + ins[2 * n + 1][...]

    g_specs, r_specs = [], []
    for nm in names:
        hr, hc = _half_shape(nm)
        if GATHER[nm][1] == 0:
            g_specs.append(pl.BlockSpec((hr // 2, hc), lambda j, i, c: ((2 * j + c[0]) * 2 + i, 0)))
        else:
            g_specs.append(pl.BlockSpec((hr // 2, hc), lambda j, i, c: (2 * c[0] + i, j)))
        r_specs.append(pl.BlockSpec((1, hr // 2, hc), lambda j, i, c: (j, i, 0)))
    sm = [pl.BlockSpec((SMALL_ROWS // 8, PACK_W), lambda j, i, c: (2 * j + i, 0))] * ns
    outs = pl.pallas_call(
        body, name=call_name,
        grid_spec=pltpu.PrefetchScalarGridSpec(num_scalar_prefetch=1, grid=(4, 2), in_specs=g_specs + sm + r_specs + sm,
                                               out_specs=r_specs + sm),
        out_shape=[jax.ShapeDtypeStruct((4,) + _half_shape(nm), bf16) for nm in names]
        + [jax.ShapeDtypeStruct((SMALL_ROWS, PACK_W), f32)] * ns,
        compiler_params=pltpu.CompilerParams(vmem_limit_bytes=VMEM_LIMIT),
    )(cidx, *[G[nm] for nm in names], *([small] * ns), *[recv[nm] for nm in names], *([small_recv] * ns))
    return dict(zip(names, outs[:n])), (outs[n] if ns else None)


def _exchange_copies(ins, outs, ssem, rsem):
    px, py, pc = _mesh_pos()
    me = 2 * px + py
    return [pltpu.make_async_remote_copy(ins[i].at[2 * qx + qy], outs[i].at[me], ssem.at[3 * i + k], rsem.at[3 * i + k],
                                         device_id=(qx, qy, pc), device_id_type=MESH)
            for i in range(len(ins)) for k, (qx, qy) in enumerate(_chip_peers(px, py))]


def _grads_chip_exchange(chip_sum, names, small):
    n = len(names)

    def body(*refs):
        ins, outs = refs[:n + 1], refs[n + 1:2 * n + 2]
        ssem, rsem, ssem_s, rsem_s, lsem = refs[2 * n + 2:]
        px, py, pc = _mesh_pos()
        me = 2 * px + py
        copies = [pltpu.make_async_copy(ins[n], outs[n].at[me], lsem)]
        copies += _exchange_copies(ins[:n], outs[:n], ssem, rsem)
        copies += [pltpu.make_async_remote_copy(ins[n], outs[n].at[me], ssem_s.at[k], rsem_s.at[k],
                                                device_id=(qx, qy, pc), device_id_type=MESH)
                   for k, (qx, qy) in enumerate(_chip_peers(px, py))]
        _run_copies(copies)

    outs = pl.pallas_call(
        body, name="grads_chip_exchange", in_specs=[ANY] * (n + 1), out_specs=[ANY] * (n + 1),
        out_shape=[jax.ShapeDtypeStruct(chip_sum[nm].shape, chip_sum[nm].dtype) for nm in names]
        + [jax.ShapeDtypeStruct((4,) + small.shape, f32)],
        scratch_shapes=[pltpu.SemaphoreType.DMA((3 * n,)), pltpu.SemaphoreType.DMA((3 * n,)),
                        pltpu.SemaphoreType.DMA((3,)), pltpu.SemaphoreType.DMA((3,)), pltpu.SemaphoreType.DMA],
    )(*[chip_sum[nm] for nm in names], small)
    return dict(zip(names, outs[:n])), outs[n]


def _sum_slots(slots, chip_sum, small4):
    n = len(BIG)
    me = (2 * lax.axis_index("x") + lax.axis_index("y")).astype(jnp.int32).reshape(1)

    def body(me_ref, *refs):
        for i in range(n):
            own = refs[5 * i + 4][0].astype(f32)
            term = [jnp.where(me_ref[0] == k, own, refs[5 * i + k][0].astype(f32)) for k in range(4)]
            refs[5 * n + 1 + i][...] = ((term[0] + term[1]) + term[2]) + term[3]
        x = refs[5 * n]
        refs[6 * n + 1][...] = ((x[0] + x[1]) + x[2]) + x[3]

    in_specs, args, specs_out, shapes = [], [], [], []
    for nm in BIG:
        hr, hc = _half_shape(nm)
        for k in range(4):
            in_specs.append(pl.BlockSpec((1, hr // 2, hc), lambda i, m, k=k: (jnp.where(m[0] == k, (k + 1) % 4, k), i, 0)))
        in_specs.append(pl.BlockSpec((1, hr // 2, hc), lambda i, m: (m[0], i, 0)))
        args += [slots[nm]] * 4 + [chip_sum[nm]]
        specs_out.append(pl.BlockSpec((hr // 2, hc), lambda i, m: (i, 0)))
        shapes.append(jax.ShapeDtypeStruct((hr, hc), f32))
    in_specs.append(pl.BlockSpec((4, SMALL_ROWS // 2, PACK_W), lambda i, m: (0, i, 0)))
    specs_out.append(pl.BlockSpec((SMALL_ROWS // 2, PACK_W), lambda i, m: (i, 0)))
    shapes.append(jax.ShapeDtypeStruct((SMALL_ROWS, PACK_W), f32))
    outs = pl.pallas_call(
        body, name="grads_chip_sum",
        grid_spec=pltpu.PrefetchScalarGridSpec(num_scalar_prefetch=1, grid=(2,), in_specs=in_specs, out_specs=specs_out),
        out_shape=shapes, compiler_params=pltpu.CompilerParams(vmem_limit_bytes=VMEM_LIMIT),
    )(me, *args, small4)
    return dict(zip(BIG, outs[:n])), outs[n]


def _halves_to_sibling(half):
    n = len(BIG)

    def body(*refs):
        ins, outs = refs[:n], refs[n:2 * n]
        ssem, rsem = refs[2 * n:]
        px, py, pc = _mesh_pos()
        _run_copies([pltpu.make_async_remote_copy(ins[i], outs[i], ssem.at[i], rsem.at[i],
                                                  device_id=(px, py, 1 - pc), device_id_type=MESH) for i in range(n)])

    outs = pl.pallas_call(
        body, name="grads_halves_to_sibling", in_specs=[ANY] * n, out_specs=[ANY] * n,
        out_shape=[jax.ShapeDtypeStruct(_half_shape(nm), f32) for nm in BIG],
        scratch_shapes=[pltpu.SemaphoreType.DMA((n,)), pltpu.SemaphoreType.DMA((n,))],
    )(*[half[nm] for nm in BIG])
    return dict(zip(BIG, outs))


def _join_halves(mine, other, pc):
    hr = mine.shape[0]
    return lax.dynamic_slice_in_dim(jnp.concatenate([other, mine, other], axis=0), (1 - pc) * hr, 2 * hr, axis=0)


def _flat_pad(v):
    v = v.reshape(-1)
    return jnp.pad(v, (0, _ceil_to(v.shape[0], PACK_W) - v.shape[0]))


def _pack_rows(parts, rows):
    flat = jnp.concatenate([_flat_pad(p) for p in parts])
    return jnp.pad(flat, (0, rows * PACK_W - flat.shape[0])).reshape(rows, PACK_W)


def _unpack_rows(buf, shapes):
    flat = buf.reshape(-1)
    out, off = [], 0
    for shp in shapes:
        n = 1
        for d in shp:
            n *= d
        out.append(flat[off:off + n].reshape(shp))
        off += _ceil_to(n, PACK_W)
    return out


def _adamw_math(w_, g_, m_, v_):
    m2 = ADAM_B1 * m_ + (1.0 - ADAM_B1) * g_
    v2 = ADAM_B2 * v_ + (1.0 - ADAM_B2) * (g_ * g_)
    m_hat = m2 / (1.0 - ADAM_B1 ** ADAM_STEP)
    v_hat = v2 / (1.0 - ADAM_B2 ** ADAM_STEP)
    return -ADAM_LR * (m_hat / (jnp.sqrt(v_hat) + ADAM_EPS) + ADAM_WD * w_), m2, v2


def _adamw(groups):
    ng = len(groups)

    def body(*refs):
        ins, outs = refs[:4 * ng], refs[4 * ng:]
        for i in range(ng):
            res = _adamw_math(*(r[...] for r in ins[4 * i:4 * i + 4]))
            for ref, val in zip(outs[3 * i:3 * i + 3], res):
                ref[...] = val

    in_specs, out_specs, out_shape = [], [], []
    for grp in groups:
        R, Cn = grp[0].shape
        spec = pl.BlockSpec((R // 8, Cn), lambda i: (i, 0))
        in_specs += [spec] * 4
        out_specs += [spec] * 3
        out_shape += [jax.ShapeDtypeStruct((R, Cn), f32)] * 3
    outs = pl.pallas_call(
        body, name="adamw", grid=(8,), in_specs=in_specs, out_specs=out_specs, out_shape=out_shape,
        compiler_params=pltpu.CompilerParams(vmem_limit_bytes=VMEM_LIMIT),
    )(*[a for grp in groups for a in grp])
    return [tuple(outs[3 * i:3 * i + 3]) for i in range(ng)]


def _forward_backward(x, tgt, W, S, late):
    L = x.shape[0]
    TM, TMW, TS = 256, 128, 512
    row = lambda c, dt=f32: (c, dt)
    hid = jnp.arange(RWKV_W) // HEAD
    E = (hid[:, None] == hid[None, :]).astype(f32)
    seg = (jnp.arange(S5_N)[None, :] // S5_P == jnp.arange(S5_G)[:, None]).astype(f32)

    w_in_t = W['w_in']
    w_p, w_u, w_g = w_in_t[:N_RWKV], w_in_t[N_RWKV:N_RWKV + S5_W], w_in_t[N_RWKV + S5_W:]
    zpad = jnp.zeros((64, RWKV_W), f32)
    w2p = jnp.concatenate([W['rwkv_w2'], zpad], axis=0)
    a2p = jnp.concatenate([zpad, W['rwkv_a2']], axis=0)
    g2 = W['rwkv_g2']
    prep_consts = [S['rwkv_shift_mu'], S['rwkv_w0'], S['rwkv_a0'], S['rwkv_k_k'], S['rwkv_k_a'], w2p, a2p, g2, E]
    out_consts = [S['rwkv_lnx_w'], S['rwkv_lnx_b'], S['rwkv_r_k'], E]
    cw, cb = W['ffn_conv_w'][:3], S['ffn_conv_b']

    a_re, a_im = S['s5_a_re'].reshape(S5_N, 1), S['s5_a_im'].reshape(S5_N, 1)
    ls = jnp.repeat(S['s5_log_step'].reshape(S5_G, 1), S5_P, axis=0)
    b_re, b_im = S['s5_b_re'].reshape(S5_N, S5_C), S['s5_b_im'].reshape(S5_N, S5_C)
    ar, ai, bbr, bbi = _s5_disc_fwd(a_re, a_im, ls, b_re, b_im)
    abar = jnp.concatenate([ar.reshape(1, S5_N), ai.reshape(1, S5_N)], axis=1)
    eye = jnp.eye(S5_G, dtype=f32)

    def bdiag_in(bb):
        t = bb.reshape(S5_G, S5_P, S5_C).transpose(0, 2, 1)
        return (t[:, :, None, :] * eye[:, None, :, None]).reshape(S5_W, S5_N)

    def bdiag_out(cc):
        t = cc.transpose(0, 2, 1)
        return (t[:, :, None, :] * eye[:, None, :, None]).reshape(S5_N, S5_W)

    eye8 = jnp.eye(8, dtype=f32)

    def undiag_in(blocks):
        t = blocks.reshape(4, 8, S5_C, 8, S5_P)
        t = jnp.sum(t * eye8[None, :, None, :, None], axis=3)
        return t.reshape(S5_G, S5_C, S5_P).transpose(0, 2, 1).reshape(S5_N, S5_C)

    def undiag_out(blocks):
        t = blocks.reshape(4, 8, S5_P, 8, S5_C)
        t = jnp.sum(t * eye8[None, :, None, :, None], axis=3)
        return t.reshape(S5_G, S5_P, S5_C).transpose(0, 2, 1)

    bmat = jnp.concatenate([bdiag_in(bbr), bdiag_in(bbi)], axis=1).astype(bf16)
    cmat = jnp.concatenate([bdiag_out(S['s5_c_re'].reshape(S5_G, S5_C, S5_P)),
                            -bdiag_out(S['s5_c_im'].reshape(S5_G, S5_C, S5_P))], axis=0).astype(bf16)

    g1, g2n, g3, g4 = S['norm_mix_pre'], S['norm_mix_post'], S['norm_ffn_pre'], S['norm_ffn_post']
    (h1,) = _rowcall("norm_pre", lambda i, n, R, P, X, C: ((_rms(R[0], C[0]),), ()), L, TS, [x], [g1],
                     out_rows=[row(D_MODEL, bf16)])
    p = _mm(h1, w_p, 'nt', "mm_p")
    u = _mm(h1, w_u, 'nt', "mm_u")
    gp = _mm(h1, w_g, 'nt', "mm_g")

    def prep_fn(i, n, R, P, X, C):
        q = R[0] + (_shift_down(R[0], P[0], i, 1) - R[0]) * C[0]
        return _prep(q, *C[1:]), ()

    r, lw, k2, v, an, bv, g = _rowcall("rwkv_prep", prep_fn, L, TM, [p], prep_consts,
                                       out_rows=[row(RWKV_W)] * 7, prev=[0])
    y, ck = _wkv7_fwd(r, lw, k2, v, an, bv)
    (o_a,) = _rowcall("rwkv_out", lambda i, n, R, P, X, C: ((_rwkv_out(*R, *C),), ()), L, TM, [y, r, k2, v, g],
                      out_consts, out_rows=[row(RWKV_W, bf16)])
    o_r = _mm(o_a, W['w_branch_rwkv'], 'nn', "mm_br")

    st, ysc, got = _s5_fwd(u, bmat, cmat, abar, late)
    W = {**W, **_gather_pair(got, list(got), "gather_weights_pair_late")}
    (yg,) = _rowcall("s5_mid", lambda i, n, R, P, X, C: ((_s5_mid(*R, *C),), ()), L, TS, [ysc, u], [S['s5_d']],
                     out_rows=[row(S5_W)])
    z2 = _mm(yg, W['s5_w_glu'], 'nn', "mm_glu")
    (o_b,) = _rowcall("s5_glu", lambda i, n, R, P, X, C: ((_s5_glu(*R, *C),), ()), L, TS, [yg, z2], [S['s5_b_glu']],
                      out_rows=[row(S5_W, bf16)])
    o_s = _mm(o_b, W['w_branch_s5'], 'nn', "mm_bs")

    (merged,) = _rowcall("merge", lambda i, n, R, P, X, C: ((_merge(*R, *C),), ()), L, TS, [gp, o_r, o_s],
                         [S['b_gate']], out_rows=[row(D_MODEL, bf16)])
    mixed = _mm(merged, W['w_out'], 'nn', "mm_out")

    def resid_fn(i, n, R, P, X, C):
        x1_ = R[0] + _rms(R[1], C[0])
        return (x1_, _rms(x1_, C[1])), ()

    x1, h2 = _rowcall("resid_norm", resid_fn, L, TS, [x, mixed], [g2n, g3], out_rows=[row(D_MODEL), row(D_MODEL, bf16)])

    z = _mm(h2, W['ffn_w_up'], 'nn', "mm_up")

    def conv(zt, zprev, i, cw_, cb_):
        z2s, z1s = _shift_down(zt, zprev, i, 2), _shift_down(zt, zprev, i, 1)
        return cb_ + cw_[0:1] * z2s + cw_[1:2] * z1s + cw_[2:3] * zt, z2s, z1s

    (act,) = _rowcall("conv_act", lambda i, n, R, P, X, C: ((_act(conv(R[0], P[0], i, C[0], C[1])[0]),), ()), L, TMW,
                      [z], [cw, cb], out_rows=[row(D_FF, bf16)], prev=[0])
    f = _mm(act, W['ffn_w_down'], 'nn', "mm_down")

    def final_fn(i, n, R, P, X, C):
        x1_, f_, t_ = R
        fn_, vjp = jax.vjp(_rms, f_, C[0])
        diff = x1_ + fn_ - t_
        loss = jnp.sum(diff * diff) * (0.5 / D_MODEL)
        dx2_ = diff * (1.0 / D_MODEL)
        df_, dg4_ = vjp(dx2_)
        return (df_, dx2_), (jnp.full((1, PACK_W), loss, f32), dg4_)

    df, dx2, loss, dg4 = _rowcall("loss_head", final_fn, L, TS, [x1, f, tgt], [g4],
                                  out_rows=[row(D_MODEL, bf16), row(D_MODEL)], out_accs=[(1, PACK_W), (1, D_MODEL)])
    G = {'norm_ffn_post': dg4}

    dact = _mm(df, W['ffn_w_down'], 'nt', "mm_down_dx")
    G['ffn_w_down'] = _mm(act, df, 'tn', "mm_down_dw")

    def conv_bwd_fn(i, n, R, P, X, C):
        z_, dact_ = R
        cw_, cb_ = C
        zc, z2s, z1s = conv(z_, P[0], i, cw_, cb_)
        _, vjp = jax.vjp(_act, zc)
        (dzc_,) = vjp(dact_)
        last8 = z_[z_.shape[0] - 8:]
        zcn = cb_ + cw_[0:1] * _shift_down(X[0], last8, 1, 2) + cw_[1:2] * _shift_down(X[0], last8, 1, 1) + cw_[2:3] * X[0]
        _, vjpn = jax.vjp(_act, zcn)
        (dzcn,) = vjpn(X[1])
        dz_ = (cw_[2:3] * dzc_ + cw_[1:2] * _shift_up(dzc_, dzcn, i, n, 1) + cw_[0:1] * _shift_up(dzc_, dzcn, i, n, 2))
        return (dz_,), (_sum0(dzc_), _sum0(dzc_ * z2s), _sum0(dzc_ * z1s), _sum0(dzc_ * z_))

    wide = (1, 2 * D_FF)
    dz, dcb, dcw0, dcw1, dcw2 = _rowcall("conv_act_bwd", conv_bwd_fn, L, TMW, [z, dact], [cw, cb],
                                         out_rows=[row(2 * D_FF, bf16)], out_accs=[wide] * 4, prev=[0], nxt=[0, 1])
    G['ffn_conv_b'] = dcb
    G['ffn_conv_w'] = jnp.concatenate([dcw0, dcw1, dcw2], axis=0)
    dh2 = _mm(dz, W['ffn_w_up'], 'nt', "mm_up_dx")
    G['ffn_w_up'] = _mm(h2, dz, 'tn', "mm_up_dw")

    def norm2_bwd_fn(i, n, R, P, X, C):
        x1_, mixed_, dx2_, dh2_ = R
        _, vjp3 = jax.vjp(_rms, x1_, C[1])
        dx1a, dg3_ = vjp3(dh2_)
        dx1_ = dx2_ + dx1a
        _, vjp2 = jax.vjp(_rms, mixed_, C[0])
        dmixed_, dg2_ = vjp2(dx1_)
        return (dx1_, dmixed_), (dg2_, dg3_)

    dx1, dmixed, dg2n, dg3 = _rowcall("norm_mid_bwd", norm2_bwd_fn, L, TS, [x1, mixed, dx2, dh2], [g2n, g3],
                                      out_rows=[row(D_MODEL), row(D_MODEL, bf16)], out_accs=[(1, D_MODEL)] * 2)
    G['norm_mix_post'], G['norm_ffn_pre'] = dg2n, dg3

    dmerged = _mm(dmixed, W['w_out'], 'nt', "mm_out_dx")
    G['w_out'] = _mm(merged, dmixed, 'tn', "mm_out_dw")

    def merge_bwd_fn(i, n, R, P, X, C):
        _, vjp = jax.vjp(_merge, R[0], R[1], R[2], C[0])
        dgp_, do_r_, do_s_, dbg_ = vjp(R[3])
        return (dgp_, do_r_, do_s_), (dbg_,)

    dgp, do_r, do_s, G['b_gate'] = _rowcall("merge_bwd", merge_bwd_fn, L, TS, [gp, o_r, o_s, dmerged], [S['b_gate']],
                                            out_rows=[row(2 * D_MODEL, bf16), row(D_MODEL, bf16), row(D_MODEL, bf16)],
                                            out_accs=[(1, 2 * D_MODEL)])
    do_a = _mm(do_r, W['w_branch_rwkv'], 'nt', "mm_br_dx")
    G['w_branch_rwkv'] = _mm(o_a, do_r, 'tn', "mm_br_dw")
    do_b = _mm(do_s, W['w_branch_s5'], 'nt', "mm_bs_dx")
    G['w_branch_s5'] = _mm(o_b, do_s, 'tn', "mm_bs_dw")

    def glu_bwd_fn(i, n, R, P, X, C):
        _, vjp = jax.vjp(_s5_glu, R[0], R[1], C[0])
        dyg1_, dz2_, dbg_ = vjp(R[2])
        return (dyg1_, dz2_), (dbg_,)

    dyg1, dz2, G['s5_b_glu'] = _rowcall("s5_glu_bwd", glu_bwd_fn, L, TS, [yg, z2, do_b], [S['s5_b_glu']],
                                        out_rows=[row(S5_W), row(S5_W, bf16)], out_accs=[(1, S5_W)])
    dyg2 = _mm(dz2, W['s5_w_glu'], 'nt', "mm_glu_dx")
    G['s5_w_glu'] = _mm(yg, dz2, 'tn', "mm_glu_dw")

    def mid_bwd_fn(i, n, R, P, X, C):
        _, vjp = jax.vjp(_s5_mid, R[0], R[1], C[0])
        dysc_, du_, dd_ = vjp(R[2] + R[3])
        return (dysc_, du_), (dd_,)

    dysc, du1, G['s5_d'] = _rowcall("s5_mid_bwd", mid_bwd_fn, L, TS, [ysc, u, dyg1, dyg2], [S['s5_d']],
                                    out_rows=[row(S5_W, bf16), row(S5_W)], out_accs=[(1, S5_W)])
    early = [n for n in BIG if n != 'w_in']
    recv_e, _ = _grads_to_sibling(G, early, "grads_to_sibling_early")
    chip_e, _ = _pair_add(G, recv_e, early, "grads_pair_sum_early")
    du, dbmat, dcmat, dabar, slots_e = _s5_bwd(dysc, st, u, du1, bmat, cmat, abar, chip_e)
    da_re, da_im, dls, db_re, db_im = _s5_disc_bwd(
        a_re, a_im, ls, b_re, b_im, dabar[:, :S5_N].reshape(S5_N, 1), dabar[:, S5_N:].reshape(S5_N, 1),
        undiag_in(dbmat[:4]), undiag_in(dbmat[4:]), seg)
    G['s5_a_re'], G['s5_a_im'], G['s5_log_step'] = da_re, da_im, dls
    G['s5_b_re'], G['s5_b_im'] = db_re, db_im
    G['s5_c_re'], G['s5_c_im'] = undiag_out(dcmat[:4]), -undiag_out(dcmat[4:])

    def out_bwd_fn(i, n, R, P, X, C):
        _, vjp = jax.vjp(_rwkv_out, *R[:5], *C)
        gs = vjp(R[5])
        return gs[:5], gs[5:8]

    dy, dr1, dk1, dv1, dg, dlw, dlb, drk = _rowcall("rwkv_out_bwd", out_bwd_fn, L, TM, [y, r, k2, v, g, do_a], out_consts,
                                                    out_rows=[row(RWKV_W)] * 5, out_accs=[(1, RWKV_W)] * 3)
    G['rwkv_lnx_w'], G['rwkv_lnx_b'], G['rwkv_r_k'] = dlw, dlb, drk
    dr2, dlwk, dk2b, dv2, dan, dbv = _wkv7_bwd(r, lw, k2, v, an, bv, ck, dy)

    def prep_bwd_fn(i, n, R, P, X, C):
        p_ = R[0]
        d1 = _shift_down(p_, P[0], i, 1) - p_
        q = p_ + d1 * C[0]
        _, vjp = jax.vjp(_prep, q, *C[1:])
        cots = (R[1] + R[2], R[3], R[4] + R[5], R[6] + R[7], R[8], R[9], R[10])
        gs = vjp(cots)
        return (gs[0],), (_sum0(gs[0] * d1),) + tuple(gs[1:8])

    small, lowr = (1, RWKV_W), (128, RWKV_W)
    dq, dmu, dw0, da0, dkk, dka, dw2p, da2p, dg2 = _rowcall(
        "rwkv_prep_bwd", prep_bwd_fn, L, TM, [p, dr1, dr2, dlwk, dk1, dk2b, dv1, dv2, dan, dbv, dg],
        prep_consts, out_rows=[row(N_RWKV)], out_accs=[(1, N_RWKV)] + [small] * 4 + [lowr] * 3, prev=[0])
    G['rwkv_shift_mu'], G['rwkv_w0'], G['rwkv_a0'], G['rwkv_k_k'], G['rwkv_k_a'] = dmu, dw0, da0, dkk, dka
    G['rwkv_w2'], G['rwkv_a2'], G['rwkv_g2'] = dw2p[:64], da2p[64:], dg2

    def shift_bwd_fn(i, n, R, P, X, C):
        dm = R[0] * C[0]
        return (R[0] - dm + _shift_up(dm, X[0] * C[0], i, n, 1),), ()

    (dp,) = _rowcall("shift_bwd", shift_bwd_fn, L, TS, [dq], [S['rwkv_shift_mu']], out_rows=[row(N_RWKV, bf16)], nxt=[0])

    dproj = jnp.concatenate([dp, du, dgp], axis=1)
    dh1 = _mm(dproj, w_in_t, 'nn', "mm_in_dx")
    G['w_in'] = _mm(dproj, h1, 'tn', "mm_in_dw")

    def norm1_bwd_fn(i, n, R, P, X, C):
        _, vjp = jax.vjp(_rms, R[0], C[0])
        dxa, dg1_ = vjp(R[2])
        return (R[1] + dxa,), (dg1_,)

    dx, G['norm_mix_pre'] = _rowcall("norm_pre_bwd", norm1_bwd_fn, L, TS, [x, dx1, dh1], [g1],
                                     out_rows=[row(D_MODEL)], out_accs=[(1, D_MODEL)])
    return loss, dx, G, chip_e, slots_e


def kernel(x, norm_mix_pre, norm_mix_post, norm_ffn_pre, norm_ffn_post, w_in, b_gate, rwkv_shift_mu, rwkv_w0, rwkv_w2, rwkv_a0, rwkv_a2, rwkv_g2, rwkv_k_k, rwkv_k_a, rwkv_r_k, rwkv_lnx_w, rwkv_lnx_b, s5_a_re, s5_a_im, s5_b_re, s5_b_im, s5_c_re, s5_c_im, s5_d, s5_log_step, s5_w_glu, s5_b_glu, w_branch_rwkv, w_branch_s5, w_out, ffn_w_up, ffn_conv_w, ffn_conv_b, ffn_w_down, loss_target, m_norm_mix_pre, m_norm_mix_post, m_norm_ffn_pre, m_norm_ffn_post, m_w_in, m_b_gate, m_rwkv_shift_mu, m_rwkv_w0, m_rwkv_w2, m_rwkv_a0, m_rwkv_a2, m_rwkv_g2, m_rwkv_k_k, m_rwkv_k_a, m_rwkv_r_k, m_rwkv_lnx_w, m_rwkv_lnx_b, m_s5_a_re, m_s5_a_im, m_s5_b_re, m_s5_b_im, m_s5_c_re, m_s5_c_im, m_s5_d, m_s5_log_step, m_s5_w_glu, m_s5_b_glu, m_w_branch_rwkv, m_w_branch_s5, m_w_out, m_ffn_w_up, m_ffn_conv_w, m_ffn_conv_b, m_ffn_w_down, v_norm_mix_pre, v_norm_mix_post, v_norm_ffn_pre, v_norm_ffn_post, v_w_in, v_b_gate, v_rwkv_shift_mu, v_rwkv_w0, v_rwkv_w2, v_rwkv_a0, v_rwkv_a2, v_rwkv_g2, v_rwkv_k_k, v_rwkv_k_a, v_rwkv_r_k, v_rwkv_lnx_w, v_rwkv_lnx_b, v_s5_a_re, v_s5_a_im, v_s5_b_re, v_s5_b_im, v_s5_c_re, v_s5_c_im, v_s5_d, v_s5_log_step, v_s5_w_glu, v_s5_b_glu, v_w_branch_rwkv, v_w_branch_s5, v_w_out, v_ffn_w_up, v_ffn_conv_w, v_ffn_conv_b, v_ffn_w_down):
    A = dict(locals())
    me = 2 * lax.axis_index("x") + lax.axis_index("y")
    blk = lambda n: A[n][0]

    mine = {n: (blk(n).T if n == 'w_in' else blk(n)).astype(bf16) for n in BIG}
    mine.update({n: blk(n) for n in TINY})
    mine['ffn_conv_w'] = jnp.pad(blk('ffn_conv_w'), ((0, 5), (0, 0)))
    late = ['ffn_w_up', 'ffn_w_down']
    W = _gather_weights({n: blkv for n, blkv in mine.items() if n not in late})
    W.update(_gather_pair(W, [n for n in BIG if n not in late], "gather_weights_pair"))
    S = {n: A[n].reshape(1, -1) for n in SMALL}

    loss, dx, G, chip_e, slots_e = _forward_backward(x[0], loss_target[0], W, S, {n: mine[n] for n in late})

    tiny_shapes = [G[n].shape for n in TINY]
    small_buf = _pack_rows([G[n] for n in SMALL] + [G[n] for n in TINY] + [loss], SMALL_ROWS)
    recv, small_recv = _grads_to_sibling(G, ['w_in'], "grads_to_sibling", small_buf)
    chip_l, small_sum = _pair_add(G, recv, ['w_in'], "grads_pair_sum", small_buf, small_recv)
    slots_l, small4 = _grads_chip_exchange(chip_l, ['w_in'], small_sum)
    half, small_tot = _sum_slots({**slots_e, **slots_l}, {**chip_e, **chip_l}, small4)
    other = _halves_to_sibling(half)
    pc = lax.axis_index("c")
    grad = {n: _join_halves(half[n], other[n], pc) for n in BIG}
    grad['w_in'] = grad['w_in'].T
    vals = _unpack_rows(small_tot, [A[n].shape for n in SMALL] + tiny_shapes + [(1, PACK_W)])
    grad.update(zip(SMALL, vals))
    for n, full in zip(TINY, vals[len(SMALL):]):
        cs = A[n].shape[2]
        grad[n] = lax.dynamic_slice_in_dim(full, me * cs, cs, axis=1)
    loss_out = vals[-1][0, 0]

    packed = SMALL + TINY
    groups = [(blk(n), grad[n], blk('m_' + n), blk('v_' + n)) for n in BIG]
    groups.append(tuple(_pack_rows([src(n) for n in packed], ADAM_ROWS)
                        for src in (lambda n: A[n], lambda n: grad[n], lambda n: A['m_' + n], lambda n: A['v_' + n])))
    res = _adamw(groups)
    outs = [dict(), dict(), dict()]
    for n, r3 in zip(BIG, res[:-1]):
        for d, val in zip(outs, r3):
            d[n] = val
    for d, buf in zip(outs, res[-1]):
        d.update(zip(packed, _unpack_rows(buf, [A[n].shape for n in packed])))
    full = lambda d: [d[n].reshape(A[n].shape) for n in WEIGHTS]
    return (loss_out, dx[None], *full(grad), *full(outs[0]), *full(outs[1]), *full(outs[2]))
```

```python
import functools

import jax
import jax.numpy as jnp
from jax import lax
from jax.experimental import pallas as pl
from jax.experimental.pallas import tpu as pltpu

f32, bf16 = jnp.float32, jnp.bfloat16
MESH = pl.DeviceIdType.MESH

D_MODEL = 1024
RWKV_W = 512
HEADS, HEAD = 8, 64
N_RWKV = 1792
S5_W = 512
S5_G, S5_P, S5_C = 32, 64, 16
S5_N = S5_G * S5_P
D_FF = 2816
NORM_EPS = 1e-6
LNX_EPS = 64e-5
ADAM_LR, ADAM_B1, ADAM_B2, ADAM_EPS, ADAM_WD, ADAM_STEP = 0.001, 0.9, 0.999, 1e-08, 0.01, 10

VMEM_LIMIT = 48 * 1024 * 1024
PACK_W = 1024
WKV_C = 64
S5_T = 256

WEIGHTS = ['norm_mix_pre', 'norm_mix_post', 'norm_ffn_pre', 'norm_ffn_post', 'w_in', 'b_gate', 'rwkv_shift_mu',
           'rwkv_w0', 'rwkv_w2', 'rwkv_a0', 'rwkv_a2', 'rwkv_g2', 'rwkv_k_k', 'rwkv_k_a', 'rwkv_r_k', 'rwkv_lnx_w',
           'rwkv_lnx_b', 's5_a_re', 's5_a_im', 's5_b_re', 's5_b_im', 's5_c_re', 's5_c_im', 's5_d', 's5_log_step',
           's5_w_glu', 's5_b_glu', 'w_branch_rwkv', 'w_branch_s5', 'w_out', 'ffn_w_up', 'ffn_conv_w', 'ffn_conv_b',
           'ffn_w_down']


def _ceil_to(n, m):
    return -(-n // m) * m


def _mesh_pos():
    return lax.axis_index("x"), lax.axis_index("y"), lax.axis_index("c")


def _pick(d, cap=4096):
    for c in (1024, 1408, 2176, 896, 512, 256, 128):
        if c <= cap and d % c == 0:
            return c
    raise ValueError(d)


def _mm_resident(a, w, mode, name, M, N, K, out_dtype):
    budget = 40 * 1024 * 1024 - 2 * K * N
    tm = next(t for t in (512, 256, 128) if 2 * t * (K * a.dtype.itemsize + 4 * N) <= budget)
    dims = _DIMS[mode]

    def body(a_ref, w_ref, o_ref):
        o_ref[...] = lax.dot_general(a_ref[...].astype(bf16), w_ref[...], (dims, ((), ())),
                                     preferred_element_type=f32).astype(o_ref.dtype)

    return pl.pallas_call(
        body, name=name, grid=(M // tm,),
        in_specs=[pl.BlockSpec((tm, K), lambda i: (i, 0)),
                  pl.BlockSpec(w.shape, lambda i: (0, 0), pipeline_mode=pl.Buffered(1))],
        out_specs=pl.BlockSpec((tm, N), lambda i: (i, 0)), out_shape=jax.ShapeDtypeStruct((M, N), out_dtype),
        compiler_params=pltpu.CompilerParams(dimension_semantics=("parallel",), vmem_limit_bytes=VMEM_LIMIT),
    )(a, w)


def _mm(a, b, mode, name, out_dtype=f32):
    if mode == 'tn':
        (K, M), (K2, N) = a.shape, b.shape
    elif mode == 'nt':
        (M, K), (N, K2) = a.shape, b.shape
    else:
        (M, K), (K2, N) = a.shape, b.shape
    assert K == K2, (name, a.shape, b.shape)
    if mode != 'tn' and b.dtype == bf16:
        return _mm_resident(a, b, mode, name, M, N, K, out_dtype)
    if mode == 'tn':
        tm = _pick(M, 2176)
        tn = _pick(N, 512 if tm > 1408 else (1024 if tm > 1024 else 1408))
        tk = _pick(K, 512)
    else:
        tm, tn, tk = _pick(M, 512), _pick(N), _pick(K)
    nk = K // tk
    dims = {'nn': ((1,), (0,)), 'nt': ((1,), (1,)), 'tn': ((0,), (0,))}[mode]

    def body(a_ref, b_ref, o_ref, acc_ref):
        k = pl.program_id(2)

        @pl.when(k == 0)
        def _():
            acc_ref[...] = jnp.zeros_like(acc_ref)

        acc_ref[...] += lax.dot_general(a_ref[...].astype(bf16), b_ref[...].astype(bf16), (dims, ((), ())),
                                        preferred_element_type=f32)

        @pl.when(k == nk - 1)
        def _():
            o_ref[...] = acc_ref[...].astype(o_ref.dtype)

    a_spec = pl.BlockSpec((tk, tm), lambda i, j, k: (k, i)) if mode == 'tn' else pl.BlockSpec((tm, tk), lambda i, j, k: (i, k))
    b_spec = pl.BlockSpec((tn, tk), lambda i, j, k: (j, k)) if mode == 'nt' else pl.BlockSpec((tk, tn), lambda i, j, k: (k, j))
    return pl.pallas_call(
        body, name=name, grid=(M // tm, N // tn, nk),
        in_specs=[a_spec, b_spec], out_specs=pl.BlockSpec((tm, tn), lambda i, j, k: (i, j)),
        out_shape=jax.ShapeDtypeStruct((M, N), out_dtype),
        scratch_shapes=[pltpu.VMEM((tm, tn), f32)],
        compiler_params=pltpu.CompilerParams(dimension_semantics=("parallel", "parallel", "arbitrary"),
                                             vmem_limit_bytes=VMEM_LIMIT),
    )(a, b)


def _rowcall(name, fn, L, tm, rows, consts=(), out_rows=(), out_accs=(), prev=(), nxt=()):
    nsteps = L // tm
    nb8 = tm // 8
    last8 = L // 8 - 1
    n_r, n_p, n_x, n_c, n_or = len(rows), len(prev), len(nxt), len(consts), len(out_rows)

    def body(*refs):
        i = pl.program_id(0)
        vals = [r[...] for r in refs[:n_r + n_p + n_x + n_c]]
        R, P = vals[:n_r], vals[n_r:n_r + n_p]
        X, C = vals[n_r + n_p:n_r + n_p + n_x], vals[n_r + n_p + n_x:]
        o_refs = refs[n_r + n_p + n_x + n_c:]
        outs_r, outs_a = fn(i, nsteps, R, P, X, C)
        for ref, v in zip(o_refs[:n_or], outs_r, strict=True):
            ref[...] = v.astype(ref.dtype)
        if out_accs:
            @pl.when(i == 0)
            def _():
                for ref in o_refs[n_or:]:
                    ref[...] = jnp.zeros_like(ref)

            for ref, v in zip(o_refs[n_or:], outs_a, strict=True):
                ref[...] += v

    def const_spec(c):
        nd = c.ndim
        return pl.BlockSpec(c.shape, lambda i: (0,) * nd)

    in_specs = ([pl.BlockSpec((tm, a.shape[1]), lambda i: (i, 0)) for a in rows]
                + [pl.BlockSpec((8, rows[j].shape[1]), lambda i: (jnp.maximum(i * nb8 - 1, 0), 0)) for j in prev]
                + [pl.BlockSpec((8, rows[j].shape[1]), lambda i: (jnp.minimum((i + 1) * nb8, last8), 0)) for j in nxt]
                + [const_spec(c) for c in consts])
    out_specs = ([pl.BlockSpec((tm, c), lambda i: (i, 0)) for c, _ in out_rows]
                 + [pl.BlockSpec(s, lambda i: (0, 0)) for s in out_accs])
    out_shape = ([jax.ShapeDtypeStruct((L, c), dt) for c, dt in out_rows]
                 + [jax.ShapeDtypeStruct(s, f32) for s in out_accs])
    args = list(rows) + [rows[j] for j in prev] + [rows[j] for j in nxt] + list(consts)
    return pl.pallas_call(
        body, name=name, grid=(nsteps,), in_specs=in_specs, out_specs=out_specs, out_shape=out_shape,
        compiler_params=pltpu.CompilerParams(dimension_semantics=("arbitrary",), vmem_limit_bytes=VMEM_LIMIT),
    )(*args)


def _shift_down(x, prev8, i, k):
    rolled = pltpu.roll(x, k, axis=0)
    pfix = jnp.where(i > 0, pltpu.roll(prev8, k, axis=0), 0.0)
    row8 = lax.broadcasted_iota(jnp.int32, pfix.shape, 0)
    top = jnp.where(row8 < k, pfix, rolled[:8])
    return top if x.shape[0] == 8 else jnp.concatenate([top, rolled[8:]], axis=0)


def _shift_up(x, next8, i, nsteps, k):
    tm = x.shape[0]
    rolled = pltpu.roll(x, tm - k, axis=0)
    nfix = jnp.where(i < nsteps - 1, pltpu.roll(next8, 8 - k, axis=0), 0.0)
    row8 = lax.broadcasted_iota(jnp.int32, nfix.shape, 0)
    bot = jnp.where(row8 >= 8 - k, nfix, rolled[tm - 8:])
    return jnp.concatenate([rolled[:tm - 8], bot], axis=0)


def _sum0(x):
    return jnp.sum(x, axis=0, keepdims=True)


def _rms(x, g):
    return x * lax.rsqrt(jnp.mean(x * x, axis=-1, keepdims=True) + NORM_EPS) * g


def _softplus(x):
    return jnp.maximum(x, 0.0) + jnp.log(1.0 + jnp.exp(-jnp.abs(x)))


def _gelu(x):
    return 0.5 * x * (1.0 + jnp.tanh(0.7978845608028654 * (x + 0.044715 * x * x * x)))


def _dot32(a, b):
    return jnp.dot(a, b, preferred_element_type=f32, precision=lax.Precision.HIGHEST)


def _seg_raw(x, E):
    hi = x.astype(bf16)
    r1 = x - hi.astype(f32)
    mid = r1.astype(bf16)
    lo = (r1 - mid.astype(f32)).astype(bf16)
    Eb = E.astype(bf16)
    dot = lambda t: jnp.dot(t, Eb, preferred_element_type=f32)
    return (dot(lo) + dot(mid)) + dot(hi)


@jax.custom_vjp
def _seg(x, E):
    return _seg_raw(x, E)


_seg.defvjp(lambda x, E: (_seg_raw(x, E), E), lambda E, g: (_seg_raw(g, E), jnp.zeros_like(E)))


def _prep(q, w0, a0, k_k, k_a, w2p, a2p, g2, E):
    r, k, v = q[:, 0:512], q[:, 512:1024], q[:, 1024:1536]
    wa, gd = q[:, 1536:1664], q[:, 1664:1792]
    wlog = -_softplus(-(w0 + _bdot(jnp.tanh(wa), w2p, 'nn'))) - 0.5
    lw = -jnp.exp(wlog)
    a = jax.nn.sigmoid(a0 + _bdot(wa, a2p, 'nn'))
    g = _bdot(jax.nn.sigmoid(gd), g2, 'nn')
    kk = k * k_k
    kkn = kk / jnp.maximum(jnp.sqrt(_seg(kk * kk, E)), 1e-12)
    k2 = k * (1.0 + (a - 1.0) * k_a)
    return r, lw, k2, v, -kkn, kkn * a, g


def _rwkv_out(y, r, k2, v, g, lnx_w, lnx_b, r_k, E):
    mean = _seg(y, E) * (1.0 / HEAD)
    yc = y - mean
    var = _seg(yc * yc, E) * (1.0 / HEAD)
    yn = yc * lax.rsqrt(var + LNX_EPS) * lnx_w + lnx_b
    bonus = _seg(r * k2 * r_k, E) * v
    return (yn + bonus) * g


def _s5_mid(ysc, u, d):
    return _gelu(ysc + d * u)


def _s5_glu(yg, z2, b_glu):
    return yg * jax.nn.sigmoid(z2 + b_glu)


def _merge(gp, o_r, o_s, b_gate):
    gates = jax.nn.sigmoid(gp + b_gate)
    return gates[:, :D_MODEL] * o_r + gates[:, D_MODEL:] * o_s


def _act(zc):
    return _gelu(zc[:, :D_FF]) * zc[:, D_FF:]


def _s5_disc(a_re, a_im, ls, b_re, b_im):
    dt = jnp.exp(ls)
    er = jnp.exp(a_re * dt)
    ar, ai = er * jnp.cos(a_im * dt), er * jnp.sin(a_im * dt)
    x, y = ar - 1.0, ai
    den = a_re * a_re + a_im * a_im
    fr, fi = (x * a_re + y * a_im) / den, (y * a_re - x * a_im) / den
    return ar, ai, fr * b_re - fi * b_im, fr * b_im + fi * b_re


_DIMS = {'nn': ((1,), (0,)), 'nt': ((1,), (1,)), 'tn': ((0,), (0,))}


def _raw_bdot(a, b, mode):
    return lax.dot_general(a.astype(bf16), b.astype(bf16), (_DIMS[mode], ((), ())), preferred_element_type=f32)


@functools.partial(jax.custom_vjp, nondiff_argnums=(2,))
def _bdot(a, b, mode):
    return _raw_bdot(a, b, mode)


def _bdot_fwd(a, b, mode):
    return _raw_bdot(a, b, mode), (a, b)


def _bdot_bwd(mode, res, g):
    a, b = res
    if mode == 'nn':
        return _raw_bdot(g, b, 'nt'), _raw_bdot(a, g, 'tn')
    if mode == 'nt':
        return _raw_bdot(g, b, 'nn'), _raw_bdot(g, a, 'tn')
    return _raw_bdot(b, g, 'nt'), _raw_bdot(a, g, 'nn')


_bdot.defvjp(_bdot_fwd, _bdot_bwd)


def _wkv_chunk(S0, r, lw, k, v, a, b, tri, bd):
    C = r[0].shape[0]
    P = range(len(r))
    lane = lax.broadcasted_iota(jnp.int32, (1, 2 * HEAD), 1)
    m0, m1 = (lane < HEAD).astype(f32), (lane >= HEAD).astype(f32)
    cat = lambda *xs: jnp.concatenate(xs, axis=0)
    stack = lambda x: cat(x * m0, x * m1)
    unstack = lambda x2: m0 * x2[:C] + m1 * x2[C:]
    rid = lax.broadcasted_iota(jnp.int32, (2 * C, 2 * C), 0)
    cid = lax.broadcasted_iota(jnp.int32, (2 * C, 2 * C), 1)
    same = (rid < C) == (cid < C)
    eye2 = (rid == cid).astype(f32)
    tri2 = (same & (rid >= cid)).astype(f32)
    sl2 = tri2 - eye2
    cum = [_dot32(tri, lw[p]) for p in P]
    g = [jnp.exp(cum[p]) for p in P]
    gi = [jnp.exp(-cum[p]) for p in P]
    at = [a[p] * jnp.exp(cum[p] - lw[p]) for p in P]
    rt = [r[p] * g[p] for p in P]
    kb = [k[p] * gi[p] for p in P]
    bb = [b[p] * gi[p] for p in P]
    lhs = [cat(stack(at[p]), stack(rt[p])) for p in P]
    pb = [_bdot(lhs[p], stack(bb[p]), 'nt') for p in P]
    pk = [_bdot(lhs[p], stack(kb[p]), 'nt') for p in P]
    aab = [pb[p][:2 * C] * sl2 for p in P]
    base = [_bdot(cat(at[p], rt[p]), S0[p], 'nt') for p in P]
    t = [_bdot(cat(pk[p][:2 * C] * sl2, pk[p][2 * C:] * tri2), cat(v[p], v[p]), 'nn') for p in P]
    rhs = [cat(base[p][:C], base[p][:C]) + t[p][:2 * C] for p in P]
    x = [eye2 + aab[p] for p in P]
    pw = aab
    n = 1
    while 2 * n < C:
        pw = [_bdot(pw[p], pw[p], 'nn') for p in P]
        x = [x[p] + _bdot(x[p], pw[p], 'nn') for p in P]
        n *= 2
    u = [unstack(_bdot(x[p], rhs[p], 'nn')) for p in P]
    w2 = [_bdot(pb[p][2 * C:] * tri2, cat(u[p], u[p]), 'nn') for p in P]
    y = [base[p][C:] + unstack(t[p][2 * C:]) + unstack(w2[p]) for p in P]
    S1 = [g[p][C - 1:C, :] * (S0[p] + bd * _bdot(cat(v[p], u[p]), cat(kb[p], bb[p]), 'tn')) for p in P]
    return y, S1


def _pairs(x):
    return [x[:, 2 * HEAD * p:2 * HEAD * (p + 1)] for p in range(HEADS // 2)]


def _wkv_consts():
    tri = jnp.tril(jnp.ones((WKV_C, WKV_C), f32))
    hid = jnp.arange(2 * HEAD) // HEAD
    return tri, (hid[:, None] == hid[None, :]).astype(f32)


def _wkv7_fwd(r, lw, k, v, a, b):
    L = r.shape[0]
    nc, npair = L // WKV_C, HEADS // 2

    def body(r_ref, lw_ref, k_ref, v_ref, a_ref, b_ref, tri_ref, bd_ref, y_ref, ck_ref, s_ref):
        @pl.when(pl.program_id(0) == 0)
        def _():
            s_ref[...] = jnp.zeros_like(s_ref)

        s0 = [s_ref[p] for p in range(npair)]
        for p in range(npair):
            ck_ref[0, p] = s0[p]
        y, s1 = _wkv_chunk(s0, *(_pairs(x) for x in (r_ref, lw_ref, k_ref, v_ref, a_ref, b_ref)), tri_ref[...], bd_ref[...])
        for p in range(npair):
            y_ref[:, 2 * HEAD * p:2 * HEAD * (p + 1)] = y[p]
            s_ref[p] = s1[p]

    row = pl.BlockSpec((WKV_C, RWKV_W), lambda c: (c, 0))
    sspec = pl.BlockSpec((1, npair, 2 * HEAD, 2 * HEAD), lambda c: (c, 0, 0, 0))
    return pl.pallas_call(
        body, name="wkv7_fwd", grid=(nc,),
        in_specs=[row] * 6 + [pl.BlockSpec((WKV_C, WKV_C), lambda c: (0, 0)), pl.BlockSpec((2 * HEAD, 2 * HEAD), lambda c: (0, 0))],
        out_specs=[row, sspec],
        out_shape=[jax.ShapeDtypeStruct((L, RWKV_W), f32), jax.ShapeDtypeStruct((nc, npair, 2 * HEAD, 2 * HEAD), f32)],
        scratch_shapes=[pltpu.VMEM((npair, 2 * HEAD, 2 * HEAD), f32)],
        compiler_params=pltpu.CompilerParams(dimension_semantics=("arbitrary",), vmem_limit_bytes=VMEM_LIMIT),
    )(r, lw, k, v, a, b, *_wkv_consts())


def _wkv7_bwd(r, lw, k, v, a, b, ck, dy):
    L = r.shape[0]
    nc, npair = L // WKV_C, HEADS // 2

    def body(r_ref, lw_ref, k_ref, v_ref, a_ref, b_ref, ck_ref, dy_ref, tri_ref, bd_ref,
             dr_ref, dlw_ref, dk_ref, dv_ref, da_ref, db_ref, ds_ref):
        @pl.when(pl.program_id(0) == 0)
        def _():
            ds_ref[...] = jnp.zeros_like(ds_ref)

        tri, bd = tri_ref[...], bd_ref[...]
        ins = [[ck_ref[0, p] for p in range(npair)]] + [_pairs(x) for x in (r_ref, lw_ref, k_ref, v_ref, a_ref, b_ref)]
        _, vjp = jax.vjp(lambda *t: _wkv_chunk(*t, tri, bd), *ins)
        gs = vjp((_pairs(dy_ref), [ds_ref[p] for p in range(npair)]))
        for p in range(npair):
            ds_ref[p] = gs[0][p]
            for ref, gval in zip((dr_ref, dlw_ref, dk_ref, dv_ref, da_ref, db_ref), gs[1:]):
                ref[:, 2 * HEAD * p:2 * HEAD * (p + 1)] = gval[p]

    row = pl.BlockSpec((WKV_C, RWKV_W), lambda c: (nc - 1 - c, 0))
    sspec = pl.BlockSpec((1, npair, 2 * HEAD, 2 * HEAD), lambda c: (nc - 1 - c, 0, 0, 0))
    return pl.pallas_call(
        body, name="wkv7_bwd", grid=(nc,),
        in_specs=[row] * 6 + [sspec, row, pl.BlockSpec((WKV_C, WKV_C), lambda c: (0, 0)),
                              pl.BlockSpec((2 * HEAD, 2 * HEAD), lambda c: (0, 0))],
        out_specs=[row] * 6,
        out_shape=[jax.ShapeDtypeStruct((L, RWKV_W), f32)] * 6,
        scratch_shapes=[pltpu.VMEM((npair, 2 * HEAD, 2 * HEAD), f32)],
        compiler_params=pltpu.CompilerParams(dimension_semantics=("arbitrary",), vmem_limit_bytes=VMEM_LIMIT),
    )(r, lw, k, v, a, b, ck, dy, *_wkv_consts())


def _cmul(ar, ai, xr, xi):
    return ar * xr - ai * xi, ar * xi + ai * xr


def _scan_init(a_ref, car_ref, pw_ref, reverse):
    car_ref[...] = jnp.zeros_like(car_ref)
    ar = jnp.broadcast_to(a_ref[:, :S5_N], (8, S5_N))
    ai = jnp.broadcast_to(a_ref[:, S5_N:], (8, S5_N))
    if reverse:
        ai = -ai
    row = lax.broadcasted_iota(jnp.int32, (8, S5_N), 0)
    pr, pi = ar, ai
    qr, qi = jnp.zeros((8, S5_N), f32), jnp.zeros((8, S5_N), f32)
    for e in range(1, 9):
        sel = (row == 8 - e) if reverse else (row == e - 1)
        qr, qi = jnp.where(sel, pr, qr), jnp.where(sel, pi, qi)
        if e in (1, 2, 4):
            j = (1, 2, 4).index(e)
            pw_ref[j, :, :S5_N] = pr
            pw_ref[j, :, S5_N:] = pi
        pr, pi = _cmul(pr, pi, ar, ai)
    pw_ref[3, :, :S5_N] = qr
    pw_ref[3, :, S5_N:] = qi


def _scan_tile(x_ref, o_ref, car_ref, pw_ref, reverse):
    ng = x_ref.shape[0] // 8
    row = lax.broadcasted_iota(jnp.int32, (8, S5_N), 0)

    def group(gi, carry):
        g = (ng - 1 - gi) if reverse else gi
        t0 = pl.multiple_of(g * 8, 8)
        xr, xi = x_ref[pl.ds(t0, 8), :S5_N], x_ref[pl.ds(t0, 8), S5_N:]
        for j, d in enumerate((1, 2, 4)):
            if reverse:
                sr = jnp.where(row < 8 - d, pltpu.roll(xr, 8 - d, axis=0), 0.0)
                si = jnp.where(row < 8 - d, pltpu.roll(xi, 8 - d, axis=0), 0.0)
            else:
                sr = jnp.where(row >= d, pltpu.roll(xr, d, axis=0), 0.0)
                si = jnp.where(row >= d, pltpu.roll(xi, d, axis=0), 0.0)
            mr, mi = _cmul(pw_ref[j, :, :S5_N], pw_ref[j, :, S5_N:], sr, si)
            xr, xi = xr + mr, xi + mi
        cr, ci = carry
        mr, mi = _cmul(pw_ref[3, :, :S5_N], pw_ref[3, :, S5_N:], cr, ci)
        xr, xi = xr + mr, xi + mi
        o_ref[pl.ds(t0, 8), :S5_N] = xr
        o_ref[pl.ds(t0, 8), S5_N:] = xi
        e = 0 if reverse else 7
        return (jnp.broadcast_to(xr[e:e + 1, :], (8, S5_N)), jnp.broadcast_to(xi[e:e + 1, :], (8, S5_N)))

    cr, ci = lax.fori_loop(0, ng, group, (car_ref[:, :S5_N], car_ref[:, S5_N:]))
    car_ref[:, :S5_N] = cr
    car_ref[:, S5_N:] = ci


_CB, _SB = 128, 512


def _cblk(k):
    return slice(_CB * k, _CB * (k + 1))


def _sblk(j):
    return slice(_SB * j, _SB * (j + 1))


def _s5_fwd(u, bmat, cmat, abar, late):
    L = u.shape[0]
    nt = L // S5_T
    names = list(late)
    nh = len(names)

    def body(u_ref, b_ref, c_ref, a_ref, *rest):
        h_in, (st_ref, y_ref), h_out = rest[:nh], rest[nh:nh + 2], rest[nh + 2:2 * nh + 2]
        bu_ref, car_ref, pw_ref, ssem, rsem, lsem = rest[2 * nh + 2:]
        i = pl.program_id(0)

        def copies():
            px, py, pc = _mesh_pos()
            me = 2 * px + py
            out = []
            for a, nm in enumerate(names):
                hr = late[nm].shape[0] // 2
                src, dst = h_in[a].at[pl.ds(pl.multiple_of(pc * hr, 16), hr), :], _slab(h_out[a], nm, me, pc)
                out.append(pltpu.make_async_copy(src, dst, lsem.at[a]))
                out += [pltpu.make_async_remote_copy(src, dst, ssem.at[3 * a + k], rsem.at[3 * a + k],
                                                     device_id=(qx, qy, pc), device_id_type=MESH)
                        for k, (qx, qy) in enumerate(_chip_peers(px, py))]
            return out

        @pl.when(i == 0)
        def _():
            _scan_init(a_ref, car_ref, pw_ref, False)
            for cp in copies():
                cp.start()

        for j in range(8):
            bu_ref[:, _sblk(j)] = _raw_bdot(u_ref[:, _cblk(j % 4)], b_ref[j], 'nn')
        _scan_tile(bu_ref, st_ref, car_ref, pw_ref, False)
        for k in range(4):
            y_ref[:, _cblk(k)] = (_raw_bdot(st_ref[:, _sblk(k)], c_ref[k], 'nn')
                                  + _raw_bdot(st_ref[:, _sblk(4 + k)], c_ref[4 + k], 'nn'))

        @pl.when(i == nt - 1)
        def _():
            for cp in copies():
                cp.wait()

    whole = lambda shape: pl.BlockSpec(shape, lambda i: (0,) * len(shape))
    outs = pl.pallas_call(
        body, name="s5_fwd", grid=(nt,),
        in_specs=[pl.BlockSpec((S5_T, S5_W), lambda i: (i, 0)), whole(bmat.shape), whole(cmat.shape), whole(abar.shape)]
        + [ANY] * nh,
        out_specs=[pl.BlockSpec((S5_T, 2 * S5_N), lambda i: (i, 0)), pl.BlockSpec((S5_T, S5_W), lambda i: (i, 0))] + [ANY] * nh,
        out_shape=[jax.ShapeDtypeStruct((L, 2 * S5_N), f32), jax.ShapeDtypeStruct((L, S5_W), f32)]
        + [jax.ShapeDtypeStruct(GATHER[nm][0], late[nm].dtype) for nm in names],
        scratch_shapes=[pltpu.VMEM((S5_T, 2 * S5_N), f32), pltpu.VMEM((8, 2 * S5_N), f32), pltpu.VMEM((4, 8, 2 * S5_N), f32),
                        pltpu.SemaphoreType.DMA((3 * nh,)), pltpu.SemaphoreType.DMA((3 * nh,)), pltpu.SemaphoreType.DMA((nh,))],
        compiler_params=pltpu.CompilerParams(dimension_semantics=("arbitrary",), vmem_limit_bytes=VMEM_LIMIT),
    )(u, bmat, cmat, abar, *[late[nm] for nm in names])
    return outs[0], outs[1], dict(zip(names, outs[2:]))


def _s5_bwd(dy, st, u, du_direct, bmat, cmat, abar, chip_sum):
    L = u.shape[0]
    nt = L // S5_T
    nb8 = S5_T // 8
    names = list(chip_sum)
    nh = len(names)

    def body(dy_ref, st_ref, sp_ref, u_ref, dud_ref, b_ref, c_ref, a_ref, *rest):
        x_in, (du_ref, db_ref, dc_ref, da_ref), x_out = rest[:nh], rest[nh:nh + 4], rest[nh + 4:2 * nh + 4]
        lam_ref, car_ref, pw_ref, ssem, rsem = rest[2 * nh + 4:]
        i = pl.program_id(0)

        @pl.when(i == 0)
        def _():
            _scan_init(a_ref, car_ref, pw_ref, True)
            db_ref[...] = jnp.zeros_like(db_ref)
            dc_ref[...] = jnp.zeros_like(dc_ref)
            da_ref[...] = jnp.zeros_like(da_ref)
            for cp in _exchange_copies(x_in, x_out, ssem, rsem):
                cp.start()

        for j in range(8):
            lam_ref[:, _sblk(j)] = _raw_bdot(dy_ref[:, _cblk(j % 4)], c_ref[j], 'nt')
        _scan_tile(lam_ref, lam_ref, car_ref, pw_ref, True)
        for k in range(4):
            du_ref[:, _cblk(k)] = (dud_ref[:, _cblk(k)] + _raw_bdot(lam_ref[:, _sblk(k)], b_ref[k], 'nt')
                                   + _raw_bdot(lam_ref[:, _sblk(4 + k)], b_ref[4 + k], 'nt')
                                   ).astype(du_ref.dtype)
            sr = _shift_down(st_ref[:, _sblk(k)], sp_ref[:, _sblk(k)], nt - 1 - i, 1)
            si = _shift_down(st_ref[:, _sblk(4 + k)], sp_ref[:, _sblk(4 + k)], nt - 1 - i, 1)
            lr, li = lam_ref[:, _sblk(k)], lam_ref[:, _sblk(4 + k)]
            da_ref[:, _sblk(k)] += _sum0(lr * sr + li * si)
            da_ref[:, _sblk(4 + k)] += _sum0(li * sr - lr * si)
        for j in range(8):
            db_ref[j] += _raw_bdot(u_ref[:, _cblk(j % 4)], lam_ref[:, _sblk(j)], 'tn')
            dc_ref[j] += _raw_bdot(st_ref[:, _sblk(j)], dy_ref[:, _cblk(j % 4)], 'tn')

        @pl.when(i == nt - 1)
        def _():
            for cp in _exchange_copies(x_in, x_out, ssem, rsem):
                cp.wait()

    whole = lambda shape: pl.BlockSpec(shape, lambda i: (0,) * len(shape))
    rev = lambda i: (nt - 1 - i, 0)
    outs = pl.pallas_call(
        body, name="s5_bwd", grid=(nt,),
        in_specs=[pl.BlockSpec((S5_T, S5_W), rev), pl.BlockSpec((S5_T, 2 * S5_N), rev),
                  pl.BlockSpec((8, 2 * S5_N), lambda i: (jnp.maximum((nt - 1 - i) * nb8 - 1, 0), 0)),
                  pl.BlockSpec((S5_T, S5_W), rev), pl.BlockSpec((S5_T, S5_W), rev), whole(bmat.shape), whole(cmat.shape),
                  whole(abar.shape)] + [ANY] * nh,
        out_specs=[pl.BlockSpec((S5_T, S5_W), rev), whole((8, _CB, _SB)), whole((8, _SB, _CB)), whole((1, 2 * S5_N))]
        + [ANY] * nh,
        out_shape=[jax.ShapeDtypeStruct((L, S5_W), bf16), jax.ShapeDtypeStruct((8, _CB, _SB), f32),
                   jax.ShapeDtypeStruct((8, _SB, _CB), f32), jax.ShapeDtypeStruct((1, 2 * S5_N), f32)]
        + [jax.ShapeDtypeStruct(chip_sum[nm].shape, chip_sum[nm].dtype) for nm in names],
        scratch_shapes=[pltpu.VMEM((S5_T, 2 * S5_N), f32), pltpu.VMEM((8, 2 * S5_N), f32), pltpu.VMEM((4, 8, 2 * S5_N), f32),
                        pltpu.SemaphoreType.DMA((3 * nh,)), pltpu.SemaphoreType.DMA((3 * nh,))],
        compiler_params=pltpu.CompilerParams(dimension_semantics=("arbitrary",), vmem_limit_bytes=VMEM_LIMIT),
    )(dy, st, st, u, du_direct, bmat, cmat, abar, *[chip_sum[nm] for nm in names])
    return outs[0], outs[1], outs[2], outs[3], dict(zip(names, outs[4:]))


def _s5_disc_fwd(a_re, a_im, ls, b_re, b_im):
    def body(a_re_ref, a_im_ref, ls_ref, b_re_ref, b_im_ref, ar_ref, ai_ref, br_ref, bi_ref):
        outs = _s5_disc(a_re_ref[...], a_im_ref[...], ls_ref[...], b_re_ref[...], b_im_ref[...])
        for ref, v in zip((ar_ref, ai_ref, br_ref, bi_ref), outs):
            ref[...] = v

    c1, c16 = jax.ShapeDtypeStruct((S5_N, 1), f32), jax.ShapeDtypeStruct((S5_N, S5_C), f32)
    return pl.pallas_call(body, name="s5_disc", out_shape=[c1, c1, c16, c16])(a_re, a_im, ls, b_re, b_im)


def _s5_disc_bwd(a_re, a_im, ls, b_re, b_im, d_ar, d_ai, d_br, d_bi, seg):
    def body(a_re_ref, a_im_ref, ls_ref, b_re_ref, b_im_ref, g1, g2, g3, g4, seg_ref, o1, o2, o3, o4, o5):
        _, vjp = jax.vjp(_s5_disc, a_re_ref[...], a_im_ref[...], ls_ref[...], b_re_ref[...], b_im_ref[...])
        da_re, da_im, dls, db_re, db_im = vjp((g1[...], g2[...], g3[...], g4[...]))
        o1[...] = da_re
        o2[...] = da_im
        o3[...] = _dot32(seg_ref[...], dls)
        o4[...] = db_re
        o5[...] = db_im

    c1, c16 = jax.ShapeDtypeStruct((S5_N, 1), f32), jax.ShapeDtypeStruct((S5_N, S5_C), f32)
    return pl.pallas_call(body, name="s5_disc_bwd", out_shape=[c1, c1, jax.ShapeDtypeStruct((S5_G, 1), f32), c16, c16])(
        a_re, a_im, ls, b_re, b_im, d_ar, d_ai, d_br, d_bi, seg)


ANY = pl.BlockSpec(memory_space=pl.ANY)

GATHER = {'w_in': ((4352, 1024), 0), 'ffn_w_up': ((1024, 5632), 1), 'w_branch_rwkv': ((512, 1024), 1),
          'w_branch_s5': ((512, 1024), 1), 'w_out': ((1024, 1024), 0), 's5_w_glu': ((512, 512), 0),
          'ffn_w_down': ((2816, 1024), 0), 'rwkv_w2': ((64, 512), 1), 'rwkv_a2': ((64, 512), 1),
          'rwkv_g2': ((128, 512), 1), 'ffn_conv_w': ((8, 5632), 1)}
BIG = ['w_in', 'ffn_w_up', 'w_branch_rwkv', 'w_branch_s5', 'w_out', 's5_w_glu', 'ffn_w_down']
TINY = ['rwkv_w2', 'rwkv_a2', 'rwkv_g2', 'ffn_conv_w']
SMALL = [n for n in WEIGHTS if n not in GATHER]
SMALL_ROWS = 320
ADAM_ROWS = 256


def _mo(v, m):
    return v if isinstance(v, int) else pl.multiple_of(v, m)


def _slab(ref, name, j, h=None):
    (R, Cn), axis = GATHER[name]
    if axis == 0:
        rs = R // 4
        if h is None:
            return ref.at[pl.ds(_mo(j * rs, 16), rs), :]
        return ref.at[pl.ds(_mo(j * rs + h * (rs // 2), 8), rs // 2), :]
    cols = pl.ds(_mo(j * (Cn // 4), 128), Cn // 4)
    if h is None:
        return ref.at[:, cols]
    return ref.at[pl.ds(_mo(h * (R // 2), 8), R // 2), cols]


def _half_shape(name):
    (R, Cn), axis = GATHER[name]
    return (R // 8, Cn) if axis == 0 else (R // 2, Cn // 4)


def _chip_peers(px, py):
    return [((1 - px) if (k >> 1) else px, (1 - py) if (k & 1) else py) for k in (1, 2, 3)]


def _run_copies(copies):
    for cp in copies:
        cp.start()
    for cp in copies:
        cp.wait()


def _gather_weights(blocks):
    names = list(blocks)
    n = len(names)

    def body(*refs):
        ins, outs = refs[:n], refs[n:2 * n]
        ssem, rsem, lsem = refs[2 * n:]
        px, py, pc = _mesh_pos()
        me = 2 * px + py
        copies = []
        for i, nm in enumerate(names):
            if nm in BIG:
                hr = blocks[nm].shape[0] // 2
                src, dst = ins[i].at[pl.ds(pl.multiple_of(pc * hr, 16), hr), :], _slab(outs[i], nm, me, pc)
            else:
                src, dst = ins[i], _slab(outs[i], nm, me)
            copies.append(pltpu.make_async_copy(src, dst, lsem.at[i]))
            for k, (qx, qy) in enumerate(_chip_peers(px, py)):
                copies.append(pltpu.make_async_remote_copy(src, dst, ssem.at[3 * i + k], rsem.at[3 * i + k],
                                                           device_id=(qx, qy, pc), device_id_type=MESH))
        _run_copies(copies)

    outs = pl.pallas_call(
        body, name="gather_weights", in_specs=[ANY] * n, out_specs=[ANY] * n,
        out_shape=[jax.ShapeDtypeStruct(GATHER[nm][0], blocks[nm].dtype) for nm in names],
        scratch_shapes=[pltpu.SemaphoreType.DMA((3 * n,)), pltpu.SemaphoreType.DMA((3 * n,)), pltpu.SemaphoreType.DMA((n,))],
    )(*[blocks[nm] for nm in names])
    return dict(zip(names, outs))


def _gather_pair(full, names, call_name):
    n = len(names)

    def body(*refs):
        ins, outs = refs[:n], refs[n:2 * n]
        ssem, rsem = refs[2 * n:]
        px, py, pc = _mesh_pos()
        copies = []
        for i, nm in enumerate(names):
            for j in range(4):
                copies.append(pltpu.make_async_remote_copy(_slab(ins[i], nm, j, pc), _slab(outs[i], nm, j, pc),
                                                           ssem.at[4 * i + j], rsem.at[4 * i + j],
                                                           device_id=(px, py, 1 - pc), device_id_type=MESH))
        _run_copies(copies)

    outs = pl.pallas_call(
        body, name=call_name, in_specs=[ANY] * n, out_specs=[ANY] * n,
        out_shape=[jax.ShapeDtypeStruct(full[nm].shape, full[nm].dtype) for nm in names],
        input_output_aliases={i: i for i in range(n)},
        scratch_shapes=[pltpu.SemaphoreType.DMA((4 * n,)), pltpu.SemaphoreType.DMA((4 * n,))],
    )(*[full[nm] for nm in names])
    return dict(zip(names, outs))


def _grads_to_sibling(G, names, call_name, small=None):
    n = len(names)
    ns = 0 if small is None else 1

    def body(*refs):
        g_refs, o_refs = refs[:n + ns], refs[n + ns:2 * (n + ns)]
        ssem, rsem = refs[2 * (n + ns):]
        px, py, pc = _mesh_pos()
        sib = (px, py, 1 - pc)
        copies = []
        for i, nm in enumerate(names):
            for j in range(4):
                copies.append(pltpu.make_async_remote_copy(_slab(g_refs[i], nm, j, 1 - pc), o_refs[i].at[j],
                                                           ssem.at[4 * i + j], rsem.at[4 * i + j],
                                                           device_id=sib, device_id_type=MESH))
        if ns:
            copies.append(pltpu.make_async_remote_copy(g_refs[n], o_refs[n], ssem.at[4 * n], rsem.at[4 * n],
                                                       device_id=sib, device_id_type=MESH))
        _run_copies(copies)

    outs = pl.pallas_call(
        body, name=call_name, in_specs=[ANY] * (n + ns), out_specs=[ANY] * (n + ns),
        out_shape=[jax.ShapeDtypeStruct((4,) + _half_shape(nm), f32) for nm in names]
        + [jax.ShapeDtypeStruct((SMALL_ROWS, PACK_W), f32)] * ns,
        scratch_shapes=[pltpu.SemaphoreType.DMA((4 * n + ns,)), pltpu.SemaphoreType.DMA((4 * n + ns,))],
    )(*[G[nm] for nm in names], *([small] * ns))
    return dict(zip(names, outs[:n])), (outs[n] if ns else None)


def _pair_add(G, recv, names, call_name, small=None, small_recv=None):
    n = len(names)
    ns = 0 if small is None else 1
    cidx = lax.axis_index("c").astype(jnp.int32).reshape(1)

    def body(c_ref, *refs):
        ins, outs = refs[:2 * (n + ns)], refs[2 * (n + ns):]
        for i in range(n):
            outs[i][...] = (ins[i][...] + ins[n + ns + i][...]).astype(bf16)
        if ns:
            outs[n][...] = ins[n][...] + ins[2 * n + 1][...]

    g_specs, r_specs = [], []
    for nm in names:
        hr, hc = _half_shape(nm)
        if GATHER[nm][1] == 0:
            g_specs.append(pl.BlockSpec((hr // 2, hc), lambda j, i, c: ((2 * j + c[0]) * 2 + i, 0)))
        else:
            g_specs.append(pl.BlockSpec((hr // 2, hc), lambda j, i, c: (2 * c[0] + i, j)))
        r_specs.append(pl.BlockSpec((1, hr // 2, hc), lambda j, i, c: (j, i, 0)))
    sm = [pl.BlockSpec((SMALL_ROWS // 8, PACK_W), lambda j, i, c: (2 * j + i, 0))] * ns
    outs = pl.pallas_call(
        body, name=call_name,
        grid_spec=pltpu.PrefetchScalarGridSpec(num_scalar_prefetch=1, grid=(4, 2), in_specs=g_specs + sm + r_specs + sm,
                                               out_specs=r_specs + sm),
        out_shape=[jax.ShapeDtypeStruct((4,) + _half_shape(nm), bf16) for nm in names]
        + [jax.ShapeDtypeStruct((SMALL_ROWS, PACK_W), f32)] * ns,
        compiler_params=pltpu.CompilerParams(vmem_limit_bytes=VMEM_LIMIT),
    )(cidx, *[G[nm] for nm in names], *([small] * ns), *[recv[nm] for nm in names], *([small_recv] * ns))
    return dict(zip(names, outs[:n])), (outs[n] if ns else None)


def _exchange_copies(ins, outs, ssem, rsem):
    px, py, pc = _mesh_pos()
    me = 2 * px + py
    return [pltpu.make_async_remote_copy(ins[i].at[2 * qx + qy], outs[i].at[me], ssem.at[3 * i + k], rsem.at[3 * i + k],
                                         device_id=(qx, qy, pc), device_id_type=MESH)
            for i in range(len(ins)) for k, (qx, qy) in enumerate(_chip_peers(px, py))]


def _grads_chip_exchange(chip_sum, names, small):
    n = len(names)

    def body(*refs):
        ins, outs = refs[:n + 1], refs[n + 1:2 * n + 2]
        ssem, rsem, ssem_s, rsem_s = refs[2 * n + 2:]
        px, py, pc = _mesh_pos()
        me = 2 * px + py
        copies = _exchange_copies(ins[:n], outs[:n], ssem, rsem)
        copies += [pltpu.make_async_remote_copy(ins[n], outs[n].at[me], ssem_s.at[k], rsem_s.at[k],
                                                device_id=(qx, qy, pc), device_id_type=MESH)
                   for k, (qx, qy) in enumerate(_chip_peers(px, py))]
        _run_copies(copies)

    outs = pl.pallas_call(
        body, name="grads_chip_exchange", in_specs=[ANY] * (n + 1), out_specs=[ANY] * (n + 1),
        out_shape=[jax.ShapeDtypeStruct(chip_sum[nm].shape, chip_sum[nm].dtype) for nm in names]
        + [jax.ShapeDtypeStruct((4,) + small.shape, f32)],
        scratch_shapes=[pltpu.SemaphoreType.DMA((3 * n,)), pltpu.SemaphoreType.DMA((3 * n,)),
                        pltpu.SemaphoreType.DMA((3,)), pltpu.SemaphoreType.DMA((3,))],
    )(*[chip_sum[nm] for nm in names], small)
    return dict(zip(names, outs[:n])), outs[n]


def _sum_slots(slots, chip_sum, small4, small_own):
    n = len(BIG)
    me = (2 * lax.axis_index("x") + lax.axis_index("y")).astype(jnp.int32).reshape(1)

    def body(me_ref, *refs):
        for i in range(n + 1):
            own = refs[5 * i + 4][...].astype(f32)
            own = own[0] if i < n else own
            term = [jnp.where(me_ref[0] == k, own, refs[5 * i + k][0].astype(f32)) for k in range(4)]
            refs[5 * (n + 1) + i][...] = ((term[0] + term[1]) + term[2]) + term[3]

    redirect = lambda k: (lambda i, m: (jnp.where(m[0] == k, (k + 1) % 4, k), i, 0))
    in_specs, args, specs_out, shapes = [], [], [], []
    for nm in BIG:
        hr, hc = _half_shape(nm)
        in_specs += [pl.BlockSpec((1, hr // 2, hc), redirect(k)) for k in range(4)]
        in_specs.append(pl.BlockSpec((1, hr // 2, hc), lambda i, m: (m[0], i, 0)))
        args += [slots[nm]] * 4 + [chip_sum[nm]]
        specs_out.append(pl.BlockSpec((hr // 2, hc), lambda i, m: (i, 0)))
        shapes.append(jax.ShapeDtypeStruct((hr, hc), f32))
    in_specs += [pl.BlockSpec((1, SMALL_ROWS // 2, PACK_W), redirect(k)) for k in range(4)]
    in_specs.append(pl.BlockSpec((SMALL_ROWS // 2, PACK_W), lambda i, m: (i, 0)))
    args += [small4] * 4 + [small_own]
    specs_out.append(pl.BlockSpec((SMALL_ROWS // 2, PACK_W), lambda i, m: (i, 0)))
    shapes.append(jax.ShapeDtypeStruct((SMALL_ROWS, PACK_W), f32))
    outs = pl.pallas_call(
        body, name="grads_chip_sum",
        grid_spec=pltpu.PrefetchScalarGridSpec(num_scalar_prefetch=1, grid=(2,), in_specs=in_specs, out_specs=specs_out),
        out_shape=shapes, compiler_params=pltpu.CompilerParams(vmem_limit_bytes=VMEM_LIMIT),
    )(me, *args)
    return dict(zip(BIG, outs[:n])), outs[n]


def _halves_to_sibling(half):
    n = len(BIG)

    def body(*refs):
        ins, outs = refs[:n], refs[n:2 * n]
        ssem, rsem = refs[2 * n:]
        px, py, pc = _mesh_pos()
        _run_copies([pltpu.make_async_remote_copy(ins[i], outs[i], ssem.at[i], rsem.at[i],
                                                  device_id=(px, py, 1 - pc), device_id_type=MESH) for i in range(n)])

    outs = pl.pallas_call(
        body, name="grads_halves_to_sibling", in_specs=[ANY] * n, out_specs=[ANY] * n,
        out_shape=[jax.ShapeDtypeStruct(_half_shape(nm), f32) for nm in BIG],
        scratch_shapes=[pltpu.SemaphoreType.DMA((n,)), pltpu.SemaphoreType.DMA((n,))],
    )(*[half[nm] for nm in BIG])
    return dict(zip(BIG, outs))


def _join_halves(mine, other, pc):
    hr = mine.shape[0]
    return lax.dynamic_slice_in_dim(jnp.concatenate([other, mine, other], axis=0), (1 - pc) * hr, 2 * hr, axis=0)


def _flat_pad(v):
    v = v.reshape(-1)
    return jnp.pad(v, (0, _ceil_to(v.shape[0], PACK_W) - v.shape[0]))


def _pack_rows(parts, rows):
    flat = jnp.concatenate([_flat_pad(p) for p in parts])
    return jnp.pad(flat, (0, rows * PACK_W - flat.shape[0])).reshape(rows, PACK_W)


def _unpack_rows(buf, shapes):
    flat = buf.reshape(-1)
    out, off = [], 0
    for shp in shapes:
        n = 1
        for d in shp:
            n *= d
        out.append(flat[off:off + n].reshape(shp))
        off += _ceil_to(n, PACK_W)
    return out


def _adamw_math(w_, g_, m_, v_):
    m2 = ADAM_B1 * m_ + (1.0 - ADAM_B1) * g_
    v2 = ADAM_B2 * v_ + (1.0 - ADAM_B2) * (g_ * g_)
    m_hat = m2 / (1.0 - ADAM_B1 ** ADAM_STEP)
    v_hat = v2 / (1.0 - ADAM_B2 ** ADAM_STEP)
    return -ADAM_LR * (m_hat / (jnp.sqrt(v_hat) + ADAM_EPS) + ADAM_WD * w_), m2, v2


def _adamw(groups):
    ng = len(groups)

    def body(*refs):
        ins, outs = refs[:4 * ng], refs[4 * ng:]
        for i in range(ng):
            res = _adamw_math(*(r[...] for r in ins[4 * i:4 * i + 4]))
            for ref, val in zip(outs[3 * i:3 * i + 3], res):
                ref[...] = val

    in_specs, out_specs, out_shape = [], [], []
    for grp in groups:
        R, Cn = grp[0].shape
        spec = pl.BlockSpec((R // 8, Cn), lambda i: (i, 0))
        in_specs += [spec] * 4
        out_specs += [spec] * 3
        out_shape += [jax.ShapeDtypeStruct((R, Cn), f32)] * 3
    outs = pl.pallas_call(
        body, name="adamw", grid=(8,), in_specs=in_specs, out_specs=out_specs, out_shape=out_shape,
        compiler_params=pltpu.CompilerParams(vmem_limit_bytes=VMEM_LIMIT),
    )(*[a for grp in groups for a in grp])
    return [tuple(outs[3 * i:3 * i + 3]) for i in range(ng)]


def _forward_backward(x, tgt, W, S, late):
    L = x.shape[0]
    TM, TMW, TS = 256, 128, 512
    row = lambda c, dt=f32: (c, dt)
    hid = jnp.arange(RWKV_W) // HEAD
    E = (hid[:, None] == hid[None, :]).astype(f32)
    seg = (jnp.arange(S5_N)[None, :] // S5_P == jnp.arange(S5_G)[:, None]).astype(f32)

    w_in_t = W['w_in']
    w_p, w_u, w_g = w_in_t[:N_RWKV], w_in_t[N_RWKV:N_RWKV + S5_W], w_in_t[N_RWKV + S5_W:]
    zpad = jnp.zeros((64, RWKV_W), f32)
    w2p = jnp.concatenate([W['rwkv_w2'], zpad], axis=0)
    a2p = jnp.concatenate([zpad, W['rwkv_a2']], axis=0)
    g2 = W['rwkv_g2']
    prep_consts = [S['rwkv_shift_mu'], S['rwkv_w0'], S['rwkv_a0'], S['rwkv_k_k'], S['rwkv_k_a'], w2p, a2p, g2, E]
    out_consts = [S['rwkv_lnx_w'], S['rwkv_lnx_b'], S['rwkv_r_k'], E]
    cw, cb = W['ffn_conv_w'][:3], S['ffn_conv_b']

    a_re, a_im = S['s5_a_re'].reshape(S5_N, 1), S['s5_a_im'].reshape(S5_N, 1)
    ls = jnp.repeat(S['s5_log_step'].reshape(S5_G, 1), S5_P, axis=0)
    b_re, b_im = S['s5_b_re'].reshape(S5_N, S5_C), S['s5_b_im'].reshape(S5_N, S5_C)
    ar, ai, bbr, bbi = _s5_disc_fwd(a_re, a_im, ls, b_re, b_im)
    abar = jnp.concatenate([ar.reshape(1, S5_N), ai.reshape(1, S5_N)], axis=1)
    eye8 = jnp.eye(8, dtype=f32)

    def blocks_in(bb):
        t = bb.reshape(4, 8, S5_P, S5_C).transpose(0, 1, 3, 2)
        return (t[:, :, :, None, :] * eye8[None, :, None, :, None]).reshape(4, _CB, _SB)

    def blocks_out(cc):
        t = cc.reshape(4, 8, S5_C, S5_P).transpose(0, 1, 3, 2)
        return (t[:, :, :, None, :] * eye8[None, :, None, :, None]).reshape(4, _SB, _CB)

    def undiag_in(blocks):
        t = blocks.reshape(4, 8, S5_C, 8, S5_P)
        t = jnp.sum(t * eye8[None, :, None, :, None], axis=3)
        return t.reshape(S5_G, S5_C, S5_P).transpose(0, 2, 1).reshape(S5_N, S5_C)

    def undiag_out(blocks):
        t = blocks.reshape(4, 8, S5_P, 8, S5_C)
        t = jnp.sum(t * eye8[None, :, None, :, None], axis=3)
        return t.reshape(S5_G, S5_P, S5_C).transpose(0, 2, 1)

    bmat = jnp.concatenate([blocks_in(bbr), blocks_in(bbi)], axis=0).astype(bf16)
    cmat = jnp.concatenate([blocks_out(S['s5_c_re'].reshape(S5_G, S5_C, S5_P)),
                            -blocks_out(S['s5_c_im'].reshape(S5_G, S5_C, S5_P))], axis=0).astype(bf16)

    g1, g2n, g3, g4 = S['norm_mix_pre'], S['norm_mix_post'], S['norm_ffn_pre'], S['norm_ffn_post']
    (h1,) = _rowcall("norm_pre", lambda i, n, R, P, X, C: ((_rms(R[0], C[0]),), ()), L, TS, [x], [g1],
                     out_rows=[row(D_MODEL, bf16)])
    p = _mm(h1, w_p, 'nt', "mm_p")
    u = _mm(h1, w_u, 'nt', "mm_u")
    gp = _mm(h1, w_g, 'nt', "mm_g")

    def prep_fn(i, n, R, P, X, C):
        q = R[0] + (_shift_down(R[0], P[0], i, 1) - R[0]) * C[0]
        return _prep(q, *C[1:]), ()

    r, lw, k2, v, an, bv, g = _rowcall("rwkv_prep", prep_fn, L, TM, [p], prep_consts,
                                       out_rows=[row(RWKV_W)] * 7, prev=[0])
    y, ck = _wkv7_fwd(r, lw, k2, v, an, bv)
    (o_a,) = _rowcall("rwkv_out", lambda i, n, R, P, X, C: ((_rwkv_out(*R, *C),), ()), L, TM, [y, r, k2, v, g],
                      out_consts, out_rows=[row(RWKV_W, bf16)])
    o_r = _mm(o_a, W['w_branch_rwkv'], 'nn', "mm_br")

    st, ysc, got = _s5_fwd(u, bmat, cmat, abar, late)
    W = {**W, **_gather_pair(got, list(got), "gather_weights_pair_late")}
    (yg,) = _rowcall("s5_mid", lambda i, n, R, P, X, C: ((_s5_mid(*R, *C),), ()), L, TS, [ysc, u], [S['s5_d']],
                     out_rows=[row(S5_W)])
    z2 = _mm(yg, W['s5_w_glu'], 'nn', "mm_glu")
    (o_b,) = _rowcall("s5_glu", lambda i, n, R, P, X, C: ((_s5_glu(*R, *C),), ()), L, TS, [yg, z2], [S['s5_b_glu']],
                      out_rows=[row(S5_W, bf16)])
    o_s = _mm(o_b, W['w_branch_s5'], 'nn', "mm_bs")

    (merged,) = _rowcall("merge", lambda i, n, R, P, X, C: ((_merge(*R, *C),), ()), L, TS, [gp, o_r, o_s],
                         [S['b_gate']], out_rows=[row(D_MODEL, bf16)])
    mixed = _mm(merged, W['w_out'], 'nn', "mm_out")

    def resid_fn(i, n, R, P, X, C):
        x1_ = R[0] + _rms(R[1], C[0])
        return (x1_, _rms(x1_, C[1])), ()

    x1, h2 = _rowcall("resid_norm", resid_fn, L, TS, [x, mixed], [g2n, g3], out_rows=[row(D_MODEL), row(D_MODEL, bf16)])

    z = _mm(h2, W['ffn_w_up'], 'nn', "mm_up")

    def conv(zt, zprev, i, cw_, cb_):
        z2s, z1s = _shift_down(zt, zprev, i, 2), _shift_down(zt, zprev, i, 1)
        return cb_ + cw_[0:1] * z2s + cw_[1:2] * z1s + cw_[2:3] * zt, z2s, z1s

    (act,) = _rowcall("conv_act", lambda i, n, R, P, X, C: ((_act(conv(R[0], P[0], i, C[0], C[1])[0]),), ()), L, TMW,
                      [z], [cw, cb], out_rows=[row(D_FF, bf16)], prev=[0])
    f = _mm(act, W['ffn_w_down'], 'nn', "mm_down")

    def final_fn(i, n, R, P, X, C):
        x1_, f_, t_ = R
        fn_, vjp = jax.vjp(_rms, f_, C[0])
        diff = x1_ + fn_ - t_
        loss = jnp.sum(diff * diff) * (0.5 / D_MODEL)
        dx2_ = diff * (1.0 / D_MODEL)
        df_, dg4_ = vjp(dx2_)
        return (df_, dx2_), (jnp.full((1, PACK_W), loss, f32), dg4_)

    df, dx2, loss, dg4 = _rowcall("loss_head", final_fn, L, TS, [x1, f, tgt], [g4],
                                  out_rows=[row(D_MODEL, bf16), row(D_MODEL)], out_accs=[(1, PACK_W), (1, D_MODEL)])
    G = {'norm_ffn_post': dg4}

    dact = _mm(df, W['ffn_w_down'], 'nt', "mm_down_dx")
    G['ffn_w_down'] = _mm(act, df, 'tn', "mm_down_dw")

    def conv_bwd_fn(i, n, R, P, X, C):
        z_, dact_ = R
        cw_, cb_ = C
        zc, z2s, z1s = conv(z_, P[0], i, cw_, cb_)
        _, vjp = jax.vjp(_act, zc)
        (dzc_,) = vjp(dact_)
        last8 = z_[z_.shape[0] - 8:]
        zcn = cb_ + cw_[0:1] * _shift_down(X[0], last8, 1, 2) + cw_[1:2] * _shift_down(X[0], last8, 1, 1) + cw_[2:3] * X[0]
        _, vjpn = jax.vjp(_act, zcn)
        (dzcn,) = vjpn(X[1])
        dz_ = (cw_[2:3] * dzc_ + cw_[1:2] * _shift_up(dzc_, dzcn, i, n, 1) + cw_[0:1] * _shift_up(dzc_, dzcn, i, n, 2))
        return (dz_,), (_sum0(dzc_), _sum0(dzc_ * z2s), _sum0(dzc_ * z1s), _sum0(dzc_ * z_))

    wide = (1, 2 * D_FF)
    dz, dcb, dcw0, dcw1, dcw2 = _rowcall("conv_act_bwd", conv_bwd_fn, L, TMW, [z, dact], [cw, cb],
                                         out_rows=[row(2 * D_FF, bf16)], out_accs=[wide] * 4, prev=[0], nxt=[0, 1])
    G['ffn_conv_b'] = dcb
    G['ffn_conv_w'] = jnp.concatenate([dcw0, dcw1, dcw2], axis=0)
    dh2 = _mm(dz, W['ffn_w_up'], 'nt', "mm_up_dx")
    G['ffn_w_up'] = _mm(h2, dz, 'tn', "mm_up_dw")

    def norm2_bwd_fn(i, n, R, P, X, C):
        x1_, mixed_, dx2_, dh2_ = R
        _, vjp3 = jax.vjp(_rms, x1_, C[1])
        dx1a, dg3_ = vjp3(dh2_)
        dx1_ = dx2_ + dx1a
        _, vjp2 = jax.vjp(_rms, mixed_, C[0])
        dmixed_, dg2_ = vjp2(dx1_)
        return (dx1_, dmixed_), (dg2_, dg3_)

    dx1, dmixed, dg2n, dg3 = _rowcall("norm_mid_bwd", norm2_bwd_fn, L, TS, [x1, mixed, dx2, dh2], [g2n, g3],
                                      out_rows=[row(D_MODEL), row(D_MODEL, bf16)], out_accs=[(1, D_MODEL)] * 2)
    G['norm_mix_post'], G['norm_ffn_pre'] = dg2n, dg3

    dmerged = _mm(dmixed, W['w_out'], 'nt', "mm_out_dx")
    G['w_out'] = _mm(merged, dmixed, 'tn', "mm_out_dw")

    def merge_bwd_fn(i, n, R, P, X, C):
        _, vjp = jax.vjp(_merge, R[0], R[1], R[2], C[0])
        dgp_, do_r_, do_s_, dbg_ = vjp(R[3])
        return (dgp_, do_r_, do_s_), (dbg_,)

    dgp, do_r, do_s, G['b_gate'] = _rowcall("merge_bwd", merge_bwd_fn, L, TS, [gp, o_r, o_s, dmerged], [S['b_gate']],
                                            out_rows=[row(2 * D_MODEL, bf16), row(D_MODEL, bf16), row(D_MODEL, bf16)],
                                            out_accs=[(1, 2 * D_MODEL)])
    do_a = _mm(do_r, W['w_branch_rwkv'], 'nt', "mm_br_dx")
    G['w_branch_rwkv'] = _mm(o_a, do_r, 'tn', "mm_br_dw")
    do_b = _mm(do_s, W['w_branch_s5'], 'nt', "mm_bs_dx")
    G['w_branch_s5'] = _mm(o_b, do_s, 'tn', "mm_bs_dw")

    def glu_bwd_fn(i, n, R, P, X, C):
        _, vjp = jax.vjp(_s5_glu, R[0], R[1], C[0])
        dyg1_, dz2_, dbg_ = vjp(R[2])
        return (dyg1_, dz2_), (dbg_,)

    dyg1, dz2, G['s5_b_glu'] = _rowcall("s5_glu_bwd", glu_bwd_fn, L, TS, [yg, z2, do_b], [S['s5_b_glu']],
                                        out_rows=[row(S5_W), row(S5_W, bf16)], out_accs=[(1, S5_W)])
    dyg2 = _mm(dz2, W['s5_w_glu'], 'nt', "mm_glu_dx")
    G['s5_w_glu'] = _mm(yg, dz2, 'tn', "mm_glu_dw")

    def mid_bwd_fn(i, n, R, P, X, C):
        _, vjp = jax.vjp(_s5_mid, R[0], R[1], C[0])
        dysc_, du_, dd_ = vjp(R[2] + R[3])
        return (dysc_, du_), (dd_,)

    dysc, du1, G['s5_d'] = _rowcall("s5_mid_bwd", mid_bwd_fn, L, TS, [ysc, u, dyg1, dyg2], [S['s5_d']],
                                    out_rows=[row(S5_W, bf16), row(S5_W)], out_accs=[(1, S5_W)])
    early = [n for n in BIG if n != 'w_in']
    recv_e, _ = _grads_to_sibling(G, early, "grads_to_sibling_early")
    chip_e, _ = _pair_add(G, recv_e, early, "grads_pair_sum_early")
    du, dbmat, dcmat, dabar, slots_e = _s5_bwd(dysc, st, u, du1, bmat, cmat, abar, chip_e)
    da_re, da_im, dls, db_re, db_im = _s5_disc_bwd(
        a_re, a_im, ls, b_re, b_im, dabar[:, :S5_N].reshape(S5_N, 1), dabar[:, S5_N:].reshape(S5_N, 1),
        undiag_in(dbmat[:4]), undiag_in(dbmat[4:]), seg)
    G['s5_a_re'], G['s5_a_im'], G['s5_log_step'] = da_re, da_im, dls
    G['s5_b_re'], G['s5_b_im'] = db_re, db_im
    G['s5_c_re'], G['s5_c_im'] = undiag_out(dcmat[:4]), -undiag_out(dcmat[4:])

    def out_bwd_fn(i, n, R, P, X, C):
        _, vjp = jax.vjp(_rwkv_out, *R[:5], *C)
        gs = vjp(R[5])
        return gs[:5], gs[5:8]

    dy, dr1, dk1, dv1, dg, dlw, dlb, drk = _rowcall("rwkv_out_bwd", out_bwd_fn, L, TM, [y, r, k2, v, g, do_a], out_consts,
                                                    out_rows=[row(RWKV_W)] * 5, out_accs=[(1, RWKV_W)] * 3)
    G['rwkv_lnx_w'], G['rwkv_lnx_b'], G['rwkv_r_k'] = dlw, dlb, drk
    dr2, dlwk, dk2b, dv2, dan, dbv = _wkv7_bwd(r, lw, k2, v, an, bv, ck, dy)

    def prep_bwd_fn(i, n, R, P, X, C):
        p_ = R[0]
        d1 = _shift_down(p_, P[0], i, 1) - p_
        q = p_ + d1 * C[0]
        _, vjp = jax.vjp(_prep, q, *C[1:])
        cots = (R[1] + R[2], R[3], R[4] + R[5], R[6] + R[7], R[8], R[9], R[10])
        gs = vjp(cots)
        return (gs[0],), (_sum0(gs[0] * d1),) + tuple(gs[1:8])

    small, lowr = (1, RWKV_W), (128, RWKV_W)
    dq, dmu, dw0, da0, dkk, dka, dw2p, da2p, dg2 = _rowcall(
        "rwkv_prep_bwd", prep_bwd_fn, L, TM, [p, dr1, dr2, dlwk, dk1, dk2b, dv1, dv2, dan, dbv, dg],
        prep_consts, out_rows=[row(N_RWKV)], out_accs=[(1, N_RWKV)] + [small] * 4 + [lowr] * 3, prev=[0])
    G['rwkv_shift_mu'], G['rwkv_w0'], G['rwkv_a0'], G['rwkv_k_k'], G['rwkv_k_a'] = dmu, dw0, da0, dkk, dka
    G['rwkv_w2'], G['rwkv_a2'], G['rwkv_g2'] = dw2p[:64], da2p[64:], dg2

    def shift_bwd_fn(i, n, R, P, X, C):
        dm = R[0] * C[0]
        return (R[0] - dm + _shift_up(dm, X[0] * C[0], i, n, 1),), ()

    (dp,) = _rowcall("shift_bwd", shift_bwd_fn, L, TS, [dq], [S['rwkv_shift_mu']], out_rows=[row(N_RWKV, bf16)], nxt=[0])

    dproj = jnp.concatenate([dp, du, dgp], axis=1)
    dh1 = _mm(dproj, w_in_t, 'nn', "mm_in_dx")
    G['w_in'] = _mm(dproj, h1, 'tn', "mm_in_dw")

    def norm1_bwd_fn(i, n, R, P, X, C):
        _, vjp = jax.vjp(_rms, R[0], C[0])
        dxa, dg1_ = vjp(R[2])
        return (R[1] + dxa,), (dg1_,)

    dx, G['norm_mix_pre'] = _rowcall("norm_pre_bwd", norm1_bwd_fn, L, TS, [x, dx1, dh1], [g1],
                                     out_rows=[row(D_MODEL)], out_accs=[(1, D_MODEL)])
    return loss, dx, G, chip_e, slots_e


def kernel(x, norm_mix_pre, norm_mix_post, norm_ffn_pre, norm_ffn_post, w_in, b_gate, rwkv_shift_mu, rwkv_w0, rwkv_w2, rwkv_a0, rwkv_a2, rwkv_g2, rwkv_k_k, rwkv_k_a, rwkv_r_k, rwkv_lnx_w, rwkv_lnx_b, s5_a_re, s5_a_im, s5_b_re, s5_b_im, s5_c_re, s5_c_im, s5_d, s5_log_step, s5_w_glu, s5_b_glu, w_branch_rwkv, w_branch_s5, w_out, ffn_w_up, ffn_conv_w, ffn_conv_b, ffn_w_down, loss_target, m_norm_mix_pre, m_norm_mix_post, m_norm_ffn_pre, m_norm_ffn_post, m_w_in, m_b_gate, m_rwkv_shift_mu, m_rwkv_w0, m_rwkv_w2, m_rwkv_a0, m_rwkv_a2, m_rwkv_g2, m_rwkv_k_k, m_rwkv_k_a, m_rwkv_r_k, m_rwkv_lnx_w, m_rwkv_lnx_b, m_s5_a_re, m_s5_a_im, m_s5_b_re, m_s5_b_im, m_s5_c_re, m_s5_c_im, m_s5_d, m_s5_log_step, m_s5_w_glu, m_s5_b_glu, m_w_branch_rwkv, m_w_branch_s5, m_w_out, m_ffn_w_up, m_ffn_conv_w, m_ffn_conv_b, m_ffn_w_down, v_norm_mix_pre, v_norm_mix_post, v_norm_ffn_pre, v_norm_ffn_post, v_w_in, v_b_gate, v_rwkv_shift_mu, v_rwkv_w0, v_rwkv_w2, v_rwkv_a0, v_rwkv_a2, v_rwkv_g2, v_rwkv_k_k, v_rwkv_k_a, v_rwkv_r_k, v_rwkv_lnx_w, v_rwkv_lnx_b, v_s5_a_re, v_s5_a_im, v_s5_b_re, v_s5_b_im, v_s5_c_re, v_s5_c_im, v_s5_d, v_s5_log_step, v_s5_w_glu, v_s5_b_glu, v_w_branch_rwkv, v_w_branch_s5, v_w_out, v_ffn_w_up, v_ffn_conv_w, v_ffn_conv_b, v_ffn_w_down):
    A = dict(locals())
    me = 2 * lax.axis_index("x") + lax.axis_index("y")
    blk = lambda n: A[n][0]

    mine = {n: (blk(n).T if n == 'w_in' else blk(n)).astype(bf16) for n in BIG}
    mine.update({n: blk(n) for n in TINY})
    mine['ffn_conv_w'] = jnp.pad(blk('ffn_conv_w'), ((0, 5), (0, 0)))
    late = ['ffn_w_up', 'ffn_w_down']
    W = _gather_weights({n: blkv for n, blkv in mine.items() if n not in late})
    W.update(_gather_pair(W, [n for n in BIG if n not in late], "gather_weights_pair"))
    S = {n: A[n].reshape(1, -1) for n in SMALL}

    loss, dx, G, chip_e, slots_e = _forward_backward(x[0], loss_target[0], W, S, {n: mine[n] for n in late})

    tiny_shapes = [G[n].shape for n in TINY]
    small_buf = _pack_rows([G[n] for n in SMALL] + [G[n] for n in TINY] + [loss], SMALL_ROWS)
    recv, small_recv = _grads_to_sibling(G, ['w_in'], "grads_to_sibling", small_buf)
    chip_l, small_sum = _pair_add(G, recv, ['w_in'], "grads_pair_sum", small_buf, small_recv)
    slots_l, small4 = _grads_chip_exchange(chip_l, ['w_in'], small_sum)
    half, small_tot = _sum_slots({**slots_e, **slots_l}, {**chip_e, **chip_l}, small4, small_sum)
    other = _halves_to_sibling(half)
    pc = lax.axis_index("c")
    grad = {n: _join_halves(half[n], other[n], pc) for n in BIG}
    grad['w_in'] = grad['w_in'].T
    vals = _unpack_rows(small_tot, [A[n].shape for n in SMALL] + tiny_shapes + [(1, PACK_W)])
    grad.update(zip(SMALL, vals))
    for n, full in zip(TINY, vals[len(SMALL):]):
        cs = A[n].shape[2]
        grad[n] = lax.dynamic_slice_in_dim(full, me * cs, cs, axis=1)
    loss_out = vals[-1][0, 0]

    packed = SMALL + TINY
    groups = [(blk(n), grad[n], blk('m_' + n), blk('v_' + n)) for n in BIG]
    groups.append(tuple(_pack_rows([src(n) for n in packed], ADAM_ROWS)
                        for src in (lambda n: A[n], lambda n: grad[n], lambda n: A['m_' + n], lambda n: A['v_' + n])))
    res = _adamw(groups)
    outs = [dict(), dict(), dict()]
    for n, r3 in zip(BIG, res[:-1]):
        for d, val in zip(outs, r3):
            d[n] = val
    for d, buf in zip(outs, res[-1]):
        d.update(zip(packed, _unpack_rows(buf, [A[n].shape for n in packed])))
    full = lambda d: [d[n].reshape(A[n].shape) for n in WEIGHTS]
    return (loss_out, dx[None], *full(grad), *full(outs[0]), *full(outs[1]), *full(outs[2]))
```

```python
import functools

import jax
import jax.numpy as jnp
from jax import lax
from jax.experimental import pallas as pl
from jax.experimental.pallas import tpu as pltpu

f32, bf16 = jnp.float32, jnp.bfloat16
MESH = pl.DeviceIdType.MESH

D_MODEL = 1024
RWKV_W = 512
HEADS, HEAD = 8, 64
N_RWKV = 1792
S5_W = 512
S5_G, S5_P, S5_C = 32, 64, 16
S5_N = S5_G * S5_P
D_FF = 2816
NORM_EPS = 1e-6
LNX_EPS = 64e-5
ADAM_LR, ADAM_B1, ADAM_B2, ADAM_EPS, ADAM_WD, ADAM_STEP = 0.001, 0.9, 0.999, 1e-08, 0.01, 10

VMEM_LIMIT = 48 * 1024 * 1024
PACK_W = 1024
WKV_C = 64
S5_T = 256

WEIGHTS = ['norm_mix_pre', 'norm_mix_post', 'norm_ffn_pre', 'norm_ffn_post', 'w_in', 'b_gate', 'rwkv_shift_mu',
           'rwkv_w0', 'rwkv_w2', 'rwkv_a0', 'rwkv_a2', 'rwkv_g2', 'rwkv_k_k', 'rwkv_k_a', 'rwkv_r_k', 'rwkv_lnx_w',
           'rwkv_lnx_b', 's5_a_re', 's5_a_im', 's5_b_re', 's5_b_im', 's5_c_re', 's5_c_im', 's5_d', 's5_log_step',
           's5_w_glu', 's5_b_glu', 'w_branch_rwkv', 'w_branch_s5', 'w_out', 'ffn_w_up', 'ffn_conv_w', 'ffn_conv_b',
           'ffn_w_down']


def _ceil_to(n, m):
    return -(-n // m) * m


def _mesh_pos():
    return lax.axis_index("x"), lax.axis_index("y"), lax.axis_index("c")


def _pick(d, cap=4096):
    for c in (1024, 1408, 2176, 896, 512, 256, 128):
        if c <= cap and d % c == 0:
            return c
    raise ValueError(d)


def _mm_resident(a, w, mode, name, M, N, K, out_dtype):
    budget = 40 * 1024 * 1024 - 2 * K * N
    tm = next(t for t in (512, 256, 128) if 2 * t * (K * a.dtype.itemsize + 4 * N) <= budget)
    dims = _DIMS[mode]

    def body(a_ref, w_ref, o_ref):
        o_ref[...] = lax.dot_general(a_ref[...].astype(bf16), w_ref[...], (dims, ((), ())),
                                     preferred_element_type=f32).astype(o_ref.dtype)

    return pl.pallas_call(
        body, name=name, grid=(M // tm,),
        in_specs=[pl.BlockSpec((tm, K), lambda i: (i, 0)),
                  pl.BlockSpec(w.shape, lambda i: (0, 0), pipeline_mode=pl.Buffered(1))],
        out_specs=pl.BlockSpec((tm, N), lambda i: (i, 0)), out_shape=jax.ShapeDtypeStruct((M, N), out_dtype),
        compiler_params=pltpu.CompilerParams(dimension_semantics=("parallel",), vmem_limit_bytes=VMEM_LIMIT),
    )(a, w)


def _mm(a, b, mode, name, out_dtype=f32):
    if mode == 'tn':
        (K, M), (K2, N) = a.shape, b.shape
    elif mode == 'nt':
        (M, K), (N, K2) = a.shape, b.shape
    else:
        (M, K), (K2, N) = a.shape, b.shape
    assert K == K2, (name, a.shape, b.shape)
    if mode != 'tn' and b.dtype == bf16:
        return _mm_resident(a, b, mode, name, M, N, K, out_dtype)
    if mode == 'tn':
        tm = _pick(M, 2176)
        tn = _pick(N, 512 if tm > 1408 else (1024 if tm > 1024 else 1408))
        tk = _pick(K, 512)
    else:
        tm, tn, tk = _pick(M, 512), _pick(N), _pick(K)
    nk = K // tk
    dims = {'nn': ((1,), (0,)), 'nt': ((1,), (1,)), 'tn': ((0,), (0,))}[mode]

    def body(a_ref, b_ref, o_ref, acc_ref):
        k = pl.program_id(2)

        @pl.when(k == 0)
        def _():
            acc_ref[...] = jnp.zeros_like(acc_ref)

        acc_ref[...] += lax.dot_general(a_ref[...].astype(bf16), b_ref[...].astype(bf16), (dims, ((), ())),
                                        preferred_element_type=f32)

        @pl.when(k == nk - 1)
        def _():
            o_ref[...] = acc_ref[...].astype(o_ref.dtype)

    a_spec = pl.BlockSpec((tk, tm), lambda i, j, k: (k, i)) if mode == 'tn' else pl.BlockSpec((tm, tk), lambda i, j, k: (i, k))
    b_spec = pl.BlockSpec((tn, tk), lambda i, j, k: (j, k)) if mode == 'nt' else pl.BlockSpec((tk, tn), lambda i, j, k: (k, j))
    return pl.pallas_call(
        body, name=name, grid=(M // tm, N // tn, nk),
        in_specs=[a_spec, b_spec], out_specs=pl.BlockSpec((tm, tn), lambda i, j, k: (i, j)),
        out_shape=jax.ShapeDtypeStruct((M, N), out_dtype),
        scratch_shapes=[pltpu.VMEM((tm, tn), f32)],
        compiler_params=pltpu.CompilerParams(dimension_semantics=("parallel", "parallel", "arbitrary"),
                                             vmem_limit_bytes=VMEM_LIMIT),
    )(a, b)


def _rowcall(name, fn, L, tm, rows, consts=(), out_rows=(), out_accs=(), prev=(), nxt=()):
    nsteps = L // tm
    nb8 = tm // 8
    last8 = L // 8 - 1
    n_r, n_p, n_x, n_c, n_or = len(rows), len(prev), len(nxt), len(consts), len(out_rows)

    def body(*refs):
        i = pl.program_id(0)
        vals = [r[...] for r in refs[:n_r + n_p + n_x + n_c]]
        R, P = vals[:n_r], vals[n_r:n_r + n_p]
        X, C = vals[n_r + n_p:n_r + n_p + n_x], vals[n_r + n_p + n_x:]
        o_refs = refs[n_r + n_p + n_x + n_c:]
        outs_r, outs_a = fn(i, nsteps, R, P, X, C)
        for ref, v in zip(o_refs[:n_or], outs_r, strict=True):
            ref[...] = v.astype(ref.dtype)
        if out_accs:
            @pl.when(i == 0)
            def _():
                for ref in o_refs[n_or:]:
                    ref[...] = jnp.zeros_like(ref)

            for ref, v in zip(o_refs[n_or:], outs_a, strict=True):
                ref[...] += v

    def const_spec(c):
        nd = c.ndim
        return pl.BlockSpec(c.shape, lambda i: (0,) * nd)

    in_specs = ([pl.BlockSpec((tm, a.shape[1]), lambda i: (i, 0)) for a in rows]
                + [pl.BlockSpec((8, rows[j].shape[1]), lambda i: (jnp.maximum(i * nb8 - 1, 0), 0)) for j in prev]
                + [pl.BlockSpec((8, rows[j].shape[1]), lambda i: (jnp.minimum((i + 1) * nb8, last8), 0)) for j in nxt]
                + [const_spec(c) for c in consts])
    out_specs = ([pl.BlockSpec((tm, c), lambda i: (i, 0)) for c, _ in out_rows]
                 + [pl.BlockSpec(s, lambda i: (0, 0)) for s in out_accs])
    out_shape = ([jax.ShapeDtypeStruct((L, c), dt) for c, dt in out_rows]
                 + [jax.ShapeDtypeStruct(s, f32) for s in out_accs])
    args = list(rows) + [rows[j] for j in prev] + [rows[j] for j in nxt] + list(consts)
    return pl.pallas_call(
        body, name=name, grid=(nsteps,), in_specs=in_specs, out_specs=out_specs, out_shape=out_shape,
        compiler_params=pltpu.CompilerParams(dimension_semantics=("arbitrary",), vmem_limit_bytes=VMEM_LIMIT),
    )(*args)


def _shift_down(x, prev8, i, k):
    rolled = pltpu.roll(x, k, axis=0)
    pfix = jnp.where(i > 0, pltpu.roll(prev8, k, axis=0), 0.0)
    row8 = lax.broadcasted_iota(jnp.int32, pfix.shape, 0)
    top = jnp.where(row8 < k, pfix, rolled[:8])
    return top if x.shape[0] == 8 else jnp.concatenate([top, rolled[8:]], axis=0)


def _shift_up(x, next8, i, nsteps, k):
    tm = x.shape[0]
    rolled = pltpu.roll(x, tm - k, axis=0)
    nfix = jnp.where(i < nsteps - 1, pltpu.roll(next8, 8 - k, axis=0), 0.0)
    row8 = lax.broadcasted_iota(jnp.int32, nfix.shape, 0)
    bot = jnp.where(row8 >= 8 - k, nfix, rolled[tm - 8:])
    return jnp.concatenate([rolled[:tm - 8], bot], axis=0)


def _sum0(x):
    return jnp.sum(x, axis=0, keepdims=True)


def _rms(x, g):
    return x * lax.rsqrt(jnp.mean(x * x, axis=-1, keepdims=True) + NORM_EPS) * g


def _softplus(x):
    return jnp.maximum(x, 0.0) + jnp.log(1.0 + jnp.exp(-jnp.abs(x)))


def _gelu(x):
    return 0.5 * x * (1.0 + jnp.tanh(0.7978845608028654 * (x + 0.044715 * x * x * x)))


def _dot32(a, b):
    return jnp.dot(a, b, preferred_element_type=f32, precision=lax.Precision.HIGHEST)


def _seg_raw(x, E):
    hi = x.astype(bf16)
    r1 = x - hi.astype(f32)
    mid = r1.astype(bf16)
    lo = (r1 - mid.astype(f32)).astype(bf16)
    Eb = E.astype(bf16)
    dot = lambda t: jnp.dot(t, Eb, preferred_element_type=f32)
    return (dot(lo) + dot(mid)) + dot(hi)


@jax.custom_vjp
def _seg(x, E):
    return _seg_raw(x, E)


_seg.defvjp(lambda x, E: (_seg_raw(x, E), E), lambda E, g: (_seg_raw(g, E), jnp.zeros_like(E)))


def _prep(q, w0, a0, k_k, k_a, w2p, a2p, g2, E):
    r, k, v = q[:, 0:512], q[:, 512:1024], q[:, 1024:1536]
    wa, gd = q[:, 1536:1664], q[:, 1664:1792]
    wlog = -_softplus(-(w0 + _bdot(jnp.tanh(wa), w2p, 'nn'))) - 0.5
    lw = -jnp.exp(wlog)
    a = jax.nn.sigmoid(a0 + _bdot(wa, a2p, 'nn'))
    g = _bdot(jax.nn.sigmoid(gd), g2, 'nn')
    kk = k * k_k
    kkn = kk / jnp.maximum(jnp.sqrt(_seg(kk * kk, E)), 1e-12)
    k2 = k * (1.0 + (a - 1.0) * k_a)
    return r, lw, k2, v, -kkn, kkn * a, g


def _rwkv_out(y, r, k2, v, g, lnx_w, lnx_b, r_k, E):
    mean = _seg(y, E) * (1.0 / HEAD)
    yc = y - mean
    var = _seg(yc * yc, E) * (1.0 / HEAD)
    yn = yc * lax.rsqrt(var + LNX_EPS) * lnx_w + lnx_b
    bonus = _seg(r * k2 * r_k, E) * v
    return (yn + bonus) * g


def _s5_mid(ysc, u, d):
    return _gelu(ysc + d * u)


def _s5_glu(yg, z2, b_glu):
    return yg * jax.nn.sigmoid(z2 + b_glu)


def _merge(gp, o_r, o_s, b_gate):
    gates = jax.nn.sigmoid(gp + b_gate)
    return gates[:, :D_MODEL] * o_r + gates[:, D_MODEL:] * o_s


def _act(zc):
    return _gelu(zc[:, :D_FF]) * zc[:, D_FF:]


def _s5_disc(a_re, a_im, ls, b_re, b_im):
    dt = jnp.exp(ls)
    er = jnp.exp(a_re * dt)
    ar, ai = er * jnp.cos(a_im * dt), er * jnp.sin(a_im * dt)
    x, y = ar - 1.0, ai
    den = a_re * a_re + a_im * a_im
    fr, fi = (x * a_re + y * a_im) / den, (y * a_re - x * a_im) / den
    return ar, ai, fr * b_re - fi * b_im, fr * b_im + fi * b_re


_DIMS = {'nn': ((1,), (0,)), 'nt': ((1,), (1,)), 'tn': ((0,), (0,))}


def _raw_bdot(a, b, mode):
    return lax.dot_general(a.astype(bf16), b.astype(bf16), (_DIMS[mode], ((), ())), preferred_element_type=f32)


@functools.partial(jax.custom_vjp, nondiff_argnums=(2,))
def _bdot(a, b, mode):
    return _raw_bdot(a, b, mode)


def _bdot_fwd(a, b, mode):
    return _raw_bdot(a, b, mode), (a, b)


def _bdot_bwd(mode, res, g):
    a, b = res
    if mode == 'nn':
        return _raw_bdot(g, b, 'nt'), _raw_bdot(a, g, 'tn')
    if mode == 'nt':
        return _raw_bdot(g, b, 'nn'), _raw_bdot(g, a, 'tn')
    return _raw_bdot(b, g, 'nt'), _raw_bdot(a, g, 'nn')


_bdot.defvjp(_bdot_fwd, _bdot_bwd)


def _wkv_chunk(S0, r, lw, k, v, a, b, tri, bd):
    C = r[0].shape[0]
    P = range(len(r))
    lane = lax.broadcasted_iota(jnp.int32, (1, 2 * HEAD), 1)
    m0, m1 = (lane < HEAD).astype(f32), (lane >= HEAD).astype(f32)
    cat = lambda *xs: jnp.concatenate(xs, axis=0)
    stack = lambda x: cat(x * m0, x * m1)
    unstack = lambda x2: m0 * x2[:C] + m1 * x2[C:]
    rid = lax.broadcasted_iota(jnp.int32, (2 * C, 2 * C), 0)
    cid = lax.broadcasted_iota(jnp.int32, (2 * C, 2 * C), 1)
    same = (rid < C) == (cid < C)
    eye2 = (rid == cid).astype(f32)
    tri2 = (same & (rid >= cid)).astype(f32)
    sl2 = tri2 - eye2
    cum = [_dot32(tri, lw[p]) for p in P]
    g = [jnp.exp(cum[p]) for p in P]
    gi = [jnp.exp(-cum[p]) for p in P]
    at = [a[p] * jnp.exp(cum[p] - lw[p]) for p in P]
    rt = [r[p] * g[p] for p in P]
    kb = [k[p] * gi[p] for p in P]
    bb = [b[p] * gi[p] for p in P]
    lhs = [cat(stack(at[p]), stack(rt[p])) for p in P]
    pb = [_bdot(lhs[p], stack(bb[p]), 'nt') for p in P]
    pk = [_bdot(lhs[p], stack(kb[p]), 'nt') for p in P]
    aab = [pb[p][:2 * C] * sl2 for p in P]
    base = [_bdot(cat(at[p], rt[p]), S0[p], 'nt') for p in P]
    t = [_bdot(cat(pk[p][:2 * C] * sl2, pk[p][2 * C:] * tri2), cat(v[p], v[p]), 'nn') for p in P]
    rhs = [cat(base[p][:C], base[p][:C]) + t[p][:2 * C] for p in P]
    x = [eye2 + aab[p] for p in P]
    pw = aab
    n = 1
    while 2 * n < C:
        pw = [_bdot(pw[p], pw[p], 'nn') for p in P]
        x = [x[p] + _bdot(x[p], pw[p], 'nn') for p in P]
        n *= 2
    u = [unstack(_bdot(x[p], rhs[p], 'nn')) for p in P]
    w2 = [_bdot(pb[p][2 * C:] * tri2, cat(u[p], u[p]), 'nn') for p in P]
    y = [base[p][C:] + unstack(t[p][2 * C:]) + unstack(w2[p]) for p in P]
    S1 = [g[p][C - 1:C, :] * (S0[p] + bd * _bdot(cat(v[p], u[p]), cat(kb[p], bb[p]), 'tn')) for p in P]
    return y, S1


def _pairs(x):
    return [x[:, 2 * HEAD * p:2 * HEAD * (p + 1)] for p in range(HEADS // 2)]


def _wkv_consts():
    tri = jnp.tril(jnp.ones((WKV_C, WKV_C), f32))
    hid = jnp.arange(2 * HEAD) // HEAD
    return tri, (hid[:, None] == hid[None, :]).astype(f32)


def _wkv7_fwd(r, lw, k, v, a, b):
    L = r.shape[0]
    nc, npair = L // WKV_C, HEADS // 2

    def body(r_ref, lw_ref, k_ref, v_ref, a_ref, b_ref, tri_ref, bd_ref, y_ref, ck_ref, s_ref):
        @pl.when(pl.program_id(0) == 0)
        def _():
            s_ref[...] = jnp.zeros_like(s_ref)

        s0 = [s_ref[p] for p in range(npair)]
        for p in range(npair):
            ck_ref[0, p] = s0[p]
        y, s1 = _wkv_chunk(s0, *(_pairs(x) for x in (r_ref, lw_ref, k_ref, v_ref, a_ref, b_ref)), tri_ref[...], bd_ref[...])
        for p in range(npair):
            y_ref[:, 2 * HEAD * p:2 * HEAD * (p + 1)] = y[p]
            s_ref[p] = s1[p]

    row = pl.BlockSpec((WKV_C, RWKV_W), lambda c: (c, 0))
    sspec = pl.BlockSpec((1, npair, 2 * HEAD, 2 * HEAD), lambda c: (c, 0, 0, 0))
    return pl.pallas_call(
        body, name="wkv7_fwd", grid=(nc,),
        in_specs=[row] * 6 + [pl.BlockSpec((WKV_C, WKV_C), lambda c: (0, 0)), pl.BlockSpec((2 * HEAD, 2 * HEAD), lambda c: (0, 0))],
        out_specs=[row, sspec],
        out_shape=[jax.ShapeDtypeStruct((L, RWKV_W), f32), jax.ShapeDtypeStruct((nc, npair, 2 * HEAD, 2 * HEAD), f32)],
        scratch_shapes=[pltpu.VMEM((npair, 2 * HEAD, 2 * HEAD), f32)],
        compiler_params=pltpu.CompilerParams(dimension_semantics=("arbitrary",), vmem_limit_bytes=VMEM_LIMIT),
    )(r, lw, k, v, a, b, *_wkv_consts())


def _wkv7_bwd(r, lw, k, v, a, b, ck, dy):
    L = r.shape[0]
    nc, npair = L // WKV_C, HEADS // 2

    def body(r_ref, lw_ref, k_ref, v_ref, a_ref, b_ref, ck_ref, dy_ref, tri_ref, bd_ref,
             dr_ref, dlw_ref, dk_ref, dv_ref, da_ref, db_ref, ds_ref):
        @pl.when(pl.program_id(0) == 0)
        def _():
            ds_ref[...] = jnp.zeros_like(ds_ref)

        tri, bd = tri_ref[...], bd_ref[...]
        ins = [[ck_ref[0, p] for p in range(npair)]] + [_pairs(x) for x in (r_ref, lw_ref, k_ref, v_ref, a_ref, b_ref)]
        _, vjp = jax.vjp(lambda *t: _wkv_chunk(*t, tri, bd), *ins)
        gs = vjp((_pairs(dy_ref), [ds_ref[p] for p in range(npair)]))
        for p in range(npair):
            ds_ref[p] = gs[0][p]
            for ref, gval in zip((dr_ref, dlw_ref, dk_ref, dv_ref, da_ref, db_ref), gs[1:]):
                ref[:, 2 * HEAD * p:2 * HEAD * (p + 1)] = gval[p]

    row = pl.BlockSpec((WKV_C, RWKV_W), lambda c: (nc - 1 - c, 0))
    sspec = pl.BlockSpec((1, npair, 2 * HEAD, 2 * HEAD), lambda c: (nc - 1 - c, 0, 0, 0))
    return pl.pallas_call(
        body, name="wkv7_bwd", grid=(nc,),
        in_specs=[row] * 6 + [sspec, row, pl.BlockSpec((WKV_C, WKV_C), lambda c: (0, 0)),
                              pl.BlockSpec((2 * HEAD, 2 * HEAD), lambda c: (0, 0))],
        out_specs=[row] * 6,
        out_shape=[jax.ShapeDtypeStruct((L, RWKV_W), f32)] * 6,
        scratch_shapes=[pltpu.VMEM((npair, 2 * HEAD, 2 * HEAD), f32)],
        compiler_params=pltpu.CompilerParams(dimension_semantics=("arbitrary",), vmem_limit_bytes=VMEM_LIMIT),
    )(r, lw, k, v, a, b, ck, dy, *_wkv_consts())


def _cmul(ar, ai, xr, xi):
    return ar * xr - ai * xi, ar * xi + ai * xr


def _scan_init(a_ref, car_ref, pw_ref, reverse):
    car_ref[...] = jnp.zeros_like(car_ref)
    ar = jnp.broadcast_to(a_ref[:, :S5_N], (8, S5_N))
    ai = jnp.broadcast_to(a_ref[:, S5_N:], (8, S5_N))
    if reverse:
        ai = -ai
    row = lax.broadcasted_iota(jnp.int32, (8, S5_N), 0)
    pr, pi = ar, ai
    qr, qi = jnp.zeros((8, S5_N), f32), jnp.zeros((8, S5_N), f32)
    for e in range(1, 9):
        sel = (row == 8 - e) if reverse else (row == e - 1)
        qr, qi = jnp.where(sel, pr, qr), jnp.where(sel, pi, qi)
        if e in (1, 2, 4):
            j = (1, 2, 4).index(e)
            pw_ref[j, :, :S5_N] = pr
            pw_ref[j, :, S5_N:] = pi
        pr, pi = _cmul(pr, pi, ar, ai)
    pw_ref[3, :, :S5_N] = qr
    pw_ref[3, :, S5_N:] = qi


def _scan_tile(x_ref, o_ref, car_ref, pw_ref, reverse):
    ng = x_ref.shape[0] // 8
    row = lax.broadcasted_iota(jnp.int32, (8, S5_N), 0)

    def group(gi, carry):
        g = (ng - 1 - gi) if reverse else gi
        t0 = pl.multiple_of(g * 8, 8)
        xr, xi = x_ref[pl.ds(t0, 8), :S5_N], x_ref[pl.ds(t0, 8), S5_N:]
        for j, d in enumerate((1, 2, 4)):
            if reverse:
                sr = jnp.where(row < 8 - d, pltpu.roll(xr, 8 - d, axis=0), 0.0)
                si = jnp.where(row < 8 - d, pltpu.roll(xi, 8 - d, axis=0), 0.0)
            else:
                sr = jnp.where(row >= d, pltpu.roll(xr, d, axis=0), 0.0)
                si = jnp.where(row >= d, pltpu.roll(xi, d, axis=0), 0.0)
            mr, mi = _cmul(pw_ref[j, :, :S5_N], pw_ref[j, :, S5_N:], sr, si)
            xr, xi = xr + mr, xi + mi
        cr, ci = carry
        mr, mi = _cmul(pw_ref[3, :, :S5_N], pw_ref[3, :, S5_N:], cr, ci)
        xr, xi = xr + mr, xi + mi
        o_ref[pl.ds(t0, 8), :S5_N] = xr
        o_ref[pl.ds(t0, 8), S5_N:] = xi
        e = 0 if reverse else 7
        return (jnp.broadcast_to(xr[e:e + 1, :], (8, S5_N)), jnp.broadcast_to(xi[e:e + 1, :], (8, S5_N)))

    cr, ci = lax.fori_loop(0, ng, group, (car_ref[:, :S5_N], car_ref[:, S5_N:]))
    car_ref[:, :S5_N] = cr
    car_ref[:, S5_N:] = ci


_CB, _SB = 128, 512


def _cblk(k):
    return slice(_CB * k, _CB * (k + 1))


def _sblk(j):
    return slice(_SB * j, _SB * (j + 1))


def _s5_fwd(u, bmat, cmat, abar, late):
    L = u.shape[0]
    nt = L // S5_T
    names = list(late)
    nh = len(names)

    def body(u_ref, b_ref, c_ref, a_ref, *rest):
        h_in, (st_ref, y_ref), h_out = rest[:nh], rest[nh:nh + 2], rest[nh + 2:2 * nh + 2]
        bu_ref, car_ref, pw_ref, ssem, rsem, lsem = rest[2 * nh + 2:]
        i = pl.program_id(0)

        def copies():
            px, py, pc = _mesh_pos()
            me = 2 * px + py
            out = []
            for a, nm in enumerate(names):
                hr = late[nm].shape[0] // 2
                src, dst = h_in[a].at[pl.ds(pl.multiple_of(pc * hr, 16), hr), :], _slab(h_out[a], nm, me, pc)
                out.append(pltpu.make_async_copy(src, dst, lsem.at[a]))
                out += [pltpu.make_async_remote_copy(src, dst, ssem.at[3 * a + k], rsem.at[3 * a + k],
                                                     device_id=(qx, qy, pc), device_id_type=MESH)
                        for k, (qx, qy) in enumerate(_chip_peers(px, py))]
            return out

        @pl.when(i == 0)
        def _():
            _scan_init(a_ref, car_ref, pw_ref, False)
            for cp in copies():
                cp.start()

        for j in range(8):
            bu_ref[:, _sblk(j)] = _raw_bdot(u_ref[:, _cblk(j % 4)], b_ref[j], 'nn')
        _scan_tile(bu_ref, st_ref, car_ref, pw_ref, False)
        for k in range(4):
            y_ref[:, _cblk(k)] = (_raw_bdot(st_ref[:, _sblk(k)], c_ref[k], 'nn')
                                  + _raw_bdot(st_ref[:, _sblk(4 + k)], c_ref[4 + k], 'nn'))

        @pl.when(i == nt - 1)
        def _():
            for cp in copies():
                cp.wait()

    whole = lambda shape: pl.BlockSpec(shape, lambda i: (0,) * len(shape))
    outs = pl.pallas_call(
        body, name="s5_fwd", grid=(nt,),
        in_specs=[pl.BlockSpec((S5_T, S5_W), lambda i: (i, 0)), whole(bmat.shape), whole(cmat.shape), whole(abar.shape)]
        + [ANY] * nh,
        out_specs=[pl.BlockSpec((S5_T, 2 * S5_N), lambda i: (i, 0)), pl.BlockSpec((S5_T, S5_W), lambda i: (i, 0))] + [ANY] * nh,
        out_shape=[jax.ShapeDtypeStruct((L, 2 * S5_N), f32), jax.ShapeDtypeStruct((L, S5_W), f32)]
        + [jax.ShapeDtypeStruct(GATHER[nm][0], late[nm].dtype) for nm in names],
        scratch_shapes=[pltpu.VMEM((S5_T, 2 * S5_N), f32), pltpu.VMEM((8, 2 * S5_N), f32), pltpu.VMEM((4, 8, 2 * S5_N), f32),
                        pltpu.SemaphoreType.DMA((3 * nh,)), pltpu.SemaphoreType.DMA((3 * nh,)), pltpu.SemaphoreType.DMA((nh,))],
        compiler_params=pltpu.CompilerParams(dimension_semantics=("arbitrary",), vmem_limit_bytes=VMEM_LIMIT),
    )(u, bmat, cmat, abar, *[late[nm] for nm in names])
    return outs[0], outs[1], dict(zip(names, outs[2:]))


def _s5_bwd(dy, st, u, du_direct, bmat, cmat, abar, chip_sum):
    L = u.shape[0]
    nt = L // S5_T
    nb8 = S5_T // 8
    names = list(chip_sum)
    nh = len(names)

    def body(dy_ref, st_ref, sp_ref, u_ref, dud_ref, b_ref, c_ref, a_ref, *rest):
        x_in, (du_ref, db_ref, dc_ref, da_ref), x_out = rest[:nh], rest[nh:nh + 4], rest[nh + 4:2 * nh + 4]
        lam_ref, car_ref, pw_ref, ssem, rsem = rest[2 * nh + 4:]
        i = pl.program_id(0)

        @pl.when(i == 0)
        def _():
            _scan_init(a_ref, car_ref, pw_ref, True)
            db_ref[...] = jnp.zeros_like(db_ref)
            dc_ref[...] = jnp.zeros_like(dc_ref)
            da_ref[...] = jnp.zeros_like(da_ref)
            for cp in _exchange_copies(x_in, x_out, ssem, rsem):
                cp.start()

        for j in range(8):
            lam_ref[:, _sblk(j)] = _raw_bdot(dy_ref[:, _cblk(j % 4)], c_ref[j], 'nt')
        _scan_tile(lam_ref, lam_ref, car_ref, pw_ref, True)
        for k in range(4):
            du_ref[:, _cblk(k)] = (dud_ref[:, _cblk(k)] + _raw_bdot(lam_ref[:, _sblk(k)], b_ref[k], 'nt')
                                   + _raw_bdot(lam_ref[:, _sblk(4 + k)], b_ref[4 + k], 'nt')
                                   ).astype(du_ref.dtype)
            sr = _shift_down(st_ref[:, _sblk(k)], sp_ref[:, _sblk(k)], nt - 1 - i, 1)
            si = _shift_down(st_ref[:, _sblk(4 + k)], sp_ref[:, _sblk(4 + k)], nt - 1 - i, 1)
            lr, li = lam_ref[:, _sblk(k)], lam_ref[:, _sblk(4 + k)]
            da_ref[:, _sblk(k)] += _sum0(lr * sr + li * si)
            da_ref[:, _sblk(4 + k)] += _sum0(li * sr - lr * si)
        for j in range(8):
            db_ref[j] += _raw_bdot(u_ref[:, _cblk(j % 4)], lam_ref[:, _sblk(j)], 'tn')
            dc_ref[j] += _raw_bdot(st_ref[:, _sblk(j)], dy_ref[:, _cblk(j % 4)], 'tn')

        @pl.when(i == nt - 1)
        def _():
            for cp in _exchange_copies(x_in, x_out, ssem, rsem):
                cp.wait()

    whole = lambda shape: pl.BlockSpec(shape, lambda i: (0,) * len(shape))
    rev = lambda i: (nt - 1 - i, 0)
    outs = pl.pallas_call(
        body, name="s5_bwd", grid=(nt,),
        in_specs=[pl.BlockSpec((S5_T, S5_W), rev), pl.BlockSpec((S5_T, 2 * S5_N), rev),
                  pl.BlockSpec((8, 2 * S5_N), lambda i: (jnp.maximum((nt - 1 - i) * nb8 - 1, 0), 0)),
                  pl.BlockSpec((S5_T, S5_W), rev), pl.BlockSpec((S5_T, S5_W), rev), whole(bmat.shape), whole(cmat.shape),
                  whole(abar.shape)] + [ANY] * nh,
        out_specs=[pl.BlockSpec((S5_T, S5_W), rev), whole((8, _CB, _SB)), whole((8, _SB, _CB)), whole((1, 2 * S5_N))]
        + [ANY] * nh,
        out_shape=[jax.ShapeDtypeStruct((L, S5_W), bf16), jax.ShapeDtypeStruct((8, _CB, _SB), f32),
                   jax.ShapeDtypeStruct((8, _SB, _CB), f32), jax.ShapeDtypeStruct((1, 2 * S5_N), f32)]
        + [jax.ShapeDtypeStruct(chip_sum[nm].shape, chip_sum[nm].dtype) for nm in names],
        scratch_shapes=[pltpu.VMEM((S5_T, 2 * S5_N), f32), pltpu.VMEM((8, 2 * S5_N), f32), pltpu.VMEM((4, 8, 2 * S5_N), f32),
                        pltpu.SemaphoreType.DMA((3 * nh,)), pltpu.SemaphoreType.DMA((3 * nh,))],
        compiler_params=pltpu.CompilerParams(dimension_semantics=("arbitrary",), vmem_limit_bytes=VMEM_LIMIT),
    )(dy, st, st, u, du_direct, bmat, cmat, abar, *[chip_sum[nm] for nm in names])
    return outs[0], outs[1], outs[2], outs[3], dict(zip(names, outs[4:]))


def _s5_disc_fwd(a_re, a_im, ls, b_re, b_im):
    def body(a_re_ref, a_im_ref, ls_ref, b_re_ref, b_im_ref, ar_ref, ai_ref, br_ref, bi_ref):
        outs = _s5_disc(a_re_ref[...], a_im_ref[...], ls_ref[...], b_re_ref[...], b_im_ref[...])
        for ref, v in zip((ar_ref, ai_ref, br_ref, bi_ref), outs):
            ref[...] = v

    c1, c16 = jax.ShapeDtypeStruct((S5_N, 1), f32), jax.ShapeDtypeStruct((S5_N, S5_C), f32)
    return pl.pallas_call(body, name="s5_disc", out_shape=[c1, c1, c16, c16])(a_re, a_im, ls, b_re, b_im)


def _s5_disc_bwd(a_re, a_im, ls, b_re, b_im, d_ar, d_ai, d_br, d_bi, seg):
    def body(a_re_ref, a_im_ref, ls_ref, b_re_ref, b_im_ref, g1, g2, g3, g4, seg_ref, o1, o2, o3, o4, o5):
        _, vjp = jax.vjp(_s5_disc, a_re_ref[...], a_im_ref[...], ls_ref[...], b_re_ref[...], b_im_ref[...])
        da_re, da_im, dls, db_re, db_im = vjp((g1[...], g2[...], g3[...], g4[...]))
        o1[...] = da_re
        o2[...] = da_im
        o3[...] = _dot32(seg_ref[...], dls)
        o4[...] = db_re
        o5[...] = db_im

    c1, c16 = jax.ShapeDtypeStruct((S5_N, 1), f32), jax.ShapeDtypeStruct((S5_N, S5_C), f32)
    return pl.pallas_call(body, name="s5_disc_bwd", out_shape=[c1, c1, jax.ShapeDtypeStruct((S5_G, 1), f32), c16, c16])(
        a_re, a_im, ls, b_re, b_im, d_ar, d_ai, d_br, d_bi, seg)


ANY = pl.BlockSpec(memory_space=pl.ANY)

GATHER = {'w_in': ((4352, 1024), 0), 'ffn_w_up': ((1024, 5632), 1), 'w_branch_rwkv': ((512, 1024), 1),
          'w_branch_s5': ((512, 1024), 1), 'w_out': ((1024, 1024), 0), 's5_w_glu': ((512, 512), 0),
          'ffn_w_down': ((2816, 1024), 0), 'rwkv_w2': ((64, 512), 1), 'rwkv_a2': ((64, 512), 1),
          'rwkv_g2': ((128, 512), 1), 'ffn_conv_w': ((8, 5632), 1)}
BIG = ['w_in', 'ffn_w_up', 'w_branch_rwkv', 'w_branch_s5', 'w_out', 's5_w_glu', 'ffn_w_down']
TINY = ['rwkv_w2', 'rwkv_a2', 'rwkv_g2', 'ffn_conv_w']
SMALL = [n for n in WEIGHTS if n not in GATHER]
SMALL_ROWS = 320
ADAM_ROWS = 256


def _mo(v, m):
    return v if isinstance(v, int) else pl.multiple_of(v, m)


def _slab(ref, name, j, h=None):
    (R, Cn), axis = GATHER[name]
    if axis == 0:
        rs = R // 4
        if h is None:
            return ref.at[pl.ds(_mo(j * rs, 16), rs), :]
        return ref.at[pl.ds(_mo(j * rs + h * (rs // 2), 8), rs // 2), :]
    cols = pl.ds(_mo(j * (Cn // 4), 128), Cn // 4)
    if h is None:
        return ref.at[:, cols]
    return ref.at[pl.ds(_mo(h * (R // 2), 8), R // 2), cols]


def _half_shape(name):
    (R, Cn), axis = GATHER[name]
    return (R // 8, Cn) if axis == 0 else (R // 2, Cn // 4)


def _chip_peers(px, py):
    return [((1 - px) if (k >> 1) else px, (1 - py) if (k & 1) else py) for k in (1, 2, 3)]


def _run_copies(copies):
    for cp in copies:
        cp.start()
    for cp in copies:
        cp.wait()


def _gather_weights(blocks):
    names = list(blocks)
    n = len(names)

    def body(*refs):
        ins, outs = refs[:n], refs[n:2 * n]
        ssem, rsem, lsem = refs[2 * n:]
        px, py, pc = _mesh_pos()
        me = 2 * px + py
        copies = []
        for i, nm in enumerate(names):
            if nm in BIG:
                hr = blocks[nm].shape[0] // 2
                src, dst = ins[i].at[pl.ds(pl.multiple_of(pc * hr, 16), hr), :], _slab(outs[i], nm, me, pc)
            else:
                src, dst = ins[i], _slab(outs[i], nm, me)
            copies.append(pltpu.make_async_copy(src, dst, lsem.at[i]))
            for k, (qx, qy) in enumerate(_chip_peers(px, py)):
                copies.append(pltpu.make_async_remote_copy(src, dst, ssem.at[3 * i + k], rsem.at[3 * i + k],
                                                           device_id=(qx, qy, pc), device_id_type=MESH))
        _run_copies(copies)

    outs = pl.pallas_call(
        body, name="gather_weights", in_specs=[ANY] * n, out_specs=[ANY] * n,
        out_shape=[jax.ShapeDtypeStruct(GATHER[nm][0], blocks[nm].dtype) for nm in names],
        scratch_shapes=[pltpu.SemaphoreType.DMA((3 * n,)), pltpu.SemaphoreType.DMA((3 * n,)), pltpu.SemaphoreType.DMA((n,))],
    )(*[blocks[nm] for nm in names])
    return dict(zip(names, outs))


def _gather_pair(full, names, call_name):
    n = len(names)

    def body(*refs):
        ins, outs = refs[:n], refs[n:2 * n]
        ssem, rsem = refs[2 * n:]
        px, py, pc = _mesh_pos()
        copies = []
        for i, nm in enumerate(names):
            for j in range(4):
                copies.append(pltpu.make_async_remote_copy(_slab(ins[i], nm, j, pc), _slab(outs[i], nm, j, pc),
                                                           ssem.at[4 * i + j], rsem.at[4 * i + j],
                                                           device_id=(px, py, 1 - pc), device_id_type=MESH))
        _run_copies(copies)

    outs = pl.pallas_call(
        body, name=call_name, in_specs=[ANY] * n, out_specs=[ANY] * n,
        out_shape=[jax.ShapeDtypeStruct(full[nm].shape, full[nm].dtype) for nm in names],
        input_output_aliases={i: i for i in range(n)},
        scratch_shapes=[pltpu.SemaphoreType.DMA((4 * n,)), pltpu.SemaphoreType.DMA((4 * n,))],
    )(*[full[nm] for nm in names])
    return dict(zip(names, outs))


def _grads_to_sibling(G, names, call_name, small=None):
    n = len(names)
    ns = 0 if small is None else 1

    def body(*refs):
        g_refs, o_refs = refs[:n + ns], refs[n + ns:2 * (n + ns)]
        ssem, rsem = refs[2 * (n + ns):]
        px, py, pc = _mesh_pos()
        sib = (px, py, 1 - pc)
        copies = []
        for i, nm in enumerate(names):
            for j in range(4):
                copies.append(pltpu.make_async_remote_copy(_slab(g_refs[i], nm, j, 1 - pc), o_refs[i].at[j],
                                                           ssem.at[4 * i + j], rsem.at[4 * i + j],
                                                           device_id=sib, device_id_type=MESH))
        if ns:
            copies.append(pltpu.make_async_remote_copy(g_refs[n], o_refs[n], ssem.at[4 * n], rsem.at[4 * n],
                                                       device_id=sib, device_id_type=MESH))
        _run_copies(copies)

    outs = pl.pallas_call(
        body, name=call_name, in_specs=[ANY] * (n + ns), out_specs=[ANY] * (n + ns),
        out_shape=[jax.ShapeDtypeStruct((4,) + _half_shape(nm), f32) for nm in names]
        + [jax.ShapeDtypeStruct((SMALL_ROWS, PACK_W), f32)] * ns,
        scratch_shapes=[pltpu.SemaphoreType.DMA((4 * n + ns,)), pltpu.SemaphoreType.DMA((4 * n + ns,))],
    )(*[G[nm] for nm in names], *([small] * ns))
    return dict(zip(names, outs[:n])), (outs[n] if ns else None)


def _pair_add(G, recv, names, call_name, small=None, small_recv=None):
    n = len(names)
    ns = 0 if small is None else 1
    cidx = lax.axis_index("c").astype(jnp.int32).reshape(1)

    def body(c_ref, *refs):
        ins, outs = refs[:2 * (n + ns)], refs[2 * (n + ns):]
        for i in range(n):
            outs[i][...] = (ins[i][...] + ins[n + ns + i][...]).astype(bf16)
        if ns:
            outs[n][...] = ins[n][...] + ins[2 * n + 1][...]

    g_specs, r_specs = [], []
    for nm in names:
        hr, hc = _half_shape(nm)
        if GATHER[nm][1] == 0:
            g_specs.append(pl.BlockSpec((hr // 2, hc), lambda j, i, c: ((2 * j + c[0]) * 2 + i, 0)))
        else:
            g_specs.append(pl.BlockSpec((hr // 2, hc), lambda j, i, c: (2 * c[0] + i, j)))
        r_specs.append(pl.BlockSpec((1, hr // 2, hc), lambda j, i, c: (j, i, 0)))
    sm = [pl.BlockSpec((SMALL_ROWS // 8, PACK_W), lambda j, i, c: (2 * j + i, 0))] * ns
    outs = pl.pallas_call(
        body, name=call_name,
        grid_spec=pltpu.PrefetchScalarGridSpec(num_scalar_prefetch=1, grid=(4, 2), in_specs=g_specs + sm + r_specs + sm,
                                               out_specs=r_specs + sm),
        out_shape=[jax.ShapeDtypeStruct((4,) + _half_shape(nm), bf16) for nm in names]
        + [jax.ShapeDtypeStruct((SMALL_ROWS, PACK_W), f32)] * ns,
        compiler_params=pltpu.CompilerParams(vmem_limit_bytes=VMEM_LIMIT),
    )(cidx, *[G[nm] for nm in names], *([small] * ns), *[recv[nm] for nm in names], *([small_recv] * ns))
    return dict(zip(names, outs[:n])), (outs[n] if ns else None)


def _exchange_copies(ins, outs, ssem, rsem):
    px, py, pc = _mesh_pos()
    me = 2 * px + py
    return [pltpu.make_async_remote_copy(ins[i].at[2 * qx + qy], outs[i].at[me], ssem.at[3 * i + k], rsem.at[3 * i + k],
                                         device_id=(qx, qy, pc), device_id_type=MESH)
            for i in range(len(ins)) for k, (qx, qy) in enumerate(_chip_peers(px, py))]


def _grads_chip_exchange(chip_sum, names, small):
    n = len(names)

    def body(*refs):
        ins, outs = refs[:n + 1], refs[n + 1:2 * n + 2]
        ssem, rsem, ssem_s, rsem_s = refs[2 * n + 2:]
        px, py, pc = _mesh_pos()
        me = 2 * px + py
        copies = _exchange_copies(ins[:n], outs[:n], ssem, rsem)
        hs = SMALL_ROWS // 2
        mine = ins[n].at[pl.ds(pl.multiple_of(pc * hs, 8), hs), :]
        copies += [pltpu.make_async_remote_copy(mine, outs[n].at[me], ssem_s.at[k], rsem_s.at[k],
                                                device_id=(qx, qy, pc), device_id_type=MESH)
                   for k, (qx, qy) in enumerate(_chip_peers(px, py))]
        _run_copies(copies)

    outs = pl.pallas_call(
        body, name="grads_chip_exchange", in_specs=[ANY] * (n + 1), out_specs=[ANY] * (n + 1),
        out_shape=[jax.ShapeDtypeStruct(chip_sum[nm].shape, chip_sum[nm].dtype) for nm in names]
        + [jax.ShapeDtypeStruct((4, SMALL_ROWS // 2, PACK_W), f32)],
        scratch_shapes=[pltpu.SemaphoreType.DMA((3 * n,)), pltpu.SemaphoreType.DMA((3 * n,)),
                        pltpu.SemaphoreType.DMA((3,)), pltpu.SemaphoreType.DMA((3,))],
    )(*[chip_sum[nm] for nm in names], small)
    return dict(zip(names, outs[:n])), outs[n]


def _sum_slots(slots, chip_sum, small4, small_own):
    n = len(BIG)
    me = jnp.stack([2 * lax.axis_index("x") + lax.axis_index("y"), lax.axis_index("c")]).astype(jnp.int32)

    def body(me_ref, *refs):
        for i in range(n + 1):
            own = refs[5 * i + 4][...].astype(f32)
            own = own[0] if i < n else own
            term = [jnp.where(me_ref[0] == k, own, refs[5 * i + k][0].astype(f32)) for k in range(4)]
            refs[5 * (n + 1) + i][...] = ((term[0] + term[1]) + term[2]) + term[3]

    redirect = lambda k: (lambda i, m: (jnp.where(m[0] == k, (k + 1) % 4, k), i, 0))
    in_specs, args, specs_out, shapes = [], [], [], []
    for nm in BIG:
        hr, hc = _half_shape(nm)
        in_specs += [pl.BlockSpec((1, hr // 2, hc), redirect(k)) for k in range(4)]
        in_specs.append(pl.BlockSpec((1, hr // 2, hc), lambda i, m: (m[0], i, 0)))
        args += [slots[nm]] * 4 + [chip_sum[nm]]
        specs_out.append(pl.BlockSpec((hr // 2, hc), lambda i, m: (i, 0)))
        shapes.append(jax.ShapeDtypeStruct((hr, hc), f32))
    in_specs += [pl.BlockSpec((1, SMALL_ROWS // 4, PACK_W), redirect(k)) for k in range(4)]
    in_specs.append(pl.BlockSpec((SMALL_ROWS // 4, PACK_W), lambda i, m: (2 * m[1] + i, 0)))
    args += [small4] * 4 + [small_own]
    specs_out.append(pl.BlockSpec((SMALL_ROWS // 4, PACK_W), lambda i, m: (i, 0)))
    shapes.append(jax.ShapeDtypeStruct((SMALL_ROWS // 2, PACK_W), f32))
    outs = pl.pallas_call(
        body, name="grads_chip_sum",
        grid_spec=pltpu.PrefetchScalarGridSpec(num_scalar_prefetch=1, grid=(2,), in_specs=in_specs, out_specs=specs_out),
        out_shape=shapes, compiler_params=pltpu.CompilerParams(vmem_limit_bytes=VMEM_LIMIT),
    )(me, *args)
    return dict(zip(BIG, outs[:n])), outs[n]


def _halves_to_sibling(half):
    names = list(half)
    n = len(names)

    def body(*refs):
        ins, outs = refs[:n], refs[n:2 * n]
        ssem, rsem = refs[2 * n:]
        px, py, pc = _mesh_pos()
        _run_copies([pltpu.make_async_remote_copy(ins[i], outs[i], ssem.at[i], rsem.at[i],
                                                  device_id=(px, py, 1 - pc), device_id_type=MESH) for i in range(n)])

    outs = pl.pallas_call(
        body, name="grads_halves_to_sibling", in_specs=[ANY] * n, out_specs=[ANY] * n,
        out_shape=[jax.ShapeDtypeStruct(half[nm].shape, f32) for nm in names],
        scratch_shapes=[pltpu.SemaphoreType.DMA((n,)), pltpu.SemaphoreType.DMA((n,))],
    )(*[half[nm] for nm in names])
    return dict(zip(names, outs))


def _join_halves(mine, other, pc):
    hr = mine.shape[0]
    return lax.dynamic_slice_in_dim(jnp.concatenate([other, mine, other], axis=0), (1 - pc) * hr, 2 * hr, axis=0)


def _flat_pad(v):
    v = v.reshape(-1)
    return jnp.pad(v, (0, _ceil_to(v.shape[0], PACK_W) - v.shape[0]))


def _pack_rows(parts, rows):
    flat = jnp.concatenate([_flat_pad(p) for p in parts])
    return jnp.pad(flat, (0, rows * PACK_W - flat.shape[0])).reshape(rows, PACK_W)


def _unpack_rows(buf, shapes):
    flat = buf.reshape(-1)
    out, off = [], 0
    for shp in shapes:
        n = 1
        for d in shp:
            n *= d
        out.append(flat[off:off + n].reshape(shp))
        off += _ceil_to(n, PACK_W)
    return out


def _adamw_math(w_, g_, m_, v_):
    m2 = ADAM_B1 * m_ + (1.0 - ADAM_B1) * g_
    v2 = ADAM_B2 * v_ + (1.0 - ADAM_B2) * (g_ * g_)
    m_hat = m2 / (1.0 - ADAM_B1 ** ADAM_STEP)
    v_hat = v2 / (1.0 - ADAM_B2 ** ADAM_STEP)
    return -ADAM_LR * (m_hat / (jnp.sqrt(v_hat) + ADAM_EPS) + ADAM_WD * w_), m2, v2


def _adamw(groups):
    ng = len(groups)

    def body(*refs):
        ins, outs = refs[:4 * ng], refs[4 * ng:]
        for i in range(ng):
            res = _adamw_math(*(r[...] for r in ins[4 * i:4 * i + 4]))
            for ref, val in zip(outs[3 * i:3 * i + 3], res):
                ref[...] = val

    in_specs, out_specs, out_shape = [], [], []
    for grp in groups:
        R, Cn = grp[0].shape
        spec = pl.BlockSpec((R // 8, Cn), lambda i: (i, 0))
        in_specs += [spec] * 4
        out_specs += [spec] * 3
        out_shape += [jax.ShapeDtypeStruct((R, Cn), f32)] * 3
    outs = pl.pallas_call(
        body, name="adamw", grid=(8,), in_specs=in_specs, out_specs=out_specs, out_shape=out_shape,
        compiler_params=pltpu.CompilerParams(vmem_limit_bytes=VMEM_LIMIT),
    )(*[a for grp in groups for a in grp])
    return [tuple(outs[3 * i:3 * i + 3]) for i in range(ng)]


def _forward_backward(x, tgt, W, S, late):
    L = x.shape[0]
    TM, TMW, TS = 256, 128, 512
    row = lambda c, dt=f32: (c, dt)
    hid = jnp.arange(RWKV_W) // HEAD
    E = (hid[:, None] == hid[None, :]).astype(f32)
    seg = (jnp.arange(S5_N)[None, :] // S5_P == jnp.arange(S5_G)[:, None]).astype(f32)

    w_in_t = W['w_in']
    w_p, w_u, w_g = w_in_t[:N_RWKV], w_in_t[N_RWKV:N_RWKV + S5_W], w_in_t[N_RWKV + S5_W:]
    zpad = jnp.zeros((64, RWKV_W), f32)
    w2p = jnp.concatenate([W['rwkv_w2'], zpad], axis=0)
    a2p = jnp.concatenate([zpad, W['rwkv_a2']], axis=0)
    g2 = W['rwkv_g2']
    prep_consts = [S['rwkv_shift_mu'], S['rwkv_w0'], S['rwkv_a0'], S['rwkv_k_k'], S['rwkv_k_a'], w2p, a2p, g2, E]
    out_consts = [S['rwkv_lnx_w'], S['rwkv_lnx_b'], S['rwkv_r_k'], E]
    cw, cb = W['ffn_conv_w'][:3], S['ffn_conv_b']

    a_re, a_im = S['s5_a_re'].reshape(S5_N, 1), S['s5_a_im'].reshape(S5_N, 1)
    ls = jnp.repeat(S['s5_log_step'].reshape(S5_G, 1), S5_P, axis=0)
    b_re, b_im = S['s5_b_re'].reshape(S5_N, S5_C), S['s5_b_im'].reshape(S5_N, S5_C)
    ar, ai, bbr, bbi = _s5_disc_fwd(a_re, a_im, ls, b_re, b_im)
    abar = jnp.concatenate([ar.reshape(1, S5_N), ai.reshape(1, S5_N)], axis=1)
    eye8 = jnp.eye(8, dtype=f32)

    def blocks_in(bb):
        t = bb.reshape(4, 8, S5_P, S5_C).transpose(0, 1, 3, 2)
        return (t[:, :, :, None, :] * eye8[None, :, None, :, None]).reshape(4, _CB, _SB)

    def blocks_out(cc):
        t = cc.reshape(4, 8, S5_C, S5_P).transpose(0, 1, 3, 2)
        return (t[:, :, :, None, :] * eye8[None, :, None, :, None]).reshape(4, _SB, _CB)

    def undiag_in(blocks):
        t = blocks.reshape(4, 8, S5_C, 8, S5_P)
        t = jnp.sum(t * eye8[None, :, None, :, None], axis=3)
        return t.reshape(S5_G, S5_C, S5_P).transpose(0, 2, 1).reshape(S5_N, S5_C)

    def undiag_out(blocks):
        t = blocks.reshape(4, 8, S5_P, 8, S5_C)
        t = jnp.sum(t * eye8[None, :, None, :, None], axis=3)
        return t.reshape(S5_G, S5_P, S5_C).transpose(0, 2, 1)

    bmat = jnp.concatenate([blocks_in(bbr), blocks_in(bbi)], axis=0).astype(bf16)
    cmat = jnp.concatenate([blocks_out(S['s5_c_re'].reshape(S5_G, S5_C, S5_P)),
                            -blocks_out(S['s5_c_im'].reshape(S5_G, S5_C, S5_P))], axis=0).astype(bf16)

    g1, g2n, g3, g4 = S['norm_mix_pre'], S['norm_mix_post'], S['norm_ffn_pre'], S['norm_ffn_post']
    (h1,) = _rowcall("norm_pre", lambda i, n, R, P, X, C: ((_rms(R[0], C[0]),), ()), L, TS, [x], [g1],
                     out_rows=[row(D_MODEL, bf16)])
    p = _mm(h1, w_p, 'nt', "mm_p")
    u = _mm(h1, w_u, 'nt', "mm_u")
    gp = _mm(h1, w_g, 'nt', "mm_g")

    def prep_fn(i, n, R, P, X, C):
        q = R[0] + (_shift_down(R[0], P[0], i, 1) - R[0]) * C[0]
        return _prep(q, *C[1:]), ()

    r, lw, k2, v, an, bv, g = _rowcall("rwkv_prep", prep_fn, L, TS, [p], prep_consts,
                                       out_rows=[row(RWKV_W)] * 7, prev=[0])
    y, ck = _wkv7_fwd(r, lw, k2, v, an, bv)
    (o_a,) = _rowcall("rwkv_out", lambda i, n, R, P, X, C: ((_rwkv_out(*R, *C),), ()), L, TS, [y, r, k2, v, g],
                      out_consts, out_rows=[row(RWKV_W, bf16)])
    o_r = _mm(o_a, W['w_branch_rwkv'], 'nn', "mm_br")

    st, ysc, got = _s5_fwd(u, bmat, cmat, abar, late)
    W = {**W, **_gather_pair(got, list(got), "gather_weights_pair_late")}
    (yg,) = _rowcall("s5_mid", lambda i, n, R, P, X, C: ((_s5_mid(*R, *C),), ()), L, TS, [ysc, u], [S['s5_d']],
                     out_rows=[row(S5_W)])
    z2 = _mm(yg, W['s5_w_glu'], 'nn', "mm_glu")
    (o_b,) = _rowcall("s5_glu", lambda i, n, R, P, X, C: ((_s5_glu(*R, *C),), ()), L, TS, [yg, z2], [S['s5_b_glu']],
                      out_rows=[row(S5_W, bf16)])
    o_s = _mm(o_b, W['w_branch_s5'], 'nn', "mm_bs")

    (merged,) = _rowcall("merge", lambda i, n, R, P, X, C: ((_merge(*R, *C),), ()), L, TS, [gp, o_r, o_s],
                         [S['b_gate']], out_rows=[row(D_MODEL, bf16)])
    mixed = _mm(merged, W['w_out'], 'nn', "mm_out")

    def resid_fn(i, n, R, P, X, C):
        x1_ = R[0] + _rms(R[1], C[0])
        return (x1_, _rms(x1_, C[1])), ()

    x1, h2 = _rowcall("resid_norm", resid_fn, L, TS, [x, mixed], [g2n, g3], out_rows=[row(D_MODEL), row(D_MODEL, bf16)])

    z = _mm(h2, W['ffn_w_up'], 'nn', "mm_up")

    def conv(zt, zprev, i, cw_, cb_):
        z2s, z1s = _shift_down(zt, zprev, i, 2), _shift_down(zt, zprev, i, 1)
        return cb_ + cw_[0:1] * z2s + cw_[1:2] * z1s + cw_[2:3] * zt, z2s, z1s

    (act,) = _rowcall("conv_act", lambda i, n, R, P, X, C: ((_act(conv(R[0], P[0], i, C[0], C[1])[0]),), ()), L, TMW,
                      [z], [cw, cb], out_rows=[row(D_FF, bf16)], prev=[0])
    f = _mm(act, W['ffn_w_down'], 'nn', "mm_down")

    def final_fn(i, n, R, P, X, C):
        x1_, f_, t_ = R
        fn_, vjp = jax.vjp(_rms, f_, C[0])
        diff = x1_ + fn_ - t_
        loss = jnp.sum(diff * diff) * (0.5 / D_MODEL)
        dx2_ = diff * (1.0 / D_MODEL)
        df_, dg4_ = vjp(dx2_)
        return (df_, dx2_), (jnp.full((1, PACK_W), loss, f32), dg4_)

    df, dx2, loss, dg4 = _rowcall("loss_head", final_fn, L, TS, [x1, f, tgt], [g4],
                                  out_rows=[row(D_MODEL, bf16), row(D_MODEL)], out_accs=[(1, PACK_W), (1, D_MODEL)])
    G = {'norm_ffn_post': dg4}

    dact = _mm(df, W['ffn_w_down'], 'nt', "mm_down_dx")
    G['ffn_w_down'] = _mm(act, df, 'tn', "mm_down_dw")

    def conv_bwd_fn(i, n, R, P, X, C):
        z_, dact_ = R
        cw_, cb_ = C
        zc, z2s, z1s = conv(z_, P[0], i, cw_, cb_)
        _, vjp = jax.vjp(_act, zc)
        (dzc_,) = vjp(dact_)
        last8 = z_[z_.shape[0] - 8:]
        zcn = cb_ + cw_[0:1] * _shift_down(X[0], last8, 1, 2) + cw_[1:2] * _shift_down(X[0], last8, 1, 1) + cw_[2:3] * X[0]
        _, vjpn = jax.vjp(_act, zcn)
        (dzcn,) = vjpn(X[1])
        dz_ = (cw_[2:3] * dzc_ + cw_[1:2] * _shift_up(dzc_, dzcn, i, n, 1) + cw_[0:1] * _shift_up(dzc_, dzcn, i, n, 2))
        return (dz_,), (_sum0(dzc_), _sum0(dzc_ * z2s), _sum0(dzc_ * z1s), _sum0(dzc_ * z_))

    wide = (1, 2 * D_FF)
    dz, dcb, dcw0, dcw1, dcw2 = _rowcall("conv_act_bwd", conv_bwd_fn, L, TMW, [z, dact], [cw, cb],
                                         out_rows=[row(2 * D_FF, bf16)], out_accs=[wide] * 4, prev=[0], nxt=[0, 1])
    G['ffn_conv_b'] = dcb
    G['ffn_conv_w'] = jnp.concatenate([dcw0, dcw1, dcw2], axis=0)
    dh2 = _mm(dz, W['ffn_w_up'], 'nt', "mm_up_dx")
    G['ffn_w_up'] = _mm(h2, dz, 'tn', "mm_up_dw")

    def norm2_bwd_fn(i, n, R, P, X, C):
        x1_, mixed_, dx2_, dh2_ = R
        _, vjp3 = jax.vjp(_rms, x1_, C[1])
        dx1a, dg3_ = vjp3(dh2_)
        dx1_ = dx2_ + dx1a
        _, vjp2 = jax.vjp(_rms, mixed_, C[0])
        dmixed_, dg2_ = vjp2(dx1_)
        return (dx1_, dmixed_), (dg2_, dg3_)

    dx1, dmixed, dg2n, dg3 = _rowcall("norm_mid_bwd", norm2_bwd_fn, L, TS, [x1, mixed, dx2, dh2], [g2n, g3],
                                      out_rows=[row(D_MODEL), row(D_MODEL, bf16)], out_accs=[(1, D_MODEL)] * 2)
    G['norm_mix_post'], G['norm_ffn_pre'] = dg2n, dg3

    dmerged = _mm(dmixed, W['w_out'], 'nt', "mm_out_dx")
    G['w_out'] = _mm(merged, dmixed, 'tn', "mm_out_dw")

    def merge_bwd_fn(i, n, R, P, X, C):
        _, vjp = jax.vjp(_merge, R[0], R[1], R[2], C[0])
        dgp_, do_r_, do_s_, dbg_ = vjp(R[3])
        return (dgp_, do_r_, do_s_), (dbg_,)

    dgp, do_r, do_s, G['b_gate'] = _rowcall("merge_bwd", merge_bwd_fn, L, TS, [gp, o_r, o_s, dmerged], [S['b_gate']],
                                            out_rows=[row(2 * D_MODEL, bf16), row(D_MODEL, bf16), row(D_MODEL, bf16)],
                                            out_accs=[(1, 2 * D_MODEL)])
    do_a = _mm(do_r, W['w_branch_rwkv'], 'nt', "mm_br_dx")
    G['w_branch_rwkv'] = _mm(o_a, do_r, 'tn', "mm_br_dw")
    do_b = _mm(do_s, W['w_branch_s5'], 'nt', "mm_bs_dx")
    G['w_branch_s5'] = _mm(o_b, do_s, 'tn', "mm_bs_dw")

    def glu_bwd_fn(i, n, R, P, X, C):
        _, vjp = jax.vjp(_s5_glu, R[0], R[1], C[0])
        dyg1_, dz2_, dbg_ = vjp(R[2])
        return (dyg1_, dz2_), (dbg_,)

    dyg1, dz2, G['s5_b_glu'] = _rowcall("s5_glu_bwd", glu_bwd_fn, L, TS, [yg, z2, do_b], [S['s5_b_glu']],
                                        out_rows=[row(S5_W), row(S5_W, bf16)], out_accs=[(1, S5_W)])
    dyg2 = _mm(dz2, W['s5_w_glu'], 'nt', "mm_glu_dx")
    G['s5_w_glu'] = _mm(yg, dz2, 'tn', "mm_glu_dw")

    def mid_bwd_fn(i, n, R, P, X, C):
        _, vjp = jax.vjp(_s5_mid, R[0], R[1], C[0])
        dysc_, du_, dd_ = vjp(R[2] + R[3])
        return (dysc_, du_), (dd_,)

    dysc, du1, G['s5_d'] = _rowcall("s5_mid_bwd", mid_bwd_fn, L, TS, [ysc, u, dyg1, dyg2], [S['s5_d']],
                                    out_rows=[row(S5_W, bf16), row(S5_W)], out_accs=[(1, S5_W)])
    early = [n for n in BIG if n != 'w_in']
    recv_e, _ = _grads_to_sibling(G, early, "grads_to_sibling_early")
    chip_e, _ = _pair_add(G, recv_e, early, "grads_pair_sum_early")
    du, dbmat, dcmat, dabar, slots_e = _s5_bwd(dysc, st, u, du1, bmat, cmat, abar, chip_e)
    da_re, da_im, dls, db_re, db_im = _s5_disc_bwd(
        a_re, a_im, ls, b_re, b_im, dabar[:, :S5_N].reshape(S5_N, 1), dabar[:, S5_N:].reshape(S5_N, 1),
        undiag_in(dbmat[:4]), undiag_in(dbmat[4:]), seg)
    G['s5_a_re'], G['s5_a_im'], G['s5_log_step'] = da_re, da_im, dls
    G['s5_b_re'], G['s5_b_im'] = db_re, db_im
    G['s5_c_re'], G['s5_c_im'] = undiag_out(dcmat[:4]), -undiag_out(dcmat[4:])

    def out_bwd_fn(i, n, R, P, X, C):
        _, vjp = jax.vjp(_rwkv_out, *R[:5], *C)
        gs = vjp(R[5])
        return gs[:5], gs[5:8]

    dy, dr1, dk1, dv1, dg, dlw, dlb, drk = _rowcall("rwkv_out_bwd", out_bwd_fn, L, TM, [y, r, k2, v, g, do_a], out_consts,
                                                    out_rows=[row(RWKV_W)] * 5, out_accs=[(1, RWKV_W)] * 3)
    G['rwkv_lnx_w'], G['rwkv_lnx_b'], G['rwkv_r_k'] = dlw, dlb, drk
    dr2, dlwk, dk2b, dv2, dan, dbv = _wkv7_bwd(r, lw, k2, v, an, bv, ck, dy)

    def prep_bwd_fn(i, n, R, P, X, C):
        p_ = R[0]
        d1 = _shift_down(p_, P[0], i, 1) - p_
        q = p_ + d1 * C[0]
        _, vjp = jax.vjp(_prep, q, *C[1:])
        cots = (R[1] + R[2], R[3], R[4] + R[5], R[6] + R[7], R[8], R[9], R[10])
        gs = vjp(cots)
        return (gs[0],), (_sum0(gs[0] * d1),) + tuple(gs[1:8])

    small, lowr = (1, RWKV_W), (128, RWKV_W)
    dq, dmu, dw0, da0, dkk, dka, dw2p, da2p, dg2 = _rowcall(
        "rwkv_prep_bwd", prep_bwd_fn, L, TM, [p, dr1, dr2, dlwk, dk1, dk2b, dv1, dv2, dan, dbv, dg],
        prep_consts, out_rows=[row(N_RWKV)], out_accs=[(1, N_RWKV)] + [small] * 4 + [lowr] * 3, prev=[0])
    G['rwkv_shift_mu'], G['rwkv_w0'], G['rwkv_a0'], G['rwkv_k_k'], G['rwkv_k_a'] = dmu, dw0, da0, dkk, dka
    G['rwkv_w2'], G['rwkv_a2'], G['rwkv_g2'] = dw2p[:64], da2p[64:], dg2

    def shift_bwd_fn(i, n, R, P, X, C):
        dm = R[0] * C[0]
        return (R[0] - dm + _shift_up(dm, X[0] * C[0], i, n, 1),), ()

    (dp,) = _rowcall("shift_bwd", shift_bwd_fn, L, TS, [dq], [S['rwkv_shift_mu']], out_rows=[row(N_RWKV, bf16)], nxt=[0])

    dproj = jnp.concatenate([dp, du, dgp], axis=1)
    dh1 = _mm(dproj, w_in_t, 'nn', "mm_in_dx")
    G['w_in'] = _mm(dproj, h1, 'tn', "mm_in_dw")

    def norm1_bwd_fn(i, n, R, P, X, C):
        _, vjp = jax.vjp(_rms, R[0], C[0])
        dxa, dg1_ = vjp(R[2])
        return (R[1] + dxa,), (dg1_,)

    dx, G['norm_mix_pre'] = _rowcall("norm_pre_bwd", norm1_bwd_fn, L, TS, [x, dx1, dh1], [g1],
                                     out_rows=[row(D_MODEL)], out_accs=[(1, D_MODEL)])
    return loss, dx, G, chip_e, slots_e


def kernel(x, norm_mix_pre, norm_mix_post, norm_ffn_pre, norm_ffn_post, w_in, b_gate, rwkv_shift_mu, rwkv_w0, rwkv_w2, rwkv_a0, rwkv_a2, rwkv_g2, rwkv_k_k, rwkv_k_a, rwkv_r_k, rwkv_lnx_w, rwkv_lnx_b, s5_a_re, s5_a_im, s5_b_re, s5_b_im, s5_c_re, s5_c_im, s5_d, s5_log_step, s5_w_glu, s5_b_glu, w_branch_rwkv, w_branch_s5, w_out, ffn_w_up, ffn_conv_w, ffn_conv_b, ffn_w_down, loss_target, m_norm_mix_pre, m_norm_mix_post, m_norm_ffn_pre, m_norm_ffn_post, m_w_in, m_b_gate, m_rwkv_shift_mu, m_rwkv_w0, m_rwkv_w2, m_rwkv_a0, m_rwkv_a2, m_rwkv_g2, m_rwkv_k_k, m_rwkv_k_a, m_rwkv_r_k, m_rwkv_lnx_w, m_rwkv_lnx_b, m_s5_a_re, m_s5_a_im, m_s5_b_re, m_s5_b_im, m_s5_c_re, m_s5_c_im, m_s5_d, m_s5_log_step, m_s5_w_glu, m_s5_b_glu, m_w_branch_rwkv, m_w_branch_s5, m_w_out, m_ffn_w_up, m_ffn_conv_w, m_ffn_conv_b, m_ffn_w_down, v_norm_mix_pre, v_norm_mix_post, v_norm_ffn_pre, v_norm_ffn_post, v_w_in, v_b_gate, v_rwkv_shift_mu, v_rwkv_w0, v_rwkv_w2, v_rwkv_a0, v_rwkv_a2, v_rwkv_g2, v_rwkv_k_k, v_rwkv_k_a, v_rwkv_r_k, v_rwkv_lnx_w, v_rwkv_lnx_b, v_s5_a_re, v_s5_a_im, v_s5_b_re, v_s5_b_im, v_s5_c_re, v_s5_c_im, v_s5_d, v_s5_log_step, v_s5_w_glu, v_s5_b_glu, v_w_branch_rwkv, v_w_branch_s5, v_w_out, v_ffn_w_up, v_ffn_conv_w, v_ffn_conv_b, v_ffn_w_down):
    A = dict(locals())
    me = 2 * lax.axis_index("x") + lax.axis_index("y")
    blk = lambda n: A[n][0]

    mine = {n: (blk(n).T if n == 'w_in' else blk(n)).astype(bf16) for n in BIG}
    mine.update({n: blk(n) for n in TINY})
    mine['ffn_conv_w'] = jnp.pad(blk('ffn_conv_w'), ((0, 5), (0, 0)))
    late = ['ffn_w_up', 'ffn_w_down']
    W = _gather_weights({n: blkv for n, blkv in mine.items() if n not in late})
    W.update(_gather_pair(W, [n for n in BIG if n not in late], "gather_weights_pair"))
    S = {n: A[n].reshape(1, -1) for n in SMALL}

    loss, dx, G, chip_e, slots_e = _forward_backward(x[0], loss_target[0], W, S, {n: mine[n] for n in late})

    tiny_shapes = [G[n].shape for n in TINY]
    small_buf = _pack_rows([G[n] for n in SMALL] + [G[n] for n in TINY] + [loss], SMALL_ROWS)
    recv, small_recv = _grads_to_sibling(G, ['w_in'], "grads_to_sibling", small_buf)
    chip_l, small_sum = _pair_add(G, recv, ['w_in'], "grads_pair_sum", small_buf, small_recv)
    slots_l, small4 = _grads_chip_exchange(chip_l, ['w_in'], small_sum)
    half, half['small'] = _sum_slots({**slots_e, **slots_l}, {**chip_e, **chip_l}, small4, small_sum)
    other = _halves_to_sibling(half)
    pc = lax.axis_index("c")
    small_tot = _join_halves(half['small'], other['small'], pc)
    grad = {n: _join_halves(half[n], other[n], pc) for n in BIG}
    grad['w_in'] = grad['w_in'].T
    vals = _unpack_rows(small_tot, [A[n].shape for n in SMALL] + tiny_shapes + [(1, PACK_W)])
    grad.update(zip(SMALL, vals))
    for n, full in zip(TINY, vals[len(SMALL):]):
        cs = A[n].shape[2]
        grad[n] = lax.dynamic_slice_in_dim(full, me * cs, cs, axis=1)
    loss_out = vals[-1][0, 0]

    packed = SMALL + TINY
    groups = [(blk(n), grad[n], blk('m_' + n), blk('v_' + n)) for n in BIG]
    groups.append(tuple(_pack_rows([src(n) for n in packed], ADAM_ROWS)
                        for src in (lambda n: A[n], lambda n: grad[n], lambda n: A['m_' + n], lambda n: A['v_' + n])))
    res = _adamw(groups)
    outs = [dict(), dict(), dict()]
    for n, r3 in zip(BIG, res[:-1]):
        for d, val in zip(outs, r3):
            d[n] = val
    for d, buf in zip(outs, res[-1]):
        d.update(zip(packed, _unpack_rows(buf, [A[n].shape for n in packed])))
    full = lambda d: [d[n].reshape(A[n].shape) for n in WEIGHTS]
    return (loss_out, dx[None], *full(grad), *full(outs[0]), *full(outs[1]), *full(outs[2]))
```

```python
import functools

import jax
import jax.numpy as jnp
from jax import lax
from jax.experimental import pallas as pl
from jax.experimental.pallas import tpu as pltpu

f32, bf16 = jnp.float32, jnp.bfloat16
MESH = pl.DeviceIdType.MESH

D_MODEL = 1024
RWKV_W = 512
HEADS, HEAD = 8, 64
N_RWKV = 1792
S5_W = 512
S5_G, S5_P, S5_C = 32, 64, 16
S5_N = S5_G * S5_P
D_FF = 2816
NORM_EPS = 1e-6
LNX_EPS = 64e-5
ADAM_LR, ADAM_B1, ADAM_B2, ADAM_EPS, ADAM_WD, ADAM_STEP = 0.001, 0.9, 0.999, 1e-08, 0.01, 10

VMEM_LIMIT = 48 * 1024 * 1024
PACK_W = 1024
WKV_C = 64
S5_T = 256

WEIGHTS = ['norm_mix_pre', 'norm_mix_post', 'norm_ffn_pre', 'norm_ffn_post', 'w_in', 'b_gate', 'rwkv_shift_mu',
           'rwkv_w0', 'rwkv_w2', 'rwkv_a0', 'rwkv_a2', 'rwkv_g2', 'rwkv_k_k', 'rwkv_k_a', 'rwkv_r_k', 'rwkv_lnx_w',
           'rwkv_lnx_b', 's5_a_re', 's5_a_im', 's5_b_re', 's5_b_im', 's5_c_re', 's5_c_im', 's5_d', 's5_log_step',
           's5_w_glu', 's5_b_glu', 'w_branch_rwkv', 'w_branch_s5', 'w_out', 'ffn_w_up', 'ffn_conv_w', 'ffn_conv_b',
           'ffn_w_down']


def _ceil_to(n, m):
    return -(-n // m) * m


def _mesh_pos():
    return lax.axis_index("x"), lax.axis_index("y"), lax.axis_index("c")


def _pick(d, cap=4096):
    for c in (1024, 1408, 2176, 896, 512, 256, 128):
        if c <= cap and d % c == 0:
            return c
    raise ValueError(d)


def _mm_resident(a, w, mode, name, M, N, K, out_dtype):
    budget = 40 * 1024 * 1024 - 2 * K * N
    tm = next(t for t in (512, 256, 128) if 2 * t * (K * a.dtype.itemsize + 4 * N) <= budget)
    dims = _DIMS[mode]

    def body(a_ref, w_ref, o_ref):
        o_ref[...] = lax.dot_general(a_ref[...].astype(bf16), w_ref[...], (dims, ((), ())),
                                     preferred_element_type=f32).astype(o_ref.dtype)

    return pl.pallas_call(
        body, name=name, grid=(M // tm,),
        in_specs=[pl.BlockSpec((tm, K), lambda i: (i, 0)),
                  pl.BlockSpec(w.shape, lambda i: (0, 0), pipeline_mode=pl.Buffered(1))],
        out_specs=pl.BlockSpec((tm, N), lambda i: (i, 0)), out_shape=jax.ShapeDtypeStruct((M, N), out_dtype),
        compiler_params=pltpu.CompilerParams(dimension_semantics=("parallel",), vmem_limit_bytes=VMEM_LIMIT),
    )(a, w)


def _mm(a, b, mode, name, out_dtype=f32):
    if mode == 'tn':
        (K, M), (K2, N) = a.shape, b.shape
    elif mode == 'nt':
        (M, K), (N, K2) = a.shape, b.shape
    else:
        (M, K), (K2, N) = a.shape, b.shape
    assert K == K2, (name, a.shape, b.shape)
    if mode != 'tn' and b.dtype == bf16:
        return _mm_resident(a, b, mode, name, M, N, K, out_dtype)
    if mode == 'tn':
        tm = _pick(M, 2176)
        tn = _pick(N, 512 if tm > 1408 else (1024 if tm > 1024 else 1408))
        tk = _pick(K, 1024 if a.dtype == bf16 and b.dtype == bf16 else 512)
    else:
        tm, tn, tk = _pick(M, 512), _pick(N), _pick(K)
    nk = K // tk
    dims = {'nn': ((1,), (0,)), 'nt': ((1,), (1,)), 'tn': ((0,), (0,))}[mode]

    def body(a_ref, b_ref, o_ref, acc_ref):
        k = pl.program_id(2)

        @pl.when(k == 0)
        def _():
            acc_ref[...] = jnp.zeros_like(acc_ref)

        acc_ref[...] += lax.dot_general(a_ref[...].astype(bf16), b_ref[...].astype(bf16), (dims, ((), ())),
                                        preferred_element_type=f32)

        @pl.when(k == nk - 1)
        def _():
            o_ref[...] = acc_ref[...].astype(o_ref.dtype)

    a_spec = pl.BlockSpec((tk, tm), lambda i, j, k: (k, i)) if mode == 'tn' else pl.BlockSpec((tm, tk), lambda i, j, k: (i, k))
    b_spec = pl.BlockSpec((tn, tk), lambda i, j, k: (j, k)) if mode == 'nt' else pl.BlockSpec((tk, tn), lambda i, j, k: (k, j))
    return pl.pallas_call(
        body, name=name, grid=(M // tm, N // tn, nk),
        in_specs=[a_spec, b_spec], out_specs=pl.BlockSpec((tm, tn), lambda i, j, k: (i, j)),
        out_shape=jax.ShapeDtypeStruct((M, N), out_dtype),
        scratch_shapes=[pltpu.VMEM((tm, tn), f32)],
        compiler_params=pltpu.CompilerParams(dimension_semantics=("parallel", "parallel", "arbitrary"),
                                             vmem_limit_bytes=VMEM_LIMIT),
    )(a, b)


def _rowcall(name, fn, L, tm, rows, consts=(), out_rows=(), out_accs=(), prev=(), nxt=()):
    nsteps = L // tm
    nb8 = tm // 8
    last8 = L // 8 - 1
    n_r, n_p, n_x, n_c, n_or = len(rows), len(prev), len(nxt), len(consts), len(out_rows)

    def body(*refs):
        i = pl.program_id(0)
        vals = [r[...] for r in refs[:n_r + n_p + n_x + n_c]]
        R, P = vals[:n_r], vals[n_r:n_r + n_p]
        X, C = vals[n_r + n_p:n_r + n_p + n_x], vals[n_r + n_p + n_x:]
        o_refs = refs[n_r + n_p + n_x + n_c:]
        outs_r, outs_a = fn(i, nsteps, R, P, X, C)
        for ref, v in zip(o_refs[:n_or], outs_r, strict=True):
            ref[...] = v.astype(ref.dtype)
        if out_accs:
            @pl.when(i == 0)
            def _():
                for ref in o_refs[n_or:]:
                    ref[...] = jnp.zeros_like(ref)

            for ref, v in zip(o_refs[n_or:], outs_a, strict=True):
                ref[...] += v

    def const_spec(c):
        nd = c.ndim
        return pl.BlockSpec(c.shape, lambda i: (0,) * nd)

    in_specs = ([pl.BlockSpec((tm, a.shape[1]), lambda i: (i, 0)) for a in rows]
                + [pl.BlockSpec((8, rows[j].shape[1]), lambda i: (jnp.maximum(i * nb8 - 1, 0), 0)) for j in prev]
                + [pl.BlockSpec((8, rows[j].shape[1]), lambda i: (jnp.minimum((i + 1) * nb8, last8), 0)) for j in nxt]
                + [const_spec(c) for c in consts])
    out_specs = ([pl.BlockSpec((tm, c), lambda i: (i, 0)) for c, _ in out_rows]
                 + [pl.BlockSpec(s, lambda i: (0, 0)) for s in out_accs])
    out_shape = ([jax.ShapeDtypeStruct((L, c), dt) for c, dt in out_rows]
                 + [jax.ShapeDtypeStruct(s, f32) for s in out_accs])
    args = list(rows) + [rows[j] for j in prev] + [rows[j] for j in nxt] + list(consts)
    return pl.pallas_call(
        body, name=name, grid=(nsteps,), in_specs=in_specs, out_specs=out_specs, out_shape=out_shape,
        compiler_params=pltpu.CompilerParams(dimension_semantics=("arbitrary",), vmem_limit_bytes=VMEM_LIMIT),
    )(*args)


def _shift_down(x, prev8, i, k):
    rolled = pltpu.roll(x, k, axis=0)
    pfix = jnp.where(i > 0, pltpu.roll(prev8, k, axis=0), 0.0)
    row8 = lax.broadcasted_iota(jnp.int32, pfix.shape, 0)
    top = jnp.where(row8 < k, pfix, rolled[:8])
    return top if x.shape[0] == 8 else jnp.concatenate([top, rolled[8:]], axis=0)


def _shift_up(x, next8, i, nsteps, k):
    tm = x.shape[0]
    rolled = pltpu.roll(x, tm - k, axis=0)
    nfix = jnp.where(i < nsteps - 1, pltpu.roll(next8, 8 - k, axis=0), 0.0)
    row8 = lax.broadcasted_iota(jnp.int32, nfix.shape, 0)
    bot = jnp.where(row8 >= 8 - k, nfix, rolled[tm - 8:])
    return jnp.concatenate([rolled[:tm - 8], bot], axis=0)


def _sum0(x):
    return jnp.sum(x, axis=0, keepdims=True)


def _rms(x, g):
    return x * lax.rsqrt(jnp.mean(x * x, axis=-1, keepdims=True) + NORM_EPS) * g


def _softplus(x):
    return jnp.maximum(x, 0.0) + jnp.log(1.0 + jnp.exp(-jnp.abs(x)))


def _gelu(x):
    return 0.5 * x * (1.0 + jnp.tanh(0.7978845608028654 * (x + 0.044715 * x * x * x)))


def _dot32(a, b):
    return jnp.dot(a, b, preferred_element_type=f32, precision=lax.Precision.HIGHEST)


def _seg_raw(x, E):
    hi = x.astype(bf16)
    r1 = x - hi.astype(f32)
    mid = r1.astype(bf16)
    lo = (r1 - mid.astype(f32)).astype(bf16)
    Eb = E.astype(bf16)
    dot = lambda t: jnp.dot(t, Eb, preferred_element_type=f32)
    return (dot(lo) + dot(mid)) + dot(hi)


@jax.custom_vjp
def _seg(x, E):
    return _seg_raw(x, E)


_seg.defvjp(lambda x, E: (_seg_raw(x, E), E), lambda E, g: (_seg_raw(g, E), jnp.zeros_like(E)))


def _prep(q, w0, a0, k_k, k_a, w2p, a2p, g2, E):
    r, k, v = q[:, 0:512], q[:, 512:1024], q[:, 1024:1536]
    wa, gd = q[:, 1536:1664], q[:, 1664:1792]
    wlog = -_softplus(-(w0 + _bdot(jnp.tanh(wa), w2p, 'nn'))) - 0.5
    lw = -jnp.exp(wlog)
    a = jax.nn.sigmoid(a0 + _bdot(wa, a2p, 'nn'))
    g = _bdot(jax.nn.sigmoid(gd), g2, 'nn')
    kk = k * k_k
    kkn = kk / jnp.maximum(jnp.sqrt(_seg(kk * kk, E)), 1e-12)
    k2 = k * (1.0 + (a - 1.0) * k_a)
    return r, lw, k2, v, -kkn, kkn * a, g


def _rwkv_out(y, r, k2, v, g, lnx_w, lnx_b, r_k, E):
    mean = _seg(y, E) * (1.0 / HEAD)
    yc = y - mean
    var = _seg(yc * yc, E) * (1.0 / HEAD)
    yn = yc * lax.rsqrt(var + LNX_EPS) * lnx_w + lnx_b
    bonus = _seg(r * k2 * r_k, E) * v
    return (yn + bonus) * g


def _s5_mid(ysc, u, d):
    return _gelu(ysc + d * u)


def _s5_glu(yg, z2, b_glu):
    return yg * jax.nn.sigmoid(z2 + b_glu)


def _merge(gp, o_r, o_s, b_gate):
    gates = jax.nn.sigmoid(gp + b_gate)
    return gates[:, :D_MODEL] * o_r + gates[:, D_MODEL:] * o_s


def _act(zc):
    return _gelu(zc[:, :D_FF]) * zc[:, D_FF:]


def _s5_disc(a_re, a_im, ls, b_re, b_im):
    dt = jnp.exp(ls)
    er = jnp.exp(a_re * dt)
    ar, ai = er * jnp.cos(a_im * dt), er * jnp.sin(a_im * dt)
    x, y = ar - 1.0, ai
    den = a_re * a_re + a_im * a_im
    fr, fi = (x * a_re + y * a_im) / den, (y * a_re - x * a_im) / den
    return ar, ai, fr * b_re - fi * b_im, fr * b_im + fi * b_re


_DIMS = {'nn': ((1,), (0,)), 'nt': ((1,), (1,)), 'tn': ((0,), (0,))}


def _raw_bdot(a, b, mode):
    return lax.dot_general(a.astype(bf16), b.astype(bf16), (_DIMS[mode], ((), ())), preferred_element_type=f32)


@functools.partial(jax.custom_vjp, nondiff_argnums=(2,))
def _bdot(a, b, mode):
    return _raw_bdot(a, b, mode)


def _bdot_fwd(a, b, mode):
    return _raw_bdot(a, b, mode), (a, b)


def _bdot_bwd(mode, res, g):
    a, b = res
    if mode == 'nn':
        return _raw_bdot(g, b, 'nt'), _raw_bdot(a, g, 'tn')
    if mode == 'nt':
        return _raw_bdot(g, b, 'nn'), _raw_bdot(g, a, 'tn')
    return _raw_bdot(b, g, 'nt'), _raw_bdot(a, g, 'nn')


_bdot.defvjp(_bdot_fwd, _bdot_bwd)


def _wkv_chunk(S0, r, lw, k, v, a, b, tri, bd):
    C = r[0].shape[0]
    P = range(len(r))
    lane = lax.broadcasted_iota(jnp.int32, (1, 2 * HEAD), 1)
    m0, m1 = (lane < HEAD).astype(f32), (lane >= HEAD).astype(f32)
    cat = lambda *xs: jnp.concatenate(xs, axis=0)
    stack = lambda x: cat(x * m0, x * m1)
    unstack = lambda x2: m0 * x2[:C] + m1 * x2[C:]
    rid = lax.broadcasted_iota(jnp.int32, (2 * C, 2 * C), 0)
    cid = lax.broadcasted_iota(jnp.int32, (2 * C, 2 * C), 1)
    same = (rid < C) == (cid < C)
    eye2 = (rid == cid).astype(f32)
    tri2 = (same & (rid >= cid)).astype(f32)
    sl2 = tri2 - eye2
    cum = [_dot32(tri, lw[p]) for p in P]
    g = [jnp.exp(cum[p]) for p in P]
    gi = [jnp.exp(-cum[p]) for p in P]
    at = [a[p] * jnp.exp(cum[p] - lw[p]) for p in P]
    rt = [r[p] * g[p] for p in P]
    kb = [k[p] * gi[p] for p in P]
    bb = [b[p] * gi[p] for p in P]
    lhs = [cat(stack(at[p]), stack(rt[p])) for p in P]
    pb = [_bdot(lhs[p], stack(bb[p]), 'nt') for p in P]
    pk = [_bdot(lhs[p], stack(kb[p]), 'nt') for p in P]
    aab = [pb[p][:2 * C] * sl2 for p in P]
    base = [_bdot(cat(at[p], rt[p]), S0[p], 'nt') for p in P]
    t = [_bdot(cat(pk[p][:2 * C] * sl2, pk[p][2 * C:] * tri2), cat(v[p], v[p]), 'nn') for p in P]
    rhs = [cat(base[p][:C], base[p][:C]) + t[p][:2 * C] for p in P]
    x = [eye2 + aab[p] for p in P]
    pw = aab
    n = 1
    while 2 * n < C:
        pw = [_bdot(pw[p], pw[p], 'nn') for p in P]
        x = [x[p] + _bdot(x[p], pw[p], 'nn') for p in P]
        n *= 2
    u = [unstack(_bdot(x[p], rhs[p], 'nn')) for p in P]
    w2 = [_bdot(pb[p][2 * C:] * tri2, cat(u[p], u[p]), 'nn') for p in P]
    y = [base[p][C:] + unstack(t[p][2 * C:]) + unstack(w2[p]) for p in P]
    S1 = [g[p][C - 1:C, :] * (S0[p] + bd * _bdot(cat(v[p], u[p]), cat(kb[p], bb[p]), 'tn')) for p in P]
    return y, S1


def _pairs(x):
    return [x[:, 2 * HEAD * p:2 * HEAD * (p + 1)] for p in range(HEADS // 2)]


def _wkv_consts():
    tri = jnp.tril(jnp.ones((WKV_C, WKV_C), f32))
    hid = jnp.arange(2 * HEAD) // HEAD
    return tri, (hid[:, None] == hid[None, :]).astype(f32)


def _wkv7_fwd(r, lw, k, v, a, b):
    L = r.shape[0]
    nc, npair = L // WKV_C, HEADS // 2

    def body(r_ref, lw_ref, k_ref, v_ref, a_ref, b_ref, tri_ref, bd_ref, y_ref, ck_ref, s_ref):
        @pl.when(pl.program_id(0) == 0)
        def _():
            s_ref[...] = jnp.zeros_like(s_ref)

        s0 = [s_ref[p] for p in range(npair)]
        for p in range(npair):
            ck_ref[0, p] = s0[p]
        y, s1 = _wkv_chunk(s0, *(_pairs(x) for x in (r_ref, lw_ref, k_ref, v_ref, a_ref, b_ref)), tri_ref[...], bd_ref[...])
        for p in range(npair):
            y_ref[:, 2 * HEAD * p:2 * HEAD * (p + 1)] = y[p]
            s_ref[p] = s1[p]

    row = pl.BlockSpec((WKV_C, RWKV_W), lambda c: (c, 0))
    sspec = pl.BlockSpec((1, npair, 2 * HEAD, 2 * HEAD), lambda c: (c, 0, 0, 0))
    return pl.pallas_call(
        body, name="wkv7_fwd", grid=(nc,),
        in_specs=[row] * 6 + [pl.BlockSpec((WKV_C, WKV_C), lambda c: (0, 0)), pl.BlockSpec((2 * HEAD, 2 * HEAD), lambda c: (0, 0))],
        out_specs=[row, sspec],
        out_shape=[jax.ShapeDtypeStruct((L, RWKV_W), f32), jax.ShapeDtypeStruct((nc, npair, 2 * HEAD, 2 * HEAD), f32)],
        scratch_shapes=[pltpu.VMEM((npair, 2 * HEAD, 2 * HEAD), f32)],
        compiler_params=pltpu.CompilerParams(dimension_semantics=("arbitrary",), vmem_limit_bytes=VMEM_LIMIT),
    )(r, lw, k, v, a, b, *_wkv_consts())


def _wkv7_bwd(r, lw, k, v, a, b, ck, dy):
    L = r.shape[0]
    nc, npair = L // WKV_C, HEADS // 2

    def body(r_ref, lw_ref, k_ref, v_ref, a_ref, b_ref, ck_ref, dy_ref, tri_ref, bd_ref,
             dr_ref, dlw_ref, dk_ref, dv_ref, da_ref, db_ref, ds_ref):
        @pl.when(pl.program_id(0) == 0)
        def _():
            ds_ref[...] = jnp.zeros_like(ds_ref)

        tri, bd = tri_ref[...], bd_ref[...]
        ins = [[ck_ref[0, p] for p in range(npair)]] + [_pairs(x) for x in (r_ref, lw_ref, k_ref, v_ref, a_ref, b_ref)]
        _, vjp = jax.vjp(lambda *t: _wkv_chunk(*t, tri, bd), *ins)
        gs = vjp((_pairs(dy_ref), [ds_ref[p] for p in range(npair)]))
        for p in range(npair):
            ds_ref[p] = gs[0][p]
            for ref, gval in zip((dr_ref, dlw_ref, dk_ref, dv_ref, da_ref, db_ref), gs[1:]):
                ref[:, 2 * HEAD * p:2 * HEAD * (p + 1)] = gval[p]

    row = pl.BlockSpec((WKV_C, RWKV_W), lambda c: (nc - 1 - c, 0))
    sspec = pl.BlockSpec((1, npair, 2 * HEAD, 2 * HEAD), lambda c: (nc - 1 - c, 0, 0, 0))
    return pl.pallas_call(
        body, name="wkv7_bwd", grid=(nc,),
        in_specs=[row] * 6 + [sspec, row, pl.BlockSpec((WKV_C, WKV_C), lambda c: (0, 0)),
                              pl.BlockSpec((2 * HEAD, 2 * HEAD), lambda c: (0, 0))],
        out_specs=[row] * 6,
        out_shape=[jax.ShapeDtypeStruct((L, RWKV_W), f32)] * 6,
        scratch_shapes=[pltpu.VMEM((npair, 2 * HEAD, 2 * HEAD), f32)],
        compiler_params=pltpu.CompilerParams(dimension_semantics=("arbitrary",), vmem_limit_bytes=VMEM_LIMIT),
    )(r, lw, k, v, a, b, ck, dy, *_wkv_consts())


def _cmul(ar, ai, xr, xi):
    return ar * xr - ai * xi, ar * xi + ai * xr


def _scan_init(a_ref, car_ref, pw_ref, reverse):
    car_ref[...] = jnp.zeros_like(car_ref)
    ar = jnp.broadcast_to(a_ref[:, :S5_N], (8, S5_N))
    ai = jnp.broadcast_to(a_ref[:, S5_N:], (8, S5_N))
    if reverse:
        ai = -ai
    row = lax.broadcasted_iota(jnp.int32, (8, S5_N), 0)
    pr, pi = ar, ai
    qr, qi = jnp.zeros((8, S5_N), f32), jnp.zeros((8, S5_N), f32)
    for e in range(1, 9):
        sel = (row == 8 - e) if reverse else (row == e - 1)
        qr, qi = jnp.where(sel, pr, qr), jnp.where(sel, pi, qi)
        if e in (1, 2, 4):
            j = (1, 2, 4).index(e)
            pw_ref[j, :, :S5_N] = pr
            pw_ref[j, :, S5_N:] = pi
        pr, pi = _cmul(pr, pi, ar, ai)
    pw_ref[3, :, :S5_N] = qr
    pw_ref[3, :, S5_N:] = qi


def _scan_tile(x_ref, o_ref, car_ref, pw_ref, reverse):
    ng = x_ref.shape[0] // 8
    row = lax.broadcasted_iota(jnp.int32, (8, S5_N), 0)

    def group(gi, carry):
        g = (ng - 1 - gi) if reverse else gi
        t0 = pl.multiple_of(g * 8, 8)
        xr, xi = x_ref[pl.ds(t0, 8), :S5_N], x_ref[pl.ds(t0, 8), S5_N:]
        for j, d in enumerate((1, 2, 4)):
            if reverse:
                sr = jnp.where(row < 8 - d, pltpu.roll(xr, 8 - d, axis=0), 0.0)
                si = jnp.where(row < 8 - d, pltpu.roll(xi, 8 - d, axis=0), 0.0)
            else:
                sr = jnp.where(row >= d, pltpu.roll(xr, d, axis=0), 0.0)
                si = jnp.where(row >= d, pltpu.roll(xi, d, axis=0), 0.0)
            mr, mi = _cmul(pw_ref[j, :, :S5_N], pw_ref[j, :, S5_N:], sr, si)
            xr, xi = xr + mr, xi + mi
        cr, ci = carry
        mr, mi = _cmul(pw_ref[3, :, :S5_N], pw_ref[3, :, S5_N:], cr, ci)
        xr, xi = xr + mr, xi + mi
        o_ref[pl.ds(t0, 8), :S5_N] = xr
        o_ref[pl.ds(t0, 8), S5_N:] = xi
        e = 0 if reverse else 7
        return (jnp.broadcast_to(xr[e:e + 1, :], (8, S5_N)), jnp.broadcast_to(xi[e:e + 1, :], (8, S5_N)))

    cr, ci = lax.fori_loop(0, ng, group, (car_ref[:, :S5_N], car_ref[:, S5_N:]))
    car_ref[:, :S5_N] = cr
    car_ref[:, S5_N:] = ci


_CB, _SB = 128, 512


def _cblk(k):
    return slice(_CB * k, _CB * (k + 1))


def _sblk(j):
    return slice(_SB * j, _SB * (j + 1))


def _s5_fwd(u, bmat, cmat, abar, late):
    L = u.shape[0]
    nt = L // S5_T
    names = list(late)
    nh = len(names)

    def body(u_ref, b_ref, c_ref, a_ref, *rest):
        h_in, (st_ref, y_ref), h_out = rest[:nh], rest[nh:nh + 2], rest[nh + 2:2 * nh + 2]
        bu_ref, car_ref, pw_ref, ssem, rsem, lsem = rest[2 * nh + 2:]
        i = pl.program_id(0)

        def copies():
            px, py, pc = _mesh_pos()
            me = 2 * px + py
            out = []
            for a, nm in enumerate(names):
                hr = late[nm].shape[0] // 2
                src, dst = h_in[a].at[pl.ds(pl.multiple_of(pc * hr, 16), hr), :], _slab(h_out[a], nm, me, pc)
                out.append(pltpu.make_async_copy(src, dst, lsem.at[a]))
                out += [pltpu.make_async_remote_copy(src, dst, ssem.at[3 * a + k], rsem.at[3 * a + k],
                                                     device_id=(qx, qy, pc), device_id_type=MESH)
                        for k, (qx, qy) in enumerate(_chip_peers(px, py))]
            return out

        @pl.when(i == 0)
        def _():
            _scan_init(a_ref, car_ref, pw_ref, False)
            for cp in copies():
                cp.start()

        for j in range(8):
            bu_ref[:, _sblk(j)] = _raw_bdot(u_ref[:, _cblk(j % 4)], b_ref[j], 'nn')
        _scan_tile(bu_ref, st_ref, car_ref, pw_ref, False)
        for k in range(4):
            y_ref[:, _cblk(k)] = (_raw_bdot(st_ref[:, _sblk(k)], c_ref[k], 'nn')
                                  + _raw_bdot(st_ref[:, _sblk(4 + k)], c_ref[4 + k], 'nn'))

        @pl.when(i == nt - 1)
        def _():
            for cp in copies():
                cp.wait()

    whole = lambda shape: pl.BlockSpec(shape, lambda i: (0,) * len(shape))
    outs = pl.pallas_call(
        body, name="s5_fwd", grid=(nt,),
        in_specs=[pl.BlockSpec((S5_T, S5_W), lambda i: (i, 0)), whole(bmat.shape), whole(cmat.shape), whole(abar.shape)]
        + [ANY] * nh,
        out_specs=[pl.BlockSpec((S5_T, 2 * S5_N), lambda i: (i, 0)), pl.BlockSpec((S5_T, S5_W), lambda i: (i, 0))] + [ANY] * nh,
        out_shape=[jax.ShapeDtypeStruct((L, 2 * S5_N), f32), jax.ShapeDtypeStruct((L, S5_W), f32)]
        + [jax.ShapeDtypeStruct(GATHER[nm][0], late[nm].dtype) for nm in names],
        scratch_shapes=[pltpu.VMEM((S5_T, 2 * S5_N), f32), pltpu.VMEM((8, 2 * S5_N), f32), pltpu.VMEM((4, 8, 2 * S5_N), f32),
                        pltpu.SemaphoreType.DMA((3 * nh,)), pltpu.SemaphoreType.DMA((3 * nh,)), pltpu.SemaphoreType.DMA((nh,))],
        compiler_params=pltpu.CompilerParams(dimension_semantics=("arbitrary",), vmem_limit_bytes=VMEM_LIMIT),
    )(u, bmat, cmat, abar, *[late[nm] for nm in names])
    return outs[0], outs[1], dict(zip(names, outs[2:]))


def _s5_bwd(dy, st, u, du_direct, bmat, cmat, abar, chip_sum):
    L = u.shape[0]
    nt = L // S5_T
    nb8 = S5_T // 8
    names = list(chip_sum)
    nh = len(names)

    def body(dy_ref, st_ref, sp_ref, u_ref, dud_ref, b_ref, c_ref, a_ref, *rest):
        x_in, (du_ref, db_ref, dc_ref, da_ref), x_out = rest[:nh], rest[nh:nh + 4], rest[nh + 4:2 * nh + 4]
        lam_ref, car_ref, pw_ref, ssem, rsem = rest[2 * nh + 4:]
        i = pl.program_id(0)

        @pl.when(i == 0)
        def _():
            _scan_init(a_ref, car_ref, pw_ref, True)
            db_ref[...] = jnp.zeros_like(db_ref)
            dc_ref[...] = jnp.zeros_like(dc_ref)
            da_ref[...] = jnp.zeros_like(da_ref)
            for cp in _exchange_copies(x_in, x_out, ssem, rsem):
                cp.start()

        for j in range(8):
            lam_ref[:, _sblk(j)] = _raw_bdot(dy_ref[:, _cblk(j % 4)], c_ref[j], 'nt')
        _scan_tile(lam_ref, lam_ref, car_ref, pw_ref, True)
        for k in range(4):
            du_ref[:, _cblk(k)] = (dud_ref[:, _cblk(k)] + _raw_bdot(lam_ref[:, _sblk(k)], b_ref[k], 'nt')
                                   + _raw_bdot(lam_ref[:, _sblk(4 + k)], b_ref[4 + k], 'nt')
                                   ).astype(du_ref.dtype)
            sr = _shift_down(st_ref[:, _sblk(k)], sp_ref[:, _sblk(k)], nt - 1 - i, 1)
            si = _shift_down(st_ref[:, _sblk(4 + k)], sp_ref[:, _sblk(4 + k)], nt - 1 - i, 1)
            lr, li = lam_ref[:, _sblk(k)], lam_ref[:, _sblk(4 + k)]
            da_ref[:, _sblk(k)] += _sum0(lr * sr + li * si)
            da_ref[:, _sblk(4 + k)] += _sum0(li * sr - lr * si)
        for j in range(8):
            db_ref[j] += _raw_bdot(u_ref[:, _cblk(j % 4)], lam_ref[:, _sblk(j)], 'tn')
            dc_ref[j] += _raw_bdot(st_ref[:, _sblk(j)], dy_ref[:, _cblk(j % 4)], 'tn')

        @pl.when(i == nt - 1)
        def _():
            for cp in _exchange_copies(x_in, x_out, ssem, rsem):
                cp.wait()

    whole = lambda shape: pl.BlockSpec(shape, lambda i: (0,) * len(shape))
    rev = lambda i: (nt - 1 - i, 0)
    outs = pl.pallas_call(
        body, name="s5_bwd", grid=(nt,),
        in_specs=[pl.BlockSpec((S5_T, S5_W), rev), pl.BlockSpec((S5_T, 2 * S5_N), rev),
                  pl.BlockSpec((8, 2 * S5_N), lambda i: (jnp.maximum((nt - 1 - i) * nb8 - 1, 0), 0)),
                  pl.BlockSpec((S5_T, S5_W), rev), pl.BlockSpec((S5_T, S5_W), rev), whole(bmat.shape), whole(cmat.shape),
                  whole(abar.shape)] + [ANY] * nh,
        out_specs=[pl.BlockSpec((S5_T, S5_W), rev), whole((8, _CB, _SB)), whole((8, _SB, _CB)), whole((1, 2 * S5_N))]
        + [ANY] * nh,
        out_shape=[jax.ShapeDtypeStruct((L, S5_W), bf16), jax.ShapeDtypeStruct((8, _CB, _SB), f32),
                   jax.ShapeDtypeStruct((8, _SB, _CB), f32), jax.ShapeDtypeStruct((1, 2 * S5_N), f32)]
        + [jax.ShapeDtypeStruct(chip_sum[nm].shape, chip_sum[nm].dtype) for nm in names],
        scratch_shapes=[pltpu.VMEM((S5_T, 2 * S5_N), f32), pltpu.VMEM((8, 2 * S5_N), f32), pltpu.VMEM((4, 8, 2 * S5_N), f32),
                        pltpu.SemaphoreType.DMA((3 * nh,)), pltpu.SemaphoreType.DMA((3 * nh,))],
        compiler_params=pltpu.CompilerParams(dimension_semantics=("arbitrary",), vmem_limit_bytes=VMEM_LIMIT),
    )(dy, st, st, u, du_direct, bmat, cmat, abar, *[chip_sum[nm] for nm in names])
    return outs[0], outs[1], outs[2], outs[3], dict(zip(names, outs[4:]))


def _s5_disc_fwd(a_re, a_im, ls, b_re, b_im):
    def body(a_re_ref, a_im_ref, ls_ref, b_re_ref, b_im_ref, ar_ref, ai_ref, br_ref, bi_ref):
        outs = _s5_disc(a_re_ref[...], a_im_ref[...], ls_ref[...], b_re_ref[...], b_im_ref[...])
        for ref, v in zip((ar_ref, ai_ref, br_ref, bi_ref), outs):
            ref[...] = v

    c1, c16 = jax.ShapeDtypeStruct((S5_N, 1), f32), jax.ShapeDtypeStruct((S5_N, S5_C), f32)
    return pl.pallas_call(body, name="s5_disc", out_shape=[c1, c1, c16, c16])(a_re, a_im, ls, b_re, b_im)


def _s5_disc_bwd(a_re, a_im, ls, b_re, b_im, d_ar, d_ai, d_br, d_bi, seg):
    def body(a_re_ref, a_im_ref, ls_ref, b_re_ref, b_im_ref, g1, g2, g3, g4, seg_ref, o1, o2, o3, o4, o5):
        _, vjp = jax.vjp(_s5_disc, a_re_ref[...], a_im_ref[...], ls_ref[...], b_re_ref[...], b_im_ref[...])
        da_re, da_im, dls, db_re, db_im = vjp((g1[...], g2[...], g3[...], g4[...]))
        o1[...] = da_re
        o2[...] = da_im
        o3[...] = _dot32(seg_ref[...], dls)
        o4[...] = db_re
        o5[...] = db_im

    c1, c16 = jax.ShapeDtypeStruct((S5_N, 1), f32), jax.ShapeDtypeStruct((S5_N, S5_C), f32)
    return pl.pallas_call(body, name="s5_disc_bwd", out_shape=[c1, c1, jax.ShapeDtypeStruct((S5_G, 1), f32), c16, c16])(
        a_re, a_im, ls, b_re, b_im, d_ar, d_ai, d_br, d_bi, seg)


ANY = pl.BlockSpec(memory_space=pl.ANY)

GATHER = {'w_in': ((4352, 1024), 0), 'ffn_w_up': ((1024, 5632), 1), 'w_branch_rwkv': ((512, 1024), 1),
          'w_branch_s5': ((512, 1024), 1), 'w_out': ((1024, 1024), 0), 's5_w_glu': ((512, 512), 0),
          'ffn_w_down': ((2816, 1024), 0), 'rwkv_w2': ((64, 512), 1), 'rwkv_a2': ((64, 512), 1),
          'rwkv_g2': ((128, 512), 1), 'ffn_conv_w': ((8, 5632), 1)}
BIG = ['w_in', 'ffn_w_up', 'w_branch_rwkv', 'w_branch_s5', 'w_out', 's5_w_glu', 'ffn_w_down']
TINY = ['rwkv_w2', 'rwkv_a2', 'rwkv_g2', 'ffn_conv_w']
SMALL = [n for n in WEIGHTS if n not in GATHER]
SMALL_ROWS = 320
ADAM_ROWS = 256


def _mo(v, m):
    return v if isinstance(v, int) else pl.multiple_of(v, m)


def _slab(ref, name, j, h=None):
    (R, Cn), axis = GATHER[name]
    if axis == 0:
        rs = R // 4
        if h is None:
            return ref.at[pl.ds(_mo(j * rs, 16), rs), :]
        return ref.at[pl.ds(_mo(j * rs + h * (rs // 2), 8), rs // 2), :]
    cols = pl.ds(_mo(j * (Cn // 4), 128), Cn // 4)
    if h is None:
        return ref.at[:, cols]
    return ref.at[pl.ds(_mo(h * (R // 2), 8), R // 2), cols]


def _half_shape(name):
    (R, Cn), axis = GATHER[name]
    return (R // 8, Cn) if axis == 0 else (R // 2, Cn // 4)


def _chip_peers(px, py):
    return [((1 - px) if (k >> 1) else px, (1 - py) if (k & 1) else py) for k in (1, 2, 3)]


def _run_copies(copies):
    for cp in copies:
        cp.start()
    for cp in copies:
        cp.wait()


def _gather_weights(blocks):
    names = list(blocks)
    n = len(names)

    def body(*refs):
        ins, outs = refs[:n], refs[n:2 * n]
        ssem, rsem, lsem = refs[2 * n:]
        px, py, pc = _mesh_pos()
        me = 2 * px + py
        copies = []
        for i, nm in enumerate(names):
            if nm in BIG:
                hr = blocks[nm].shape[0] // 2
                src, dst = ins[i].at[pl.ds(pl.multiple_of(pc * hr, 16), hr), :], _slab(outs[i], nm, me, pc)
            else:
                src, dst = ins[i], _slab(outs[i], nm, me)
            copies.append(pltpu.make_async_copy(src, dst, lsem.at[i]))
            for k, (qx, qy) in enumerate(_chip_peers(px, py)):
                copies.append(pltpu.make_async_remote_copy(src, dst, ssem.at[3 * i + k], rsem.at[3 * i + k],
                                                           device_id=(qx, qy, pc), device_id_type=MESH))
        _run_copies(copies)

    outs = pl.pallas_call(
        body, name="gather_weights", in_specs=[ANY] * n, out_specs=[ANY] * n,
        out_shape=[jax.ShapeDtypeStruct(GATHER[nm][0], blocks[nm].dtype) for nm in names],
        scratch_shapes=[pltpu.SemaphoreType.DMA((3 * n,)), pltpu.SemaphoreType.DMA((3 * n,)), pltpu.SemaphoreType.DMA((n,))],
    )(*[blocks[nm] for nm in names])
    return dict(zip(names, outs))


def _gather_pair(full, names, call_name):
    n = len(names)

    def body(*refs):
        ins, outs = refs[:n], refs[n:2 * n]
        ssem, rsem = refs[2 * n:]
        px, py, pc = _mesh_pos()
        copies = []
        for i, nm in enumerate(names):
            for j in range(4):
                copies.append(pltpu.make_async_remote_copy(_slab(ins[i], nm, j, pc), _slab(outs[i], nm, j, pc),
                                                           ssem.at[4 * i + j], rsem.at[4 * i + j],
                                                           device_id=(px, py, 1 - pc), device_id_type=MESH))
        _run_copies(copies)

    outs = pl.pallas_call(
        body, name=call_name, in_specs=[ANY] * n, out_specs=[ANY] * n,
        out_shape=[jax.ShapeDtypeStruct(full[nm].shape, full[nm].dtype) for nm in names],
        input_output_aliases={i: i for i in range(n)},
        scratch_shapes=[pltpu.SemaphoreType.DMA((4 * n,)), pltpu.SemaphoreType.DMA((4 * n,))],
    )(*[full[nm] for nm in names])
    return dict(zip(names, outs))


def _grads_to_sibling(G, names, call_name, small=None):
    n = len(names)
    ns = 0 if small is None else 1

    def body(*refs):
        g_refs, o_refs = refs[:n + ns], refs[n + ns:2 * (n + ns)]
        ssem, rsem = refs[2 * (n + ns):]
        px, py, pc = _mesh_pos()
        sib = (px, py, 1 - pc)
        copies = []
        for i, nm in enumerate(names):
            for j in range(4):
                copies.append(pltpu.make_async_remote_copy(_slab(g_refs[i], nm, j, 1 - pc), o_refs[i].at[j],
                                                           ssem.at[4 * i + j], rsem.at[4 * i + j],
                                                           device_id=sib, device_id_type=MESH))
        if ns:
            copies.append(pltpu.make_async_remote_copy(g_refs[n], o_refs[n], ssem.at[4 * n], rsem.at[4 * n],
                                                       device_id=sib, device_id_type=MESH))
        _run_copies(copies)

    outs = pl.pallas_call(
        body, name=call_name, in_specs=[ANY] * (n + ns), out_specs=[ANY] * (n + ns),
        out_shape=[jax.ShapeDtypeStruct((4,) + _half_shape(nm), f32) for nm in names]
        + [jax.ShapeDtypeStruct((SMALL_ROWS, PACK_W), f32)] * ns,
        scratch_shapes=[pltpu.SemaphoreType.DMA((4 * n + ns,)), pltpu.SemaphoreType.DMA((4 * n + ns,))],
    )(*[G[nm] for nm in names], *([small] * ns))
    return dict(zip(names, outs[:n])), (outs[n] if ns else None)


def _pair_add(G, recv, names, call_name, small=None, small_recv=None):
    n = len(names)
    ns = 0 if small is None else 1
    cidx = lax.axis_index("c").astype(jnp.int32).reshape(1)

    def body(c_ref, *refs):
        ins, outs = refs[:2 * (n + ns)], refs[2 * (n + ns):]
        for i in range(n):
            outs[i][...] = (ins[i][...] + ins[n + ns + i][...]).astype(bf16)
        if ns:
            outs[n][...] = ins[n][...] + ins[2 * n + 1][...]

    g_specs, r_specs = [], []
    for nm in names:
        hr, hc = _half_shape(nm)
        if GATHER[nm][1] == 0:
            g_specs.append(pl.BlockSpec((hr // 2, hc), lambda j, i, c: ((2 * j + c[0]) * 2 + i, 0)))
        else:
            g_specs.append(pl.BlockSpec((hr // 2, hc), lambda j, i, c: (2 * c[0] + i, j)))
        r_specs.append(pl.BlockSpec((1, hr // 2, hc), lambda j, i, c: (j, i, 0)))
    sm = [pl.BlockSpec((SMALL_ROWS // 8, PACK_W), lambda j, i, c: (2 * j + i, 0))] * ns
    outs = pl.pallas_call(
        body, name=call_name,
        grid_spec=pltpu.PrefetchScalarGridSpec(num_scalar_prefetch=1, grid=(4, 2), in_specs=g_specs + sm + r_specs + sm,
                                               out_specs=r_specs + sm),
        out_shape=[jax.ShapeDtypeStruct((4,) + _half_shape(nm), bf16) for nm in names]
        + [jax.ShapeDtypeStruct((SMALL_ROWS, PACK_W), f32)] * ns,
        compiler_params=pltpu.CompilerParams(vmem_limit_bytes=VMEM_LIMIT),
    )(cidx, *[G[nm] for nm in names], *([small] * ns), *[recv[nm] for nm in names], *([small_recv] * ns))
    return dict(zip(names, outs[:n])), (outs[n] if ns else None)


def _exchange_copies(ins, outs, ssem, rsem):
    px, py, pc = _mesh_pos()
    me = 2 * px + py
    return [pltpu.make_async_remote_copy(ins[i].at[2 * qx + qy], outs[i].at[me], ssem.at[3 * i + k], rsem.at[3 * i + k],
                                         device_id=(qx, qy, pc), device_id_type=MESH)
            for i in range(len(ins)) for k, (qx, qy) in enumerate(_chip_peers(px, py))]


def _grads_chip_exchange(chip_sum, names, small):
    n = len(names)

    def body(*refs):
        ins, outs = refs[:n + 1], refs[n + 1:2 * n + 2]
        ssem, rsem, ssem_s, rsem_s = refs[2 * n + 2:]
        px, py, pc = _mesh_pos()
        me = 2 * px + py
        copies = _exchange_copies(ins[:n], outs[:n], ssem, rsem)
        hs = SMALL_ROWS // 2
        mine = ins[n].at[pl.ds(pl.multiple_of(pc * hs, 8), hs), :]
        copies += [pltpu.make_async_remote_copy(mine, outs[n].at[me], ssem_s.at[k], rsem_s.at[k],
                                                device_id=(qx, qy, pc), device_id_type=MESH)
                   for k, (qx, qy) in enumerate(_chip_peers(px, py))]
        _run_copies(copies)

    outs = pl.pallas_call(
        body, name="grads_chip_exchange", in_specs=[ANY] * (n + 1), out_specs=[ANY] * (n + 1),
        out_shape=[jax.ShapeDtypeStruct(chip_sum[nm].shape, chip_sum[nm].dtype) for nm in names]
        + [jax.ShapeDtypeStruct((4, SMALL_ROWS // 2, PACK_W), f32)],
        scratch_shapes=[pltpu.SemaphoreType.DMA((3 * n,)), pltpu.SemaphoreType.DMA((3 * n,)),
                        pltpu.SemaphoreType.DMA((3,)), pltpu.SemaphoreType.DMA((3,))],
    )(*[chip_sum[nm] for nm in names], small)
    return dict(zip(names, outs[:n])), outs[n]


def _sum_slots(slots, chip_sum, small4, small_own):
    n = len(BIG)
    me = jnp.stack([2 * lax.axis_index("x") + lax.axis_index("y"), lax.axis_index("c")]).astype(jnp.int32)

    def body(me_ref, *refs):
        for i in range(n + 1):
            own = refs[5 * i + 4][...].astype(f32)
            own = own[0] if i < n else own
            term = [jnp.where(me_ref[0] == k, own, refs[5 * i + k][0].astype(f32)) for k in range(4)]
            refs[5 * (n + 1) + i][...] = ((term[0] + term[1]) + term[2]) + term[3]

    redirect = lambda k: (lambda i, m: (jnp.where(m[0] == k, (k + 1) % 4, k), i, 0))
    in_specs, args, specs_out, shapes = [], [], [], []
    for nm in BIG:
        hr, hc = _half_shape(nm)
        in_specs += [pl.BlockSpec((1, hr // 2, hc), redirect(k)) for k in range(4)]
        in_specs.append(pl.BlockSpec((1, hr // 2, hc), lambda i, m: (m[0], i, 0)))
        args += [slots[nm]] * 4 + [chip_sum[nm]]
        specs_out.append(pl.BlockSpec((hr // 2, hc), lambda i, m: (i, 0)))
        shapes.append(jax.ShapeDtypeStruct((hr, hc), f32))
    in_specs += [pl.BlockSpec((1, SMALL_ROWS // 4, PACK_W), redirect(k)) for k in range(4)]
    in_specs.append(pl.BlockSpec((SMALL_ROWS // 4, PACK_W), lambda i, m: (2 * m[1] + i, 0)))
    args += [small4] * 4 + [small_own]
    specs_out.append(pl.BlockSpec((SMALL_ROWS // 4, PACK_W), lambda i, m: (i, 0)))
    shapes.append(jax.ShapeDtypeStruct((SMALL_ROWS // 2, PACK_W), f32))
    outs = pl.pallas_call(
        body, name="grads_chip_sum",
        grid_spec=pltpu.PrefetchScalarGridSpec(num_scalar_prefetch=1, grid=(2,), in_specs=in_specs, out_specs=specs_out),
        out_shape=shapes, compiler_params=pltpu.CompilerParams(vmem_limit_bytes=VMEM_LIMIT),
    )(me, *args)
    return dict(zip(BIG, outs[:n])), outs[n]


def _halves_to_sibling(half):
    names = list(half)
    n = len(names)

    def body(*refs):
        ins, outs = refs[:n], refs[n:2 * n]
        ssem, rsem = refs[2 * n:]
        px, py, pc = _mesh_pos()
        _run_copies([pltpu.make_async_remote_copy(ins[i], outs[i], ssem.at[i], rsem.at[i],
                                                  device_id=(px, py, 1 - pc), device_id_type=MESH) for i in range(n)])

    outs = pl.pallas_call(
        body, name="grads_halves_to_sibling", in_specs=[ANY] * n, out_specs=[ANY] * n,
        out_shape=[jax.ShapeDtypeStruct(half[nm].shape, f32) for nm in names],
        scratch_shapes=[pltpu.SemaphoreType.DMA((n,)), pltpu.SemaphoreType.DMA((n,))],
    )(*[half[nm] for nm in names])
    return dict(zip(names, outs))


def _join_halves(mine, other, pc):
    hr = mine.shape[0]
    return lax.dynamic_slice_in_dim(jnp.concatenate([other, mine, other], axis=0), (1 - pc) * hr, 2 * hr, axis=0)


def _flat_pad(v):
    v = v.reshape(-1)
    return jnp.pad(v, (0, _ceil_to(v.shape[0], PACK_W) - v.shape[0]))


def _pack_rows(parts, rows):
    flat = jnp.concatenate([_flat_pad(p) for p in parts])
    return jnp.pad(flat, (0, rows * PACK_W - flat.shape[0])).reshape(rows, PACK_W)


def _unpack_rows(buf, shapes):
    flat = buf.reshape(-1)
    out, off = [], 0
    for shp in shapes:
        n = 1
        for d in shp:
            n *= d
        out.append(flat[off:off + n].reshape(shp))
        off += _ceil_to(n, PACK_W)
    return out


def _adamw_math(w_, g_, m_, v_):
    m2 = ADAM_B1 * m_ + (1.0 - ADAM_B1) * g_
    v2 = ADAM_B2 * v_ + (1.0 - ADAM_B2) * (g_ * g_)
    m_hat = m2 / (1.0 - ADAM_B1 ** ADAM_STEP)
    v_hat = v2 / (1.0 - ADAM_B2 ** ADAM_STEP)
    return -ADAM_LR * (m_hat / (jnp.sqrt(v_hat) + ADAM_EPS) + ADAM_WD * w_), m2, v2


def _adamw(groups):
    ng = len(groups)

    def body(*refs):
        ins, outs = refs[:4 * ng], refs[4 * ng:]
        for i in range(ng):
            res = _adamw_math(*(r[...] for r in ins[4 * i:4 * i + 4]))
            for ref, val in zip(outs[3 * i:3 * i + 3], res):
                ref[...] = val

    in_specs, out_specs, out_shape = [], [], []
    for grp in groups:
        R, Cn = grp[0].shape
        spec = pl.BlockSpec((R // 8, Cn), lambda i: (i, 0))
        in_specs += [spec] * 4
        out_specs += [spec] * 3
        out_shape += [jax.ShapeDtypeStruct((R, Cn), f32)] * 3
    outs = pl.pallas_call(
        body, name="adamw", grid=(8,), in_specs=in_specs, out_specs=out_specs, out_shape=out_shape,
        compiler_params=pltpu.CompilerParams(vmem_limit_bytes=VMEM_LIMIT),
    )(*[a for grp in groups for a in grp])
    return [tuple(outs[3 * i:3 * i + 3]) for i in range(ng)]


def _forward_backward(x, tgt, W, S, late):
    L = x.shape[0]
    TM, TMW, TS = 256, 128, 512
    row = lambda c, dt=f32: (c, dt)
    hid = jnp.arange(RWKV_W) // HEAD
    E = (hid[:, None] == hid[None, :]).astype(f32)
    seg = (jnp.arange(S5_N)[None, :] // S5_P == jnp.arange(S5_G)[:, None]).astype(f32)

    w_in_t = W['w_in']
    w_p, w_u, w_g = w_in_t[:N_RWKV], w_in_t[N_RWKV:N_RWKV + S5_W], w_in_t[N_RWKV + S5_W:]
    zpad = jnp.zeros((64, RWKV_W), f32)
    w2p = jnp.concatenate([W['rwkv_w2'], zpad], axis=0)
    a2p = jnp.concatenate([zpad, W['rwkv_a2']], axis=0)
    g2 = W['rwkv_g2']
    prep_consts = [S['rwkv_shift_mu'], S['rwkv_w0'], S['rwkv_a0'], S['rwkv_k_k'], S['rwkv_k_a'], w2p, a2p, g2, E]
    out_consts = [S['rwkv_lnx_w'], S['rwkv_lnx_b'], S['rwkv_r_k'], E]
    cw, cb = W['ffn_conv_w'][:3], S['ffn_conv_b']

    a_re, a_im = S['s5_a_re'].reshape(S5_N, 1), S['s5_a_im'].reshape(S5_N, 1)
    ls = jnp.repeat(S['s5_log_step'].reshape(S5_G, 1), S5_P, axis=0)
    b_re, b_im = S['s5_b_re'].reshape(S5_N, S5_C), S['s5_b_im'].reshape(S5_N, S5_C)
    ar, ai, bbr, bbi = _s5_disc_fwd(a_re, a_im, ls, b_re, b_im)
    abar = jnp.concatenate([ar.reshape(1, S5_N), ai.reshape(1, S5_N)], axis=1)
    eye8 = jnp.eye(8, dtype=f32)

    def blocks_in(bb):
        t = bb.reshape(4, 8, S5_P, S5_C).transpose(0, 1, 3, 2)
        return (t[:, :, :, None, :] * eye8[None, :, None, :, None]).reshape(4, _CB, _SB)

    def blocks_out(cc):
        t = cc.reshape(4, 8, S5_C, S5_P).transpose(0, 1, 3, 2)
        return (t[:, :, :, None, :] * eye8[None, :, None, :, None]).reshape(4, _SB, _CB)

    def undiag_in(blocks):
        t = blocks.reshape(4, 8, S5_C, 8, S5_P)
        t = jnp.sum(t * eye8[None, :, None, :, None], axis=3)
        return t.reshape(S5_G, S5_C, S5_P).transpose(0, 2, 1).reshape(S5_N, S5_C)

    def undiag_out(blocks):
        t = blocks.reshape(4, 8, S5_P, 8, S5_C)
        t = jnp.sum(t * eye8[None, :, None, :, None], axis=3)
        return t.reshape(S5_G, S5_P, S5_C).transpose(0, 2, 1)

    bmat = jnp.concatenate([blocks_in(bbr), blocks_in(bbi)], axis=0).astype(bf16)
    cmat = jnp.concatenate([blocks_out(S['s5_c_re'].reshape(S5_G, S5_C, S5_P)),
                            -blocks_out(S['s5_c_im'].reshape(S5_G, S5_C, S5_P))], axis=0).astype(bf16)

    g1, g2n, g3, g4 = S['norm_mix_pre'], S['norm_mix_post'], S['norm_ffn_pre'], S['norm_ffn_post']
    (h1,) = _rowcall("norm_pre", lambda i, n, R, P, X, C: ((_rms(R[0], C[0]),), ()), L, TS, [x], [g1],
                     out_rows=[row(D_MODEL, bf16)])
    p = _mm(h1, w_p, 'nt', "mm_p")
    u = _mm(h1, w_u, 'nt', "mm_u")
    gp = _mm(h1, w_g, 'nt', "mm_g")

    def prep_fn(i, n, R, P, X, C):
        q = R[0] + (_shift_down(R[0], P[0], i, 1) - R[0]) * C[0]
        return _prep(q, *C[1:]), ()

    r, lw, k2, v, an, bv, g = _rowcall("rwkv_prep", prep_fn, L, TS, [p], prep_consts,
                                       out_rows=[row(RWKV_W)] * 7, prev=[0])
    y, ck = _wkv7_fwd(r, lw, k2, v, an, bv)
    (o_a,) = _rowcall("rwkv_out", lambda i, n, R, P, X, C: ((_rwkv_out(*R, *C),), ()), L, TS, [y, r, k2, v, g],
                      out_consts, out_rows=[row(RWKV_W, bf16)])
    o_r = _mm(o_a, W['w_branch_rwkv'], 'nn', "mm_br")

    st, ysc, got = _s5_fwd(u, bmat, cmat, abar, late)
    W = {**W, **_gather_pair(got, list(got), "gather_weights_pair_late")}
    (yg,) = _rowcall("s5_mid", lambda i, n, R, P, X, C: ((_s5_mid(*R, *C),), ()), L, TS, [ysc, u], [S['s5_d']],
                     out_rows=[row(S5_W)])
    z2 = _mm(yg, W['s5_w_glu'], 'nn', "mm_glu")
    (o_b,) = _rowcall("s5_glu", lambda i, n, R, P, X, C: ((_s5_glu(*R, *C),), ()), L, TS, [yg, z2], [S['s5_b_glu']],
                      out_rows=[row(S5_W, bf16)])
    o_s = _mm(o_b, W['w_branch_s5'], 'nn', "mm_bs")

    (merged,) = _rowcall("merge", lambda i, n, R, P, X, C: ((_merge(*R, *C),), ()), L, TS, [gp, o_r, o_s],
                         [S['b_gate']], out_rows=[row(D_MODEL, bf16)])
    mixed = _mm(merged, W['w_out'], 'nn', "mm_out")

    def resid_fn(i, n, R, P, X, C):
        x1_ = R[0] + _rms(R[1], C[0])
        return (x1_, _rms(x1_, C[1])), ()

    x1, h2 = _rowcall("resid_norm", resid_fn, L, TS, [x, mixed], [g2n, g3], out_rows=[row(D_MODEL), row(D_MODEL, bf16)])

    z = _mm(h2, W['ffn_w_up'], 'nn', "mm_up")

    def conv(zt, zprev, i, cw_, cb_):
        z2s, z1s = _shift_down(zt, zprev, i, 2), _shift_down(zt, zprev, i, 1)
        return cb_ + cw_[0:1] * z2s + cw_[1:2] * z1s + cw_[2:3] * zt, z2s, z1s

    (act,) = _rowcall("conv_act", lambda i, n, R, P, X, C: ((_act(conv(R[0], P[0], i, C[0], C[1])[0]),), ()), L, TMW,
                      [z], [cw, cb], out_rows=[row(D_FF, bf16)], prev=[0])
    f = _mm(act, W['ffn_w_down'], 'nn', "mm_down")

    def final_fn(i, n, R, P, X, C):
        x1_, f_, t_ = R
        fn_, vjp = jax.vjp(_rms, f_, C[0])
        diff = x1_ + fn_ - t_
        loss = jnp.sum(diff * diff) * (0.5 / D_MODEL)
        dx2_ = diff * (1.0 / D_MODEL)
        df_, dg4_ = vjp(dx2_)
        return (df_, dx2_), (jnp.full((1, PACK_W), loss, f32), dg4_)

    df, dx2, loss, dg4 = _rowcall("loss_head", final_fn, L, TS, [x1, f, tgt], [g4],
                                  out_rows=[row(D_MODEL, bf16), row(D_MODEL)], out_accs=[(1, PACK_W), (1, D_MODEL)])
    G = {'norm_ffn_post': dg4}

    dact = _mm(df, W['ffn_w_down'], 'nt', "mm_down_dx")
    G['ffn_w_down'] = _mm(act, df, 'tn', "mm_down_dw")

    def conv_bwd_fn(i, n, R, P, X, C):
        z_, dact_ = R
        cw_, cb_ = C
        zc, z2s, z1s = conv(z_, P[0], i, cw_, cb_)
        _, vjp = jax.vjp(_act, zc)
        (dzc_,) = vjp(dact_)
        last8 = z_[z_.shape[0] - 8:]
        zcn = cb_ + cw_[0:1] * _shift_down(X[0], last8, 1, 2) + cw_[1:2] * _shift_down(X[0], last8, 1, 1) + cw_[2:3] * X[0]
        _, vjpn = jax.vjp(_act, zcn)
        (dzcn,) = vjpn(X[1])
        dz_ = (cw_[2:3] * dzc_ + cw_[1:2] * _shift_up(dzc_, dzcn, i, n, 1) + cw_[0:1] * _shift_up(dzc_, dzcn, i, n, 2))
        return (dz_,), (_sum0(dzc_), _sum0(dzc_ * z2s), _sum0(dzc_ * z1s), _sum0(dzc_ * z_))

    wide = (1, 2 * D_FF)
    dz, dcb, dcw0, dcw1, dcw2 = _rowcall("conv_act_bwd", conv_bwd_fn, L, TMW, [z, dact], [cw, cb],
                                         out_rows=[row(2 * D_FF, bf16)], out_accs=[wide] * 4, prev=[0], nxt=[0, 1])
    G['ffn_conv_b'] = dcb
    G['ffn_conv_w'] = jnp.concatenate([dcw0, dcw1, dcw2], axis=0)
    dh2 = _mm(dz, W['ffn_w_up'], 'nt', "mm_up_dx")
    G['ffn_w_up'] = _mm(h2, dz, 'tn', "mm_up_dw")

    def norm2_bwd_fn(i, n, R, P, X, C):
        x1_, mixed_, dx2_, dh2_ = R
        _, vjp3 = jax.vjp(_rms, x1_, C[1])
        dx1a, dg3_ = vjp3(dh2_)
        dx1_ = dx2_ + dx1a
        _, vjp2 = jax.vjp(_rms, mixed_, C[0])
        dmixed_, dg2_ = vjp2(dx1_)
        return (dx1_, dmixed_), (dg2_, dg3_)

    dx1, dmixed, dg2n, dg3 = _rowcall("norm_mid_bwd", norm2_bwd_fn, L, TS, [x1, mixed, dx2, dh2], [g2n, g3],
                                      out_rows=[row(D_MODEL), row(D_MODEL, bf16)], out_accs=[(1, D_MODEL)] * 2)
    G['norm_mix_post'], G['norm_ffn_pre'] = dg2n, dg3

    dmerged = _mm(dmixed, W['w_out'], 'nt', "mm_out_dx")
    G['w_out'] = _mm(merged, dmixed, 'tn', "mm_out_dw")

    def merge_bwd_fn(i, n, R, P, X, C):
        _, vjp = jax.vjp(_merge, R[0], R[1], R[2], C[0])
        dgp_, do_r_, do_s_, dbg_ = vjp(R[3])
        return (dgp_, do_r_, do_s_), (dbg_,)

    dgp, do_r, do_s, G['b_gate'] = _rowcall("merge_bwd", merge_bwd_fn, L, TS, [gp, o_r, o_s, dmerged], [S['b_gate']],
                                            out_rows=[row(2 * D_MODEL, bf16), row(D_MODEL, bf16), row(D_MODEL, bf16)],
                                            out_accs=[(1, 2 * D_MODEL)])
    do_a = _mm(do_r, W['w_branch_rwkv'], 'nt', "mm_br_dx")
    G['w_branch_rwkv'] = _mm(o_a, do_r, 'tn', "mm_br_dw")
    do_b = _mm(do_s, W['w_branch_s5'], 'nt', "mm_bs_dx")
    G['w_branch_s5'] = _mm(o_b, do_s, 'tn', "mm_bs_dw")

    def glu_bwd_fn(i, n, R, P, X, C):
        _, vjp = jax.vjp(_s5_glu, R[0], R[1], C[0])
        dyg1_, dz2_, dbg_ = vjp(R[2])
        return (dyg1_, dz2_), (dbg_,)

    dyg1, dz2, G['s5_b_glu'] = _rowcall("s5_glu_bwd", glu_bwd_fn, L, TS, [yg, z2, do_b], [S['s5_b_glu']],
                                        out_rows=[row(S5_W), row(S5_W, bf16)], out_accs=[(1, S5_W)])
    dyg2 = _mm(dz2, W['s5_w_glu'], 'nt', "mm_glu_dx")
    G['s5_w_glu'] = _mm(yg, dz2, 'tn', "mm_glu_dw")

    def mid_bwd_fn(i, n, R, P, X, C):
        _, vjp = jax.vjp(_s5_mid, R[0], R[1], C[0])
        dysc_, du_, dd_ = vjp(R[2] + R[3])
        return (dysc_, du_), (dd_,)

    dysc, du1, G['s5_d'] = _rowcall("s5_mid_bwd", mid_bwd_fn, L, TS, [ysc, u, dyg1, dyg2], [S['s5_d']],
                                    out_rows=[row(S5_W, bf16), row(S5_W)], out_accs=[(1, S5_W)])
    early = [n for n in BIG if n != 'w_in']
    recv_e, _ = _grads_to_sibling(G, early, "grads_to_sibling_early")
    chip_e, _ = _pair_add(G, recv_e, early, "grads_pair_sum_early")
    du, dbmat, dcmat, dabar, slots_e = _s5_bwd(dysc, st, u, du1, bmat, cmat, abar, chip_e)
    da_re, da_im, dls, db_re, db_im = _s5_disc_bwd(
        a_re, a_im, ls, b_re, b_im, dabar[:, :S5_N].reshape(S5_N, 1), dabar[:, S5_N:].reshape(S5_N, 1),
        undiag_in(dbmat[:4]), undiag_in(dbmat[4:]), seg)
    G['s5_a_re'], G['s5_a_im'], G['s5_log_step'] = da_re, da_im, dls
    G['s5_b_re'], G['s5_b_im'] = db_re, db_im
    G['s5_c_re'], G['s5_c_im'] = undiag_out(dcmat[:4]), -undiag_out(dcmat[4:])

    def out_bwd_fn(i, n, R, P, X, C):
        _, vjp = jax.vjp(_rwkv_out, *R[:5], *C)
        gs = vjp(R[5])
        return gs[:5], gs[5:8]

    dy, dr1, dk1, dv1, dg, dlw, dlb, drk = _rowcall("rwkv_out_bwd", out_bwd_fn, L, TM, [y, r, k2, v, g, do_a], out_consts,
                                                    out_rows=[row(RWKV_W)] * 5, out_accs=[(1, RWKV_W)] * 3)
    G['rwkv_lnx_w'], G['rwkv_lnx_b'], G['rwkv_r_k'] = dlw, dlb, drk
    dr2, dlwk, dk2b, dv2, dan, dbv = _wkv7_bwd(r, lw, k2, v, an, bv, ck, dy)

    def prep_bwd_fn(i, n, R, P, X, C):
        p_ = R[0]
        d1 = _shift_down(p_, P[0], i, 1) - p_
        q = p_ + d1 * C[0]
        _, vjp = jax.vjp(_prep, q, *C[1:])
        cots = (R[1] + R[2], R[3], R[4] + R[5], R[6] + R[7], R[8], R[9], R[10])
        gs = vjp(cots)
        return (gs[0],), (_sum0(gs[0] * d1),) + tuple(gs[1:8])

    small, lowr = (1, RWKV_W), (128, RWKV_W)
    dq, dmu, dw0, da0, dkk, dka, dw2p, da2p, dg2 = _rowcall(
        "rwkv_prep_bwd", prep_bwd_fn, L, TM, [p, dr1, dr2, dlwk, dk1, dk2b, dv1, dv2, dan, dbv, dg],
        prep_consts, out_rows=[row(N_RWKV)], out_accs=[(1, N_RWKV)] + [small] * 4 + [lowr] * 3, prev=[0])
    G['rwkv_shift_mu'], G['rwkv_w0'], G['rwkv_a0'], G['rwkv_k_k'], G['rwkv_k_a'] = dmu, dw0, da0, dkk, dka
    G['rwkv_w2'], G['rwkv_a2'], G['rwkv_g2'] = dw2p[:64], da2p[64:], dg2

    def shift_bwd_fn(i, n, R, P, X, C):
        dm = R[0] * C[0]
        return (R[0] - dm + _shift_up(dm, X[0] * C[0], i, n, 1),), ()

    (dp,) = _rowcall("shift_bwd", shift_bwd_fn, L, TS, [dq], [S['rwkv_shift_mu']], out_rows=[row(N_RWKV, bf16)], nxt=[0])

    dproj = jnp.concatenate([dp, du, dgp], axis=1)
    dh1 = _mm(dproj, w_in_t, 'nn', "mm_in_dx")
    G['w_in'] = _mm(dproj, h1, 'tn', "mm_in_dw")

    def norm1_bwd_fn(i, n, R, P, X, C):
        _, vjp = jax.vjp(_rms, R[0], C[0])
        dxa, dg1_ = vjp(R[2])
        return (R[1] + dxa,), (dg1_,)

    dx, G['norm_mix_pre'] = _rowcall("norm_pre_bwd", norm1_bwd_fn, L, TS, [x, dx1, dh1], [g1],
                                     out_rows=[row(D_MODEL)], out_accs=[(1, D_MODEL)])
    return loss, dx, G, chip_e, slots_e


def kernel(x, norm_mix_pre, norm_mix_post, norm_ffn_pre, norm_ffn_post, w_in, b_gate, rwkv_shift_mu, rwkv_w0, rwkv_w2, rwkv_a0, rwkv_a2, rwkv_g2, rwkv_k_k, rwkv_k_a, rwkv_r_k, rwkv_lnx_w, rwkv_lnx_b, s5_a_re, s5_a_im, s5_b_re, s5_b_im, s5_c_re, s5_c_im, s5_d, s5_log_step, s5_w_glu, s5_b_glu, w_branch_rwkv, w_branch_s5, w_out, ffn_w_up, ffn_conv_w, ffn_conv_b, ffn_w_down, loss_target, m_norm_mix_pre, m_norm_mix_post, m_norm_ffn_pre, m_norm_ffn_post, m_w_in, m_b_gate, m_rwkv_shift_mu, m_rwkv_w0, m_rwkv_w2, m_rwkv_a0, m_rwkv_a2, m_rwkv_g2, m_rwkv_k_k, m_rwkv_k_a, m_rwkv_r_k, m_rwkv_lnx_w, m_rwkv_lnx_b, m_s5_a_re, m_s5_a_im, m_s5_b_re, m_s5_b_im, m_s5_c_re, m_s5_c_im, m_s5_d, m_s5_log_step, m_s5_w_glu, m_s5_b_glu, m_w_branch_rwkv, m_w_branch_s5, m_w_out, m_ffn_w_up, m_ffn_conv_w, m_ffn_conv_b, m_ffn_w_down, v_norm_mix_pre, v_norm_mix_post, v_norm_ffn_pre, v_norm_ffn_post, v_w_in, v_b_gate, v_rwkv_shift_mu, v_rwkv_w0, v_rwkv_w2, v_rwkv_a0, v_rwkv_a2, v_rwkv_g2, v_rwkv_k_k, v_rwkv_k_a, v_rwkv_r_k, v_rwkv_lnx_w, v_rwkv_lnx_b, v_s5_a_re, v_s5_a_im, v_s5_b_re, v_s5_b_im, v_s5_c_re, v_s5_c_im, v_s5_d, v_s5_log_step, v_s5_w_glu, v_s5_b_glu, v_w_branch_rwkv, v_w_branch_s5, v_w_out, v_ffn_w_up, v_ffn_conv_w, v_ffn_conv_b, v_ffn_w_down):
    A = dict(locals())
    me = 2 * lax.axis_index("x") + lax.axis_index("y")
    blk = lambda n: A[n][0]

    mine = {n: (blk(n).T if n == 'w_in' else blk(n)).astype(bf16) for n in BIG}
    mine.update({n: blk(n) for n in TINY})
    mine['ffn_conv_w'] = jnp.pad(blk('ffn_conv_w'), ((0, 5), (0, 0)))
    late = ['ffn_w_up', 'ffn_w_down']
    W = _gather_weights({n: blkv for n, blkv in mine.items() if n not in late})
    W.update(_gather_pair(W, [n for n in BIG if n not in late], "gather_weights_pair"))
    S = {n: A[n].reshape(1, -1) for n in SMALL}

    loss, dx, G, chip_e, slots_e = _forward_backward(x[0], loss_target[0], W, S, {n: mine[n] for n in late})

    tiny_shapes = [G[n].shape for n in TINY]
    small_buf = _pack_rows([G[n] for n in SMALL] + [G[n] for n in TINY] + [loss], SMALL_ROWS)
    recv, small_recv = _grads_to_sibling(G, ['w_in'], "grads_to_sibling", small_buf)
    chip_l, small_sum = _pair_add(G, recv, ['w_in'], "grads_pair_sum", small_buf, small_recv)
    slots_l, small4 = _grads_chip_exchange(chip_l, ['w_in'], small_sum)
    half, half['small'] = _sum_slots({**slots_e, **slots_l}, {**chip_e, **chip_l}, small4, small_sum)
    other = _halves_to_sibling(half)
    pc = lax.axis_index("c")
    small_tot = _join_halves(half['small'], other['small'], pc)
    grad = {n: _join_halves(half[n], other[n], pc) for n in BIG}
    grad['w_in'] = grad['w_in'].T
    vals = _unpack_rows(small_tot, [A[n].shape for n in SMALL] + tiny_shapes + [(1, PACK_W)])
    grad.update(zip(SMALL, vals))
    for n, full in zip(TINY, vals[len(SMALL):]):
        cs = A[n].shape[2]
        grad[n] = lax.dynamic_slice_in_dim(full, me * cs, cs, axis=1)
    loss_out = vals[-1][0, 0]

    packed = SMALL + TINY
    groups = [(blk(n), grad[n], blk('m_' + n), blk('v_' + n)) for n in BIG]
    groups.append(tuple(_pack_rows([src(n) for n in packed], ADAM_ROWS)
                        for src in (lambda n: A[n], lambda n: grad[n], lambda n: A['m_' + n], lambda n: A['v_' + n])))
    res = _adamw(groups)
    outs = [dict(), dict(), dict()]
    for n, r3 in zip(BIG, res[:-1]):
        for d, val in zip(outs, r3):
            d[n] = val
    for d, buf in zip(outs, res[-1]):
        d.update(zip(packed, _unpack_rows(buf, [A[n].shape for n in packed])))
    full = lambda d: [d[n].reshape(A[n].shape) for n in WEIGHTS]
    return (loss_out, dx[None], *full(grad), *full(outs[0]), *full(outs[1]), *full(outs[2]))
```

```python
import functools

import jax
import jax.numpy as jnp
from jax import lax
from jax.experimental import pallas as pl
from jax.experimental.pallas import tpu as pltpu

f32, bf16 = jnp.float32, jnp.bfloat16
MESH = pl.DeviceIdType.MESH

D_MODEL = 1024
RWKV_W = 512
HEADS, HEAD = 8, 64
N_RWKV = 1792
S5_W = 512
S5_G, S5_P, S5_C = 32, 64, 16
S5_N = S5_G * S5_P
D_FF = 2816
NORM_EPS = 1e-6
LNX_EPS = 64e-5
ADAM_LR, ADAM_B1, ADAM_B2, ADAM_EPS, ADAM_WD, ADAM_STEP = 0.001, 0.9, 0.999, 1e-08, 0.01, 10

VMEM_LIMIT = 48 * 1024 * 1024
PACK_W = 1024
WKV_C = 64
WKV_SUB = 2
WKV_ROWS = WKV_C * WKV_SUB
RESIDENT_BUDGET = 40 * 1024 * 1024
S5_T = 256

WEIGHTS = ['norm_mix_pre', 'norm_mix_post', 'norm_ffn_pre', 'norm_ffn_post', 'w_in', 'b_gate', 'rwkv_shift_mu',
           'rwkv_w0', 'rwkv_w2', 'rwkv_a0', 'rwkv_a2', 'rwkv_g2', 'rwkv_k_k', 'rwkv_k_a', 'rwkv_r_k', 'rwkv_lnx_w',
           'rwkv_lnx_b', 's5_a_re', 's5_a_im', 's5_b_re', 's5_b_im', 's5_c_re', 's5_c_im', 's5_d', 's5_log_step',
           's5_w_glu', 's5_b_glu', 'w_branch_rwkv', 'w_branch_s5', 'w_out', 'ffn_w_up', 'ffn_conv_w', 'ffn_conv_b',
           'ffn_w_down']


def _ceil_to(n, m):
    return -(-n // m) * m


def _mesh_pos():
    return lax.axis_index("x"), lax.axis_index("y"), lax.axis_index("c")


def _pick(d, cap=4096):
    for c in (1024, 1408, 2176, 896, 512, 256, 128):
        if c <= cap and d % c == 0:
            return c
    raise ValueError(d)


def _mm_resident(a, w, mode, name, M, N, K, out_dtype):
    budget = RESIDENT_BUDGET - 2 * K * N
    tm = next(t for t in (512, 256, 128) if 2 * t * (K * a.dtype.itemsize + 4 * N) <= budget)
    dims = _DIMS[mode]

    def body(a_ref, w_ref, o_ref):
        o_ref[...] = lax.dot_general(a_ref[...].astype(bf16), w_ref[...], (dims, ((), ())),
                                     preferred_element_type=f32).astype(o_ref.dtype)

    return pl.pallas_call(
        body, name=name, grid=(M // tm,),
        in_specs=[pl.BlockSpec((tm, K), lambda i: (i, 0)),
                  pl.BlockSpec(w.shape, lambda i: (0, 0), pipeline_mode=pl.Buffered(1))],
        out_specs=pl.BlockSpec((tm, N), lambda i: (i, 0)), out_shape=jax.ShapeDtypeStruct((M, N), out_dtype),
        compiler_params=pltpu.CompilerParams(dimension_semantics=("parallel",), vmem_limit_bytes=VMEM_LIMIT),
    )(a, w)


def _mm(a, b, mode, name, out_dtype=f32):
    if mode == 'tn':
        (K, M), (K2, N) = a.shape, b.shape
    elif mode == 'nt':
        (M, K), (N, K2) = a.shape, b.shape
    else:
        (M, K), (K2, N) = a.shape, b.shape
    assert K == K2, (name, a.shape, b.shape)
    if mode != 'tn' and b.dtype == bf16:
        return _mm_resident(a, b, mode, name, M, N, K, out_dtype)
    if mode == 'tn':
        tm = _pick(M, 2176)
        tn = _pick(N, 512 if tm > 1408 else (1024 if tm > 1024 else 1408))
        tk = _pick(K, 1024 if a.dtype == bf16 and b.dtype == bf16 else 512)
    else:
        tm, tn, tk = _pick(M, 512), _pick(N), _pick(K)
    nk = K // tk
    dims = {'nn': ((1,), (0,)), 'nt': ((1,), (1,)), 'tn': ((0,), (0,))}[mode]

    def body(a_ref, b_ref, o_ref, acc_ref):
        k = pl.program_id(2)

        @pl.when(k == 0)
        def _():
            acc_ref[...] = jnp.zeros_like(acc_ref)

        acc_ref[...] += lax.dot_general(a_ref[...].astype(bf16), b_ref[...].astype(bf16), (dims, ((), ())),
                                        preferred_element_type=f32)

        @pl.when(k == nk - 1)
        def _():
            o_ref[...] = acc_ref[...].astype(o_ref.dtype)

    a_spec = pl.BlockSpec((tk, tm), lambda i, j, k: (k, i)) if mode == 'tn' else pl.BlockSpec((tm, tk), lambda i, j, k: (i, k))
    b_spec = pl.BlockSpec((tn, tk), lambda i, j, k: (j, k)) if mode == 'nt' else pl.BlockSpec((tk, tn), lambda i, j, k: (k, j))
    return pl.pallas_call(
        body, name=name, grid=(M // tm, N // tn, nk),
        in_specs=[a_spec, b_spec], out_specs=pl.BlockSpec((tm, tn), lambda i, j, k: (i, j)),
        out_shape=jax.ShapeDtypeStruct((M, N), out_dtype),
        scratch_shapes=[pltpu.VMEM((tm, tn), f32)],
        compiler_params=pltpu.CompilerParams(dimension_semantics=("parallel", "parallel", "arbitrary"),
                                             vmem_limit_bytes=VMEM_LIMIT),
    )(a, b)


def _rowcall(name, fn, L, tm, rows, consts=(), out_rows=(), out_accs=(), prev=(), nxt=()):
    nsteps = L // tm
    nb8 = tm // 8
    last8 = L // 8 - 1
    n_r, n_p, n_x, n_c, n_or = len(rows), len(prev), len(nxt), len(consts), len(out_rows)

    def body(*refs):
        i = pl.program_id(0)
        vals = [r[...] for r in refs[:n_r + n_p + n_x + n_c]]
        R, P = vals[:n_r], vals[n_r:n_r + n_p]
        X, C = vals[n_r + n_p:n_r + n_p + n_x], vals[n_r + n_p + n_x:]
        o_refs = refs[n_r + n_p + n_x + n_c:]
        outs_r, outs_a = fn(i, nsteps, R, P, X, C)
        for ref, v in zip(o_refs[:n_or], outs_r, strict=True):
            ref[...] = v.astype(ref.dtype)
        if out_accs:
            @pl.when(i == 0)
            def _():
                for ref in o_refs[n_or:]:
                    ref[...] = jnp.zeros_like(ref)

            for ref, v in zip(o_refs[n_or:], outs_a, strict=True):
                ref[...] += v

    def const_spec(c):
        nd = c.ndim
        return pl.BlockSpec(c.shape, lambda i: (0,) * nd)

    in_specs = ([pl.BlockSpec((tm, a.shape[1]), lambda i: (i, 0)) for a in rows]
                + [pl.BlockSpec((8, rows[j].shape[1]), lambda i: (jnp.maximum(i * nb8 - 1, 0), 0)) for j in prev]
                + [pl.BlockSpec((8, rows[j].shape[1]), lambda i: (jnp.minimum((i + 1) * nb8, last8), 0)) for j in nxt]
                + [const_spec(c) for c in consts])
    out_specs = ([pl.BlockSpec((tm, c), lambda i: (i, 0)) for c, _ in out_rows]
                 + [pl.BlockSpec(s, lambda i: (0, 0)) for s in out_accs])
    out_shape = ([jax.ShapeDtypeStruct((L, c), dt) for c, dt in out_rows]
                 + [jax.ShapeDtypeStruct(s, f32) for s in out_accs])
    args = list(rows) + [rows[j] for j in prev] + [rows[j] for j in nxt] + list(consts)
    return pl.pallas_call(
        body, name=name, grid=(nsteps,), in_specs=in_specs, out_specs=out_specs, out_shape=out_shape,
        compiler_params=pltpu.CompilerParams(dimension_semantics=("arbitrary",), vmem_limit_bytes=VMEM_LIMIT),
    )(*args)


def _shift_down(x, prev8, i, k):
    rolled = pltpu.roll(x, k, axis=0)
    pfix = jnp.where(i > 0, pltpu.roll(prev8, k, axis=0), 0.0)
    row8 = lax.broadcasted_iota(jnp.int32, pfix.shape, 0)
    top = jnp.where(row8 < k, pfix, rolled[:8])
    return top if x.shape[0] == 8 else jnp.concatenate([top, rolled[8:]], axis=0)


def _shift_up(x, next8, i, nsteps, k):
    tm = x.shape[0]
    rolled = pltpu.roll(x, tm - k, axis=0)
    nfix = jnp.where(i < nsteps - 1, pltpu.roll(next8, 8 - k, axis=0), 0.0)
    row8 = lax.broadcasted_iota(jnp.int32, nfix.shape, 0)
    bot = jnp.where(row8 >= 8 - k, nfix, rolled[tm - 8:])
    return jnp.concatenate([rolled[:tm - 8], bot], axis=0)


def _sum0(x):
    return jnp.sum(x, axis=0, keepdims=True)


def _rms(x, g):
    return x * lax.rsqrt(jnp.mean(x * x, axis=-1, keepdims=True) + NORM_EPS) * g


def _softplus(x):
    return jnp.maximum(x, 0.0) + jnp.log(1.0 + jnp.exp(-jnp.abs(x)))


def _gelu(x):
    return 0.5 * x * (1.0 + jnp.tanh(0.7978845608028654 * (x + 0.044715 * x * x * x)))


def _dot32(a, b):
    return jnp.dot(a, b, preferred_element_type=f32, precision=lax.Precision.HIGHEST)


def _seg_raw(x, E):
    hi = x.astype(bf16)
    r1 = x - hi.astype(f32)
    mid = r1.astype(bf16)
    lo = (r1 - mid.astype(f32)).astype(bf16)
    Eb = E.astype(bf16)
    dot = lambda t: jnp.dot(t, Eb, preferred_element_type=f32)
    return (dot(lo) + dot(mid)) + dot(hi)


@jax.custom_vjp
def _seg(x, E):
    return _seg_raw(x, E)


_seg.defvjp(lambda x, E: (_seg_raw(x, E), E), lambda E, g: (_seg_raw(g, E), jnp.zeros_like(E)))


def _prep(q, w0, a0, k_k, k_a, w2p, a2p, g2, E):
    r, k, v = q[:, 0:512], q[:, 512:1024], q[:, 1024:1536]
    wa, gd = q[:, 1536:1664], q[:, 1664:1792]
    wlog = -_softplus(-(w0 + _bdot(jnp.tanh(wa), w2p, 'nn'))) - 0.5
    lw = -jnp.exp(wlog)
    a = jax.nn.sigmoid(a0 + _bdot(wa, a2p, 'nn'))
    g = _bdot(jax.nn.sigmoid(gd), g2, 'nn')
    kk = k * k_k
    kkn = kk / jnp.maximum(jnp.sqrt(_seg(kk * kk, E)), 1e-12)
    k2 = k * (1.0 + (a - 1.0) * k_a)
    return r, lw, k2, v, -kkn, kkn * a, g


def _rwkv_out(y, r, k2, v, g, lnx_w, lnx_b, r_k, E):
    mean = _seg(y, E) * (1.0 / HEAD)
    yc = y - mean
    var = _seg(yc * yc, E) * (1.0 / HEAD)
    yn = yc * lax.rsqrt(var + LNX_EPS) * lnx_w + lnx_b
    bonus = _seg(r * k2 * r_k, E) * v
    return (yn + bonus) * g


def _s5_mid(ysc, u, d):
    return _gelu(ysc + d * u)


def _s5_glu(yg, z2, b_glu):
    return yg * jax.nn.sigmoid(z2 + b_glu)


def _merge(gp, o_r, o_s, b_gate):
    gates = jax.nn.sigmoid(gp + b_gate)
    return gates[:, :D_MODEL] * o_r + gates[:, D_MODEL:] * o_s


def _act(zc):
    return _gelu(zc[:, :D_FF]) * zc[:, D_FF:]


def _s5_disc(a_re, a_im, ls, b_re, b_im):
    dt = jnp.exp(ls)
    er = jnp.exp(a_re * dt)
    ar, ai = er * jnp.cos(a_im * dt), er * jnp.sin(a_im * dt)
    x, y = ar - 1.0, ai
    den = a_re * a_re + a_im * a_im
    fr, fi = (x * a_re + y * a_im) / den, (y * a_re - x * a_im) / den
    return ar, ai, fr * b_re - fi * b_im, fr * b_im + fi * b_re


_DIMS = {'nn': ((1,), (0,)), 'nt': ((1,), (1,)), 'tn': ((0,), (0,))}


def _raw_bdot(a, b, mode):
    return lax.dot_general(a.astype(bf16), b.astype(bf16), (_DIMS[mode], ((), ())), preferred_element_type=f32)


@functools.partial(jax.custom_vjp, nondiff_argnums=(2,))
def _bdot(a, b, mode):
    return _raw_bdot(a, b, mode)


def _bdot_fwd(a, b, mode):
    return _raw_bdot(a, b, mode), (a, b)


def _bdot_bwd(mode, res, g):
    a, b = res
    if mode == 'nn':
        return _raw_bdot(g, b, 'nt'), _raw_bdot(a, g, 'tn')
    if mode == 'nt':
        return _raw_bdot(g, b, 'nn'), _raw_bdot(g, a, 'tn')
    return _raw_bdot(b, g, 'nt'), _raw_bdot(a, g, 'nn')


_bdot.defvjp(_bdot_fwd, _bdot_bwd)


def _wkv_chunk(S0, r, lw, k, v, a, b, tri, bd):
    C = r[0].shape[0]
    P = range(len(r))
    lane = lax.broadcasted_iota(jnp.int32, (1, 2 * HEAD), 1)
    m0, m1 = (lane < HEAD).astype(f32), (lane >= HEAD).astype(f32)
    cat = lambda *xs: jnp.concatenate(xs, axis=0)
    stack = lambda x: cat(x * m0, x * m1)
    unstack = lambda x2: m0 * x2[:C] + m1 * x2[C:]
    rid = lax.broadcasted_iota(jnp.int32, (2 * C, 2 * C), 0)
    cid = lax.broadcasted_iota(jnp.int32, (2 * C, 2 * C), 1)
    same = (rid < C) == (cid < C)
    eye2 = (rid == cid).astype(f32)
    tri2 = (same & (rid >= cid)).astype(f32)
    sl2 = tri2 - eye2
    cum = [_dot32(tri, lw[p]) for p in P]
    g = [jnp.exp(cum[p]) for p in P]
    gi = [jnp.exp(-cum[p]) for p in P]
    at = [a[p] * jnp.exp(cum[p] - lw[p]) for p in P]
    rt = [r[p] * g[p] for p in P]
    kb = [k[p] * gi[p] for p in P]
    bb = [b[p] * gi[p] for p in P]
    lhs = [cat(stack(at[p]), stack(rt[p])) for p in P]
    pb = [_bdot(lhs[p], stack(bb[p]), 'nt') for p in P]
    pk = [_bdot(lhs[p], stack(kb[p]), 'nt') for p in P]
    aab = [pb[p][:2 * C] * sl2 for p in P]
    base = [_bdot(cat(at[p], rt[p]), S0[p], 'nt') for p in P]
    t = [_bdot(cat(pk[p][:2 * C] * sl2, pk[p][2 * C:] * tri2), cat(v[p], v[p]), 'nn') for p in P]
    rhs = [cat(base[p][:C], base[p][:C]) + t[p][:2 * C] for p in P]
    x = [eye2 + aab[p] for p in P]
    pw = aab
    n = 1
    while 2 * n < C:
        pw = [_bdot(pw[p], pw[p], 'nn') for p in P]
        x = [x[p] + _bdot(x[p], pw[p], 'nn') for p in P]
        n *= 2
    u = [unstack(_bdot(x[p], rhs[p], 'nn')) for p in P]
    w2 = [_bdot(pb[p][2 * C:] * tri2, cat(u[p], u[p]), 'nn') for p in P]
    y = [base[p][C:] + unstack(t[p][2 * C:]) + unstack(w2[p]) for p in P]
    S1 = [g[p][C - 1:C, :] * (S0[p] + bd * _bdot(cat(v[p], u[p]), cat(kb[p], bb[p]), 'tn')) for p in P]
    return y, S1


def _pairs(x):
    return [x[:, 2 * HEAD * p:2 * HEAD * (p + 1)] for p in range(HEADS // 2)]


def _wkv_consts():
    tri = jnp.tril(jnp.ones((WKV_C, WKV_C), f32))
    hid = jnp.arange(2 * HEAD) // HEAD
    return tri, (hid[:, None] == hid[None, :]).astype(f32)


def _wkv_step(S0, r, lw, k, v, a, b, tri, bd):
    ys, S = [], S0
    for c in range(WKV_SUB):
        sub = lambda xs: [x[c * WKV_C:(c + 1) * WKV_C] for x in xs]
        y, S = _wkv_chunk(S, sub(r), sub(lw), sub(k), sub(v), sub(a), sub(b), tri, bd)
        ys.append(y)
    return [jnp.concatenate([y[p] for y in ys], axis=0) for p in range(len(S0))], S


def _wkv7_fwd(r, lw, k, v, a, b):
    L = r.shape[0]
    nc, npair = L // WKV_ROWS, HEADS // 2

    def body(r_ref, lw_ref, k_ref, v_ref, a_ref, b_ref, tri_ref, bd_ref, y_ref, ck_ref, s_ref):
        @pl.when(pl.program_id(0) == 0)
        def _():
            s_ref[...] = jnp.zeros_like(s_ref)

        s0 = [s_ref[p] for p in range(npair)]
        for p in range(npair):
            ck_ref[0, p] = s0[p]
        y, s1 = _wkv_step(s0, *(_pairs(x) for x in (r_ref, lw_ref, k_ref, v_ref, a_ref, b_ref)), tri_ref[...], bd_ref[...])
        for p in range(npair):
            y_ref[:, 2 * HEAD * p:2 * HEAD * (p + 1)] = y[p]
            s_ref[p] = s1[p]

    row = pl.BlockSpec((WKV_ROWS, RWKV_W), lambda c: (c, 0))
    sspec = pl.BlockSpec((1, npair, 2 * HEAD, 2 * HEAD), lambda c: (c, 0, 0, 0))
    return pl.pallas_call(
        body, name="wkv7_fwd", grid=(nc,),
        in_specs=[row] * 6 + [pl.BlockSpec((WKV_C, WKV_C), lambda c: (0, 0)), pl.BlockSpec((2 * HEAD, 2 * HEAD), lambda c: (0, 0))],
        out_specs=[row, sspec],
        out_shape=[jax.ShapeDtypeStruct((L, RWKV_W), f32), jax.ShapeDtypeStruct((nc, npair, 2 * HEAD, 2 * HEAD), f32)],
        scratch_shapes=[pltpu.VMEM((npair, 2 * HEAD, 2 * HEAD), f32)],
        compiler_params=pltpu.CompilerParams(dimension_semantics=("arbitrary",), vmem_limit_bytes=VMEM_LIMIT),
    )(r, lw, k, v, a, b, *_wkv_consts())


def _wkv7_bwd(r, lw, k, v, a, b, ck, dy):
    L = r.shape[0]
    nc, npair = L // WKV_ROWS, HEADS // 2

    def body(r_ref, lw_ref, k_ref, v_ref, a_ref, b_ref, ck_ref, dy_ref, tri_ref, bd_ref,
             dr_ref, dlw_ref, dk_ref, dv_ref, da_ref, db_ref, ds_ref):
        @pl.when(pl.program_id(0) == 0)
        def _():
            ds_ref[...] = jnp.zeros_like(ds_ref)

        tri, bd = tri_ref[...], bd_ref[...]
        ins = [[ck_ref[0, p] for p in range(npair)]] + [_pairs(x) for x in (r_ref, lw_ref, k_ref, v_ref, a_ref, b_ref)]
        _, vjp = jax.vjp(lambda *t: _wkv_step(*t, tri, bd), *ins)
        gs = vjp((_pairs(dy_ref), [ds_ref[p] for p in range(npair)]))
        for p in range(npair):
            ds_ref[p] = gs[0][p]
            for ref, gval in zip((dr_ref, dlw_ref, dk_ref, dv_ref, da_ref, db_ref), gs[1:]):
                ref[:, 2 * HEAD * p:2 * HEAD * (p + 1)] = gval[p]

    row = pl.BlockSpec((WKV_ROWS, RWKV_W), lambda c: (nc - 1 - c, 0))
    sspec = pl.BlockSpec((1, npair, 2 * HEAD, 2 * HEAD), lambda c: (nc - 1 - c, 0, 0, 0))
    return pl.pallas_call(
        body, name="wkv7_bwd", grid=(nc,),
        in_specs=[row] * 6 + [sspec, row, pl.BlockSpec((WKV_C, WKV_C), lambda c: (0, 0)),
                              pl.BlockSpec((2 * HEAD, 2 * HEAD), lambda c: (0, 0))],
        out_specs=[row] * 6,
        out_shape=[jax.ShapeDtypeStruct((L, RWKV_W), f32)] * 6,
        scratch_shapes=[pltpu.VMEM((npair, 2 * HEAD, 2 * HEAD), f32)],
        compiler_params=pltpu.CompilerParams(dimension_semantics=("arbitrary",), vmem_limit_bytes=VMEM_LIMIT),
    )(r, lw, k, v, a, b, ck, dy, *_wkv_consts())


def _cmul(ar, ai, xr, xi):
    return ar * xr - ai * xi, ar * xi + ai * xr


def _scan_init(a_ref, car_ref, pw_ref, reverse):
    car_ref[...] = jnp.zeros_like(car_ref)
    ar = jnp.broadcast_to(a_ref[:, :S5_N], (8, S5_N))
    ai = jnp.broadcast_to(a_ref[:, S5_N:], (8, S5_N))
    if reverse:
        ai = -ai
    row = lax.broadcasted_iota(jnp.int32, (8, S5_N), 0)
    pr, pi = ar, ai
    qr, qi = jnp.zeros((8, S5_N), f32), jnp.zeros((8, S5_N), f32)
    for e in range(1, 9):
        sel = (row == 8 - e) if reverse else (row == e - 1)
        qr, qi = jnp.where(sel, pr, qr), jnp.where(sel, pi, qi)
        if e in (1, 2, 4):
            j = (1, 2, 4).index(e)
            pw_ref[j, :, :S5_N] = pr
            pw_ref[j, :, S5_N:] = pi
        pr, pi = _cmul(pr, pi, ar, ai)
    pw_ref[3, :, :S5_N] = qr
    pw_ref[3, :, S5_N:] = qi


def _scan_tile(x_ref, o_ref, car_ref, pw_ref, reverse):
    ng = x_ref.shape[0] // 8
    row = lax.broadcasted_iota(jnp.int32, (8, S5_N), 0)

    def group(gi, carry):
        g = (ng - 1 - gi) if reverse else gi
        t0 = pl.multiple_of(g * 8, 8)
        xr, xi = x_ref[pl.ds(t0, 8), :S5_N], x_ref[pl.ds(t0, 8), S5_N:]
        for j, d in enumerate((1, 2, 4)):
            if reverse:
                sr = jnp.where(row < 8 - d, pltpu.roll(xr, 8 - d, axis=0), 0.0)
                si = jnp.where(row < 8 - d, pltpu.roll(xi, 8 - d, axis=0), 0.0)
            else:
                sr = jnp.where(row >= d, pltpu.roll(xr, d, axis=0), 0.0)
                si = jnp.where(row >= d, pltpu.roll(xi, d, axis=0), 0.0)
            mr, mi = _cmul(pw_ref[j, :, :S5_N], pw_ref[j, :, S5_N:], sr, si)
            xr, xi = xr + mr, xi + mi
        cr, ci = carry
        mr, mi = _cmul(pw_ref[3, :, :S5_N], pw_ref[3, :, S5_N:], cr, ci)
        xr, xi = xr + mr, xi + mi
        o_ref[pl.ds(t0, 8), :S5_N] = xr
        o_ref[pl.ds(t0, 8), S5_N:] = xi
        e = 0 if reverse else 7
        return (jnp.broadcast_to(xr[e:e + 1, :], (8, S5_N)), jnp.broadcast_to(xi[e:e + 1, :], (8, S5_N)))

    cr, ci = lax.fori_loop(0, ng, group, (car_ref[:, :S5_N], car_ref[:, S5_N:]))
    car_ref[:, :S5_N] = cr
    car_ref[:, S5_N:] = ci


_CB, _SB = 128, 512


def _cblk(k):
    return slice(_CB * k, _CB * (k + 1))


def _sblk(j):
    return slice(_SB * j, _SB * (j + 1))


def _s5_fwd(u, bmat, cmat, abar, late):
    L = u.shape[0]
    nt = L // S5_T
    names = list(late)
    nh = len(names)

    def body(u_ref, b_ref, c_ref, a_ref, *rest):
        h_in, (st_ref, y_ref), h_out = rest[:nh], rest[nh:nh + 2], rest[nh + 2:2 * nh + 2]
        bu_ref, car_ref, pw_ref, ssem, rsem, lsem = rest[2 * nh + 2:]
        i = pl.program_id(0)

        def copies():
            px, py, pc = _mesh_pos()
            me = 2 * px + py
            out = []
            for a, nm in enumerate(names):
                hr = late[nm].shape[0] // 2
                src, dst = h_in[a].at[pl.ds(pl.multiple_of(pc * hr, 16), hr), :], _slab(h_out[a], nm, me, pc)
                out.append(pltpu.make_async_copy(src, dst, lsem.at[a]))
                out += [pltpu.make_async_remote_copy(src, dst, ssem.at[3 * a + k], rsem.at[3 * a + k],
                                                     device_id=(qx, qy, pc), device_id_type=MESH)
                        for k, (qx, qy) in enumerate(_chip_peers(px, py))]
            return out

        @pl.when(i == 0)
        def _():
            _scan_init(a_ref, car_ref, pw_ref, False)
            for cp in copies():
                cp.start()

        for j in range(8):
            bu_ref[:, _sblk(j)] = _raw_bdot(u_ref[:, _cblk(j % 4)], b_ref[j], 'nn')
        _scan_tile(bu_ref, st_ref, car_ref, pw_ref, False)
        for k in range(4):
            y_ref[:, _cblk(k)] = (_raw_bdot(st_ref[:, _sblk(k)], c_ref[k], 'nn')
                                  + _raw_bdot(st_ref[:, _sblk(4 + k)], c_ref[4 + k], 'nn'))

        @pl.when(i == nt - 1)
        def _():
            for cp in copies():
                cp.wait()

    whole = lambda shape: pl.BlockSpec(shape, lambda i: (0,) * len(shape))
    outs = pl.pallas_call(
        body, name="s5_fwd", grid=(nt,),
        in_specs=[pl.BlockSpec((S5_T, S5_W), lambda i: (i, 0)), whole(bmat.shape), whole(cmat.shape), whole(abar.shape)]
        + [ANY] * nh,
        out_specs=[pl.BlockSpec((S5_T, 2 * S5_N), lambda i: (i, 0)), pl.BlockSpec((S5_T, S5_W), lambda i: (i, 0))] + [ANY] * nh,
        out_shape=[jax.ShapeDtypeStruct((L, 2 * S5_N), f32), jax.ShapeDtypeStruct((L, S5_W), f32)]
        + [jax.ShapeDtypeStruct(GATHER[nm][0], late[nm].dtype) for nm in names],
        scratch_shapes=[pltpu.VMEM((S5_T, 2 * S5_N), f32), pltpu.VMEM((8, 2 * S5_N), f32), pltpu.VMEM((4, 8, 2 * S5_N), f32),
                        pltpu.SemaphoreType.DMA((3 * nh,)), pltpu.SemaphoreType.DMA((3 * nh,)), pltpu.SemaphoreType.DMA((nh,))],
        compiler_params=pltpu.CompilerParams(dimension_semantics=("arbitrary",), vmem_limit_bytes=VMEM_LIMIT),
    )(u, bmat, cmat, abar, *[late[nm] for nm in names])
    return outs[0], outs[1], dict(zip(names, outs[2:]))


def _s5_bwd(dy, st, u, du_direct, bmat, cmat, abar, chip_sum):
    L = u.shape[0]
    nt = L // S5_T
    nb8 = S5_T // 8
    names = list(chip_sum)
    nh = len(names)

    def body(dy_ref, st_ref, sp_ref, u_ref, dud_ref, b_ref, c_ref, a_ref, *rest):
        x_in, (du_ref, db_ref, dc_ref, da_ref), x_out = rest[:nh], rest[nh:nh + 4], rest[nh + 4:2 * nh + 4]
        lam_ref, car_ref, pw_ref, ssem, rsem = rest[2 * nh + 4:]
        i = pl.program_id(0)

        @pl.when(i == 0)
        def _():
            _scan_init(a_ref, car_ref, pw_ref, True)
            db_ref[...] = jnp.zeros_like(db_ref)
            dc_ref[...] = jnp.zeros_like(dc_ref)
            da_ref[...] = jnp.zeros_like(da_ref)
            for cp in _exchange_copies(x_in, x_out, ssem, rsem):
                cp.start()

        for j in range(8):
            lam_ref[:, _sblk(j)] = _raw_bdot(dy_ref[:, _cblk(j % 4)], c_ref[j], 'nt')
        _scan_tile(lam_ref, lam_ref, car_ref, pw_ref, True)
        for k in range(4):
            du_ref[:, _cblk(k)] = (dud_ref[:, _cblk(k)] + _raw_bdot(lam_ref[:, _sblk(k)], b_ref[k], 'nt')
                                   + _raw_bdot(lam_ref[:, _sblk(4 + k)], b_ref[4 + k], 'nt')
                                   ).astype(du_ref.dtype)
            sr = _shift_down(st_ref[:, _sblk(k)], sp_ref[:, _sblk(k)], nt - 1 - i, 1)
            si = _shift_down(st_ref[:, _sblk(4 + k)], sp_ref[:, _sblk(4 + k)], nt - 1 - i, 1)
            lr, li = lam_ref[:, _sblk(k)], lam_ref[:, _sblk(4 + k)]
            da_ref[:, _sblk(k)] += _sum0(lr * sr + li * si)
            da_ref[:, _sblk(4 + k)] += _sum0(li * sr - lr * si)
        for j in range(8):
            db_ref[j] += _raw_bdot(u_ref[:, _cblk(j % 4)], lam_ref[:, _sblk(j)], 'tn')
            dc_ref[j] += _raw_bdot(st_ref[:, _sblk(j)], dy_ref[:, _cblk(j % 4)], 'tn')

        @pl.when(i == nt - 1)
        def _():
            for cp in _exchange_copies(x_in, x_out, ssem, rsem):
                cp.wait()

    whole = lambda shape: pl.BlockSpec(shape, lambda i: (0,) * len(shape))
    rev = lambda i: (nt - 1 - i, 0)
    outs = pl.pallas_call(
        body, name="s5_bwd", grid=(nt,),
        in_specs=[pl.BlockSpec((S5_T, S5_W), rev), pl.BlockSpec((S5_T, 2 * S5_N), rev),
                  pl.BlockSpec((8, 2 * S5_N), lambda i: (jnp.maximum((nt - 1 - i) * nb8 - 1, 0), 0)),
                  pl.BlockSpec((S5_T, S5_W), rev), pl.BlockSpec((S5_T, S5_W), rev), whole(bmat.shape), whole(cmat.shape),
                  whole(abar.shape)] + [ANY] * nh,
        out_specs=[pl.BlockSpec((S5_T, S5_W), rev), whole((8, _CB, _SB)), whole((8, _SB, _CB)), whole((1, 2 * S5_N))]
        + [ANY] * nh,
        out_shape=[jax.ShapeDtypeStruct((L, S5_W), bf16), jax.ShapeDtypeStruct((8, _CB, _SB), f32),
                   jax.ShapeDtypeStruct((8, _SB, _CB), f32), jax.ShapeDtypeStruct((1, 2 * S5_N), f32)]
        + [jax.ShapeDtypeStruct(chip_sum[nm].shape, chip_sum[nm].dtype) for nm in names],
        scratch_shapes=[pltpu.VMEM((S5_T, 2 * S5_N), f32), pltpu.VMEM((8, 2 * S5_N), f32), pltpu.VMEM((4, 8, 2 * S5_N), f32),
                        pltpu.SemaphoreType.DMA((3 * nh,)), pltpu.SemaphoreType.DMA((3 * nh,))],
        compiler_params=pltpu.CompilerParams(dimension_semantics=("arbitrary",), vmem_limit_bytes=VMEM_LIMIT),
    )(dy, st, st, u, du_direct, bmat, cmat, abar, *[chip_sum[nm] for nm in names])
    return outs[0], outs[1], outs[2], outs[3], dict(zip(names, outs[4:]))


def _s5_disc_fwd(a_re, a_im, ls, b_re, b_im):
    def body(a_re_ref, a_im_ref, ls_ref, b_re_ref, b_im_ref, ar_ref, ai_ref, br_ref, bi_ref):
        outs = _s5_disc(a_re_ref[...], a_im_ref[...], ls_ref[...], b_re_ref[...], b_im_ref[...])
        for ref, v in zip((ar_ref, ai_ref, br_ref, bi_ref), outs):
            ref[...] = v

    c1, c16 = jax.ShapeDtypeStruct((S5_N, 1), f32), jax.ShapeDtypeStruct((S5_N, S5_C), f32)
    return pl.pallas_call(body, name="s5_disc", out_shape=[c1, c1, c16, c16])(a_re, a_im, ls, b_re, b_im)


def _s5_disc_bwd(a_re, a_im, ls, b_re, b_im, d_ar, d_ai, d_br, d_bi, seg):
    def body(a_re_ref, a_im_ref, ls_ref, b_re_ref, b_im_ref, g1, g2, g3, g4, seg_ref, o1, o2, o3, o4, o5):
        _, vjp = jax.vjp(_s5_disc, a_re_ref[...], a_im_ref[...], ls_ref[...], b_re_ref[...], b_im_ref[...])
        da_re, da_im, dls, db_re, db_im = vjp((g1[...], g2[...], g3[...], g4[...]))
        o1[...] = da_re
        o2[...] = da_im
        o3[...] = _dot32(seg_ref[...], dls)
        o4[...] = db_re
        o5[...] = db_im

    c1, c16 = jax.ShapeDtypeStruct((S5_N, 1), f32), jax.ShapeDtypeStruct((S5_N, S5_C), f32)
    return pl.pallas_call(body, name="s5_disc_bwd", out_shape=[c1, c1, jax.ShapeDtypeStruct((S5_G, 1), f32), c16, c16])(
        a_re, a_im, ls, b_re, b_im, d_ar, d_ai, d_br, d_bi, seg)


ANY = pl.BlockSpec(memory_space=pl.ANY)

GATHER = {'w_in': ((4352, 1024), 0), 'ffn_w_up': ((1024, 5632), 1), 'w_branch_rwkv': ((512, 1024), 1),
          'w_branch_s5': ((512, 1024), 1), 'w_out': ((1024, 1024), 0), 's5_w_glu': ((512, 512), 0),
          'ffn_w_down': ((2816, 1024), 0), 'rwkv_w2': ((64, 512), 1), 'rwkv_a2': ((64, 512), 1),
          'rwkv_g2': ((128, 512), 1), 'ffn_conv_w': ((8, 5632), 1)}
BIG = ['w_in', 'ffn_w_up', 'w_branch_rwkv', 'w_branch_s5', 'w_out', 's5_w_glu', 'ffn_w_down']
TINY = ['rwkv_w2', 'rwkv_a2', 'rwkv_g2', 'ffn_conv_w']
SMALL = [n for n in WEIGHTS if n not in GATHER]
SMALL_ROWS = 320
ADAM_ROWS = 256


def _mo(v, m):
    return v if isinstance(v, int) else pl.multiple_of(v, m)


def _slab(ref, name, j, h=None):
    (R, Cn), axis = GATHER[name]
    if axis == 0:
        rs = R // 4
        if h is None:
            return ref.at[pl.ds(_mo(j * rs, 16), rs), :]
        return ref.at[pl.ds(_mo(j * rs + h * (rs // 2), 8), rs // 2), :]
    cols = pl.ds(_mo(j * (Cn // 4), 128), Cn // 4)
    if h is None:
        return ref.at[:, cols]
    return ref.at[pl.ds(_mo(h * (R // 2), 8), R // 2), cols]


def _half_shape(name):
    (R, Cn), axis = GATHER[name]
    return (R // 8, Cn) if axis == 0 else (R // 2, Cn // 4)


def _chip_peers(px, py):
    return [((1 - px) if (k >> 1) else px, (1 - py) if (k & 1) else py) for k in (1, 2, 3)]


def _run_copies(copies):
    for cp in copies:
        cp.start()
    for cp in copies:
        cp.wait()


def _gather_weights(blocks):
    names = list(blocks)
    n = len(names)

    def body(*refs):
        ins, outs = refs[:n], refs[n:2 * n]
        ssem, rsem, lsem = refs[2 * n:]
        px, py, pc = _mesh_pos()
        me = 2 * px + py
        copies = []
        for i, nm in enumerate(names):
            if nm in BIG:
                hr = blocks[nm].shape[0] // 2
                src, dst = ins[i].at[pl.ds(pl.multiple_of(pc * hr, 16), hr), :], _slab(outs[i], nm, me, pc)
            else:
                src, dst = ins[i], _slab(outs[i], nm, me)
            copies.append(pltpu.make_async_copy(src, dst, lsem.at[i]))
            for k, (qx, qy) in enumerate(_chip_peers(px, py)):
                copies.append(pltpu.make_async_remote_copy(src, dst, ssem.at[3 * i + k], rsem.at[3 * i + k],
                                                           device_id=(qx, qy, pc), device_id_type=MESH))
        _run_copies(copies)

    outs = pl.pallas_call(
        body, name="gather_weights", in_specs=[ANY] * n, out_specs=[ANY] * n,
        out_shape=[jax.ShapeDtypeStruct(GATHER[nm][0], blocks[nm].dtype) for nm in names],
        scratch_shapes=[pltpu.SemaphoreType.DMA((3 * n,)), pltpu.SemaphoreType.DMA((3 * n,)), pltpu.SemaphoreType.DMA((n,))],
    )(*[blocks[nm] for nm in names])
    return dict(zip(names, outs))


def _gather_pair(full, names, call_name):
    n = len(names)

    def body(*refs):
        ins, outs = refs[:n], refs[n:2 * n]
        ssem, rsem = refs[2 * n:]
        px, py, pc = _mesh_pos()
        copies = []
        for i, nm in enumerate(names):
            for j in range(4):
                copies.append(pltpu.make_async_remote_copy(_slab(ins[i], nm, j, pc), _slab(outs[i], nm, j, pc),
                                                           ssem.at[4 * i + j], rsem.at[4 * i + j],
                                                           device_id=(px, py, 1 - pc), device_id_type=MESH))
        _run_copies(copies)

    outs = pl.pallas_call(
        body, name=call_name, in_specs=[ANY] * n, out_specs=[ANY] * n,
        out_shape=[jax.ShapeDtypeStruct(full[nm].shape, full[nm].dtype) for nm in names],
        input_output_aliases={i: i for i in range(n)},
        scratch_shapes=[pltpu.SemaphoreType.DMA((4 * n,)), pltpu.SemaphoreType.DMA((4 * n,))],
    )(*[full[nm] for nm in names])
    return dict(zip(names, outs))


def _grads_to_sibling(G, names, call_name, small=None):
    n = len(names)
    ns = 0 if small is None else 1

    def body(*refs):
        g_refs, o_refs = refs[:n + ns], refs[n + ns:2 * (n + ns)]
        ssem, rsem = refs[2 * (n + ns):]
        px, py, pc = _mesh_pos()
        sib = (px, py, 1 - pc)
        copies = []
        for i, nm in enumerate(names):
            for j in range(4):
                copies.append(pltpu.make_async_remote_copy(_slab(g_refs[i], nm, j, 1 - pc), o_refs[i].at[j],
                                                           ssem.at[4 * i + j], rsem.at[4 * i + j],
                                                           device_id=sib, device_id_type=MESH))
        if ns:
            copies.append(pltpu.make_async_remote_copy(g_refs[n], o_refs[n], ssem.at[4 * n], rsem.at[4 * n],
                                                       device_id=sib, device_id_type=MESH))
        _run_copies(copies)

    outs = pl.pallas_call(
        body, name=call_name, in_specs=[ANY] * (n + ns), out_specs=[ANY] * (n + ns),
        out_shape=[jax.ShapeDtypeStruct((4,) + _half_shape(nm), f32) for nm in names]
        + [jax.ShapeDtypeStruct((SMALL_ROWS, PACK_W), f32)] * ns,
        scratch_shapes=[pltpu.SemaphoreType.DMA((4 * n + ns,)), pltpu.SemaphoreType.DMA((4 * n + ns,))],
    )(*[G[nm] for nm in names], *([small] * ns))
    return dict(zip(names, outs[:n])), (outs[n] if ns else None)


def _pair_add(G, recv, names, call_name, small=None, small_recv=None):
    n = len(names)
    ns = 0 if small is None else 1
    cidx = lax.axis_index("c").astype(jnp.int32).reshape(1)

    def body(c_ref, *refs):
        ins, outs = refs[:2 * (n + ns)], refs[2 * (n + ns):]
        for i in range(n):
            outs[i][...] = (ins[i][...] + ins[n + ns + i][...]).astype(bf16)
        if ns:
            outs[n][...] = ins[n][...] + ins[2 * n + 1][...]

    g_specs, r_specs = [], []
    for nm in names:
        hr, hc = _half_shape(nm)
        if GATHER[nm][1] == 0:
            g_specs.append(pl.BlockSpec((hr // 2, hc), lambda j, i, c: ((2 * j + c[0]) * 2 + i, 0)))
        else:
            g_specs.append(pl.BlockSpec((hr // 2, hc), lambda j, i, c: (2 * c[0] + i, j)))
        r_specs.append(pl.BlockSpec((1, hr // 2, hc), lambda j, i, c: (j, i, 0)))
    sm = [pl.BlockSpec((SMALL_ROWS // 8, PACK_W), lambda j, i, c: (2 * j + i, 0))] * ns
    outs = pl.pallas_call(
        body, name=call_name,
        grid_spec=pltpu.PrefetchScalarGridSpec(num_scalar_prefetch=1, grid=(4, 2), in_specs=g_specs + sm + r_specs + sm,
                                               out_specs=r_specs + sm),
        out_shape=[jax.ShapeDtypeStruct((4,) + _half_shape(nm), bf16) for nm in names]
        + [jax.ShapeDtypeStruct((SMALL_ROWS, PACK_W), f32)] * ns,
        compiler_params=pltpu.CompilerParams(vmem_limit_bytes=VMEM_LIMIT),
    )(cidx, *[G[nm] for nm in names], *([small] * ns), *[recv[nm] for nm in names], *([small_recv] * ns))
    return dict(zip(names, outs[:n])), (outs[n] if ns else None)


def _exchange_copies(ins, outs, ssem, rsem):
    px, py, pc = _mesh_pos()
    me = 2 * px + py
    return [pltpu.make_async_remote_copy(ins[i].at[2 * qx + qy], outs[i].at[me], ssem.at[3 * i + k], rsem.at[3 * i + k],
                                         device_id=(qx, qy, pc), device_id_type=MESH)
            for i in range(len(ins)) for k, (qx, qy) in enumerate(_chip_peers(px, py))]


def _grads_chip_exchange(chip_sum, names, small):
    n = len(names)

    def body(*refs):
        ins, outs = refs[:n + 1], refs[n + 1:2 * n + 2]
        ssem, rsem, ssem_s, rsem_s = refs[2 * n + 2:]
        px, py, pc = _mesh_pos()
        me = 2 * px + py
        copies = _exchange_copies(ins[:n], outs[:n], ssem, rsem)
        hs = SMALL_ROWS // 2
        mine = ins[n].at[pl.ds(pl.multiple_of(pc * hs, 8), hs), :]
        copies += [pltpu.make_async_remote_copy(mine, outs[n].at[me], ssem_s.at[k], rsem_s.at[k],
                                                device_id=(qx, qy, pc), device_id_type=MESH)
                   for k, (qx, qy) in enumerate(_chip_peers(px, py))]
        _run_copies(copies)

    outs = pl.pallas_call(
        body, name="grads_chip_exchange", in_specs=[ANY] * (n + 1), out_specs=[ANY] * (n + 1),
        out_shape=[jax.ShapeDtypeStruct(chip_sum[nm].shape, chip_sum[nm].dtype) for nm in names]
        + [jax.ShapeDtypeStruct((4, SMALL_ROWS // 2, PACK_W), f32)],
        scratch_shapes=[pltpu.SemaphoreType.DMA((3 * n,)), pltpu.SemaphoreType.DMA((3 * n,)),
                        pltpu.SemaphoreType.DMA((3,)), pltpu.SemaphoreType.DMA((3,))],
    )(*[chip_sum[nm] for nm in names], small)
    return dict(zip(names, outs[:n])), outs[n]


def _sum_slots(slots, chip_sum, small4, small_own):
    n = len(BIG)
    me = jnp.stack([2 * lax.axis_index("x") + lax.axis_index("y"), lax.axis_index("c")]).astype(jnp.int32)

    def body(me_ref, *refs):
        for i in range(n + 1):
            own = refs[5 * i + 4][...].astype(f32)
            own = own[0] if i < n else own
            term = [jnp.where(me_ref[0] == k, own, refs[5 * i + k][0].astype(f32)) for k in range(4)]
            refs[5 * (n + 1) + i][...] = ((term[0] + term[1]) + term[2]) + term[3]

    redirect = lambda k: (lambda i, m: (jnp.where(m[0] == k, (k + 1) % 4, k), i, 0))
    in_specs, args, specs_out, shapes = [], [], [], []
    for nm in BIG:
        hr, hc = _half_shape(nm)
        in_specs += [pl.BlockSpec((1, hr // 2, hc), redirect(k)) for k in range(4)]
        in_specs.append(pl.BlockSpec((1, hr // 2, hc), lambda i, m: (m[0], i, 0)))
        args += [slots[nm]] * 4 + [chip_sum[nm]]
        specs_out.append(pl.BlockSpec((hr // 2, hc), lambda i, m: (i, 0)))
        shapes.append(jax.ShapeDtypeStruct((hr, hc), f32))
    in_specs += [pl.BlockSpec((1, SMALL_ROWS // 4, PACK_W), redirect(k)) for k in range(4)]
    in_specs.append(pl.BlockSpec((SMALL_ROWS // 4, PACK_W), lambda i, m: (2 * m[1] + i, 0)))
    args += [small4] * 4 + [small_own]
    specs_out.append(pl.BlockSpec((SMALL_ROWS // 4, PACK_W), lambda i, m: (i, 0)))
    shapes.append(jax.ShapeDtypeStruct((SMALL_ROWS // 2, PACK_W), f32))
    outs = pl.pallas_call(
        body, name="grads_chip_sum",
        grid_spec=pltpu.PrefetchScalarGridSpec(num_scalar_prefetch=1, grid=(2,), in_specs=in_specs, out_specs=specs_out),
        out_shape=shapes, compiler_params=pltpu.CompilerParams(vmem_limit_bytes=VMEM_LIMIT),
    )(me, *args)
    return dict(zip(BIG, outs[:n])), outs[n]


def _halves_to_sibling(half):
    names = list(half)
    n = len(names)

    def body(*refs):
        ins, outs = refs[:n], refs[n:2 * n]
        ssem, rsem = refs[2 * n:]
        px, py, pc = _mesh_pos()
        _run_copies([pltpu.make_async_remote_copy(ins[i], outs[i], ssem.at[i], rsem.at[i],
                                                  device_id=(px, py, 1 - pc), device_id_type=MESH) for i in range(n)])

    outs = pl.pallas_call(
        body, name="grads_halves_to_sibling", in_specs=[ANY] * n, out_specs=[ANY] * n,
        out_shape=[jax.ShapeDtypeStruct(half[nm].shape, f32) for nm in names],
        scratch_shapes=[pltpu.SemaphoreType.DMA((n,)), pltpu.SemaphoreType.DMA((n,))],
    )(*[half[nm] for nm in names])
    return dict(zip(names, outs))


def _join_halves(mine, other, pc):
    hr = mine.shape[0]
    return lax.dynamic_slice_in_dim(jnp.concatenate([other, mine, other], axis=0), (1 - pc) * hr, 2 * hr, axis=0)


def _flat_pad(v):
    v = v.reshape(-1)
    return jnp.pad(v, (0, _ceil_to(v.shape[0], PACK_W) - v.shape[0]))


def _pack_rows(parts, rows):
    flat = jnp.concatenate([_flat_pad(p) for p in parts])
    return jnp.pad(flat, (0, rows * PACK_W - flat.shape[0])).reshape(rows, PACK_W)


def _unpack_rows(buf, shapes):
    flat = buf.reshape(-1)
    out, off = [], 0
    for shp in shapes:
        n = 1
        for d in shp:
            n *= d
        out.append(flat[off:off + n].reshape(shp))
        off += _ceil_to(n, PACK_W)
    return out


def _adamw_math(w_, g_, m_, v_):
    m2 = ADAM_B1 * m_ + (1.0 - ADAM_B1) * g_
    v2 = ADAM_B2 * v_ + (1.0 - ADAM_B2) * (g_ * g_)
    m_hat = m2 / (1.0 - ADAM_B1 ** ADAM_STEP)
    v_hat = v2 / (1.0 - ADAM_B2 ** ADAM_STEP)
    return -ADAM_LR * (m_hat / (jnp.sqrt(v_hat) + ADAM_EPS) + ADAM_WD * w_), m2, v2


def _adamw(groups):
    ng = len(groups)

    def body(*refs):
        ins, outs = refs[:4 * ng], refs[4 * ng:]
        for i in range(ng):
            res = _adamw_math(*(r[...] for r in ins[4 * i:4 * i + 4]))
            for ref, val in zip(outs[3 * i:3 * i + 3], res):
                ref[...] = val

    in_specs, out_specs, out_shape = [], [], []
    for grp in groups:
        R, Cn = grp[0].shape
        spec = pl.BlockSpec((R // 8, Cn), lambda i: (i, 0))
        in_specs += [spec] * 4
        out_specs += [spec] * 3
        out_shape += [jax.ShapeDtypeStruct((R, Cn), f32)] * 3
    outs = pl.pallas_call(
        body, name="adamw", grid=(8,), in_specs=in_specs, out_specs=out_specs, out_shape=out_shape,
        compiler_params=pltpu.CompilerParams(vmem_limit_bytes=VMEM_LIMIT),
    )(*[a for grp in groups for a in grp])
    return [tuple(outs[3 * i:3 * i + 3]) for i in range(ng)]


def _forward_backward(x, tgt, W, S, late):
    L = x.shape[0]
    TM, TMW, TS = 256, 128, 512
    row = lambda c, dt=f32: (c, dt)
    hid = jnp.arange(RWKV_W) // HEAD
    E = (hid[:, None] == hid[None, :]).astype(f32)
    seg = (jnp.arange(S5_N)[None, :] // S5_P == jnp.arange(S5_G)[:, None]).astype(f32)

    w_in_t = W['w_in']
    w_p, w_u, w_g = w_in_t[:N_RWKV], w_in_t[N_RWKV:N_RWKV + S5_W], w_in_t[N_RWKV + S5_W:]
    zpad = jnp.zeros((64, RWKV_W), f32)
    w2p = jnp.concatenate([W['rwkv_w2'], zpad], axis=0)
    a2p = jnp.concatenate([zpad, W['rwkv_a2']], axis=0)
    g2 = W['rwkv_g2']
    prep_consts = [S['rwkv_shift_mu'], S['rwkv_w0'], S['rwkv_a0'], S['rwkv_k_k'], S['rwkv_k_a'], w2p, a2p, g2, E]
    out_consts = [S['rwkv_lnx_w'], S['rwkv_lnx_b'], S['rwkv_r_k'], E]
    cw, cb = W['ffn_conv_w'][:3], S['ffn_conv_b']

    a_re, a_im = S['s5_a_re'].reshape(S5_N, 1), S['s5_a_im'].reshape(S5_N, 1)
    ls = jnp.repeat(S['s5_log_step'].reshape(S5_G, 1), S5_P, axis=0)
    b_re, b_im = S['s5_b_re'].reshape(S5_N, S5_C), S['s5_b_im'].reshape(S5_N, S5_C)
    ar, ai, bbr, bbi = _s5_disc_fwd(a_re, a_im, ls, b_re, b_im)
    abar = jnp.concatenate([ar.reshape(1, S5_N), ai.reshape(1, S5_N)], axis=1)
    eye8 = jnp.eye(8, dtype=f32)

    def blocks_in(bb):
        t = bb.reshape(4, 8, S5_P, S5_C).transpose(0, 1, 3, 2)
        return (t[:, :, :, None, :] * eye8[None, :, None, :, None]).reshape(4, _CB, _SB)

    def blocks_out(cc):
        t = cc.reshape(4, 8, S5_C, S5_P).transpose(0, 1, 3, 2)
        return (t[:, :, :, None, :] * eye8[None, :, None, :, None]).reshape(4, _SB, _CB)

    def undiag_in(blocks):
        t = blocks.reshape(4, 8, S5_C, 8, S5_P)
        t = jnp.sum(t * eye8[None, :, None, :, None], axis=3)
        return t.reshape(S5_G, S5_C, S5_P).transpose(0, 2, 1).reshape(S5_N, S5_C)

    def undiag_out(blocks):
        t = blocks.reshape(4, 8, S5_P, 8, S5_C)
        t = jnp.sum(t * eye8[None, :, None, :, None], axis=3)
        return t.reshape(S5_G, S5_P, S5_C).transpose(0, 2, 1)

    bmat = jnp.concatenate([blocks_in(bbr), blocks_in(bbi)], axis=0).astype(bf16)
    cmat = jnp.concatenate([blocks_out(S['s5_c_re'].reshape(S5_G, S5_C, S5_P)),
                            -blocks_out(S['s5_c_im'].reshape(S5_G, S5_C, S5_P))], axis=0).astype(bf16)

    g1, g2n, g3, g4 = S['norm_mix_pre'], S['norm_mix_post'], S['norm_ffn_pre'], S['norm_ffn_post']
    (h1,) = _rowcall("norm_pre", lambda i, n, R, P, X, C: ((_rms(R[0], C[0]),), ()), L, TS, [x], [g1],
                     out_rows=[row(D_MODEL, bf16)])
    p = _mm(h1, w_p, 'nt', "mm_p")
    u = _mm(h1, w_u, 'nt', "mm_u")
    gp = _mm(h1, w_g, 'nt', "mm_g")

    def prep_fn(i, n, R, P, X, C):
        q = R[0] + (_shift_down(R[0], P[0], i, 1) - R[0]) * C[0]
        return _prep(q, *C[1:]), ()

    r, lw, k2, v, an, bv, g = _rowcall("rwkv_prep", prep_fn, L, TS, [p], prep_consts,
                                       out_rows=[row(RWKV_W)] * 7, prev=[0])
    y, ck = _wkv7_fwd(r, lw, k2, v, an, bv)
    (o_a,) = _rowcall("rwkv_out", lambda i, n, R, P, X, C: ((_rwkv_out(*R, *C),), ()), L, TS, [y, r, k2, v, g],
                      out_consts, out_rows=[row(RWKV_W, bf16)])
    o_r = _mm(o_a, W['w_branch_rwkv'], 'nn', "mm_br")

    st, ysc, got = _s5_fwd(u, bmat, cmat, abar, late)
    W = {**W, **_gather_pair(got, list(got), "gather_weights_pair_late")}
    (yg,) = _rowcall("s5_mid", lambda i, n, R, P, X, C: ((_s5_mid(*R, *C),), ()), L, TS, [ysc, u], [S['s5_d']],
                     out_rows=[row(S5_W)])
    z2 = _mm(yg, W['s5_w_glu'], 'nn', "mm_glu")
    (o_b,) = _rowcall("s5_glu", lambda i, n, R, P, X, C: ((_s5_glu(*R, *C),), ()), L, TS, [yg, z2], [S['s5_b_glu']],
                      out_rows=[row(S5_W, bf16)])
    o_s = _mm(o_b, W['w_branch_s5'], 'nn', "mm_bs")

    (merged,) = _rowcall("merge", lambda i, n, R, P, X, C: ((_merge(*R, *C),), ()), L, TS, [gp, o_r, o_s],
                         [S['b_gate']], out_rows=[row(D_MODEL, bf16)])
    mixed = _mm(merged, W['w_out'], 'nn', "mm_out")

    def resid_fn(i, n, R, P, X, C):
        x1_ = R[0] + _rms(R[1], C[0])
        return (x1_, _rms(x1_, C[1])), ()

    x1, h2 = _rowcall("resid_norm", resid_fn, L, TS, [x, mixed], [g2n, g3], out_rows=[row(D_MODEL), row(D_MODEL, bf16)])

    z = _mm(h2, W['ffn_w_up'], 'nn', "mm_up")

    def conv(zt, zprev, i, cw_, cb_):
        z2s, z1s = _shift_down(zt, zprev, i, 2), _shift_down(zt, zprev, i, 1)
        return cb_ + cw_[0:1] * z2s + cw_[1:2] * z1s + cw_[2:3] * zt, z2s, z1s

    (act,) = _rowcall("conv_act", lambda i, n, R, P, X, C: ((_act(conv(R[0], P[0], i, C[0], C[1])[0]),), ()), L, TMW,
                      [z], [cw, cb], out_rows=[row(D_FF, bf16)], prev=[0])
    f = _mm(act, W['ffn_w_down'], 'nn', "mm_down")

    def final_fn(i, n, R, P, X, C):
        x1_, f_, t_ = R
        fn_, vjp = jax.vjp(_rms, f_, C[0])
        diff = x1_ + fn_ - t_
        loss = jnp.sum(diff * diff) * (0.5 / D_MODEL)
        dx2_ = diff * (1.0 / D_MODEL)
        df_, dg4_ = vjp(dx2_)
        return (df_, dx2_), (jnp.full((1, PACK_W), loss, f32), dg4_)

    df, dx2, loss, dg4 = _rowcall("loss_head", final_fn, L, TS, [x1, f, tgt], [g4],
                                  out_rows=[row(D_MODEL, bf16), row(D_MODEL)], out_accs=[(1, PACK_W), (1, D_MODEL)])
    G = {'norm_ffn_post': dg4}

    dact = _mm(df, W['ffn_w_down'], 'nt', "mm_down_dx")
    G['ffn_w_down'] = _mm(act, df, 'tn', "mm_down_dw")

    def conv_bwd_fn(i, n, R, P, X, C):
        z_, dact_ = R
        cw_, cb_ = C
        zc, z2s, z1s = conv(z_, P[0], i, cw_, cb_)
        _, vjp = jax.vjp(_act, zc)
        (dzc_,) = vjp(dact_)
        last8 = z_[z_.shape[0] - 8:]
        zcn = cb_ + cw_[0:1] * _shift_down(X[0], last8, 1, 2) + cw_[1:2] * _shift_down(X[0], last8, 1, 1) + cw_[2:3] * X[0]
        _, vjpn = jax.vjp(_act, zcn)
        (dzcn,) = vjpn(X[1])
        dz_ = (cw_[2:3] * dzc_ + cw_[1:2] * _shift_up(dzc_, dzcn, i, n, 1) + cw_[0:1] * _shift_up(dzc_, dzcn, i, n, 2))
        return (dz_,), (_sum0(dzc_), _sum0(dzc_ * z2s), _sum0(dzc_ * z1s), _sum0(dzc_ * z_))

    wide = (1, 2 * D_FF)
    dz, dcb, dcw0, dcw1, dcw2 = _rowcall("conv_act_bwd", conv_bwd_fn, L, TMW, [z, dact], [cw, cb],
                                         out_rows=[row(2 * D_FF, bf16)], out_accs=[wide] * 4, prev=[0], nxt=[0, 1])
    G['ffn_conv_b'] = dcb
    G['ffn_conv_w'] = jnp.concatenate([dcw0, dcw1, dcw2], axis=0)
    dh2 = _mm(dz, W['ffn_w_up'], 'nt', "mm_up_dx")
    G['ffn_w_up'] = _mm(h2, dz, 'tn', "mm_up_dw")

    def norm2_bwd_fn(i, n, R, P, X, C):
        x1_, mixed_, dx2_, dh2_ = R
        _, vjp3 = jax.vjp(_rms, x1_, C[1])
        dx1a, dg3_ = vjp3(dh2_)
        dx1_ = dx2_ + dx1a
        _, vjp2 = jax.vjp(_rms, mixed_, C[0])
        dmixed_, dg2_ = vjp2(dx1_)
        return (dx1_, dmixed_), (dg2_, dg3_)

    dx1, dmixed, dg2n, dg3 = _rowcall("norm_mid_bwd", norm2_bwd_fn, L, TS, [x1, mixed, dx2, dh2], [g2n, g3],
                                      out_rows=[row(D_MODEL), row(D_MODEL, bf16)], out_accs=[(1, D_MODEL)] * 2)
    G['norm_mix_post'], G['norm_ffn_pre'] = dg2n, dg3

    dmerged = _mm(dmixed, W['w_out'], 'nt', "mm_out_dx")
    G['w_out'] = _mm(merged, dmixed, 'tn', "mm_out_dw")

    def merge_bwd_fn(i, n, R, P, X, C):
        _, vjp = jax.vjp(_merge, R[0], R[1], R[2], C[0])
        dgp_, do_r_, do_s_, dbg_ = vjp(R[3])
        return (dgp_, do_r_, do_s_), (dbg_,)

    dgp, do_r, do_s, G['b_gate'] = _rowcall("merge_bwd", merge_bwd_fn, L, TS, [gp, o_r, o_s, dmerged], [S['b_gate']],
                                            out_rows=[row(2 * D_MODEL, bf16), row(D_MODEL, bf16), row(D_MODEL, bf16)],
                                            out_accs=[(1, 2 * D_MODEL)])
    do_a = _mm(do_r, W['w_branch_rwkv'], 'nt', "mm_br_dx")
    G['w_branch_rwkv'] = _mm(o_a, do_r, 'tn', "mm_br_dw")
    do_b = _mm(do_s, W['w_branch_s5'], 'nt', "mm_bs_dx")
    G['w_branch_s5'] = _mm(o_b, do_s, 'tn', "mm_bs_dw")

    def glu_bwd_fn(i, n, R, P, X, C):
        _, vjp = jax.vjp(_s5_glu, R[0], R[1], C[0])
        dyg1_, dz2_, dbg_ = vjp(R[2])
        return (dyg1_, dz2_), (dbg_,)

    dyg1, dz2, G['s5_b_glu'] = _rowcall("s5_glu_bwd", glu_bwd_fn, L, TS, [yg, z2, do_b], [S['s5_b_glu']],
                                        out_rows=[row(S5_W), row(S5_W, bf16)], out_accs=[(1, S5_W)])
    dyg2 = _mm(dz2, W['s5_w_glu'], 'nt', "mm_glu_dx")
    G['s5_w_glu'] = _mm(yg, dz2, 'tn', "mm_glu_dw")

    def mid_bwd_fn(i, n, R, P, X, C):
        _, vjp = jax.vjp(_s5_mid, R[0], R[1], C[0])
        dysc_, du_, dd_ = vjp(R[2] + R[3])
        return (dysc_, du_), (dd_,)

    dysc, du1, G['s5_d'] = _rowcall("s5_mid_bwd", mid_bwd_fn, L, TS, [ysc, u, dyg1, dyg2], [S['s5_d']],
                                    out_rows=[row(S5_W, bf16), row(S5_W)], out_accs=[(1, S5_W)])
    early = [n for n in BIG if n != 'w_in']
    recv_e, _ = _grads_to_sibling(G, early, "grads_to_sibling_early")
    chip_e, _ = _pair_add(G, recv_e, early, "grads_pair_sum_early")
    du, dbmat, dcmat, dabar, slots_e = _s5_bwd(dysc, st, u, du1, bmat, cmat, abar, chip_e)
    da_re, da_im, dls, db_re, db_im = _s5_disc_bwd(
        a_re, a_im, ls, b_re, b_im, dabar[:, :S5_N].reshape(S5_N, 1), dabar[:, S5_N:].reshape(S5_N, 1),
        undiag_in(dbmat[:4]), undiag_in(dbmat[4:]), seg)
    G['s5_a_re'], G['s5_a_im'], G['s5_log_step'] = da_re, da_im, dls
    G['s5_b_re'], G['s5_b_im'] = db_re, db_im
    G['s5_c_re'], G['s5_c_im'] = undiag_out(dcmat[:4]), -undiag_out(dcmat[4:])

    def out_bwd_fn(i, n, R, P, X, C):
        _, vjp = jax.vjp(_rwkv_out, *R[:5], *C)
        gs = vjp(R[5])
        return gs[:5], gs[5:8]

    dy, dr1, dk1, dv1, dg, dlw, dlb, drk = _rowcall("rwkv_out_bwd", out_bwd_fn, L, TM, [y, r, k2, v, g, do_a], out_consts,
                                                    out_rows=[row(RWKV_W)] * 5, out_accs=[(1, RWKV_W)] * 3)
    G['rwkv_lnx_w'], G['rwkv_lnx_b'], G['rwkv_r_k'] = dlw, dlb, drk
    dr2, dlwk, dk2b, dv2, dan, dbv = _wkv7_bwd(r, lw, k2, v, an, bv, ck, dy)

    def prep_bwd_fn(i, n, R, P, X, C):
        p_ = R[0]
        d1 = _shift_down(p_, P[0], i, 1) - p_
        q = p_ + d1 * C[0]
        _, vjp = jax.vjp(_prep, q, *C[1:])
        cots = (R[1] + R[2], R[3], R[4] + R[5], R[6] + R[7], R[8], R[9], R[10])
        gs = vjp(cots)
        return (gs[0],), (_sum0(gs[0] * d1),) + tuple(gs[1:8])

    small, lowr = (1, RWKV_W), (128, RWKV_W)
    dq, dmu, dw0, da0, dkk, dka, dw2p, da2p, dg2 = _rowcall(
        "rwkv_prep_bwd", prep_bwd_fn, L, TM, [p, dr1, dr2, dlwk, dk1, dk2b, dv1, dv2, dan, dbv, dg],
        prep_consts, out_rows=[row(N_RWKV)], out_accs=[(1, N_RWKV)] + [small] * 4 + [lowr] * 3, prev=[0])
    G['rwkv_shift_mu'], G['rwkv_w0'], G['rwkv_a0'], G['rwkv_k_k'], G['rwkv_k_a'] = dmu, dw0, da0, dkk, dka
    G['rwkv_w2'], G['rwkv_a2'], G['rwkv_g2'] = dw2p[:64], da2p[64:], dg2

    def shift_bwd_fn(i, n, R, P, X, C):
        dm = R[0] * C[0]
        return (R[0] - dm + _shift_up(dm, X[0] * C[0], i, n, 1),), ()

    (dp,) = _rowcall("shift_bwd", shift_bwd_fn, L, TS, [dq], [S['rwkv_shift_mu']], out_rows=[row(N_RWKV, bf16)], nxt=[0])

    dproj = jnp.concatenate([dp, du, dgp], axis=1)
    dh1 = _mm(dproj, w_in_t, 'nn', "mm_in_dx")
    G['w_in'] = _mm(dproj, h1, 'tn', "mm_in_dw")

    def norm1_bwd_fn(i, n, R, P, X, C):
        _, vjp = jax.vjp(_rms, R[0], C[0])
        dxa, dg1_ = vjp(R[2])
        return (R[1] + dxa,), (dg1_,)

    dx, G['norm_mix_pre'] = _rowcall("norm_pre_bwd", norm1_bwd_fn, L, TS, [x, dx1, dh1], [g1],
                                     out_rows=[row(D_MODEL)], out_accs=[(1, D_MODEL)])
    return loss, dx, G, chip_e, slots_e


def kernel(x, norm_mix_pre, norm_mix_post, norm_ffn_pre, norm_ffn_post, w_in, b_gate, rwkv_shift_mu, rwkv_w0, rwkv_w2, rwkv_a0, rwkv_a2, rwkv_g2, rwkv_k_k, rwkv_k_a, rwkv_r_k, rwkv_lnx_w, rwkv_lnx_b, s5_a_re, s5_a_im, s5_b_re, s5_b_im, s5_c_re, s5_c_im, s5_d, s5_log_step, s5_w_glu, s5_b_glu, w_branch_rwkv, w_branch_s5, w_out, ffn_w_up, ffn_conv_w, ffn_conv_b, ffn_w_down, loss_target, m_norm_mix_pre, m_norm_mix_post, m_norm_ffn_pre, m_norm_ffn_post, m_w_in, m_b_gate, m_rwkv_shift_mu, m_rwkv_w0, m_rwkv_w2, m_rwkv_a0, m_rwkv_a2, m_rwkv_g2, m_rwkv_k_k, m_rwkv_k_a, m_rwkv_r_k, m_rwkv_lnx_w, m_rwkv_lnx_b, m_s5_a_re, m_s5_a_im, m_s5_b_re, m_s5_b_im, m_s5_c_re, m_s5_c_im, m_s5_d, m_s5_log_step, m_s5_w_glu, m_s5_b_glu, m_w_branch_rwkv, m_w_branch_s5, m_w_out, m_ffn_w_up, m_ffn_conv_w, m_ffn_conv_b, m_ffn_w_down, v_norm_mix_pre, v_norm_mix_post, v_norm_ffn_pre, v_norm_ffn_post, v_w_in, v_b_gate, v_rwkv_shift_mu, v_rwkv_w0, v_rwkv_w2, v_rwkv_a0, v_rwkv_a2, v_rwkv_g2, v_rwkv_k_k, v_rwkv_k_a, v_rwkv_r_k, v_rwkv_lnx_w, v_rwkv_lnx_b, v_s5_a_re, v_s5_a_im, v_s5_b_re, v_s5_b_im, v_s5_c_re, v_s5_c_im, v_s5_d, v_s5_log_step, v_s5_w_glu, v_s5_b_glu, v_w_branch_rwkv, v_w_branch_s5, v_w_out, v_ffn_w_up, v_ffn_conv_w, v_ffn_conv_b, v_ffn_w_down):
    A = dict(locals())
    me = 2 * lax.axis_index("x") + lax.axis_index("y")
    blk = lambda n: A[n][0]

    mine = {n: (blk(n).T if n == 'w_in' else blk(n)).astype(bf16) for n in BIG}
    mine.update({n: blk(n) for n in TINY})
    mine['ffn_conv_w'] = jnp.pad(blk('ffn_conv_w'), ((0, 5), (0, 0)))
    late = ['ffn_w_up', 'ffn_w_down']
    W = _gather_weights({n: blkv for n, blkv in mine.items() if n not in late})
    W.update(_gather_pair(W, [n for n in BIG if n not in late], "gather_weights_pair"))
    S = {n: A[n].reshape(1, -1) for n in SMALL}

    loss, dx, G, chip_e, slots_e = _forward_backward(x[0], loss_target[0], W, S, {n: mine[n] for n in late})

    tiny_shapes = [G[n].shape for n in TINY]
    small_buf = _pack_rows([G[n] for n in SMALL] + [G[n] for n in TINY] + [loss], SMALL_ROWS)
    recv, small_recv = _grads_to_sibling(G, ['w_in'], "grads_to_sibling", small_buf)
    chip_l, small_sum = _pair_add(G, recv, ['w_in'], "grads_pair_sum", small_buf, small_recv)
    slots_l, small4 = _grads_chip_exchange(chip_l, ['w_in'], small_sum)
    half, half['small'] = _sum_slots({**slots_e, **slots_l}, {**chip_e, **chip_l}, small4, small_sum)
    other = _halves_to_sibling(half)
    pc = lax.axis_index("c")
    small_tot = _join_halves(half['small'], other['small'], pc)
    grad = {n: _join_halves(half[n], other[n], pc) for n in BIG}
    grad['w_in'] = grad['w_in'].T
    vals = _unpack_rows(small_tot, [A[n].shape for n in SMALL] + tiny_shapes + [(1, PACK_W)])
    grad.update(zip(SMALL, vals))
    for n, full in zip(TINY, vals[len(SMALL):]):
        cs = A[n].shape[2]
        grad[n] = lax.dynamic_slice_in_dim(full, me * cs, cs, axis=1)
    loss_out = vals[-1][0, 0]

    packed = SMALL + TINY
    groups = [(blk(n), grad[n], blk('m_' + n), blk('v_' + n)) for n in BIG]
    groups.append(tuple(_pack_rows([src(n) for n in packed], ADAM_ROWS)
                        for src in (lambda n: A[n], lambda n: grad[n], lambda n: A['m_' + n], lambda n: A['v_' + n])))
    res = _adamw(groups)
    outs = [dict(), dict(), dict()]
    for n, r3 in zip(BIG, res[:-1]):
        for d, val in zip(outs, r3):
            d[n] = val
    for d, buf in zip(outs, res[-1]):
        d.update(zip(packed, _unpack_rows(buf, [A[n].shape for n in packed])))
    full = lambda d: [d[n].reshape(A[n].shape) for n in WEIGHTS]
    return (loss_out, dx[None], *full(grad), *full(outs[0]), *full(outs[1]), *full(outs[2]))
```

```python
import functools

import jax
import jax.numpy as jnp
from jax import lax
from jax.experimental import pallas as pl
from jax.experimental.pallas import tpu as pltpu

f32, bf16 = jnp.float32, jnp.bfloat16
MESH = pl.DeviceIdType.MESH

D_MODEL = 1024
RWKV_W = 512
HEADS, HEAD = 8, 64
N_RWKV = 1792
S5_W = 512
S5_G, S5_P, S5_C = 32, 64, 16
S5_N = S5_G * S5_P
D_FF = 2816
NORM_EPS = 1e-6
LNX_EPS = 64e-5
ADAM_LR, ADAM_B1, ADAM_B2, ADAM_EPS, ADAM_WD, ADAM_STEP = 0.001, 0.9, 0.999, 1e-08, 0.01, 10

VMEM_LIMIT = 48 * 1024 * 1024
PACK_W = 1024
WKV_C = 64
WKV_SUB = 2
WKV_ROWS = WKV_C * WKV_SUB
RESIDENT_BUDGET = 40 * 1024 * 1024
S5_T = 256

WEIGHTS = ['norm_mix_pre', 'norm_mix_post', 'norm_ffn_pre', 'norm_ffn_post', 'w_in', 'b_gate', 'rwkv_shift_mu',
           'rwkv_w0', 'rwkv_w2', 'rwkv_a0', 'rwkv_a2', 'rwkv_g2', 'rwkv_k_k', 'rwkv_k_a', 'rwkv_r_k', 'rwkv_lnx_w',
           'rwkv_lnx_b', 's5_a_re', 's5_a_im', 's5_b_re', 's5_b_im', 's5_c_re', 's5_c_im', 's5_d', 's5_log_step',
           's5_w_glu', 's5_b_glu', 'w_branch_rwkv', 'w_branch_s5', 'w_out', 'ffn_w_up', 'ffn_conv_w', 'ffn_conv_b',
           'ffn_w_down']


def _ceil_to(n, m):
    return -(-n // m) * m


def _mesh_pos():
    return lax.axis_index("x"), lax.axis_index("y"), lax.axis_index("c")


def _pick(d, cap=4096):
    for c in (1024, 1408, 2176, 896, 512, 256, 128):
        if c <= cap and d % c == 0:
            return c
    raise ValueError(d)


def _mm_resident(a, w, mode, name, M, N, K, out_dtype):
    budget = RESIDENT_BUDGET - 2 * K * N
    tm = next(t for t in (512, 256, 128) if 2 * t * (K * a.dtype.itemsize + 4 * N) <= budget)
    dims = _DIMS[mode]

    def body(a_ref, w_ref, o_ref):
        o_ref[...] = lax.dot_general(a_ref[...].astype(bf16), w_ref[...], (dims, ((), ())),
                                     preferred_element_type=f32).astype(o_ref.dtype)

    return pl.pallas_call(
        body, name=name, grid=(M // tm,),
        in_specs=[pl.BlockSpec((tm, K), lambda i: (i, 0)),
                  pl.BlockSpec(w.shape, lambda i: (0, 0), pipeline_mode=pl.Buffered(1))],
        out_specs=pl.BlockSpec((tm, N), lambda i: (i, 0)), out_shape=jax.ShapeDtypeStruct((M, N), out_dtype),
        compiler_params=pltpu.CompilerParams(dimension_semantics=("parallel",), vmem_limit_bytes=VMEM_LIMIT),
    )(a, w)


def _mm(a, b, mode, name, out_dtype=f32):
    if mode == 'tn':
        (K, M), (K2, N) = a.shape, b.shape
    elif mode == 'nt':
        (M, K), (N, K2) = a.shape, b.shape
    else:
        (M, K), (K2, N) = a.shape, b.shape
    assert K == K2, (name, a.shape, b.shape)
    if mode != 'tn' and b.dtype == bf16:
        return _mm_resident(a, b, mode, name, M, N, K, out_dtype)
    if mode == 'tn':
        tm = _pick(M, 2176)
        tn = _pick(N, 512 if tm > 1408 else (1024 if tm > 1024 else 1408))
        tk = _pick(K, 1024 if a.dtype == bf16 and b.dtype == bf16 else 512)
    else:
        tm, tn, tk = _pick(M, 512), _pick(N), _pick(K)
    nk = K // tk
    dims = {'nn': ((1,), (0,)), 'nt': ((1,), (1,)), 'tn': ((0,), (0,))}[mode]

    def body(a_ref, b_ref, o_ref, acc_ref):
        k = pl.program_id(2)

        @pl.when(k == 0)
        def _():
            acc_ref[...] = jnp.zeros_like(acc_ref)

        acc_ref[...] += lax.dot_general(a_ref[...].astype(bf16), b_ref[...].astype(bf16), (dims, ((), ())),
                                        preferred_element_type=f32)

        @pl.when(k == nk - 1)
        def _():
            o_ref[...] = acc_ref[...].astype(o_ref.dtype)

    a_spec = pl.BlockSpec((tk, tm), lambda i, j, k: (k, i)) if mode == 'tn' else pl.BlockSpec((tm, tk), lambda i, j, k: (i, k))
    b_spec = pl.BlockSpec((tn, tk), lambda i, j, k: (j, k)) if mode == 'nt' else pl.BlockSpec((tk, tn), lambda i, j, k: (k, j))
    return pl.pallas_call(
        body, name=name, grid=(M // tm, N // tn, nk),
        in_specs=[a_spec, b_spec], out_specs=pl.BlockSpec((tm, tn), lambda i, j, k: (i, j)),
        out_shape=jax.ShapeDtypeStruct((M, N), out_dtype),
        scratch_shapes=[pltpu.VMEM((tm, tn), f32)],
        compiler_params=pltpu.CompilerParams(dimension_semantics=("parallel", "parallel", "arbitrary"),
                                             vmem_limit_bytes=VMEM_LIMIT),
    )(a, b)


def _rowcall(name, fn, L, tm, rows, consts=(), out_rows=(), out_accs=(), prev=(), nxt=()):
    nsteps = L // tm
    nb8 = tm // 8
    last8 = L // 8 - 1
    n_r, n_p, n_x, n_c, n_or = len(rows), len(prev), len(nxt), len(consts), len(out_rows)

    def body(*refs):
        i = pl.program_id(0)
        vals = [r[...] for r in refs[:n_r + n_p + n_x + n_c]]
        R, P = vals[:n_r], vals[n_r:n_r + n_p]
        X, C = vals[n_r + n_p:n_r + n_p + n_x], vals[n_r + n_p + n_x:]
        o_refs = refs[n_r + n_p + n_x + n_c:]
        outs_r, outs_a = fn(i, nsteps, R, P, X, C)
        for ref, v in zip(o_refs[:n_or], outs_r, strict=True):
            ref[...] = v.astype(ref.dtype)
        if out_accs:
            @pl.when(i == 0)
            def _():
                for ref in o_refs[n_or:]:
                    ref[...] = jnp.zeros_like(ref)

            for ref, v in zip(o_refs[n_or:], outs_a, strict=True):
                ref[...] += v

    def const_spec(c):
        nd = c.ndim
        return pl.BlockSpec(c.shape, lambda i: (0,) * nd)

    in_specs = ([pl.BlockSpec((tm, a.shape[1]), lambda i: (i, 0)) for a in rows]
                + [pl.BlockSpec((8, rows[j].shape[1]), lambda i: (jnp.maximum(i * nb8 - 1, 0), 0)) for j in prev]
                + [pl.BlockSpec((8, rows[j].shape[1]), lambda i: (jnp.minimum((i + 1) * nb8, last8), 0)) for j in nxt]
                + [const_spec(c) for c in consts])
    out_specs = ([pl.BlockSpec((tm, c), lambda i: (i, 0)) for c, _ in out_rows]
                 + [pl.BlockSpec(s, lambda i: (0, 0)) for s in out_accs])
    out_shape = ([jax.ShapeDtypeStruct((L, c), dt) for c, dt in out_rows]
                 + [jax.ShapeDtypeStruct(s, f32) for s in out_accs])
    args = list(rows) + [rows[j] for j in prev] + [rows[j] for j in nxt] + list(consts)
    return pl.pallas_call(
        body, name=name, grid=(nsteps,), in_specs=in_specs, out_specs=out_specs, out_shape=out_shape,
        compiler_params=pltpu.CompilerParams(dimension_semantics=("arbitrary",), vmem_limit_bytes=VMEM_LIMIT),
    )(*args)


def _shift_down(x, prev8, i, k):
    rolled = pltpu.roll(x, k, axis=0)
    pfix = jnp.where(i > 0, pltpu.roll(prev8, k, axis=0), 0.0)
    row8 = lax.broadcasted_iota(jnp.int32, pfix.shape, 0)
    top = jnp.where(row8 < k, pfix, rolled[:8])
    return top if x.shape[0] == 8 else jnp.concatenate([top, rolled[8:]], axis=0)


def _shift_up(x, next8, i, nsteps, k):
    tm = x.shape[0]
    rolled = pltpu.roll(x, tm - k, axis=0)
    nfix = jnp.where(i < nsteps - 1, pltpu.roll(next8, 8 - k, axis=0), 0.0)
    row8 = lax.broadcasted_iota(jnp.int32, nfix.shape, 0)
    bot = jnp.where(row8 >= 8 - k, nfix, rolled[tm - 8:])
    return jnp.concatenate([rolled[:tm - 8], bot], axis=0)


def _sum0(x):
    return jnp.sum(x, axis=0, keepdims=True)


def _rms(x, g):
    return x * lax.rsqrt(jnp.mean(x * x, axis=-1, keepdims=True) + NORM_EPS) * g


def _softplus(x):
    return jnp.maximum(x, 0.0) + jnp.log(1.0 + jnp.exp(-jnp.abs(x)))


def _gelu(x):
    return 0.5 * x * (1.0 + jnp.tanh(0.7978845608028654 * (x + 0.044715 * x * x * x)))


def _dot32(a, b):
    return jnp.dot(a, b, preferred_element_type=f32, precision=lax.Precision.HIGHEST)


def _seg_raw(x, E):
    hi = x.astype(bf16)
    r1 = x - hi.astype(f32)
    mid = r1.astype(bf16)
    lo = (r1 - mid.astype(f32)).astype(bf16)
    Eb = E.astype(bf16)
    dot = lambda t: jnp.dot(t, Eb, preferred_element_type=f32)
    return (dot(lo) + dot(mid)) + dot(hi)


@jax.custom_vjp
def _seg(x, E):
    return _seg_raw(x, E)


_seg.defvjp(lambda x, E: (_seg_raw(x, E), E), lambda E, g: (_seg_raw(g, E), jnp.zeros_like(E)))


def _prep(q, w0, a0, k_k, k_a, w2p, a2p, g2, E):
    r, k, v = q[:, 0:512], q[:, 512:1024], q[:, 1024:1536]
    wa, gd = q[:, 1536:1664], q[:, 1664:1792]
    wlog = -_softplus(-(w0 + _bdot(jnp.tanh(wa), w2p, 'nn'))) - 0.5
    lw = -jnp.exp(wlog)
    a = jax.nn.sigmoid(a0 + _bdot(wa, a2p, 'nn'))
    g = _bdot(jax.nn.sigmoid(gd), g2, 'nn')
    kk = k * k_k
    kkn = kk / jnp.maximum(jnp.sqrt(_seg(kk * kk, E)), 1e-12)
    k2 = k * (1.0 + (a - 1.0) * k_a)
    return r, lw, k2, v, -kkn, kkn * a, g


def _rwkv_out(y, r, k2, v, g, lnx_w, lnx_b, r_k, E):
    mean = _seg(y, E) * (1.0 / HEAD)
    yc = y - mean
    var = _seg(yc * yc, E) * (1.0 / HEAD)
    yn = yc * lax.rsqrt(var + LNX_EPS) * lnx_w + lnx_b
    bonus = _seg(r * k2 * r_k, E) * v
    return (yn + bonus) * g


def _s5_mid(ysc, u, d):
    return _gelu(ysc + d * u)


def _s5_glu(yg, z2, b_glu):
    return yg * jax.nn.sigmoid(z2 + b_glu)


def _merge(gp, o_r, o_s, b_gate):
    gates = jax.nn.sigmoid(gp + b_gate)
    return gates[:, :D_MODEL] * o_r + gates[:, D_MODEL:] * o_s


def _act(zc):
    return _gelu(zc[:, :D_FF]) * zc[:, D_FF:]


def _s5_disc(a_re, a_im, ls, b_re, b_im):
    dt = jnp.exp(ls)
    er = jnp.exp(a_re * dt)
    ar, ai = er * jnp.cos(a_im * dt), er * jnp.sin(a_im * dt)
    x, y = ar - 1.0, ai
    den = a_re * a_re + a_im * a_im
    fr, fi = (x * a_re + y * a_im) / den, (y * a_re - x * a_im) / den
    return ar, ai, fr * b_re - fi * b_im, fr * b_im + fi * b_re


_DIMS = {'nn': ((1,), (0,)), 'nt': ((1,), (1,)), 'tn': ((0,), (0,))}


def _raw_bdot(a, b, mode):
    return lax.dot_general(a.astype(bf16), b.astype(bf16), (_DIMS[mode], ((), ())), preferred_element_type=f32)


@functools.partial(jax.custom_vjp, nondiff_argnums=(2,))
def _bdot(a, b, mode):
    return _raw_bdot(a, b, mode)


def _bdot_fwd(a, b, mode):
    return _raw_bdot(a, b, mode), (a, b)


def _bdot_bwd(mode, res, g):
    a, b = res
    if mode == 'nn':
        return _raw_bdot(g, b, 'nt'), _raw_bdot(a, g, 'tn')
    if mode == 'nt':
        return _raw_bdot(g, b, 'nn'), _raw_bdot(g, a, 'tn')
    return _raw_bdot(b, g, 'nt'), _raw_bdot(a, g, 'nn')


_bdot.defvjp(_bdot_fwd, _bdot_bwd)


def _tri_inv_raw(A):
    n = A[0].shape[0]
    eye = (lax.broadcasted_iota(jnp.int32, (n, n), 0) == lax.broadcasted_iota(jnp.int32, (n, n), 1)).astype(f32)
    x = [eye + a for a in A]
    pw, m = A, 1
    while 2 * m < n // 2:
        pw = [_raw_bdot(p, p, 'nn') for p in pw]
        x = [xi + _raw_bdot(xi, p, 'nn') for xi, p in zip(x, pw)]
        m *= 2
    return x


@jax.custom_vjp
def _tri_inv(A):
    return _tri_inv_raw(A)


def _tri_inv_fwd(A):
    x = _tri_inv_raw(A)
    return x, x


def _tri_inv_bwd(x, g):
    return ([_raw_bdot(_raw_bdot(xi, gi, 'tn'), xi, 'nt') for xi, gi in zip(x, g)],)


_tri_inv.defvjp(_tri_inv_fwd, _tri_inv_bwd)


def _wkv_chunk(S0, r, lw, k, v, a, b, tri, bd):
    C = r[0].shape[0]
    P = range(len(r))
    lane = lax.broadcasted_iota(jnp.int32, (1, 2 * HEAD), 1)
    m0, m1 = (lane < HEAD).astype(f32), (lane >= HEAD).astype(f32)
    cat = lambda *xs: jnp.concatenate(xs, axis=0)
    stack = lambda x: cat(x * m0, x * m1)
    unstack = lambda x2: m0 * x2[:C] + m1 * x2[C:]
    rid = lax.broadcasted_iota(jnp.int32, (2 * C, 2 * C), 0)
    cid = lax.broadcasted_iota(jnp.int32, (2 * C, 2 * C), 1)
    same = (rid < C) == (cid < C)
    eye2 = (rid == cid).astype(f32)
    tri2 = (same & (rid >= cid)).astype(f32)
    sl2 = tri2 - eye2
    cum = [_dot32(tri, lw[p]) for p in P]
    g = [jnp.exp(cum[p]) for p in P]
    gi = [jnp.exp(-cum[p]) for p in P]
    at = [a[p] * jnp.exp(cum[p] - lw[p]) for p in P]
    rt = [r[p] * g[p] for p in P]
    kb = [k[p] * gi[p] for p in P]
    bb = [b[p] * gi[p] for p in P]
    lhs = [cat(stack(at[p]), stack(rt[p])) for p in P]
    pb = [_bdot(lhs[p], stack(bb[p]), 'nt') for p in P]
    pk = [_bdot(lhs[p], stack(kb[p]), 'nt') for p in P]
    aab = [pb[p][:2 * C] * sl2 for p in P]
    base = [_bdot(cat(at[p], rt[p]), S0[p], 'nt') for p in P]
    t = [_bdot(cat(pk[p][:2 * C] * sl2, pk[p][2 * C:] * tri2), cat(v[p], v[p]), 'nn') for p in P]
    rhs = [cat(base[p][:C], base[p][:C]) + t[p][:2 * C] for p in P]
    x = _tri_inv(aab)
    u =[unstack(_bdot(x[p], rhs[p], 'nn')) for p in P]
    w2 = [_bdot(pb[p][2 * C:] * tri2, cat(u[p], u[p]), 'nn') for p in P]
    y = [base[p][C:] + unstack(t[p][2 * C:]) + unstack(w2[p]) for p in P]
    S1 = [g[p][C - 1:C, :] * (S0[p] + bd * _bdot(cat(v[p], u[p]), cat(kb[p], bb[p]), 'tn')) for p in P]
    return y, S1


def _pairs(x):
    return [x[:, 2 * HEAD * p:2 * HEAD * (p + 1)] for p in range(HEADS // 2)]


def _wkv_consts():
    tri = jnp.tril(jnp.ones((WKV_C, WKV_C), f32))
    hid = jnp.arange(2 * HEAD) // HEAD
    return tri, (hid[:, None] == hid[None, :]).astype(f32)


def _wkv_step(S0, r, lw, k, v, a, b, tri, bd):
    ys, S = [], S0
    for c in range(WKV_SUB):
        sub = lambda xs: [x[c * WKV_C:(c + 1) * WKV_C] for x in xs]
        y, S = _wkv_chunk(S, sub(r), sub(lw), sub(k), sub(v), sub(a), sub(b), tri, bd)
        ys.append(y)
    return [jnp.concatenate([y[p] for y in ys], axis=0) for p in range(len(S0))], S


def _wkv7_fwd(r, lw, k, v, a, b):
    L = r.shape[0]
    nc, npair = L // WKV_ROWS, HEADS // 2

    def body(r_ref, lw_ref, k_ref, v_ref, a_ref, b_ref, tri_ref, bd_ref, y_ref, ck_ref, s_ref):
        @pl.when(pl.program_id(0) == 0)
        def _():
            s_ref[...] = jnp.zeros_like(s_ref)

        s0 = [s_ref[p] for p in range(npair)]
        for p in range(npair):
            ck_ref[0, p] = s0[p]
        y, s1 = _wkv_step(s0, *(_pairs(x) for x in (r_ref, lw_ref, k_ref, v_ref, a_ref, b_ref)), tri_ref[...], bd_ref[...])
        for p in range(npair):
            y_ref[:, 2 * HEAD * p:2 * HEAD * (p + 1)] = y[p]
            s_ref[p] = s1[p]

    row = pl.BlockSpec((WKV_ROWS, RWKV_W), lambda c: (c, 0))
    sspec = pl.BlockSpec((1, npair, 2 * HEAD, 2 * HEAD), lambda c: (c, 0, 0, 0))
    return pl.pallas_call(
        body, name="wkv7_fwd", grid=(nc,),
        in_specs=[row] * 6 + [pl.BlockSpec((WKV_C, WKV_C), lambda c: (0, 0)), pl.BlockSpec((2 * HEAD, 2 * HEAD), lambda c: (0, 0))],
        out_specs=[row, sspec],
        out_shape=[jax.ShapeDtypeStruct((L, RWKV_W), f32), jax.ShapeDtypeStruct((nc, npair, 2 * HEAD, 2 * HEAD), f32)],
        scratch_shapes=[pltpu.VMEM((npair, 2 * HEAD, 2 * HEAD), f32)],
        compiler_params=pltpu.CompilerParams(dimension_semantics=("arbitrary",), vmem_limit_bytes=VMEM_LIMIT),
    )(r, lw, k, v, a, b, *_wkv_consts())


def _wkv7_bwd(r, lw, k, v, a, b, ck, dy):
    L = r.shape[0]
    nc, npair = L // WKV_ROWS, HEADS // 2

    def body(r_ref, lw_ref, k_ref, v_ref, a_ref, b_ref, ck_ref, dy_ref, tri_ref, bd_ref,
             dr_ref, dlw_ref, dk_ref, dv_ref, da_ref, db_ref, ds_ref):
        @pl.when(pl.program_id(0) == 0)
        def _():
            ds_ref[...] = jnp.zeros_like(ds_ref)

        tri, bd = tri_ref[...], bd_ref[...]
        ins = [[ck_ref[0, p] for p in range(npair)]] + [_pairs(x) for x in (r_ref, lw_ref, k_ref, v_ref, a_ref, b_ref)]
        _, vjp = jax.vjp(lambda *t: _wkv_step(*t, tri, bd), *ins)
        gs = vjp((_pairs(dy_ref), [ds_ref[p] for p in range(npair)]))
        for p in range(npair):
            ds_ref[p] = gs[0][p]
            for ref, gval in zip((dr_ref, dlw_ref, dk_ref, dv_ref, da_ref, db_ref), gs[1:]):
                ref[:, 2 * HEAD * p:2 * HEAD * (p + 1)] = gval[p]

    row = pl.BlockSpec((WKV_ROWS, RWKV_W), lambda c: (nc - 1 - c, 0))
    sspec = pl.BlockSpec((1, npair, 2 * HEAD, 2 * HEAD), lambda c: (nc - 1 - c, 0, 0, 0))
    return pl.pallas_call(
        body, name="wkv7_bwd", grid=(nc,),
        in_specs=[row] * 6 + [sspec, row, pl.BlockSpec((WKV_C, WKV_C), lambda c: (0, 0)),
                              pl.BlockSpec((2 * HEAD, 2 * HEAD), lambda c: (0, 0))],
        out_specs=[row] * 6,
        out_shape=[jax.ShapeDtypeStruct((L, RWKV_W), f32)] * 6,
        scratch_shapes=[pltpu.VMEM((npair, 2 * HEAD, 2 * HEAD), f32)],
        compiler_params=pltpu.CompilerParams(dimension_semantics=("arbitrary",), vmem_limit_bytes=VMEM_LIMIT),
    )(r, lw, k, v, a, b, ck, dy, *_wkv_consts())


def _cmul(ar, ai, xr, xi):
    return ar * xr - ai * xi, ar * xi + ai * xr


def _scan_init(a_ref, car_ref, pw_ref, reverse):
    car_ref[...] = jnp.zeros_like(car_ref)
    ar = jnp.broadcast_to(a_ref[:, :S5_N], (8, S5_N))
    ai = jnp.broadcast_to(a_ref[:, S5_N:], (8, S5_N))
    if reverse:
        ai = -ai
    row = lax.broadcasted_iota(jnp.int32, (8, S5_N), 0)
    pr, pi = ar, ai
    qr, qi = jnp.zeros((8, S5_N), f32), jnp.zeros((8, S5_N), f32)
    for e in range(1, 9):
        sel = (row == 8 - e) if reverse else (row == e - 1)
        qr, qi = jnp.where(sel, pr, qr), jnp.where(sel, pi, qi)
        if e in (1, 2, 4):
            j = (1, 2, 4).index(e)
            pw_ref[j, :, :S5_N] = pr
            pw_ref[j, :, S5_N:] = pi
        pr, pi = _cmul(pr, pi, ar, ai)
    pw_ref[3, :, :S5_N] = qr
    pw_ref[3, :, S5_N:] = qi


def _scan_tile(x_ref, o_ref, car_ref, pw_ref, reverse):
    ng = x_ref.shape[0] // 8
    row = lax.broadcasted_iota(jnp.int32, (8, S5_N), 0)

    def group(gi, carry):
        g = (ng - 1 - gi) if reverse else gi
        t0 = pl.multiple_of(g * 8, 8)
        xr, xi = x_ref[pl.ds(t0, 8), :S5_N], x_ref[pl.ds(t0, 8), S5_N:]
        for j, d in enumerate((1, 2, 4)):
            if reverse:
                sr = jnp.where(row < 8 - d, pltpu.roll(xr, 8 - d, axis=0), 0.0)
                si = jnp.where(row < 8 - d, pltpu.roll(xi, 8 - d, axis=0), 0.0)
            else:
                sr = jnp.where(row >= d, pltpu.roll(xr, d, axis=0), 0.0)
                si = jnp.where(row >= d, pltpu.roll(xi, d, axis=0), 0.0)
            mr, mi = _cmul(pw_ref[j, :, :S5_N], pw_ref[j, :, S5_N:], sr, si)
            xr, xi = xr + mr, xi + mi
        cr, ci = carry
        mr, mi = _cmul(pw_ref[3, :, :S5_N], pw_ref[3, :, S5_N:], cr, ci)
        xr, xi = xr + mr, xi + mi
        o_ref[pl.ds(t0, 8), :S5_N] = xr
        o_ref[pl.ds(t0, 8), S5_N:] = xi
        e = 0 if reverse else 7
        return (jnp.broadcast_to(xr[e:e + 1, :], (8, S5_N)), jnp.broadcast_to(xi[e:e + 1, :], (8, S5_N)))

    cr, ci = lax.fori_loop(0, ng, group, (car_ref[:, :S5_N], car_ref[:, S5_N:]))
    car_ref[:, :S5_N] = cr
    car_ref[:, S5_N:] = ci


_CB, _SB = 128, 512


def _cblk(k):
    return slice(_CB * k, _CB * (k + 1))


def _sblk(j):
    return slice(_SB * j, _SB * (j + 1))


def _s5_fwd(u, bmat, cmat, abar, late):
    L = u.shape[0]
    nt = L // S5_T
    names = list(late)
    nh = len(names)

    def body(u_ref, b_ref, c_ref, a_ref, *rest):
        h_in, (st_ref, y_ref), h_out = rest[:nh], rest[nh:nh + 2], rest[nh + 2:2 * nh + 2]
        bu_ref, car_ref, pw_ref, ssem, rsem, lsem = rest[2 * nh + 2:]
        i = pl.program_id(0)

        def copies():
            px, py, pc = _mesh_pos()
            me = 2 * px + py
            out = []
            for a, nm in enumerate(names):
                hr = late[nm].shape[0] // 2
                src, dst = h_in[a].at[pl.ds(pl.multiple_of(pc * hr, 16), hr), :], _slab(h_out[a], nm, me, pc)
                out.append(pltpu.make_async_copy(src, dst, lsem.at[a]))
                out += [pltpu.make_async_remote_copy(src, dst, ssem.at[3 * a + k], rsem.at[3 * a + k],
                                                     device_id=(qx, qy, pc), device_id_type=MESH)
                        for k, (qx, qy) in enumerate(_chip_peers(px, py))]
            return out

        @pl.when(i == 0)
        def _():
            _scan_init(a_ref, car_ref, pw_ref, False)
            for cp in copies():
                cp.start()

        for j in range(8):
            bu_ref[:, _sblk(j)] = _raw_bdot(u_ref[:, _cblk(j % 4)], b_ref[j], 'nn')
        _scan_tile(bu_ref, st_ref, car_ref, pw_ref, False)
        for k in range(4):
            y_ref[:, _cblk(k)] = (_raw_bdot(st_ref[:, _sblk(k)], c_ref[k], 'nn')
                                  + _raw_bdot(st_ref[:, _sblk(4 + k)], c_ref[4 + k], 'nn'))

        @pl.when(i == nt - 1)
        def _():
            for cp in copies():
                cp.wait()

    whole = lambda shape: pl.BlockSpec(shape, lambda i: (0,) * len(shape))
    outs = pl.pallas_call(
        body, name="s5_fwd", grid=(nt,),
        in_specs=[pl.BlockSpec((S5_T, S5_W), lambda i: (i, 0)), whole(bmat.shape), whole(cmat.shape), whole(abar.shape)]
        + [ANY] * nh,
        out_specs=[pl.BlockSpec((S5_T, 2 * S5_N), lambda i: (i, 0)), pl.BlockSpec((S5_T, S5_W), lambda i: (i, 0))] + [ANY] * nh,
        out_shape=[jax.ShapeDtypeStruct((L, 2 * S5_N), f32), jax.ShapeDtypeStruct((L, S5_W), f32)]
        + [jax.ShapeDtypeStruct(GATHER[nm][0], late[nm].dtype) for nm in names],
        scratch_shapes=[pltpu.VMEM((S5_T, 2 * S5_N), f32), pltpu.VMEM((8, 2 * S5_N), f32), pltpu.VMEM((4, 8, 2 * S5_N), f32),
                        pltpu.SemaphoreType.DMA((3 * nh,)), pltpu.SemaphoreType.DMA((3 * nh,)), pltpu.SemaphoreType.DMA((nh,))],
        compiler_params=pltpu.CompilerParams(dimension_semantics=("arbitrary",), vmem_limit_bytes=VMEM_LIMIT),
    )(u, bmat, cmat, abar, *[late[nm] for nm in names])
    return outs[0], outs[1], dict(zip(names, outs[2:]))


def _s5_bwd(dy, st, u, du_direct, bmat, cmat, abar, chip_sum):
    L = u.shape[0]
    nt = L // S5_T
    nb8 = S5_T // 8
    names = list(chip_sum)
    nh = len(names)

    def body(dy_ref, st_ref, sp_ref, u_ref, dud_ref, b_ref, c_ref, a_ref, *rest):
        x_in, (du_ref, db_ref, dc_ref, da_ref), x_out = rest[:nh], rest[nh:nh + 4], rest[nh + 4:2 * nh + 4]
        lam_ref, car_ref, pw_ref, ssem, rsem = rest[2 * nh + 4:]
        i = pl.program_id(0)

        @pl.when(i == 0)
        def _():
            _scan_init(a_ref, car_ref, pw_ref, True)
            db_ref[...] = jnp.zeros_like(db_ref)
            dc_ref[...] = jnp.zeros_like(dc_ref)
            da_ref[...] = jnp.zeros_like(da_ref)
            for cp in _exchange_copies(x_in, x_out, ssem, rsem):
                cp.start()

        for j in range(8):
            lam_ref[:, _sblk(j)] = _raw_bdot(dy_ref[:, _cblk(j % 4)], c_ref[j], 'nt')
        _scan_tile(lam_ref, lam_ref, car_ref, pw_ref, True)
        for k in range(4):
            du_ref[:, _cblk(k)] = (dud_ref[:, _cblk(k)] + _raw_bdot(lam_ref[:, _sblk(k)], b_ref[k], 'nt')
                                   + _raw_bdot(lam_ref[:, _sblk(4 + k)], b_ref[4 + k], 'nt')
                                   ).astype(du_ref.dtype)
            sr = _shift_down(st_ref[:, _sblk(k)], sp_ref[:, _sblk(k)], nt - 1 - i, 1)
            si = _shift_down(st_ref[:, _sblk(4 + k)], sp_ref[:, _sblk(4 + k)], nt - 1 - i, 1)
            lr, li = lam_ref[:, _sblk(k)], lam_ref[:, _sblk(4 + k)]
            da_ref[:, _sblk(k)] += _sum0(lr * sr + li * si)
            da_ref[:, _sblk(4 + k)] += _sum0(li * sr - lr * si)
        for j in range(8):
            db_ref[j] += _raw_bdot(u_ref[:, _cblk(j % 4)], lam_ref[:, _sblk(j)], 'tn')
            dc_ref[j] += _raw_bdot(st_ref[:, _sblk(j)], dy_ref[:, _cblk(j % 4)], 'tn')

        @pl.when(i == nt - 1)
        def _():
            for cp in _exchange_copies(x_in, x_out, ssem, rsem):
                cp.wait()

    whole = lambda shape: pl.BlockSpec(shape, lambda i: (0,) * len(shape))
    rev = lambda i: (nt - 1 - i, 0)
    outs = pl.pallas_call(
        body, name="s5_bwd", grid=(nt,),
        in_specs=[pl.BlockSpec((S5_T, S5_W), rev), pl.BlockSpec((S5_T, 2 * S5_N), rev),
                  pl.BlockSpec((8, 2 * S5_N), lambda i: (jnp.maximum((nt - 1 - i) * nb8 - 1, 0), 0)),
                  pl.BlockSpec((S5_T, S5_W), rev), pl.BlockSpec((S5_T, S5_W), rev), whole(bmat.shape), whole(cmat.shape),
                  whole(abar.shape)] + [ANY] * nh,
        out_specs=[pl.BlockSpec((S5_T, S5_W), rev), whole((8, _CB, _SB)), whole((8, _SB, _CB)), whole((1, 2 * S5_N))]
        + [ANY] * nh,
        out_shape=[jax.ShapeDtypeStruct((L, S5_W), bf16), jax.ShapeDtypeStruct((8, _CB, _SB), f32),
                   jax.ShapeDtypeStruct((8, _SB, _CB), f32), jax.ShapeDtypeStruct((1, 2 * S5_N), f32)]
        + [jax.ShapeDtypeStruct(chip_sum[nm].shape, chip_sum[nm].dtype) for nm in names],
        scratch_shapes=[pltpu.VMEM((S5_T, 2 * S5_N), f32), pltpu.VMEM((8, 2 * S5_N), f32), pltpu.VMEM((4, 8, 2 * S5_N), f32),
                        pltpu.SemaphoreType.DMA((3 * nh,)), pltpu.SemaphoreType.DMA((3 * nh,))],
        compiler_params=pltpu.CompilerParams(dimension_semantics=("arbitrary",), vmem_limit_bytes=VMEM_LIMIT),
    )(dy, st, st, u, du_direct, bmat, cmat, abar, *[chip_sum[nm] for nm in names])
    return outs[0], outs[1], outs[2], outs[3], dict(zip(names, outs[4:]))


def _s5_disc_fwd(a_re, a_im, ls, b_re, b_im):
    def body(a_re_ref, a_im_ref, ls_ref, b_re_ref, b_im_ref, ar_ref, ai_ref, br_ref, bi_ref):
        outs = _s5_disc(a_re_ref[...], a_im_ref[...], ls_ref[...], b_re_ref[...], b_im_ref[...])
        for ref, v in zip((ar_ref, ai_ref, br_ref, bi_ref), outs):
            ref[...] = v

    c1, c16 = jax.ShapeDtypeStruct((S5_N, 1), f32), jax.ShapeDtypeStruct((S5_N, S5_C), f32)
    return pl.pallas_call(body, name="s5_disc", out_shape=[c1, c1, c16, c16])(a_re, a_im, ls, b_re, b_im)


def _s5_disc_bwd(a_re, a_im, ls, b_re, b_im, d_ar, d_ai, d_br, d_bi, seg):
    def body(a_re_ref, a_im_ref, ls_ref, b_re_ref, b_im_ref, g1, g2, g3, g4, seg_ref, o1, o2, o3, o4, o5):
        _, vjp = jax.vjp(_s5_disc, a_re_ref[...], a_im_ref[...], ls_ref[...], b_re_ref[...], b_im_ref[...])
        da_re, da_im, dls, db_re, db_im = vjp((g1[...], g2[...], g3[...], g4[...]))
        o1[...] = da_re
        o2[...] = da_im
        o3[...] = _dot32(seg_ref[...], dls)
        o4[...] = db_re
        o5[...] = db_im

    c1, c16 = jax.ShapeDtypeStruct((S5_N, 1), f32), jax.ShapeDtypeStruct((S5_N, S5_C), f32)
    return pl.pallas_call(body, name="s5_disc_bwd", out_shape=[c1, c1, jax.ShapeDtypeStruct((S5_G, 1), f32), c16, c16])(
        a_re, a_im, ls, b_re, b_im, d_ar, d_ai, d_br, d_bi, seg)


ANY = pl.BlockSpec(memory_space=pl.ANY)

GATHER = {'w_in': ((4352, 1024), 0), 'ffn_w_up': ((1024, 5632), 1), 'w_branch_rwkv': ((512, 1024), 1),
          'w_branch_s5': ((512, 1024), 1), 'w_out': ((1024, 1024), 0), 's5_w_glu': ((512, 512), 0),
          'ffn_w_down': ((2816, 1024), 0), 'rwkv_w2': ((64, 512), 1), 'rwkv_a2': ((64, 512), 1),
          'rwkv_g2': ((128, 512), 1), 'ffn_conv_w': ((8, 5632), 1)}
BIG = ['w_in', 'ffn_w_up', 'w_branch_rwkv', 'w_branch_s5', 'w_out', 's5_w_glu', 'ffn_w_down']
TINY = ['rwkv_w2', 'rwkv_a2', 'rwkv_g2', 'ffn_conv_w']
SMALL = [n for n in WEIGHTS if n not in GATHER]
SMALL_ROWS = 320
ADAM_ROWS = 256


def _mo(v, m):
    return v if isinstance(v, int) else pl.multiple_of(v, m)


def _slab(ref, name, j, h=None):
    (R, Cn), axis = GATHER[name]
    if axis == 0:
        rs = R // 4
        if h is None:
            return ref.at[pl.ds(_mo(j * rs, 16), rs), :]
        return ref.at[pl.ds(_mo(j * rs + h * (rs // 2), 8), rs // 2), :]
    cols = pl.ds(_mo(j * (Cn // 4), 128), Cn // 4)
    if h is None:
        return ref.at[:, cols]
    return ref.at[pl.ds(_mo(h * (R // 2), 8), R // 2), cols]


def _half_shape(name):
    (R, Cn), axis = GATHER[name]
    return (R // 8, Cn) if axis == 0 else (R // 2, Cn // 4)


def _chip_peers(px, py):
    return [((1 - px) if (k >> 1) else px, (1 - py) if (k & 1) else py) for k in (1, 2, 3)]


def _run_copies(copies):
    for cp in copies:
        cp.start()
    for cp in copies:
        cp.wait()


def _gather_weights(blocks):
    names = list(blocks)
    n = len(names)

    def body(*refs):
        ins, outs = refs[:n], refs[n:2 * n]
        ssem, rsem, lsem = refs[2 * n:]
        px, py, pc = _mesh_pos()
        me = 2 * px + py
        copies = []
        for i, nm in enumerate(names):
            if nm in BIG:
                hr = blocks[nm].shape[0] // 2
                src, dst = ins[i].at[pl.ds(pl.multiple_of(pc * hr, 16), hr), :], _slab(outs[i], nm, me, pc)
            else:
                src, dst = ins[i], _slab(outs[i], nm, me)
            copies.append(pltpu.make_async_copy(src, dst, lsem.at[i]))
            for k, (qx, qy) in enumerate(_chip_peers(px, py)):
                copies.append(pltpu.make_async_remote_copy(src, dst, ssem.at[3 * i + k], rsem.at[3 * i + k],
                                                           device_id=(qx, qy, pc), device_id_type=MESH))
        _run_copies(copies)

    outs = pl.pallas_call(
        body, name="gather_weights", in_specs=[ANY] * n, out_specs=[ANY] * n,
        out_shape=[jax.ShapeDtypeStruct(GATHER[nm][0], blocks[nm].dtype) for nm in names],
        scratch_shapes=[pltpu.SemaphoreType.DMA((3 * n,)), pltpu.SemaphoreType.DMA((3 * n,)), pltpu.SemaphoreType.DMA((n,))],
    )(*[blocks[nm] for nm in names])
    return dict(zip(names, outs))


def _gather_pair(full, names, call_name):
    n = len(names)

    def body(*refs):
        ins, outs = refs[:n], refs[n:2 * n]
        ssem, rsem = refs[2 * n:]
        px, py, pc = _mesh_pos()
        copies = []
        for i, nm in enumerate(names):
            for j in range(4):
                copies.append(pltpu.make_async_remote_copy(_slab(ins[i], nm, j, pc), _slab(outs[i], nm, j, pc),
                                                           ssem.at[4 * i + j], rsem.at[4 * i + j],
                                                           device_id=(px, py, 1 - pc), device_id_type=MESH))
        _run_copies(copies)

    outs = pl.pallas_call(
        body, name=call_name, in_specs=[ANY] * n, out_specs=[ANY] * n,
        out_shape=[jax.ShapeDtypeStruct(full[nm].shape, full[nm].dtype) for nm in names],
        input_output_aliases={i: i for i in range(n)},
        scratch_shapes=[pltpu.SemaphoreType.DMA((4 * n,)), pltpu.SemaphoreType.DMA((4 * n,))],
    )(*[full[nm] for nm in names])
    return dict(zip(names, outs))


def _grads_to_sibling(G, names, call_name, small=None):
    n = len(names)
    ns = 0 if small is None else 1

    def body(*refs):
        g_refs, o_refs = refs[:n + ns], refs[n + ns:2 * (n + ns)]
        ssem, rsem = refs[2 * (n + ns):]
        px, py, pc = _mesh_pos()
        sib = (px, py, 1 - pc)
        copies = []
        for i, nm in enumerate(names):
            for j in range(4):
                copies.append(pltpu.make_async_remote_copy(_slab(g_refs[i], nm, j, 1 - pc), o_refs[i].at[j],
                                                           ssem.at[4 * i + j], rsem.at[4 * i + j],
                                                           device_id=sib, device_id_type=MESH))
        if ns:
            copies.append(pltpu.make_async_remote_copy(g_refs[n], o_refs[n], ssem.at[4 * n], rsem.at[4 * n],
                                                       device_id=sib, device_id_type=MESH))
        _run_copies(copies)

    outs = pl.pallas_call(
        body, name=call_name, in_specs=[ANY] * (n + ns), out_specs=[ANY] * (n + ns),
        out_shape=[jax.ShapeDtypeStruct((4,) + _half_shape(nm), f32) for nm in names]
        + [jax.ShapeDtypeStruct((SMALL_ROWS, PACK_W), f32)] * ns,
        scratch_shapes=[pltpu.SemaphoreType.DMA((4 * n + ns,)), pltpu.SemaphoreType.DMA((4 * n + ns,))],
    )(*[G[nm] for nm in names], *([small] * ns))
    return dict(zip(names, outs[:n])), (outs[n] if ns else None)


def _pair_add(G, recv, names, call_name, small=None, small_recv=None):
    n = len(names)
    ns = 0 if small is None else 1
    cidx = lax.axis_index("c").astype(jnp.int32).reshape(1)

    def body(c_ref, *refs):
        ins, outs = refs[:2 * (n + ns)], refs[2 * (n + ns):]
        for i in range(n):
            outs[i][...] = (ins[i][...] + ins[n + ns + i][...]).astype(bf16)
        if ns:
            outs[n][...] = ins[n][...] + ins[2 * n + 1][...]

    g_specs, r_specs = [], []
    for nm in names:
        hr, hc = _half_shape(nm)
        if GATHER[nm][1] == 0:
            g_specs.append(pl.BlockSpec((hr // 2, hc), lambda j, i, c: ((2 * j + c[0]) * 2 + i, 0)))
        else:
            g_specs.append(pl.BlockSpec((hr // 2, hc), lambda j, i, c: (2 * c[0] + i, j)))
        r_specs.append(pl.BlockSpec((1, hr // 2, hc), lambda j, i, c: (j, i, 0)))
    sm = [pl.BlockSpec((SMALL_ROWS // 8, PACK_W), lambda j, i, c: (2 * j + i, 0))] * ns
    outs = pl.pallas_call(
        body, name=call_name,
        grid_spec=pltpu.PrefetchScalarGridSpec(num_scalar_prefetch=1, grid=(4, 2), in_specs=g_specs + sm + r_specs + sm,
                                               out_specs=r_specs + sm),
        out_shape=[jax.ShapeDtypeStruct((4,) + _half_shape(nm), bf16) for nm in names]
        + [jax.ShapeDtypeStruct((SMALL_ROWS, PACK_W), f32)] * ns,
        compiler_params=pltpu.CompilerParams(vmem_limit_bytes=VMEM_LIMIT),
    )(cidx, *[G[nm] for nm in names], *([small] * ns), *[recv[nm] for nm in names], *([small_recv] * ns))
    return dict(zip(names, outs[:n])), (outs[n] if ns else None)


def _exchange_copies(ins, outs, ssem, rsem):
    px, py, pc = _mesh_pos()
    me = 2 * px + py
    return [pltpu.make_async_remote_copy(ins[i].at[2 * qx + qy], outs[i].at[me], ssem.at[3 * i + k], rsem.at[3 * i + k],
                                         device_id=(qx, qy, pc), device_id_type=MESH)
            for i in range(len(ins)) for k, (qx, qy) in enumerate(_chip_peers(px, py))]


def _grads_chip_exchange(chip_sum, names, small):
    n = len(names)

    def body(*refs):
        ins, outs = refs[:n + 1], refs[n + 1:2 * n + 2]
        ssem, rsem, ssem_s, rsem_s = refs[2 * n + 2:]
        px, py, pc = _mesh_pos()
        me = 2 * px + py
        copies = _exchange_copies(ins[:n], outs[:n], ssem, rsem)
        hs = SMALL_ROWS // 2
        mine = ins[n].at[pl.ds(pl.multiple_of(pc * hs, 8), hs), :]
        copies += [pltpu.make_async_remote_copy(mine, outs[n].at[me], ssem_s.at[k], rsem_s.at[k],
                                                device_id=(qx, qy, pc), device_id_type=MESH)
                   for k, (qx, qy) in enumerate(_chip_peers(px, py))]
        _run_copies(copies)

    outs = pl.pallas_call(
        body, name="grads_chip_exchange", in_specs=[ANY] * (n + 1), out_specs=[ANY] * (n + 1),
        out_shape=[jax.ShapeDtypeStruct(chip_sum[nm].shape, chip_sum[nm].dtype) for nm in names]
        + [jax.ShapeDtypeStruct((4, SMALL_ROWS // 2, PACK_W), f32)],
        scratch_shapes=[pltpu.SemaphoreType.DMA((3 * n,)), pltpu.SemaphoreType.DMA((3 * n,)),
                        pltpu.SemaphoreType.DMA((3,)), pltpu.SemaphoreType.DMA((3,))],
    )(*[chip_sum[nm] for nm in names], small)
    return dict(zip(names, outs[:n])), outs[n]


def _sum_slots(slots, chip_sum, small4, small_own):
    n = len(BIG)
    me = jnp.stack([2 * lax.axis_index("x") + lax.axis_index("y"), lax.axis_index("c")]).astype(jnp.int32)

    def body(me_ref, *refs):
        for i in range(n + 1):
            own = refs[5 * i + 4][...].astype(f32)
            own = own[0] if i < n else own
            term = [jnp.where(me_ref[0] == k, own, refs[5 * i + k][0].astype(f32)) for k in range(4)]
            refs[5 * (n + 1) + i][...] = ((term[0] + term[1]) + term[2]) + term[3]

    redirect = lambda k: (lambda i, m: (jnp.where(m[0] == k, (k + 1) % 4, k), i, 0))
    in_specs, args, specs_out, shapes = [], [], [], []
    for nm in BIG:
        hr, hc = _half_shape(nm)
        in_specs += [pl.BlockSpec((1, hr // 2, hc), redirect(k)) for k in range(4)]
        in_specs.append(pl.BlockSpec((1, hr // 2, hc), lambda i, m: (m[0], i, 0)))
        args += [slots[nm]] * 4 + [chip_sum[nm]]
        specs_out.append(pl.BlockSpec((hr // 2, hc), lambda i, m: (i, 0)))
        shapes.append(jax.ShapeDtypeStruct((hr, hc), f32))
    in_specs += [pl.BlockSpec((1, SMALL_ROWS // 4, PACK_W), redirect(k)) for k in range(4)]
    in_specs.append(pl.BlockSpec((SMALL_ROWS // 4, PACK_W), lambda i, m: (2 * m[1] + i, 0)))
    args += [small4] * 4 + [small_own]
    specs_out.append(pl.BlockSpec((SMALL_ROWS // 4, PACK_W), lambda i, m: (i, 0)))
    shapes.append(jax.ShapeDtypeStruct((SMALL_ROWS // 2, PACK_W), f32))
    outs = pl.pallas_call(
        body, name="grads_chip_sum",
        grid_spec=pltpu.PrefetchScalarGridSpec(num_scalar_prefetch=1, grid=(2,), in_specs=in_specs, out_specs=specs_out),
        out_shape=shapes, compiler_params=pltpu.CompilerParams(vmem_limit_bytes=VMEM_LIMIT),
    )(me, *args)
    return dict(zip(BIG, outs[:n])), outs[n]


def _halves_to_sibling(half):
    names = list(half)
    n = len(names)

    def body(*refs):
        ins, outs = refs[:n], refs[n:2 * n]
        ssem, rsem = refs[2 * n:]
        px, py, pc = _mesh_pos()
        _run_copies([pltpu.make_async_remote_copy(ins[i], outs[i], ssem.at[i], rsem.at[i],
                                                  device_id=(px, py, 1 - pc), device_id_type=MESH) for i in range(n)])

    outs = pl.pallas_call(
        body, name="grads_halves_to_sibling", in_specs=[ANY] * n, out_specs=[ANY] * n,
        out_shape=[jax.ShapeDtypeStruct(half[nm].shape, f32) for nm in names],
        scratch_shapes=[pltpu.SemaphoreType.DMA((n,)), pltpu.SemaphoreType.DMA((n,))],
    )(*[half[nm] for nm in names])
    return dict(zip(names, outs))


def _join_halves(mine, other, pc):
    hr = mine.shape[0]
    return lax.dynamic_slice_in_dim(jnp.concatenate([other, mine, other], axis=0), (1 - pc) * hr, 2 * hr, axis=0)


def _flat_pad(v):
    v = v.reshape(-1)
    return jnp.pad(v, (0, _ceil_to(v.shape[0], PACK_W) - v.shape[0]))


def _pack_rows(parts, rows):
    flat = jnp.concatenate([_flat_pad(p) for p in parts])
    return jnp.pad(flat, (0, rows * PACK_W - flat.shape[0])).reshape(rows, PACK_W)


def _unpack_rows(buf, shapes):
    flat = buf.reshape(-1)
    out, off = [], 0
    for shp in shapes:
        n = 1
        for d in shp:
            n *= d
        out.append(flat[off:off + n].reshape(shp))
        off += _ceil_to(n, PACK_W)
    return out


def _adamw_math(w_, g_, m_, v_):
    m2 = ADAM_B1 * m_ + (1.0 - ADAM_B1) * g_
    v2 = ADAM_B2 * v_ + (1.0 - ADAM_B2) * (g_ * g_)
    m_hat = m2 / (1.0 - ADAM_B1 ** ADAM_STEP)
    v_hat = v2 / (1.0 - ADAM_B2 ** ADAM_STEP)
    return -ADAM_LR * (m_hat / (jnp.sqrt(v_hat) + ADAM_EPS) + ADAM_WD * w_), m2, v2


def _adamw(groups):
    ng = len(groups)

    def body(*refs):
        ins, outs = refs[:4 * ng], refs[4 * ng:]
        for i in range(ng):
            res = _adamw_math(*(r[...] for r in ins[4 * i:4 * i + 4]))
            for ref, val in zip(outs[3 * i:3 * i + 3], res):
                ref[...] = val

    in_specs, out_specs, out_shape = [], [], []
    for grp in groups:
        R, Cn = grp[0].shape
        spec = pl.BlockSpec((R // 8, Cn), lambda i: (i, 0))
        in_specs += [spec] * 4
        out_specs += [spec] * 3
        out_shape += [jax.ShapeDtypeStruct((R, Cn), f32)] * 3
    outs = pl.pallas_call(
        body, name="adamw", grid=(8,), in_specs=in_specs, out_specs=out_specs, out_shape=out_shape,
        compiler_params=pltpu.CompilerParams(vmem_limit_bytes=VMEM_LIMIT),
    )(*[a for grp in groups for a in grp])
    return [tuple(outs[3 * i:3 * i + 3]) for i in range(ng)]


def _forward_backward(x, tgt, W, S, late):
    L = x.shape[0]
    TM, TMW, TS = 256, 128, 512
    row = lambda c, dt=f32: (c, dt)
    hid = jnp.arange(RWKV_W) // HEAD
    E = (hid[:, None] == hid[None, :]).astype(f32)
    seg = (jnp.arange(S5_N)[None, :] // S5_P == jnp.arange(S5_G)[:, None]).astype(f32)

    w_in_t = W['w_in']
    w_p, w_u, w_g = w_in_t[:N_RWKV], w_in_t[N_RWKV:N_RWKV + S5_W], w_in_t[N_RWKV + S5_W:]
    zpad = jnp.zeros((64, RWKV_W), f32)
    w2p = jnp.concatenate([W['rwkv_w2'], zpad], axis=0)
    a2p = jnp.concatenate([zpad, W['rwkv_a2']], axis=0)
    g2 = W['rwkv_g2']
    prep_consts = [S['rwkv_shift_mu'], S['rwkv_w0'], S['rwkv_a0'], S['rwkv_k_k'], S['rwkv_k_a'], w2p, a2p, g2, E]
    out_consts = [S['rwkv_lnx_w'], S['rwkv_lnx_b'], S['rwkv_r_k'], E]
    cw, cb = W['ffn_conv_w'][:3], S['ffn_conv_b']

    a_re, a_im = S['s5_a_re'].reshape(S5_N, 1), S['s5_a_im'].reshape(S5_N, 1)
    ls = jnp.repeat(S['s5_log_step'].reshape(S5_G, 1), S5_P, axis=0)
    b_re, b_im = S['s5_b_re'].reshape(S5_N, S5_C), S['s5_b_im'].reshape(S5_N, S5_C)
    ar, ai, bbr, bbi = _s5_disc_fwd(a_re, a_im, ls, b_re, b_im)
    abar = jnp.concatenate([ar.reshape(1, S5_N), ai.reshape(1, S5_N)], axis=1)
    eye8 = jnp.eye(8, dtype=f32)

    def blocks_in(bb):
        t = bb.reshape(4, 8, S5_P, S5_C).transpose(0, 1, 3, 2)
        return (t[:, :, :, None, :] * eye8[None, :, None, :, None]).reshape(4, _CB, _SB)

    def blocks_out(cc):
        t = cc.reshape(4, 8, S5_C, S5_P).transpose(0, 1, 3, 2)
        return (t[:, :, :, None, :] * eye8[None, :, None, :, None]).reshape(4, _SB, _CB)

    def undiag_in(blocks):
        t = blocks.reshape(4, 8, S5_C, 8, S5_P)
        t = jnp.sum(t * eye8[None, :, None, :, None], axis=3)
        return t.reshape(S5_G, S5_C, S5_P).transpose(0, 2, 1).reshape(S5_N, S5_C)

    def undiag_out(blocks):
        t = blocks.reshape(4, 8, S5_P, 8, S5_C)
        t = jnp.sum(t * eye8[None, :, None, :, None], axis=3)
        return t.reshape(S5_G, S5_P, S5_C).transpose(0, 2, 1)

    bmat = jnp.concatenate([blocks_in(bbr), blocks_in(bbi)], axis=0).astype(bf16)
    cmat = jnp.concatenate([blocks_out(S['s5_c_re'].reshape(S5_G, S5_C, S5_P)),
                            -blocks_out(S['s5_c_im'].reshape(S5_G, S5_C, S5_P))], axis=0).astype(bf16)

    g1, g2n, g3, g4 = S['norm_mix_pre'], S['norm_mix_post'], S['norm_ffn_pre'], S['norm_ffn_post']
    (h1,) = _rowcall("norm_pre", lambda i, n, R, P, X, C: ((_rms(R[0], C[0]),), ()), L, TS, [x], [g1],
                     out_rows=[row(D_MODEL, bf16)])
    p = _mm(h1, w_p, 'nt', "mm_p")
    u = _mm(h1, w_u, 'nt', "mm_u")
    gp = _mm(h1, w_g, 'nt', "mm_g")

    def prep_fn(i, n, R, P, X, C):
        q = R[0] + (_shift_down(R[0], P[0], i, 1) - R[0]) * C[0]
        return _prep(q, *C[1:]), ()

    r, lw, k2, v, an, bv, g = _rowcall("rwkv_prep", prep_fn, L, TS, [p], prep_consts,
                                       out_rows=[row(RWKV_W)] * 7, prev=[0])
    y, ck = _wkv7_fwd(r, lw, k2, v, an, bv)
    (o_a,) = _rowcall("rwkv_out", lambda i, n, R, P, X, C: ((_rwkv_out(*R, *C),), ()), L, TS, [y, r, k2, v, g],
                      out_consts, out_rows=[row(RWKV_W, bf16)])
    o_r = _mm(o_a, W['w_branch_rwkv'], 'nn', "mm_br")

    st, ysc, got = _s5_fwd(u, bmat, cmat, abar, late)
    W = {**W, **_gather_pair(got, list(got), "gather_weights_pair_late")}
    (yg,) = _rowcall("s5_mid", lambda i, n, R, P, X, C: ((_s5_mid(*R, *C),), ()), L, TS, [ysc, u], [S['s5_d']],
                     out_rows=[row(S5_W)])
    z2 = _mm(yg, W['s5_w_glu'], 'nn', "mm_glu")
    (o_b,) = _rowcall("s5_glu", lambda i, n, R, P, X, C: ((_s5_glu(*R, *C),), ()), L, TS, [yg, z2], [S['s5_b_glu']],
                      out_rows=[row(S5_W, bf16)])
    o_s = _mm(o_b, W['w_branch_s5'], 'nn', "mm_bs")

    (merged,) = _rowcall("merge", lambda i, n, R, P, X, C: ((_merge(*R, *C),), ()), L, TS, [gp, o_r, o_s],
                         [S['b_gate']], out_rows=[row(D_MODEL, bf16)])
    mixed = _mm(merged, W['w_out'], 'nn', "mm_out")

    def resid_fn(i, n, R, P, X, C):
        x1_ = R[0] + _rms(R[1], C[0])
        return (x1_, _rms(x1_, C[1])), ()

    x1, h2 = _rowcall("resid_norm", resid_fn, L, TS, [x, mixed], [g2n, g3], out_rows=[row(D_MODEL), row(D_MODEL, bf16)])

    z = _mm(h2, W['ffn_w_up'], 'nn', "mm_up")

    def conv(zt, zprev, i, cw_, cb_):
        z2s, z1s = _shift_down(zt, zprev, i, 2), _shift_down(zt, zprev, i, 1)
        return cb_ + cw_[0:1] * z2s + cw_[1:2] * z1s + cw_[2:3] * zt, z2s, z1s

    (act,) = _rowcall("conv_act", lambda i, n, R, P, X, C: ((_act(conv(R[0], P[0], i, C[0], C[1])[0]),), ()), L, TMW,
                      [z], [cw, cb], out_rows=[row(D_FF, bf16)], prev=[0])
    f = _mm(act, W['ffn_w_down'], 'nn', "mm_down")

    def final_fn(i, n, R, P, X, C):
        x1_, f_, t_ = R
        fn_, vjp = jax.vjp(_rms, f_, C[0])
        diff = x1_ + fn_ - t_
        loss = jnp.sum(diff * diff) * (0.5 / D_MODEL)
        dx2_ = diff * (1.0 / D_MODEL)
        df_, dg4_ = vjp(dx2_)
        return (df_, dx2_), (jnp.full((1, PACK_W), loss, f32), dg4_)

    df, dx2, loss, dg4 = _rowcall("loss_head", final_fn, L, TS, [x1, f, tgt], [g4],
                                  out_rows=[row(D_MODEL, bf16), row(D_MODEL)], out_accs=[(1, PACK_W), (1, D_MODEL)])
    G = {'norm_ffn_post': dg4}

    dact = _mm(df, W['ffn_w_down'], 'nt', "mm_down_dx")
    G['ffn_w_down'] = _mm(act, df, 'tn', "mm_down_dw")

    def conv_bwd_fn(i, n, R, P, X, C):
        z_, dact_ = R
        cw_, cb_ = C
        zc, z2s, z1s = conv(z_, P[0], i, cw_, cb_)
        _, vjp = jax.vjp(_act, zc)
        (dzc_,) = vjp(dact_)
        last8 = z_[z_.shape[0] - 8:]
        zcn = cb_ + cw_[0:1] * _shift_down(X[0], last8, 1, 2) + cw_[1:2] * _shift_down(X[0], last8, 1, 1) + cw_[2:3] * X[0]
        _, vjpn = jax.vjp(_act, zcn)
        (dzcn,) = vjpn(X[1])
        dz_ = (cw_[2:3] * dzc_ + cw_[1:2] * _shift_up(dzc_, dzcn, i, n, 1) + cw_[0:1] * _shift_up(dzc_, dzcn, i, n, 2))
        return (dz_,), (_sum0(dzc_), _sum0(dzc_ * z2s), _sum0(dzc_ * z1s), _sum0(dzc_ * z_))

    wide = (1, 2 * D_FF)
    dz, dcb, dcw0, dcw1, dcw2 = _rowcall("conv_act_bwd", conv_bwd_fn, L, TMW, [z, dact], [cw, cb],
                                         out_rows=[row(2 * D_FF, bf16)], out_accs=[wide] * 4, prev=[0], nxt=[0, 1])
    G['ffn_conv_b'] = dcb
    G['ffn_conv_w'] = jnp.concatenate([dcw0, dcw1, dcw2], axis=0)
    dh2 = _mm(dz, W['ffn_w_up'], 'nt', "mm_up_dx")
    G['ffn_w_up'] = _mm(h2, dz, 'tn', "mm_up_dw")

    def norm2_bwd_fn(i, n, R, P, X, C):
        x1_, mixed_, dx2_, dh2_ = R
        _, vjp3 = jax.vjp(_rms, x1_, C[1])
        dx1a, dg3_ = vjp3(dh2_)
        dx1_ = dx2_ + dx1a
        _, vjp2 = jax.vjp(_rms, mixed_, C[0])
        dmixed_, dg2_ = vjp2(dx1_)
        return (dx1_, dmixed_), (dg2_, dg3_)

    dx1, dmixed, dg2n, dg3 = _rowcall("norm_mid_bwd", norm2_bwd_fn, L, TS, [x1, mixed, dx2, dh2], [g2n, g3],
                                      out_rows=[row(D_MODEL), row(D_MODEL, bf16)], out_accs=[(1, D_MODEL)] * 2)
    G['norm_mix_post'], G['norm_ffn_pre'] = dg2n, dg3

    dmerged = _mm(dmixed, W['w_out'], 'nt', "mm_out_dx")
    G['w_out'] = _mm(merged, dmixed, 'tn', "mm_out_dw")

    def merge_bwd_fn(i, n, R, P, X, C):
        _, vjp = jax.vjp(_merge, R[0], R[1], R[2], C[0])
        dgp_, do_r_, do_s_, dbg_ = vjp(R[3])
        return (dgp_, do_r_, do_s_), (dbg_,)

    dgp, do_r, do_s, G['b_gate'] = _rowcall("merge_bwd", merge_bwd_fn, L, TS, [gp, o_r, o_s, dmerged], [S['b_gate']],
                                            out_rows=[row(2 * D_MODEL, bf16), row(D_MODEL, bf16), row(D_MODEL, bf16)],
                                            out_accs=[(1, 2 * D_MODEL)])
    do_a = _mm(do_r, W['w_branch_rwkv'], 'nt', "mm_br_dx")
    G['w_branch_rwkv'] = _mm(o_a, do_r, 'tn', "mm_br_dw")
    do_b = _mm(do_s, W['w_branch_s5'], 'nt', "mm_bs_dx")
    G['w_branch_s5'] = _mm(o_b, do_s, 'tn', "mm_bs_dw")

    def glu_bwd_fn(i, n, R, P, X, C):
        _, vjp = jax.vjp(_s5_glu, R[0], R[1], C[0])
        dyg1_, dz2_, dbg_ = vjp(R[2])
        return (dyg1_, dz2_), (dbg_,)

    dyg1, dz2, G['s5_b_glu'] = _rowcall("s5_glu_bwd", glu_bwd_fn, L, TS, [yg, z2, do_b], [S['s5_b_glu']],
                                        out_rows=[row(S5_W), row(S5_W, bf16)], out_accs=[(1, S5_W)])
    dyg2 = _mm(dz2, W['s5_w_glu'], 'nt', "mm_glu_dx")
    G['s5_w_glu'] = _mm(yg, dz2, 'tn', "mm_glu_dw")

    def mid_bwd_fn(i, n, R, P, X, C):
        _, vjp = jax.vjp(_s5_mid, R[0], R[1], C[0])
        dysc_, du_, dd_ = vjp(R[2] + R[3])
        return (dysc_, du_), (dd_,)

    dysc, du1, G['s5_d'] = _rowcall("s5_mid_bwd", mid_bwd_fn, L, TS, [ysc, u, dyg1, dyg2], [S['s5_d']],
                                    out_rows=[row(S5_W, bf16), row(S5_W)], out_accs=[(1, S5_W)])
    early = [n for n in BIG if n != 'w_in']
    recv_e, _ = _grads_to_sibling(G, early, "grads_to_sibling_early")
    chip_e, _ = _pair_add(G, recv_e, early, "grads_pair_sum_early")
    du, dbmat, dcmat, dabar, slots_e = _s5_bwd(dysc, st, u, du1, bmat, cmat, abar, chip_e)
    da_re, da_im, dls, db_re, db_im = _s5_disc_bwd(
        a_re, a_im, ls, b_re, b_im, dabar[:, :S5_N].reshape(S5_N, 1), dabar[:, S5_N:].reshape(S5_N, 1),
        undiag_in(dbmat[:4]), undiag_in(dbmat[4:]), seg)
    G['s5_a_re'], G['s5_a_im'], G['s5_log_step'] = da_re, da_im, dls
    G['s5_b_re'], G['s5_b_im'] = db_re, db_im
    G['s5_c_re'], G['s5_c_im'] = undiag_out(dcmat[:4]), -undiag_out(dcmat[4:])

    def out_bwd_fn(i, n, R, P, X, C):
        _, vjp = jax.vjp(_rwkv_out, *R[:5], *C)
        gs = vjp(R[5])
        return gs[:5], gs[5:8]

    dy, dr1, dk1, dv1, dg, dlw, dlb, drk = _rowcall("rwkv_out_bwd", out_bwd_fn, L, TM, [y, r, k2, v, g, do_a], out_consts,
                                                    out_rows=[row(RWKV_W)] * 5, out_accs=[(1, RWKV_W)] * 3)
    G['rwkv_lnx_w'], G['rwkv_lnx_b'], G['rwkv_r_k'] = dlw, dlb, drk
    dr2, dlwk, dk2b, dv2, dan, dbv = _wkv7_bwd(r, lw, k2, v, an, bv, ck, dy)

    def prep_bwd_fn(i, n, R, P, X, C):
        p_ = R[0]
        d1 = _shift_down(p_, P[0], i, 1) - p_
        q = p_ + d1 * C[0]
        _, vjp = jax.vjp(_prep, q, *C[1:])
        cots = (R[1] + R[2], R[3], R[4] + R[5], R[6] + R[7], R[8], R[9], R[10])
        gs = vjp(cots)
        return (gs[0],), (_sum0(gs[0] * d1),) + tuple(gs[1:8])

    small, lowr = (1, RWKV_W), (128, RWKV_W)
    dq, dmu, dw0, da0, dkk, dka, dw2p, da2p, dg2 = _rowcall(
        "rwkv_prep_bwd", prep_bwd_fn, L, TM, [p, dr1, dr2, dlwk, dk1, dk2b, dv1, dv2, dan, dbv, dg],
        prep_consts, out_rows=[row(N_RWKV)], out_accs=[(1, N_RWKV)] + [small] * 4 + [lowr] * 3, prev=[0])
    G['rwkv_shift_mu'], G['rwkv_w0'], G['rwkv_a0'], G['rwkv_k_k'], G['rwkv_k_a'] = dmu, dw0, da0, dkk, dka
    G['rwkv_w2'], G['rwkv_a2'], G['rwkv_g2'] = dw2p[:64], da2p[64:], dg2

    def shift_bwd_fn(i, n, R, P, X, C):
        dm = R[0] * C[0]
        return (R[0] - dm + _shift_up(dm, X[0] * C[0], i, n, 1),), ()

    (dp,) = _rowcall("shift_bwd", shift_bwd_fn, L, TS, [dq], [S['rwkv_shift_mu']], out_rows=[row(N_RWKV, bf16)], nxt=[0])

    dproj = jnp.concatenate([dp, du, dgp], axis=1)
    dh1 = _mm(dproj, w_in_t, 'nn', "mm_in_dx")
    G['w_in'] = _mm(dproj, h1, 'tn', "mm_in_dw")

    def norm1_bwd_fn(i, n, R, P, X, C):
        _, vjp = jax.vjp(_rms, R[0], C[0])
        dxa, dg1_ = vjp(R[2])
        return (R[1] + dxa,), (dg1_,)

    dx, G['norm_mix_pre'] = _rowcall("norm_pre_bwd", norm1_bwd_fn, L, TS, [x, dx1, dh1], [g1],
                                     out_rows=[row(D_MODEL)], out_accs=[(1, D_MODEL)])
    return loss, dx, G, chip_e, slots_e


def kernel(x, norm_mix_pre, norm_mix_post, norm_ffn_pre, norm_ffn_post, w_in, b_gate, rwkv_shift_mu, rwkv_w0, rwkv_w2, rwkv_a0, rwkv_a2, rwkv_g2, rwkv_k_k, rwkv_k_a, rwkv_r_k, rwkv_lnx_w, rwkv_lnx_b, s5_a_re, s5_a_im, s5_b_re, s5_b_im, s5_c_re, s5_c_im, s5_d, s5_log_step, s5_w_glu, s5_b_glu, w_branch_rwkv, w_branch_s5, w_out, ffn_w_up, ffn_conv_w, ffn_conv_b, ffn_w_down, loss_target, m_norm_mix_pre, m_norm_mix_post, m_norm_ffn_pre, m_norm_ffn_post, m_w_in, m_b_gate, m_rwkv_shift_mu, m_rwkv_w0, m_rwkv_w2, m_rwkv_a0, m_rwkv_a2, m_rwkv_g2, m_rwkv_k_k, m_rwkv_k_a, m_rwkv_r_k, m_rwkv_lnx_w, m_rwkv_lnx_b, m_s5_a_re, m_s5_a_im, m_s5_b_re, m_s5_b_im, m_s5_c_re, m_s5_c_im, m_s5_d, m_s5_log_step, m_s5_w_glu, m_s5_b_glu, m_w_branch_rwkv, m_w_branch_s5, m_w_out, m_ffn_w_up, m_ffn_conv_w, m_ffn_conv_b, m_ffn_w_down, v_norm_mix_pre, v_norm_mix_post, v_norm_ffn_pre, v_norm_ffn_post, v_w_in, v_b_gate, v_rwkv_shift_mu, v_rwkv_w0, v_rwkv_w2, v_rwkv_a0, v_rwkv_a2, v_rwkv_g2, v_rwkv_k_k, v_rwkv_k_a, v_rwkv_r_k, v_rwkv_lnx_w, v_rwkv_lnx_b, v_s5_a_re, v_s5_a_im, v_s5_b_re, v_s5_b_im, v_s5_c_re, v_s5_c_im, v_s5_d, v_s5_log_step, v_s5_w_glu, v_s5_b_glu, v_w_branch_rwkv, v_w_branch_s5, v_w_out, v_ffn_w_up, v_ffn_conv_w, v_ffn_conv_b, v_ffn_w_down):
    A = dict(locals())
    me = 2 * lax.axis_index("x") + lax.axis_index("y")
    blk = lambda n: A[n][0]

    mine = {n: (blk(n).T if n == 'w_in' else blk(n)).astype(bf16) for n in BIG}
    mine.update({n: blk(n) for n in TINY})
    mine['ffn_conv_w'] = jnp.pad(blk('ffn_conv_w'), ((0, 5), (0, 0)))
    late = ['ffn_w_up', 'ffn_w_down']
    W = _gather_weights({n: blkv for n, blkv in mine.items() if n not in late})
    W.update(_gather_pair(W, [n for n in BIG if n not in late], "gather_weights_pair"))
    S = {n: A[n].reshape(1, -1) for n in SMALL}

    loss, dx, G, chip_e, slots_e = _forward_backward(x[0], loss_target[0], W, S, {n: mine[n] for n in late})

    tiny_shapes = [G[n].shape for n in TINY]
    small_buf = _pack_rows([G[n] for n in SMALL] + [G[n] for n in TINY] + [loss], SMALL_ROWS)
    recv, small_recv = _grads_to_sibling(G, ['w_in'], "grads_to_sibling", small_buf)
    chip_l, small_sum = _pair_add(G, recv, ['w_in'], "grads_pair_sum", small_buf, small_recv)
    slots_l, small4 = _grads_chip_exchange(chip_l, ['w_in'], small_sum)
    half, half['small'] = _sum_slots({**slots_e, **slots_l}, {**chip_e, **chip_l}, small4, small_sum)
    other = _halves_to_sibling(half)
    pc = lax.axis_index("c")
    small_tot = _join_halves(half['small'], other['small'], pc)
    grad = {n: _join_halves(half[n], other[n], pc) for n in BIG}
    grad['w_in'] = grad['w_in'].T
    vals = _unpack_rows(small_tot, [A[n].shape for n in SMALL] + tiny_shapes + [(1, PACK_W)])
    grad.update(zip(SMALL, vals))
    for n, full in zip(TINY, vals[len(SMALL):]):
        cs = A[n].shape[2]
        grad[n] = lax.dynamic_slice_in_dim(full, me * cs, cs, axis=1)
    loss_out = vals[-1][0, 0]

    packed = SMALL + TINY
    groups = [(blk(n), grad[n], blk('m_' + n), blk('v_' + n)) for n in BIG]
    groups.append(tuple(_pack_rows([src(n) for n in packed], ADAM_ROWS)
                        for src in (lambda n: A[n], lambda n: grad[n], lambda n: A['m_' + n], lambda n: A['v_' + n])))
    res = _adamw(groups)
    outs = [dict(), dict(), dict()]
    for n, r3 in zip(BIG, res[:-1]):
        for d, val in zip(outs, r3):
            d[n] = val
    for d, buf in zip(outs, res[-1]):
        d.update(zip(packed, _unpack_rows(buf, [A[n].shape for n in packed])))
    full = lambda d: [d[n].reshape(A[n].shape) for n in WEIGHTS]
    return (loss_out, dx[None], *full(grad), *full(outs[0]), *full(outs[1]), *full(outs[2]))
```

```python
import functools

import jax
import jax.numpy as jnp
from jax import lax
from jax.experimental import pallas as pl
from jax.experimental.pallas import tpu as pltpu

f32, bf16 = jnp.float32, jnp.bfloat16
MESH = pl.DeviceIdType.MESH

D_MODEL = 1024
RWKV_W = 512
HEADS, HEAD = 8, 64
N_RWKV = 1792
S5_W = 512
S5_G, S5_P, S5_C = 32, 64, 16
S5_N = S5_G * S5_P
D_FF = 2816
NORM_EPS = 1e-6
LNX_EPS = 64e-5
ADAM_LR, ADAM_B1, ADAM_B2, ADAM_EPS, ADAM_WD, ADAM_STEP = 0.001, 0.9, 0.999, 1e-08, 0.01, 10

VMEM_LIMIT = 48 * 1024 * 1024
PACK_W = 1024
WKV_C = 64
WKV_SUB = 2
WKV_ROWS = WKV_C * WKV_SUB
RESIDENT_BUDGET = 40 * 1024 * 1024
S5_T = 256

WEIGHTS = ['norm_mix_pre', 'norm_mix_post', 'norm_ffn_pre', 'norm_ffn_post', 'w_in', 'b_gate', 'rwkv_shift_mu',
           'rwkv_w0', 'rwkv_w2', 'rwkv_a0', 'rwkv_a2', 'rwkv_g2', 'rwkv_k_k', 'rwkv_k_a', 'rwkv_r_k', 'rwkv_lnx_w',
           'rwkv_lnx_b', 's5_a_re', 's5_a_im', 's5_b_re', 's5_b_im', 's5_c_re', 's5_c_im', 's5_d', 's5_log_step',
           's5_w_glu', 's5_b_glu', 'w_branch_rwkv', 'w_branch_s5', 'w_out', 'ffn_w_up', 'ffn_conv_w', 'ffn_conv_b',
           'ffn_w_down']


def _ceil_to(n, m):
    return -(-n // m) * m


def _mesh_pos():
    return lax.axis_index("x"), lax.axis_index("y"), lax.axis_index("c")


def _pick(d, cap=4096):
    for c in (1024, 1408, 2176, 896, 512, 256, 128):
        if c <= cap and d % c == 0:
            return c
    raise ValueError(d)


def _mm_resident(a, w, mode, name, M, N, K, out_dtype):
    budget = RESIDENT_BUDGET - 2 * K * N
    tm = next(t for t in (512, 256, 128) if 2 * t * (K * a.dtype.itemsize + 4 * N) <= budget)
    dims = _DIMS[mode]

    def body(a_ref, w_ref, o_ref):
        o_ref[...] = lax.dot_general(a_ref[...].astype(bf16), w_ref[...], (dims, ((), ())),
                                     preferred_element_type=f32).astype(o_ref.dtype)

    return pl.pallas_call(
        body, name=name, grid=(M // tm,),
        in_specs=[pl.BlockSpec((tm, K), lambda i: (i, 0)),
                  pl.BlockSpec(w.shape, lambda i: (0, 0), pipeline_mode=pl.Buffered(1))],
        out_specs=pl.BlockSpec((tm, N), lambda i: (i, 0)), out_shape=jax.ShapeDtypeStruct((M, N), out_dtype),
        compiler_params=pltpu.CompilerParams(dimension_semantics=("parallel",), vmem_limit_bytes=VMEM_LIMIT),
    )(a, w)


def _mm(a, b, mode, name, out_dtype=f32):
    if mode == 'tn':
        (K, M), (K2, N) = a.shape, b.shape
    elif mode == 'nt':
        (M, K), (N, K2) = a.shape, b.shape
    else:
        (M, K), (K2, N) = a.shape, b.shape
    assert K == K2, (name, a.shape, b.shape)
    if mode != 'tn' and b.dtype == bf16:
        return _mm_resident(a, b, mode, name, M, N, K, out_dtype)
    if mode == 'tn':
        tm = _pick(M, 2176)
        tn = _pick(N, 512 if tm > 1408 else (1024 if tm > 1024 else 1408))
        tk = _pick(K, 1024 if a.dtype == bf16 and b.dtype == bf16 else 512)
    else:
        tm, tn, tk = _pick(M, 512), _pick(N), _pick(K)
    nk = K // tk
    dims = {'nn': ((1,), (0,)), 'nt': ((1,), (1,)), 'tn': ((0,), (0,))}[mode]

    def body(a_ref, b_ref, o_ref, acc_ref):
        k = pl.program_id(2)

        @pl.when(k == 0)
        def _():
            acc_ref[...] = jnp.zeros_like(acc_ref)

        acc_ref[...] += lax.dot_general(a_ref[...].astype(bf16), b_ref[...].astype(bf16), (dims, ((), ())),
                                        preferred_element_type=f32)

        @pl.when(k == nk - 1)
        def _():
            o_ref[...] = acc_ref[...].astype(o_ref.dtype)

    a_spec = pl.BlockSpec((tk, tm), lambda i, j, k: (k, i)) if mode == 'tn' else pl.BlockSpec((tm, tk), lambda i, j, k: (i, k))
    b_spec = pl.BlockSpec((tn, tk), lambda i, j, k: (j, k)) if mode == 'nt' else pl.BlockSpec((tk, tn), lambda i, j, k: (k, j))
    return pl.pallas_call(
        body, name=name, grid=(M // tm, N // tn, nk),
        in_specs=[a_spec, b_spec], out_specs=pl.BlockSpec((tm, tn), lambda i, j, k: (i, j)),
        out_shape=jax.ShapeDtypeStruct((M, N), out_dtype),
        scratch_shapes=[pltpu.VMEM((tm, tn), f32)],
        compiler_params=pltpu.CompilerParams(dimension_semantics=("parallel", "parallel", "arbitrary"),
                                             vmem_limit_bytes=VMEM_LIMIT),
    )(a, b)


def _rowcall(name, fn, L, tm, rows, consts=(), out_rows=(), out_accs=(), prev=(), nxt=()):
    nsteps = L // tm
    nb8 = tm // 8
    last8 = L // 8 - 1
    n_r, n_p, n_x, n_c, n_or = len(rows), len(prev), len(nxt), len(consts), len(out_rows)

    def body(*refs):
        i = pl.program_id(0)
        vals = [r[...] for r in refs[:n_r + n_p + n_x + n_c]]
        R, P = vals[:n_r], vals[n_r:n_r + n_p]
        X, C = vals[n_r + n_p:n_r + n_p + n_x], vals[n_r + n_p + n_x:]
        o_refs = refs[n_r + n_p + n_x + n_c:]
        outs_r, outs_a = fn(i, nsteps, R, P, X, C)
        for ref, v in zip(o_refs[:n_or], outs_r, strict=True):
            ref[...] = v.astype(ref.dtype)
        if out_accs:
            @pl.when(i == 0)
            def _():
                for ref in o_refs[n_or:]:
                    ref[...] = jnp.zeros_like(ref)

            for ref, v in zip(o_refs[n_or:], outs_a, strict=True):
                ref[...] += v

    def const_spec(c):
        nd = c.ndim
        return pl.BlockSpec(c.shape, lambda i: (0,) * nd)

    in_specs = ([pl.BlockSpec((tm, a.shape[1]), lambda i: (i, 0)) for a in rows]
                + [pl.BlockSpec((8, rows[j].shape[1]), lambda i: (jnp.maximum(i * nb8 - 1, 0), 0)) for j in prev]
                + [pl.BlockSpec((8, rows[j].shape[1]), lambda i: (jnp.minimum((i + 1) * nb8, last8), 0)) for j in nxt]
                + [const_spec(c) for c in consts])
    out_specs = ([pl.BlockSpec((tm, c), lambda i: (i, 0)) for c, _ in out_rows]
                 + [pl.BlockSpec(s, lambda i: (0, 0)) for s in out_accs])
    out_shape = ([jax.ShapeDtypeStruct((L, c), dt) for c, dt in out_rows]
                 + [jax.ShapeDtypeStruct(s, f32) for s in out_accs])
    args = list(rows) + [rows[j] for j in prev] + [rows[j] for j in nxt] + list(consts)
    return pl.pallas_call(
        body, name=name, grid=(nsteps,), in_specs=in_specs, out_specs=out_specs, out_shape=out_shape,
        compiler_params=pltpu.CompilerParams(dimension_semantics=("arbitrary",), vmem_limit_bytes=VMEM_LIMIT),
    )(*args)


def _shift_down(x, prev8, i, k):
    rolled = pltpu.roll(x, k, axis=0)
    pfix = jnp.where(i > 0, pltpu.roll(prev8, k, axis=0), 0.0)
    row8 = lax.broadcasted_iota(jnp.int32, pfix.shape, 0)
    top = jnp.where(row8 < k, pfix, rolled[:8])
    return top if x.shape[0] == 8 else jnp.concatenate([top, rolled[8:]], axis=0)


def _shift_up(x, next8, i, nsteps, k):
    tm = x.shape[0]
    rolled = pltpu.roll(x, tm - k, axis=0)
    nfix = jnp.where(i < nsteps - 1, pltpu.roll(next8, 8 - k, axis=0), 0.0)
    row8 = lax.broadcasted_iota(jnp.int32, nfix.shape, 0)
    bot = jnp.where(row8 >= 8 - k, nfix, rolled[tm - 8:])
    return jnp.concatenate([rolled[:tm - 8], bot], axis=0)


def _sum0(x):
    return jnp.sum(x, axis=0, keepdims=True)


def _rms(x, g):
    return x * lax.rsqrt(jnp.mean(x * x, axis=-1, keepdims=True) + NORM_EPS) * g


def _softplus(x):
    return jnp.maximum(x, 0.0) + jnp.log(1.0 + jnp.exp(-jnp.abs(x)))


def _gelu(x):
    return 0.5 * x * (1.0 + jnp.tanh(0.7978845608028654 * (x + 0.044715 * x * x * x)))


def _dot32(a, b):
    return jnp.dot(a, b, preferred_element_type=f32, precision=lax.Precision.HIGHEST)


def _seg_raw(x, E):
    hi = x.astype(bf16)
    r1 = x - hi.astype(f32)
    mid = r1.astype(bf16)
    lo = (r1 - mid.astype(f32)).astype(bf16)
    Eb = E.astype(bf16)
    dot = lambda t: jnp.dot(t, Eb, preferred_element_type=f32)
    return (dot(lo) + dot(mid)) + dot(hi)


@jax.custom_vjp
def _seg(x, E):
    return _seg_raw(x, E)


_seg.defvjp(lambda x, E: (_seg_raw(x, E), E), lambda E, g: (_seg_raw(g, E), jnp.zeros_like(E)))


def _prep(q, w0, a0, k_k, k_a, w2p, a2p, g2, E):
    r, k, v = q[:, 0:512], q[:, 512:1024], q[:, 1024:1536]
    wa, gd = q[:, 1536:1664], q[:, 1664:1792]
    wlog = -_softplus(-(w0 + _bdot(jnp.tanh(wa), w2p, 'nn'))) - 0.5
    lw = -jnp.exp(wlog)
    a = jax.nn.sigmoid(a0 + _bdot(wa, a2p, 'nn'))
    g = _bdot(jax.nn.sigmoid(gd), g2, 'nn')
    kk = k * k_k
    kkn = kk / jnp.maximum(jnp.sqrt(_seg(kk * kk, E)), 1e-12)
    k2 = k * (1.0 + (a - 1.0) * k_a)
    return r, lw, k2, v, -kkn, kkn * a, g


def _rwkv_out(y, r, k2, v, g, lnx_w, lnx_b, r_k, E):
    mean = _seg(y, E) * (1.0 / HEAD)
    yc = y - mean
    var = _seg(yc * yc, E) * (1.0 / HEAD)
    yn = yc * lax.rsqrt(var + LNX_EPS) * lnx_w + lnx_b
    bonus = _seg(r * k2 * r_k, E) * v
    return (yn + bonus) * g


def _s5_mid(ysc, u, d):
    return _gelu(ysc + d * u)


def _s5_glu(yg, z2, b_glu):
    return yg * jax.nn.sigmoid(z2 + b_glu)


def _merge(gp, o_r, o_s, b_gate):
    gates = jax.nn.sigmoid(gp + b_gate)
    return gates[:, :D_MODEL] * o_r + gates[:, D_MODEL:] * o_s


def _act(zc):
    return _gelu(zc[:, :D_FF]) * zc[:, D_FF:]


def _s5_disc(a_re, a_im, ls, b_re, b_im):
    dt = jnp.exp(ls)
    er = jnp.exp(a_re * dt)
    ar, ai = er * jnp.cos(a_im * dt), er * jnp.sin(a_im * dt)
    x, y = ar - 1.0, ai
    den = a_re * a_re + a_im * a_im
    fr, fi = (x * a_re + y * a_im) / den, (y * a_re - x * a_im) / den
    return ar, ai, fr * b_re - fi * b_im, fr * b_im + fi * b_re


_DIMS = {'nn': ((1,), (0,)), 'nt': ((1,), (1,)), 'tn': ((0,), (0,))}


def _raw_bdot(a, b, mode):
    return lax.dot_general(a.astype(bf16), b.astype(bf16), (_DIMS[mode], ((), ())), preferred_element_type=f32)


@functools.partial(jax.custom_vjp, nondiff_argnums=(2,))
def _bdot(a, b, mode):
    return _raw_bdot(a, b, mode)


def _bdot_fwd(a, b, mode):
    return _raw_bdot(a, b, mode), (a, b)


def _bdot_bwd(mode, res, g):
    a, b = res
    if mode == 'nn':
        return _raw_bdot(g, b, 'nt'), _raw_bdot(a, g, 'tn')
    if mode == 'nt':
        return _raw_bdot(g, b, 'nn'), _raw_bdot(g, a, 'tn')
    return _raw_bdot(b, g, 'nt'), _raw_bdot(a, g, 'nn')


_bdot.defvjp(_bdot_fwd, _bdot_bwd)


def _tri_inv_raw(A):
    n = A[0].shape[0]
    eye = (lax.broadcasted_iota(jnp.int32, (n, n), 0) == lax.broadcasted_iota(jnp.int32, (n, n), 1)).astype(f32)
    x = [eye + a for a in A]
    pw, m = A, 1
    while 2 * m < n // 2:
        pw = [_raw_bdot(p, p, 'nn') for p in pw]
        x = [xi + _raw_bdot(xi, p, 'nn') for xi, p in zip(x, pw)]
        m *= 2
    return x


@jax.custom_vjp
def _tri_inv(A):
    return _tri_inv_raw(A)


def _tri_inv_fwd(A):
    x = _tri_inv_raw(A)
    return x, x


def _tri_inv_bwd(x, g):
    return ([_raw_bdot(_raw_bdot(xi, gi, 'tn'), xi, 'nt') for xi, gi in zip(x, g)],)


_tri_inv.defvjp(_tri_inv_fwd, _tri_inv_bwd)


@jax.custom_vjp
def _inv_given(A, X):
    return X


_inv_given.defvjp(lambda A, X: (X, X),
                  lambda x, g: (_tri_inv_bwd(x, g)[0], [jnp.zeros_like(xi) for xi in x]))


def _wkv_chunk(S0, r, lw, k, v, a, b, tri, bd, xinv=None):
    C = r[0].shape[0]
    P = range(len(r))
    lane = lax.broadcasted_iota(jnp.int32, (1, 2 * HEAD), 1)
    m0, m1 = (lane < HEAD).astype(f32), (lane >= HEAD).astype(f32)
    cat = lambda *xs: jnp.concatenate(xs, axis=0)
    stack = lambda x: cat(x * m0, x * m1)
    unstack = lambda x2: m0 * x2[:C] + m1 * x2[C:]
    rid = lax.broadcasted_iota(jnp.int32, (2 * C, 2 * C), 0)
    cid = lax.broadcasted_iota(jnp.int32, (2 * C, 2 * C), 1)
    same = (rid < C) == (cid < C)
    eye2 = (rid == cid).astype(f32)
    tri2 = (same & (rid >= cid)).astype(f32)
    sl2 = tri2 - eye2
    cum = [_dot32(tri, lw[p]) for p in P]
    g = [jnp.exp(cum[p]) for p in P]
    gi = [jnp.exp(-cum[p]) for p in P]
    at = [a[p] * jnp.exp(cum[p] - lw[p]) for p in P]
    rt = [r[p] * g[p] for p in P]
    kb = [k[p] * gi[p] for p in P]
    bb = [b[p] * gi[p] for p in P]
    lhs = [cat(stack(at[p]), stack(rt[p])) for p in P]
    pb = [_bdot(lhs[p], stack(bb[p]), 'nt') for p in P]
    pk = [_bdot(lhs[p], stack(kb[p]), 'nt') for p in P]
    aab = [pb[p][:2 * C] * sl2 for p in P]
    base = [_bdot(cat(at[p], rt[p]), S0[p], 'nt') for p in P]
    t = [_bdot(cat(pk[p][:2 * C] * sl2, pk[p][2 * C:] * tri2), cat(v[p], v[p]), 'nn') for p in P]
    rhs = [cat(base[p][:C], base[p][:C]) + t[p][:2 * C] for p in P]
    x = _tri_inv(aab) if xinv is None else _inv_given(aab, xinv)
    u = [unstack(_bdot(x[p], rhs[p], 'nn')) for p in P]
    w2 = [_bdot(pb[p][2 * C:] * tri2, cat(u[p], u[p]), 'nn') for p in P]
    y = [base[p][C:] + unstack(t[p][2 * C:]) + unstack(w2[p]) for p in P]
    S1 = [g[p][C - 1:C, :] * (S0[p] + bd * _bdot(cat(v[p], u[p]), cat(kb[p], bb[p]), 'tn')) for p in P]
    return y, S1, x


def _pairs(x):
    return [x[:, 2 * HEAD * p:2 * HEAD * (p + 1)] for p in range(HEADS // 2)]


def _wkv_consts():
    tri = jnp.tril(jnp.ones((WKV_C, WKV_C), f32))
    hid = jnp.arange(2 * HEAD) // HEAD
    return tri, (hid[:, None] == hid[None, :]).astype(f32)


def _wkv_step(S0, r, lw, k, v, a, b, tri, bd, xinv=None):
    ys, xs, S = [], [], S0
    for c in range(WKV_SUB):
        sub = lambda t: [x[c * WKV_C:(c + 1) * WKV_C] for x in t]
        y, S, x = _wkv_chunk(S, sub(r), sub(lw), sub(k), sub(v), sub(a), sub(b), tri, bd, None if xinv is None else xinv[c])
        ys.append(y)
        xs.append(x)
    return [jnp.concatenate([y[p] for y in ys], axis=0) for p in range(len(S0))], S, xs


def _wkv7_fwd(r, lw, k, v, a, b):
    L = r.shape[0]
    nc, npair = L // WKV_ROWS, HEADS // 2

    def body(r_ref, lw_ref, k_ref, v_ref, a_ref, b_ref, tri_ref, bd_ref, y_ref, ck_ref, xi_ref, s_ref):
        @pl.when(pl.program_id(0) == 0)
        def _():
            s_ref[...] = jnp.zeros_like(s_ref)

        s0 = [s_ref[p] for p in range(npair)]
        for p in range(npair):
            ck_ref[0, p] = s0[p]
        y, s1, xs = _wkv_step(s0, *(_pairs(x) for x in (r_ref, lw_ref, k_ref, v_ref, a_ref, b_ref)), tri_ref[...], bd_ref[...])
        for p in range(npair):
            y_ref[:, 2 * HEAD * p:2 * HEAD * (p + 1)] = y[p]
            s_ref[p] = s1[p]
            for c in range(WKV_SUB):
                xi_ref[0, c, p] = xs[c][p].astype(xi_ref.dtype)

    row = pl.BlockSpec((WKV_ROWS, RWKV_W), lambda c: (c, 0))
    sspec = pl.BlockSpec((1, npair, 2 * HEAD, 2 * HEAD), lambda c: (c, 0, 0, 0))
    xspec = pl.BlockSpec((1, WKV_SUB, npair, 2 * HEAD, 2 * HEAD), lambda c: (c, 0, 0, 0, 0))
    return pl.pallas_call(
        body, name="wkv7_fwd", grid=(nc,),
        in_specs=[row] * 6 + [pl.BlockSpec((WKV_C, WKV_C), lambda c: (0, 0)), pl.BlockSpec((2 * HEAD, 2 * HEAD), lambda c: (0, 0))],
        out_specs=[row, sspec, xspec],
        out_shape=[jax.ShapeDtypeStruct((L, RWKV_W), f32), jax.ShapeDtypeStruct((nc, npair, 2 * HEAD, 2 * HEAD), f32),
                   jax.ShapeDtypeStruct((nc, WKV_SUB, npair, 2 * HEAD, 2 * HEAD), bf16)],
        scratch_shapes=[pltpu.VMEM((npair, 2 * HEAD, 2 * HEAD), f32)],
        compiler_params=pltpu.CompilerParams(dimension_semantics=("arbitrary",), vmem_limit_bytes=VMEM_LIMIT),
    )(r, lw, k, v, a, b, *_wkv_consts())


def _wkv7_bwd(r, lw, k, v, a, b, ck, xinv, dy):
    L = r.shape[0]
    nc, npair = L // WKV_ROWS, HEADS // 2

    def body(r_ref, lw_ref, k_ref, v_ref, a_ref, b_ref, ck_ref, xi_ref, dy_ref, tri_ref, bd_ref,
             dr_ref, dlw_ref, dk_ref, dv_ref, da_ref, db_ref, ds_ref):
        @pl.when(pl.program_id(0) == 0)
        def _():
            ds_ref[...] = jnp.zeros_like(ds_ref)

        tri, bd = tri_ref[...], bd_ref[...]
        ins = [[ck_ref[0, p] for p in range(npair)]] + [_pairs(x) for x in (r_ref, lw_ref, k_ref, v_ref, a_ref, b_ref)]
        xs = [[xi_ref[0, c, p].astype(f32) for p in range(npair)] for c in range(WKV_SUB)]
        _, vjp = jax.vjp(lambda *t: _wkv_step(*t, tri, bd, xs)[:2], *ins)
        gs = vjp((_pairs(dy_ref), [ds_ref[p] for p in range(npair)]))
        for p in range(npair):
            ds_ref[p] = gs[0][p]
            for ref, gval in zip((dr_ref, dlw_ref, dk_ref, dv_ref, da_ref, db_ref), gs[1:]):
                ref[:, 2 * HEAD * p:2 * HEAD * (p + 1)] = gval[p]

    row = pl.BlockSpec((WKV_ROWS, RWKV_W), lambda c: (nc - 1 - c, 0))
    sspec = pl.BlockSpec((1, npair, 2 * HEAD, 2 * HEAD), lambda c: (nc - 1 - c, 0, 0, 0))
    xspec = pl.BlockSpec((1, WKV_SUB, npair, 2 * HEAD, 2 * HEAD), lambda c: (nc - 1 - c, 0, 0, 0, 0))
    return pl.pallas_call(
        body, name="wkv7_bwd", grid=(nc,),
        in_specs=[row] * 6 + [sspec, xspec, row, pl.BlockSpec((WKV_C, WKV_C), lambda c: (0, 0)),
                              pl.BlockSpec((2 * HEAD, 2 * HEAD), lambda c: (0, 0))],
        out_specs=[row] * 6,
        out_shape=[jax.ShapeDtypeStruct((L, RWKV_W), f32)] * 6,
        scratch_shapes=[pltpu.VMEM((npair, 2 * HEAD, 2 * HEAD), f32)],
        compiler_params=pltpu.CompilerParams(dimension_semantics=("arbitrary",), vmem_limit_bytes=VMEM_LIMIT),
    )(r, lw, k, v, a, b, ck, xinv, dy, *_wkv_consts())


def _cmul(ar, ai, xr, xi):
    return ar * xr - ai * xi, ar * xi + ai * xr


def _scan_init(a_ref, car_ref, pw_ref, reverse):
    car_ref[...] = jnp.zeros_like(car_ref)
    ar = jnp.broadcast_to(a_ref[:, :S5_N], (8, S5_N))
    ai = jnp.broadcast_to(a_ref[:, S5_N:], (8, S5_N))
    if reverse:
        ai = -ai
    row = lax.broadcasted_iota(jnp.int32, (8, S5_N), 0)
    pr, pi = ar, ai
    qr, qi = jnp.zeros((8, S5_N), f32), jnp.zeros((8, S5_N), f32)
    for e in range(1, 9):
        sel = (row == 8 - e) if reverse else (row == e - 1)
        qr, qi = jnp.where(sel, pr, qr), jnp.where(sel, pi, qi)
        if e in (1, 2, 4):
            j = (1, 2, 4).index(e)
            pw_ref[j, :, :S5_N] = pr
            pw_ref[j, :, S5_N:] = pi
        pr, pi = _cmul(pr, pi, ar, ai)
    pw_ref[3, :, :S5_N] = qr
    pw_ref[3, :, S5_N:] = qi


def _scan_tile(x_ref, o_ref, car_ref, pw_ref, reverse):
    ng = x_ref.shape[0] // 8
    row = lax.broadcasted_iota(jnp.int32, (8, S5_N), 0)

    def group(gi, carry):
        g = (ng - 1 - gi) if reverse else gi
        t0 = pl.multiple_of(g * 8, 8)
        xr, xi = x_ref[pl.ds(t0, 8), :S5_N], x_ref[pl.ds(t0, 8), S5_N:]
        for j, d in enumerate((1, 2, 4)):
            if reverse:
                sr = jnp.where(row < 8 - d, pltpu.roll(xr, 8 - d, axis=0), 0.0)
                si = jnp.where(row < 8 - d, pltpu.roll(xi, 8 - d, axis=0), 0.0)
            else:
                sr = jnp.where(row >= d, pltpu.roll(xr, d, axis=0), 0.0)
                si = jnp.where(row >= d, pltpu.roll(xi, d, axis=0), 0.0)
            mr, mi = _cmul(pw_ref[j, :, :S5_N], pw_ref[j, :, S5_N:], sr, si)
            xr, xi = xr + mr, xi + mi
        cr, ci = carry
        mr, mi = _cmul(pw_ref[3, :, :S5_N], pw_ref[3, :, S5_N:], cr, ci)
        xr, xi = xr + mr, xi + mi
        o_ref[pl.ds(t0, 8), :S5_N] = xr
        o_ref[pl.ds(t0, 8), S5_N:] = xi
        e = 0 if reverse else 7
        return (jnp.broadcast_to(xr[e:e + 1, :], (8, S5_N)), jnp.broadcast_to(xi[e:e + 1, :], (8, S5_N)))

    cr, ci = lax.fori_loop(0, ng, group, (car_ref[:, :S5_N], car_ref[:, S5_N:]))
    car_ref[:, :S5_N] = cr
    car_ref[:, S5_N:] = ci


_CB, _SB = 128, 512


def _cblk(k):
    return slice(_CB * k, _CB * (k + 1))


def _sblk(j):
    return slice(_SB * j, _SB * (j + 1))


def _s5_fwd(u, bmat, cmat, abar, late):
    L = u.shape[0]
    nt = L // S5_T
    names = list(late)
    nh = len(names)

    def body(u_ref, b_ref, c_ref, a_ref, *rest):
        h_in, (st_ref, y_ref), h_out = rest[:nh], rest[nh:nh + 2], rest[nh + 2:2 * nh + 2]
        bu_ref, car_ref, pw_ref, ssem, rsem, lsem = rest[2 * nh + 2:]
        i = pl.program_id(0)

        def copies():
            px, py, pc = _mesh_pos()
            me = 2 * px + py
            out = []
            for a, nm in enumerate(names):
                hr = late[nm].shape[0] // 2
                src, dst = h_in[a].at[pl.ds(pl.multiple_of(pc * hr, 16), hr), :], _slab(h_out[a], nm, me, pc)
                out.append(pltpu.make_async_copy(src, dst, lsem.at[a]))
                out += [pltpu.make_async_remote_copy(src, dst, ssem.at[3 * a + k], rsem.at[3 * a + k],
                                                     device_id=(qx, qy, pc), device_id_type=MESH)
                        for k, (qx, qy) in enumerate(_chip_peers(px, py))]
            return out

        @pl.when(i == 0)
        def _():
            _scan_init(a_ref, car_ref, pw_ref, False)
            for cp in copies():
                cp.start()

        for j in range(8):
            bu_ref[:, _sblk(j)] = _raw_bdot(u_ref[:, _cblk(j % 4)], b_ref[j], 'nn')
        _scan_tile(bu_ref, st_ref, car_ref, pw_ref, False)
        for k in range(4):
            y_ref[:, _cblk(k)] = (_raw_bdot(st_ref[:, _sblk(k)], c_ref[k], 'nn')
                                  + _raw_bdot(st_ref[:, _sblk(4 + k)], c_ref[4 + k], 'nn'))

        @pl.when(i == nt - 1)
        def _():
            for cp in copies():
                cp.wait()

    whole = lambda shape: pl.BlockSpec(shape, lambda i: (0,) * len(shape))
    outs = pl.pallas_call(
        body, name="s5_fwd", grid=(nt,),
        in_specs=[pl.BlockSpec((S5_T, S5_W), lambda i: (i, 0)), whole(bmat.shape), whole(cmat.shape), whole(abar.shape)]
        + [ANY] * nh,
        out_specs=[pl.BlockSpec((S5_T, 2 * S5_N), lambda i: (i, 0)), pl.BlockSpec((S5_T, S5_W), lambda i: (i, 0))] + [ANY] * nh,
        out_shape=[jax.ShapeDtypeStruct((L, 2 * S5_N), f32), jax.ShapeDtypeStruct((L, S5_W), f32)]
        + [jax.ShapeDtypeStruct(GATHER[nm][0], late[nm].dtype) for nm in names],
        scratch_shapes=[pltpu.VMEM((S5_T, 2 * S5_N), f32), pltpu.VMEM((8, 2 * S5_N), f32), pltpu.VMEM((4, 8, 2 * S5_N), f32),
                        pltpu.SemaphoreType.DMA((3 * nh,)), pltpu.SemaphoreType.DMA((3 * nh,)), pltpu.SemaphoreType.DMA((nh,))],
        compiler_params=pltpu.CompilerParams(dimension_semantics=("arbitrary",), vmem_limit_bytes=VMEM_LIMIT),
    )(u, bmat, cmat, abar, *[late[nm] for nm in names])
    return outs[0], outs[1], dict(zip(names, outs[2:]))


def _s5_bwd(dy, st, u, du_direct, bmat, cmat, abar, chip_sum):
    L = u.shape[0]
    nt = L // S5_T
    nb8 = S5_T // 8
    names = list(chip_sum)
    nh = len(names)

    def body(dy_ref, st_ref, sp_ref, u_ref, dud_ref, b_ref, c_ref, a_ref, *rest):
        x_in, (du_ref, db_ref, dc_ref, da_ref), x_out = rest[:nh], rest[nh:nh + 4], rest[nh + 4:2 * nh + 4]
        lam_ref, car_ref, pw_ref, ssem, rsem = rest[2 * nh + 4:]
        i = pl.program_id(0)

        @pl.when(i == 0)
        def _():
            _scan_init(a_ref, car_ref, pw_ref, True)
            db_ref[...] = jnp.zeros_like(db_ref)
            dc_ref[...] = jnp.zeros_like(dc_ref)
            da_ref[...] = jnp.zeros_like(da_ref)
            for cp in _exchange_copies(x_in, x_out, ssem, rsem):
                cp.start()

        for j in range(8):
            lam_ref[:, _sblk(j)] = _raw_bdot(dy_ref[:, _cblk(j % 4)], c_ref[j], 'nt')
        _scan_tile(lam_ref, lam_ref, car_ref, pw_ref, True)
        for k in range(4):
            du_ref[:, _cblk(k)] = (dud_ref[:, _cblk(k)] + _raw_bdot(lam_ref[:, _sblk(k)], b_ref[k], 'nt')
                                   + _raw_bdot(lam_ref[:, _sblk(4 + k)], b_ref[4 + k], 'nt')
                                   ).astype(du_ref.dtype)
            sr = _shift_down(st_ref[:, _sblk(k)], sp_ref[:, _sblk(k)], nt - 1 - i, 1)
            si = _shift_down(st_ref[:, _sblk(4 + k)], sp_ref[:, _sblk(4 + k)], nt - 1 - i, 1)
            lr, li = lam_ref[:, _sblk(k)], lam_ref[:, _sblk(4 + k)]
            da_ref[:, _sblk(k)] += _sum0(lr * sr + li * si)
            da_ref[:, _sblk(4 + k)] += _sum0(li * sr - lr * si)
        for j in range(8):
            db_ref[j] += _raw_bdot(u_ref[:, _cblk(j % 4)], lam_ref[:, _sblk(j)], 'tn')
            dc_ref[j] += _raw_bdot(st_ref[:, _sblk(j)], dy_ref[:, _cblk(j % 4)], 'tn')

        @pl.when(i == nt - 1)
        def _():
            for cp in _exchange_copies(x_in, x_out, ssem, rsem):
                cp.wait()

    whole = lambda shape: pl.BlockSpec(shape, lambda i: (0,) * len(shape))
    rev = lambda i: (nt - 1 - i, 0)
    outs = pl.pallas_call(
        body, name="s5_bwd", grid=(nt,),
        in_specs=[pl.BlockSpec((S5_T, S5_W), rev), pl.BlockSpec((S5_T, 2 * S5_N), rev),
                  pl.BlockSpec((8, 2 * S5_N), lambda i: (jnp.maximum((nt - 1 - i) * nb8 - 1, 0), 0)),
                  pl.BlockSpec((S5_T, S5_W), rev), pl.BlockSpec((S5_T, S5_W), rev), whole(bmat.shape), whole(cmat.shape),
                  whole(abar.shape)] + [ANY] * nh,
        out_specs=[pl.BlockSpec((S5_T, S5_W), rev), whole((8, _CB, _SB)), whole((8, _SB, _CB)), whole((1, 2 * S5_N))]
        + [ANY] * nh,
        out_shape=[jax.ShapeDtypeStruct((L, S5_W), bf16), jax.ShapeDtypeStruct((8, _CB, _SB), f32),
                   jax.ShapeDtypeStruct((8, _SB, _CB), f32), jax.ShapeDtypeStruct((1, 2 * S5_N), f32)]
        + [jax.ShapeDtypeStruct(chip_sum[nm].shape, chip_sum[nm].dtype) for nm in names],
        scratch_shapes=[pltpu.VMEM((S5_T, 2 * S5_N), f32), pltpu.VMEM((8, 2 * S5_N), f32), pltpu.VMEM((4, 8, 2 * S5_N), f32),
                        pltpu.SemaphoreType.DMA((3 * nh,)), pltpu.SemaphoreType.DMA((3 * nh,))],
        compiler_params=pltpu.CompilerParams(dimension_semantics=("arbitrary",), vmem_limit_bytes=VMEM_LIMIT),
    )(dy, st, st, u, du_direct, bmat, cmat, abar, *[chip_sum[nm] for nm in names])
    return outs[0], outs[1], outs[2], outs[3], dict(zip(names, outs[4:]))


def _s5_disc_fwd(a_re, a_im, ls, b_re, b_im):
    def body(a_re_ref, a_im_ref, ls_ref, b_re_ref, b_im_ref, ar_ref, ai_ref, br_ref, bi_ref):
        outs = _s5_disc(a_re_ref[...], a_im_ref[...], ls_ref[...], b_re_ref[...], b_im_ref[...])
        for ref, v in zip((ar_ref, ai_ref, br_ref, bi_ref), outs):
            ref[...] = v

    c1, c16 = jax.ShapeDtypeStruct((S5_N, 1), f32), jax.ShapeDtypeStruct((S5_N, S5_C), f32)
    return pl.pallas_call(body, name="s5_disc", out_shape=[c1, c1, c16, c16])(a_re, a_im, ls, b_re, b_im)


def _s5_disc_bwd(a_re, a_im, ls, b_re, b_im, d_ar, d_ai, d_br, d_bi, seg):
    def body(a_re_ref, a_im_ref, ls_ref, b_re_ref, b_im_ref, g1, g2, g3, g4, seg_ref, o1, o2, o3, o4, o5):
        _, vjp = jax.vjp(_s5_disc, a_re_ref[...], a_im_ref[...], ls_ref[...], b_re_ref[...], b_im_ref[...])
        da_re, da_im, dls, db_re, db_im = vjp((g1[...], g2[...], g3[...], g4[...]))
        o1[...] = da_re
        o2[...] = da_im
        o3[...] = _dot32(seg_ref[...], dls)
        o4[...] = db_re
        o5[...] = db_im

    c1, c16 = jax.ShapeDtypeStruct((S5_N, 1), f32), jax.ShapeDtypeStruct((S5_N, S5_C), f32)
    return pl.pallas_call(body, name="s5_disc_bwd", out_shape=[c1, c1, jax.ShapeDtypeStruct((S5_G, 1), f32), c16, c16])(
        a_re, a_im, ls, b_re, b_im, d_ar, d_ai, d_br, d_bi, seg)


ANY = pl.BlockSpec(memory_space=pl.ANY)

GATHER = {'w_in': ((4352, 1024), 0), 'ffn_w_up': ((1024, 5632), 1), 'w_branch_rwkv': ((512, 1024), 1),
          'w_branch_s5': ((512, 1024), 1), 'w_out': ((1024, 1024), 0), 's5_w_glu': ((512, 512), 0),
          'ffn_w_down': ((2816, 1024), 0), 'rwkv_w2': ((64, 512), 1), 'rwkv_a2': ((64, 512), 1),
          'rwkv_g2': ((128, 512), 1), 'ffn_conv_w': ((8, 5632), 1)}
BIG = ['w_in', 'ffn_w_up', 'w_branch_rwkv', 'w_branch_s5', 'w_out', 's5_w_glu', 'ffn_w_down']
TINY = ['rwkv_w2', 'rwkv_a2', 'rwkv_g2', 'ffn_conv_w']
SMALL = [n for n in WEIGHTS if n not in GATHER]
SMALL_ROWS = 320
ADAM_ROWS = 256


def _mo(v, m):
    return v if isinstance(v, int) else pl.multiple_of(v, m)


def _slab(ref, name, j, h=None):
    (R, Cn), axis = GATHER[name]
    if axis == 0:
        rs = R // 4
        if h is None:
            return ref.at[pl.ds(_mo(j * rs, 16), rs), :]
        return ref.at[pl.ds(_mo(j * rs + h * (rs // 2), 8), rs // 2), :]
    cols = pl.ds(_mo(j * (Cn // 4), 128), Cn // 4)
    if h is None:
        return ref.at[:, cols]
    return ref.at[pl.ds(_mo(h * (R // 2), 8), R // 2), cols]


def _half_shape(name):
    (R, Cn), axis = GATHER[name]
    return (R // 8, Cn) if axis == 0 else (R // 2, Cn // 4)


def _chip_peers(px, py):
    return [((1 - px) if (k >> 1) else px, (1 - py) if (k & 1) else py) for k in (1, 2, 3)]


def _run_copies(copies):
    for cp in copies:
        cp.start()
    for cp in copies:
        cp.wait()


def _gather_weights(blocks):
    names = list(blocks)
    n = len(names)

    def body(*refs):
        ins, outs = refs[:n], refs[n:2 * n]
        ssem, rsem, lsem = refs[2 * n:]
        px, py, pc = _mesh_pos()
        me = 2 * px + py
        copies = []
        for i, nm in enumerate(names):
            if nm in BIG:
                hr = blocks[nm].shape[0] // 2
                src, dst = ins[i].at[pl.ds(pl.multiple_of(pc * hr, 16), hr), :], _slab(outs[i], nm, me, pc)
            else:
                src, dst = ins[i], _slab(outs[i], nm, me)
            copies.append(pltpu.make_async_copy(src, dst, lsem.at[i]))
            for k, (qx, qy) in enumerate(_chip_peers(px, py)):
                copies.append(pltpu.make_async_remote_copy(src, dst, ssem.at[3 * i + k], rsem.at[3 * i + k],
                                                           device_id=(qx, qy, pc), device_id_type=MESH))
        _run_copies(copies)

    outs = pl.pallas_call(
        body, name="gather_weights", in_specs=[ANY] * n, out_specs=[ANY] * n,
        out_shape=[jax.ShapeDtypeStruct(GATHER[nm][0], blocks[nm].dtype) for nm in names],
        scratch_shapes=[pltpu.SemaphoreType.DMA((3 * n,)), pltpu.SemaphoreType.DMA((3 * n,)), pltpu.SemaphoreType.DMA((n,))],
    )(*[blocks[nm] for nm in names])
    return dict(zip(names, outs))


def _gather_pair(full, names, call_name):
    n = len(names)

    def body(*refs):
        ins, outs = refs[:n], refs[n:2 * n]
        ssem, rsem = refs[2 * n:]
        px, py, pc = _mesh_pos()
        copies = []
        for i, nm in enumerate(names):
            for j in range(4):
                copies.append(pltpu.make_async_remote_copy(_slab(ins[i], nm, j, pc), _slab(outs[i], nm, j, pc),
                                                           ssem.at[4 * i + j], rsem.at[4 * i + j],
                                                           device_id=(px, py, 1 - pc), device_id_type=MESH))
        _run_copies(copies)

    outs = pl.pallas_call(
        body, name=call_name, in_specs=[ANY] * n, out_specs=[ANY] * n,
        out_shape=[jax.ShapeDtypeStruct(full[nm].shape, full[nm].dtype) for nm in names],
        input_output_aliases={i: i for i in range(n)},
        scratch_shapes=[pltpu.SemaphoreType.DMA((4 * n,)), pltpu.SemaphoreType.DMA((4 * n,))],
    )(*[full[nm] for nm in names])
    return dict(zip(names, outs))


def _grads_to_sibling(G, names, call_name, small=None):
    n = len(names)
    ns = 0 if small is None else 1

    def body(*refs):
        g_refs, o_refs = refs[:n + ns], refs[n + ns:2 * (n + ns)]
        ssem, rsem = refs[2 * (n + ns):]
        px, py, pc = _mesh_pos()
        sib = (px, py, 1 - pc)
        copies = []
        for i, nm in enumerate(names):
            for j in range(4):
                copies.append(pltpu.make_async_remote_copy(_slab(g_refs[i], nm, j, 1 - pc), o_refs[i].at[j],
                                                           ssem.at[4 * i + j], rsem.at[4 * i + j],
                                                           device_id=sib, device_id_type=MESH))
        if ns:
            copies.append(pltpu.make_async_remote_copy(g_refs[n], o_refs[n], ssem.at[4 * n], rsem.at[4 * n],
                                                       device_id=sib, device_id_type=MESH))
        _run_copies(copies)

    outs = pl.pallas_call(
        body, name=call_name, in_specs=[ANY] * (n + ns), out_specs=[ANY] * (n + ns),
        out_shape=[jax.ShapeDtypeStruct((4,) + _half_shape(nm), f32) for nm in names]
        + [jax.ShapeDtypeStruct((SMALL_ROWS, PACK_W), f32)] * ns,
        scratch_shapes=[pltpu.SemaphoreType.DMA((4 * n + ns,)), pltpu.SemaphoreType.DMA((4 * n + ns,))],
    )(*[G[nm] for nm in names], *([small] * ns))
    return dict(zip(names, outs[:n])), (outs[n] if ns else None)


def _pair_add(G, recv, names, call_name, small=None, small_recv=None):
    n = len(names)
    ns = 0 if small is None else 1
    cidx = lax.axis_index("c").astype(jnp.int32).reshape(1)

    def body(c_ref, *refs):
        ins, outs = refs[:2 * (n + ns)], refs[2 * (n + ns):]
        for i in range(n):
            outs[i][...] = (ins[i][...] + ins[n + ns + i][...]).astype(bf16)
        if ns:
            outs[n][...] = ins[n][...] + ins[2 * n + 1][...]

    g_specs, r_specs = [], []
    for nm in names:
        hr, hc = _half_shape(nm)
        if GATHER[nm][1] == 0:
            g_specs.append(pl.BlockSpec((hr // 2, hc), lambda j, i, c: ((2 * j + c[0]) * 2 + i, 0)))
        else:
            g_specs.append(pl.BlockSpec((hr // 2, hc), lambda j, i, c: (2 * c[0] + i, j)))
        r_specs.append(pl.BlockSpec((1, hr // 2, hc), lambda j, i, c: (j, i, 0)))
    sm = [pl.BlockSpec((SMALL_ROWS // 8, PACK_W), lambda j, i, c: (2 * j + i, 0))] * ns
    outs = pl.pallas_call(
        body, name=call_name,
        grid_spec=pltpu.PrefetchScalarGridSpec(num_scalar_prefetch=1, grid=(4, 2), in_specs=g_specs + sm + r_specs + sm,
                                               out_specs=r_specs + sm),
        out_shape=[jax.ShapeDtypeStruct((4,) + _half_shape(nm), bf16) for nm in names]
        + [jax.ShapeDtypeStruct((SMALL_ROWS, PACK_W), f32)] * ns,
        compiler_params=pltpu.CompilerParams(vmem_limit_bytes=VMEM_LIMIT),
    )(cidx, *[G[nm] for nm in names], *([small] * ns), *[recv[nm] for nm in names], *([small_recv] * ns))
    return dict(zip(names, outs[:n])), (outs[n] if ns else None)


def _exchange_copies(ins, outs, ssem, rsem):
    px, py, pc = _mesh_pos()
    me = 2 * px + py
    return [pltpu.make_async_remote_copy(ins[i].at[2 * qx + qy], outs[i].at[me], ssem.at[3 * i + k], rsem.at[3 * i + k],
                                         device_id=(qx, qy, pc), device_id_type=MESH)
            for i in range(len(ins)) for k, (qx, qy) in enumerate(_chip_peers(px, py))]


def _grads_chip_exchange(chip_sum, names, small):
    n = len(names)

    def body(*refs):
        ins, outs = refs[:n + 1], refs[n + 1:2 * n + 2]
        ssem, rsem, ssem_s, rsem_s = refs[2 * n + 2:]
        px, py, pc = _mesh_pos()
        me = 2 * px + py
        copies = _exchange_copies(ins[:n], outs[:n], ssem, rsem)
        hs = SMALL_ROWS // 2
        mine = ins[n].at[pl.ds(pl.multiple_of(pc * hs, 8), hs), :]
        copies += [pltpu.make_async_remote_copy(mine, outs[n].at[me], ssem_s.at[k], rsem_s.at[k],
                                                device_id=(qx, qy, pc), device_id_type=MESH)
                   for k, (qx, qy) in enumerate(_chip_peers(px, py))]
        _run_copies(copies)

    outs = pl.pallas_call(
        body, name="grads_chip_exchange", in_specs=[ANY] * (n + 1), out_specs=[ANY] * (n + 1),
        out_shape=[jax.ShapeDtypeStruct(chip_sum[nm].shape, chip_sum[nm].dtype) for nm in names]
        + [jax.ShapeDtypeStruct((4, SMALL_ROWS // 2, PACK_W), f32)],
        scratch_shapes=[pltpu.SemaphoreType.DMA((3 * n,)), pltpu.SemaphoreType.DMA((3 * n,)),
                        pltpu.SemaphoreType.DMA((3,)), pltpu.SemaphoreType.DMA((3,))],
    )(*[chip_sum[nm] for nm in names], small)
    return dict(zip(names, outs[:n])), outs[n]


def _sum_slots(slots, chip_sum, small4, small_own):
    n = len(BIG)
    me = jnp.stack([2 * lax.axis_index("x") + lax.axis_index("y"), lax.axis_index("c")]).astype(jnp.int32)

    def body(me_ref, *refs):
        for i in range(n + 1):
            own = refs[5 * i + 4][...].astype(f32)
            own = own[0] if i < n else own
            term = [jnp.where(me_ref[0] == k, own, refs[5 * i + k][0].astype(f32)) for k in range(4)]
            refs[5 * (n + 1) + i][...] = ((term[0] + term[1]) + term[2]) + term[3]

    redirect = lambda k: (lambda i, m: (jnp.where(m[0] == k, (k + 1) % 4, k), i, 0))
    in_specs, args, specs_out, shapes = [], [], [], []
    for nm in BIG:
        hr, hc = _half_shape(nm)
        in_specs += [pl.BlockSpec((1, hr // 2, hc), redirect(k)) for k in range(4)]
        in_specs.append(pl.BlockSpec((1, hr // 2, hc), lambda i, m: (m[0], i, 0)))
        args += [slots[nm]] * 4 + [chip_sum[nm]]
        specs_out.append(pl.BlockSpec((hr // 2, hc), lambda i, m: (i, 0)))
        shapes.append(jax.ShapeDtypeStruct((hr, hc), f32))
    in_specs += [pl.BlockSpec((1, SMALL_ROWS // 4, PACK_W), redirect(k)) for k in range(4)]
    in_specs.append(pl.BlockSpec((SMALL_ROWS // 4, PACK_W), lambda i, m: (2 * m[1] + i, 0)))
    args += [small4] * 4 + [small_own]
    specs_out.append(pl.BlockSpec((SMALL_ROWS // 4, PACK_W), lambda i, m: (i, 0)))
    shapes.append(jax.ShapeDtypeStruct((SMALL_ROWS // 2, PACK_W), f32))
    outs = pl.pallas_call(
        body, name="grads_chip_sum",
        grid_spec=pltpu.PrefetchScalarGridSpec(num_scalar_prefetch=1, grid=(2,), in_specs=in_specs, out_specs=specs_out),
        out_shape=shapes, compiler_params=pltpu.CompilerParams(vmem_limit_bytes=VMEM_LIMIT),
    )(me, *args)
    return dict(zip(BIG, outs[:n])), outs[n]


def _halves_to_sibling(half):
    names = list(half)
    n = len(names)

    def body(*refs):
        ins, outs = refs[:n], refs[n:2 * n]
        ssem, rsem = refs[2 * n:]
        px, py, pc = _mesh_pos()
        _run_copies([pltpu.make_async_remote_copy(ins[i], outs[i], ssem.at[i], rsem.at[i],
                                                  device_id=(px, py, 1 - pc), device_id_type=MESH) for i in range(n)])

    outs = pl.pallas_call(
        body, name="grads_halves_to_sibling", in_specs=[ANY] * n, out_specs=[ANY] * n,
        out_shape=[jax.ShapeDtypeStruct(half[nm].shape, f32) for nm in names],
        scratch_shapes=[pltpu.SemaphoreType.DMA((n,)), pltpu.SemaphoreType.DMA((n,))],
    )(*[half[nm] for nm in names])
    return dict(zip(names, outs))


def _join_halves(mine, other, pc):
    hr = mine.shape[0]
    return lax.dynamic_slice_in_dim(jnp.concatenate([other, mine, other], axis=0), (1 - pc) * hr, 2 * hr, axis=0)


def _flat_pad(v):
    v = v.reshape(-1)
    return jnp.pad(v, (0, _ceil_to(v.shape[0], PACK_W) - v.shape[0]))


def _pack_rows(parts, rows):
    flat = jnp.concatenate([_flat_pad(p) for p in parts])
    return jnp.pad(flat, (0, rows * PACK_W - flat.shape[0])).reshape(rows, PACK_W)


def _unpack_rows(buf, shapes):
    flat = buf.reshape(-1)
    out, off = [], 0
    for shp in shapes:
        n = 1
        for d in shp:
            n *= d
        out.append(flat[off:off + n].reshape(shp))
        off += _ceil_to(n, PACK_W)
    return out


def _adamw_math(w_, g_, m_, v_):
    m2 = ADAM_B1 * m_ + (1.0 - ADAM_B1) * g_
    v2 = ADAM_B2 * v_ + (1.0 - ADAM_B2) * (g_ * g_)
    m_hat = m2 / (1.0 - ADAM_B1 ** ADAM_STEP)
    v_hat = v2 / (1.0 - ADAM_B2 ** ADAM_STEP)
    return -ADAM_LR * (m_hat / (jnp.sqrt(v_hat) + ADAM_EPS) + ADAM_WD * w_), m2, v2


def _adamw(groups):
    ng = len(groups)

    def body(*refs):
        ins, outs = refs[:4 * ng], refs[4 * ng:]
        for i in range(ng):
            res = _adamw_math(*(r[...] for r in ins[4 * i:4 * i + 4]))
            for ref, val in zip(outs[3 * i:3 * i + 3], res):
                ref[...] = val

    in_specs, out_specs, out_shape = [], [], []
    for grp in groups:
        R, Cn = grp[0].shape
        spec = pl.BlockSpec((R // 8, Cn), lambda i: (i, 0))
        in_specs += [spec] * 4
        out_specs += [spec] * 3
        out_shape += [jax.ShapeDtypeStruct((R, Cn), f32)] * 3
    outs = pl.pallas_call(
        body, name="adamw", grid=(8,), in_specs=in_specs, out_specs=out_specs, out_shape=out_shape,
        compiler_params=pltpu.CompilerParams(vmem_limit_bytes=VMEM_LIMIT),
    )(*[a for grp in groups for a in grp])
    return [tuple(outs[3 * i:3 * i + 3]) for i in range(ng)]


def _forward_backward(x, tgt, W, S, late):
    L = x.shape[0]
    TM, TMW, TS = 256, 128, 512
    row = lambda c, dt=f32: (c, dt)
    hid = jnp.arange(RWKV_W) // HEAD
    E = (hid[:, None] == hid[None, :]).astype(f32)
    seg = (jnp.arange(S5_N)[None, :] // S5_P == jnp.arange(S5_G)[:, None]).astype(f32)

    w_in_t = W['w_in']
    w_p, w_u, w_g = w_in_t[:N_RWKV], w_in_t[N_RWKV:N_RWKV + S5_W], w_in_t[N_RWKV + S5_W:]
    zpad = jnp.zeros((64, RWKV_W), f32)
    w2p = jnp.concatenate([W['rwkv_w2'], zpad], axis=0)
    a2p = jnp.concatenate([zpad, W['rwkv_a2']], axis=0)
    g2 = W['rwkv_g2']
    prep_consts = [S['rwkv_shift_mu'], S['rwkv_w0'], S['rwkv_a0'], S['rwkv_k_k'], S['rwkv_k_a'], w2p, a2p, g2, E]
    out_consts = [S['rwkv_lnx_w'], S['rwkv_lnx_b'], S['rwkv_r_k'], E]
    cw, cb = W['ffn_conv_w'][:3], S['ffn_conv_b']

    a_re, a_im = S['s5_a_re'].reshape(S5_N, 1), S['s5_a_im'].reshape(S5_N, 1)
    ls = jnp.repeat(S['s5_log_step'].reshape(S5_G, 1), S5_P, axis=0)
    b_re, b_im = S['s5_b_re'].reshape(S5_N, S5_C), S['s5_b_im'].reshape(S5_N, S5_C)
    ar, ai, bbr, bbi = _s5_disc_fwd(a_re, a_im, ls, b_re, b_im)
    abar = jnp.concatenate([ar.reshape(1, S5_N), ai.reshape(1, S5_N)], axis=1)
    eye8 = jnp.eye(8, dtype=f32)

    def blocks_in(bb):
        t = bb.reshape(4, 8, S5_P, S5_C).transpose(0, 1, 3, 2)
        return (t[:, :, :, None, :] * eye8[None, :, None, :, None]).reshape(4, _CB, _SB)

    def blocks_out(cc):
        t = cc.reshape(4, 8, S5_C, S5_P).transpose(0, 1, 3, 2)
        return (t[:, :, :, None, :] * eye8[None, :, None, :, None]).reshape(4, _SB, _CB)

    def undiag_in(blocks):
        t = blocks.reshape(4, 8, S5_C, 8, S5_P)
        t = jnp.sum(t * eye8[None, :, None, :, None], axis=3)
        return t.reshape(S5_G, S5_C, S5_P).transpose(0, 2, 1).reshape(S5_N, S5_C)

    def undiag_out(blocks):
        t = blocks.reshape(4, 8, S5_P, 8, S5_C)
        t = jnp.sum(t * eye8[None, :, None, :, None], axis=3)
        return t.reshape(S5_G, S5_P, S5_C).transpose(0, 2, 1)

    bmat = jnp.concatenate([blocks_in(bbr), blocks_in(bbi)], axis=0).astype(bf16)
    cmat = jnp.concatenate([blocks_out(S['s5_c_re'].reshape(S5_G, S5_C, S5_P)),
                            -blocks_out(S['s5_c_im'].reshape(S5_G, S5_C, S5_P))], axis=0).astype(bf16)

    g1, g2n, g3, g4 = S['norm_mix_pre'], S['norm_mix_post'], S['norm_ffn_pre'], S['norm_ffn_post']
    (h1,) = _rowcall("norm_pre", lambda i, n, R, P, X, C: ((_rms(R[0], C[0]),), ()), L, TS, [x], [g1],
                     out_rows=[row(D_MODEL, bf16)])
    p = _mm(h1, w_p, 'nt', "mm_p")
    u = _mm(h1, w_u, 'nt', "mm_u")
    gp = _mm(h1, w_g, 'nt', "mm_g")

    def prep_fn(i, n, R, P, X, C):
        q = R[0] + (_shift_down(R[0], P[0], i, 1) - R[0]) * C[0]
        return _prep(q, *C[1:]), ()

    r, lw, k2, v, an, bv, g = _rowcall("rwkv_prep", prep_fn, L, TS, [p], prep_consts,
                                       out_rows=[row(RWKV_W)] * 7, prev=[0])
    y, ck, xinv = _wkv7_fwd(r, lw, k2, v, an, bv)
    (o_a,) = _rowcall("rwkv_out", lambda i, n, R, P, X, C: ((_rwkv_out(*R, *C),), ()), L, TS, [y, r, k2, v, g],
                      out_consts, out_rows=[row(RWKV_W, bf16)])
    o_r = _mm(o_a, W['w_branch_rwkv'], 'nn', "mm_br")

    st, ysc, got = _s5_fwd(u, bmat, cmat, abar, late)
    W = {**W, **_gather_pair(got, list(got), "gather_weights_pair_late")}
    (yg,) = _rowcall("s5_mid", lambda i, n, R, P, X, C: ((_s5_mid(*R, *C),), ()), L, TS, [ysc, u], [S['s5_d']],
                     out_rows=[row(S5_W)])
    z2 = _mm(yg, W['s5_w_glu'], 'nn', "mm_glu")
    (o_b,) = _rowcall("s5_glu", lambda i, n, R, P, X, C: ((_s5_glu(*R, *C),), ()), L, TS, [yg, z2], [S['s5_b_glu']],
                      out_rows=[row(S5_W, bf16)])
    o_s = _mm(o_b, W['w_branch_s5'], 'nn', "mm_bs")

    (merged,) = _rowcall("merge", lambda i, n, R, P, X, C: ((_merge(*R, *C),), ()), L, TS, [gp, o_r, o_s],
                         [S['b_gate']], out_rows=[row(D_MODEL, bf16)])
    mixed = _mm(merged, W['w_out'], 'nn', "mm_out")

    def resid_fn(i, n, R, P, X, C):
        x1_ = R[0] + _rms(R[1], C[0])
        return (x1_, _rms(x1_, C[1])), ()

    x1, h2 = _rowcall("resid_norm", resid_fn, L, TS, [x, mixed], [g2n, g3], out_rows=[row(D_MODEL), row(D_MODEL, bf16)])

    z = _mm(h2, W['ffn_w_up'], 'nn', "mm_up")

    def conv(zt, zprev, i, cw_, cb_):
        z2s, z1s = _shift_down(zt, zprev, i, 2), _shift_down(zt, zprev, i, 1)
        return cb_ + cw_[0:1] * z2s + cw_[1:2] * z1s + cw_[2:3] * zt, z2s, z1s

    (act,) = _rowcall("conv_act", lambda i, n, R, P, X, C: ((_act(conv(R[0], P[0], i, C[0], C[1])[0]),), ()), L, TMW,
                      [z], [cw, cb], out_rows=[row(D_FF, bf16)], prev=[0])
    f = _mm(act, W['ffn_w_down'], 'nn', "mm_down")

    def final_fn(i, n, R, P, X, C):
        x1_, f_, t_ = R
        fn_, vjp = jax.vjp(_rms, f_, C[0])
        diff = x1_ + fn_ - t_
        loss = jnp.sum(diff * diff) * (0.5 / D_MODEL)
        dx2_ = diff * (1.0 / D_MODEL)
        df_, dg4_ = vjp(dx2_)
        return (df_, dx2_), (jnp.full((1, PACK_W), loss, f32), dg4_)

    df, dx2, loss, dg4 = _rowcall("loss_head", final_fn, L, TS, [x1, f, tgt], [g4],
                                  out_rows=[row(D_MODEL, bf16), row(D_MODEL)], out_accs=[(1, PACK_W), (1, D_MODEL)])
    G = {'norm_ffn_post': dg4}

    dact = _mm(df, W['ffn_w_down'], 'nt', "mm_down_dx")
    G['ffn_w_down'] = _mm(act, df, 'tn', "mm_down_dw")

    def conv_bwd_fn(i, n, R, P, X, C):
        z_, dact_ = R
        cw_, cb_ = C
        zc, z2s, z1s = conv(z_, P[0], i, cw_, cb_)
        _, vjp = jax.vjp(_act, zc)
        (dzc_,) = vjp(dact_)
        last8 = z_[z_.shape[0] - 8:]
        zcn = cb_ + cw_[0:1] * _shift_down(X[0], last8, 1, 2) + cw_[1:2] * _shift_down(X[0], last8, 1, 1) + cw_[2:3] * X[0]
        _, vjpn = jax.vjp(_act, zcn)
        (dzcn,) = vjpn(X[1])
        dz_ = (cw_[2:3] * dzc_ + cw_[1:2] * _shift_up(dzc_, dzcn, i, n, 1) + cw_[0:1] * _shift_up(dzc_, dzcn, i, n, 2))
        return (dz_,), (_sum0(dzc_), _sum0(dzc_ * z2s), _sum0(dzc_ * z1s), _sum0(dzc_ * z_))

    wide = (1, 2 * D_FF)
    dz, dcb, dcw0, dcw1, dcw2 = _rowcall("conv_act_bwd", conv_bwd_fn, L, TMW, [z, dact], [cw, cb],
                                         out_rows=[row(2 * D_FF, bf16)], out_accs=[wide] * 4, prev=[0], nxt=[0, 1])
    G['ffn_conv_b'] = dcb
    G['ffn_conv_w'] = jnp.concatenate([dcw0, dcw1, dcw2], axis=0)
    dh2 = _mm(dz, W['ffn_w_up'], 'nt', "mm_up_dx")
    G['ffn_w_up'] = _mm(h2, dz, 'tn', "mm_up_dw")

    def norm2_bwd_fn(i, n, R, P, X, C):
        x1_, mixed_, dx2_, dh2_ = R
        _, vjp3 = jax.vjp(_rms, x1_, C[1])
        dx1a, dg3_ = vjp3(dh2_)
        dx1_ = dx2_ + dx1a
        _, vjp2 = jax.vjp(_rms, mixed_, C[0])
        dmixed_, dg2_ = vjp2(dx1_)
        return (dx1_, dmixed_), (dg2_, dg3_)

    dx1, dmixed, dg2n, dg3 = _rowcall("norm_mid_bwd", norm2_bwd_fn, L, TS, [x1, mixed, dx2, dh2], [g2n, g3],
                                      out_rows=[row(D_MODEL), row(D_MODEL, bf16)], out_accs=[(1, D_MODEL)] * 2)
    G['norm_mix_post'], G['norm_ffn_pre'] = dg2n, dg3

    dmerged = _mm(dmixed, W['w_out'], 'nt', "mm_out_dx")
    G['w_out'] = _mm(merged, dmixed, 'tn', "mm_out_dw")

    def merge_bwd_fn(i, n, R, P, X, C):
        _, vjp = jax.vjp(_merge, R[0], R[1], R[2], C[0])
        dgp_, do_r_, do_s_, dbg_ = vjp(R[3])
        return (dgp_, do_r_, do_s_), (dbg_,)

    dgp, do_r, do_s, G['b_gate'] = _rowcall("merge_bwd", merge_bwd_fn, L, TS, [gp, o_r, o_s, dmerged], [S['b_gate']],
                                            out_rows=[row(2 * D_MODEL, bf16), row(D_MODEL, bf16), row(D_MODEL, bf16)],
                                            out_accs=[(1, 2 * D_MODEL)])
    do_a = _mm(do_r, W['w_branch_rwkv'], 'nt', "mm_br_dx")
    G['w_branch_rwkv'] = _mm(o_a, do_r, 'tn', "mm_br_dw")
    do_b = _mm(do_s, W['w_branch_s5'], 'nt', "mm_bs_dx")
    G['w_branch_s5'] = _mm(o_b, do_s, 'tn', "mm_bs_dw")

    def glu_bwd_fn(i, n, R, P, X, C):
        _, vjp = jax.vjp(_s5_glu, R[0], R[1], C[0])
        dyg1_, dz2_, dbg_ = vjp(R[2])
        return (dyg1_, dz2_), (dbg_,)

    dyg1, dz2, G['s5_b_glu'] = _rowcall("s5_glu_bwd", glu_bwd_fn, L, TS, [yg, z2, do_b], [S['s5_b_glu']],
                                        out_rows=[row(S5_W), row(S5_W, bf16)], out_accs=[(1, S5_W)])
    dyg2 = _mm(dz2, W['s5_w_glu'], 'nt', "mm_glu_dx")
    G['s5_w_glu'] = _mm(yg, dz2, 'tn', "mm_glu_dw")

    def mid_bwd_fn(i, n, R, P, X, C):
        _, vjp = jax.vjp(_s5_mid, R[0], R[1], C[0])
        dysc_, du_, dd_ = vjp(R[2] + R[3])
        return (dysc_, du_), (dd_,)

    dysc, du1, G['s5_d'] = _rowcall("s5_mid_bwd", mid_bwd_fn, L, TS, [ysc, u, dyg1, dyg2], [S['s5_d']],
                                    out_rows=[row(S5_W, bf16), row(S5_W)], out_accs=[(1, S5_W)])
    early = [n for n in BIG if n != 'w_in']
    recv_e, _ = _grads_to_sibling(G, early, "grads_to_sibling_early")
    chip_e, _ = _pair_add(G, recv_e, early, "grads_pair_sum_early")
    du, dbmat, dcmat, dabar, slots_e = _s5_bwd(dysc, st, u, du1, bmat, cmat, abar, chip_e)
    da_re, da_im, dls, db_re, db_im = _s5_disc_bwd(
        a_re, a_im, ls, b_re, b_im, dabar[:, :S5_N].reshape(S5_N, 1), dabar[:, S5_N:].reshape(S5_N, 1),
        undiag_in(dbmat[:4]), undiag_in(dbmat[4:]), seg)
    G['s5_a_re'], G['s5_a_im'], G['s5_log_step'] = da_re, da_im, dls
    G['s5_b_re'], G['s5_b_im'] = db_re, db_im
    G['s5_c_re'], G['s5_c_im'] = undiag_out(dcmat[:4]), -undiag_out(dcmat[4:])

    def out_bwd_fn(i, n, R, P, X, C):
        _, vjp = jax.vjp(_rwkv_out, *R[:5], *C)
        gs = vjp(R[5])
        return gs[:5], gs[5:8]

    dy, dr1, dk1, dv1, dg, dlw, dlb, drk = _rowcall("rwkv_out_bwd", out_bwd_fn, L, TM, [y, r, k2, v, g, do_a], out_consts,
                                                    out_rows=[row(RWKV_W)] * 5, out_accs=[(1, RWKV_W)] * 3)
    G['rwkv_lnx_w'], G['rwkv_lnx_b'], G['rwkv_r_k'] = dlw, dlb, drk
    dr2, dlwk, dk2b, dv2, dan, dbv = _wkv7_bwd(r, lw, k2, v, an, bv, ck, xinv, dy)

    def prep_bwd_fn(i, n, R, P, X, C):
        p_ = R[0]
        d1 = _shift_down(p_, P[0], i, 1) - p_
        q = p_ + d1 * C[0]
        _, vjp = jax.vjp(_prep, q, *C[1:])
        cots = (R[1] + R[2], R[3], R[4] + R[5], R[6] + R[7], R[8], R[9], R[10])
        gs = vjp(cots)
        return (gs[0],), (_sum0(gs[0] * d1),) + tuple(gs[1:8])

    small, lowr = (1, RWKV_W), (128, RWKV_W)
    dq, dmu, dw0, da0, dkk, dka, dw2p, da2p, dg2 = _rowcall(
        "rwkv_prep_bwd", prep_bwd_fn, L, TM, [p, dr1, dr2, dlwk, dk1, dk2b, dv1, dv2, dan, dbv, dg],
        prep_consts, out_rows=[row(N_RWKV)], out_accs=[(1, N_RWKV)] + [small] * 4 + [lowr] * 3, prev=[0])
    G['rwkv_shift_mu'], G['rwkv_w0'], G['rwkv_a0'], G['rwkv_k_k'], G['rwkv_k_a'] = dmu, dw0, da0, dkk, dka
    G['rwkv_w2'], G['rwkv_a2'], G['rwkv_g2'] = dw2p[:64], da2p[64:], dg2

    def shift_bwd_fn(i, n, R, P, X, C):
        dm = R[0] * C[0]
        return (R[0] - dm + _shift_up(dm, X[0] * C[0], i, n, 1),), ()

    (dp,) = _rowcall("shift_bwd", shift_bwd_fn, L, TS, [dq], [S['rwkv_shift_mu']], out_rows=[row(N_RWKV, bf16)], nxt=[0])

    dproj = jnp.concatenate([dp, du, dgp], axis=1)
    dh1 = _mm(dproj, w_in_t, 'nn', "mm_in_dx")
    G['w_in'] = _mm(dproj, h1, 'tn', "mm_in_dw")

    def norm1_bwd_fn(i, n, R, P, X, C):
        _, vjp = jax.vjp(_rms, R[0], C[0])
        dxa, dg1_ = vjp(R[2])
        return (R[1] + dxa,), (dg1_,)

    dx, G['norm_mix_pre'] = _rowcall("norm_pre_bwd", norm1_bwd_fn, L, TS, [x, dx1, dh1], [g1],
                                     out_rows=[row(D_MODEL)], out_accs=[(1, D_MODEL)])
    return loss, dx, G, chip_e, slots_e


def kernel(x, norm_mix_pre, norm_mix_post, norm_ffn_pre, norm_ffn_post, w_in, b_gate, rwkv_shift_mu, rwkv_w0, rwkv_w2, rwkv_a0, rwkv_a2, rwkv_g2, rwkv_k_k, rwkv_k_a, rwkv_r_k, rwkv_lnx_w, rwkv_lnx_b, s5_a_re, s5_a_im, s5_b_re, s5_b_im, s5_c_re, s5_c_im, s5_d, s5_log_step, s5_w_glu, s5_b_glu, w_branch_rwkv, w_branch_s5, w_out, ffn_w_up, ffn_conv_w, ffn_conv_b, ffn_w_down, loss_target, m_norm_mix_pre, m_norm_mix_post, m_norm_ffn_pre, m_norm_ffn_post, m_w_in, m_b_gate, m_rwkv_shift_mu, m_rwkv_w0, m_rwkv_w2, m_rwkv_a0, m_rwkv_a2, m_rwkv_g2, m_rwkv_k_k, m_rwkv_k_a, m_rwkv_r_k, m_rwkv_lnx_w, m_rwkv_lnx_b, m_s5_a_re, m_s5_a_im, m_s5_b_re, m_s5_b_im, m_s5_c_re, m_s5_c_im, m_s5_d, m_s5_log_step, m_s5_w_glu, m_s5_b_glu, m_w_branch_rwkv, m_w_branch_s5, m_w_out, m_ffn_w_up, m_ffn_conv_w, m_ffn_conv_b, m_ffn_w_down, v_norm_mix_pre, v_norm_mix_post, v_norm_ffn_pre, v_norm_ffn_post, v_w_in, v_b_gate, v_rwkv_shift_mu, v_rwkv_w0, v_rwkv_w2, v_rwkv_a0, v_rwkv_a2, v_rwkv_g2, v_rwkv_k_k, v_rwkv_k_a, v_rwkv_r_k, v_rwkv_lnx_w, v_rwkv_lnx_b, v_s5_a_re, v_s5_a_im, v_s5_b_re, v_s5_b_im, v_s5_c_re, v_s5_c_im, v_s5_d, v_s5_log_step, v_s5_w_glu, v_s5_b_glu, v_w_branch_rwkv, v_w_branch_s5, v_w_out, v_ffn_w_up, v_ffn_conv_w, v_ffn_conv_b, v_ffn_w_down):
    A = dict(locals())
    me = 2 * lax.axis_index("x") + lax.axis_index("y")
    blk = lambda n: A[n][0]

    mine = {n: (blk(n).T if n == 'w_in' else blk(n)).astype(bf16) for n in BIG}
    mine.update({n: blk(n) for n in TINY})
    mine['ffn_conv_w'] = jnp.pad(blk('ffn_conv_w'), ((0, 5), (0, 0)))
    late = ['ffn_w_up', 'ffn_w_down']
    W = _gather_weights({n: blkv for n, blkv in mine.items() if n not in late})
    W.update(_gather_pair(W, [n for n in BIG if n not in late], "gather_weights_pair"))
    S = {n: A[n].reshape(1, -1) for n in SMALL}

    loss, dx, G, chip_e, slots_e = _forward_backward(x[0], loss_target[0], W, S, {n: mine[n] for n in late})

    tiny_shapes = [G[n].shape for n in TINY]
    small_buf = _pack_rows([G[n] for n in SMALL] + [G[n] for n in TINY] + [loss], SMALL_ROWS)
    recv, small_recv = _grads_to_sibling(G, ['w_in'], "grads_to_sibling", small_buf)
    chip_l, small_sum = _pair_add(G, recv, ['w_in'], "grads_pair_sum", small_buf, small_recv)
    slots_l, small4 = _grads_chip_exchange(chip_l, ['w_in'], small_sum)
    half, half['small'] = _sum_slots({**slots_e, **slots_l}, {**chip_e, **chip_l}, small4, small_sum)
    other = _halves_to_sibling(half)
    pc = lax.axis_index("c")
    small_tot = _join_halves(half['small'], other['small'], pc)
    grad = {n: _join_halves(half[n], other[n], pc) for n in BIG}
    grad['w_in'] = grad['w_in'].T
    vals = _unpack_rows(small_tot, [A[n].shape for n in SMALL] + tiny_shapes + [(1, PACK_W)])
    grad.update(zip(SMALL, vals))
    for n, full in zip(TINY, vals[len(SMALL):]):
        cs = A[n].shape[2]
        grad[n] = lax.dynamic_slice_in_dim(full, me * cs, cs, axis=1)
    loss_out = vals[-1][0, 0]

    packed = SMALL + TINY
    groups = [(blk(n), grad[n], blk('m_' + n), blk('v_' + n)) for n in BIG]
    groups.append(tuple(_pack_rows([src(n) for n in packed], ADAM_ROWS)
                        for src in (lambda n: A[n], lambda n: grad[n], lambda n: A['m_' + n], lambda n: A['v_' + n])))
    res = _adamw(groups)
    outs = [dict(), dict(), dict()]
    for n, r3 in zip(BIG, res[:-1]):
        for d, val in zip(outs, r3):
            d[n] = val
    for d, buf in zip(outs, res[-1]):
        d.update(zip(packed, _unpack_rows(buf, [A[n].shape for n in packed])))
    full = lambda d: [d[n].reshape(A[n].shape) for n in WEIGHTS]
    return (loss_out, dx[None], *full(grad), *full(outs[0]), *full(outs[1]), *full(outs[2]))
```

```python
import functools

import jax
import jax.numpy as jnp
from jax import lax
from jax.experimental import pallas as pl
from jax.experimental.pallas import tpu as pltpu

f32, bf16 = jnp.float32, jnp.bfloat16
MESH = pl.DeviceIdType.MESH

D_MODEL = 1024
RWKV_W = 512
HEADS, HEAD = 8, 64
N_RWKV = 1792
S5_W = 512
S5_G, S5_P, S5_C = 32, 64, 16
S5_N = S5_G * S5_P
D_FF = 2816
NORM_EPS = 1e-6
LNX_EPS = 64e-5
ADAM_LR, ADAM_B1, ADAM_B2, ADAM_EPS, ADAM_WD, ADAM_STEP = 0.001, 0.9, 0.999, 1e-08, 0.01, 10

VMEM_LIMIT = 48 * 1024 * 1024
PACK_W = 1024
WKV_C = 64
WKV_SUB = 2
WKV_ROWS = WKV_C * WKV_SUB
RESIDENT_BUDGET = 40 * 1024 * 1024
S5_T = 256

WEIGHTS = ['norm_mix_pre', 'norm_mix_post', 'norm_ffn_pre', 'norm_ffn_post', 'w_in', 'b_gate', 'rwkv_shift_mu',
           'rwkv_w0', 'rwkv_w2', 'rwkv_a0', 'rwkv_a2', 'rwkv_g2', 'rwkv_k_k', 'rwkv_k_a', 'rwkv_r_k', 'rwkv_lnx_w',
           'rwkv_lnx_b', 's5_a_re', 's5_a_im', 's5_b_re', 's5_b_im', 's5_c_re', 's5_c_im', 's5_d', 's5_log_step',
           's5_w_glu', 's5_b_glu', 'w_branch_rwkv', 'w_branch_s5', 'w_out', 'ffn_w_up', 'ffn_conv_w', 'ffn_conv_b',
           'ffn_w_down']


def _ceil_to(n, m):
    return -(-n // m) * m


def _mesh_pos():
    return lax.axis_index("x"), lax.axis_index("y"), lax.axis_index("c")


def _pick(d, cap=4096):
    for c in (1024, 1408, 2176, 896, 512, 256, 128):
        if c <= cap and d % c == 0:
            return c
    raise ValueError(d)


def _mm_resident(a, w, mode, name, M, N, K, out_dtype):
    budget = RESIDENT_BUDGET - 2 * K * N
    tm = next(t for t in (512, 256, 128) if 2 * t * (K * a.dtype.itemsize + 4 * N) <= budget)
    dims = _DIMS[mode]

    def body(a_ref, w_ref, o_ref):
        o_ref[...] = lax.dot_general(a_ref[...].astype(bf16), w_ref[...], (dims, ((), ())),
                                     preferred_element_type=f32).astype(o_ref.dtype)

    return pl.pallas_call(
        body, name=name, grid=(M // tm,),
        in_specs=[pl.BlockSpec((tm, K), lambda i: (i, 0)),
                  pl.BlockSpec(w.shape, lambda i: (0, 0), pipeline_mode=pl.Buffered(1))],
        out_specs=pl.BlockSpec((tm, N), lambda i: (i, 0)), out_shape=jax.ShapeDtypeStruct((M, N), out_dtype),
        compiler_params=pltpu.CompilerParams(dimension_semantics=("parallel",), vmem_limit_bytes=VMEM_LIMIT),
    )(a, w)


def _mm(a, b, mode, name, out_dtype=f32):
    if mode == 'tn':
        (K, M), (K2, N) = a.shape, b.shape
    elif mode == 'nt':
        (M, K), (N, K2) = a.shape, b.shape
    else:
        (M, K), (K2, N) = a.shape, b.shape
    assert K == K2, (name, a.shape, b.shape)
    if mode != 'tn' and b.dtype == bf16:
        return _mm_resident(a, b, mode, name, M, N, K, out_dtype)
    if mode == 'tn':
        tm = _pick(M, 2176)
        tn = _pick(N, 512 if tm > 1408 else (1024 if tm > 1024 else 1408))
        tk = _pick(K, 1024 if a.dtype == bf16 and b.dtype == bf16 else 512)
    else:
        tm, tn, tk = _pick(M, 512), _pick(N), _pick(K)
    nk = K // tk
    dims = {'nn': ((1,), (0,)), 'nt': ((1,), (1,)), 'tn': ((0,), (0,))}[mode]

    def body(a_ref, b_ref, o_ref, acc_ref):
        k = pl.program_id(2)

        @pl.when(k == 0)
        def _():
            acc_ref[...] = jnp.zeros_like(acc_ref)

        acc_ref[...] += lax.dot_general(a_ref[...].astype(bf16), b_ref[...].astype(bf16), (dims, ((), ())),
                                        preferred_element_type=f32)

        @pl.when(k == nk - 1)
        def _():
            o_ref[...] = acc_ref[...].astype(o_ref.dtype)

    a_spec = pl.BlockSpec((tk, tm), lambda i, j, k: (k, i)) if mode == 'tn' else pl.BlockSpec((tm, tk), lambda i, j, k: (i, k))
    b_spec = pl.BlockSpec((tn, tk), lambda i, j, k: (j, k)) if mode == 'nt' else pl.BlockSpec((tk, tn), lambda i, j, k: (k, j))
    return pl.pallas_call(
        body, name=name, grid=(M // tm, N // tn, nk),
        in_specs=[a_spec, b_spec], out_specs=pl.BlockSpec((tm, tn), lambda i, j, k: (i, j)),
        out_shape=jax.ShapeDtypeStruct((M, N), out_dtype),
        scratch_shapes=[pltpu.VMEM((tm, tn), f32)],
        compiler_params=pltpu.CompilerParams(dimension_semantics=("parallel", "parallel", "arbitrary"),
                                             vmem_limit_bytes=VMEM_LIMIT),
    )(a, b)


def _rowcall(name, fn, L, tm, rows, consts=(), out_rows=(), out_accs=(), prev=(), nxt=()):
    nsteps = L // tm
    nb8 = tm // 8
    last8 = L // 8 - 1
    n_r, n_p, n_x, n_c, n_or = len(rows), len(prev), len(nxt), len(consts), len(out_rows)

    def body(*refs):
        i = pl.program_id(0)
        vals = [r[...] for r in refs[:n_r + n_p + n_x + n_c]]
        R, P = vals[:n_r], vals[n_r:n_r + n_p]
        X, C = vals[n_r + n_p:n_r + n_p + n_x], vals[n_r + n_p + n_x:]
        o_refs = refs[n_r + n_p + n_x + n_c:]
        outs_r, outs_a = fn(i, nsteps, R, P, X, C)
        for ref, v in zip(o_refs[:n_or], outs_r, strict=True):
            ref[...] = v.astype(ref.dtype)
        if out_accs:
            @pl.when(i == 0)
            def _():
                for ref in o_refs[n_or:]:
                    ref[...] = jnp.zeros_like(ref)

            for ref, v in zip(o_refs[n_or:], outs_a, strict=True):
                ref[...] += v

    def const_spec(c):
        nd = c.ndim
        return pl.BlockSpec(c.shape, lambda i: (0,) * nd)

    in_specs = ([pl.BlockSpec((tm, a.shape[1]), lambda i: (i, 0)) for a in rows]
                + [pl.BlockSpec((8, rows[j].shape[1]), lambda i: (jnp.maximum(i * nb8 - 1, 0), 0)) for j in prev]
                + [pl.BlockSpec((8, rows[j].shape[1]), lambda i: (jnp.minimum((i + 1) * nb8, last8), 0)) for j in nxt]
                + [const_spec(c) for c in consts])
    out_specs = ([pl.BlockSpec((tm, c), lambda i: (i, 0)) for c, _ in out_rows]
                 + [pl.BlockSpec(s, lambda i: (0, 0)) for s in out_accs])
    out_shape = ([jax.ShapeDtypeStruct((L, c), dt) for c, dt in out_rows]
                 + [jax.ShapeDtypeStruct(s, f32) for s in out_accs])
    args = list(rows) + [rows[j] for j in prev] + [rows[j] for j in nxt] + list(consts)
    return pl.pallas_call(
        body, name=name, grid=(nsteps,), in_specs=in_specs, out_specs=out_specs, out_shape=out_shape,
        compiler_params=pltpu.CompilerParams(dimension_semantics=("arbitrary",), vmem_limit_bytes=VMEM_LIMIT),
    )(*args)


def _shift_down(x, prev8, i, k):
    rolled = pltpu.roll(x, k, axis=0)
    pfix = jnp.where(i > 0, pltpu.roll(prev8, k, axis=0), 0.0)
    row8 = lax.broadcasted_iota(jnp.int32, pfix.shape, 0)
    top = jnp.where(row8 < k, pfix, rolled[:8])
    return top if x.shape[0] == 8 else jnp.concatenate([top, rolled[8:]], axis=0)


def _shift_up(x, next8, i, nsteps, k):
    tm = x.shape[0]
    rolled = pltpu.roll(x, tm - k, axis=0)
    nfix = jnp.where(i < nsteps - 1, pltpu.roll(next8, 8 - k, axis=0), 0.0)
    row8 = lax.broadcasted_iota(jnp.int32, nfix.shape, 0)
    bot = jnp.where(row8 >= 8 - k, nfix, rolled[tm - 8:])
    return jnp.concatenate([rolled[:tm - 8], bot], axis=0)


def _sum0(x):
    return jnp.sum(x, axis=0, keepdims=True)


def _rms(x, g):
    return x * lax.rsqrt(jnp.mean(x * x, axis=-1, keepdims=True) + NORM_EPS) * g


def _softplus(x):
    return jnp.maximum(x, 0.0) + jnp.log(1.0 + jnp.exp(-jnp.abs(x)))


def _gelu(x):
    return 0.5 * x * (1.0 + jnp.tanh(0.7978845608028654 * (x + 0.044715 * x * x * x)))


def _dot32(a, b):
    return jnp.dot(a, b, preferred_element_type=f32, precision=lax.Precision.HIGHEST)


def _seg_raw(x, E):
    hi = x.astype(bf16)
    r1 = x - hi.astype(f32)
    mid = r1.astype(bf16)
    lo = (r1 - mid.astype(f32)).astype(bf16)
    Eb = E.astype(bf16)
    dot = lambda t: jnp.dot(t, Eb, preferred_element_type=f32)
    return (dot(lo) + dot(mid)) + dot(hi)


@jax.custom_vjp
def _seg(x, E):
    return _seg_raw(x, E)


_seg.defvjp(lambda x, E: (_seg_raw(x, E), E), lambda E, g: (_seg_raw(g, E), jnp.zeros_like(E)))


def _prep(q, w0, a0, k_k, k_a, w2p, a2p, g2, E):
    r, k, v = q[:, 0:512], q[:, 512:1024], q[:, 1024:1536]
    wa, gd = q[:, 1536:1664], q[:, 1664:1792]
    wlog = -_softplus(-(w0 + _bdot(jnp.tanh(wa), w2p, 'nn'))) - 0.5
    lw = -jnp.exp(wlog)
    a = jax.nn.sigmoid(a0 + _bdot(wa, a2p, 'nn'))
    g = _bdot(jax.nn.sigmoid(gd), g2, 'nn')
    kk = k * k_k
    kkn = kk / jnp.maximum(jnp.sqrt(_seg(kk * kk, E)), 1e-12)
    k2 = k * (1.0 + (a - 1.0) * k_a)
    return r, lw, k2, v, -kkn, kkn * a, g


def _rwkv_out(y, r, k2, v, g, lnx_w, lnx_b, r_k, E):
    mean = _seg(y, E) * (1.0 / HEAD)
    yc = y - mean
    var = _seg(yc * yc, E) * (1.0 / HEAD)
    yn = yc * lax.rsqrt(var + LNX_EPS) * lnx_w + lnx_b
    bonus = _seg(r * k2 * r_k, E) * v
    return (yn + bonus) * g


def _s5_mid(ysc, u, d):
    return _gelu(ysc + d * u)


def _s5_glu(yg, z2, b_glu):
    return yg * jax.nn.sigmoid(z2 + b_glu)


def _merge(gp, o_r, o_s, b_gate):
    gates = jax.nn.sigmoid(gp + b_gate)
    return gates[:, :D_MODEL] * o_r + gates[:, D_MODEL:] * o_s


def _act(zc):
    return _gelu(zc[:, :D_FF]) * zc[:, D_FF:]


def _s5_disc(a_re, a_im, ls, b_re, b_im):
    dt = jnp.exp(ls)
    er = jnp.exp(a_re * dt)
    ar, ai = er * jnp.cos(a_im * dt), er * jnp.sin(a_im * dt)
    x, y = ar - 1.0, ai
    den = a_re * a_re + a_im * a_im
    fr, fi = (x * a_re + y * a_im) / den, (y * a_re - x * a_im) / den
    return ar, ai, fr * b_re - fi * b_im, fr * b_im + fi * b_re


_DIMS = {'nn': ((1,), (0,)), 'nt': ((1,), (1,)), 'tn': ((0,), (0,))}


def _raw_bdot(a, b, mode):
    return lax.dot_general(a.astype(bf16), b.astype(bf16), (_DIMS[mode], ((), ())), preferred_element_type=f32)


@functools.partial(jax.custom_vjp, nondiff_argnums=(2,))
def _bdot(a, b, mode):
    return _raw_bdot(a, b, mode)


def _bdot_fwd(a, b, mode):
    return _raw_bdot(a, b, mode), (a, b)


def _bdot_bwd(mode, res, g):
    a, b = res
    if mode == 'nn':
        return _raw_bdot(g, b, 'nt'), _raw_bdot(a, g, 'tn')
    if mode == 'nt':
        return _raw_bdot(g, b, 'nn'), _raw_bdot(g, a, 'tn')
    return _raw_bdot(b, g, 'nt'), _raw_bdot(a, g, 'nn')


_bdot.defvjp(_bdot_fwd, _bdot_bwd)


def _tri_inv_raw(A):
    n = A[0].shape[0]
    eye = (lax.broadcasted_iota(jnp.int32, (n, n), 0) == lax.broadcasted_iota(jnp.int32, (n, n), 1)).astype(f32)
    x = [eye + a for a in A]
    pw, m = A, 1
    while 2 * m < n // 2:
        pw = [_raw_bdot(p, p, 'nn') for p in pw]
        x = [xi + _raw_bdot(xi, p, 'nn') for xi, p in zip(x, pw)]
        m *= 2
    return x


@jax.custom_vjp
def _tri_inv(A):
    return _tri_inv_raw(A)


def _tri_inv_fwd(A):
    x = _tri_inv_raw(A)
    return x, x


def _tri_inv_bwd(x, g):
    return ([_raw_bdot(_raw_bdot(xi, gi, 'tn'), xi, 'nt') for xi, gi in zip(x, g)],)


_tri_inv.defvjp(_tri_inv_fwd, _tri_inv_bwd)


@jax.custom_vjp
def _inv_given(A, X):
    return X


_inv_given.defvjp(lambda A, X: (X, X),
                  lambda x, g: (_tri_inv_bwd(x, g)[0], [jnp.zeros_like(xi) for xi in x]))


def _wkv_chunk(S0, r, lw, k, v, a, b, tri, bd, xinv=None):
    C = r[0].shape[0]
    P = range(len(r))
    lane = lax.broadcasted_iota(jnp.int32, (1, 2 * HEAD), 1)
    m0, m1 = (lane < HEAD).astype(f32), (lane >= HEAD).astype(f32)
    cat = lambda *xs: jnp.concatenate(xs, axis=0)
    stack = lambda x: cat(x * m0, x * m1)
    unstack = lambda x2: m0 * x2[:C] + m1 * x2[C:]
    rid = lax.broadcasted_iota(jnp.int32, (2 * C, 2 * C), 0)
    cid = lax.broadcasted_iota(jnp.int32, (2 * C, 2 * C), 1)
    same = (rid < C) == (cid < C)
    eye2 = (rid == cid).astype(f32)
    tri2 = (same & (rid >= cid)).astype(f32)
    sl2 = tri2 - eye2
    cum = [_dot32(tri, lw[p]) for p in P]
    g = [jnp.exp(cum[p]) for p in P]
    gi = [jnp.exp(-cum[p]) for p in P]
    at = [a[p] * jnp.exp(cum[p] - lw[p]) for p in P]
    rt = [r[p] * g[p] for p in P]
    kb = [k[p] * gi[p] for p in P]
    bb = [b[p] * gi[p] for p in P]
    lhs = [cat(stack(at[p]), stack(rt[p])) for p in P]
    pb = [_bdot(lhs[p], stack(bb[p]), 'nt') for p in P]
    pk = [_bdot(lhs[p], stack(kb[p]), 'nt') for p in P]
    aab = [pb[p][:2 * C] * sl2 for p in P]
    base = [_bdot(cat(at[p], rt[p]), S0[p], 'nt') for p in P]
    t = [_bdot(cat(pk[p][:2 * C] * sl2, pk[p][2 * C:] * tri2), cat(v[p], v[p]), 'nn') for p in P]
    rhs = [cat(base[p][:C], base[p][:C]) + t[p][:2 * C] for p in P]
    x = _tri_inv(aab) if xinv is None else _inv_given(aab, xinv)
    u = [unstack(_bdot(x[p], rhs[p], 'nn')) for p in P]
    w2 = [_bdot(pb[p][2 * C:] * tri2, cat(u[p], u[p]), 'nn') for p in P]
    y = [base[p][C:] + unstack(t[p][2 * C:]) + unstack(w2[p]) for p in P]
    S1 = [g[p][C - 1:C, :] * (S0[p] + bd * _bdot(cat(v[p], u[p]), cat(kb[p], bb[p]), 'tn')) for p in P]
    return y, S1, x


def _pairs(x):
    return [x[:, 2 * HEAD * p:2 * HEAD * (p + 1)] for p in range(HEADS // 2)]


def _wkv_consts():
    tri = jnp.tril(jnp.ones((WKV_C, WKV_C), f32))
    hid = jnp.arange(2 * HEAD) // HEAD
    return tri, (hid[:, None] == hid[None, :]).astype(f32)


def _wkv_step(S0, r, lw, k, v, a, b, tri, bd, xinv=None):
    ys, xs, S = [], [], S0
    for c in range(WKV_SUB):
        sub = lambda t: [x[c * WKV_C:(c + 1) * WKV_C] for x in t]
        y, S, x = _wkv_chunk(S, sub(r), sub(lw), sub(k), sub(v), sub(a), sub(b), tri, bd, None if xinv is None else xinv[c])
        ys.append(y)
        xs.append(x)
    return [jnp.concatenate([y[p] for y in ys], axis=0) for p in range(len(S0))], S, xs


def _wkv7_fwd(r, lw, k, v, a, b):
    L = r.shape[0]
    nc, npair = L // WKV_ROWS, HEADS // 2

    def body(r_ref, lw_ref, k_ref, v_ref, a_ref, b_ref, tri_ref, bd_ref, y_ref, ck_ref, xi_ref, s_ref):
        @pl.when(pl.program_id(0) == 0)
        def _():
            s_ref[...] = jnp.zeros_like(s_ref)

        s0 = [s_ref[p] for p in range(npair)]
        for p in range(npair):
            ck_ref[0, p] = s0[p]
        y, s1, xs = _wkv_step(s0, *(_pairs(x) for x in (r_ref, lw_ref, k_ref, v_ref, a_ref, b_ref)), tri_ref[...], bd_ref[...])
        for p in range(npair):
            y_ref[:, 2 * HEAD * p:2 * HEAD * (p + 1)] = y[p]
            s_ref[p] = s1[p]
            for c in range(WKV_SUB):
                xi_ref[0, c, p] = xs[c][p].astype(xi_ref.dtype)

    row = pl.BlockSpec((WKV_ROWS, RWKV_W), lambda c: (c, 0))
    sspec = pl.BlockSpec((1, npair, 2 * HEAD, 2 * HEAD), lambda c: (c, 0, 0, 0))
    xspec = pl.BlockSpec((1, WKV_SUB, npair, 2 * HEAD, 2 * HEAD), lambda c: (c, 0, 0, 0, 0))
    return pl.pallas_call(
        body, name="wkv7_fwd", grid=(nc,),
        in_specs=[row] * 6 + [pl.BlockSpec((WKV_C, WKV_C), lambda c: (0, 0)), pl.BlockSpec((2 * HEAD, 2 * HEAD), lambda c: (0, 0))],
        out_specs=[row, sspec, xspec],
        out_shape=[jax.ShapeDtypeStruct((L, RWKV_W), f32), jax.ShapeDtypeStruct((nc, npair, 2 * HEAD, 2 * HEAD), f32),
                   jax.ShapeDtypeStruct((nc, WKV_SUB, npair, 2 * HEAD, 2 * HEAD), bf16)],
        scratch_shapes=[pltpu.VMEM((npair, 2 * HEAD, 2 * HEAD), f32)],
        compiler_params=pltpu.CompilerParams(dimension_semantics=("arbitrary",), vmem_limit_bytes=VMEM_LIMIT),
    )(r, lw, k, v, a, b, *_wkv_consts())


def _wkv7_bwd(r, lw, k, v, a, b, ck, xinv, dy):
    L = r.shape[0]
    nc, npair = L // WKV_ROWS, HEADS // 2

    def body(r_ref, lw_ref, k_ref, v_ref, a_ref, b_ref, ck_ref, xi_ref, dy_ref, tri_ref, bd_ref,
             dr_ref, dlw_ref, dk_ref, dv_ref, da_ref, db_ref, ds_ref):
        @pl.when(pl.program_id(0) == 0)
        def _():
            ds_ref[...] = jnp.zeros_like(ds_ref)

        tri, bd = tri_ref[...], bd_ref[...]
        ins = [[ck_ref[0, p] for p in range(npair)]] + [_pairs(x) for x in (r_ref, lw_ref, k_ref, v_ref, a_ref, b_ref)]
        xs = [[xi_ref[0, c, p].astype(f32) for p in range(npair)] for c in range(WKV_SUB)]
        _, vjp = jax.vjp(lambda *t: _wkv_step(*t, tri, bd, xs)[:2], *ins)
        gs = vjp((_pairs(dy_ref), [ds_ref[p] for p in range(npair)]))
        for p in range(npair):
            ds_ref[p] = gs[0][p]
            for ref, gval in zip((dr_ref, dlw_ref, dk_ref, dv_ref, da_ref, db_ref), gs[1:]):
                ref[:, 2 * HEAD * p:2 * HEAD * (p + 1)] = gval[p]

    row = pl.BlockSpec((WKV_ROWS, RWKV_W), lambda c: (nc - 1 - c, 0))
    sspec = pl.BlockSpec((1, npair, 2 * HEAD, 2 * HEAD), lambda c: (nc - 1 - c, 0, 0, 0))
    xspec = pl.BlockSpec((1, WKV_SUB, npair, 2 * HEAD, 2 * HEAD), lambda c: (nc - 1 - c, 0, 0, 0, 0))
    return pl.pallas_call(
        body, name="wkv7_bwd", grid=(nc,),
        in_specs=[row] * 6 + [sspec, xspec, row, pl.BlockSpec((WKV_C, WKV_C), lambda c: (0, 0)),
                              pl.BlockSpec((2 * HEAD, 2 * HEAD), lambda c: (0, 0))],
        out_specs=[row] * 6,
        out_shape=[jax.ShapeDtypeStruct((L, RWKV_W), f32)] * 6,
        scratch_shapes=[pltpu.VMEM((npair, 2 * HEAD, 2 * HEAD), f32)],
        compiler_params=pltpu.CompilerParams(dimension_semantics=("arbitrary",), vmem_limit_bytes=VMEM_LIMIT),
    )(r, lw, k, v, a, b, ck, xinv, dy, *_wkv_consts())


def _cmul(ar, ai, xr, xi):
    return ar * xr - ai * xi, ar * xi + ai * xr


def _scan_init(a_ref, car_ref, pw_ref, reverse):
    car_ref[...] = jnp.zeros_like(car_ref)
    ar = jnp.broadcast_to(a_ref[:, :S5_N], (8, S5_N))
    ai = jnp.broadcast_to(a_ref[:, S5_N:], (8, S5_N))
    if reverse:
        ai = -ai
    row = lax.broadcasted_iota(jnp.int32, (8, S5_N), 0)
    pr, pi = ar, ai
    qr, qi = jnp.zeros((8, S5_N), f32), jnp.zeros((8, S5_N), f32)
    for e in range(1, 9):
        sel = (row == 8 - e) if reverse else (row == e - 1)
        qr, qi = jnp.where(sel, pr, qr), jnp.where(sel, pi, qi)
        if e in (1, 2, 4):
            j = (1, 2, 4).index(e)
            pw_ref[j, :, :S5_N] = pr
            pw_ref[j, :, S5_N:] = pi
        pr, pi = _cmul(pr, pi, ar, ai)
    pw_ref[3, :, :S5_N] = qr
    pw_ref[3, :, S5_N:] = qi


def _scan_tile(x_ref, o_ref, car_ref, pw_ref, reverse):
    ng = x_ref.shape[0] // 8
    row = lax.broadcasted_iota(jnp.int32, (8, S5_N), 0)

    def group(gi, carry):
        g = (ng - 1 - gi) if reverse else gi
        t0 = pl.multiple_of(g * 8, 8)
        xr, xi = x_ref[pl.ds(t0, 8), :S5_N], x_ref[pl.ds(t0, 8), S5_N:]
        for j, d in enumerate((1, 2, 4)):
            if reverse:
                sr = jnp.where(row < 8 - d, pltpu.roll(xr, 8 - d, axis=0), 0.0)
                si = jnp.where(row < 8 - d, pltpu.roll(xi, 8 - d, axis=0), 0.0)
            else:
                sr = jnp.where(row >= d, pltpu.roll(xr, d, axis=0), 0.0)
                si = jnp.where(row >= d, pltpu.roll(xi, d, axis=0), 0.0)
            mr, mi = _cmul(pw_ref[j, :, :S5_N], pw_ref[j, :, S5_N:], sr, si)
            xr, xi = xr + mr, xi + mi
        cr, ci = carry
        mr, mi = _cmul(pw_ref[3, :, :S5_N], pw_ref[3, :, S5_N:], cr, ci)
        xr, xi = xr + mr, xi + mi
        o_ref[pl.ds(t0, 8), :S5_N] = xr
        o_ref[pl.ds(t0, 8), S5_N:] = xi
        e = 0 if reverse else 7
        return (jnp.broadcast_to(xr[e:e + 1, :], (8, S5_N)), jnp.broadcast_to(xi[e:e + 1, :], (8, S5_N)))

    cr, ci = lax.fori_loop(0, ng, group, (car_ref[:, :S5_N], car_ref[:, S5_N:]))
    car_ref[:, :S5_N] = cr
    car_ref[:, S5_N:] = ci


_CB, _SB = 128, 512


def _cblk(k):
    return slice(_CB * k, _CB * (k + 1))


def _sblk(j):
    return slice(_SB * j, _SB * (j + 1))


def _s5_fwd(u, bmat, cmat, abar, late):
    L = u.shape[0]
    nt = L // S5_T
    names = list(late)
    nh = len(names)

    def body(u_ref, b_ref, c_ref, a_ref, *rest):
        h_in, (st_ref, y_ref), h_out = rest[:nh], rest[nh:nh + 2], rest[nh + 2:2 * nh + 2]
        bu_ref, car_ref, pw_ref, ssem, rsem, lsem = rest[2 * nh + 2:]
        i = pl.program_id(0)

        def copies():
            px, py, pc = _mesh_pos()
            me = 2 * px + py
            out = []
            for a, nm in enumerate(names):
                hr = late[nm].shape[0] // 2
                src, dst = h_in[a].at[pl.ds(pl.multiple_of(pc * hr, 16), hr), :], _slab(h_out[a], nm, me, pc)
                out.append(pltpu.make_async_copy(src, dst, lsem.at[a]))
                out += [pltpu.make_async_remote_copy(src, dst, ssem.at[3 * a + k], rsem.at[3 * a + k],
                                                     device_id=(qx, qy, pc), device_id_type=MESH)
                        for k, (qx, qy) in enumerate(_chip_peers(px, py))]
            return out

        @pl.when(i == 0)
        def _():
            _scan_init(a_ref, car_ref, pw_ref, False)
            for cp in copies():
                cp.start()

        for j in range(8):
            bu_ref[:, _sblk(j)] = _raw_bdot(u_ref[:, _cblk(j % 4)], b_ref[j], 'nn')
        _scan_tile(bu_ref, st_ref, car_ref, pw_ref, False)
        for k in range(4):
            y_ref[:, _cblk(k)] = (_raw_bdot(st_ref[:, _sblk(k)], c_ref[k], 'nn')
                                  + _raw_bdot(st_ref[:, _sblk(4 + k)], c_ref[4 + k], 'nn'))

        @pl.when(i == nt - 1)
        def _():
            for cp in copies():
                cp.wait()

    whole = lambda shape: pl.BlockSpec(shape, lambda i: (0,) * len(shape))
    outs = pl.pallas_call(
        body, name="s5_fwd", grid=(nt,),
        in_specs=[pl.BlockSpec((S5_T, S5_W), lambda i: (i, 0)), whole(bmat.shape), whole(cmat.shape), whole(abar.shape)]
        + [ANY] * nh,
        out_specs=[pl.BlockSpec((S5_T, 2 * S5_N), lambda i: (i, 0)), pl.BlockSpec((S5_T, S5_W), lambda i: (i, 0))] + [ANY] * nh,
        out_shape=[jax.ShapeDtypeStruct((L, 2 * S5_N), f32), jax.ShapeDtypeStruct((L, S5_W), f32)]
        + [jax.ShapeDtypeStruct(GATHER[nm][0], late[nm].dtype) for nm in names],
        scratch_shapes=[pltpu.VMEM((S5_T, 2 * S5_N), f32), pltpu.VMEM((8, 2 * S5_N), f32), pltpu.VMEM((4, 8, 2 * S5_N), f32),
                        pltpu.SemaphoreType.DMA((3 * nh,)), pltpu.SemaphoreType.DMA((3 * nh,)), pltpu.SemaphoreType.DMA((nh,))],
        compiler_params=pltpu.CompilerParams(dimension_semantics=("arbitrary",), vmem_limit_bytes=VMEM_LIMIT),
    )(u, bmat, cmat, abar, *[late[nm] for nm in names])
    return outs[0], outs[1], dict(zip(names, outs[2:]))


def _s5_bwd(dy, st, u, du_direct, bmat, cmat, abar, chip_sum):
    L = u.shape[0]
    nt = L // S5_T
    nb8 = S5_T // 8
    names = list(chip_sum)
    nh = len(names)

    def body(dy_ref, st_ref, sp_ref, u_ref, dud_ref, b_ref, c_ref, a_ref, *rest):
        x_in, (du_ref, db_ref, dc_ref, da_ref), x_out = rest[:nh], rest[nh:nh + 4], rest[nh + 4:2 * nh + 4]
        lam_ref, car_ref, pw_ref, ssem, rsem = rest[2 * nh + 4:]
        i = pl.program_id(0)

        @pl.when(i == 0)
        def _():
            _scan_init(a_ref, car_ref, pw_ref, True)
            db_ref[...] = jnp.zeros_like(db_ref)
            dc_ref[...] = jnp.zeros_like(dc_ref)
            da_ref[...] = jnp.zeros_like(da_ref)
            for cp in _exchange_copies(x_in, x_out, ssem, rsem):
                cp.start()

        for j in range(8):
            lam_ref[:, _sblk(j)] = _raw_bdot(dy_ref[:, _cblk(j % 4)], c_ref[j], 'nt')
        _scan_tile(lam_ref, lam_ref, car_ref, pw_ref, True)
        for k in range(4):
            du_ref[:, _cblk(k)] = (dud_ref[:, _cblk(k)] + _raw_bdot(lam_ref[:, _sblk(k)], b_ref[k], 'nt')
                                   + _raw_bdot(lam_ref[:, _sblk(4 + k)], b_ref[4 + k], 'nt')
                                   ).astype(du_ref.dtype)
            sr = _shift_down(st_ref[:, _sblk(k)], sp_ref[:, _sblk(k)], nt - 1 - i, 1)
            si = _shift_down(st_ref[:, _sblk(4 + k)], sp_ref[:, _sblk(4 + k)], nt - 1 - i, 1)
            lr, li = lam_ref[:, _sblk(k)], lam_ref[:, _sblk(4 + k)]
            da_ref[:, _sblk(k)] += _sum0(lr * sr + li * si)
            da_ref[:, _sblk(4 + k)] += _sum0(li * sr - lr * si)
        for j in range(8):
            db_ref[j] += _raw_bdot(u_ref[:, _cblk(j % 4)], lam_ref[:, _sblk(j)], 'tn')
            dc_ref[j] += _raw_bdot(st_ref[:, _sblk(j)], dy_ref[:, _cblk(j % 4)], 'tn')

        @pl.when(i == nt - 1)
        def _():
            for cp in _exchange_copies(x_in, x_out, ssem, rsem):
                cp.wait()

    whole = lambda shape: pl.BlockSpec(shape, lambda i: (0,) * len(shape))
    rev = lambda i: (nt - 1 - i, 0)
    outs = pl.pallas_call(
        body, name="s5_bwd", grid=(nt,),
        in_specs=[pl.BlockSpec((S5_T, S5_W), rev), pl.BlockSpec((S5_T, 2 * S5_N), rev),
                  pl.BlockSpec((8, 2 * S5_N), lambda i: (jnp.maximum((nt - 1 - i) * nb8 - 1, 0), 0)),
                  pl.BlockSpec((S5_T, S5_W), rev), pl.BlockSpec((S5_T, S5_W), rev), whole(bmat.shape), whole(cmat.shape),
                  whole(abar.shape)] + [ANY] * nh,
        out_specs=[pl.BlockSpec((S5_T, S5_W), rev), whole((8, _CB, _SB)), whole((8, _SB, _CB)), whole((1, 2 * S5_N))]
        + [ANY] * nh,
        out_shape=[jax.ShapeDtypeStruct((L, S5_W), bf16), jax.ShapeDtypeStruct((8, _CB, _SB), f32),
                   jax.ShapeDtypeStruct((8, _SB, _CB), f32), jax.ShapeDtypeStruct((1, 2 * S5_N), f32)]
        + [jax.ShapeDtypeStruct(chip_sum[nm].shape, chip_sum[nm].dtype) for nm in names],
        scratch_shapes=[pltpu.VMEM((S5_T, 2 * S5_N), f32), pltpu.VMEM((8, 2 * S5_N), f32), pltpu.VMEM((4, 8, 2 * S5_N), f32),
                        pltpu.SemaphoreType.DMA((3 * nh,)), pltpu.SemaphoreType.DMA((3 * nh,))],
        compiler_params=pltpu.CompilerParams(dimension_semantics=("arbitrary",), vmem_limit_bytes=VMEM_LIMIT),
    )(dy, st, st, u, du_direct, bmat, cmat, abar, *[chip_sum[nm] for nm in names])
    return outs[0], outs[1], outs[2], outs[3], dict(zip(names, outs[4:]))


def _s5_disc_fwd(a_re, a_im, ls, b_re, b_im):
    def body(a_re_ref, a_im_ref, ls_ref, b_re_ref, b_im_ref, ar_ref, ai_ref, br_ref, bi_ref):
        outs = _s5_disc(a_re_ref[...], a_im_ref[...], ls_ref[...], b_re_ref[...], b_im_ref[...])
        for ref, v in zip((ar_ref, ai_ref, br_ref, bi_ref), outs):
            ref[...] = v

    c1, c16 = jax.ShapeDtypeStruct((S5_N, 1), f32), jax.ShapeDtypeStruct((S5_N, S5_C), f32)
    return pl.pallas_call(body, name="s5_disc", out_shape=[c1, c1, c16, c16])(a_re, a_im, ls, b_re, b_im)


def _s5_disc_bwd(a_re, a_im, ls, b_re, b_im, d_ar, d_ai, d_br, d_bi, seg):
    def body(a_re_ref, a_im_ref, ls_ref, b_re_ref, b_im_ref, g1, g2, g3, g4, seg_ref, o1, o2, o3, o4, o5):
        _, vjp = jax.vjp(_s5_disc, a_re_ref[...], a_im_ref[...], ls_ref[...], b_re_ref[...], b_im_ref[...])
        da_re, da_im, dls, db_re, db_im = vjp((g1[...], g2[...], g3[...], g4[...]))
        o1[...] = da_re
        o2[...] = da_im
        o3[...] = _dot32(seg_ref[...], dls)
        o4[...] = db_re
        o5[...] = db_im

    c1, c16 = jax.ShapeDtypeStruct((S5_N, 1), f32), jax.ShapeDtypeStruct((S5_N, S5_C), f32)
    return pl.pallas_call(body, name="s5_disc_bwd", out_shape=[c1, c1, jax.ShapeDtypeStruct((S5_G, 1), f32), c16, c16])(
        a_re, a_im, ls, b_re, b_im, d_ar, d_ai, d_br, d_bi, seg)


ANY = pl.BlockSpec(memory_space=pl.ANY)

GATHER = {'w_in': ((4352, 1024), 0), 'ffn_w_up': ((1024, 5632), 1), 'w_branch_rwkv': ((512, 1024), 1),
          'w_branch_s5': ((512, 1024), 1), 'w_out': ((1024, 1024), 0), 's5_w_glu': ((512, 512), 0),
          'ffn_w_down': ((2816, 1024), 0), 'rwkv_w2': ((64, 512), 1), 'rwkv_a2': ((64, 512), 1),
          'rwkv_g2': ((128, 512), 1), 'ffn_conv_w': ((8, 5632), 1)}
BIG = ['w_in', 'ffn_w_up', 'w_branch_rwkv', 'w_branch_s5', 'w_out', 's5_w_glu', 'ffn_w_down']
TINY = ['rwkv_w2', 'rwkv_a2', 'rwkv_g2', 'ffn_conv_w']
SMALL = [n for n in WEIGHTS if n not in GATHER]
SMALL_ROWS = 320
ADAM_ROWS = 256


def _mo(v, m):
    return v if isinstance(v, int) else pl.multiple_of(v, m)


def _slab(ref, name, j, h=None):
    (R, Cn), axis = GATHER[name]
    if axis == 0:
        rs = R // 4
        if h is None:
            return ref.at[pl.ds(_mo(j * rs, 16), rs), :]
        return ref.at[pl.ds(_mo(j * rs + h * (rs // 2), 8), rs // 2), :]
    cols = pl.ds(_mo(j * (Cn // 4), 128), Cn // 4)
    if h is None:
        return ref.at[:, cols]
    return ref.at[pl.ds(_mo(h * (R // 2), 8), R // 2), cols]


def _half_shape(name):
    (R, Cn), axis = GATHER[name]
    return (R // 8, Cn) if axis == 0 else (R // 2, Cn // 4)


def _chip_peers(px, py):
    return [((1 - px) if (k >> 1) else px, (1 - py) if (k & 1) else py) for k in (1, 2, 3)]


def _run_copies(copies):
    for cp in copies:
        cp.start()
    for cp in copies:
        cp.wait()


def _gather_weights(blocks):
    names = list(blocks)
    n = len(names)

    def body(*refs):
        ins, outs = refs[:n], refs[n:2 * n]
        ssem, rsem, lsem = refs[2 * n:]
        px, py, pc = _mesh_pos()
        me = 2 * px + py
        copies = []
        for i, nm in enumerate(names):
            if nm in BIG:
                hr = blocks[nm].shape[0] // 2
                src, dst = ins[i].at[pl.ds(pl.multiple_of(pc * hr, 16), hr), :], _slab(outs[i], nm, me, pc)
            else:
                src, dst = ins[i], _slab(outs[i], nm, me)
            copies.append(pltpu.make_async_copy(src, dst, lsem.at[i]))
            for k, (qx, qy) in enumerate(_chip_peers(px, py)):
                copies.append(pltpu.make_async_remote_copy(src, dst, ssem.at[3 * i + k], rsem.at[3 * i + k],
                                                           device_id=(qx, qy, pc), device_id_type=MESH))
        _run_copies(copies)

    outs = pl.pallas_call(
        body, name="gather_weights", in_specs=[ANY] * n, out_specs=[ANY] * n,
        out_shape=[jax.ShapeDtypeStruct(GATHER[nm][0], blocks[nm].dtype) for nm in names],
        scratch_shapes=[pltpu.SemaphoreType.DMA((3 * n,)), pltpu.SemaphoreType.DMA((3 * n,)), pltpu.SemaphoreType.DMA((n,))],
    )(*[blocks[nm] for nm in names])
    return dict(zip(names, outs))


def _gather_pair(full, names, call_name):
    n = len(names)

    def body(*refs):
        ins, outs = refs[:n], refs[n:2 * n]
        ssem, rsem = refs[2 * n:]
        px, py, pc = _mesh_pos()
        copies = []
        for i, nm in enumerate(names):
            for j in range(4):
                copies.append(pltpu.make_async_remote_copy(_slab(ins[i], nm, j, pc), _slab(outs[i], nm, j, pc),
                                                           ssem.at[4 * i + j], rsem.at[4 * i + j],
                                                           device_id=(px, py, 1 - pc), device_id_type=MESH))
        _run_copies(copies)

    outs = pl.pallas_call(
        body, name=call_name, in_specs=[ANY] * n, out_specs=[ANY] * n,
        out_shape=[jax.ShapeDtypeStruct(full[nm].shape, full[nm].dtype) for nm in names],
        input_output_aliases={i: i for i in range(n)},
        scratch_shapes=[pltpu.SemaphoreType.DMA((4 * n,)), pltpu.SemaphoreType.DMA((4 * n,))],
    )(*[full[nm] for nm in names])
    return dict(zip(names, outs))


def _grads_to_sibling(G, names, call_name, small=None):
    n = len(names)
    ns = 0 if small is None else 1

    def body(*refs):
        g_refs, o_refs = refs[:n + ns], refs[n + ns:2 * (n + ns)]
        ssem, rsem = refs[2 * (n + ns):]
        px, py, pc = _mesh_pos()
        sib = (px, py, 1 - pc)
        copies = []
        for i, nm in enumerate(names):
            for j in range(4):
                copies.append(pltpu.make_async_remote_copy(_slab(g_refs[i], nm, j, 1 - pc), o_refs[i].at[j],
                                                           ssem.at[4 * i + j], rsem.at[4 * i + j],
                                                           device_id=sib, device_id_type=MESH))
        if ns:
            copies.append(pltpu.make_async_remote_copy(g_refs[n], o_refs[n], ssem.at[4 * n], rsem.at[4 * n],
                                                       device_id=sib, device_id_type=MESH))
        _run_copies(copies)

    outs = pl.pallas_call(
        body, name=call_name, in_specs=[ANY] * (n + ns), out_specs=[ANY] * (n + ns),
        out_shape=[jax.ShapeDtypeStruct((4,) + _half_shape(nm), f32) for nm in names]
        + [jax.ShapeDtypeStruct((SMALL_ROWS, PACK_W), f32)] * ns,
        scratch_shapes=[pltpu.SemaphoreType.DMA((4 * n + ns,)), pltpu.SemaphoreType.DMA((4 * n + ns,))],
    )(*[G[nm] for nm in names], *([small] * ns))
    return dict(zip(names, outs[:n])), (outs[n] if ns else None)


def _pair_add(G, recv, names, call_name, small=None, small_recv=None):
    n = len(names)
    ns = 0 if small is None else 1
    cidx = lax.axis_index("c").astype(jnp.int32).reshape(1)

    def body(c_ref, *refs):
        ins, outs = refs[:2 * (n + ns)], refs[2 * (n + ns):]
        for i in range(n):
            outs[i][...] = (ins[i][...] + ins[n + ns + i][...]).astype(bf16)
        if ns:
            outs[n][...] = ins[n][...] + ins[2 * n + 1][...]

    g_specs, r_specs = [], []
    for nm in names:
        hr, hc = _half_shape(nm)
        if GATHER[nm][1] == 0:
            g_specs.append(pl.BlockSpec((hr // 2, hc), lambda j, i, c: ((2 * j + c[0]) * 2 + i, 0)))
        else:
            g_specs.append(pl.BlockSpec((hr // 2, hc), lambda j, i, c: (2 * c[0] + i, j)))
        r_specs.append(pl.BlockSpec((1, hr // 2, hc), lambda j, i, c: (j, i, 0)))
    sm = [pl.BlockSpec((SMALL_ROWS // 8, PACK_W), lambda j, i, c: (2 * j + i, 0))] * ns
    outs = pl.pallas_call(
        body, name=call_name,
        grid_spec=pltpu.PrefetchScalarGridSpec(num_scalar_prefetch=1, grid=(4, 2), in_specs=g_specs + sm + r_specs + sm,
                                               out_specs=r_specs + sm),
        out_shape=[jax.ShapeDtypeStruct((4,) + _half_shape(nm), bf16) for nm in names]
        + [jax.ShapeDtypeStruct((SMALL_ROWS, PACK_W), f32)] * ns,
        compiler_params=pltpu.CompilerParams(vmem_limit_bytes=VMEM_LIMIT),
    )(cidx, *[G[nm] for nm in names], *([small] * ns), *[recv[nm] for nm in names], *([small_recv] * ns))
    return dict(zip(names, outs[:n])), (outs[n] if ns else None)


def _exchange_copies(ins, outs, ssem, rsem):
    px, py, pc = _mesh_pos()
    me = 2 * px + py
    return [pltpu.make_async_remote_copy(ins[i].at[2 * qx + qy], outs[i].at[me], ssem.at[3 * i + k], rsem.at[3 * i + k],
                                         device_id=(qx, qy, pc), device_id_type=MESH)
            for i in range(len(ins)) for k, (qx, qy) in enumerate(_chip_peers(px, py))]


def _grads_chip_exchange(chip_sum, names, small):
    n = len(names)

    def body(*refs):
        ins, outs = refs[:n + 1], refs[n + 1:2 * n + 2]
        ssem, rsem, ssem_s, rsem_s = refs[2 * n + 2:]
        px, py, pc = _mesh_pos()
        me = 2 * px + py
        copies = _exchange_copies(ins[:n], outs[:n], ssem, rsem)
        hs = SMALL_ROWS // 2
        mine = ins[n].at[pl.ds(pl.multiple_of(pc * hs, 8), hs), :]
        copies += [pltpu.make_async_remote_copy(mine, outs[n].at[me], ssem_s.at[k], rsem_s.at[k],
                                                device_id=(qx, qy, pc), device_id_type=MESH)
                   for k, (qx, qy) in enumerate(_chip_peers(px, py))]
        _run_copies(copies)

    outs = pl.pallas_call(
        body, name="grads_chip_exchange", in_specs=[ANY] * (n + 1), out_specs=[ANY] * (n + 1),
        out_shape=[jax.ShapeDtypeStruct(chip_sum[nm].shape, chip_sum[nm].dtype) for nm in names]
        + [jax.ShapeDtypeStruct((4, SMALL_ROWS // 2, PACK_W), f32)],
        scratch_shapes=[pltpu.SemaphoreType.DMA((3 * n,)), pltpu.SemaphoreType.DMA((3 * n,)),
                        pltpu.SemaphoreType.DMA((3,)), pltpu.SemaphoreType.DMA((3,))],
    )(*[chip_sum[nm] for nm in names], small)
    return dict(zip(names, outs[:n])), outs[n]


def _sum_slots(slots, chip_sum, small4, small_own):
    n = len(BIG)
    me = jnp.stack([2 * lax.axis_index("x") + lax.axis_index("y"), lax.axis_index("c")]).astype(jnp.int32)

    def body(me_ref, *refs):
        for i in range(n + 1):
            own = refs[5 * i + 4][...].astype(f32)
            own = own[0] if i < n else own
            term = [jnp.where(me_ref[0] == k, own, refs[5 * i + k][0].astype(f32)) for k in range(4)]
            refs[5 * (n + 1) + i][...] = ((term[0] + term[1]) + term[2]) + term[3]

    redirect = lambda k: (lambda i, m: (jnp.where(m[0] == k, (k + 1) % 4, k), i, 0))
    in_specs, args, specs_out, shapes = [], [], [], []
    for nm in BIG:
        hr, hc = _half_shape(nm)
        in_specs += [pl.BlockSpec((1, hr // 2, hc), redirect(k)) for k in range(4)]
        in_specs.append(pl.BlockSpec((1, hr // 2, hc), lambda i, m: (m[0], i, 0)))
        args += [slots[nm]] * 4 + [chip_sum[nm]]
        specs_out.append(pl.BlockSpec((hr // 2, hc), lambda i, m: (i, 0)))
        shapes.append(jax.ShapeDtypeStruct((hr, hc), f32))
    in_specs += [pl.BlockSpec((1, SMALL_ROWS // 4, PACK_W), redirect(k)) for k in range(4)]
    in_specs.append(pl.BlockSpec((SMALL_ROWS // 4, PACK_W), lambda i, m: (2 * m[1] + i, 0)))
    args += [small4] * 4 + [small_own]
    specs_out.append(pl.BlockSpec((SMALL_ROWS // 4, PACK_W), lambda i, m: (i, 0)))
    shapes.append(jax.ShapeDtypeStruct((SMALL_ROWS // 2, PACK_W), f32))
    outs = pl.pallas_call(
        body, name="grads_chip_sum",
        grid_spec=pltpu.PrefetchScalarGridSpec(num_scalar_prefetch=1, grid=(2,), in_specs=in_specs, out_specs=specs_out),
        out_shape=shapes, compiler_params=pltpu.CompilerParams(vmem_limit_bytes=VMEM_LIMIT),
    )(me, *args)
    return dict(zip(BIG, outs[:n])), outs[n]


def _halves_to_sibling(half):
    names = list(half)
    n = len(names)

    def body(*refs):
        ins, outs = refs[:n], refs[n:2 * n]
        ssem, rsem = refs[2 * n:]
        px, py, pc = _mesh_pos()
        _run_copies([pltpu.make_async_remote_copy(ins[i], outs[i], ssem.at[i], rsem.at[i],
                                                  device_id=(px, py, 1 - pc), device_id_type=MESH) for i in range(n)])

    outs = pl.pallas_call(
        body, name="grads_halves_to_sibling", in_specs=[ANY] * n, out_specs=[ANY] * n,
        out_shape=[jax.ShapeDtypeStruct(half[nm].shape, f32) for nm in names],
        scratch_shapes=[pltpu.SemaphoreType.DMA((n,)), pltpu.SemaphoreType.DMA((n,))],
    )(*[half[nm] for nm in names])
    return dict(zip(names, outs))


def _join_halves(mine, other, pc):
    hr = mine.shape[0]
    return lax.dynamic_slice_in_dim(jnp.concatenate([other, mine, other], axis=0), (1 - pc) * hr, 2 * hr, axis=0)


def _flat_pad(v):
    v = v.reshape(-1)
    return jnp.pad(v, (0, _ceil_to(v.shape[0], PACK_W) - v.shape[0]))


def _pack_rows(parts, rows):
    flat = jnp.concatenate([_flat_pad(p) for p in parts])
    return jnp.pad(flat, (0, rows * PACK_W - flat.shape[0])).reshape(rows, PACK_W)


def _unpack_rows(buf, shapes):
    flat = buf.reshape(-1)
    out, off = [], 0
    for shp in shapes:
        n = 1
        for d in shp:
            n *= d
        out.append(flat[off:off + n].reshape(shp))
        off += _ceil_to(n, PACK_W)
    return out


def _adamw_math(w_, g_, m_, v_):
    m2 = ADAM_B1 * m_ + (1.0 - ADAM_B1) * g_
    v2 = ADAM_B2 * v_ + (1.0 - ADAM_B2) * (g_ * g_)
    m_hat = m2 / (1.0 - ADAM_B1 ** ADAM_STEP)
    v_hat = v2 / (1.0 - ADAM_B2 ** ADAM_STEP)
    return -ADAM_LR * (m_hat / (jnp.sqrt(v_hat) + ADAM_EPS) + ADAM_WD * w_), m2, v2


def _adamw(groups):
    ng = len(groups)

    def body(*refs):
        ins, outs = refs[:4 * ng], refs[4 * ng:]
        for i in range(ng):
            res = _adamw_math(*(r[...] for r in ins[4 * i:4 * i + 4]))
            for ref, val in zip(outs[3 * i:3 * i + 3], res):
                ref[...] = val

    in_specs, out_specs, out_shape = [], [], []
    for grp in groups:
        R, Cn = grp[0].shape
        spec = pl.BlockSpec((R // 8, Cn), lambda i: (i, 0))
        in_specs += [spec] * 4
        out_specs += [spec] * 3
        out_shape += [jax.ShapeDtypeStruct((R, Cn), f32)] * 3
    outs = pl.pallas_call(
        body, name="adamw", grid=(8,), in_specs=in_specs, out_specs=out_specs, out_shape=out_shape,
        compiler_params=pltpu.CompilerParams(vmem_limit_bytes=VMEM_LIMIT),
    )(*[a for grp in groups for a in grp])
    return [tuple(outs[3 * i:3 * i + 3]) for i in range(ng)]


def _forward_backward(x, tgt, W, S, late):
    L = x.shape[0]
    TM, TMW, TS = 256, 128, 512
    row = lambda c, dt=f32: (c, dt)
    hid = jnp.arange(RWKV_W) // HEAD
    E = (hid[:, None] == hid[None, :]).astype(f32)
    seg = (jnp.arange(S5_N)[None, :] // S5_P == jnp.arange(S5_G)[:, None]).astype(f32)

    w_in_t = W['w_in']
    w_p, w_u, w_g = w_in_t[:N_RWKV], w_in_t[N_RWKV:N_RWKV + S5_W], w_in_t[N_RWKV + S5_W:]
    zpad = jnp.zeros((64, RWKV_W), f32)
    w2p = jnp.concatenate([W['rwkv_w2'], zpad], axis=0)
    a2p = jnp.concatenate([zpad, W['rwkv_a2']], axis=0)
    g2 = W['rwkv_g2']
    prep_consts = [S['rwkv_shift_mu'], S['rwkv_w0'], S['rwkv_a0'], S['rwkv_k_k'], S['rwkv_k_a'], w2p, a2p, g2, E]
    out_consts = [S['rwkv_lnx_w'], S['rwkv_lnx_b'], S['rwkv_r_k'], E]
    cw, cb = W['ffn_conv_w'][:3], S['ffn_conv_b']

    a_re, a_im = S['s5_a_re'].reshape(S5_N, 1), S['s5_a_im'].reshape(S5_N, 1)
    ls = jnp.repeat(S['s5_log_step'].reshape(S5_G, 1), S5_P, axis=0)
    b_re, b_im = S['s5_b_re'].reshape(S5_N, S5_C), S['s5_b_im'].reshape(S5_N, S5_C)
    ar, ai, bbr, bbi = _s5_disc_fwd(a_re, a_im, ls, b_re, b_im)
    abar = jnp.concatenate([ar.reshape(1, S5_N), ai.reshape(1, S5_N)], axis=1)
    eye8 = jnp.eye(8, dtype=f32)

    def blocks_in(bb):
        t = bb.reshape(4, 8, S5_P, S5_C).transpose(0, 1, 3, 2)
        return (t[:, :, :, None, :] * eye8[None, :, None, :, None]).reshape(4, _CB, _SB)

    def blocks_out(cc):
        t = cc.reshape(4, 8, S5_C, S5_P).transpose(0, 1, 3, 2)
        return (t[:, :, :, None, :] * eye8[None, :, None, :, None]).reshape(4, _SB, _CB)

    def undiag_in(blocks):
        t = blocks.reshape(4, 8, S5_C, 8, S5_P)
        t = jnp.sum(t * eye8[None, :, None, :, None], axis=3)
        return t.reshape(S5_G, S5_C, S5_P).transpose(0, 2, 1).reshape(S5_N, S5_C)

    def undiag_out(blocks):
        t = blocks.reshape(4, 8, S5_P, 8, S5_C)
        t = jnp.sum(t * eye8[None, :, None, :, None], axis=3)
        return t.reshape(S5_G, S5_P, S5_C).transpose(0, 2, 1)

    bmat = jnp.concatenate([blocks_in(bbr), blocks_in(bbi)], axis=0).astype(bf16)
    cmat = jnp.concatenate([blocks_out(S['s5_c_re'].reshape(S5_G, S5_C, S5_P)),
                            -blocks_out(S['s5_c_im'].reshape(S5_G, S5_C, S5_P))], axis=0).astype(bf16)

    g1, g2n, g3, g4 = S['norm_mix_pre'], S['norm_mix_post'], S['norm_ffn_pre'], S['norm_ffn_post']
    (h1,) = _rowcall("norm_pre", lambda i, n, R, P, X, C: ((_rms(R[0], C[0]),), ()), L, TS, [x], [g1],
                     out_rows=[row(D_MODEL, bf16)])
    p = _mm(h1, w_p, 'nt', "mm_p")
    u = _mm(h1, w_u, 'nt', "mm_u")
    gp = _mm(h1, w_g, 'nt', "mm_g")
    st, ysc, got = _s5_fwd(u, bmat, cmat, abar, late)
    W = {**W, **_gather_pair(got, list(got), "gather_weights_pair_late")}

    def prep_fn(i, n, R, P, X, C):
        q = R[0] + (_shift_down(R[0], P[0], i, 1) - R[0]) * C[0]
        return _prep(q, *C[1:]), ()

    r, lw, k2, v, an, bv, g = _rowcall("rwkv_prep", prep_fn, L, TS, [p], prep_consts,
                                       out_rows=[row(RWKV_W)] * 7, prev=[0])
    y, ck, xinv = _wkv7_fwd(r, lw, k2, v, an, bv)
    (o_a,) = _rowcall("rwkv_out", lambda i, n, R, P, X, C: ((_rwkv_out(*R, *C),), ()), L, TS, [y, r, k2, v, g],
                      out_consts, out_rows=[row(RWKV_W, bf16)])
    o_r = _mm(o_a, W['w_branch_rwkv'], 'nn', "mm_br")

    (yg,) = _rowcall("s5_mid", lambda i, n, R, P, X, C: ((_s5_mid(*R, *C),), ()), L, TS, [ysc, u], [S['s5_d']],
                     out_rows=[row(S5_W)])
    z2 = _mm(yg, W['s5_w_glu'], 'nn', "mm_glu")
    (o_b,) = _rowcall("s5_glu", lambda i, n, R, P, X, C: ((_s5_glu(*R, *C),), ()), L, TS, [yg, z2], [S['s5_b_glu']],
                      out_rows=[row(S5_W, bf16)])
    o_s = _mm(o_b, W['w_branch_s5'], 'nn', "mm_bs")

    (merged,) = _rowcall("merge", lambda i, n, R, P, X, C: ((_merge(*R, *C),), ()), L, TS, [gp, o_r, o_s],
                         [S['b_gate']], out_rows=[row(D_MODEL, bf16)])
    mixed = _mm(merged, W['w_out'], 'nn', "mm_out")

    def resid_fn(i, n, R, P, X, C):
        x1_ = R[0] + _rms(R[1], C[0])
        return (x1_, _rms(x1_, C[1])), ()

    x1, h2 = _rowcall("resid_norm", resid_fn, L, TS, [x, mixed], [g2n, g3], out_rows=[row(D_MODEL), row(D_MODEL, bf16)])

    z = _mm(h2, W['ffn_w_up'], 'nn', "mm_up")

    def conv(zt, zprev, i, cw_, cb_):
        z2s, z1s = _shift_down(zt, zprev, i, 2), _shift_down(zt, zprev, i, 1)
        return cb_ + cw_[0:1] * z2s + cw_[1:2] * z1s + cw_[2:3] * zt, z2s, z1s

    (act,) = _rowcall("conv_act", lambda i, n, R, P, X, C: ((_act(conv(R[0], P[0], i, C[0], C[1])[0]),), ()), L, TMW,
                      [z], [cw, cb], out_rows=[row(D_FF, bf16)], prev=[0])
    f = _mm(act, W['ffn_w_down'], 'nn', "mm_down")

    def final_fn(i, n, R, P, X, C):
        x1_, f_, t_ = R
        fn_, vjp = jax.vjp(_rms, f_, C[0])
        diff = x1_ + fn_ - t_
        loss = jnp.sum(diff * diff) * (0.5 / D_MODEL)
        dx2_ = diff * (1.0 / D_MODEL)
        df_, dg4_ = vjp(dx2_)
        return (df_, dx2_), (jnp.full((1, PACK_W), loss, f32), dg4_)

    df, dx2, loss, dg4 = _rowcall("loss_head", final_fn, L, TS, [x1, f, tgt], [g4],
                                  out_rows=[row(D_MODEL, bf16), row(D_MODEL)], out_accs=[(1, PACK_W), (1, D_MODEL)])
    G = {'norm_ffn_post': dg4}

    dact = _mm(df, W['ffn_w_down'], 'nt', "mm_down_dx")
    G['ffn_w_down'] = _mm(act, df, 'tn', "mm_down_dw")

    def conv_bwd_fn(i, n, R, P, X, C):
        z_, dact_ = R
        cw_, cb_ = C
        zc, z2s, z1s = conv(z_, P[0], i, cw_, cb_)
        _, vjp = jax.vjp(_act, zc)
        (dzc_,) = vjp(dact_)
        last8 = z_[z_.shape[0] - 8:]
        zcn = cb_ + cw_[0:1] * _shift_down(X[0], last8, 1, 2) + cw_[1:2] * _shift_down(X[0], last8, 1, 1) + cw_[2:3] * X[0]
        _, vjpn = jax.vjp(_act, zcn)
        (dzcn,) = vjpn(X[1])
        dz_ = (cw_[2:3] * dzc_ + cw_[1:2] * _shift_up(dzc_, dzcn, i, n, 1) + cw_[0:1] * _shift_up(dzc_, dzcn, i, n, 2))
        return (dz_,), (_sum0(dzc_), _sum0(dzc_ * z2s), _sum0(dzc_ * z1s), _sum0(dzc_ * z_))

    wide = (1, 2 * D_FF)
    dz, dcb, dcw0, dcw1, dcw2 = _rowcall("conv_act_bwd", conv_bwd_fn, L, TMW, [z, dact], [cw, cb],
                                         out_rows=[row(2 * D_FF, bf16)], out_accs=[wide] * 4, prev=[0], nxt=[0, 1])
    G['ffn_conv_b'] = dcb
    G['ffn_conv_w'] = jnp.concatenate([dcw0, dcw1, dcw2], axis=0)
    dh2 = _mm(dz, W['ffn_w_up'], 'nt', "mm_up_dx")
    G['ffn_w_up'] = _mm(h2, dz, 'tn', "mm_up_dw")

    def norm2_bwd_fn(i, n, R, P, X, C):
        x1_, mixed_, dx2_, dh2_ = R
        _, vjp3 = jax.vjp(_rms, x1_, C[1])
        dx1a, dg3_ = vjp3(dh2_)
        dx1_ = dx2_ + dx1a
        _, vjp2 = jax.vjp(_rms, mixed_, C[0])
        dmixed_, dg2_ = vjp2(dx1_)
        return (dx1_, dmixed_), (dg2_, dg3_)

    dx1, dmixed, dg2n, dg3 = _rowcall("norm_mid_bwd", norm2_bwd_fn, L, TS, [x1, mixed, dx2, dh2], [g2n, g3],
                                      out_rows=[row(D_MODEL), row(D_MODEL, bf16)], out_accs=[(1, D_MODEL)] * 2)
    G['norm_mix_post'], G['norm_ffn_pre'] = dg2n, dg3

    dmerged = _mm(dmixed, W['w_out'], 'nt', "mm_out_dx")
    G['w_out'] = _mm(merged, dmixed, 'tn', "mm_out_dw")

    def merge_bwd_fn(i, n, R, P, X, C):
        _, vjp = jax.vjp(_merge, R[0], R[1], R[2], C[0])
        dgp_, do_r_, do_s_, dbg_ = vjp(R[3])
        return (dgp_, do_r_, do_s_), (dbg_,)

    dgp, do_r, do_s, G['b_gate'] = _rowcall("merge_bwd", merge_bwd_fn, L, TS, [gp, o_r, o_s, dmerged], [S['b_gate']],
                                            out_rows=[row(2 * D_MODEL, bf16), row(D_MODEL, bf16), row(D_MODEL, bf16)],
                                            out_accs=[(1, 2 * D_MODEL)])
    do_a = _mm(do_r, W['w_branch_rwkv'], 'nt', "mm_br_dx")
    G['w_branch_rwkv'] = _mm(o_a, do_r, 'tn', "mm_br_dw")
    do_b = _mm(do_s, W['w_branch_s5'], 'nt', "mm_bs_dx")
    G['w_branch_s5'] = _mm(o_b, do_s, 'tn', "mm_bs_dw")

    def glu_bwd_fn(i, n, R, P, X, C):
        _, vjp = jax.vjp(_s5_glu, R[0], R[1], C[0])
        dyg1_, dz2_, dbg_ = vjp(R[2])
        return (dyg1_, dz2_), (dbg_,)

    dyg1, dz2, G['s5_b_glu'] = _rowcall("s5_glu_bwd", glu_bwd_fn, L, TS, [yg, z2, do_b], [S['s5_b_glu']],
                                        out_rows=[row(S5_W), row(S5_W, bf16)], out_accs=[(1, S5_W)])
    dyg2 = _mm(dz2, W['s5_w_glu'], 'nt', "mm_glu_dx")
    G['s5_w_glu'] = _mm(yg, dz2, 'tn', "mm_glu_dw")

    def mid_bwd_fn(i, n, R, P, X, C):
        _, vjp = jax.vjp(_s5_mid, R[0], R[1], C[0])
        dysc_, du_, dd_ = vjp(R[2] + R[3])
        return (dysc_, du_), (dd_,)

    dysc, du1, G['s5_d'] = _rowcall("s5_mid_bwd", mid_bwd_fn, L, TS, [ysc, u, dyg1, dyg2], [S['s5_d']],
                                    out_rows=[row(S5_W, bf16), row(S5_W)], out_accs=[(1, S5_W)])
    early = [n for n in BIG if n != 'w_in']
    recv_e, _ = _grads_to_sibling(G, early, "grads_to_sibling_early")
    chip_e, _ = _pair_add(G, recv_e, early, "grads_pair_sum_early")
    du, dbmat, dcmat, dabar, slots_e = _s5_bwd(dysc, st, u, du1, bmat, cmat, abar, chip_e)
    da_re, da_im, dls, db_re, db_im = _s5_disc_bwd(
        a_re, a_im, ls, b_re, b_im, dabar[:, :S5_N].reshape(S5_N, 1), dabar[:, S5_N:].reshape(S5_N, 1),
        undiag_in(dbmat[:4]), undiag_in(dbmat[4:]), seg)
    G['s5_a_re'], G['s5_a_im'], G['s5_log_step'] = da_re, da_im, dls
    G['s5_b_re'], G['s5_b_im'] = db_re, db_im
    G['s5_c_re'], G['s5_c_im'] = undiag_out(dcmat[:4]), -undiag_out(dcmat[4:])

    def out_bwd_fn(i, n, R, P, X, C):
        _, vjp = jax.vjp(_rwkv_out, *R[:5], *C)
        gs = vjp(R[5])
        return gs[:5], gs[5:8]

    dy, dr1, dk1, dv1, dg, dlw, dlb, drk = _rowcall("rwkv_out_bwd", out_bwd_fn, L, TM, [y, r, k2, v, g, do_a], out_consts,
                                                    out_rows=[row(RWKV_W)] * 5, out_accs=[(1, RWKV_W)] * 3)
    G['rwkv_lnx_w'], G['rwkv_lnx_b'], G['rwkv_r_k'] = dlw, dlb, drk
    dr2, dlwk, dk2b, dv2, dan, dbv = _wkv7_bwd(r, lw, k2, v, an, bv, ck, xinv, dy)

    def prep_bwd_fn(i, n, R, P, X, C):
        p_ = R[0]
        d1 = _shift_down(p_, P[0], i, 1) - p_
        q = p_ + d1 * C[0]
        _, vjp = jax.vjp(_prep, q, *C[1:])
        cots = (R[1] + R[2], R[3], R[4] + R[5], R[6] + R[7], R[8], R[9], R[10])
        gs = vjp(cots)
        return (gs[0],), (_sum0(gs[0] * d1),) + tuple(gs[1:8])

    small, lowr = (1, RWKV_W), (128, RWKV_W)
    dq, dmu, dw0, da0, dkk, dka, dw2p, da2p, dg2 = _rowcall(
        "rwkv_prep_bwd", prep_bwd_fn, L, TM, [p, dr1, dr2, dlwk, dk1, dk2b, dv1, dv2, dan, dbv, dg],
        prep_consts, out_rows=[row(N_RWKV)], out_accs=[(1, N_RWKV)] + [small] * 4 + [lowr] * 3, prev=[0])
    G['rwkv_shift_mu'], G['rwkv_w0'], G['rwkv_a0'], G['rwkv_k_k'], G['rwkv_k_a'] = dmu, dw0, da0, dkk, dka
    G['rwkv_w2'], G['rwkv_a2'], G['rwkv_g2'] = dw2p[:64], da2p[64:], dg2

    def shift_bwd_fn(i, n, R, P, X, C):
        dm = R[0] * C[0]
        return (R[0] - dm + _shift_up(dm, X[0] * C[0], i, n, 1),), ()

    (dp,) = _rowcall("shift_bwd", shift_bwd_fn, L, TS, [dq], [S['rwkv_shift_mu']], out_rows=[row(N_RWKV, bf16)], nxt=[0])

    dproj = jnp.concatenate([dp, du, dgp], axis=1)
    dh1 = _mm(dproj, w_in_t, 'nn', "mm_in_dx")
    G['w_in'] = _mm(dproj, h1, 'tn', "mm_in_dw")

    def norm1_bwd_fn(i, n, R, P, X, C):
        _, vjp = jax.vjp(_rms, R[0], C[0])
        dxa, dg1_ = vjp(R[2])
        return (R[1] + dxa,), (dg1_,)

    dx, G['norm_mix_pre'] = _rowcall("norm_pre_bwd", norm1_bwd_fn, L, TS, [x, dx1, dh1], [g1],
                                     out_rows=[row(D_MODEL)], out_accs=[(1, D_MODEL)])
    return loss, dx, G, chip_e, slots_e


def kernel(x, norm_mix_pre, norm_mix_post, norm_ffn_pre, norm_ffn_post, w_in, b_gate, rwkv_shift_mu, rwkv_w0, rwkv_w2, rwkv_a0, rwkv_a2, rwkv_g2, rwkv_k_k, rwkv_k_a, rwkv_r_k, rwkv_lnx_w, rwkv_lnx_b, s5_a_re, s5_a_im, s5_b_re, s5_b_im, s5_c_re, s5_c_im, s5_d, s5_log_step, s5_w_glu, s5_b_glu, w_branch_rwkv, w_branch_s5, w_out, ffn_w_up, ffn_conv_w, ffn_conv_b, ffn_w_down, loss_target, m_norm_mix_pre, m_norm_mix_post, m_norm_ffn_pre, m_norm_ffn_post, m_w_in, m_b_gate, m_rwkv_shift_mu, m_rwkv_w0, m_rwkv_w2, m_rwkv_a0, m_rwkv_a2, m_rwkv_g2, m_rwkv_k_k, m_rwkv_k_a, m_rwkv_r_k, m_rwkv_lnx_w, m_rwkv_lnx_b, m_s5_a_re, m_s5_a_im, m_s5_b_re, m_s5_b_im, m_s5_c_re, m_s5_c_im, m_s5_d, m_s5_log_step, m_s5_w_glu, m_s5_b_glu, m_w_branch_rwkv, m_w_branch_s5, m_w_out, m_ffn_w_up, m_ffn_conv_w, m_ffn_conv_b, m_ffn_w_down, v_norm_mix_pre, v_norm_mix_post, v_norm_ffn_pre, v_norm_ffn_post, v_w_in, v_b_gate, v_rwkv_shift_mu, v_rwkv_w0, v_rwkv_w2, v_rwkv_a0, v_rwkv_a2, v_rwkv_g2, v_rwkv_k_k, v_rwkv_k_a, v_rwkv_r_k, v_rwkv_lnx_w, v_rwkv_lnx_b, v_s5_a_re, v_s5_a_im, v_s5_b_re, v_s5_b_im, v_s5_c_re, v_s5_c_im, v_s5_d, v_s5_log_step, v_s5_w_glu, v_s5_b_glu, v_w_branch_rwkv, v_w_branch_s5, v_w_out, v_ffn_w_up, v_ffn_conv_w, v_ffn_conv_b, v_ffn_w_down):
    A = dict(locals())
    me = 2 * lax.axis_index("x") + lax.axis_index("y")
    blk = lambda n: A[n][0]

    mine = {n: (blk(n).T if n == 'w_in' else blk(n)).astype(bf16) for n in BIG}
    mine.update({n: blk(n) for n in TINY})
    mine['ffn_conv_w'] = jnp.pad(blk('ffn_conv_w'), ((0, 5), (0, 0)))
    late = [n for n in BIG if n != 'w_in']
    W = _gather_weights({n: blkv for n, blkv in mine.items() if n not in late})
    W.update(_gather_pair(W, [n for n in BIG if n not in late], "gather_weights_pair"))
    S = {n: A[n].reshape(1, -1) for n in SMALL}

    loss, dx, G, chip_e, slots_e = _forward_backward(x[0], loss_target[0], W, S, {n: mine[n] for n in late})

    tiny_shapes = [G[n].shape for n in TINY]
    small_buf = _pack_rows([G[n] for n in SMALL] + [G[n] for n in TINY] + [loss], SMALL_ROWS)
    recv, small_recv = _grads_to_sibling(G, ['w_in'], "grads_to_sibling", small_buf)
    chip_l, small_sum = _pair_add(G, recv, ['w_in'], "grads_pair_sum", small_buf, small_recv)
    slots_l, small4 = _grads_chip_exchange(chip_l, ['w_in'], small_sum)
    half, half['small'] = _sum_slots({**slots_e, **slots_l}, {**chip_e, **chip_l}, small4, small_sum)
    other = _halves_to_sibling(half)
    pc = lax.axis_index("c")
    small_tot = _join_halves(half['small'], other['small'], pc)
    grad = {n: _join_halves(half[n], other[n], pc) for n in BIG}
    grad['w_in'] = grad['w_in'].T
    vals = _unpack_rows(small_tot, [A[n].shape for n in SMALL] + tiny_shapes + [(1, PACK_W)])
    grad.update(zip(SMALL, vals))
    for n, full in zip(TINY, vals[len(SMALL):]):
        cs = A[n].shape[2]
        grad[n] = lax.dynamic_slice_in_dim(full, me * cs, cs, axis=1)
    loss_out = vals[-1][0, 0]

    packed = SMALL + TINY
    groups = [(blk(n), grad[n], blk('m_' + n), blk('v_' + n)) for n in BIG]
    groups.append(tuple(_pack_rows([src(n) for n in packed], ADAM_ROWS)
                        for src in (lambda n: A[n], lambda n: grad[n], lambda n: A['m_' + n], lambda n: A['v_' + n])))
    res = _adamw(groups)
    outs = [dict(), dict(), dict()]
    for n, r3 in zip(BIG, res[:-1]):
        for d, val in zip(outs, r3):
            d[n] = val
    for d, buf in zip(outs, res[-1]):
        d.update(zip(packed, _unpack_rows(buf, [A[n].shape for n in packed])))
    full = lambda d: [d[n].reshape(A[n].shape) for n in WEIGHTS]
    return (loss_out, dx[None], *full(grad), *full(outs[0]), *full(outs[1]), *full(outs[2]))
```

```python
import functools

import jax
import jax.numpy as jnp
from jax import lax
from jax.experimental import pallas as pl
from jax.experimental.pallas import tpu as pltpu

f32, bf16 = jnp.float32, jnp.bfloat16
MESH = pl.DeviceIdType.MESH

D_MODEL = 1024
RWKV_W = 512
HEADS, HEAD = 8, 64
N_RWKV = 1792
S5_W = 512
S5_G, S5_P, S5_C = 32, 64, 16
S5_N = S5_G * S5_P
D_FF = 2816
NORM_EPS = 1e-6
LNX_EPS = 64e-5
ADAM_LR, ADAM_B1, ADAM_B2, ADAM_EPS, ADAM_WD, ADAM_STEP = 0.001, 0.9, 0.999, 1e-08, 0.01, 10

VMEM_LIMIT = 48 * 1024 * 1024
PACK_W = 1024
WKV_C = 64
WKV_SUB = 4
WKV_ROWS = WKV_C * WKV_SUB
RESIDENT_BUDGET = 40 * 1024 * 1024
S5_T = 256

WEIGHTS = ['norm_mix_pre', 'norm_mix_post', 'norm_ffn_pre', 'norm_ffn_post', 'w_in', 'b_gate', 'rwkv_shift_mu',
           'rwkv_w0', 'rwkv_w2', 'rwkv_a0', 'rwkv_a2', 'rwkv_g2', 'rwkv_k_k', 'rwkv_k_a', 'rwkv_r_k', 'rwkv_lnx_w',
           'rwkv_lnx_b', 's5_a_re', 's5_a_im', 's5_b_re', 's5_b_im', 's5_c_re', 's5_c_im', 's5_d', 's5_log_step',
           's5_w_glu', 's5_b_glu', 'w_branch_rwkv', 'w_branch_s5', 'w_out', 'ffn_w_up', 'ffn_conv_w', 'ffn_conv_b',
           'ffn_w_down']


def _ceil_to(n, m):
    return -(-n // m) * m


def _mesh_pos():
    return lax.axis_index("x"), lax.axis_index("y"), lax.axis_index("c")


def _pick(d, cap=4096):
    for c in (1024, 1408, 2176, 896, 512, 256, 128):
        if c <= cap and d % c == 0:
            return c
    raise ValueError(d)


def _mm_resident(a, w, mode, name, M, N, K, out_dtype):
    budget = RESIDENT_BUDGET - 2 * K * N
    tm = next(t for t in (512, 256, 128) if 2 * t * (K * a.dtype.itemsize + 4 * N) <= budget)
    dims = _DIMS[mode]

    def body(a_ref, w_ref, o_ref):
        o_ref[...] = lax.dot_general(a_ref[...].astype(bf16), w_ref[...], (dims, ((), ())),
                                     preferred_element_type=f32).astype(o_ref.dtype)

    return pl.pallas_call(
        body, name=name, grid=(M // tm,),
        in_specs=[pl.BlockSpec((tm, K), lambda i: (i, 0)),
                  pl.BlockSpec(w.shape, lambda i: (0, 0), pipeline_mode=pl.Buffered(1))],
        out_specs=pl.BlockSpec((tm, N), lambda i: (i, 0)), out_shape=jax.ShapeDtypeStruct((M, N), out_dtype),
        compiler_params=pltpu.CompilerParams(dimension_semantics=("parallel",), vmem_limit_bytes=VMEM_LIMIT),
    )(a, w)


def _mm(a, b, mode, name, out_dtype=f32):
    if mode == 'tn':
        (K, M), (K2, N) = a.shape, b.shape
    elif mode == 'nt':
        (M, K), (N, K2) = a.shape, b.shape
    else:
        (M, K), (K2, N) = a.shape, b.shape
    assert K == K2, (name, a.shape, b.shape)
    if mode != 'tn' and b.dtype == bf16:
        return _mm_resident(a, b, mode, name, M, N, K, out_dtype)
    if mode == 'tn':
        tm = _pick(M, 2176)
        tn = _pick(N, 512 if tm > 1408 else (1024 if tm > 1024 else 1408))
        tk = _pick(K, 1024 if a.dtype == bf16 and b.dtype == bf16 else 512)
    else:
        tm, tn, tk = _pick(M, 512), _pick(N), _pick(K)
    nk = K // tk
    dims = {'nn': ((1,), (0,)), 'nt': ((1,), (1,)), 'tn': ((0,), (0,))}[mode]

    def body(a_ref, b_ref, o_ref, acc_ref):
        k = pl.program_id(2)

        @pl.when(k == 0)
        def _():
            acc_ref[...] = jnp.zeros_like(acc_ref)

        acc_ref[...] += lax.dot_general(a_ref[...].astype(bf16), b_ref[...].astype(bf16), (dims, ((), ())),
                                        preferred_element_type=f32)

        @pl.when(k == nk - 1)
        def _():
            o_ref[...] = acc_ref[...].astype(o_ref.dtype)

    a_spec = pl.BlockSpec((tk, tm), lambda i, j, k: (k, i)) if mode == 'tn' else pl.BlockSpec((tm, tk), lambda i, j, k: (i, k))
    b_spec = pl.BlockSpec((tn, tk), lambda i, j, k: (j, k)) if mode == 'nt' else pl.BlockSpec((tk, tn), lambda i, j, k: (k, j))
    return pl.pallas_call(
        body, name=name, grid=(M // tm, N // tn, nk),
        in_specs=[a_spec, b_spec], out_specs=pl.BlockSpec((tm, tn), lambda i, j, k: (i, j)),
        out_shape=jax.ShapeDtypeStruct((M, N), out_dtype),
        scratch_shapes=[pltpu.VMEM((tm, tn), f32)],
        compiler_params=pltpu.CompilerParams(dimension_semantics=("parallel", "parallel", "arbitrary"),
                                             vmem_limit_bytes=VMEM_LIMIT),
    )(a, b)


def _rowcall(name, fn, L, tm, rows, consts=(), out_rows=(), out_accs=(), prev=(), nxt=()):
    nsteps = L // tm
    nb8 = tm // 8
    last8 = L // 8 - 1
    n_r, n_p, n_x, n_c, n_or = len(rows), len(prev), len(nxt), len(consts), len(out_rows)

    def body(*refs):
        i = pl.program_id(0)
        vals = [r[...] for r in refs[:n_r + n_p + n_x + n_c]]
        R, P = vals[:n_r], vals[n_r:n_r + n_p]
        X, C = vals[n_r + n_p:n_r + n_p + n_x], vals[n_r + n_p + n_x:]
        o_refs = refs[n_r + n_p + n_x + n_c:]
        outs_r, outs_a = fn(i, nsteps, R, P, X, C)
        for ref, v in zip(o_refs[:n_or], outs_r, strict=True):
            ref[...] = v.astype(ref.dtype)
        if out_accs:
            @pl.when(i == 0)
            def _():
                for ref in o_refs[n_or:]:
                    ref[...] = jnp.zeros_like(ref)

            for ref, v in zip(o_refs[n_or:], outs_a, strict=True):
                ref[...] += v

    def const_spec(c):
        nd = c.ndim
        return pl.BlockSpec(c.shape, lambda i: (0,) * nd)

    in_specs = ([pl.BlockSpec((tm, a.shape[1]), lambda i: (i, 0)) for a in rows]
                + [pl.BlockSpec((8, rows[j].shape[1]), lambda i: (jnp.maximum(i * nb8 - 1, 0), 0)) for j in prev]
                + [pl.BlockSpec((8, rows[j].shape[1]), lambda i: (jnp.minimum((i + 1) * nb8, last8), 0)) for j in nxt]
                + [const_spec(c) for c in consts])
    out_specs = ([pl.BlockSpec((tm, c), lambda i: (i, 0)) for c, _ in out_rows]
                 + [pl.BlockSpec(s, lambda i: (0, 0)) for s in out_accs])
    out_shape = ([jax.ShapeDtypeStruct((L, c), dt) for c, dt in out_rows]
                 + [jax.ShapeDtypeStruct(s, f32) for s in out_accs])
    args = list(rows) + [rows[j] for j in prev] + [rows[j] for j in nxt] + list(consts)
    return pl.pallas_call(
        body, name=name, grid=(nsteps,), in_specs=in_specs, out_specs=out_specs, out_shape=out_shape,
        compiler_params=pltpu.CompilerParams(dimension_semantics=("arbitrary",), vmem_limit_bytes=VMEM_LIMIT),
    )(*args)


def _shift_down(x, prev8, i, k):
    rolled = pltpu.roll(x, k, axis=0)
    pfix = jnp.where(i > 0, pltpu.roll(prev8, k, axis=0), 0.0)
    row8 = lax.broadcasted_iota(jnp.int32, pfix.shape, 0)
    top = jnp.where(row8 < k, pfix, rolled[:8])
    return top if x.shape[0] == 8 else jnp.concatenate([top, rolled[8:]], axis=0)


def _shift_up(x, next8, i, nsteps, k):
    tm = x.shape[0]
    rolled = pltpu.roll(x, tm - k, axis=0)
    nfix = jnp.where(i < nsteps - 1, pltpu.roll(next8, 8 - k, axis=0), 0.0)
    row8 = lax.broadcasted_iota(jnp.int32, nfix.shape, 0)
    bot = jnp.where(row8 >= 8 - k, nfix, rolled[tm - 8:])
    return jnp.concatenate([rolled[:tm - 8], bot], axis=0)


def _sum0(x):
    return jnp.sum(x, axis=0, keepdims=True)


def _rms(x, g):
    return x * lax.rsqrt(jnp.mean(x * x, axis=-1, keepdims=True) + NORM_EPS) * g


def _softplus(x):
    return jnp.maximum(x, 0.0) + jnp.log(1.0 + jnp.exp(-jnp.abs(x)))


def _gelu(x):
    return 0.5 * x * (1.0 + jnp.tanh(0.7978845608028654 * (x + 0.044715 * x * x * x)))


def _dot32(a, b):
    return jnp.dot(a, b, preferred_element_type=f32, precision=lax.Precision.HIGHEST)


def _seg_raw(x, E):
    hi = x.astype(bf16)
    r1 = x - hi.astype(f32)
    mid = r1.astype(bf16)
    lo = (r1 - mid.astype(f32)).astype(bf16)
    Eb = E.astype(bf16)
    dot = lambda t: jnp.dot(t, Eb, preferred_element_type=f32)
    return (dot(lo) + dot(mid)) + dot(hi)


@jax.custom_vjp
def _seg(x, E):
    return _seg_raw(x, E)


_seg.defvjp(lambda x, E: (_seg_raw(x, E), E), lambda E, g: (_seg_raw(g, E), jnp.zeros_like(E)))


def _prep(q, w0, a0, k_k, k_a, w2p, a2p, g2, E):
    r, k, v = q[:, 0:512], q[:, 512:1024], q[:, 1024:1536]
    wa, gd = q[:, 1536:1664], q[:, 1664:1792]
    wlog = -_softplus(-(w0 + _bdot(jnp.tanh(wa), w2p, 'nn'))) - 0.5
    lw = -jnp.exp(wlog)
    a = jax.nn.sigmoid(a0 + _bdot(wa, a2p, 'nn'))
    g = _bdot(jax.nn.sigmoid(gd), g2, 'nn')
    kk = k * k_k
    kkn = kk / jnp.maximum(jnp.sqrt(_seg(kk * kk, E)), 1e-12)
    k2 = k * (1.0 + (a - 1.0) * k_a)
    return r, lw, k2, v, -kkn, kkn * a, g


def _rwkv_out(y, r, k2, v, g, lnx_w, lnx_b, r_k, E):
    mean = _seg(y, E) * (1.0 / HEAD)
    yc = y - mean
    var = _seg(yc * yc, E) * (1.0 / HEAD)
    yn = yc * lax.rsqrt(var + LNX_EPS) * lnx_w + lnx_b
    bonus = _seg(r * k2 * r_k, E) * v
    return (yn + bonus) * g


def _s5_mid(ysc, u, d):
    return _gelu(ysc + d * u)


def _s5_glu(yg, z2, b_glu):
    return yg * jax.nn.sigmoid(z2 + b_glu)


def _merge(gp, o_r, o_s, b_gate):
    gates = jax.nn.sigmoid(gp + b_gate)
    return gates[:, :D_MODEL] * o_r + gates[:, D_MODEL:] * o_s


def _act(zc):
    return _gelu(zc[:, :D_FF]) * zc[:, D_FF:]


def _s5_disc(a_re, a_im, ls, b_re, b_im):
    dt = jnp.exp(ls)
    er = jnp.exp(a_re * dt)
    ar, ai = er * jnp.cos(a_im * dt), er * jnp.sin(a_im * dt)
    x, y = ar - 1.0, ai
    den = a_re * a_re + a_im * a_im
    fr, fi = (x * a_re + y * a_im) / den, (y * a_re - x * a_im) / den
    return ar, ai, fr * b_re - fi * b_im, fr * b_im + fi * b_re


_DIMS = {'nn': ((1,), (0,)), 'nt': ((1,), (1,)), 'tn': ((0,), (0,))}


def _raw_bdot(a, b, mode):
    return lax.dot_general(a.astype(bf16), b.astype(bf16), (_DIMS[mode], ((), ())), preferred_element_type=f32)


@functools.partial(jax.custom_vjp, nondiff_argnums=(2,))
def _bdot(a, b, mode):
    return _raw_bdot(a, b, mode)


def _bdot_fwd(a, b, mode):
    return _raw_bdot(a, b, mode), (a, b)


def _bdot_bwd(mode, res, g):
    a, b = res
    if mode == 'nn':
        return _raw_bdot(g, b, 'nt'), _raw_bdot(a, g, 'tn')
    if mode == 'nt':
        return _raw_bdot(g, b, 'nn'), _raw_bdot(g, a, 'tn')
    return _raw_bdot(b, g, 'nt'), _raw_bdot(a, g, 'nn')


_bdot.defvjp(_bdot_fwd, _bdot_bwd)


def _tri_inv_raw(A):
    n = A[0].shape[0]
    eye = (lax.broadcasted_iota(jnp.int32, (n, n), 0) == lax.broadcasted_iota(jnp.int32, (n, n), 1)).astype(f32)
    x = [eye + a for a in A]
    pw, m = A, 1
    while 2 * m < n // 2:
        pw = [_raw_bdot(p, p, 'nn') for p in pw]
        x = [xi + _raw_bdot(xi, p, 'nn') for xi, p in zip(x, pw)]
        m *= 2
    return x


@jax.custom_vjp
def _tri_inv(A):
    return _tri_inv_raw(A)


def _tri_inv_fwd(A):
    x = _tri_inv_raw(A)
    return x, x


def _tri_inv_bwd(x, g):
    return ([_raw_bdot(_raw_bdot(xi, gi, 'tn'), xi, 'nt') for xi, gi in zip(x, g)],)


_tri_inv.defvjp(_tri_inv_fwd, _tri_inv_bwd)


@jax.custom_vjp
def _inv_given(A, X):
    return X


_inv_given.defvjp(lambda A, X: (X, X),
                  lambda x, g: (_tri_inv_bwd(x, g)[0], [jnp.zeros_like(xi) for xi in x]))


def _wkv_chunk(S0, r, lw, k, v, a, b, tri, bd, xinv=None):
    C = r[0].shape[0]
    P = range(len(r))
    lane = lax.broadcasted_iota(jnp.int32, (1, 2 * HEAD), 1)
    m0, m1 = (lane < HEAD).astype(f32), (lane >= HEAD).astype(f32)
    cat = lambda *xs: jnp.concatenate(xs, axis=0)
    stack = lambda x: cat(x * m0, x * m1)
    unstack = lambda x2: m0 * x2[:C] + m1 * x2[C:]
    rid = lax.broadcasted_iota(jnp.int32, (2 * C, 2 * C), 0)
    cid = lax.broadcasted_iota(jnp.int32, (2 * C, 2 * C), 1)
    same = (rid < C) == (cid < C)
    eye2 = (rid == cid).astype(f32)
    tri2 = (same & (rid >= cid)).astype(f32)
    sl2 = tri2 - eye2
    cum = [_dot32(tri, lw[p]) for p in P]
    g = [jnp.exp(cum[p]) for p in P]
    gi = [jnp.exp(-cum[p]) for p in P]
    at = [a[p] * jnp.exp(cum[p] - lw[p]) for p in P]
    rt = [r[p] * g[p] for p in P]
    kb = [k[p] * gi[p] for p in P]
    bb = [b[p] * gi[p] for p in P]
    lhs = [cat(stack(at[p]), stack(rt[p])) for p in P]
    pb = [_bdot(lhs[p], stack(bb[p]), 'nt') for p in P]
    pk = [_bdot(lhs[p], stack(kb[p]), 'nt') for p in P]
    aab = [pb[p][:2 * C] * sl2 for p in P]
    base = [_bdot(cat(at[p], rt[p]), S0[p], 'nt') for p in P]
    t = [_bdot(cat(pk[p][:2 * C] * sl2, pk[p][2 * C:] * tri2), cat(v[p], v[p]), 'nn') for p in P]
    rhs = [cat(base[p][:C], base[p][:C]) + t[p][:2 * C] for p in P]
    x = _tri_inv(aab) if xinv is None else _inv_given(aab, xinv)
    u = [unstack(_bdot(x[p], rhs[p], 'nn')) for p in P]
    w2 = [_bdot(pb[p][2 * C:] * tri2, cat(u[p], u[p]), 'nn') for p in P]
    y = [base[p][C:] + unstack(t[p][2 * C:]) + unstack(w2[p]) for p in P]
    S1 = [g[p][C - 1:C, :] * (S0[p] + bd * _bdot(cat(v[p], u[p]), cat(kb[p], bb[p]), 'tn')) for p in P]
    return y, S1, x


def _pairs(x):
    return [x[:, 2 * HEAD * p:2 * HEAD * (p + 1)] for p in range(HEADS // 2)]


def _wkv_consts():
    tri = jnp.tril(jnp.ones((WKV_C, WKV_C), f32))
    hid = jnp.arange(2 * HEAD) // HEAD
    return tri, (hid[:, None] == hid[None, :]).astype(f32)


def _wkv_step(S0, r, lw, k, v, a, b, tri, bd, xinv=None):
    ys, xs, S = [], [], S0
    for c in range(WKV_SUB):
        sub = lambda t: [x[c * WKV_C:(c + 1) * WKV_C] for x in t]
        y, S, x = _wkv_chunk(S, sub(r), sub(lw), sub(k), sub(v), sub(a), sub(b), tri, bd, None if xinv is None else xinv[c])
        ys.append(y)
        xs.append(x)
    return [jnp.concatenate([y[p] for y in ys], axis=0) for p in range(len(S0))], S, xs


def _wkv7_fwd(r, lw, k, v, a, b):
    L = r.shape[0]
    nc, npair = L // WKV_ROWS, HEADS // 2

    def body(r_ref, lw_ref, k_ref, v_ref, a_ref, b_ref, tri_ref, bd_ref, y_ref, ck_ref, xi_ref, s_ref):
        @pl.when(pl.program_id(0) == 0)
        def _():
            s_ref[...] = jnp.zeros_like(s_ref)

        s0 = [s_ref[p] for p in range(npair)]
        for p in range(npair):
            ck_ref[0, p] = s0[p]
        y, s1, xs = _wkv_step(s0, *(_pairs(x) for x in (r_ref, lw_ref, k_ref, v_ref, a_ref, b_ref)), tri_ref[...], bd_ref[...])
        for p in range(npair):
            y_ref[:, 2 * HEAD * p:2 * HEAD * (p + 1)] = y[p]
            s_ref[p] = s1[p]
            for c in range(WKV_SUB):
                xi_ref[0, c, p] = xs[c][p].astype(xi_ref.dtype)

    row = pl.BlockSpec((WKV_ROWS, RWKV_W), lambda c: (c, 0))
    sspec = pl.BlockSpec((1, npair, 2 * HEAD, 2 * HEAD), lambda c: (c, 0, 0, 0))
    xspec = pl.BlockSpec((1, WKV_SUB, npair, 2 * HEAD, 2 * HEAD), lambda c: (c, 0, 0, 0, 0))
    return pl.pallas_call(
        body, name="wkv7_fwd", grid=(nc,),
        in_specs=[row] * 6 + [pl.BlockSpec((WKV_C, WKV_C), lambda c: (0, 0)), pl.BlockSpec((2 * HEAD, 2 * HEAD), lambda c: (0, 0))],
        out_specs=[row, sspec, xspec],
        out_shape=[jax.ShapeDtypeStruct((L, RWKV_W), f32), jax.ShapeDtypeStruct((nc, npair, 2 * HEAD, 2 * HEAD), f32),
                   jax.ShapeDtypeStruct((nc, WKV_SUB, npair, 2 * HEAD, 2 * HEAD), bf16)],
        scratch_shapes=[pltpu.VMEM((npair, 2 * HEAD, 2 * HEAD), f32)],
        compiler_params=pltpu.CompilerParams(dimension_semantics=("arbitrary",), vmem_limit_bytes=VMEM_LIMIT),
    )(r, lw, k, v, a, b, *_wkv_consts())


def _wkv7_bwd(r, lw, k, v, a, b, ck, xinv, dy):
    L = r.shape[0]
    nc, npair = L // WKV_ROWS, HEADS // 2

    def body(r_ref, lw_ref, k_ref, v_ref, a_ref, b_ref, ck_ref, xi_ref, dy_ref, tri_ref, bd_ref,
             dr_ref, dlw_ref, dk_ref, dv_ref, da_ref, db_ref, ds_ref):
        @pl.when(pl.program_id(0) == 0)
        def _():
            ds_ref[...] = jnp.zeros_like(ds_ref)

        tri, bd = tri_ref[...], bd_ref[...]
        ins = [[ck_ref[0, p] for p in range(npair)]] + [_pairs(x) for x in (r_ref, lw_ref, k_ref, v_ref, a_ref, b_ref)]
        xs = [[xi_ref[0, c, p].astype(f32) for p in range(npair)] for c in range(WKV_SUB)]
        _, vjp = jax.vjp(lambda *t: _wkv_step(*t, tri, bd, xs)[:2], *ins)
        gs = vjp((_pairs(dy_ref), [ds_ref[p] for p in range(npair)]))
        for p in range(npair):
            ds_ref[p] = gs[0][p]
            for ref, gval in zip((dr_ref, dlw_ref, dk_ref, dv_ref, da_ref, db_ref), gs[1:]):
                ref[:, 2 * HEAD * p:2 * HEAD * (p + 1)] = gval[p]

    row = pl.BlockSpec((WKV_ROWS, RWKV_W), lambda c: (nc - 1 - c, 0))
    sspec = pl.BlockSpec((1, npair, 2 * HEAD, 2 * HEAD), lambda c: (nc - 1 - c, 0, 0, 0))
    xspec = pl.BlockSpec((1, WKV_SUB, npair, 2 * HEAD, 2 * HEAD), lambda c: (nc - 1 - c, 0, 0, 0, 0))
    return pl.pallas_call(
        body, name="wkv7_bwd", grid=(nc,),
        in_specs=[row] * 6 + [sspec, xspec, row, pl.BlockSpec((WKV_C, WKV_C), lambda c: (0, 0)),
                              pl.BlockSpec((2 * HEAD, 2 * HEAD), lambda c: (0, 0))],
        out_specs=[row] * 6,
        out_shape=[jax.ShapeDtypeStruct((L, RWKV_W), f32)] * 6,
        scratch_shapes=[pltpu.VMEM((npair, 2 * HEAD, 2 * HEAD), f32)],
        compiler_params=pltpu.CompilerParams(dimension_semantics=("arbitrary",), vmem_limit_bytes=VMEM_LIMIT),
    )(r, lw, k, v, a, b, ck, xinv, dy, *_wkv_consts())


def _cmul(ar, ai, xr, xi):
    return ar * xr - ai * xi, ar * xi + ai * xr


def _scan_init(a_ref, car_ref, pw_ref, reverse):
    car_ref[...] = jnp.zeros_like(car_ref)
    ar = jnp.broadcast_to(a_ref[:, :S5_N], (8, S5_N))
    ai = jnp.broadcast_to(a_ref[:, S5_N:], (8, S5_N))
    if reverse:
        ai = -ai
    row = lax.broadcasted_iota(jnp.int32, (8, S5_N), 0)
    pr, pi = ar, ai
    qr, qi = jnp.zeros((8, S5_N), f32), jnp.zeros((8, S5_N), f32)
    for e in range(1, 9):
        sel = (row == 8 - e) if reverse else (row == e - 1)
        qr, qi = jnp.where(sel, pr, qr), jnp.where(sel, pi, qi)
        if e in (1, 2, 4):
            j = (1, 2, 4).index(e)
            pw_ref[j, :, :S5_N] = pr
            pw_ref[j, :, S5_N:] = pi
        pr, pi = _cmul(pr, pi, ar, ai)
    pw_ref[3, :, :S5_N] = qr
    pw_ref[3, :, S5_N:] = qi


def _scan_tile(x_ref, o_ref, car_ref, pw_ref, reverse):
    ng = x_ref.shape[0] // 8
    row = lax.broadcasted_iota(jnp.int32, (8, S5_N), 0)

    def group(gi, carry):
        g = (ng - 1 - gi) if reverse else gi
        t0 = pl.multiple_of(g * 8, 8)
        xr, xi = x_ref[pl.ds(t0, 8), :S5_N], x_ref[pl.ds(t0, 8), S5_N:]
        for j, d in enumerate((1, 2, 4)):
            if reverse:
                sr = jnp.where(row < 8 - d, pltpu.roll(xr, 8 - d, axis=0), 0.0)
                si = jnp.where(row < 8 - d, pltpu.roll(xi, 8 - d, axis=0), 0.0)
            else:
                sr = jnp.where(row >= d, pltpu.roll(xr, d, axis=0), 0.0)
                si = jnp.where(row >= d, pltpu.roll(xi, d, axis=0), 0.0)
            mr, mi = _cmul(pw_ref[j, :, :S5_N], pw_ref[j, :, S5_N:], sr, si)
            xr, xi = xr + mr, xi + mi
        cr, ci = carry
        mr, mi = _cmul(pw_ref[3, :, :S5_N], pw_ref[3, :, S5_N:], cr, ci)
        xr, xi = xr + mr, xi + mi
        o_ref[pl.ds(t0, 8), :S5_N] = xr
        o_ref[pl.ds(t0, 8), S5_N:] = xi
        e = 0 if reverse else 7
        return (jnp.broadcast_to(xr[e:e + 1, :], (8, S5_N)), jnp.broadcast_to(xi[e:e + 1, :], (8, S5_N)))

    cr, ci = lax.fori_loop(0, ng, group, (car_ref[:, :S5_N], car_ref[:, S5_N:]))
    car_ref[:, :S5_N] = cr
    car_ref[:, S5_N:] = ci


_CB, _SB = 128, 512


def _cblk(k):
    return slice(_CB * k, _CB * (k + 1))


def _sblk(j):
    return slice(_SB * j, _SB * (j + 1))


def _s5_fwd(u, bmat, cmat, abar, late):
    L = u.shape[0]
    nt = L // S5_T
    names = list(late)
    nh = len(names)

    def body(u_ref, b_ref, c_ref, a_ref, *rest):
        h_in, (st_ref, y_ref), h_out = rest[:nh], rest[nh:nh + 2], rest[nh + 2:2 * nh + 2]
        bu_ref, car_ref, pw_ref, ssem, rsem, lsem = rest[2 * nh + 2:]
        i = pl.program_id(0)

        def copies():
            px, py, pc = _mesh_pos()
            me = 2 * px + py
            out = []
            for a, nm in enumerate(names):
                hr = late[nm].shape[0] // 2
                src, dst = h_in[a].at[pl.ds(pl.multiple_of(pc * hr, 16), hr), :], _slab(h_out[a], nm, me, pc)
                out.append(pltpu.make_async_copy(src, dst, lsem.at[a]))
                out += [pltpu.make_async_remote_copy(src, dst, ssem.at[3 * a + k], rsem.at[3 * a + k],
                                                     device_id=(qx, qy, pc), device_id_type=MESH)
                        for k, (qx, qy) in enumerate(_chip_peers(px, py))]
            return out

        @pl.when(i == 0)
        def _():
            _scan_init(a_ref, car_ref, pw_ref, False)
            for cp in copies():
                cp.start()

        for j in range(8):
            bu_ref[:, _sblk(j)] = _raw_bdot(u_ref[:, _cblk(j % 4)], b_ref[j], 'nn')
        _scan_tile(bu_ref, st_ref, car_ref, pw_ref, False)
        for k in range(4):
            y_ref[:, _cblk(k)] = (_raw_bdot(st_ref[:, _sblk(k)], c_ref[k], 'nn')
                                  + _raw_bdot(st_ref[:, _sblk(4 + k)], c_ref[4 + k], 'nn'))

        @pl.when(i == nt - 1)
        def _():
            for cp in copies():
                cp.wait()

    whole = lambda shape: pl.BlockSpec(shape, lambda i: (0,) * len(shape))
    outs = pl.pallas_call(
        body, name="s5_fwd", grid=(nt,),
        in_specs=[pl.BlockSpec((S5_T, S5_W), lambda i: (i, 0)), whole(bmat.shape), whole(cmat.shape), whole(abar.shape)]
        + [ANY] * nh,
        out_specs=[pl.BlockSpec((S5_T, 2 * S5_N), lambda i: (i, 0)), pl.BlockSpec((S5_T, S5_W), lambda i: (i, 0))] + [ANY] * nh,
        out_shape=[jax.ShapeDtypeStruct((L, 2 * S5_N), f32), jax.ShapeDtypeStruct((L, S5_W), f32)]
        + [jax.ShapeDtypeStruct(GATHER[nm][0], late[nm].dtype) for nm in names],
        scratch_shapes=[pltpu.VMEM((S5_T, 2 * S5_N), f32), pltpu.VMEM((8, 2 * S5_N), f32), pltpu.VMEM((4, 8, 2 * S5_N), f32),
                        pltpu.SemaphoreType.DMA((3 * nh,)), pltpu.SemaphoreType.DMA((3 * nh,)), pltpu.SemaphoreType.DMA((nh,))],
        compiler_params=pltpu.CompilerParams(dimension_semantics=("arbitrary",), vmem_limit_bytes=VMEM_LIMIT),
    )(u, bmat, cmat, abar, *[late[nm] for nm in names])
    return outs[0], outs[1], dict(zip(names, outs[2:]))


def _s5_bwd(dy, st, u, du_direct, bmat, cmat, abar, chip_sum):
    L = u.shape[0]
    nt = L // S5_T
    nb8 = S5_T // 8
    names = list(chip_sum)
    nh = len(names)

    def body(dy_ref, st_ref, sp_ref, u_ref, dud_ref, b_ref, c_ref, a_ref, *rest):
        x_in, (du_ref, db_ref, dc_ref, da_ref), x_out = rest[:nh], rest[nh:nh + 4], rest[nh + 4:2 * nh + 4]
        lam_ref, car_ref, pw_ref, ssem, rsem = rest[2 * nh + 4:]
        i = pl.program_id(0)

        @pl.when(i == 0)
        def _():
            _scan_init(a_ref, car_ref, pw_ref, True)
            db_ref[...] = jnp.zeros_like(db_ref)
            dc_ref[...] = jnp.zeros_like(dc_ref)
            da_ref[...] = jnp.zeros_like(da_ref)
            for cp in _exchange_copies(x_in, x_out, ssem, rsem):
                cp.start()

        for j in range(8):
            lam_ref[:, _sblk(j)] = _raw_bdot(dy_ref[:, _cblk(j % 4)], c_ref[j], 'nt')
        _scan_tile(lam_ref, lam_ref, car_ref, pw_ref, True)
        for k in range(4):
            du_ref[:, _cblk(k)] = (dud_ref[:, _cblk(k)] + _raw_bdot(lam_ref[:, _sblk(k)], b_ref[k], 'nt')
                                   + _raw_bdot(lam_ref[:, _sblk(4 + k)], b_ref[4 + k], 'nt')
                                   ).astype(du_ref.dtype)
            sr = _shift_down(st_ref[:, _sblk(k)], sp_ref[:, _sblk(k)], nt - 1 - i, 1)
            si = _shift_down(st_ref[:, _sblk(4 + k)], sp_ref[:, _sblk(4 + k)], nt - 1 - i, 1)
            lr, li = lam_ref[:, _sblk(k)], lam_ref[:, _sblk(4 + k)]
            da_ref[:, _sblk(k)] += _sum0(lr * sr + li * si)
            da_ref[:, _sblk(4 + k)] += _sum0(li * sr - lr * si)
        for j in range(8):
            db_ref[j] += _raw_bdot(u_ref[:, _cblk(j % 4)], lam_ref[:, _sblk(j)], 'tn')
            dc_ref[j] += _raw_bdot(st_ref[:, _sblk(j)], dy_ref[:, _cblk(j % 4)], 'tn')

        @pl.when(i == nt - 1)
        def _():
            for cp in _exchange_copies(x_in, x_out, ssem, rsem):
                cp.wait()

    whole = lambda shape: pl.BlockSpec(shape, lambda i: (0,) * len(shape))
    rev = lambda i: (nt - 1 - i, 0)
    outs = pl.pallas_call(
        body, name="s5_bwd", grid=(nt,),
        in_specs=[pl.BlockSpec((S5_T, S5_W), rev), pl.BlockSpec((S5_T, 2 * S5_N), rev),
                  pl.BlockSpec((8, 2 * S5_N), lambda i: (jnp.maximum((nt - 1 - i) * nb8 - 1, 0), 0)),
                  pl.BlockSpec((S5_T, S5_W), rev), pl.BlockSpec((S5_T, S5_W), rev), whole(bmat.shape), whole(cmat.shape),
                  whole(abar.shape)] + [ANY] * nh,
        out_specs=[pl.BlockSpec((S5_T, S5_W), rev), whole((8, _CB, _SB)), whole((8, _SB, _CB)), whole((1, 2 * S5_N))]
        + [ANY] * nh,
        out_shape=[jax.ShapeDtypeStruct((L, S5_W), bf16), jax.ShapeDtypeStruct((8, _CB, _SB), f32),
                   jax.ShapeDtypeStruct((8, _SB, _CB), f32), jax.ShapeDtypeStruct((1, 2 * S5_N), f32)]
        + [jax.ShapeDtypeStruct(chip_sum[nm].shape, chip_sum[nm].dtype) for nm in names],
        scratch_shapes=[pltpu.VMEM((S5_T, 2 * S5_N), f32), pltpu.VMEM((8, 2 * S5_N), f32), pltpu.VMEM((4, 8, 2 * S5_N), f32),
                        pltpu.SemaphoreType.DMA((3 * nh,)), pltpu.SemaphoreType.DMA((3 * nh,))],
        compiler_params=pltpu.CompilerParams(dimension_semantics=("arbitrary",), vmem_limit_bytes=VMEM_LIMIT),
    )(dy, st, st, u, du_direct, bmat, cmat, abar, *[chip_sum[nm] for nm in names])
    return outs[0], outs[1], outs[2], outs[3], dict(zip(names, outs[4:]))


def _s5_disc_fwd(a_re, a_im, ls, b_re, b_im):
    def body(a_re_ref, a_im_ref, ls_ref, b_re_ref, b_im_ref, ar_ref, ai_ref, br_ref, bi_ref):
        outs = _s5_disc(a_re_ref[...], a_im_ref[...], ls_ref[...], b_re_ref[...], b_im_ref[...])
        for ref, v in zip((ar_ref, ai_ref, br_ref, bi_ref), outs):
            ref[...] = v

    c1, c16 = jax.ShapeDtypeStruct((S5_N, 1), f32), jax.ShapeDtypeStruct((S5_N, S5_C), f32)
    return pl.pallas_call(body, name="s5_disc", out_shape=[c1, c1, c16, c16])(a_re, a_im, ls, b_re, b_im)


def _s5_disc_bwd(a_re, a_im, ls, b_re, b_im, d_ar, d_ai, d_br, d_bi, seg):
    def body(a_re_ref, a_im_ref, ls_ref, b_re_ref, b_im_ref, g1, g2, g3, g4, seg_ref, o1, o2, o3, o4, o5):
        _, vjp = jax.vjp(_s5_disc, a_re_ref[...], a_im_ref[...], ls_ref[...], b_re_ref[...], b_im_ref[...])
        da_re, da_im, dls, db_re, db_im = vjp((g1[...], g2[...], g3[...], g4[...]))
        o1[...] = da_re
        o2[...] = da_im
        o3[...] = _dot32(seg_ref[...], dls)
        o4[...] = db_re
        o5[...] = db_im

    c1, c16 = jax.ShapeDtypeStruct((S5_N, 1), f32), jax.ShapeDtypeStruct((S5_N, S5_C), f32)
    return pl.pallas_call(body, name="s5_disc_bwd", out_shape=[c1, c1, jax.ShapeDtypeStruct((S5_G, 1), f32), c16, c16])(
        a_re, a_im, ls, b_re, b_im, d_ar, d_ai, d_br, d_bi, seg)


ANY = pl.BlockSpec(memory_space=pl.ANY)

GATHER = {'w_in': ((4352, 1024), 0), 'ffn_w_up': ((1024, 5632), 1), 'w_branch_rwkv': ((512, 1024), 1),
          'w_branch_s5': ((512, 1024), 1), 'w_out': ((1024, 1024), 0), 's5_w_glu': ((512, 512), 0),
          'ffn_w_down': ((2816, 1024), 0), 'rwkv_w2': ((64, 512), 1), 'rwkv_a2': ((64, 512), 1),
          'rwkv_g2': ((128, 512), 1), 'ffn_conv_w': ((8, 5632), 1)}
BIG = ['w_in', 'ffn_w_up', 'w_branch_rwkv', 'w_branch_s5', 'w_out', 's5_w_glu', 'ffn_w_down']
TINY = ['rwkv_w2', 'rwkv_a2', 'rwkv_g2', 'ffn_conv_w']
SMALL = [n for n in WEIGHTS if n not in GATHER]
SMALL_ROWS = 320
ADAM_ROWS = 256


def _mo(v, m):
    return v if isinstance(v, int) else pl.multiple_of(v, m)


def _slab(ref, name, j, h=None):
    (R, Cn), axis = GATHER[name]
    if axis == 0:
        rs = R // 4
        if h is None:
            return ref.at[pl.ds(_mo(j * rs, 16), rs), :]
        return ref.at[pl.ds(_mo(j * rs + h * (rs // 2), 8), rs // 2), :]
    cols = pl.ds(_mo(j * (Cn // 4), 128), Cn // 4)
    if h is None:
        return ref.at[:, cols]
    return ref.at[pl.ds(_mo(h * (R // 2), 8), R // 2), cols]


def _half_shape(name):
    (R, Cn), axis = GATHER[name]
    return (R // 8, Cn) if axis == 0 else (R // 2, Cn // 4)


def _chip_peers(px, py):
    return [((1 - px) if (k >> 1) else px, (1 - py) if (k & 1) else py) for k in (1, 2, 3)]


def _run_copies(copies):
    for cp in copies:
        cp.start()
    for cp in copies:
        cp.wait()


def _gather_weights(blocks):
    names = list(blocks)
    n = len(names)

    def body(*refs):
        ins, outs = refs[:n], refs[n:2 * n]
        ssem, rsem, lsem = refs[2 * n:]
        px, py, pc = _mesh_pos()
        me = 2 * px + py
        copies = []
        for i, nm in enumerate(names):
            if nm in BIG:
                hr = blocks[nm].shape[0] // 2
                src, dst = ins[i].at[pl.ds(pl.multiple_of(pc * hr, 16), hr), :], _slab(outs[i], nm, me, pc)
            else:
                src, dst = ins[i], _slab(outs[i], nm, me)
            copies.append(pltpu.make_async_copy(src, dst, lsem.at[i]))
            for k, (qx, qy) in enumerate(_chip_peers(px, py)):
                copies.append(pltpu.make_async_remote_copy(src, dst, ssem.at[3 * i + k], rsem.at[3 * i + k],
                                                           device_id=(qx, qy, pc), device_id_type=MESH))
        _run_copies(copies)

    outs = pl.pallas_call(
        body, name="gather_weights", in_specs=[ANY] * n, out_specs=[ANY] * n,
        out_shape=[jax.ShapeDtypeStruct(GATHER[nm][0], blocks[nm].dtype) for nm in names],
        scratch_shapes=[pltpu.SemaphoreType.DMA((3 * n,)), pltpu.SemaphoreType.DMA((3 * n,)), pltpu.SemaphoreType.DMA((n,))],
    )(*[blocks[nm] for nm in names])
    return dict(zip(names, outs))


def _gather_pair(full, names, call_name):
    n = len(names)

    def body(*refs):
        ins, outs = refs[:n], refs[n:2 * n]
        ssem, rsem = refs[2 * n:]
        px, py, pc = _mesh_pos()
        copies = []
        for i, nm in enumerate(names):
            for j in range(4):
                copies.append(pltpu.make_async_remote_copy(_slab(ins[i], nm, j, pc), _slab(outs[i], nm, j, pc),
                                                           ssem.at[4 * i + j], rsem.at[4 * i + j],
                                                           device_id=(px, py, 1 - pc), device_id_type=MESH))
        _run_copies(copies)

    outs = pl.pallas_call(
        body, name=call_name, in_specs=[ANY] * n, out_specs=[ANY] * n,
        out_shape=[jax.ShapeDtypeStruct(full[nm].shape, full[nm].dtype) for nm in names],
        input_output_aliases={i: i for i in range(n)},
        scratch_shapes=[pltpu.SemaphoreType.DMA((4 * n,)), pltpu.SemaphoreType.DMA((4 * n,))],
    )(*[full[nm] for nm in names])
    return dict(zip(names, outs))


def _grads_to_sibling(G, names, call_name, small=None):
    n = len(names)
    ns = 0 if small is None else 1

    def body(*refs):
        g_refs, o_refs = refs[:n + ns], refs[n + ns:2 * (n + ns)]
        ssem, rsem = refs[2 * (n + ns):]
        px, py, pc = _mesh_pos()
        sib = (px, py, 1 - pc)
        copies = []
        for i, nm in enumerate(names):
            for j in range(4):
                copies.append(pltpu.make_async_remote_copy(_slab(g_refs[i], nm, j, 1 - pc), o_refs[i].at[j],
                                                           ssem.at[4 * i + j], rsem.at[4 * i + j],
                                                           device_id=sib, device_id_type=MESH))
        if ns:
            copies.append(pltpu.make_async_remote_copy(g_refs[n], o_refs[n], ssem.at[4 * n], rsem.at[4 * n],
                                                       device_id=sib, device_id_type=MESH))
        _run_copies(copies)

    outs = pl.pallas_call(
        body, name=call_name, in_specs=[ANY] * (n + ns), out_specs=[ANY] * (n + ns),
        out_shape=[jax.ShapeDtypeStruct((4,) + _half_shape(nm), f32) for nm in names]
        + [jax.ShapeDtypeStruct((SMALL_ROWS, PACK_W), f32)] * ns,
        scratch_shapes=[pltpu.SemaphoreType.DMA((4 * n + ns,)), pltpu.SemaphoreType.DMA((4 * n + ns,))],
    )(*[G[nm] for nm in names], *([small] * ns))
    return dict(zip(names, outs[:n])), (outs[n] if ns else None)


def _pair_add(G, recv, names, call_name, small=None, small_recv=None):
    n = len(names)
    ns = 0 if small is None else 1
    cidx = lax.axis_index("c").astype(jnp.int32).reshape(1)

    def body(c_ref, *refs):
        ins, outs = refs[:2 * (n + ns)], refs[2 * (n + ns):]
        for i in range(n):
            outs[i][...] = (ins[i][...] + ins[n + ns + i][...]).astype(bf16)
        if ns:
            outs[n][...] = ins[n][...] + ins[2 * n + 1][...]

    g_specs, r_specs = [], []
    for nm in names:
        hr, hc = _half_shape(nm)
        if GATHER[nm][1] == 0:
            g_specs.append(pl.BlockSpec((hr // 2, hc), lambda j, i, c: ((2 * j + c[0]) * 2 + i, 0)))
        else:
            g_specs.append(pl.BlockSpec((hr // 2, hc), lambda j, i, c: (2 * c[0] + i, j)))
        r_specs.append(pl.BlockSpec((1, hr // 2, hc), lambda j, i, c: (j, i, 0)))
    sm = [pl.BlockSpec((SMALL_ROWS // 8, PACK_W), lambda j, i, c: (2 * j + i, 0))] * ns
    outs = pl.pallas_call(
        body, name=call_name,
        grid_spec=pltpu.PrefetchScalarGridSpec(num_scalar_prefetch=1, grid=(4, 2), in_specs=g_specs + sm + r_specs + sm,
                                               out_specs=r_specs + sm),
        out_shape=[jax.ShapeDtypeStruct((4,) + _half_shape(nm), bf16) for nm in names]
        + [jax.ShapeDtypeStruct((SMALL_ROWS, PACK_W), f32)] * ns,
        compiler_params=pltpu.CompilerParams(vmem_limit_bytes=VMEM_LIMIT),
    )(cidx, *[G[nm] for nm in names], *([small] * ns), *[recv[nm] for nm in names], *([small_recv] * ns))
    return dict(zip(names, outs[:n])), (outs[n] if ns else None)


def _exchange_copies(ins, outs, ssem, rsem):
    px, py, pc = _mesh_pos()
    me = 2 * px + py
    return [pltpu.make_async_remote_copy(ins[i].at[2 * qx + qy], outs[i].at[me], ssem.at[3 * i + k], rsem.at[3 * i + k],
                                         device_id=(qx, qy, pc), device_id_type=MESH)
            for i in range(len(ins)) for k, (qx, qy) in enumerate(_chip_peers(px, py))]


def _grads_chip_exchange(chip_sum, names, small):
    n = len(names)

    def body(*refs):
        ins, outs = refs[:n + 1], refs[n + 1:2 * n + 2]
        ssem, rsem, ssem_s, rsem_s = refs[2 * n + 2:]
        px, py, pc = _mesh_pos()
        me = 2 * px + py
        copies = _exchange_copies(ins[:n], outs[:n], ssem, rsem)
        hs = SMALL_ROWS // 2
        mine = ins[n].at[pl.ds(pl.multiple_of(pc * hs, 8), hs), :]
        copies += [pltpu.make_async_remote_copy(mine, outs[n].at[me], ssem_s.at[k], rsem_s.at[k],
                                                device_id=(qx, qy, pc), device_id_type=MESH)
                   for k, (qx, qy) in enumerate(_chip_peers(px, py))]
        _run_copies(copies)

    outs = pl.pallas_call(
        body, name="grads_chip_exchange", in_specs=[ANY] * (n + 1), out_specs=[ANY] * (n + 1),
        out_shape=[jax.ShapeDtypeStruct(chip_sum[nm].shape, chip_sum[nm].dtype) for nm in names]
        + [jax.ShapeDtypeStruct((4, SMALL_ROWS // 2, PACK_W), f32)],
        scratch_shapes=[pltpu.SemaphoreType.DMA((3 * n,)), pltpu.SemaphoreType.DMA((3 * n,)),
                        pltpu.SemaphoreType.DMA((3,)), pltpu.SemaphoreType.DMA((3,))],
    )(*[chip_sum[nm] for nm in names], small)
    return dict(zip(names, outs[:n])), outs[n]


def _sum_slots(slots, chip_sum, small4, small_own):
    n = len(BIG)
    me = jnp.stack([2 * lax.axis_index("x") + lax.axis_index("y"), lax.axis_index("c")]).astype(jnp.int32)

    def body(me_ref, *refs):
        for i in range(n + 1):
            own = refs[5 * i + 4][...].astype(f32)
            own = own[0] if i < n else own
            term = [jnp.where(me_ref[0] == k, own, refs[5 * i + k][0].astype(f32)) for k in range(4)]
            refs[5 * (n + 1) + i][...] = ((term[0] + term[1]) + term[2]) + term[3]

    redirect = lambda k: (lambda i, m: (jnp.where(m[0] == k, (k + 1) % 4, k), i, 0))
    in_specs, args, specs_out, shapes = [], [], [], []
    for nm in BIG:
        hr, hc = _half_shape(nm)
        in_specs += [pl.BlockSpec((1, hr // 2, hc), redirect(k)) for k in range(4)]
        in_specs.append(pl.BlockSpec((1, hr // 2, hc), lambda i, m: (m[0], i, 0)))
        args += [slots[nm]] * 4 + [chip_sum[nm]]
        specs_out.append(pl.BlockSpec((hr // 2, hc), lambda i, m: (i, 0)))
        shapes.append(jax.ShapeDtypeStruct((hr, hc), f32))
    in_specs += [pl.BlockSpec((1, SMALL_ROWS // 4, PACK_W), redirect(k)) for k in range(4)]
    in_specs.append(pl.BlockSpec((SMALL_ROWS // 4, PACK_W), lambda i, m: (2 * m[1] + i, 0)))
    args += [small4] * 4 + [small_own]
    specs_out.append(pl.BlockSpec((SMALL_ROWS // 4, PACK_W), lambda i, m: (i, 0)))
    shapes.append(jax.ShapeDtypeStruct((SMALL_ROWS // 2, PACK_W), f32))
    outs = pl.pallas_call(
        body, name="grads_chip_sum",
        grid_spec=pltpu.PrefetchScalarGridSpec(num_scalar_prefetch=1, grid=(2,), in_specs=in_specs, out_specs=specs_out),
        out_shape=shapes, compiler_params=pltpu.CompilerParams(vmem_limit_bytes=VMEM_LIMIT),
    )(me, *args)
    return dict(zip(BIG, outs[:n])), outs[n]


def _halves_to_sibling(half):
    names = list(half)
    n = len(names)

    def body(*refs):
        ins, outs = refs[:n], refs[n:2 * n]
        ssem, rsem = refs[2 * n:]
        px, py, pc = _mesh_pos()
        _run_copies([pltpu.make_async_remote_copy(ins[i], outs[i], ssem.at[i], rsem.at[i],
                                                  device_id=(px, py, 1 - pc), device_id_type=MESH) for i in range(n)])

    outs = pl.pallas_call(
        body, name="grads_halves_to_sibling", in_specs=[ANY] * n, out_specs=[ANY] * n,
        out_shape=[jax.ShapeDtypeStruct(half[nm].shape, f32) for nm in names],
        scratch_shapes=[pltpu.SemaphoreType.DMA((n,)), pltpu.SemaphoreType.DMA((n,))],
    )(*[half[nm] for nm in names])
    return dict(zip(names, outs))


def _join_halves(mine, other, pc):
    hr = mine.shape[0]
    return lax.dynamic_slice_in_dim(jnp.concatenate([other, mine, other], axis=0), (1 - pc) * hr, 2 * hr, axis=0)


def _flat_pad(v):
    v = v.reshape(-1)
    return jnp.pad(v, (0, _ceil_to(v.shape[0], PACK_W) - v.shape[0]))


def _pack_rows(parts, rows):
    flat = jnp.concatenate([_flat_pad(p) for p in parts])
    return jnp.pad(flat, (0, rows * PACK_W - flat.shape[0])).reshape(rows, PACK_W)


def _unpack_rows(buf, shapes):
    flat = buf.reshape(-1)
    out, off = [], 0
    for shp in shapes:
        n = 1
        for d in shp:
            n *= d
        out.append(flat[off:off + n].reshape(shp))
        off += _ceil_to(n, PACK_W)
    return out


def _adamw_math(w_, g_, m_, v_):
    m2 = ADAM_B1 * m_ + (1.0 - ADAM_B1) * g_
    v2 = ADAM_B2 * v_ + (1.0 - ADAM_B2) * (g_ * g_)
    m_hat = m2 / (1.0 - ADAM_B1 ** ADAM_STEP)
    v_hat = v2 / (1.0 - ADAM_B2 ** ADAM_STEP)
    return -ADAM_LR * (m_hat / (jnp.sqrt(v_hat) + ADAM_EPS) + ADAM_WD * w_), m2, v2


def _adamw(groups):
    ng = len(groups)

    def body(*refs):
        ins, outs = refs[:4 * ng], refs[4 * ng:]
        for i in range(ng):
            res = _adamw_math(*(r[...] for r in ins[4 * i:4 * i + 4]))
            for ref, val in zip(outs[3 * i:3 * i + 3], res):
                ref[...] = val

    in_specs, out_specs, out_shape = [], [], []
    for grp in groups:
        R, Cn = grp[0].shape
        spec = pl.BlockSpec((R // 8, Cn), lambda i: (i, 0))
        in_specs += [spec] * 4
        out_specs += [spec] * 3
        out_shape += [jax.ShapeDtypeStruct((R, Cn), f32)] * 3
    outs = pl.pallas_call(
        body, name="adamw", grid=(8,), in_specs=in_specs, out_specs=out_specs, out_shape=out_shape,
        compiler_params=pltpu.CompilerParams(vmem_limit_bytes=VMEM_LIMIT),
    )(*[a for grp in groups for a in grp])
    return [tuple(outs[3 * i:3 * i + 3]) for i in range(ng)]


def _forward_backward(x, tgt, W, S, late):
    L = x.shape[0]
    TM, TMW, TS = 256, 128, 512
    row = lambda c, dt=f32: (c, dt)
    hid = jnp.arange(RWKV_W) // HEAD
    E = (hid[:, None] == hid[None, :]).astype(f32)
    seg = (jnp.arange(S5_N)[None, :] // S5_P == jnp.arange(S5_G)[:, None]).astype(f32)

    w_in_t = W['w_in']
    w_p, w_u, w_g = w_in_t[:N_RWKV], w_in_t[N_RWKV:N_RWKV + S5_W], w_in_t[N_RWKV + S5_W:]
    zpad = jnp.zeros((64, RWKV_W), f32)
    w2p = jnp.concatenate([W['rwkv_w2'], zpad], axis=0)
    a2p = jnp.concatenate([zpad, W['rwkv_a2']], axis=0)
    g2 = W['rwkv_g2']
    prep_consts = [S['rwkv_shift_mu'], S['rwkv_w0'], S['rwkv_a0'], S['rwkv_k_k'], S['rwkv_k_a'], w2p, a2p, g2, E]
    out_consts = [S['rwkv_lnx_w'], S['rwkv_lnx_b'], S['rwkv_r_k'], E]
    cw, cb = W['ffn_conv_w'][:3], S['ffn_conv_b']

    a_re, a_im = S['s5_a_re'].reshape(S5_N, 1), S['s5_a_im'].reshape(S5_N, 1)
    ls = jnp.repeat(S['s5_log_step'].reshape(S5_G, 1), S5_P, axis=0)
    b_re, b_im = S['s5_b_re'].reshape(S5_N, S5_C), S['s5_b_im'].reshape(S5_N, S5_C)
    ar, ai, bbr, bbi = _s5_disc_fwd(a_re, a_im, ls, b_re, b_im)
    abar = jnp.concatenate([ar.reshape(1, S5_N), ai.reshape(1, S5_N)], axis=1)
    eye8 = jnp.eye(8, dtype=f32)

    def blocks_in(bb):
        t = bb.reshape(4, 8, S5_P, S5_C).transpose(0, 1, 3, 2)
        return (t[:, :, :, None, :] * eye8[None, :, None, :, None]).reshape(4, _CB, _SB)

    def blocks_out(cc):
        t = cc.reshape(4, 8, S5_C, S5_P).transpose(0, 1, 3, 2)
        return (t[:, :, :, None, :] * eye8[None, :, None, :, None]).reshape(4, _SB, _CB)

    def undiag_in(blocks):
        t = blocks.reshape(4, 8, S5_C, 8, S5_P)
        t = jnp.sum(t * eye8[None, :, None, :, None], axis=3)
        return t.reshape(S5_G, S5_C, S5_P).transpose(0, 2, 1).reshape(S5_N, S5_C)

    def undiag_out(blocks):
        t = blocks.reshape(4, 8, S5_P, 8, S5_C)
        t = jnp.sum(t * eye8[None, :, None, :, None], axis=3)
        return t.reshape(S5_G, S5_P, S5_C).transpose(0, 2, 1)

    bmat = jnp.concatenate([blocks_in(bbr), blocks_in(bbi)], axis=0).astype(bf16)
    cmat = jnp.concatenate([blocks_out(S['s5_c_re'].reshape(S5_G, S5_C, S5_P)),
                            -blocks_out(S['s5_c_im'].reshape(S5_G, S5_C, S5_P))], axis=0).astype(bf16)

    g1, g2n, g3, g4 = S['norm_mix_pre'], S['norm_mix_post'], S['norm_ffn_pre'], S['norm_ffn_post']
    (h1,) = _rowcall("norm_pre", lambda i, n, R, P, X, C: ((_rms(R[0], C[0]),), ()), L, TS, [x], [g1],
                     out_rows=[row(D_MODEL, bf16)])
    p = _mm(h1, w_p, 'nt', "mm_p")
    u = _mm(h1, w_u, 'nt', "mm_u")
    gp = _mm(h1, w_g, 'nt', "mm_g")

    def prep_fn(i, n, R, P, X, C):
        q = R[0] + (_shift_down(R[0], P[0], i, 1) - R[0]) * C[0]
        return _prep(q, *C[1:]), ()

    r, lw, k2, v, an, bv, g = _rowcall("rwkv_prep", prep_fn, L, TS, [p], prep_consts,
                                       out_rows=[row(RWKV_W)] * 7, prev=[0])
    y, ck, xinv = _wkv7_fwd(r, lw, k2, v, an, bv)
    (o_a,) = _rowcall("rwkv_out", lambda i, n, R, P, X, C: ((_rwkv_out(*R, *C),), ()), L, TS, [y, r, k2, v, g],
                      out_consts, out_rows=[row(RWKV_W, bf16)])
    o_r = _mm(o_a, W['w_branch_rwkv'], 'nn', "mm_br")

    st, ysc, got = _s5_fwd(u, bmat, cmat, abar, late)
    W = {**W, **_gather_pair(got, list(got), "gather_weights_pair_late")}
    (yg,) = _rowcall("s5_mid", lambda i, n, R, P, X, C: ((_s5_mid(*R, *C),), ()), L, TS, [ysc, u], [S['s5_d']],
                     out_rows=[row(S5_W)])
    z2 = _mm(yg, W['s5_w_glu'], 'nn', "mm_glu")
    (o_b,) = _rowcall("s5_glu", lambda i, n, R, P, X, C: ((_s5_glu(*R, *C),), ()), L, TS, [yg, z2], [S['s5_b_glu']],
                      out_rows=[row(S5_W, bf16)])
    o_s = _mm(o_b, W['w_branch_s5'], 'nn', "mm_bs")

    (merged,) = _rowcall("merge", lambda i, n, R, P, X, C: ((_merge(*R, *C),), ()), L, TS, [gp, o_r, o_s],
                         [S['b_gate']], out_rows=[row(D_MODEL, bf16)])
    mixed = _mm(merged, W['w_out'], 'nn', "mm_out")

    def resid_fn(i, n, R, P, X, C):
        x1_ = R[0] + _rms(R[1], C[0])
        return (x1_, _rms(x1_, C[1])), ()

    x1, h2 = _rowcall("resid_norm", resid_fn, L, TS, [x, mixed], [g2n, g3], out_rows=[row(D_MODEL), row(D_MODEL, bf16)])

    z = _mm(h2, W['ffn_w_up'], 'nn', "mm_up")

    def conv(zt, zprev, i, cw_, cb_):
        z2s, z1s = _shift_down(zt, zprev, i, 2), _shift_down(zt, zprev, i, 1)
        return cb_ + cw_[0:1] * z2s + cw_[1:2] * z1s + cw_[2:3] * zt, z2s, z1s

    (act,) = _rowcall("conv_act", lambda i, n, R, P, X, C: ((_act(conv(R[0], P[0], i, C[0], C[1])[0]),), ()), L, TMW,
                      [z], [cw, cb], out_rows=[row(D_FF, bf16)], prev=[0])
    f = _mm(act, W['ffn_w_down'], 'nn', "mm_down")

    def final_fn(i, n, R, P, X, C):
        x1_, f_, t_ = R
        fn_, vjp = jax.vjp(_rms, f_, C[0])
        diff = x1_ + fn_ - t_
        loss = jnp.sum(diff * diff) * (0.5 / D_MODEL)
        dx2_ = diff * (1.0 / D_MODEL)
        df_, dg4_ = vjp(dx2_)
        return (df_, dx2_), (jnp.full((1, PACK_W), loss, f32), dg4_)

    df, dx2, loss, dg4 = _rowcall("loss_head", final_fn, L, TS, [x1, f, tgt], [g4],
                                  out_rows=[row(D_MODEL, bf16), row(D_MODEL)], out_accs=[(1, PACK_W), (1, D_MODEL)])
    G = {'norm_ffn_post': dg4}

    dact = _mm(df, W['ffn_w_down'], 'nt', "mm_down_dx")
    G['ffn_w_down'] = _mm(act, df, 'tn', "mm_down_dw")

    def conv_bwd_fn(i, n, R, P, X, C):
        z_, dact_ = R
        cw_, cb_ = C
        zc, z2s, z1s = conv(z_, P[0], i, cw_, cb_)
        _, vjp = jax.vjp(_act, zc)
        (dzc_,) = vjp(dact_)
        last8 = z_[z_.shape[0] - 8:]
        zcn = cb_ + cw_[0:1] * _shift_down(X[0], last8, 1, 2) + cw_[1:2] * _shift_down(X[0], last8, 1, 1) + cw_[2:3] * X[0]
        _, vjpn = jax.vjp(_act, zcn)
        (dzcn,) = vjpn(X[1])
        dz_ = (cw_[2:3] * dzc_ + cw_[1:2] * _shift_up(dzc_, dzcn, i, n, 1) + cw_[0:1] * _shift_up(dzc_, dzcn, i, n, 2))
        return (dz_,), (_sum0(dzc_), _sum0(dzc_ * z2s), _sum0(dzc_ * z1s), _sum0(dzc_ * z_))

    wide = (1, 2 * D_FF)
    dz, dcb, dcw0, dcw1, dcw2 = _rowcall("conv_act_bwd", conv_bwd_fn, L, TMW, [z, dact], [cw, cb],
                                         out_rows=[row(2 * D_FF, bf16)], out_accs=[wide] * 4, prev=[0], nxt=[0, 1])
    G['ffn_conv_b'] = dcb
    G['ffn_conv_w'] = jnp.concatenate([dcw0, dcw1, dcw2], axis=0)
    dh2 = _mm(dz, W['ffn_w_up'], 'nt', "mm_up_dx")
    G['ffn_w_up'] = _mm(h2, dz, 'tn', "mm_up_dw")

    def norm2_bwd_fn(i, n, R, P, X, C):
        x1_, mixed_, dx2_, dh2_ = R
        _, vjp3 = jax.vjp(_rms, x1_, C[1])
        dx1a, dg3_ = vjp3(dh2_)
        dx1_ = dx2_ + dx1a
        _, vjp2 = jax.vjp(_rms, mixed_, C[0])
        dmixed_, dg2_ = vjp2(dx1_)
        return (dx1_, dmixed_), (dg2_, dg3_)

    dx1, dmixed, dg2n, dg3 = _rowcall("norm_mid_bwd", norm2_bwd_fn, L, TS, [x1, mixed, dx2, dh2], [g2n, g3],
                                      out_rows=[row(D_MODEL), row(D_MODEL, bf16)], out_accs=[(1, D_MODEL)] * 2)
    G['norm_mix_post'], G['norm_ffn_pre'] = dg2n, dg3

    dmerged = _mm(dmixed, W['w_out'], 'nt', "mm_out_dx")
    G['w_out'] = _mm(merged, dmixed, 'tn', "mm_out_dw")

    def merge_bwd_fn(i, n, R, P, X, C):
        _, vjp = jax.vjp(_merge, R[0], R[1], R[2], C[0])
        dgp_, do_r_, do_s_, dbg_ = vjp(R[3])
        return (dgp_, do_r_, do_s_), (dbg_,)

    dgp, do_r, do_s, G['b_gate'] = _rowcall("merge_bwd", merge_bwd_fn, L, TS, [gp, o_r, o_s, dmerged], [S['b_gate']],
                                            out_rows=[row(2 * D_MODEL, bf16), row(D_MODEL, bf16), row(D_MODEL, bf16)],
                                            out_accs=[(1, 2 * D_MODEL)])
    do_a = _mm(do_r, W['w_branch_rwkv'], 'nt', "mm_br_dx")
    G['w_branch_rwkv'] = _mm(o_a, do_r, 'tn', "mm_br_dw")
    do_b = _mm(do_s, W['w_branch_s5'], 'nt', "mm_bs_dx")
    G['w_branch_s5'] = _mm(o_b, do_s, 'tn', "mm_bs_dw")

    def glu_bwd_fn(i, n, R, P, X, C):
        _, vjp = jax.vjp(_s5_glu, R[0], R[1], C[0])
        dyg1_, dz2_, dbg_ = vjp(R[2])
        return (dyg1_, dz2_), (dbg_,)

    dyg1, dz2, G['s5_b_glu'] = _rowcall("s5_glu_bwd", glu_bwd_fn, L, TS, [yg, z2, do_b], [S['s5_b_glu']],
                                        out_rows=[row(S5_W), row(S5_W, bf16)], out_accs=[(1, S5_W)])
    dyg2 = _mm(dz2, W['s5_w_glu'], 'nt', "mm_glu_dx")
    G['s5_w_glu'] = _mm(yg, dz2, 'tn', "mm_glu_dw")

    def mid_bwd_fn(i, n, R, P, X, C):
        _, vjp = jax.vjp(_s5_mid, R[0], R[1], C[0])
        dysc_, du_, dd_ = vjp(R[2] + R[3])
        return (dysc_, du_), (dd_,)

    dysc, du1, G['s5_d'] = _rowcall("s5_mid_bwd", mid_bwd_fn, L, TS, [ysc, u, dyg1, dyg2], [S['s5_d']],
                                    out_rows=[row(S5_W, bf16), row(S5_W)], out_accs=[(1, S5_W)])
    early = [n for n in BIG if n != 'w_in']
    recv_e, _ = _grads_to_sibling(G, early, "grads_to_sibling_early")
    chip_e, _ = _pair_add(G, recv_e, early, "grads_pair_sum_early")
    du, dbmat, dcmat, dabar, slots_e = _s5_bwd(dysc, st, u, du1, bmat, cmat, abar, chip_e)
    da_re, da_im, dls, db_re, db_im = _s5_disc_bwd(
        a_re, a_im, ls, b_re, b_im, dabar[:, :S5_N].reshape(S5_N, 1), dabar[:, S5_N:].reshape(S5_N, 1),
        undiag_in(dbmat[:4]), undiag_in(dbmat[4:]), seg)
    G['s5_a_re'], G['s5_a_im'], G['s5_log_step'] = da_re, da_im, dls
    G['s5_b_re'], G['s5_b_im'] = db_re, db_im
    G['s5_c_re'], G['s5_c_im'] = undiag_out(dcmat[:4]), -undiag_out(dcmat[4:])

    def out_bwd_fn(i, n, R, P, X, C):
        _, vjp = jax.vjp(_rwkv_out, *R[:5], *C)
        gs = vjp(R[5])
        return gs[:5], gs[5:8]

    dy, dr1, dk1, dv1, dg, dlw, dlb, drk = _rowcall("rwkv_out_bwd", out_bwd_fn, L, TM, [y, r, k2, v, g, do_a], out_consts,
                                                    out_rows=[row(RWKV_W)] * 5, out_accs=[(1, RWKV_W)] * 3)
    G['rwkv_lnx_w'], G['rwkv_lnx_b'], G['rwkv_r_k'] = dlw, dlb, drk
    dr2, dlwk, dk2b, dv2, dan, dbv = _wkv7_bwd(r, lw, k2, v, an, bv, ck, xinv, dy)

    def prep_bwd_fn(i, n, R, P, X, C):
        p_ = R[0]
        d1 = _shift_down(p_, P[0], i, 1) - p_
        q = p_ + d1 * C[0]
        _, vjp = jax.vjp(_prep, q, *C[1:])
        cots = (R[1] + R[2], R[3], R[4] + R[5], R[6] + R[7], R[8], R[9], R[10])
        gs = vjp(cots)
        return (gs[0],), (_sum0(gs[0] * d1),) + tuple(gs[1:8])

    small, lowr = (1, RWKV_W), (128, RWKV_W)
    dq, dmu, dw0, da0, dkk, dka, dw2p, da2p, dg2 = _rowcall(
        "rwkv_prep_bwd", prep_bwd_fn, L, TM, [p, dr1, dr2, dlwk, dk1, dk2b, dv1, dv2, dan, dbv, dg],
        prep_consts, out_rows=[row(N_RWKV)], out_accs=[(1, N_RWKV)] + [small] * 4 + [lowr] * 3, prev=[0])
    G['rwkv_shift_mu'], G['rwkv_w0'], G['rwkv_a0'], G['rwkv_k_k'], G['rwkv_k_a'] = dmu, dw0, da0, dkk, dka
    G['rwkv_w2'], G['rwkv_a2'], G['rwkv_g2'] = dw2p[:64], da2p[64:], dg2

    def shift_bwd_fn(i, n, R, P, X, C):
        dm = R[0] * C[0]
        return (R[0] - dm + _shift_up(dm, X[0] * C[0], i, n, 1),), ()

    (dp,) = _rowcall("shift_bwd", shift_bwd_fn, L, TS, [dq], [S['rwkv_shift_mu']], out_rows=[row(N_RWKV, bf16)], nxt=[0])

    dproj = jnp.concatenate([dp, du, dgp], axis=1)
    dh1 = _mm(dproj, w_in_t, 'nn', "mm_in_dx")
    G['w_in'] = _mm(dproj, h1, 'tn', "mm_in_dw")

    def norm1_bwd_fn(i, n, R, P, X, C):
        _, vjp = jax.vjp(_rms, R[0], C[0])
        dxa, dg1_ = vjp(R[2])
        return (R[1] + dxa,), (dg1_,)

    dx, G['norm_mix_pre'] = _rowcall("norm_pre_bwd", norm1_bwd_fn, L, TS, [x, dx1, dh1], [g1],
                                     out_rows=[row(D_MODEL)], out_accs=[(1, D_MODEL)])
    return loss, dx, G, chip_e, slots_e


def kernel(x, norm_mix_pre, norm_mix_post, norm_ffn_pre, norm_ffn_post, w_in, b_gate, rwkv_shift_mu, rwkv_w0, rwkv_w2, rwkv_a0, rwkv_a2, rwkv_g2, rwkv_k_k, rwkv_k_a, rwkv_r_k, rwkv_lnx_w, rwkv_lnx_b, s5_a_re, s5_a_im, s5_b_re, s5_b_im, s5_c_re, s5_c_im, s5_d, s5_log_step, s5_w_glu, s5_b_glu, w_branch_rwkv, w_branch_s5, w_out, ffn_w_up, ffn_conv_w, ffn_conv_b, ffn_w_down, loss_target, m_norm_mix_pre, m_norm_mix_post, m_norm_ffn_pre, m_norm_ffn_post, m_w_in, m_b_gate, m_rwkv_shift_mu, m_rwkv_w0, m_rwkv_w2, m_rwkv_a0, m_rwkv_a2, m_rwkv_g2, m_rwkv_k_k, m_rwkv_k_a, m_rwkv_r_k, m_rwkv_lnx_w, m_rwkv_lnx_b, m_s5_a_re, m_s5_a_im, m_s5_b_re, m_s5_b_im, m_s5_c_re, m_s5_c_im, m_s5_d, m_s5_log_step, m_s5_w_glu, m_s5_b_glu, m_w_branch_rwkv, m_w_branch_s5, m_w_out, m_ffn_w_up, m_ffn_conv_w, m_ffn_conv_b, m_ffn_w_down, v_norm_mix_pre, v_norm_mix_post, v_norm_ffn_pre, v_norm_ffn_post, v_w_in, v_b_gate, v_rwkv_shift_mu, v_rwkv_w0, v_rwkv_w2, v_rwkv_a0, v_rwkv_a2, v_rwkv_g2, v_rwkv_k_k, v_rwkv_k_a, v_rwkv_r_k, v_rwkv_lnx_w, v_rwkv_lnx_b, v_s5_a_re, v_s5_a_im, v_s5_b_re, v_s5_b_im, v_s5_c_re, v_s5_c_im, v_s5_d, v_s5_log_step, v_s5_w_glu, v_s5_b_glu, v_w_branch_rwkv, v_w_branch_s5, v_w_out, v_ffn_w_up, v_ffn_conv_w, v_ffn_conv_b, v_ffn_w_down):
    A = dict(locals())
    me = 2 * lax.axis_index("x") + lax.axis_index("y")
    blk = lambda n: A[n][0]

    mine = {n: (blk(n).T if n == 'w_in' else blk(n)).astype(bf16) for n in BIG}
    mine.update({n: blk(n) for n in TINY})
    mine['ffn_conv_w'] = jnp.pad(blk('ffn_conv_w'), ((0, 5), (0, 0)))
    late = ['ffn_w_up', 'ffn_w_down']
    W = _gather_weights({n: blkv for n, blkv in mine.items() if n not in late})
    W.update(_gather_pair(W, [n for n in BIG if n not in late], "gather_weights_pair"))
    S = {n: A[n].reshape(1, -1) for n in SMALL}

    loss, dx, G, chip_e, slots_e = _forward_backward(x[0], loss_target[0], W, S, {n: mine[n] for n in late})

    tiny_shapes = [G[n].shape for n in TINY]
    small_buf = _pack_rows([G[n] for n in SMALL] + [G[n] for n in TINY] + [loss], SMALL_ROWS)
    recv, small_recv = _grads_to_sibling(G, ['w_in'], "grads_to_sibling", small_buf)
    chip_l, small_sum = _pair_add(G, recv, ['w_in'], "grads_pair_sum", small_buf, small_recv)
    slots_l, small4 = _grads_chip_exchange(chip_l, ['w_in'], small_sum)
    half, half['small'] = _sum_slots({**slots_e, **slots_l}, {**chip_e, **chip_l}, small4, small_sum)
    other = _halves_to_sibling(half)
    pc = lax.axis_index("c")
    small_tot = _join_halves(half['small'], other['small'], pc)
    grad = {n: _join_halves(half[n], other[n], pc) for n in BIG}
    grad['w_in'] = grad['w_in'].T
    vals = _unpack_rows(small_tot, [A[n].shape for n in SMALL] + tiny_shapes + [(1, PACK_W)])
    grad.update(zip(SMALL, vals))
    for n, full in zip(TINY, vals[len(SMALL):]):
        cs = A[n].shape[2]
        grad[n] = lax.dynamic_slice_in_dim(full, me * cs, cs, axis=1)
    loss_out = vals[-1][0, 0]

    packed = SMALL + TINY
    groups = [(blk(n), grad[n], blk('m_' + n), blk('v_' + n)) for n in BIG]
    groups.append(tuple(_pack_rows([src(n) for n in packed], ADAM_ROWS)
                        for src in (lambda n: A[n], lambda n: grad[n], lambda n: A['m_' + n], lambda n: A['v_' + n])))
    res = _adamw(groups)
    outs = [dict(), dict(), dict()]
    for n, r3 in zip(BIG, res[:-1]):
        for d, val in zip(outs, r3):
            d[n] = val
    for d, buf in zip(outs, res[-1]):
        d.update(zip(packed, _unpack_rows(buf, [A[n].shape for n in packed])))
    full = lambda d: [d[n].reshape(A[n].shape) for n in WEIGHTS]
    return (loss_out, dx[None], *full(grad), *full(outs[0]), *full(outs[1]), *full(outs[2]))
```

```python
import functools

import jax
import jax.numpy as jnp
from jax import lax
from jax.experimental import pallas as pl
from jax.experimental.pallas import tpu as pltpu

f32, bf16 = jnp.float32, jnp.bfloat16
MESH = pl.DeviceIdType.MESH

D_MODEL = 1024
RWKV_W = 512
HEADS, HEAD = 8, 64
N_RWKV = 1792
S5_W = 512
S5_G, S5_P, S5_C = 32, 64, 16
S5_N = S5_G * S5_P
D_FF = 2816
NORM_EPS = 1e-6
LNX_EPS = 64e-5
ADAM_LR, ADAM_B1, ADAM_B2, ADAM_EPS, ADAM_WD, ADAM_STEP = 0.001, 0.9, 0.999, 1e-08, 0.01, 10

VMEM_LIMIT = 48 * 1024 * 1024
PACK_W = 1024
WKV_C = 64
WKV_SUB = 4
WKV_ROWS = WKV_C * WKV_SUB
RESIDENT_BUDGET = 40 * 1024 * 1024
S5_T = 256

WEIGHTS = ['norm_mix_pre', 'norm_mix_post', 'norm_ffn_pre', 'norm_ffn_post', 'w_in', 'b_gate', 'rwkv_shift_mu',
           'rwkv_w0', 'rwkv_w2', 'rwkv_a0', 'rwkv_a2', 'rwkv_g2', 'rwkv_k_k', 'rwkv_k_a', 'rwkv_r_k', 'rwkv_lnx_w',
           'rwkv_lnx_b', 's5_a_re', 's5_a_im', 's5_b_re', 's5_b_im', 's5_c_re', 's5_c_im', 's5_d', 's5_log_step',
           's5_w_glu', 's5_b_glu', 'w_branch_rwkv', 'w_branch_s5', 'w_out', 'ffn_w_up', 'ffn_conv_w', 'ffn_conv_b',
           'ffn_w_down']


def _ceil_to(n, m):
    return -(-n // m) * m


def _mesh_pos():
    return lax.axis_index("x"), lax.axis_index("y"), lax.axis_index("c")


def _pick(d, cap=4096):
    for c in (1024, 1408, 2176, 896, 512, 256, 128):
        if c <= cap and d % c == 0:
            return c
    raise ValueError(d)


def _mm_resident(a, w, mode, name, M, N, K, out_dtype):
    budget = RESIDENT_BUDGET - 2 * K * N
    tm = next(t for t in (512, 256, 128) if 2 * t * (K * a.dtype.itemsize + 4 * N) <= budget)
    dims = _DIMS[mode]

    def body(a_ref, w_ref, o_ref):
        o_ref[...] = lax.dot_general(a_ref[...].astype(bf16), w_ref[...], (dims, ((), ())),
                                     preferred_element_type=f32).astype(o_ref.dtype)

    return pl.pallas_call(
        body, name=name, grid=(M // tm,),
        in_specs=[pl.BlockSpec((tm, K), lambda i: (i, 0)),
                  pl.BlockSpec(w.shape, lambda i: (0, 0), pipeline_mode=pl.Buffered(1))],
        out_specs=pl.BlockSpec((tm, N), lambda i: (i, 0)), out_shape=jax.ShapeDtypeStruct((M, N), out_dtype),
        compiler_params=pltpu.CompilerParams(dimension_semantics=("parallel",), vmem_limit_bytes=VMEM_LIMIT),
    )(a, w)


def _mm(a, b, mode, name, out_dtype=f32):
    if mode == 'tn':
        (K, M), (K2, N) = a.shape, b.shape
    elif mode == 'nt':
        (M, K), (N, K2) = a.shape, b.shape
    else:
        (M, K), (K2, N) = a.shape, b.shape
    assert K == K2, (name, a.shape, b.shape)
    if mode != 'tn' and b.dtype == bf16:
        return _mm_resident(a, b, mode, name, M, N, K, out_dtype)
    if mode == 'tn':
        tm = _pick(M, 2176)
        tn = _pick(N, 512 if tm > 1408 else (1024 if tm > 1024 else 1408))
        tk = _pick(K, 1024 if a.dtype == bf16 and b.dtype == bf16 else 512)
    else:
        tm, tn, tk = _pick(M, 512), _pick(N), _pick(K)
    nk = K // tk
    dims = {'nn': ((1,), (0,)), 'nt': ((1,), (1,)), 'tn': ((0,), (0,))}[mode]

    def body(a_ref, b_ref, o_ref, acc_ref):
        k = pl.program_id(2)

        @pl.when(k == 0)
        def _():
            acc_ref[...] = jnp.zeros_like(acc_ref)

        acc_ref[...] += lax.dot_general(a_ref[...].astype(bf16), b_ref[...].astype(bf16), (dims, ((), ())),
                                        preferred_element_type=f32)

        @pl.when(k == nk - 1)
        def _():
            o_ref[...] = acc_ref[...].astype(o_ref.dtype)

    a_spec = pl.BlockSpec((tk, tm), lambda i, j, k: (k, i)) if mode == 'tn' else pl.BlockSpec((tm, tk), lambda i, j, k: (i, k))
    b_spec = pl.BlockSpec((tn, tk), lambda i, j, k: (j, k)) if mode == 'nt' else pl.BlockSpec((tk, tn), lambda i, j, k: (k, j))
    return pl.pallas_call(
        body, name=name, grid=(M // tm, N // tn, nk),
        in_specs=[a_spec, b_spec], out_specs=pl.BlockSpec((tm, tn), lambda i, j, k: (i, j)),
        out_shape=jax.ShapeDtypeStruct((M, N), out_dtype),
        scratch_shapes=[pltpu.VMEM((tm, tn), f32)],
        compiler_params=pltpu.CompilerParams(dimension_semantics=("parallel", "parallel", "arbitrary"),
                                             vmem_limit_bytes=VMEM_LIMIT),
    )(a, b)


def _rowcall(name, fn, L, tm, rows, consts=(), out_rows=(), out_accs=(), prev=(), nxt=()):
    nsteps = L // tm
    nb8 = tm // 8
    last8 = L // 8 - 1
    n_r, n_p, n_x, n_c, n_or = len(rows), len(prev), len(nxt), len(consts), len(out_rows)

    def body(*refs):
        i = pl.program_id(0)
        vals = [r[...] for r in refs[:n_r + n_p + n_x + n_c]]
        R, P = vals[:n_r], vals[n_r:n_r + n_p]
        X, C = vals[n_r + n_p:n_r + n_p + n_x], vals[n_r + n_p + n_x:]
        o_refs = refs[n_r + n_p + n_x + n_c:]
        outs_r, outs_a = fn(i, nsteps, R, P, X, C)
        for ref, v in zip(o_refs[:n_or], outs_r, strict=True):
            ref[...] = v.astype(ref.dtype)
        if out_accs:
            @pl.when(i == 0)
            def _():
                for ref in o_refs[n_or:]:
                    ref[...] = jnp.zeros_like(ref)

            for ref, v in zip(o_refs[n_or:], outs_a, strict=True):
                ref[...] += v

    def const_spec(c):
        nd = c.ndim
        return pl.BlockSpec(c.shape, lambda i: (0,) * nd)

    in_specs = ([pl.BlockSpec((tm, a.shape[1]), lambda i: (i, 0)) for a in rows]
                + [pl.BlockSpec((8, rows[j].shape[1]), lambda i: (jnp.maximum(i * nb8 - 1, 0), 0)) for j in prev]
                + [pl.BlockSpec((8, rows[j].shape[1]), lambda i: (jnp.minimum((i + 1) * nb8, last8), 0)) for j in nxt]
                + [const_spec(c) for c in consts])
    out_specs = ([pl.BlockSpec((tm, c), lambda i: (i, 0)) for c, _ in out_rows]
                 + [pl.BlockSpec(s, lambda i: (0, 0)) for s in out_accs])
    out_shape = ([jax.ShapeDtypeStruct((L, c), dt) for c, dt in out_rows]
                 + [jax.ShapeDtypeStruct(s, f32) for s in out_accs])
    args = list(rows) + [rows[j] for j in prev] + [rows[j] for j in nxt] + list(consts)
    return pl.pallas_call(
        body, name=name, grid=(nsteps,), in_specs=in_specs, out_specs=out_specs, out_shape=out_shape,
        compiler_params=pltpu.CompilerParams(dimension_semantics=("arbitrary",), vmem_limit_bytes=VMEM_LIMIT),
    )(*args)


def _rows_mm(name, fn, L, tm, rows, consts, w, mode, splits):
    n_r, n_c = len(rows), len(consts)
    K = w.shape[0] if mode == 'nn' else w.shape[1]

    def body(*refs):
        R, C = [r[...] for r in refs[:n_r]], [r[...] for r in refs[n_r:n_r + n_c]]
        w_ref, lhs_ref, outs = refs[n_r + n_c], refs[n_r + n_c + 1], refs[n_r + n_c + 2:]
        lhs = fn(R, C).astype(bf16)
        lhs_ref[...] = lhs
        off = 0
        for ref, wd in zip(outs, splits):
            wk = w_ref[:, off:off + wd] if mode == 'nn' else w_ref[off:off + wd, :]
            ref[...] = lax.dot_general(lhs, wk, (_DIMS[mode], ((), ())), preferred_element_type=f32)
            off += wd

    def const_spec(c):
        nd = c.ndim
        return pl.BlockSpec(c.shape, lambda i: (0,) * nd)

    return pl.pallas_call(
        body, name=name, grid=(L // tm,),
        in_specs=[pl.BlockSpec((tm, a.shape[1]), lambda i: (i, 0)) for a in rows] + [const_spec(c) for c in consts]
        + [pl.BlockSpec(w.shape, lambda i: (0, 0), pipeline_mode=pl.Buffered(1))],
        out_specs=[pl.BlockSpec((tm, K), lambda i: (i, 0))] + [pl.BlockSpec((tm, wd), lambda i: (i, 0)) for wd in splits],
        out_shape=[jax.ShapeDtypeStruct((L, K), bf16)] + [jax.ShapeDtypeStruct((L, wd), f32) for wd in splits],
        compiler_params=pltpu.CompilerParams(dimension_semantics=("parallel",), vmem_limit_bytes=VMEM_LIMIT),
    )(*rows, *consts, w)


def _shift_down(x, prev8, i, k):
    rolled = pltpu.roll(x, k, axis=0)
    pfix = jnp.where(i > 0, pltpu.roll(prev8, k, axis=0), 0.0)
    row8 = lax.broadcasted_iota(jnp.int32, pfix.shape, 0)
    top = jnp.where(row8 < k, pfix, rolled[:8])
    return top if x.shape[0] == 8 else jnp.concatenate([top, rolled[8:]], axis=0)


def _shift_up(x, next8, i, nsteps, k):
    tm = x.shape[0]
    rolled = pltpu.roll(x, tm - k, axis=0)
    nfix = jnp.where(i < nsteps - 1, pltpu.roll(next8, 8 - k, axis=0), 0.0)
    row8 = lax.broadcasted_iota(jnp.int32, nfix.shape, 0)
    bot = jnp.where(row8 >= 8 - k, nfix, rolled[tm - 8:])
    return jnp.concatenate([rolled[:tm - 8], bot], axis=0)


def _sum0(x):
    return jnp.sum(x, axis=0, keepdims=True)


def _rms(x, g):
    return x * lax.rsqrt(jnp.mean(x * x, axis=-1, keepdims=True) + NORM_EPS) * g


def _softplus(x):
    return jnp.maximum(x, 0.0) + jnp.log(1.0 + jnp.exp(-jnp.abs(x)))


def _gelu(x):
    return 0.5 * x * (1.0 + jnp.tanh(0.7978845608028654 * (x + 0.044715 * x * x * x)))


def _dot32(a, b):
    return jnp.dot(a, b, preferred_element_type=f32, precision=lax.Precision.HIGHEST)


def _seg_raw(x, E):
    hi = x.astype(bf16)
    r1 = x - hi.astype(f32)
    mid = r1.astype(bf16)
    lo = (r1 - mid.astype(f32)).astype(bf16)
    Eb = E.astype(bf16)
    dot = lambda t: jnp.dot(t, Eb, preferred_element_type=f32)
    return (dot(lo) + dot(mid)) + dot(hi)


@jax.custom_vjp
def _seg(x, E):
    return _seg_raw(x, E)


_seg.defvjp(lambda x, E: (_seg_raw(x, E), E), lambda E, g: (_seg_raw(g, E), jnp.zeros_like(E)))


def _prep(q, w0, a0, k_k, k_a, w2p, a2p, g2, E):
    r, k, v = q[:, 0:512], q[:, 512:1024], q[:, 1024:1536]
    wa, gd = q[:, 1536:1664], q[:, 1664:1792]
    wlog = -_softplus(-(w0 + _bdot(jnp.tanh(wa), w2p, 'nn'))) - 0.5
    lw = -jnp.exp(wlog)
    a = jax.nn.sigmoid(a0 + _bdot(wa, a2p, 'nn'))
    g = _bdot(jax.nn.sigmoid(gd), g2, 'nn')
    kk = k * k_k
    kkn = kk / jnp.maximum(jnp.sqrt(_seg(kk * kk, E)), 1e-12)
    k2 = k * (1.0 + (a - 1.0) * k_a)
    return r, lw, k2, v, -kkn, kkn * a, g


def _rwkv_out(y, r, k2, v, g, lnx_w, lnx_b, r_k, E):
    mean = _seg(y, E) * (1.0 / HEAD)
    yc = y - mean
    var = _seg(yc * yc, E) * (1.0 / HEAD)
    yn = yc * lax.rsqrt(var + LNX_EPS) * lnx_w + lnx_b
    bonus = _seg(r * k2 * r_k, E) * v
    return (yn + bonus) * g


def _s5_mid(ysc, u, d):
    return _gelu(ysc + d * u)


def _s5_glu(yg, z2, b_glu):
    return yg * jax.nn.sigmoid(z2 + b_glu)


def _merge(gp, o_r, o_s, b_gate):
    gates = jax.nn.sigmoid(gp + b_gate)
    return gates[:, :D_MODEL] * o_r + gates[:, D_MODEL:] * o_s


def _act(zc):
    return _gelu(zc[:, :D_FF]) * zc[:, D_FF:]


def _s5_disc(a_re, a_im, ls, b_re, b_im):
    dt = jnp.exp(ls)
    er = jnp.exp(a_re * dt)
    ar, ai = er * jnp.cos(a_im * dt), er * jnp.sin(a_im * dt)
    x, y = ar - 1.0, ai
    den = a_re * a_re + a_im * a_im
    fr, fi = (x * a_re + y * a_im) / den, (y * a_re - x * a_im) / den
    return ar, ai, fr * b_re - fi * b_im, fr * b_im + fi * b_re


_DIMS = {'nn': ((1,), (0,)), 'nt': ((1,), (1,)), 'tn': ((0,), (0,))}


def _raw_bdot(a, b, mode):
    return lax.dot_general(a.astype(bf16), b.astype(bf16), (_DIMS[mode], ((), ())), preferred_element_type=f32)


@functools.partial(jax.custom_vjp, nondiff_argnums=(2,))
def _bdot(a, b, mode):
    return _raw_bdot(a, b, mode)


def _bdot_fwd(a, b, mode):
    return _raw_bdot(a, b, mode), (a, b)


def _bdot_bwd(mode, res, g):
    a, b = res
    if mode == 'nn':
        return _raw_bdot(g, b, 'nt'), _raw_bdot(a, g, 'tn')
    if mode == 'nt':
        return _raw_bdot(g, b, 'nn'), _raw_bdot(g, a, 'tn')
    return _raw_bdot(b, g, 'nt'), _raw_bdot(a, g, 'nn')


_bdot.defvjp(_bdot_fwd, _bdot_bwd)


def _tri_inv_raw(A):
    n = A[0].shape[0]
    eye = (lax.broadcasted_iota(jnp.int32, (n, n), 0) == lax.broadcasted_iota(jnp.int32, (n, n), 1)).astype(f32)
    x = [eye + a for a in A]
    pw, m = A, 1
    while 2 * m < n // 2:
        pw = [_raw_bdot(p, p, 'nn') for p in pw]
        x = [xi + _raw_bdot(xi, p, 'nn') for xi, p in zip(x, pw)]
        m *= 2
    return x


@jax.custom_vjp
def _tri_inv(A):
    return _tri_inv_raw(A)


def _tri_inv_fwd(A):
    x = _tri_inv_raw(A)
    return x, x


def _tri_inv_bwd(x, g):
    return ([_raw_bdot(_raw_bdot(xi, gi, 'tn'), xi, 'nt') for xi, gi in zip(x, g)],)


_tri_inv.defvjp(_tri_inv_fwd, _tri_inv_bwd)


@jax.custom_vjp
def _inv_given(A, X):
    return X


_inv_given.defvjp(lambda A, X: (X, X),
                  lambda x, g: (_tri_inv_bwd(x, g)[0], [jnp.zeros_like(xi) for xi in x]))


def _wkv_chunk(S0, r, lw, k, v, a, b, tri, bd, xinv=None):
    C = r[0].shape[0]
    P = range(len(r))
    lane = lax.broadcasted_iota(jnp.int32, (1, 2 * HEAD), 1)
    m0, m1 = (lane < HEAD).astype(f32), (lane >= HEAD).astype(f32)
    cat = lambda *xs: jnp.concatenate(xs, axis=0)
    stack = lambda x: cat(x * m0, x * m1)
    unstack = lambda x2: m0 * x2[:C] + m1 * x2[C:]
    rid = lax.broadcasted_iota(jnp.int32, (2 * C, 2 * C), 0)
    cid = lax.broadcasted_iota(jnp.int32, (2 * C, 2 * C), 1)
    same = (rid < C) == (cid < C)
    eye2 = (rid == cid).astype(f32)
    tri2 = (same & (rid >= cid)).astype(f32)
    sl2 = tri2 - eye2
    cum = [_dot32(tri, lw[p]) for p in P]
    g = [jnp.exp(cum[p]) for p in P]
    gi = [jnp.exp(-cum[p]) for p in P]
    at = [a[p] * jnp.exp(cum[p] - lw[p]) for p in P]
    rt = [r[p] * g[p] for p in P]
    kb = [k[p] * gi[p] for p in P]
    bb = [b[p] * gi[p] for p in P]
    lhs = [cat(stack(at[p]), stack(rt[p])) for p in P]
    pb = [_bdot(lhs[p], stack(bb[p]), 'nt') for p in P]
    pk = [_bdot(lhs[p], stack(kb[p]), 'nt') for p in P]
    aab = [pb[p][:2 * C] * sl2 for p in P]
    base = [_bdot(cat(at[p], rt[p]), S0[p], 'nt') for p in P]
    t = [_bdot(cat(pk[p][:2 * C] * sl2, pk[p][2 * C:] * tri2), cat(v[p], v[p]), 'nn') for p in P]
    rhs = [cat(base[p][:C], base[p][:C]) + t[p][:2 * C] for p in P]
    x = _tri_inv(aab) if xinv is None else _inv_given(aab, xinv)
    u = [unstack(_bdot(x[p], rhs[p], 'nn')) for p in P]
    w2 = [_bdot(pb[p][2 * C:] * tri2, cat(u[p], u[p]), 'nn') for p in P]
    y = [base[p][C:] + unstack(t[p][2 * C:]) + unstack(w2[p]) for p in P]
    S1 = [g[p][C - 1:C, :] * (S0[p] + bd * _bdot(cat(v[p], u[p]), cat(kb[p], bb[p]), 'tn')) for p in P]
    return y, S1, x


def _pairs(x):
    return [x[:, 2 * HEAD * p:2 * HEAD * (p + 1)] for p in range(HEADS // 2)]


def _wkv_consts():
    tri = jnp.tril(jnp.ones((WKV_C, WKV_C), f32))
    hid = jnp.arange(2 * HEAD) // HEAD
    return tri, (hid[:, None] == hid[None, :]).astype(f32)


def _wkv_step(S0, r, lw, k, v, a, b, tri, bd, xinv=None):
    ys, xs, S = [], [], S0
    for c in range(WKV_SUB):
        sub = lambda t: [x[c * WKV_C:(c + 1) * WKV_C] for x in t]
        y, S, x = _wkv_chunk(S, sub(r), sub(lw), sub(k), sub(v), sub(a), sub(b), tri, bd, None if xinv is None else xinv[c])
        ys.append(y)
        xs.append(x)
    return [jnp.concatenate([y[p] for y in ys], axis=0) for p in range(len(S0))], S, xs


def _wkv7_fwd(r, lw, k, v, a, b):
    L = r.shape[0]
    nc, npair = L // WKV_ROWS, HEADS // 2

    def body(r_ref, lw_ref, k_ref, v_ref, a_ref, b_ref, tri_ref, bd_ref, y_ref, ck_ref, xi_ref, s_ref):
        @pl.when(pl.program_id(0) == 0)
        def _():
            s_ref[...] = jnp.zeros_like(s_ref)

        s0 = [s_ref[p] for p in range(npair)]
        for p in range(npair):
            ck_ref[0, p] = s0[p]
        y, s1, xs = _wkv_step(s0, *(_pairs(x) for x in (r_ref, lw_ref, k_ref, v_ref, a_ref, b_ref)), tri_ref[...], bd_ref[...])
        for p in range(npair):
            y_ref[:, 2 * HEAD * p:2 * HEAD * (p + 1)] = y[p]
            s_ref[p] = s1[p]
            for c in range(WKV_SUB):
                xi_ref[0, c, p] = xs[c][p].astype(xi_ref.dtype)

    row = pl.BlockSpec((WKV_ROWS, RWKV_W), lambda c: (c, 0))
    sspec = pl.BlockSpec((1, npair, 2 * HEAD, 2 * HEAD), lambda c: (c, 0, 0, 0))
    xspec = pl.BlockSpec((1, WKV_SUB, npair, 2 * HEAD, 2 * HEAD), lambda c: (c, 0, 0, 0, 0))
    return pl.pallas_call(
        body, name="wkv7_fwd", grid=(nc,),
        in_specs=[row] * 6 + [pl.BlockSpec((WKV_C, WKV_C), lambda c: (0, 0)), pl.BlockSpec((2 * HEAD, 2 * HEAD), lambda c: (0, 0))],
        out_specs=[row, sspec, xspec],
        out_shape=[jax.ShapeDtypeStruct((L, RWKV_W), f32), jax.ShapeDtypeStruct((nc, npair, 2 * HEAD, 2 * HEAD), f32),
                   jax.ShapeDtypeStruct((nc, WKV_SUB, npair, 2 * HEAD, 2 * HEAD), bf16)],
        scratch_shapes=[pltpu.VMEM((npair, 2 * HEAD, 2 * HEAD), f32)],
        compiler_params=pltpu.CompilerParams(dimension_semantics=("arbitrary",), vmem_limit_bytes=VMEM_LIMIT),
    )(r, lw, k, v, a, b, *_wkv_consts())


def _wkv7_bwd(r, lw, k, v, a, b, ck, xinv, dy):
    L = r.shape[0]
    nc, npair = L // WKV_ROWS, HEADS // 2

    def body(r_ref, lw_ref, k_ref, v_ref, a_ref, b_ref, ck_ref, xi_ref, dy_ref, tri_ref, bd_ref,
             dr_ref, dlw_ref, dk_ref, dv_ref, da_ref, db_ref, ds_ref):
        @pl.when(pl.program_id(0) == 0)
        def _():
            ds_ref[...] = jnp.zeros_like(ds_ref)

        tri, bd = tri_ref[...], bd_ref[...]
        ins = [[ck_ref[0, p] for p in range(npair)]] + [_pairs(x) for x in (r_ref, lw_ref, k_ref, v_ref, a_ref, b_ref)]
        xs = [[xi_ref[0, c, p].astype(f32) for p in range(npair)] for c in range(WKV_SUB)]
        _, vjp = jax.vjp(lambda *t: _wkv_step(*t, tri, bd, xs)[:2], *ins)
        gs = vjp((_pairs(dy_ref), [ds_ref[p] for p in range(npair)]))
        for p in range(npair):
            ds_ref[p] = gs[0][p]
            for ref, gval in zip((dr_ref, dlw_ref, dk_ref, dv_ref, da_ref, db_ref), gs[1:]):
                ref[:, 2 * HEAD * p:2 * HEAD * (p + 1)] = gval[p]

    row = pl.BlockSpec((WKV_ROWS, RWKV_W), lambda c: (nc - 1 - c, 0))
    sspec = pl.BlockSpec((1, npair, 2 * HEAD, 2 * HEAD), lambda c: (nc - 1 - c, 0, 0, 0))
    xspec = pl.BlockSpec((1, WKV_SUB, npair, 2 * HEAD, 2 * HEAD), lambda c: (nc - 1 - c, 0, 0, 0, 0))
    return pl.pallas_call(
        body, name="wkv7_bwd", grid=(nc,),
        in_specs=[row] * 6 + [sspec, xspec, row, pl.BlockSpec((WKV_C, WKV_C), lambda c: (0, 0)),
                              pl.BlockSpec((2 * HEAD, 2 * HEAD), lambda c: (0, 0))],
        out_specs=[row] * 6,
        out_shape=[jax.ShapeDtypeStruct((L, RWKV_W), f32)] * 6,
        scratch_shapes=[pltpu.VMEM((npair, 2 * HEAD, 2 * HEAD), f32)],
        compiler_params=pltpu.CompilerParams(dimension_semantics=("arbitrary",), vmem_limit_bytes=VMEM_LIMIT),
    )(r, lw, k, v, a, b, ck, xinv, dy, *_wkv_consts())


def _cmul(ar, ai, xr, xi):
    return ar * xr - ai * xi, ar * xi + ai * xr


def _scan_init(a_ref, car_ref, pw_ref, reverse):
    car_ref[...] = jnp.zeros_like(car_ref)
    ar = jnp.broadcast_to(a_ref[:, :S5_N], (8, S5_N))
    ai = jnp.broadcast_to(a_ref[:, S5_N:], (8, S5_N))
    if reverse:
        ai = -ai
    row = lax.broadcasted_iota(jnp.int32, (8, S5_N), 0)
    pr, pi = ar, ai
    qr, qi = jnp.zeros((8, S5_N), f32), jnp.zeros((8, S5_N), f32)
    for e in range(1, 9):
        sel = (row == 8 - e) if reverse else (row == e - 1)
        qr, qi = jnp.where(sel, pr, qr), jnp.where(sel, pi, qi)
        if e in (1, 2, 4):
            j = (1, 2, 4).index(e)
            pw_ref[j, :, :S5_N] = pr
            pw_ref[j, :, S5_N:] = pi
        pr, pi = _cmul(pr, pi, ar, ai)
    pw_ref[3, :, :S5_N] = qr
    pw_ref[3, :, S5_N:] = qi


def _scan_tile(x_ref, o_ref, car_ref, pw_ref, reverse):
    ng = x_ref.shape[0] // 8
    row = lax.broadcasted_iota(jnp.int32, (8, S5_N), 0)

    def group(gi, carry):
        g = (ng - 1 - gi) if reverse else gi
        t0 = pl.multiple_of(g * 8, 8)
        xr, xi = x_ref[pl.ds(t0, 8), :S5_N], x_ref[pl.ds(t0, 8), S5_N:]
        for j, d in enumerate((1, 2, 4)):
            if reverse:
                sr = jnp.where(row < 8 - d, pltpu.roll(xr, 8 - d, axis=0), 0.0)
                si = jnp.where(row < 8 - d, pltpu.roll(xi, 8 - d, axis=0), 0.0)
            else:
                sr = jnp.where(row >= d, pltpu.roll(xr, d, axis=0), 0.0)
                si = jnp.where(row >= d, pltpu.roll(xi, d, axis=0), 0.0)
            mr, mi = _cmul(pw_ref[j, :, :S5_N], pw_ref[j, :, S5_N:], sr, si)
            xr, xi = xr + mr, xi + mi
        cr, ci = carry
        mr, mi = _cmul(pw_ref[3, :, :S5_N], pw_ref[3, :, S5_N:], cr, ci)
        xr, xi = xr + mr, xi + mi
        o_ref[pl.ds(t0, 8), :S5_N] = xr
        o_ref[pl.ds(t0, 8), S5_N:] = xi
        e = 0 if reverse else 7
        return (jnp.broadcast_to(xr[e:e + 1, :], (8, S5_N)), jnp.broadcast_to(xi[e:e + 1, :], (8, S5_N)))

    cr, ci = lax.fori_loop(0, ng, group, (car_ref[:, :S5_N], car_ref[:, S5_N:]))
    car_ref[:, :S5_N] = cr
    car_ref[:, S5_N:] = ci


_CB, _SB = 128, 512


def _cblk(k):
    return slice(_CB * k, _CB * (k + 1))


def _sblk(j):
    return slice(_SB * j, _SB * (j + 1))


def _s5_fwd(u, bmat, cmat, abar, late):
    L = u.shape[0]
    nt = L // S5_T
    names = list(late)
    nh = len(names)

    def body(u_ref, b_ref, c_ref, a_ref, *rest):
        h_in, (st_ref, y_ref), h_out = rest[:nh], rest[nh:nh + 2], rest[nh + 2:2 * nh + 2]
        bu_ref, car_ref, pw_ref, ssem, rsem, lsem = rest[2 * nh + 2:]
        i = pl.program_id(0)

        def copies():
            px, py, pc = _mesh_pos()
            me = 2 * px + py
            out = []
            for a, nm in enumerate(names):
                hr = late[nm].shape[0] // 2
                src, dst = h_in[a].at[pl.ds(pl.multiple_of(pc * hr, 16), hr), :], _slab(h_out[a], nm, me, pc)
                out.append(pltpu.make_async_copy(src, dst, lsem.at[a]))
                out += [pltpu.make_async_remote_copy(src, dst, ssem.at[3 * a + k], rsem.at[3 * a + k],
                                                     device_id=(qx, qy, pc), device_id_type=MESH)
                        for k, (qx, qy) in enumerate(_chip_peers(px, py))]
            return out

        @pl.when(i == 0)
        def _():
            _scan_init(a_ref, car_ref, pw_ref, False)
            for cp in copies():
                cp.start()

        for j in range(8):
            bu_ref[:, _sblk(j)] = _raw_bdot(u_ref[:, _cblk(j % 4)], b_ref[j], 'nn')
        _scan_tile(bu_ref, st_ref, car_ref, pw_ref, False)
        for k in range(4):
            y_ref[:, _cblk(k)] = (_raw_bdot(st_ref[:, _sblk(k)], c_ref[k], 'nn')
                                  + _raw_bdot(st_ref[:, _sblk(4 + k)], c_ref[4 + k], 'nn'))

        @pl.when(i == nt - 1)
        def _():
            for cp in copies():
                cp.wait()

    whole = lambda shape: pl.BlockSpec(shape, lambda i: (0,) * len(shape))
    outs = pl.pallas_call(
        body, name="s5_fwd", grid=(nt,),
        in_specs=[pl.BlockSpec((S5_T, S5_W), lambda i: (i, 0)), whole(bmat.shape), whole(cmat.shape), whole(abar.shape)]
        + [ANY] * nh,
        out_specs=[pl.BlockSpec((S5_T, 2 * S5_N), lambda i: (i, 0)), pl.BlockSpec((S5_T, S5_W), lambda i: (i, 0))] + [ANY] * nh,
        out_shape=[jax.ShapeDtypeStruct((L, 2 * S5_N), f32), jax.ShapeDtypeStruct((L, S5_W), f32)]
        + [jax.ShapeDtypeStruct(GATHER[nm][0], late[nm].dtype) for nm in names],
        scratch_shapes=[pltpu.VMEM((S5_T, 2 * S5_N), f32), pltpu.VMEM((8, 2 * S5_N), f32), pltpu.VMEM((4, 8, 2 * S5_N), f32),
                        pltpu.SemaphoreType.DMA((3 * nh,)), pltpu.SemaphoreType.DMA((3 * nh,)), pltpu.SemaphoreType.DMA((nh,))],
        compiler_params=pltpu.CompilerParams(dimension_semantics=("arbitrary",), vmem_limit_bytes=VMEM_LIMIT),
    )(u, bmat, cmat, abar, *[late[nm] for nm in names])
    return outs[0], outs[1], dict(zip(names, outs[2:]))


def _s5_bwd(dy, st, u, du_direct, bmat, cmat, abar, chip_sum):
    L = u.shape[0]
    nt = L // S5_T
    nb8 = S5_T // 8
    names = list(chip_sum)
    nh = len(names)

    def body(dy_ref, st_ref, sp_ref, u_ref, dud_ref, b_ref, c_ref, a_ref, *rest):
        x_in, (du_ref, db_ref, dc_ref, da_ref), x_out = rest[:nh], rest[nh:nh + 4], rest[nh + 4:2 * nh + 4]
        lam_ref, car_ref, pw_ref, ssem, rsem = rest[2 * nh + 4:]
        i = pl.program_id(0)

        @pl.when(i == 0)
        def _():
            _scan_init(a_ref, car_ref, pw_ref, True)
            db_ref[...] = jnp.zeros_like(db_ref)
            dc_ref[...] = jnp.zeros_like(dc_ref)
            da_ref[...] = jnp.zeros_like(da_ref)
            for cp in _exchange_copies(x_in, x_out, ssem, rsem):
                cp.start()

        for j in range(8):
            lam_ref[:, _sblk(j)] = _raw_bdot(dy_ref[:, _cblk(j % 4)], c_ref[j], 'nt')
        _scan_tile(lam_ref, lam_ref, car_ref, pw_ref, True)
        for k in range(4):
            du_ref[:, _cblk(k)] = (dud_ref[:, _cblk(k)] + _raw_bdot(lam_ref[:, _sblk(k)], b_ref[k], 'nt')
                                   + _raw_bdot(lam_ref[:, _sblk(4 + k)], b_ref[4 + k], 'nt')
                                   ).astype(du_ref.dtype)
            sr = _shift_down(st_ref[:, _sblk(k)], sp_ref[:, _sblk(k)], nt - 1 - i, 1)
            si = _shift_down(st_ref[:, _sblk(4 + k)], sp_ref[:, _sblk(4 + k)], nt - 1 - i, 1)
            lr, li = lam_ref[:, _sblk(k)], lam_ref[:, _sblk(4 + k)]
            da_ref[:, _sblk(k)] += _sum0(lr * sr + li * si)
            da_ref[:, _sblk(4 + k)] += _sum0(li * sr - lr * si)
        for j in range(8):
            db_ref[j] += _raw_bdot(u_ref[:, _cblk(j % 4)], lam_ref[:, _sblk(j)], 'tn')
            dc_ref[j] += _raw_bdot(st_ref[:, _sblk(j)], dy_ref[:, _cblk(j % 4)], 'tn')

        @pl.when(i == nt - 1)
        def _():
            for cp in _exchange_copies(x_in, x_out, ssem, rsem):
                cp.wait()

    whole = lambda shape: pl.BlockSpec(shape, lambda i: (0,) * len(shape))
    rev = lambda i: (nt - 1 - i, 0)
    outs = pl.pallas_call(
        body, name="s5_bwd", grid=(nt,),
        in_specs=[pl.BlockSpec((S5_T, S5_W), rev), pl.BlockSpec((S5_T, 2 * S5_N), rev),
                  pl.BlockSpec((8, 2 * S5_N), lambda i: (jnp.maximum((nt - 1 - i) * nb8 - 1, 0), 0)),
                  pl.BlockSpec((S5_T, S5_W), rev), pl.BlockSpec((S5_T, S5_W), rev), whole(bmat.shape), whole(cmat.shape),
                  whole(abar.shape)] + [ANY] * nh,
        out_specs=[pl.BlockSpec((S5_T, S5_W), rev), whole((8, _CB, _SB)), whole((8, _SB, _CB)), whole((1, 2 * S5_N))]
        + [ANY] * nh,
        out_shape=[jax.ShapeDtypeStruct((L, S5_W), bf16), jax.ShapeDtypeStruct((8, _CB, _SB), f32),
                   jax.ShapeDtypeStruct((8, _SB, _CB), f32), jax.ShapeDtypeStruct((1, 2 * S5_N), f32)]
        + [jax.ShapeDtypeStruct(chip_sum[nm].shape, chip_sum[nm].dtype) for nm in names],
        scratch_shapes=[pltpu.VMEM((S5_T, 2 * S5_N), f32), pltpu.VMEM((8, 2 * S5_N), f32), pltpu.VMEM((4, 8, 2 * S5_N), f32),
                        pltpu.SemaphoreType.DMA((3 * nh,)), pltpu.SemaphoreType.DMA((3 * nh,))],
        compiler_params=pltpu.CompilerParams(dimension_semantics=("arbitrary",), vmem_limit_bytes=VMEM_LIMIT),
    )(dy, st, st, u, du_direct, bmat, cmat, abar, *[chip_sum[nm] for nm in names])
    return outs[0], outs[1], outs[2], outs[3], dict(zip(names, outs[4:]))


def _s5_disc_fwd(a_re, a_im, ls, b_re, b_im):
    def body(a_re_ref, a_im_ref, ls_ref, b_re_ref, b_im_ref, ar_ref, ai_ref, br_ref, bi_ref):
        outs = _s5_disc(a_re_ref[...], a_im_ref[...], ls_ref[...], b_re_ref[...], b_im_ref[...])
        for ref, v in zip((ar_ref, ai_ref, br_ref, bi_ref), outs):
            ref[...] = v

    c1, c16 = jax.ShapeDtypeStruct((S5_N, 1), f32), jax.ShapeDtypeStruct((S5_N, S5_C), f32)
    return pl.pallas_call(body, name="s5_disc", out_shape=[c1, c1, c16, c16])(a_re, a_im, ls, b_re, b_im)


def _s5_disc_bwd(a_re, a_im, ls, b_re, b_im, d_ar, d_ai, d_br, d_bi, seg):
    def body(a_re_ref, a_im_ref, ls_ref, b_re_ref, b_im_ref, g1, g2, g3, g4, seg_ref, o1, o2, o3, o4, o5):
        _, vjp = jax.vjp(_s5_disc, a_re_ref[...], a_im_ref[...], ls_ref[...], b_re_ref[...], b_im_ref[...])
        da_re, da_im, dls, db_re, db_im = vjp((g1[...], g2[...], g3[...], g4[...]))
        o1[...] = da_re
        o2[...] = da_im
        o3[...] = _dot32(seg_ref[...], dls)
        o4[...] = db_re
        o5[...] = db_im

    c1, c16 = jax.ShapeDtypeStruct((S5_N, 1), f32), jax.ShapeDtypeStruct((S5_N, S5_C), f32)
    return pl.pallas_call(body, name="s5_disc_bwd", out_shape=[c1, c1, jax.ShapeDtypeStruct((S5_G, 1), f32), c16, c16])(
        a_re, a_im, ls, b_re, b_im, d_ar, d_ai, d_br, d_bi, seg)


ANY = pl.BlockSpec(memory_space=pl.ANY)

GATHER = {'w_in': ((4352, 1024), 0), 'ffn_w_up': ((1024, 5632), 1), 'w_branch_rwkv': ((512, 1024), 1),
          'w_branch_s5': ((512, 1024), 1), 'w_out': ((1024, 1024), 0), 's5_w_glu': ((512, 512), 0),
          'ffn_w_down': ((2816, 1024), 0), 'rwkv_w2': ((64, 512), 1), 'rwkv_a2': ((64, 512), 1),
          'rwkv_g2': ((128, 512), 1), 'ffn_conv_w': ((8, 5632), 1)}
BIG = ['w_in', 'ffn_w_up', 'w_branch_rwkv', 'w_branch_s5', 'w_out', 's5_w_glu', 'ffn_w_down']
TINY = ['rwkv_w2', 'rwkv_a2', 'rwkv_g2', 'ffn_conv_w']
SMALL = [n for n in WEIGHTS if n not in GATHER]
SMALL_ROWS = 320
ADAM_ROWS = 256


def _mo(v, m):
    return v if isinstance(v, int) else pl.multiple_of(v, m)


def _slab(ref, name, j, h=None):
    (R, Cn), axis = GATHER[name]
    if axis == 0:
        rs = R // 4
        if h is None:
            return ref.at[pl.ds(_mo(j * rs, 16), rs), :]
        return ref.at[pl.ds(_mo(j * rs + h * (rs // 2), 8), rs // 2), :]
    cols = pl.ds(_mo(j * (Cn // 4), 128), Cn // 4)
    if h is None:
        return ref.at[:, cols]
    return ref.at[pl.ds(_mo(h * (R // 2), 8), R // 2), cols]


def _half_shape(name):
    (R, Cn), axis = GATHER[name]
    return (R // 8, Cn) if axis == 0 else (R // 2, Cn // 4)


def _chip_peers(px, py):
    return [((1 - px) if (k >> 1) else px, (1 - py) if (k & 1) else py) for k in (1, 2, 3)]


def _run_copies(copies):
    for cp in copies:
        cp.start()
    for cp in copies:
        cp.wait()


def _gather_weights(blocks):
    names = list(blocks)
    n = len(names)

    def body(*refs):
        ins, outs = refs[:n], refs[n:2 * n]
        ssem, rsem, lsem = refs[2 * n:]
        px, py, pc = _mesh_pos()
        me = 2 * px + py
        copies = []
        for i, nm in enumerate(names):
            if nm in BIG:
                hr = blocks[nm].shape[0] // 2
                src, dst = ins[i].at[pl.ds(pl.multiple_of(pc * hr, 16), hr), :], _slab(outs[i], nm, me, pc)
            else:
                src, dst = ins[i], _slab(outs[i], nm, me)
            copies.append(pltpu.make_async_copy(src, dst, lsem.at[i]))
            for k, (qx, qy) in enumerate(_chip_peers(px, py)):
                copies.append(pltpu.make_async_remote_copy(src, dst, ssem.at[3 * i + k], rsem.at[3 * i + k],
                                                           device_id=(qx, qy, pc), device_id_type=MESH))
        _run_copies(copies)

    outs = pl.pallas_call(
        body, name="gather_weights", in_specs=[ANY] * n, out_specs=[ANY] * n,
        out_shape=[jax.ShapeDtypeStruct(GATHER[nm][0], blocks[nm].dtype) for nm in names],
        scratch_shapes=[pltpu.SemaphoreType.DMA((3 * n,)), pltpu.SemaphoreType.DMA((3 * n,)), pltpu.SemaphoreType.DMA((n,))],
    )(*[blocks[nm] for nm in names])
    return dict(zip(names, outs))


def _gather_pair(full, names, call_name):
    n = len(names)

    def body(*refs):
        ins, outs = refs[:n], refs[n:2 * n]
        ssem, rsem = refs[2 * n:]
        px, py, pc = _mesh_pos()
        copies = []
        for i, nm in enumerate(names):
            for j in range(4):
                copies.append(pltpu.make_async_remote_copy(_slab(ins[i], nm, j, pc), _slab(outs[i], nm, j, pc),
                                                           ssem.at[4 * i + j], rsem.at[4 * i + j],
                                                           device_id=(px, py, 1 - pc), device_id_type=MESH))
        _run_copies(copies)

    outs = pl.pallas_call(
        body, name=call_name, in_specs=[ANY] * n, out_specs=[ANY] * n,
        out_shape=[jax.ShapeDtypeStruct(full[nm].shape, full[nm].dtype) for nm in names],
        input_output_aliases={i: i for i in range(n)},
        scratch_shapes=[pltpu.SemaphoreType.DMA((4 * n,)), pltpu.SemaphoreType.DMA((4 * n,))],
    )(*[full[nm] for nm in names])
    return dict(zip(names, outs))


def _grads_to_sibling(G, names, call_name, small=None):
    n = len(names)
    ns = 0 if small is None else 1

    def body(*refs):
        g_refs, o_refs = refs[:n + ns], refs[n + ns:2 * (n + ns)]
        ssem, rsem = refs[2 * (n + ns):]
        px, py, pc = _mesh_pos()
        sib = (px, py, 1 - pc)
        copies = []
        for i, nm in enumerate(names):
            for j in range(4):
                copies.append(pltpu.make_async_remote_copy(_slab(g_refs[i], nm, j, 1 - pc), o_refs[i].at[j],
                                                           ssem.at[4 * i + j], rsem.at[4 * i + j],
                                                           device_id=sib, device_id_type=MESH))
        if ns:
            copies.append(pltpu.make_async_remote_copy(g_refs[n], o_refs[n], ssem.at[4 * n], rsem.at[4 * n],
                                                       device_id=sib, device_id_type=MESH))
        _run_copies(copies)

    outs = pl.pallas_call(
        body, name=call_name, in_specs=[ANY] * (n + ns), out_specs=[ANY] * (n + ns),
        out_shape=[jax.ShapeDtypeStruct((4,) + _half_shape(nm), f32) for nm in names]
        + [jax.ShapeDtypeStruct((SMALL_ROWS, PACK_W), f32)] * ns,
        scratch_shapes=[pltpu.SemaphoreType.DMA((4 * n + ns,)), pltpu.SemaphoreType.DMA((4 * n + ns,))],
    )(*[G[nm] for nm in names], *([small] * ns))
    return dict(zip(names, outs[:n])), (outs[n] if ns else None)


def _pair_add(G, recv, names, call_name, small=None, small_recv=None):
    n = len(names)
    ns = 0 if small is None else 1
    cidx = lax.axis_index("c").astype(jnp.int32).reshape(1)

    def body(c_ref, *refs):
        ins, outs = refs[:2 * (n + ns)], refs[2 * (n + ns):]
        for i in range(n):
            outs[i][...] = (ins[i][...] + ins[n + ns + i][...]).astype(bf16)
        if ns:
            outs[n][...] = ins[n][...] + ins[2 * n + 1][...]

    g_specs, r_specs = [], []
    for nm in names:
        hr, hc = _half_shape(nm)
        if GATHER[nm][1] == 0:
            g_specs.append(pl.BlockSpec((hr // 2, hc), lambda j, i, c: ((2 * j + c[0]) * 2 + i, 0)))
        else:
            g_specs.append(pl.BlockSpec((hr // 2, hc), lambda j, i, c: (2 * c[0] + i, j)))
        r_specs.append(pl.BlockSpec((1, hr // 2, hc), lambda j, i, c: (j, i, 0)))
    sm = [pl.BlockSpec((SMALL_ROWS // 8, PACK_W), lambda j, i, c: (2 * j + i, 0))] * ns
    outs = pl.pallas_call(
        body, name=call_name,
        grid_spec=pltpu.PrefetchScalarGridSpec(num_scalar_prefetch=1, grid=(4, 2), in_specs=g_specs + sm + r_specs + sm,
                                               out_specs=r_specs + sm),
        out_shape=[jax.ShapeDtypeStruct((4,) + _half_shape(nm), bf16) for nm in names]
        + [jax.ShapeDtypeStruct((SMALL_ROWS, PACK_W), f32)] * ns,
        compiler_params=pltpu.CompilerParams(vmem_limit_bytes=VMEM_LIMIT),
    )(cidx, *[G[nm] for nm in names], *([small] * ns), *[recv[nm] for nm in names], *([small_recv] * ns))
    return dict(zip(names, outs[:n])), (outs[n] if ns else None)


def _exchange_copies(ins, outs, ssem, rsem):
    px, py, pc = _mesh_pos()
    me = 2 * px + py
    return [pltpu.make_async_remote_copy(ins[i].at[2 * qx + qy], outs[i].at[me], ssem.at[3 * i + k], rsem.at[3 * i + k],
                                         device_id=(qx, qy, pc), device_id_type=MESH)
            for i in range(len(ins)) for k, (qx, qy) in enumerate(_chip_peers(px, py))]


def _grads_chip_exchange(chip_sum, names, small):
    n = len(names)

    def body(*refs):
        ins, outs = refs[:n + 1], refs[n + 1:2 * n + 2]
        ssem, rsem, ssem_s, rsem_s = refs[2 * n + 2:]
        px, py, pc = _mesh_pos()
        me = 2 * px + py
        copies = _exchange_copies(ins[:n], outs[:n], ssem, rsem)
        hs = SMALL_ROWS // 2
        mine = ins[n].at[pl.ds(pl.multiple_of(pc * hs, 8), hs), :]
        copies += [pltpu.make_async_remote_copy(mine, outs[n].at[me], ssem_s.at[k], rsem_s.at[k],
                                                device_id=(qx, qy, pc), device_id_type=MESH)
                   for k, (qx, qy) in enumerate(_chip_peers(px, py))]
        _run_copies(copies)

    outs = pl.pallas_call(
        body, name="grads_chip_exchange", in_specs=[ANY] * (n + 1), out_specs=[ANY] * (n + 1),
        out_shape=[jax.ShapeDtypeStruct(chip_sum[nm].shape, chip_sum[nm].dtype) for nm in names]
        + [jax.ShapeDtypeStruct((4, SMALL_ROWS // 2, PACK_W), f32)],
        scratch_shapes=[pltpu.SemaphoreType.DMA((3 * n,)), pltpu.SemaphoreType.DMA((3 * n,)),
                        pltpu.SemaphoreType.DMA((3,)), pltpu.SemaphoreType.DMA((3,))],
    )(*[chip_sum[nm] for nm in names], small)
    return dict(zip(names, outs[:n])), outs[n]


def _sum_slots(slots, chip_sum, small4, small_own):
    n = len(BIG)
    me = jnp.stack([2 * lax.axis_index("x") + lax.axis_index("y"), lax.axis_index("c")]).astype(jnp.int32)

    def body(me_ref, *refs):
        for i in range(n + 1):
            own = refs[5 * i + 4][...].astype(f32)
            own = own[0] if i < n else own
            term = [jnp.where(me_ref[0] == k, own, refs[5 * i + k][0].astype(f32)) for k in range(4)]
            refs[5 * (n + 1) + i][...] = ((term[0] + term[1]) + term[2]) + term[3]

    redirect = lambda k: (lambda i, m: (jnp.where(m[0] == k, (k + 1) % 4, k), i, 0))
    in_specs, args, specs_out, shapes = [], [], [], []
    for nm in BIG:
        hr, hc = _half_shape(nm)
        in_specs += [pl.BlockSpec((1, hr // 2, hc), redirect(k)) for k in range(4)]
        in_specs.append(pl.BlockSpec((1, hr // 2, hc), lambda i, m: (m[0], i, 0)))
        args += [slots[nm]] * 4 + [chip_sum[nm]]
        specs_out.append(pl.BlockSpec((hr // 2, hc), lambda i, m: (i, 0)))
        shapes.append(jax.ShapeDtypeStruct((hr, hc), f32))
    in_specs += [pl.BlockSpec((1, SMALL_ROWS // 4, PACK_W), redirect(k)) for k in range(4)]
    in_specs.append(pl.BlockSpec((SMALL_ROWS // 4, PACK_W), lambda i, m: (2 * m[1] + i, 0)))
    args += [small4] * 4 + [small_own]
    specs_out.append(pl.BlockSpec((SMALL_ROWS // 4, PACK_W), lambda i, m: (i, 0)))
    shapes.append(jax.ShapeDtypeStruct((SMALL_ROWS // 2, PACK_W), f32))
    outs = pl.pallas_call(
        body, name="grads_chip_sum",
        grid_spec=pltpu.PrefetchScalarGridSpec(num_scalar_prefetch=1, grid=(2,), in_specs=in_specs, out_specs=specs_out),
        out_shape=shapes, compiler_params=pltpu.CompilerParams(vmem_limit_bytes=VMEM_LIMIT),
    )(me, *args)
    return dict(zip(BIG, outs[:n])), outs[n]


def _halves_to_sibling(half):
    names = list(half)
    n = len(names)

    def body(*refs):
        ins, outs = refs[:n], refs[n:2 * n]
        ssem, rsem = refs[2 * n:]
        px, py, pc = _mesh_pos()
        _run_copies([pltpu.make_async_remote_copy(ins[i], outs[i], ssem.at[i], rsem.at[i],
                                                  device_id=(px, py, 1 - pc), device_id_type=MESH) for i in range(n)])

    outs = pl.pallas_call(
        body, name="grads_halves_to_sibling", in_specs=[ANY] * n, out_specs=[ANY] * n,
        out_shape=[jax.ShapeDtypeStruct(half[nm].shape, f32) for nm in names],
        scratch_shapes=[pltpu.SemaphoreType.DMA((n,)), pltpu.SemaphoreType.DMA((n,))],
    )(*[half[nm] for nm in names])
    return dict(zip(names, outs))


def _join_halves(mine, other, pc):
    hr = mine.shape[0]
    return lax.dynamic_slice_in_dim(jnp.concatenate([other, mine, other], axis=0), (1 - pc) * hr, 2 * hr, axis=0)


def _flat_pad(v):
    v = v.reshape(-1)
    return jnp.pad(v, (0, _ceil_to(v.shape[0], PACK_W) - v.shape[0]))


def _pack_rows(parts, rows):
    flat = jnp.concatenate([_flat_pad(p) for p in parts])
    return jnp.pad(flat, (0, rows * PACK_W - flat.shape[0])).reshape(rows, PACK_W)


def _unpack_rows(buf, shapes):
    flat = buf.reshape(-1)
    out, off = [], 0
    for shp in shapes:
        n = 1
        for d in shp:
            n *= d
        out.append(flat[off:off + n].reshape(shp))
        off += _ceil_to(n, PACK_W)
    return out


def _adamw_math(w_, g_, m_, v_):
    m2 = ADAM_B1 * m_ + (1.0 - ADAM_B1) * g_
    v2 = ADAM_B2 * v_ + (1.0 - ADAM_B2) * (g_ * g_)
    m_hat = m2 / (1.0 - ADAM_B1 ** ADAM_STEP)
    v_hat = v2 / (1.0 - ADAM_B2 ** ADAM_STEP)
    return -ADAM_LR * (m_hat / (jnp.sqrt(v_hat) + ADAM_EPS) + ADAM_WD * w_), m2, v2


def _adamw(groups):
    ng = len(groups)

    def body(*refs):
        ins, outs = refs[:4 * ng], refs[4 * ng:]
        for i in range(ng):
            res = _adamw_math(*(r[...] for r in ins[4 * i:4 * i + 4]))
            for ref, val in zip(outs[3 * i:3 * i + 3], res):
                ref[...] = val

    in_specs, out_specs, out_shape = [], [], []
    for grp in groups:
        R, Cn = grp[0].shape
        spec = pl.BlockSpec((R // 8, Cn), lambda i: (i, 0))
        in_specs += [spec] * 4
        out_specs += [spec] * 3
        out_shape += [jax.ShapeDtypeStruct((R, Cn), f32)] * 3
    outs = pl.pallas_call(
        body, name="adamw", grid=(8,), in_specs=in_specs, out_specs=out_specs, out_shape=out_shape,
        compiler_params=pltpu.CompilerParams(vmem_limit_bytes=VMEM_LIMIT),
    )(*[a for grp in groups for a in grp])
    return [tuple(outs[3 * i:3 * i + 3]) for i in range(ng)]


def _forward_backward(x, tgt, W, S, late):
    L = x.shape[0]
    TM, TMW, TS = 256, 128, 512
    row = lambda c, dt=f32: (c, dt)
    hid = jnp.arange(RWKV_W) // HEAD
    E = (hid[:, None] == hid[None, :]).astype(f32)
    seg = (jnp.arange(S5_N)[None, :] // S5_P == jnp.arange(S5_G)[:, None]).astype(f32)

    w_in_t = W['w_in']
    w_p, w_u, w_g = w_in_t[:N_RWKV], w_in_t[N_RWKV:N_RWKV + S5_W], w_in_t[N_RWKV + S5_W:]
    zpad = jnp.zeros((64, RWKV_W), f32)
    w2p = jnp.concatenate([W['rwkv_w2'], zpad], axis=0)
    a2p = jnp.concatenate([zpad, W['rwkv_a2']], axis=0)
    g2 = W['rwkv_g2']
    prep_consts = [S['rwkv_shift_mu'], S['rwkv_w0'], S['rwkv_a0'], S['rwkv_k_k'], S['rwkv_k_a'], w2p, a2p, g2, E]
    out_consts = [S['rwkv_lnx_w'], S['rwkv_lnx_b'], S['rwkv_r_k'], E]
    cw, cb = W['ffn_conv_w'][:3], S['ffn_conv_b']

    a_re, a_im = S['s5_a_re'].reshape(S5_N, 1), S['s5_a_im'].reshape(S5_N, 1)
    ls = jnp.repeat(S['s5_log_step'].reshape(S5_G, 1), S5_P, axis=0)
    b_re, b_im = S['s5_b_re'].reshape(S5_N, S5_C), S['s5_b_im'].reshape(S5_N, S5_C)
    ar, ai, bbr, bbi = _s5_disc_fwd(a_re, a_im, ls, b_re, b_im)
    abar = jnp.concatenate([ar.reshape(1, S5_N), ai.reshape(1, S5_N)], axis=1)
    eye8 = jnp.eye(8, dtype=f32)

    def blocks_in(bb):
        t = bb.reshape(4, 8, S5_P, S5_C).transpose(0, 1, 3, 2)
        return (t[:, :, :, None, :] * eye8[None, :, None, :, None]).reshape(4, _CB, _SB)

    def blocks_out(cc):
        t = cc.reshape(4, 8, S5_C, S5_P).transpose(0, 1, 3, 2)
        return (t[:, :, :, None, :] * eye8[None, :, None, :, None]).reshape(4, _SB, _CB)

    def undiag_in(blocks):
        t = blocks.reshape(4, 8, S5_C, 8, S5_P)
        t = jnp.sum(t * eye8[None, :, None, :, None], axis=3)
        return t.reshape(S5_G, S5_C, S5_P).transpose(0, 2, 1).reshape(S5_N, S5_C)

    def undiag_out(blocks):
        t = blocks.reshape(4, 8, S5_P, 8, S5_C)
        t = jnp.sum(t * eye8[None, :, None, :, None], axis=3)
        return t.reshape(S5_G, S5_P, S5_C).transpose(0, 2, 1)

    bmat = jnp.concatenate([blocks_in(bbr), blocks_in(bbi)], axis=0).astype(bf16)
    cmat = jnp.concatenate([blocks_out(S['s5_c_re'].reshape(S5_G, S5_C, S5_P)),
                            -blocks_out(S['s5_c_im'].reshape(S5_G, S5_C, S5_P))], axis=0).astype(bf16)

    g1, g2n, g3, g4 = S['norm_mix_pre'], S['norm_mix_post'], S['norm_ffn_pre'], S['norm_ffn_post']
    h1, p, u, gp = _rows_mm("in_proj", lambda R, C: _rms(R[0], C[0]), L, TM, [x], [g1], w_in_t, 'nt',
                            [N_RWKV, S5_W, 2 * D_MODEL])

    def prep_fn(i, n, R, P, X, C):
        q = R[0] + (_shift_down(R[0], P[0], i, 1) - R[0]) * C[0]
        return _prep(q, *C[1:]), ()

    r, lw, k2, v, an, bv, g = _rowcall("rwkv_prep", prep_fn, L, TS, [p], prep_consts,
                                       out_rows=[row(RWKV_W)] * 7, prev=[0])
    y, ck, xinv = _wkv7_fwd(r, lw, k2, v, an, bv)
    (o_a,) = _rowcall("rwkv_out", lambda i, n, R, P, X, C: ((_rwkv_out(*R, *C),), ()), L, TS, [y, r, k2, v, g],
                      out_consts, out_rows=[row(RWKV_W, bf16)])
    o_r = _mm(o_a, W['w_branch_rwkv'], 'nn', "mm_br")

    st, ysc, got = _s5_fwd(u, bmat, cmat, abar, late)
    W = {**W, **_gather_pair(got, list(got), "gather_weights_pair_late")}
    (yg,) = _rowcall("s5_mid", lambda i, n, R, P, X, C: ((_s5_mid(*R, *C),), ()), L, TS, [ysc, u], [S['s5_d']],
                     out_rows=[row(S5_W)])
    z2 = _mm(yg, W['s5_w_glu'], 'nn', "mm_glu")
    (o_b,) = _rowcall("s5_glu", lambda i, n, R, P, X, C: ((_s5_glu(*R, *C),), ()), L, TS, [yg, z2], [S['s5_b_glu']],
                      out_rows=[row(S5_W, bf16)])
    o_s = _mm(o_b, W['w_branch_s5'], 'nn', "mm_bs")

    merged, mixed = _rows_mm("merge_out", lambda R, C: _merge(*R, *C), L, TS, [gp, o_r, o_s], [S['b_gate']],
                             W['w_out'], 'nn', [D_MODEL])

    def resid_fn(i, n, R, P, X, C):
        x1_ = R[0] + _rms(R[1], C[0])
        return (x1_, _rms(x1_, C[1])), ()

    x1, h2 = _rowcall("resid_norm", resid_fn, L, TS, [x, mixed], [g2n, g3], out_rows=[row(D_MODEL), row(D_MODEL, bf16)])

    z = _mm(h2, W['ffn_w_up'], 'nn', "mm_up")

    def conv(zt, zprev, i, cw_, cb_):
        z2s, z1s = _shift_down(zt, zprev, i, 2), _shift_down(zt, zprev, i, 1)
        return cb_ + cw_[0:1] * z2s + cw_[1:2] * z1s + cw_[2:3] * zt, z2s, z1s

    (act,) = _rowcall("conv_act", lambda i, n, R, P, X, C: ((_act(conv(R[0], P[0], i, C[0], C[1])[0]),), ()), L, TMW,
                      [z], [cw, cb], out_rows=[row(D_FF, bf16)], prev=[0])
    f = _mm(act, W['ffn_w_down'], 'nn', "mm_down")

    def final_fn(i, n, R, P, X, C):
        x1_, f_, t_ = R
        fn_, vjp = jax.vjp(_rms, f_, C[0])
        diff = x1_ + fn_ - t_
        loss = jnp.sum(diff * diff) * (0.5 / D_MODEL)
        dx2_ = diff * (1.0 / D_MODEL)
        df_, dg4_ = vjp(dx2_)
        return (df_, dx2_), (jnp.full((1, PACK_W), loss, f32), dg4_)

    df, dx2, loss, dg4 = _rowcall("loss_head", final_fn, L, TS, [x1, f, tgt], [g4],
                                  out_rows=[row(D_MODEL, bf16), row(D_MODEL)], out_accs=[(1, PACK_W), (1, D_MODEL)])
    G = {'norm_ffn_post': dg4}

    dact = _mm(df, W['ffn_w_down'], 'nt', "mm_down_dx")
    G['ffn_w_down'] = _mm(act, df, 'tn', "mm_down_dw")

    def conv_bwd_fn(i, n, R, P, X, C):
        z_, dact_ = R
        cw_, cb_ = C
        zc, z2s, z1s = conv(z_, P[0], i, cw_, cb_)
        _, vjp = jax.vjp(_act, zc)
        (dzc_,) = vjp(dact_)
        last8 = z_[z_.shape[0] - 8:]
        zcn = cb_ + cw_[0:1] * _shift_down(X[0], last8, 1, 2) + cw_[1:2] * _shift_down(X[0], last8, 1, 1) + cw_[2:3] * X[0]
        _, vjpn = jax.vjp(_act, zcn)
        (dzcn,) = vjpn(X[1])
        dz_ = (cw_[2:3] * dzc_ + cw_[1:2] * _shift_up(dzc_, dzcn, i, n, 1) + cw_[0:1] * _shift_up(dzc_, dzcn, i, n, 2))
        return (dz_,), (_sum0(dzc_), _sum0(dzc_ * z2s), _sum0(dzc_ * z1s), _sum0(dzc_ * z_))

    wide = (1, 2 * D_FF)
    dz, dcb, dcw0, dcw1, dcw2 = _rowcall("conv_act_bwd", conv_bwd_fn, L, TMW, [z, dact], [cw, cb],
                                         out_rows=[row(2 * D_FF, bf16)], out_accs=[wide] * 4, prev=[0], nxt=[0, 1])
    G['ffn_conv_b'] = dcb
    G['ffn_conv_w'] = jnp.concatenate([dcw0, dcw1, dcw2], axis=0)
    dh2 = _mm(dz, W['ffn_w_up'], 'nt', "mm_up_dx")
    G['ffn_w_up'] = _mm(h2, dz, 'tn', "mm_up_dw")

    def norm2_bwd_fn(i, n, R, P, X, C):
        x1_, mixed_, dx2_, dh2_ = R
        _, vjp3 = jax.vjp(_rms, x1_, C[1])
        dx1a, dg3_ = vjp3(dh2_)
        dx1_ = dx2_ + dx1a
        _, vjp2 = jax.vjp(_rms, mixed_, C[0])
        dmixed_, dg2_ = vjp2(dx1_)
        return (dx1_, dmixed_), (dg2_, dg3_)

    dx1, dmixed, dg2n, dg3 = _rowcall("norm_mid_bwd", norm2_bwd_fn, L, TS, [x1, mixed, dx2, dh2], [g2n, g3],
                                      out_rows=[row(D_MODEL), row(D_MODEL, bf16)], out_accs=[(1, D_MODEL)] * 2)
    G['norm_mix_post'], G['norm_ffn_pre'] = dg2n, dg3

    dmerged = _mm(dmixed, W['w_out'], 'nt', "mm_out_dx")
    G['w_out'] = _mm(merged, dmixed, 'tn', "mm_out_dw")

    def merge_bwd_fn(i, n, R, P, X, C):
        _, vjp = jax.vjp(_merge, R[0], R[1], R[2], C[0])
        dgp_, do_r_, do_s_, dbg_ = vjp(R[3])
        return (dgp_, do_r_, do_s_), (dbg_,)

    dgp, do_r, do_s, G['b_gate'] = _rowcall("merge_bwd", merge_bwd_fn, L, TS, [gp, o_r, o_s, dmerged], [S['b_gate']],
                                            out_rows=[row(2 * D_MODEL, bf16), row(D_MODEL, bf16), row(D_MODEL, bf16)],
                                            out_accs=[(1, 2 * D_MODEL)])
    do_a = _mm(do_r, W['w_branch_rwkv'], 'nt', "mm_br_dx")
    G['w_branch_rwkv'] = _mm(o_a, do_r, 'tn', "mm_br_dw")
    do_b = _mm(do_s, W['w_branch_s5'], 'nt', "mm_bs_dx")
    G['w_branch_s5'] = _mm(o_b, do_s, 'tn', "mm_bs_dw")

    def glu_bwd_fn(i, n, R, P, X, C):
        _, vjp = jax.vjp(_s5_glu, R[0], R[1], C[0])
        dyg1_, dz2_, dbg_ = vjp(R[2])
        return (dyg1_, dz2_), (dbg_,)

    dyg1, dz2, G['s5_b_glu'] = _rowcall("s5_glu_bwd", glu_bwd_fn, L, TS, [yg, z2, do_b], [S['s5_b_glu']],
                                        out_rows=[row(S5_W), row(S5_W, bf16)], out_accs=[(1, S5_W)])
    dyg2 = _mm(dz2, W['s5_w_glu'], 'nt', "mm_glu_dx")
    G['s5_w_glu'] = _mm(yg, dz2, 'tn', "mm_glu_dw")

    def mid_bwd_fn(i, n, R, P, X, C):
        _, vjp = jax.vjp(_s5_mid, R[0], R[1], C[0])
        dysc_, du_, dd_ = vjp(R[2] + R[3])
        return (dysc_, du_), (dd_,)

    dysc, du1, G['s5_d'] = _rowcall("s5_mid_bwd", mid_bwd_fn, L, TS, [ysc, u, dyg1, dyg2], [S['s5_d']],
                                    out_rows=[row(S5_W, bf16), row(S5_W)], out_accs=[(1, S5_W)])
    early = [n for n in BIG if n != 'w_in']
    recv_e, _ = _grads_to_sibling(G, early, "grads_to_sibling_early")
    chip_e, _ = _pair_add(G, recv_e, early, "grads_pair_sum_early")
    du, dbmat, dcmat, dabar, slots_e = _s5_bwd(dysc, st, u, du1, bmat, cmat, abar, chip_e)
    da_re, da_im, dls, db_re, db_im = _s5_disc_bwd(
        a_re, a_im, ls, b_re, b_im, dabar[:, :S5_N].reshape(S5_N, 1), dabar[:, S5_N:].reshape(S5_N, 1),
        undiag_in(dbmat[:4]), undiag_in(dbmat[4:]), seg)
    G['s5_a_re'], G['s5_a_im'], G['s5_log_step'] = da_re, da_im, dls
    G['s5_b_re'], G['s5_b_im'] = db_re, db_im
    G['s5_c_re'], G['s5_c_im'] = undiag_out(dcmat[:4]), -undiag_out(dcmat[4:])

    def out_bwd_fn(i, n, R, P, X, C):
        _, vjp = jax.vjp(_rwkv_out, *R[:5], *C)
        gs = vjp(R[5])
        return gs[:5], gs[5:8]

    dy, dr1, dk1, dv1, dg, dlw, dlb, drk = _rowcall("rwkv_out_bwd", out_bwd_fn, L, TM, [y, r, k2, v, g, do_a], out_consts,
                                                    out_rows=[row(RWKV_W)] * 5, out_accs=[(1, RWKV_W)] * 3)
    G['rwkv_lnx_w'], G['rwkv_lnx_b'], G['rwkv_r_k'] = dlw, dlb, drk
    dr2, dlwk, dk2b, dv2, dan, dbv = _wkv7_bwd(r, lw, k2, v, an, bv, ck, xinv, dy)

    def prep_bwd_fn(i, n, R, P, X, C):
        p_ = R[0]
        d1 = _shift_down(p_, P[0], i, 1) - p_
        q = p_ + d1 * C[0]
        _, vjp = jax.vjp(_prep, q, *C[1:])
        cots = (R[1] + R[2], R[3], R[4] + R[5], R[6] + R[7], R[8], R[9], R[10])
        gs = vjp(cots)
        return (gs[0],), (_sum0(gs[0] * d1),) + tuple(gs[1:8])

    small, lowr = (1, RWKV_W), (128, RWKV_W)
    dq, dmu, dw0, da0, dkk, dka, dw2p, da2p, dg2 = _rowcall(
        "rwkv_prep_bwd", prep_bwd_fn, L, TM, [p, dr1, dr2, dlwk, dk1, dk2b, dv1, dv2, dan, dbv, dg],
        prep_consts, out_rows=[row(N_RWKV)], out_accs=[(1, N_RWKV)] + [small] * 4 + [lowr] * 3, prev=[0])
    G['rwkv_shift_mu'], G['rwkv_w0'], G['rwkv_a0'], G['rwkv_k_k'], G['rwkv_k_a'] = dmu, dw0, da0, dkk, dka
    G['rwkv_w2'], G['rwkv_a2'], G['rwkv_g2'] = dw2p[:64], da2p[64:], dg2

    def shift_bwd_fn(i, n, R, P, X, C):
        dm = R[0] * C[0]
        return (R[0] - dm + _shift_up(dm, X[0] * C[0], i, n, 1),), ()

    (dp,) = _rowcall("shift_bwd", shift_bwd_fn, L, TS, [dq], [S['rwkv_shift_mu']], out_rows=[row(N_RWKV, bf16)], nxt=[0])

    dproj = jnp.concatenate([dp, du, dgp], axis=1)
    dh1 = _mm(dproj, w_in_t, 'nn', "mm_in_dx")
    G['w_in'] = _mm(dproj, h1, 'tn', "mm_in_dw")

    def norm1_bwd_fn(i, n, R, P, X, C):
        _, vjp = jax.vjp(_rms, R[0], C[0])
        dxa, dg1_ = vjp(R[2])
        return (R[1] + dxa,), (dg1_,)

    dx, G['norm_mix_pre'] = _rowcall("norm_pre_bwd", norm1_bwd_fn, L, TS, [x, dx1, dh1], [g1],
                                     out_rows=[row(D_MODEL)], out_accs=[(1, D_MODEL)])
    return loss, dx, G, chip_e, slots_e


def kernel(x, norm_mix_pre, norm_mix_post, norm_ffn_pre, norm_ffn_post, w_in, b_gate, rwkv_shift_mu, rwkv_w0, rwkv_w2, rwkv_a0, rwkv_a2, rwkv_g2, rwkv_k_k, rwkv_k_a, rwkv_r_k, rwkv_lnx_w, rwkv_lnx_b, s5_a_re, s5_a_im, s5_b_re, s5_b_im, s5_c_re, s5_c_im, s5_d, s5_log_step, s5_w_glu, s5_b_glu, w_branch_rwkv, w_branch_s5, w_out, ffn_w_up, ffn_conv_w, ffn_conv_b, ffn_w_down, loss_target, m_norm_mix_pre, m_norm_mix_post, m_norm_ffn_pre, m_norm_ffn_post, m_w_in, m_b_gate, m_rwkv_shift_mu, m_rwkv_w0, m_rwkv_w2, m_rwkv_a0, m_rwkv_a2, m_rwkv_g2, m_rwkv_k_k, m_rwkv_k_a, m_rwkv_r_k, m_rwkv_lnx_w, m_rwkv_lnx_b, m_s5_a_re, m_s5_a_im, m_s5_b_re, m_s5_b_im, m_s5_c_re, m_s5_c_im, m_s5_d, m_s5_log_step, m_s5_w_glu, m_s5_b_glu, m_w_branch_rwkv, m_w_branch_s5, m_w_out, m_ffn_w_up, m_ffn_conv_w, m_ffn_conv_b, m_ffn_w_down, v_norm_mix_pre, v_norm_mix_post, v_norm_ffn_pre, v_norm_ffn_post, v_w_in, v_b_gate, v_rwkv_shift_mu, v_rwkv_w0, v_rwkv_w2, v_rwkv_a0, v_rwkv_a2, v_rwkv_g2, v_rwkv_k_k, v_rwkv_k_a, v_rwkv_r_k, v_rwkv_lnx_w, v_rwkv_lnx_b, v_s5_a_re, v_s5_a_im, v_s5_b_re, v_s5_b_im, v_s5_c_re, v_s5_c_im, v_s5_d, v_s5_log_step, v_s5_w_glu, v_s5_b_glu, v_w_branch_rwkv, v_w_branch_s5, v_w_out, v_ffn_w_up, v_ffn_conv_w, v_ffn_conv_b, v_ffn_w_down):
    A = dict(locals())
    me = 2 * lax.axis_index("x") + lax.axis_index("y")
    blk = lambda n: A[n][0]

    mine = {n: (blk(n).T if n == 'w_in' else blk(n)).astype(bf16) for n in BIG}
    mine.update({n: blk(n) for n in TINY})
    mine['ffn_conv_w'] = jnp.pad(blk('ffn_conv_w'), ((0, 5), (0, 0)))
    late = ['ffn_w_up', 'ffn_w_down']
    W = _gather_weights({n: blkv for n, blkv in mine.items() if n not in late})
    W.update(_gather_pair(W, [n for n in BIG if n not in late], "gather_weights_pair"))
    S = {n: A[n].reshape(1, -1) for n in SMALL}

    loss, dx, G, chip_e, slots_e = _forward_backward(x[0], loss_target[0], W, S, {n: mine[n] for n in late})

    tiny_shapes = [G[n].shape for n in TINY]
    small_buf = _pack_rows([G[n] for n in SMALL] + [G[n] for n in TINY] + [loss], SMALL_ROWS)
    recv, small_recv = _grads_to_sibling(G, ['w_in'], "grads_to_sibling", small_buf)
    chip_l, small_sum = _pair_add(G, recv, ['w_in'], "grads_pair_sum", small_buf, small_recv)
    slots_l, small4 = _grads_chip_exchange(chip_l, ['w_in'], small_sum)
    half, half['small'] = _sum_slots({**slots_e, **slots_l}, {**chip_e, **chip_l}, small4, small_sum)
    other = _halves_to_sibling(half)
    pc = lax.axis_index("c")
    small_tot = _join_halves(half['small'], other['small'], pc)
    grad = {n: _join_halves(half[n], other[n], pc) for n in BIG}
    grad['w_in'] = grad['w_in'].T
    vals = _unpack_rows(small_tot, [A[n].shape for n in SMALL] + tiny_shapes + [(1, PACK_W)])
    grad.update(zip(SMALL, vals))
    for n, full in zip(TINY, vals[len(SMALL):]):
        cs = A[n].shape[2]
        grad[n] = lax.dynamic_slice_in_dim(full, me * cs, cs, axis=1)
    loss_out = vals[-1][0, 0]

    packed = SMALL + TINY
    groups = [(blk(n), grad[n], blk('m_' + n), blk('v_' + n)) for n in BIG]
    groups.append(tuple(_pack_rows([src(n) for n in packed], ADAM_ROWS)
                        for src in (lambda n: A[n], lambda n: grad[n], lambda n: A['m_' + n], lambda n: A['v_' + n])))
    res = _adamw(groups)
    outs = [dict(), dict(), dict()]
    for n, r3 in zip(BIG, res[:-1]):
        for d, val in zip(outs, r3):
            d[n] = val
    for d, buf in zip(outs, res[-1]):
        d.update(zip(packed, _unpack_rows(buf, [A[n].shape for n in packed])))
    full = lambda d: [d[n].reshape(A[n].shape) for n in WEIGHTS]
    return (loss_out, dx[None], *full(grad), *full(outs[0]), *full(outs[1]), *full(outs[2]))
```

```python
import functools

import jax
import jax.numpy as jnp
from jax import lax
from jax.experimental import pallas as pl
from jax.experimental.pallas import tpu as pltpu

f32, bf16 = jnp.float32, jnp.bfloat16
MESH = pl.DeviceIdType.MESH

D_MODEL = 1024
RWKV_W = 512
HEADS, HEAD = 8, 64
N_RWKV = 1792
S5_W = 512
S5_G, S5_P, S5_C = 32, 64, 16
S5_N = S5_G * S5_P
D_FF = 2816
NORM_EPS = 1e-6
LNX_EPS = 64e-5
ADAM_LR, ADAM_B1, ADAM_B2, ADAM_EPS, ADAM_WD, ADAM_STEP = 0.001, 0.9, 0.999, 1e-08, 0.01, 10

VMEM_LIMIT = 48 * 1024 * 1024
PACK_W = 1024
WKV_C = 64
WKV_SUB = 4
WKV_ROWS = WKV_C * WKV_SUB
RESIDENT_BUDGET = 40 * 1024 * 1024
S5_T = 256

WEIGHTS = ['norm_mix_pre', 'norm_mix_post', 'norm_ffn_pre', 'norm_ffn_post', 'w_in', 'b_gate', 'rwkv_shift_mu',
           'rwkv_w0', 'rwkv_w2', 'rwkv_a0', 'rwkv_a2', 'rwkv_g2', 'rwkv_k_k', 'rwkv_k_a', 'rwkv_r_k', 'rwkv_lnx_w',
           'rwkv_lnx_b', 's5_a_re', 's5_a_im', 's5_b_re', 's5_b_im', 's5_c_re', 's5_c_im', 's5_d', 's5_log_step',
           's5_w_glu', 's5_b_glu', 'w_branch_rwkv', 'w_branch_s5', 'w_out', 'ffn_w_up', 'ffn_conv_w', 'ffn_conv_b',
           'ffn_w_down']


def _ceil_to(n, m):
    return -(-n // m) * m


def _mesh_pos():
    return lax.axis_index("x"), lax.axis_index("y"), lax.axis_index("c")


def _pick(d, cap=4096):
    for c in (1024, 1408, 2176, 896, 512, 256, 128):
        if c <= cap and d % c == 0:
            return c
    raise ValueError(d)


def _mm_resident(a, w, mode, name, M, N, K, out_dtype):
    budget = RESIDENT_BUDGET - 2 * K * N
    tm = next(t for t in (512, 256, 128) if 2 * t * (K * a.dtype.itemsize + 4 * N) <= budget)
    dims = _DIMS[mode]

    def body(a_ref, w_ref, o_ref):
        o_ref[...] = lax.dot_general(a_ref[...].astype(bf16), w_ref[...], (dims, ((), ())),
                                     preferred_element_type=f32).astype(o_ref.dtype)

    return pl.pallas_call(
        body, name=name, grid=(M // tm,),
        in_specs=[pl.BlockSpec((tm, K), lambda i: (i, 0)),
                  pl.BlockSpec(w.shape, lambda i: (0, 0), pipeline_mode=pl.Buffered(1))],
        out_specs=pl.BlockSpec((tm, N), lambda i: (i, 0)), out_shape=jax.ShapeDtypeStruct((M, N), out_dtype),
        compiler_params=pltpu.CompilerParams(dimension_semantics=("parallel",), vmem_limit_bytes=VMEM_LIMIT),
    )(a, w)


def _mm(a, b, mode, name, out_dtype=f32):
    if mode == 'tn':
        (K, M), (K2, N) = a.shape, b.shape
    elif mode == 'nt':
        (M, K), (N, K2) = a.shape, b.shape
    else:
        (M, K), (K2, N) = a.shape, b.shape
    assert K == K2, (name, a.shape, b.shape)
    if mode != 'tn' and b.dtype == bf16:
        return _mm_resident(a, b, mode, name, M, N, K, out_dtype)
    if mode == 'tn':
        tm = _pick(M, 2176)
        tn = _pick(N, 512 if tm > 1408 else (1024 if tm > 1024 else 1408))
        tk = _pick(K, 1024 if a.dtype == bf16 and b.dtype == bf16 else 512)
    else:
        tm, tn, tk = _pick(M, 512), _pick(N), _pick(K)
    nk = K // tk
    dims = {'nn': ((1,), (0,)), 'nt': ((1,), (1,)), 'tn': ((0,), (0,))}[mode]

    def body(a_ref, b_ref, o_ref, acc_ref):
        k = pl.program_id(2)

        @pl.when(k == 0)
        def _():
            acc_ref[...] = jnp.zeros_like(acc_ref)

        acc_ref[...] += lax.dot_general(a_ref[...].astype(bf16), b_ref[...].astype(bf16), (dims, ((), ())),
                                        preferred_element_type=f32)

        @pl.when(k == nk - 1)
        def _():
            o_ref[...] = acc_ref[...].astype(o_ref.dtype)

    a_spec = pl.BlockSpec((tk, tm), lambda i, j, k: (k, i)) if mode == 'tn' else pl.BlockSpec((tm, tk), lambda i, j, k: (i, k))
    b_spec = pl.BlockSpec((tn, tk), lambda i, j, k: (j, k)) if mode == 'nt' else pl.BlockSpec((tk, tn), lambda i, j, k: (k, j))
    return pl.pallas_call(
        body, name=name, grid=(M // tm, N // tn, nk),
        in_specs=[a_spec, b_spec], out_specs=pl.BlockSpec((tm, tn), lambda i, j, k: (i, j)),
        out_shape=jax.ShapeDtypeStruct((M, N), out_dtype),
        scratch_shapes=[pltpu.VMEM((tm, tn), f32)],
        compiler_params=pltpu.CompilerParams(dimension_semantics=("parallel", "parallel", "arbitrary"),
                                             vmem_limit_bytes=VMEM_LIMIT),
    )(a, b)


def _rowcall(name, fn, L, tm, rows, consts=(), out_rows=(), out_accs=(), prev=(), nxt=()):
    nsteps = L // tm
    nb8 = tm // 8
    last8 = L // 8 - 1
    n_r, n_p, n_x, n_c, n_or = len(rows), len(prev), len(nxt), len(consts), len(out_rows)

    def body(*refs):
        i = pl.program_id(0)
        vals = [r[...] for r in refs[:n_r + n_p + n_x + n_c]]
        R, P = vals[:n_r], vals[n_r:n_r + n_p]
        X, C = vals[n_r + n_p:n_r + n_p + n_x], vals[n_r + n_p + n_x:]
        o_refs = refs[n_r + n_p + n_x + n_c:]
        outs_r, outs_a = fn(i, nsteps, R, P, X, C)
        for ref, v in zip(o_refs[:n_or], outs_r, strict=True):
            ref[...] = v.astype(ref.dtype)
        if out_accs:
            @pl.when(i == 0)
            def _():
                for ref in o_refs[n_or:]:
                    ref[...] = jnp.zeros_like(ref)

            for ref, v in zip(o_refs[n_or:], outs_a, strict=True):
                ref[...] += v

    def const_spec(c):
        nd = c.ndim
        return pl.BlockSpec(c.shape, lambda i: (0,) * nd)

    in_specs = ([pl.BlockSpec((tm, a.shape[1]), lambda i: (i, 0)) for a in rows]
                + [pl.BlockSpec((8, rows[j].shape[1]), lambda i: (jnp.maximum(i * nb8 - 1, 0), 0)) for j in prev]
                + [pl.BlockSpec((8, rows[j].shape[1]), lambda i: (jnp.minimum((i + 1) * nb8, last8), 0)) for j in nxt]
                + [const_spec(c) for c in consts])
    out_specs = ([pl.BlockSpec((tm, c), lambda i: (i, 0)) for c, _ in out_rows]
                 + [pl.BlockSpec(s, lambda i: (0, 0)) for s in out_accs])
    out_shape = ([jax.ShapeDtypeStruct((L, c), dt) for c, dt in out_rows]
                 + [jax.ShapeDtypeStruct(s, f32) for s in out_accs])
    args = list(rows) + [rows[j] for j in prev] + [rows[j] for j in nxt] + list(consts)
    return pl.pallas_call(
        body, name=name, grid=(nsteps,), in_specs=in_specs, out_specs=out_specs, out_shape=out_shape,
        compiler_params=pltpu.CompilerParams(dimension_semantics=("arbitrary",), vmem_limit_bytes=VMEM_LIMIT),
    )(*args)


def _rows_mm(name, fn, L, tm, rows, consts, w, mode, splits, extra=()):
    n_r, n_c, n_e = len(rows), len(consts), len(extra)
    K = w.shape[0] if mode == 'nn' else w.shape[1]

    def body(*refs):
        R, C = [r[...] for r in refs[:n_r]], [r[...] for r in refs[n_r:n_r + n_c]]
        w_ref, lhs_ref = refs[n_r + n_c], refs[n_r + n_c + 1]
        e_refs, outs = refs[n_r + n_c + 2:n_r + n_c + 2 + n_e], refs[n_r + n_c + 2 + n_e:]
        lhs, ex = fn(R, C) if n_e else (fn(R, C), ())
        for ref, val in zip(e_refs, ex, strict=True):
            ref[...] = val.astype(ref.dtype)
        lhs = lhs.astype(bf16)
        lhs_ref[...] = lhs
        off = 0
        for ref, wd in zip(outs, splits):
            wk = w_ref[:, off:off + wd] if mode == 'nn' else w_ref[off:off + wd, :]
            ref[...] = lax.dot_general(lhs, wk, (_DIMS[mode], ((), ())), preferred_element_type=f32)
            off += wd

    def const_spec(c):
        nd = c.ndim
        return pl.BlockSpec(c.shape, lambda i: (0,) * nd)

    return pl.pallas_call(
        body, name=name, grid=(L // tm,),
        in_specs=[pl.BlockSpec((tm, a.shape[1]), lambda i: (i, 0)) for a in rows] + [const_spec(c) for c in consts]
        + [pl.BlockSpec(w.shape, lambda i: (0, 0), pipeline_mode=pl.Buffered(1))],
        out_specs=[pl.BlockSpec((tm, c), lambda i: (i, 0)) for c in [K] + [c for c, _ in extra] + list(splits)],
        out_shape=[jax.ShapeDtypeStruct((L, K), bf16)] + [jax.ShapeDtypeStruct((L, c), dt) for c, dt in extra]
        + [jax.ShapeDtypeStruct((L, wd), f32) for wd in splits],
        compiler_params=pltpu.CompilerParams(dimension_semantics=("parallel",), vmem_limit_bytes=VMEM_LIMIT),
    )(*rows, *consts, w)


def _shift_down(x, prev8, i, k):
    rolled = pltpu.roll(x, k, axis=0)
    pfix = jnp.where(i > 0, pltpu.roll(prev8, k, axis=0), 0.0)
    row8 = lax.broadcasted_iota(jnp.int32, pfix.shape, 0)
    top = jnp.where(row8 < k, pfix, rolled[:8])
    return top if x.shape[0] == 8 else jnp.concatenate([top, rolled[8:]], axis=0)


def _shift_up(x, next8, i, nsteps, k):
    tm = x.shape[0]
    rolled = pltpu.roll(x, tm - k, axis=0)
    nfix = jnp.where(i < nsteps - 1, pltpu.roll(next8, 8 - k, axis=0), 0.0)
    row8 = lax.broadcasted_iota(jnp.int32, nfix.shape, 0)
    bot = jnp.where(row8 >= 8 - k, nfix, rolled[tm - 8:])
    return jnp.concatenate([rolled[:tm - 8], bot], axis=0)


def _sum0(x):
    return jnp.sum(x, axis=0, keepdims=True)


def _rms(x, g):
    return x * lax.rsqrt(jnp.mean(x * x, axis=-1, keepdims=True) + NORM_EPS) * g


def _softplus(x):
    return jnp.maximum(x, 0.0) + jnp.log(1.0 + jnp.exp(-jnp.abs(x)))


def _gelu(x):
    return 0.5 * x * (1.0 + jnp.tanh(0.7978845608028654 * (x + 0.044715 * x * x * x)))


def _dot32(a, b):
    return jnp.dot(a, b, preferred_element_type=f32, precision=lax.Precision.HIGHEST)


def _seg_raw(x, E):
    hi = x.astype(bf16)
    r1 = x - hi.astype(f32)
    mid = r1.astype(bf16)
    lo = (r1 - mid.astype(f32)).astype(bf16)
    Eb = E.astype(bf16)
    dot = lambda t: jnp.dot(t, Eb, preferred_element_type=f32)
    return (dot(lo) + dot(mid)) + dot(hi)


@jax.custom_vjp
def _seg(x, E):
    return _seg_raw(x, E)


_seg.defvjp(lambda x, E: (_seg_raw(x, E), E), lambda E, g: (_seg_raw(g, E), jnp.zeros_like(E)))


def _prep(q, w0, a0, k_k, k_a, w2p, a2p, g2, E):
    r, k, v = q[:, 0:512], q[:, 512:1024], q[:, 1024:1536]
    wa, gd = q[:, 1536:1664], q[:, 1664:1792]
    wlog = -_softplus(-(w0 + _bdot(jnp.tanh(wa), w2p, 'nn'))) - 0.5
    lw = -jnp.exp(wlog)
    a = jax.nn.sigmoid(a0 + _bdot(wa, a2p, 'nn'))
    g = _bdot(jax.nn.sigmoid(gd), g2, 'nn')
    kk = k * k_k
    kkn = kk / jnp.maximum(jnp.sqrt(_seg(kk * kk, E)), 1e-12)
    k2 = k * (1.0 + (a - 1.0) * k_a)
    return r, lw, k2, v, -kkn, kkn * a, g


def _rwkv_out(y, r, k2, v, g, lnx_w, lnx_b, r_k, E):
    mean = _seg(y, E) * (1.0 / HEAD)
    yc = y - mean
    var = _seg(yc * yc, E) * (1.0 / HEAD)
    yn = yc * lax.rsqrt(var + LNX_EPS) * lnx_w + lnx_b
    bonus = _seg(r * k2 * r_k, E) * v
    return (yn + bonus) * g


def _s5_mid(ysc, u, d):
    return _gelu(ysc + d * u)


def _s5_glu(yg, z2, b_glu):
    return yg * jax.nn.sigmoid(z2 + b_glu)


def _merge(gp, o_r, o_s, b_gate):
    gates = jax.nn.sigmoid(gp + b_gate)
    return gates[:, :D_MODEL] * o_r + gates[:, D_MODEL:] * o_s


def _act(zc):
    return _gelu(zc[:, :D_FF]) * zc[:, D_FF:]


def _s5_disc(a_re, a_im, ls, b_re, b_im):
    dt = jnp.exp(ls)
    er = jnp.exp(a_re * dt)
    ar, ai = er * jnp.cos(a_im * dt), er * jnp.sin(a_im * dt)
    x, y = ar - 1.0, ai
    den = a_re * a_re + a_im * a_im
    fr, fi = (x * a_re + y * a_im) / den, (y * a_re - x * a_im) / den
    return ar, ai, fr * b_re - fi * b_im, fr * b_im + fi * b_re


_DIMS = {'nn': ((1,), (0,)), 'nt': ((1,), (1,)), 'tn': ((0,), (0,))}


def _raw_bdot(a, b, mode):
    return lax.dot_general(a.astype(bf16), b.astype(bf16), (_DIMS[mode], ((), ())), preferred_element_type=f32)


@functools.partial(jax.custom_vjp, nondiff_argnums=(2,))
def _bdot(a, b, mode):
    return _raw_bdot(a, b, mode)


def _bdot_fwd(a, b, mode):
    return _raw_bdot(a, b, mode), (a, b)


def _bdot_bwd(mode, res, g):
    a, b = res
    if mode == 'nn':
        return _raw_bdot(g, b, 'nt'), _raw_bdot(a, g, 'tn')
    if mode == 'nt':
        return _raw_bdot(g, b, 'nn'), _raw_bdot(g, a, 'tn')
    return _raw_bdot(b, g, 'nt'), _raw_bdot(a, g, 'nn')


_bdot.defvjp(_bdot_fwd, _bdot_bwd)


def _tri_inv_raw(A):
    n = A[0].shape[0]
    eye = (lax.broadcasted_iota(jnp.int32, (n, n), 0) == lax.broadcasted_iota(jnp.int32, (n, n), 1)).astype(f32)
    x = [eye + a for a in A]
    pw, m = A, 1
    while 2 * m < n // 2:
        pw = [_raw_bdot(p, p, 'nn') for p in pw]
        x = [xi + _raw_bdot(xi, p, 'nn') for xi, p in zip(x, pw)]
        m *= 2
    return x


@jax.custom_vjp
def _tri_inv(A):
    return _tri_inv_raw(A)


def _tri_inv_fwd(A):
    x = _tri_inv_raw(A)
    return x, x


def _tri_inv_bwd(x, g):
    return ([_raw_bdot(_raw_bdot(xi, gi, 'tn'), xi, 'nt') for xi, gi in zip(x, g)],)


_tri_inv.defvjp(_tri_inv_fwd, _tri_inv_bwd)


@jax.custom_vjp
def _inv_given(A, X):
    return X


_inv_given.defvjp(lambda A, X: (X, X),
                  lambda x, g: (_tri_inv_bwd(x, g)[0], [jnp.zeros_like(xi) for xi in x]))


def _wkv_chunk(S0, r, lw, k, v, a, b, tri, bd, xinv=None):
    C = r[0].shape[0]
    P = range(len(r))
    lane = lax.broadcasted_iota(jnp.int32, (1, 2 * HEAD), 1)
    m0, m1 = (lane < HEAD).astype(f32), (lane >= HEAD).astype(f32)
    cat = lambda *xs: jnp.concatenate(xs, axis=0)
    stack = lambda x: cat(x * m0, x * m1)
    unstack = lambda x2: m0 * x2[:C] + m1 * x2[C:]
    rid = lax.broadcasted_iota(jnp.int32, (2 * C, 2 * C), 0)
    cid = lax.broadcasted_iota(jnp.int32, (2 * C, 2 * C), 1)
    same = (rid < C) == (cid < C)
    eye2 = (rid == cid).astype(f32)
    tri2 = (same & (rid >= cid)).astype(f32)
    sl2 = tri2 - eye2
    cum = [_dot32(tri, lw[p]) for p in P]
    g = [jnp.exp(cum[p]) for p in P]
    gi = [jnp.exp(-cum[p]) for p in P]
    at = [a[p] * jnp.exp(cum[p] - lw[p]) for p in P]
    rt = [r[p] * g[p] for p in P]
    kb = [k[p] * gi[p] for p in P]
    bb = [b[p] * gi[p] for p in P]
    lhs = [cat(stack(at[p]), stack(rt[p])) for p in P]
    pb = [_bdot(lhs[p], stack(bb[p]), 'nt') for p in P]
    pk = [_bdot(lhs[p], stack(kb[p]), 'nt') for p in P]
    aab = [pb[p][:2 * C] * sl2 for p in P]
    base = [_bdot(cat(at[p], rt[p]), S0[p], 'nt') for p in P]
    t = [_bdot(cat(pk[p][:2 * C] * sl2, pk[p][2 * C:] * tri2), cat(v[p], v[p]), 'nn') for p in P]
    rhs = [cat(base[p][:C], base[p][:C]) + t[p][:2 * C] for p in P]
    x = _tri_inv(aab) if xinv is None else _inv_given(aab, xinv)
    u = [unstack(_bdot(x[p], rhs[p], 'nn')) for p in P]
    w2 = [_bdot(pb[p][2 * C:] * tri2, cat(u[p], u[p]), 'nn') for p in P]
    y = [base[p][C:] + unstack(t[p][2 * C:]) + unstack(w2[p]) for p in P]
    S1 = [g[p][C - 1:C, :] * (S0[p] + bd * _bdot(cat(v[p], u[p]), cat(kb[p], bb[p]), 'tn')) for p in P]
    return y, S1, x


def _pairs(x):
    return [x[:, 2 * HEAD * p:2 * HEAD * (p + 1)] for p in range(HEADS // 2)]


def _wkv_consts():
    tri = jnp.tril(jnp.ones((WKV_C, WKV_C), f32))
    hid = jnp.arange(2 * HEAD) // HEAD
    return tri, (hid[:, None] == hid[None, :]).astype(f32)


def _wkv_step(S0, r, lw, k, v, a, b, tri, bd, xinv=None):
    ys, xs, S = [], [], S0
    for c in range(WKV_SUB):
        sub = lambda t: [x[c * WKV_C:(c + 1) * WKV_C] for x in t]
        y, S, x = _wkv_chunk(S, sub(r), sub(lw), sub(k), sub(v), sub(a), sub(b), tri, bd, None if xinv is None else xinv[c])
        ys.append(y)
        xs.append(x)
    return [jnp.concatenate([y[p] for y in ys], axis=0) for p in range(len(S0))], S, xs


def _wkv7_fwd(r, lw, k, v, a, b):
    L = r.shape[0]
    nc, npair = L // WKV_ROWS, HEADS // 2

    def body(r_ref, lw_ref, k_ref, v_ref, a_ref, b_ref, tri_ref, bd_ref, y_ref, ck_ref, xi_ref, s_ref):
        @pl.when(pl.program_id(0) == 0)
        def _():
            s_ref[...] = jnp.zeros_like(s_ref)

        s0 = [s_ref[p] for p in range(npair)]
        for p in range(npair):
            ck_ref[0, p] = s0[p]
        y, s1, xs = _wkv_step(s0, *(_pairs(x) for x in (r_ref, lw_ref, k_ref, v_ref, a_ref, b_ref)), tri_ref[...], bd_ref[...])
        for p in range(npair):
            y_ref[:, 2 * HEAD * p:2 * HEAD * (p + 1)] = y[p]
            s_ref[p] = s1[p]
            for c in range(WKV_SUB):
                xi_ref[0, c, p] = xs[c][p].astype(xi_ref.dtype)

    row = pl.BlockSpec((WKV_ROWS, RWKV_W), lambda c: (c, 0))
    sspec = pl.BlockSpec((1, npair, 2 * HEAD, 2 * HEAD), lambda c: (c, 0, 0, 0))
    xspec = pl.BlockSpec((1, WKV_SUB, npair, 2 * HEAD, 2 * HEAD), lambda c: (c, 0, 0, 0, 0))
    return pl.pallas_call(
        body, name="wkv7_fwd", grid=(nc,),
        in_specs=[row] * 6 + [pl.BlockSpec((WKV_C, WKV_C), lambda c: (0, 0)), pl.BlockSpec((2 * HEAD, 2 * HEAD), lambda c: (0, 0))],
        out_specs=[row, sspec, xspec],
        out_shape=[jax.ShapeDtypeStruct((L, RWKV_W), f32), jax.ShapeDtypeStruct((nc, npair, 2 * HEAD, 2 * HEAD), f32),
                   jax.ShapeDtypeStruct((nc, WKV_SUB, npair, 2 * HEAD, 2 * HEAD), bf16)],
        scratch_shapes=[pltpu.VMEM((npair, 2 * HEAD, 2 * HEAD), f32)],
        compiler_params=pltpu.CompilerParams(dimension_semantics=("arbitrary",), vmem_limit_bytes=VMEM_LIMIT),
    )(r, lw, k, v, a, b, *_wkv_consts())


def _wkv7_bwd(r, lw, k, v, a, b, ck, xinv, dy):
    L = r.shape[0]
    nc, npair = L // WKV_ROWS, HEADS // 2

    def body(r_ref, lw_ref, k_ref, v_ref, a_ref, b_ref, ck_ref, xi_ref, dy_ref, tri_ref, bd_ref,
             dr_ref, dlw_ref, dk_ref, dv_ref, da_ref, db_ref, ds_ref):
        @pl.when(pl.program_id(0) == 0)
        def _():
            ds_ref[...] = jnp.zeros_like(ds_ref)

        tri, bd = tri_ref[...], bd_ref[...]
        ins = [[ck_ref[0, p] for p in range(npair)]] + [_pairs(x) for x in (r_ref, lw_ref, k_ref, v_ref, a_ref, b_ref)]
        xs = [[xi_ref[0, c, p].astype(f32) for p in range(npair)] for c in range(WKV_SUB)]
        _, vjp = jax.vjp(lambda *t: _wkv_step(*t, tri, bd, xs)[:2], *ins)
        gs = vjp((_pairs(dy_ref), [ds_ref[p] for p in range(npair)]))
        for p in range(npair):
            ds_ref[p] = gs[0][p]
            for ref, gval in zip((dr_ref, dlw_ref, dk_ref, dv_ref, da_ref, db_ref), gs[1:]):
                ref[:, 2 * HEAD * p:2 * HEAD * (p + 1)] = gval[p]

    row = pl.BlockSpec((WKV_ROWS, RWKV_W), lambda c: (nc - 1 - c, 0))
    sspec = pl.BlockSpec((1, npair, 2 * HEAD, 2 * HEAD), lambda c: (nc - 1 - c, 0, 0, 0))
    xspec = pl.BlockSpec((1, WKV_SUB, npair, 2 * HEAD, 2 * HEAD), lambda c: (nc - 1 - c, 0, 0, 0, 0))
    return pl.pallas_call(
        body, name="wkv7_bwd", grid=(nc,),
        in_specs=[row] * 6 + [sspec, xspec, row, pl.BlockSpec((WKV_C, WKV_C), lambda c: (0, 0)),
                              pl.BlockSpec((2 * HEAD, 2 * HEAD), lambda c: (0, 0))],
        out_specs=[row] * 6,
        out_shape=[jax.ShapeDtypeStruct((L, RWKV_W), f32)] * 6,
        scratch_shapes=[pltpu.VMEM((npair, 2 * HEAD, 2 * HEAD), f32)],
        compiler_params=pltpu.CompilerParams(dimension_semantics=("arbitrary",), vmem_limit_bytes=VMEM_LIMIT),
    )(r, lw, k, v, a, b, ck, xinv, dy, *_wkv_consts())


def _cmul(ar, ai, xr, xi):
    return ar * xr - ai * xi, ar * xi + ai * xr


def _scan_init(a_ref, car_ref, pw_ref, reverse):
    car_ref[...] = jnp.zeros_like(car_ref)
    ar = jnp.broadcast_to(a_ref[:, :S5_N], (8, S5_N))
    ai = jnp.broadcast_to(a_ref[:, S5_N:], (8, S5_N))
    if reverse:
        ai = -ai
    row = lax.broadcasted_iota(jnp.int32, (8, S5_N), 0)
    pr, pi = ar, ai
    qr, qi = jnp.zeros((8, S5_N), f32), jnp.zeros((8, S5_N), f32)
    for e in range(1, 9):
        sel = (row == 8 - e) if reverse else (row == e - 1)
        qr, qi = jnp.where(sel, pr, qr), jnp.where(sel, pi, qi)
        if e in (1, 2, 4):
            j = (1, 2, 4).index(e)
            pw_ref[j, :, :S5_N] = pr
            pw_ref[j, :, S5_N:] = pi
        pr, pi = _cmul(pr, pi, ar, ai)
    pw_ref[3, :, :S5_N] = qr
    pw_ref[3, :, S5_N:] = qi


def _scan_tile(x_ref, o_ref, car_ref, pw_ref, reverse):
    ng = x_ref.shape[0] // 8
    row = lax.broadcasted_iota(jnp.int32, (8, S5_N), 0)

    def group(gi, carry):
        g = (ng - 1 - gi) if reverse else gi
        t0 = pl.multiple_of(g * 8, 8)
        xr, xi = x_ref[pl.ds(t0, 8), :S5_N], x_ref[pl.ds(t0, 8), S5_N:]
        for j, d in enumerate((1, 2, 4)):
            if reverse:
                sr = jnp.where(row < 8 - d, pltpu.roll(xr, 8 - d, axis=0), 0.0)
                si = jnp.where(row < 8 - d, pltpu.roll(xi, 8 - d, axis=0), 0.0)
            else:
                sr = jnp.where(row >= d, pltpu.roll(xr, d, axis=0), 0.0)
                si = jnp.where(row >= d, pltpu.roll(xi, d, axis=0), 0.0)
            mr, mi = _cmul(pw_ref[j, :, :S5_N], pw_ref[j, :, S5_N:], sr, si)
            xr, xi = xr + mr, xi + mi
        cr, ci = carry
        mr, mi = _cmul(pw_ref[3, :, :S5_N], pw_ref[3, :, S5_N:], cr, ci)
        xr, xi = xr + mr, xi + mi
        o_ref[pl.ds(t0, 8), :S5_N] = xr
        o_ref[pl.ds(t0, 8), S5_N:] = xi
        e = 0 if reverse else 7
        return (jnp.broadcast_to(xr[e:e + 1, :], (8, S5_N)), jnp.broadcast_to(xi[e:e + 1, :], (8, S5_N)))

    cr, ci = lax.fori_loop(0, ng, group, (car_ref[:, :S5_N], car_ref[:, S5_N:]))
    car_ref[:, :S5_N] = cr
    car_ref[:, S5_N:] = ci


_CB, _SB = 128, 512


def _cblk(k):
    return slice(_CB * k, _CB * (k + 1))


def _sblk(j):
    return slice(_SB * j, _SB * (j + 1))


def _s5_fwd(u, bmat, cmat, abar, late):
    L = u.shape[0]
    nt = L // S5_T
    names = list(late)
    nh = len(names)

    def body(u_ref, b_ref, c_ref, a_ref, *rest):
        h_in, (st_ref, y_ref), h_out = rest[:nh], rest[nh:nh + 2], rest[nh + 2:2 * nh + 2]
        bu_ref, car_ref, pw_ref, ssem, rsem, lsem = rest[2 * nh + 2:]
        i = pl.program_id(0)

        def copies():
            px, py, pc = _mesh_pos()
            me = 2 * px + py
            out = []
            for a, nm in enumerate(names):
                hr = late[nm].shape[0] // 2
                src, dst = h_in[a].at[pl.ds(pl.multiple_of(pc * hr, 16), hr), :], _slab(h_out[a], nm, me, pc)
                out.append(pltpu.make_async_copy(src, dst, lsem.at[a]))
                out += [pltpu.make_async_remote_copy(src, dst, ssem.at[3 * a + k], rsem.at[3 * a + k],
                                                     device_id=(qx, qy, pc), device_id_type=MESH)
                        for k, (qx, qy) in enumerate(_chip_peers(px, py))]
            return out

        @pl.when(i == 0)
        def _():
            _scan_init(a_ref, car_ref, pw_ref, False)
            for cp in copies():
                cp.start()

        for j in range(8):
            bu_ref[:, _sblk(j)] = _raw_bdot(u_ref[:, _cblk(j % 4)], b_ref[j], 'nn')
        _scan_tile(bu_ref, st_ref, car_ref, pw_ref, False)
        for k in range(4):
            y_ref[:, _cblk(k)] = (_raw_bdot(st_ref[:, _sblk(k)], c_ref[k], 'nn')
                                  + _raw_bdot(st_ref[:, _sblk(4 + k)], c_ref[4 + k], 'nn'))

        @pl.when(i == nt - 1)
        def _():
            for cp in copies():
                cp.wait()

    whole = lambda shape: pl.BlockSpec(shape, lambda i: (0,) * len(shape))
    outs = pl.pallas_call(
        body, name="s5_fwd", grid=(nt,),
        in_specs=[pl.BlockSpec((S5_T, S5_W), lambda i: (i, 0)), whole(bmat.shape), whole(cmat.shape), whole(abar.shape)]
        + [ANY] * nh,
        out_specs=[pl.BlockSpec((S5_T, 2 * S5_N), lambda i: (i, 0)), pl.BlockSpec((S5_T, S5_W), lambda i: (i, 0))] + [ANY] * nh,
        out_shape=[jax.ShapeDtypeStruct((L, 2 * S5_N), f32), jax.ShapeDtypeStruct((L, S5_W), f32)]
        + [jax.ShapeDtypeStruct(GATHER[nm][0], late[nm].dtype) for nm in names],
        scratch_shapes=[pltpu.VMEM((S5_T, 2 * S5_N), f32), pltpu.VMEM((8, 2 * S5_N), f32), pltpu.VMEM((4, 8, 2 * S5_N), f32),
                        pltpu.SemaphoreType.DMA((3 * nh,)), pltpu.SemaphoreType.DMA((3 * nh,)), pltpu.SemaphoreType.DMA((nh,))],
        compiler_params=pltpu.CompilerParams(dimension_semantics=("arbitrary",), vmem_limit_bytes=VMEM_LIMIT),
    )(u, bmat, cmat, abar, *[late[nm] for nm in names])
    return outs[0], outs[1], dict(zip(names, outs[2:]))


def _s5_bwd(dy, st, u, du_direct, bmat, cmat, abar, chip_sum):
    L = u.shape[0]
    nt = L // S5_T
    nb8 = S5_T // 8
    names = list(chip_sum)
    nh = len(names)

    def body(dy_ref, st_ref, sp_ref, u_ref, dud_ref, b_ref, c_ref, a_ref, *rest):
        x_in, (du_ref, db_ref, dc_ref, da_ref), x_out = rest[:nh], rest[nh:nh + 4], rest[nh + 4:2 * nh + 4]
        lam_ref, car_ref, pw_ref, ssem, rsem = rest[2 * nh + 4:]
        i = pl.program_id(0)

        @pl.when(i == 0)
        def _():
            _scan_init(a_ref, car_ref, pw_ref, True)
            db_ref[...] = jnp.zeros_like(db_ref)
            dc_ref[...] = jnp.zeros_like(dc_ref)
            da_ref[...] = jnp.zeros_like(da_ref)
            for cp in _exchange_copies(x_in, x_out, ssem, rsem):
                cp.start()

        for j in range(8):
            lam_ref[:, _sblk(j)] = _raw_bdot(dy_ref[:, _cblk(j % 4)], c_ref[j], 'nt')
        _scan_tile(lam_ref, lam_ref, car_ref, pw_ref, True)
        for k in range(4):
            du_ref[:, _cblk(k)] = (dud_ref[:, _cblk(k)] + _raw_bdot(lam_ref[:, _sblk(k)], b_ref[k], 'nt')
                                   + _raw_bdot(lam_ref[:, _sblk(4 + k)], b_ref[4 + k], 'nt')
                                   ).astype(du_ref.dtype)
            sr = _shift_down(st_ref[:, _sblk(k)], sp_ref[:, _sblk(k)], nt - 1 - i, 1)
            si = _shift_down(st_ref[:, _sblk(4 + k)], sp_ref[:, _sblk(4 + k)], nt - 1 - i, 1)
            lr, li = lam_ref[:, _sblk(k)], lam_ref[:, _sblk(4 + k)]
            da_ref[:, _sblk(k)] += _sum0(lr * sr + li * si)
            da_ref[:, _sblk(4 + k)] += _sum0(li * sr - lr * si)
        for j in range(8):
            db_ref[j] += _raw_bdot(u_ref[:, _cblk(j % 4)], lam_ref[:, _sblk(j)], 'tn')
            dc_ref[j] += _raw_bdot(st_ref[:, _sblk(j)], dy_ref[:, _cblk(j % 4)], 'tn')

        @pl.when(i == nt - 1)
        def _():
            for cp in _exchange_copies(x_in, x_out, ssem, rsem):
                cp.wait()

    whole = lambda shape: pl.BlockSpec(shape, lambda i: (0,) * len(shape))
    rev = lambda i: (nt - 1 - i, 0)
    outs = pl.pallas_call(
        body, name="s5_bwd", grid=(nt,),
        in_specs=[pl.BlockSpec((S5_T, S5_W), rev), pl.BlockSpec((S5_T, 2 * S5_N), rev),
                  pl.BlockSpec((8, 2 * S5_N), lambda i: (jnp.maximum((nt - 1 - i) * nb8 - 1, 0), 0)),
                  pl.BlockSpec((S5_T, S5_W), rev), pl.BlockSpec((S5_T, S5_W), rev), whole(bmat.shape), whole(cmat.shape),
                  whole(abar.shape)] + [ANY] * nh,
        out_specs=[pl.BlockSpec((S5_T, S5_W), rev), whole((8, _CB, _SB)), whole((8, _SB, _CB)), whole((1, 2 * S5_N))]
        + [ANY] * nh,
        out_shape=[jax.ShapeDtypeStruct((L, S5_W), bf16), jax.ShapeDtypeStruct((8, _CB, _SB), f32),
                   jax.ShapeDtypeStruct((8, _SB, _CB), f32), jax.ShapeDtypeStruct((1, 2 * S5_N), f32)]
        + [jax.ShapeDtypeStruct(chip_sum[nm].shape, chip_sum[nm].dtype) for nm in names],
        scratch_shapes=[pltpu.VMEM((S5_T, 2 * S5_N), f32), pltpu.VMEM((8, 2 * S5_N), f32), pltpu.VMEM((4, 8, 2 * S5_N), f32),
                        pltpu.SemaphoreType.DMA((3 * nh,)), pltpu.SemaphoreType.DMA((3 * nh,))],
        compiler_params=pltpu.CompilerParams(dimension_semantics=("arbitrary",), vmem_limit_bytes=VMEM_LIMIT),
    )(dy, st, st, u, du_direct, bmat, cmat, abar, *[chip_sum[nm] for nm in names])
    return outs[0], outs[1], outs[2], outs[3], dict(zip(names, outs[4:]))


def _s5_disc_fwd(a_re, a_im, ls, b_re, b_im):
    def body(a_re_ref, a_im_ref, ls_ref, b_re_ref, b_im_ref, ar_ref, ai_ref, br_ref, bi_ref):
        outs = _s5_disc(a_re_ref[...], a_im_ref[...], ls_ref[...], b_re_ref[...], b_im_ref[...])
        for ref, v in zip((ar_ref, ai_ref, br_ref, bi_ref), outs):
            ref[...] = v

    c1, c16 = jax.ShapeDtypeStruct((S5_N, 1), f32), jax.ShapeDtypeStruct((S5_N, S5_C), f32)
    return pl.pallas_call(body, name="s5_disc", out_shape=[c1, c1, c16, c16])(a_re, a_im, ls, b_re, b_im)


def _s5_disc_bwd(a_re, a_im, ls, b_re, b_im, d_ar, d_ai, d_br, d_bi, seg):
    def body(a_re_ref, a_im_ref, ls_ref, b_re_ref, b_im_ref, g1, g2, g3, g4, seg_ref, o1, o2, o3, o4, o5):
        _, vjp = jax.vjp(_s5_disc, a_re_ref[...], a_im_ref[...], ls_ref[...], b_re_ref[...], b_im_ref[...])
        da_re, da_im, dls, db_re, db_im = vjp((g1[...], g2[...], g3[...], g4[...]))
        o1[...] = da_re
        o2[...] = da_im
        o3[...] = _dot32(seg_ref[...], dls)
        o4[...] = db_re
        o5[...] = db_im

    c1, c16 = jax.ShapeDtypeStruct((S5_N, 1), f32), jax.ShapeDtypeStruct((S5_N, S5_C), f32)
    return pl.pallas_call(body, name="s5_disc_bwd", out_shape=[c1, c1, jax.ShapeDtypeStruct((S5_G, 1), f32), c16, c16])(
        a_re, a_im, ls, b_re, b_im, d_ar, d_ai, d_br, d_bi, seg)


ANY = pl.BlockSpec(memory_space=pl.ANY)

GATHER = {'w_in': ((4352, 1024), 0), 'ffn_w_up': ((1024, 5632), 1), 'w_branch_rwkv': ((512, 1024), 1),
          'w_branch_s5': ((512, 1024), 1), 'w_out': ((1024, 1024), 0), 's5_w_glu': ((512, 512), 0),
          'ffn_w_down': ((2816, 1024), 0), 'rwkv_w2': ((64, 512), 1), 'rwkv_a2': ((64, 512), 1),
          'rwkv_g2': ((128, 512), 1), 'ffn_conv_w': ((8, 5632), 1)}
BIG = ['w_in', 'ffn_w_up', 'w_branch_rwkv', 'w_branch_s5', 'w_out', 's5_w_glu', 'ffn_w_down']
TINY = ['rwkv_w2', 'rwkv_a2', 'rwkv_g2', 'ffn_conv_w']
SMALL = [n for n in WEIGHTS if n not in GATHER]
SMALL_ROWS = 320
ADAM_ROWS = 256


def _mo(v, m):
    return v if isinstance(v, int) else pl.multiple_of(v, m)


def _slab(ref, name, j, h=None):
    (R, Cn), axis = GATHER[name]
    if axis == 0:
        rs = R // 4
        if h is None:
            return ref.at[pl.ds(_mo(j * rs, 16), rs), :]
        return ref.at[pl.ds(_mo(j * rs + h * (rs // 2), 8), rs // 2), :]
    cols = pl.ds(_mo(j * (Cn // 4), 128), Cn // 4)
    if h is None:
        return ref.at[:, cols]
    return ref.at[pl.ds(_mo(h * (R // 2), 8), R // 2), cols]


def _half_shape(name):
    (R, Cn), axis = GATHER[name]
    return (R // 8, Cn) if axis == 0 else (R // 2, Cn // 4)


def _chip_peers(px, py):
    return [((1 - px) if (k >> 1) else px, (1 - py) if (k & 1) else py) for k in (1, 2, 3)]


def _run_copies(copies):
    for cp in copies:
        cp.start()
    for cp in copies:
        cp.wait()


def _gather_weights(blocks):
    names = list(blocks)
    n = len(names)

    def body(*refs):
        ins, outs = refs[:n], refs[n:2 * n]
        ssem, rsem, lsem = refs[2 * n:]
        px, py, pc = _mesh_pos()
        me = 2 * px + py
        copies = []
        for i, nm in enumerate(names):
            if nm in BIG:
                hr = blocks[nm].shape[0] // 2
                src, dst = ins[i].at[pl.ds(pl.multiple_of(pc * hr, 16), hr), :], _slab(outs[i], nm, me, pc)
            else:
                src, dst = ins[i], _slab(outs[i], nm, me)
            copies.append(pltpu.make_async_copy(src, dst, lsem.at[i]))
            for k, (qx, qy) in enumerate(_chip_peers(px, py)):
                copies.append(pltpu.make_async_remote_copy(src, dst, ssem.at[3 * i + k], rsem.at[3 * i + k],
                                                           device_id=(qx, qy, pc), device_id_type=MESH))
        _run_copies(copies)

    outs = pl.pallas_call(
        body, name="gather_weights", in_specs=[ANY] * n, out_specs=[ANY] * n,
        out_shape=[jax.ShapeDtypeStruct(GATHER[nm][0], blocks[nm].dtype) for nm in names],
        scratch_shapes=[pltpu.SemaphoreType.DMA((3 * n,)), pltpu.SemaphoreType.DMA((3 * n,)), pltpu.SemaphoreType.DMA((n,))],
    )(*[blocks[nm] for nm in names])
    return dict(zip(names, outs))


def _gather_pair(full, names, call_name):
    n = len(names)

    def body(*refs):
        ins, outs = refs[:n], refs[n:2 * n]
        ssem, rsem = refs[2 * n:]
        px, py, pc = _mesh_pos()
        copies = []
        for i, nm in enumerate(names):
            for j in range(4):
                copies.append(pltpu.make_async_remote_copy(_slab(ins[i], nm, j, pc), _slab(outs[i], nm, j, pc),
                                                           ssem.at[4 * i + j], rsem.at[4 * i + j],
                                                           device_id=(px, py, 1 - pc), device_id_type=MESH))
        _run_copies(copies)

    outs = pl.pallas_call(
        body, name=call_name, in_specs=[ANY] * n, out_specs=[ANY] * n,
        out_shape=[jax.ShapeDtypeStruct(full[nm].shape, full[nm].dtype) for nm in names],
        input_output_aliases={i: i for i in range(n)},
        scratch_shapes=[pltpu.SemaphoreType.DMA((4 * n,)), pltpu.SemaphoreType.DMA((4 * n,))],
    )(*[full[nm] for nm in names])
    return dict(zip(names, outs))


def _grads_to_sibling(G, names, call_name, small=None):
    n = len(names)
    ns = 0 if small is None else 1

    def body(*refs):
        g_refs, o_refs = refs[:n + ns], refs[n + ns:2 * (n + ns)]
        ssem, rsem = refs[2 * (n + ns):]
        px, py, pc = _mesh_pos()
        sib = (px, py, 1 - pc)
        copies = []
        for i, nm in enumerate(names):
            for j in range(4):
                copies.append(pltpu.make_async_remote_copy(_slab(g_refs[i], nm, j, 1 - pc), o_refs[i].at[j],
                                                           ssem.at[4 * i + j], rsem.at[4 * i + j],
                                                           device_id=sib, device_id_type=MESH))
        if ns:
            copies.append(pltpu.make_async_remote_copy(g_refs[n], o_refs[n], ssem.at[4 * n], rsem.at[4 * n],
                                                       device_id=sib, device_id_type=MESH))
        _run_copies(copies)

    outs = pl.pallas_call(
        body, name=call_name, in_specs=[ANY] * (n + ns), out_specs=[ANY] * (n + ns),
        out_shape=[jax.ShapeDtypeStruct((4,) + _half_shape(nm), f32) for nm in names]
        + [jax.ShapeDtypeStruct((SMALL_ROWS, PACK_W), f32)] * ns,
        scratch_shapes=[pltpu.SemaphoreType.DMA((4 * n + ns,)), pltpu.SemaphoreType.DMA((4 * n + ns,))],
    )(*[G[nm] for nm in names], *([small] * ns))
    return dict(zip(names, outs[:n])), (outs[n] if ns else None)


def _pair_add(G, recv, names, call_name, small=None, small_recv=None):
    n = len(names)
    ns = 0 if small is None else 1
    cidx = lax.axis_index("c").astype(jnp.int32).reshape(1)

    def body(c_ref, *refs):
        ins, outs = refs[:2 * (n + ns)], refs[2 * (n + ns):]
        for i in range(n):
            outs[i][...] = (ins[i][...] + ins[n + ns + i][...]).astype(bf16)
        if ns:
            outs[n][...] = ins[n][...] + ins[2 * n + 1][...]

    g_specs, r_specs = [], []
    for nm in names:
        hr, hc = _half_shape(nm)
        if GATHER[nm][1] == 0:
            g_specs.append(pl.BlockSpec((hr // 2, hc), lambda j, i, c: ((2 * j + c[0]) * 2 + i, 0)))
        else:
            g_specs.append(pl.BlockSpec((hr // 2, hc), lambda j, i, c: (2 * c[0] + i, j)))
        r_specs.append(pl.BlockSpec((1, hr // 2, hc), lambda j, i, c: (j, i, 0)))
    sm = [pl.BlockSpec((SMALL_ROWS // 8, PACK_W), lambda j, i, c: (2 * j + i, 0))] * ns
    outs = pl.pallas_call(
        body, name=call_name,
        grid_spec=pltpu.PrefetchScalarGridSpec(num_scalar_prefetch=1, grid=(4, 2), in_specs=g_specs + sm + r_specs + sm,
                                               out_specs=r_specs + sm),
        out_shape=[jax.ShapeDtypeStruct((4,) + _half_shape(nm), bf16) for nm in names]
        + [jax.ShapeDtypeStruct((SMALL_ROWS, PACK_W), f32)] * ns,
        compiler_params=pltpu.CompilerParams(vmem_limit_bytes=VMEM_LIMIT),
    )(cidx, *[G[nm] for nm in names], *([small] * ns), *[recv[nm] for nm in names], *([small_recv] * ns))
    return dict(zip(names, outs[:n])), (outs[n] if ns else None)


def _exchange_copies(ins, outs, ssem, rsem):
    px, py, pc = _mesh_pos()
    me = 2 * px + py
    return [pltpu.make_async_remote_copy(ins[i].at[2 * qx + qy], outs[i].at[me], ssem.at[3 * i + k], rsem.at[3 * i + k],
                                         device_id=(qx, qy, pc), device_id_type=MESH)
            for i in range(len(ins)) for k, (qx, qy) in enumerate(_chip_peers(px, py))]


def _grads_chip_exchange(chip_sum, names, small):
    n = len(names)

    def body(*refs):
        ins, outs = refs[:n + 1], refs[n + 1:2 * n + 2]
        ssem, rsem, ssem_s, rsem_s = refs[2 * n + 2:]
        px, py, pc = _mesh_pos()
        me = 2 * px + py
        copies = _exchange_copies(ins[:n], outs[:n], ssem, rsem)
        hs = SMALL_ROWS // 2
        mine = ins[n].at[pl.ds(pl.multiple_of(pc * hs, 8), hs), :]
        copies += [pltpu.make_async_remote_copy(mine, outs[n].at[me], ssem_s.at[k], rsem_s.at[k],
                                                device_id=(qx, qy, pc), device_id_type=MESH)
                   for k, (qx, qy) in enumerate(_chip_peers(px, py))]
        _run_copies(copies)

    outs = pl.pallas_call(
        body, name="grads_chip_exchange", in_specs=[ANY] * (n + 1), out_specs=[ANY] * (n + 1),
        out_shape=[jax.ShapeDtypeStruct(chip_sum[nm].shape, chip_sum[nm].dtype) for nm in names]
        + [jax.ShapeDtypeStruct((4, SMALL_ROWS // 2, PACK_W), f32)],
        scratch_shapes=[pltpu.SemaphoreType.DMA((3 * n,)), pltpu.SemaphoreType.DMA((3 * n,)),
                        pltpu.SemaphoreType.DMA((3,)), pltpu.SemaphoreType.DMA((3,))],
    )(*[chip_sum[nm] for nm in names], small)
    return dict(zip(names, outs[:n])), outs[n]


def _sum_slots(slots, chip_sum, small4, small_own):
    n = len(BIG)
    me = jnp.stack([2 * lax.axis_index("x") + lax.axis_index("y"), lax.axis_index("c")]).astype(jnp.int32)

    def body(me_ref, *refs):
        for i in range(n + 1):
            own = refs[5 * i + 4][...].astype(f32)
            own = own[0] if i < n else own
            term = [jnp.where(me_ref[0] == k, own, refs[5 * i + k][0].astype(f32)) for k in range(4)]
            refs[5 * (n + 1) + i][...] = ((term[0] + term[1]) + term[2]) + term[3]

    redirect = lambda k: (lambda i, m: (jnp.where(m[0] == k, (k + 1) % 4, k), i, 0))
    in_specs, args, specs_out, shapes = [], [], [], []
    for nm in BIG:
        hr, hc = _half_shape(nm)
        in_specs += [pl.BlockSpec((1, hr // 2, hc), redirect(k)) for k in range(4)]
        in_specs.append(pl.BlockSpec((1, hr // 2, hc), lambda i, m: (m[0], i, 0)))
        args += [slots[nm]] * 4 + [chip_sum[nm]]
        specs_out.append(pl.BlockSpec((hr // 2, hc), lambda i, m: (i, 0)))
        shapes.append(jax.ShapeDtypeStruct((hr, hc), f32))
    in_specs += [pl.BlockSpec((1, SMALL_ROWS // 4, PACK_W), redirect(k)) for k in range(4)]
    in_specs.append(pl.BlockSpec((SMALL_ROWS // 4, PACK_W), lambda i, m: (2 * m[1] + i, 0)))
    args += [small4] * 4 + [small_own]
    specs_out.append(pl.BlockSpec((SMALL_ROWS // 4, PACK_W), lambda i, m: (i, 0)))
    shapes.append(jax.ShapeDtypeStruct((SMALL_ROWS // 2, PACK_W), f32))
    outs = pl.pallas_call(
        body, name="grads_chip_sum",
        grid_spec=pltpu.PrefetchScalarGridSpec(num_scalar_prefetch=1, grid=(2,), in_specs=in_specs, out_specs=specs_out),
        out_shape=shapes, compiler_params=pltpu.CompilerParams(vmem_limit_bytes=VMEM_LIMIT),
    )(me, *args)
    return dict(zip(BIG, outs[:n])), outs[n]


def _halves_to_sibling(half):
    names = list(half)
    n = len(names)

    def body(*refs):
        ins, outs = refs[:n], refs[n:2 * n]
        ssem, rsem = refs[2 * n:]
        px, py, pc = _mesh_pos()
        _run_copies([pltpu.make_async_remote_copy(ins[i], outs[i], ssem.at[i], rsem.at[i],
                                                  device_id=(px, py, 1 - pc), device_id_type=MESH) for i in range(n)])

    outs = pl.pallas_call(
        body, name="grads_halves_to_sibling", in_specs=[ANY] * n, out_specs=[ANY] * n,
        out_shape=[jax.ShapeDtypeStruct(half[nm].shape, f32) for nm in names],
        scratch_shapes=[pltpu.SemaphoreType.DMA((n,)), pltpu.SemaphoreType.DMA((n,))],
    )(*[half[nm] for nm in names])
    return dict(zip(names, outs))


def _join_halves(mine, other, pc):
    hr = mine.shape[0]
    return lax.dynamic_slice_in_dim(jnp.concatenate([other, mine, other], axis=0), (1 - pc) * hr, 2 * hr, axis=0)


def _flat_pad(v):
    v = v.reshape(-1)
    return jnp.pad(v, (0, _ceil_to(v.shape[0], PACK_W) - v.shape[0]))


def _pack_rows(parts, rows):
    flat = jnp.concatenate([_flat_pad(p) for p in parts])
    return jnp.pad(flat, (0, rows * PACK_W - flat.shape[0])).reshape(rows, PACK_W)


def _unpack_rows(buf, shapes):
    flat = buf.reshape(-1)
    out, off = [], 0
    for shp in shapes:
        n = 1
        for d in shp:
            n *= d
        out.append(flat[off:off + n].reshape(shp))
        off += _ceil_to(n, PACK_W)
    return out


def _adamw_math(w_, g_, m_, v_):
    m2 = ADAM_B1 * m_ + (1.0 - ADAM_B1) * g_
    v2 = ADAM_B2 * v_ + (1.0 - ADAM_B2) * (g_ * g_)
    m_hat = m2 / (1.0 - ADAM_B1 ** ADAM_STEP)
    v_hat = v2 / (1.0 - ADAM_B2 ** ADAM_STEP)
    return -ADAM_LR * (m_hat / (jnp.sqrt(v_hat) + ADAM_EPS) + ADAM_WD * w_), m2, v2


def _adamw(groups):
    ng = len(groups)

    def body(*refs):
        ins, outs = refs[:4 * ng], refs[4 * ng:]
        for i in range(ng):
            res = _adamw_math(*(r[...] for r in ins[4 * i:4 * i + 4]))
            for ref, val in zip(outs[3 * i:3 * i + 3], res):
                ref[...] = val

    in_specs, out_specs, out_shape = [], [], []
    for grp in groups:
        R, Cn = grp[0].shape
        spec = pl.BlockSpec((R // 8, Cn), lambda i: (i, 0))
        in_specs += [spec] * 4
        out_specs += [spec] * 3
        out_shape += [jax.ShapeDtypeStruct((R, Cn), f32)] * 3
    outs = pl.pallas_call(
        body, name="adamw", grid=(8,), in_specs=in_specs, out_specs=out_specs, out_shape=out_shape,
        compiler_params=pltpu.CompilerParams(vmem_limit_bytes=VMEM_LIMIT),
    )(*[a for grp in groups for a in grp])
    return [tuple(outs[3 * i:3 * i + 3]) for i in range(ng)]


def _forward_backward(x, tgt, W, S, late):
    L = x.shape[0]
    TM, TMW, TS = 256, 128, 512
    row = lambda c, dt=f32: (c, dt)
    hid = jnp.arange(RWKV_W) // HEAD
    E = (hid[:, None] == hid[None, :]).astype(f32)
    seg = (jnp.arange(S5_N)[None, :] // S5_P == jnp.arange(S5_G)[:, None]).astype(f32)

    w_in_t = W['w_in']
    w_p, w_u, w_g = w_in_t[:N_RWKV], w_in_t[N_RWKV:N_RWKV + S5_W], w_in_t[N_RWKV + S5_W:]
    zpad = jnp.zeros((64, RWKV_W), f32)
    w2p = jnp.concatenate([W['rwkv_w2'], zpad], axis=0)
    a2p = jnp.concatenate([zpad, W['rwkv_a2']], axis=0)
    g2 = W['rwkv_g2']
    prep_consts = [S['rwkv_shift_mu'], S['rwkv_w0'], S['rwkv_a0'], S['rwkv_k_k'], S['rwkv_k_a'], w2p, a2p, g2, E]
    out_consts = [S['rwkv_lnx_w'], S['rwkv_lnx_b'], S['rwkv_r_k'], E]
    cw, cb = W['ffn_conv_w'][:3], S['ffn_conv_b']

    a_re, a_im = S['s5_a_re'].reshape(S5_N, 1), S['s5_a_im'].reshape(S5_N, 1)
    ls = jnp.repeat(S['s5_log_step'].reshape(S5_G, 1), S5_P, axis=0)
    b_re, b_im = S['s5_b_re'].reshape(S5_N, S5_C), S['s5_b_im'].reshape(S5_N, S5_C)
    ar, ai, bbr, bbi = _s5_disc_fwd(a_re, a_im, ls, b_re, b_im)
    abar = jnp.concatenate([ar.reshape(1, S5_N), ai.reshape(1, S5_N)], axis=1)
    eye8 = jnp.eye(8, dtype=f32)

    def blocks_in(bb):
        t = bb.reshape(4, 8, S5_P, S5_C).transpose(0, 1, 3, 2)
        return (t[:, :, :, None, :] * eye8[None, :, None, :, None]).reshape(4, _CB, _SB)

    def blocks_out(cc):
        t = cc.reshape(4, 8, S5_C, S5_P).transpose(0, 1, 3, 2)
        return (t[:, :, :, None, :] * eye8[None, :, None, :, None]).reshape(4, _SB, _CB)

    def undiag_in(blocks):
        t = blocks.reshape(4, 8, S5_C, 8, S5_P)
        t = jnp.sum(t * eye8[None, :, None, :, None], axis=3)
        return t.reshape(S5_G, S5_C, S5_P).transpose(0, 2, 1).reshape(S5_N, S5_C)

    def undiag_out(blocks):
        t = blocks.reshape(4, 8, S5_P, 8, S5_C)
        t = jnp.sum(t * eye8[None, :, None, :, None], axis=3)
        return t.reshape(S5_G, S5_P, S5_C).transpose(0, 2, 1)

    bmat = jnp.concatenate([blocks_in(bbr), blocks_in(bbi)], axis=0).astype(bf16)
    cmat = jnp.concatenate([blocks_out(S['s5_c_re'].reshape(S5_G, S5_C, S5_P)),
                            -blocks_out(S['s5_c_im'].reshape(S5_G, S5_C, S5_P))], axis=0).astype(bf16)

    g1, g2n, g3, g4 = S['norm_mix_pre'], S['norm_mix_post'], S['norm_ffn_pre'], S['norm_ffn_post']
    h1, p, u, gp = _rows_mm("in_proj", lambda R, C: _rms(R[0], C[0]), L, TM, [x], [g1], w_in_t, 'nt',
                            [N_RWKV, S5_W, 2 * D_MODEL])

    def prep_fn(i, n, R, P, X, C):
        q = R[0] + (_shift_down(R[0], P[0], i, 1) - R[0]) * C[0]
        return _prep(q, *C[1:]), ()

    r, lw, k2, v, an, bv, g = _rowcall("rwkv_prep", prep_fn, L, TS, [p], prep_consts,
                                       out_rows=[row(RWKV_W)] * 7, prev=[0])
    y, ck, xinv = _wkv7_fwd(r, lw, k2, v, an, bv)
    o_a, o_r = _rows_mm("rwkv_out", lambda R, C: _rwkv_out(*R, *C), L, TS, [y, r, k2, v, g], out_consts,
                        W['w_branch_rwkv'], 'nn', [D_MODEL])

    st, ysc, got = _s5_fwd(u, bmat, cmat, abar, late)
    W = {**W, **_gather_pair(got, list(got), "gather_weights_pair_late")}
    (yg,) = _rowcall("s5_mid", lambda i, n, R, P, X, C: ((_s5_mid(*R, *C),), ()), L, TS, [ysc, u], [S['s5_d']],
                     out_rows=[row(S5_W)])
    z2 = _mm(yg, W['s5_w_glu'], 'nn', "mm_glu")
    o_b, o_s = _rows_mm("s5_glu", lambda R, C: _s5_glu(*R, *C), L, TS, [yg, z2], [S['s5_b_glu']],
                        W['w_branch_s5'], 'nn', [D_MODEL])

    merged, mixed = _rows_mm("merge_out", lambda R, C: _merge(*R, *C), L, TS, [gp, o_r, o_s], [S['b_gate']],
                             W['w_out'], 'nn', [D_MODEL])

    def resid_fn(R, C):
        x1_ = R[0] + _rms(R[1], C[0])
        return _rms(x1_, C[1]), [x1_]

    h2, x1, z = _rows_mm("resid_up", resid_fn, L, TM, [x, mixed], [g2n, g3], W['ffn_w_up'], 'nn', [2 * D_FF],
                         extra=[row(D_MODEL)])

    def conv(zt, zprev, i, cw_, cb_):
        z2s, z1s = _shift_down(zt, zprev, i, 2), _shift_down(zt, zprev, i, 1)
        return cb_ + cw_[0:1] * z2s + cw_[1:2] * z1s + cw_[2:3] * zt, z2s, z1s

    (act,) = _rowcall("conv_act", lambda i, n, R, P, X, C: ((_act(conv(R[0], P[0], i, C[0], C[1])[0]),), ()), L, TMW,
                      [z], [cw, cb], out_rows=[row(D_FF, bf16)], prev=[0])
    f = _mm(act, W['ffn_w_down'], 'nn', "mm_down")

    def final_fn(i, n, R, P, X, C):
        x1_, f_, t_ = R
        fn_, vjp = jax.vjp(_rms, f_, C[0])
        diff = x1_ + fn_ - t_
        loss = jnp.sum(diff * diff) * (0.5 / D_MODEL)
        dx2_ = diff * (1.0 / D_MODEL)
        df_, dg4_ = vjp(dx2_)
        return (df_, dx2_), (jnp.full((1, PACK_W), loss, f32), dg4_)

    df, dx2, loss, dg4 = _rowcall("loss_head", final_fn, L, TS, [x1, f, tgt], [g4],
                                  out_rows=[row(D_MODEL, bf16), row(D_MODEL)], out_accs=[(1, PACK_W), (1, D_MODEL)])
    G = {'norm_ffn_post': dg4}

    dact = _mm(df, W['ffn_w_down'], 'nt', "mm_down_dx")
    G['ffn_w_down'] = _mm(act, df, 'tn', "mm_down_dw")

    def conv_bwd_fn(i, n, R, P, X, C):
        z_, dact_ = R
        cw_, cb_ = C
        zc, z2s, z1s = conv(z_, P[0], i, cw_, cb_)
        _, vjp = jax.vjp(_act, zc)
        (dzc_,) = vjp(dact_)
        last8 = z_[z_.shape[0] - 8:]
        zcn = cb_ + cw_[0:1] * _shift_down(X[0], last8, 1, 2) + cw_[1:2] * _shift_down(X[0], last8, 1, 1) + cw_[2:3] * X[0]
        _, vjpn = jax.vjp(_act, zcn)
        (dzcn,) = vjpn(X[1])
        dz_ = (cw_[2:3] * dzc_ + cw_[1:2] * _shift_up(dzc_, dzcn, i, n, 1) + cw_[0:1] * _shift_up(dzc_, dzcn, i, n, 2))
        return (dz_,), (_sum0(dzc_), _sum0(dzc_ * z2s), _sum0(dzc_ * z1s), _sum0(dzc_ * z_))

    wide = (1, 2 * D_FF)
    dz, dcb, dcw0, dcw1, dcw2 = _rowcall("conv_act_bwd", conv_bwd_fn, L, TMW, [z, dact], [cw, cb],
                                         out_rows=[row(2 * D_FF, bf16)], out_accs=[wide] * 4, prev=[0], nxt=[0, 1])
    G['ffn_conv_b'] = dcb
    G['ffn_conv_w'] = jnp.concatenate([dcw0, dcw1, dcw2], axis=0)
    dh2 = _mm(dz, W['ffn_w_up'], 'nt', "mm_up_dx")
    G['ffn_w_up'] = _mm(h2, dz, 'tn', "mm_up_dw")

    def norm2_bwd_fn(i, n, R, P, X, C):
        x1_, mixed_, dx2_, dh2_ = R
        _, vjp3 = jax.vjp(_rms, x1_, C[1])
        dx1a, dg3_ = vjp3(dh2_)
        dx1_ = dx2_ + dx1a
        _, vjp2 = jax.vjp(_rms, mixed_, C[0])
        dmixed_, dg2_ = vjp2(dx1_)
        return (dx1_, dmixed_), (dg2_, dg3_)

    dx1, dmixed, dg2n, dg3 = _rowcall("norm_mid_bwd", norm2_bwd_fn, L, TS, [x1, mixed, dx2, dh2], [g2n, g3],
                                      out_rows=[row(D_MODEL), row(D_MODEL, bf16)], out_accs=[(1, D_MODEL)] * 2)
    G['norm_mix_post'], G['norm_ffn_pre'] = dg2n, dg3

    dmerged = _mm(dmixed, W['w_out'], 'nt', "mm_out_dx")
    G['w_out'] = _mm(merged, dmixed, 'tn', "mm_out_dw")

    def merge_bwd_fn(i, n, R, P, X, C):
        _, vjp = jax.vjp(_merge, R[0], R[1], R[2], C[0])
        dgp_, do_r_, do_s_, dbg_ = vjp(R[3])
        return (dgp_, do_r_, do_s_), (dbg_,)

    dgp, do_r, do_s, G['b_gate'] = _rowcall("merge_bwd", merge_bwd_fn, L, TS, [gp, o_r, o_s, dmerged], [S['b_gate']],
                                            out_rows=[row(2 * D_MODEL, bf16), row(D_MODEL, bf16), row(D_MODEL, bf16)],
                                            out_accs=[(1, 2 * D_MODEL)])
    do_a = _mm(do_r, W['w_branch_rwkv'], 'nt', "mm_br_dx")
    G['w_branch_rwkv'] = _mm(o_a, do_r, 'tn', "mm_br_dw")
    do_b = _mm(do_s, W['w_branch_s5'], 'nt', "mm_bs_dx")
    G['w_branch_s5'] = _mm(o_b, do_s, 'tn', "mm_bs_dw")

    def glu_bwd_fn(i, n, R, P, X, C):
        _, vjp = jax.vjp(_s5_glu, R[0], R[1], C[0])
        dyg1_, dz2_, dbg_ = vjp(R[2])
        return (dyg1_, dz2_), (dbg_,)

    dyg1, dz2, G['s5_b_glu'] = _rowcall("s5_glu_bwd", glu_bwd_fn, L, TS, [yg, z2, do_b], [S['s5_b_glu']],
                                        out_rows=[row(S5_W), row(S5_W, bf16)], out_accs=[(1, S5_W)])
    dyg2 = _mm(dz2, W['s5_w_glu'], 'nt', "mm_glu_dx")
    G['s5_w_glu'] = _mm(yg, dz2, 'tn', "mm_glu_dw")

    def mid_bwd_fn(i, n, R, P, X, C):
        _, vjp = jax.vjp(_s5_mid, R[0], R[1], C[0])
        dysc_, du_, dd_ = vjp(R[2] + R[3])
        return (dysc_, du_), (dd_,)

    dysc, du1, G['s5_d'] = _rowcall("s5_mid_bwd", mid_bwd_fn, L, TS, [ysc, u, dyg1, dyg2], [S['s5_d']],
                                    out_rows=[row(S5_W, bf16), row(S5_W)], out_accs=[(1, S5_W)])
    early = [n for n in BIG if n != 'w_in']
    recv_e, _ = _grads_to_sibling(G, early, "grads_to_sibling_early")
    chip_e, _ = _pair_add(G, recv_e, early, "grads_pair_sum_early")
    du, dbmat, dcmat, dabar, slots_e = _s5_bwd(dysc, st, u, du1, bmat, cmat, abar, chip_e)
    da_re, da_im, dls, db_re, db_im = _s5_disc_bwd(
        a_re, a_im, ls, b_re, b_im, dabar[:, :S5_N].reshape(S5_N, 1), dabar[:, S5_N:].reshape(S5_N, 1),
        undiag_in(dbmat[:4]), undiag_in(dbmat[4:]), seg)
    G['s5_a_re'], G['s5_a_im'], G['s5_log_step'] = da_re, da_im, dls
    G['s5_b_re'], G['s5_b_im'] = db_re, db_im
    G['s5_c_re'], G['s5_c_im'] = undiag_out(dcmat[:4]), -undiag_out(dcmat[4:])

    def out_bwd_fn(i, n, R, P, X, C):
        _, vjp = jax.vjp(_rwkv_out, *R[:5], *C)
        gs = vjp(R[5])
        return gs[:5], gs[5:8]

    dy, dr1, dk1, dv1, dg, dlw, dlb, drk = _rowcall("rwkv_out_bwd", out_bwd_fn, L, TM, [y, r, k2, v, g, do_a], out_consts,
                                                    out_rows=[row(RWKV_W)] * 5, out_accs=[(1, RWKV_W)] * 3)
    G['rwkv_lnx_w'], G['rwkv_lnx_b'], G['rwkv_r_k'] = dlw, dlb, drk
    dr2, dlwk, dk2b, dv2, dan, dbv = _wkv7_bwd(r, lw, k2, v, an, bv, ck, xinv, dy)

    def prep_bwd_fn(i, n, R, P, X, C):
        p_ = R[0]
        d1 = _shift_down(p_, P[0], i, 1) - p_
        q = p_ + d1 * C[0]
        _, vjp = jax.vjp(_prep, q, *C[1:])
        cots = (R[1] + R[2], R[3], R[4] + R[5], R[6] + R[7], R[8], R[9], R[10])
        gs = vjp(cots)
        return (gs[0],), (_sum0(gs[0] * d1),) + tuple(gs[1:8])

    small, lowr = (1, RWKV_W), (128, RWKV_W)
    dq, dmu, dw0, da0, dkk, dka, dw2p, da2p, dg2 = _rowcall(
        "rwkv_prep_bwd", prep_bwd_fn, L, TM, [p, dr1, dr2, dlwk, dk1, dk2b, dv1, dv2, dan, dbv, dg],
        prep_consts, out_rows=[row(N_RWKV)], out_accs=[(1, N_RWKV)] + [small] * 4 + [lowr] * 3, prev=[0])
    G['rwkv_shift_mu'], G['rwkv_w0'], G['rwkv_a0'], G['rwkv_k_k'], G['rwkv_k_a'] = dmu, dw0, da0, dkk, dka
    G['rwkv_w2'], G['rwkv_a2'], G['rwkv_g2'] = dw2p[:64], da2p[64:], dg2

    def shift_bwd_fn(i, n, R, P, X, C):
        dm = R[0] * C[0]
        return (R[0] - dm + _shift_up(dm, X[0] * C[0], i, n, 1),), ()

    (dp,) = _rowcall("shift_bwd", shift_bwd_fn, L, TS, [dq], [S['rwkv_shift_mu']], out_rows=[row(N_RWKV, bf16)], nxt=[0])

    dproj = jnp.concatenate([dp, du, dgp], axis=1)
    dh1 = _mm(dproj, w_in_t, 'nn', "mm_in_dx")
    G['w_in'] = _mm(dproj, h1, 'tn', "mm_in_dw")

    def norm1_bwd_fn(i, n, R, P, X, C):
        _, vjp = jax.vjp(_rms, R[0], C[0])
        dxa, dg1_ = vjp(R[2])
        return (R[1] + dxa,), (dg1_,)

    dx, G['norm_mix_pre'] = _rowcall("norm_pre_bwd", norm1_bwd_fn, L, TS, [x, dx1, dh1], [g1],
                                     out_rows=[row(D_MODEL)], out_accs=[(1, D_MODEL)])
    return loss, dx, G, chip_e, slots_e


def kernel(x, norm_mix_pre, norm_mix_post, norm_ffn_pre, norm_ffn_post, w_in, b_gate, rwkv_shift_mu, rwkv_w0, rwkv_w2, rwkv_a0, rwkv_a2, rwkv_g2, rwkv_k_k, rwkv_k_a, rwkv_r_k, rwkv_lnx_w, rwkv_lnx_b, s5_a_re, s5_a_im, s5_b_re, s5_b_im, s5_c_re, s5_c_im, s5_d, s5_log_step, s5_w_glu, s5_b_glu, w_branch_rwkv, w_branch_s5, w_out, ffn_w_up, ffn_conv_w, ffn_conv_b, ffn_w_down, loss_target, m_norm_mix_pre, m_norm_mix_post, m_norm_ffn_pre, m_norm_ffn_post, m_w_in, m_b_gate, m_rwkv_shift_mu, m_rwkv_w0, m_rwkv_w2, m_rwkv_a0, m_rwkv_a2, m_rwkv_g2, m_rwkv_k_k, m_rwkv_k_a, m_rwkv_r_k, m_rwkv_lnx_w, m_rwkv_lnx_b, m_s5_a_re, m_s5_a_im, m_s5_b_re, m_s5_b_im, m_s5_c_re, m_s5_c_im, m_s5_d, m_s5_log_step, m_s5_w_glu, m_s5_b_glu, m_w_branch_rwkv, m_w_branch_s5, m_w_out, m_ffn_w_up, m_ffn_conv_w, m_ffn_conv_b, m_ffn_w_down, v_norm_mix_pre, v_norm_mix_post, v_norm_ffn_pre, v_norm_ffn_post, v_w_in, v_b_gate, v_rwkv_shift_mu, v_rwkv_w0, v_rwkv_w2, v_rwkv_a0, v_rwkv_a2, v_rwkv_g2, v_rwkv_k_k, v_rwkv_k_a, v_rwkv_r_k, v_rwkv_lnx_w, v_rwkv_lnx_b, v_s5_a_re, v_s5_a_im, v_s5_b_re, v_s5_b_im, v_s5_c_re, v_s5_c_im, v_s5_d, v_s5_log_step, v_s5_w_glu, v_s5_b_glu, v_w_branch_rwkv, v_w_branch_s5, v_w_out, v_ffn_w_up, v_ffn_conv_w, v_ffn_conv_b, v_ffn_w_down):
    A = dict(locals())
    me = 2 * lax.axis_index("x") + lax.axis_index("y")
    blk = lambda n: A[n][0]

    mine = {n: (blk(n).T if n == 'w_in' else blk(n)).astype(bf16) for n in BIG}
    mine.update({n: blk(n) for n in TINY})
    mine['ffn_conv_w'] = jnp.pad(blk('ffn_conv_w'), ((0, 5), (0, 0)))
    late = ['ffn_w_up', 'ffn_w_down']
    W = _gather_weights({n: blkv for n, blkv in mine.items() if n not in late})
    W.update(_gather_pair(W, [n for n in BIG if n not in late], "gather_weights_pair"))
    S = {n: A[n].reshape(1, -1) for n in SMALL}

    loss, dx, G, chip_e, slots_e = _forward_backward(x[0], loss_target[0], W, S, {n: mine[n] for n in late})

    tiny_shapes = [G[n].shape for n in TINY]
    small_buf = _pack_rows([G[n] for n in SMALL] + [G[n] for n in TINY] + [loss], SMALL_ROWS)
    recv, small_recv = _grads_to_sibling(G, ['w_in'], "grads_to_sibling", small_buf)
    chip_l, small_sum = _pair_add(G, recv, ['w_in'], "grads_pair_sum", small_buf, small_recv)
    slots_l, small4 = _grads_chip_exchange(chip_l, ['w_in'], small_sum)
    half, half['small'] = _sum_slots({**slots_e, **slots_l}, {**chip_e, **chip_l}, small4, small_sum)
    other = _halves_to_sibling(half)
    pc = lax.axis_index("c")
    small_tot = _join_halves(half['small'], other['small'], pc)
    grad = {n: _join_halves(half[n], other[n], pc) for n in BIG}
    grad['w_in'] = grad['w_in'].T
    vals = _unpack_rows(small_tot, [A[n].shape for n in SMALL] + tiny_shapes + [(1, PACK_W)])
    grad.update(zip(SMALL, vals))
    for n, full in zip(TINY, vals[len(SMALL):]):
        cs = A[n].shape[2]
        grad[n] = lax.dynamic_slice_in_dim(full, me * cs, cs, axis=1)
    loss_out = vals[-1][0, 0]

    packed = SMALL + TINY
    groups = [(blk(n), grad[n], blk('m_' + n), blk('v_' + n)) for n in BIG]
    groups.append(tuple(_pack_rows([src(n) for n in packed], ADAM_ROWS)
                        for src in (lambda n: A[n], lambda n: grad[n], lambda n: A['m_' + n], lambda n: A['v_' + n])))
    res = _adamw(groups)
    outs = [dict(), dict(), dict()]
    for n, r3 in zip(BIG, res[:-1]):
        for d, val in zip(outs, r3):
            d[n] = val
    for d, buf in zip(outs, res[-1]):
        d.update(zip(packed, _unpack_rows(buf, [A[n].shape for n in packed])))
    full = lambda d: [d[n].reshape(A[n].shape) for n in WEIGHTS]
    return (loss_out, dx[None], *full(grad), *full(outs[0]), *full(outs[1]), *full(outs[2]))
```

```python
import functools

import jax
import jax.numpy as jnp
from jax import lax
from jax.experimental import pallas as pl
from jax.experimental.pallas import tpu as pltpu

f32, bf16 = jnp.float32, jnp.bfloat16
MESH = pl.DeviceIdType.MESH

D_MODEL = 1024
RWKV_W = 512
HEADS, HEAD = 8, 64
N_RWKV = 1792
S5_W = 512
S5_G, S5_P, S5_C = 32, 64, 16
S5_N = S5_G * S5_P
D_FF = 2816
NORM_EPS = 1e-6
LNX_EPS = 64e-5
ADAM_LR, ADAM_B1, ADAM_B2, ADAM_EPS, ADAM_WD, ADAM_STEP = 0.001, 0.9, 0.999, 1e-08, 0.01, 10

VMEM_LIMIT = 48 * 1024 * 1024
PACK_W = 1024
WKV_C = 64
WKV_SUB = 4
WKV_ROWS = WKV_C * WKV_SUB
RESIDENT_BUDGET = 40 * 1024 * 1024
S5_T = 256

WEIGHTS = ['norm_mix_pre', 'norm_mix_post', 'norm_ffn_pre', 'norm_ffn_post', 'w_in', 'b_gate', 'rwkv_shift_mu',
           'rwkv_w0', 'rwkv_w2', 'rwkv_a0', 'rwkv_a2', 'rwkv_g2', 'rwkv_k_k', 'rwkv_k_a', 'rwkv_r_k', 'rwkv_lnx_w',
           'rwkv_lnx_b', 's5_a_re', 's5_a_im', 's5_b_re', 's5_b_im', 's5_c_re', 's5_c_im', 's5_d', 's5_log_step',
           's5_w_glu', 's5_b_glu', 'w_branch_rwkv', 'w_branch_s5', 'w_out', 'ffn_w_up', 'ffn_conv_w', 'ffn_conv_b',
           'ffn_w_down']


def _ceil_to(n, m):
    return -(-n // m) * m


def _mesh_pos():
    return lax.axis_index("x"), lax.axis_index("y"), lax.axis_index("c")


def _pick(d, cap=4096):
    for c in (1024, 1408, 2176, 896, 512, 256, 128):
        if c <= cap and d % c == 0:
            return c
    raise ValueError(d)


def _mm_resident(a, w, mode, name, M, N, K, out_dtype):
    budget = RESIDENT_BUDGET - 2 * K * N
    tm = next(t for t in (512, 256, 128) if 2 * t * (K * a.dtype.itemsize + 4 * N) <= budget)
    dims = _DIMS[mode]

    def body(a_ref, w_ref, o_ref):
        o_ref[...] = lax.dot_general(a_ref[...].astype(bf16), w_ref[...], (dims, ((), ())),
                                     preferred_element_type=f32).astype(o_ref.dtype)

    return pl.pallas_call(
        body, name=name, grid=(M // tm,),
        in_specs=[pl.BlockSpec((tm, K), lambda i: (i, 0)),
                  pl.BlockSpec(w.shape, lambda i: (0, 0), pipeline_mode=pl.Buffered(1))],
        out_specs=pl.BlockSpec((tm, N), lambda i: (i, 0)), out_shape=jax.ShapeDtypeStruct((M, N), out_dtype),
        compiler_params=pltpu.CompilerParams(dimension_semantics=("parallel",), vmem_limit_bytes=VMEM_LIMIT),
    )(a, w)


def _mm(a, b, mode, name, out_dtype=f32):
    if mode == 'tn':
        (K, M), (K2, N) = a.shape, b.shape
    elif mode == 'nt':
        (M, K), (N, K2) = a.shape, b.shape
    else:
        (M, K), (K2, N) = a.shape, b.shape
    assert K == K2, (name, a.shape, b.shape)
    if mode != 'tn' and b.dtype == bf16:
        return _mm_resident(a, b, mode, name, M, N, K, out_dtype)
    if mode == 'tn':
        tm = _pick(M, 2176)
        tn = _pick(N, 512 if tm > 1408 else (1024 if tm > 1024 else 1408))
        tk = _pick(K, 1024 if a.dtype == bf16 and b.dtype == bf16 else 512)
    else:
        tm, tn, tk = _pick(M, 512), _pick(N), _pick(K)
    nk = K // tk
    dims = {'nn': ((1,), (0,)), 'nt': ((1,), (1,)), 'tn': ((0,), (0,))}[mode]

    def body(a_ref, b_ref, o_ref, acc_ref):
        k = pl.program_id(2)

        @pl.when(k == 0)
        def _():
            acc_ref[...] = jnp.zeros_like(acc_ref)

        acc_ref[...] += lax.dot_general(a_ref[...].astype(bf16), b_ref[...].astype(bf16), (dims, ((), ())),
                                        preferred_element_type=f32)

        @pl.when(k == nk - 1)
        def _():
            o_ref[...] = acc_ref[...].astype(o_ref.dtype)

    a_spec = pl.BlockSpec((tk, tm), lambda i, j, k: (k, i)) if mode == 'tn' else pl.BlockSpec((tm, tk), lambda i, j, k: (i, k))
    b_spec = pl.BlockSpec((tn, tk), lambda i, j, k: (j, k)) if mode == 'nt' else pl.BlockSpec((tk, tn), lambda i, j, k: (k, j))
    return pl.pallas_call(
        body, name=name, grid=(M // tm, N // tn, nk),
        in_specs=[a_spec, b_spec], out_specs=pl.BlockSpec((tm, tn), lambda i, j, k: (i, j)),
        out_shape=jax.ShapeDtypeStruct((M, N), out_dtype),
        scratch_shapes=[pltpu.VMEM((tm, tn), f32)],
        compiler_params=pltpu.CompilerParams(dimension_semantics=("parallel", "parallel", "arbitrary"),
                                             vmem_limit_bytes=VMEM_LIMIT),
    )(a, b)


def _rowcall(name, fn, L, tm, rows, consts=(), out_rows=(), out_accs=(), prev=(), nxt=()):
    nsteps = L // tm
    nb8 = tm // 8
    last8 = L // 8 - 1
    n_r, n_p, n_x, n_c, n_or = len(rows), len(prev), len(nxt), len(consts), len(out_rows)

    def body(*refs):
        i = pl.program_id(0)
        vals = [r[...] for r in refs[:n_r + n_p + n_x + n_c]]
        R, P = vals[:n_r], vals[n_r:n_r + n_p]
        X, C = vals[n_r + n_p:n_r + n_p + n_x], vals[n_r + n_p + n_x:]
        o_refs = refs[n_r + n_p + n_x + n_c:]
        outs_r, outs_a = fn(i, nsteps, R, P, X, C)
        for ref, v in zip(o_refs[:n_or], outs_r, strict=True):
            ref[...] = v.astype(ref.dtype)
        if out_accs:
            @pl.when(i == 0)
            def _():
                for ref in o_refs[n_or:]:
                    ref[...] = jnp.zeros_like(ref)

            for ref, v in zip(o_refs[n_or:], outs_a, strict=True):
                ref[...] += v

    def const_spec(c):
        nd = c.ndim
        return pl.BlockSpec(c.shape, lambda i: (0,) * nd)

    in_specs = ([pl.BlockSpec((tm, a.shape[1]), lambda i: (i, 0)) for a in rows]
                + [pl.BlockSpec((8, rows[j].shape[1]), lambda i: (jnp.maximum(i * nb8 - 1, 0), 0)) for j in prev]
                + [pl.BlockSpec((8, rows[j].shape[1]), lambda i: (jnp.minimum((i + 1) * nb8, last8), 0)) for j in nxt]
                + [const_spec(c) for c in consts])
    out_specs = ([pl.BlockSpec((tm, c), lambda i: (i, 0)) for c, _ in out_rows]
                 + [pl.BlockSpec(s, lambda i: (0, 0)) for s in out_accs])
    out_shape = ([jax.ShapeDtypeStruct((L, c), dt) for c, dt in out_rows]
                 + [jax.ShapeDtypeStruct(s, f32) for s in out_accs])
    args = list(rows) + [rows[j] for j in prev] + [rows[j] for j in nxt] + list(consts)
    return pl.pallas_call(
        body, name=name, grid=(nsteps,), in_specs=in_specs, out_specs=out_specs, out_shape=out_shape,
        compiler_params=pltpu.CompilerParams(dimension_semantics=("arbitrary",), vmem_limit_bytes=VMEM_LIMIT),
    )(*args)


def _rows_mm(name, fn, L, tm, rows, consts, w, mode, splits, extra=()):
    n_r, n_c, n_e = len(rows), len(consts), len(extra)
    K = w.shape[0] if mode == 'nn' else w.shape[1]

    def body(*refs):
        R, C = [r[...] for r in refs[:n_r]], [r[...] for r in refs[n_r:n_r + n_c]]
        w_ref, lhs_ref = refs[n_r + n_c], refs[n_r + n_c + 1]
        e_refs, outs = refs[n_r + n_c + 2:n_r + n_c + 2 + n_e], refs[n_r + n_c + 2 + n_e:]
        lhs, ex = fn(R, C) if n_e else (fn(R, C), ())
        for ref, val in zip(e_refs, ex, strict=True):
            ref[...] = val.astype(ref.dtype)
        lhs = lhs.astype(bf16)
        lhs_ref[...] = lhs
        off = 0
        for ref, wd in zip(outs, splits):
            wk = w_ref[:, off:off + wd] if mode == 'nn' else w_ref[off:off + wd, :]
            ref[...] = lax.dot_general(lhs, wk, (_DIMS[mode], ((), ())), preferred_element_type=f32)
            off += wd

    def const_spec(c):
        nd = c.ndim
        return pl.BlockSpec(c.shape, lambda i: (0,) * nd)

    return pl.pallas_call(
        body, name=name, grid=(L // tm,),
        in_specs=[pl.BlockSpec((tm, a.shape[1]), lambda i: (i, 0)) for a in rows] + [const_spec(c) for c in consts]
        + [pl.BlockSpec(w.shape, lambda i: (0, 0), pipeline_mode=pl.Buffered(1))],
        out_specs=[pl.BlockSpec((tm, c), lambda i: (i, 0)) for c in [K] + [c for c, _ in extra] + list(splits)],
        out_shape=[jax.ShapeDtypeStruct((L, K), bf16)] + [jax.ShapeDtypeStruct((L, c), dt) for c, dt in extra]
        + [jax.ShapeDtypeStruct((L, wd), f32) for wd in splits],
        compiler_params=pltpu.CompilerParams(dimension_semantics=("parallel",), vmem_limit_bytes=VMEM_LIMIT),
    )(*rows, *consts, w)


def _shift_down(x, prev8, i, k):
    rolled = pltpu.roll(x, k, axis=0)
    pfix = jnp.where(i > 0, pltpu.roll(prev8, k, axis=0), 0.0)
    row8 = lax.broadcasted_iota(jnp.int32, pfix.shape, 0)
    top = jnp.where(row8 < k, pfix, rolled[:8])
    return top if x.shape[0] == 8 else jnp.concatenate([top, rolled[8:]], axis=0)


def _shift_up(x, next8, i, nsteps, k):
    tm = x.shape[0]
    rolled = pltpu.roll(x, tm - k, axis=0)
    nfix = jnp.where(i < nsteps - 1, pltpu.roll(next8, 8 - k, axis=0), 0.0)
    row8 = lax.broadcasted_iota(jnp.int32, nfix.shape, 0)
    bot = jnp.where(row8 >= 8 - k, nfix, rolled[tm - 8:])
    return jnp.concatenate([rolled[:tm - 8], bot], axis=0)


def _sum0(x):
    return jnp.sum(x, axis=0, keepdims=True)


def _rms(x, g):
    return x * lax.rsqrt(jnp.mean(x * x, axis=-1, keepdims=True) + NORM_EPS) * g


def _softplus(x):
    return jnp.maximum(x, 0.0) + jnp.log(1.0 + jnp.exp(-jnp.abs(x)))


def _gelu(x):
    return 0.5 * x * (1.0 + jnp.tanh(0.7978845608028654 * (x + 0.044715 * x * x * x)))


def _dot32(a, b):
    return jnp.dot(a, b, preferred_element_type=f32, precision=lax.Precision.HIGHEST)


def _seg_raw(x, E):
    hi = x.astype(bf16)
    r1 = x - hi.astype(f32)
    mid = r1.astype(bf16)
    lo = (r1 - mid.astype(f32)).astype(bf16)
    Eb = E.astype(bf16)
    dot = lambda t: jnp.dot(t, Eb, preferred_element_type=f32)
    return (dot(lo) + dot(mid)) + dot(hi)


@jax.custom_vjp
def _seg(x, E):
    return _seg_raw(x, E)


_seg.defvjp(lambda x, E: (_seg_raw(x, E), E), lambda E, g: (_seg_raw(g, E), jnp.zeros_like(E)))


def _prep(q, w0, a0, k_k, k_a, w2p, a2p, g2, E):
    r, k, v = q[:, 0:512], q[:, 512:1024], q[:, 1024:1536]
    wa, gd = q[:, 1536:1664], q[:, 1664:1792]
    wlog = -_softplus(-(w0 + _bdot(jnp.tanh(wa), w2p, 'nn'))) - 0.5
    lw = -jnp.exp(wlog)
    a = jax.nn.sigmoid(a0 + _bdot(wa, a2p, 'nn'))
    g = _bdot(jax.nn.sigmoid(gd), g2, 'nn')
    kk = k * k_k
    kkn = kk / jnp.maximum(jnp.sqrt(_seg(kk * kk, E)), 1e-12)
    k2 = k * (1.0 + (a - 1.0) * k_a)
    return r, lw, k2, v, -kkn, kkn * a, g


def _rwkv_out(y, r, k2, v, g, lnx_w, lnx_b, r_k, E):
    mean = _seg(y, E) * (1.0 / HEAD)
    yc = y - mean
    var = _seg(yc * yc, E) * (1.0 / HEAD)
    yn = yc * lax.rsqrt(var + LNX_EPS) * lnx_w + lnx_b
    bonus = _seg(r * k2 * r_k, E) * v
    return (yn + bonus) * g


def _s5_mid(ysc, u, d):
    return _gelu(ysc + d * u)


def _s5_glu(yg, z2, b_glu):
    return yg * jax.nn.sigmoid(z2 + b_glu)


def _merge(gp, o_r, o_s, b_gate):
    gates = jax.nn.sigmoid(gp + b_gate)
    return gates[:, :D_MODEL] * o_r + gates[:, D_MODEL:] * o_s


def _act(zc):
    return _gelu(zc[:, :D_FF]) * zc[:, D_FF:]


def _s5_disc(a_re, a_im, ls, b_re, b_im):
    dt = jnp.exp(ls)
    er = jnp.exp(a_re * dt)
    ar, ai = er * jnp.cos(a_im * dt), er * jnp.sin(a_im * dt)
    x, y = ar - 1.0, ai
    den = a_re * a_re + a_im * a_im
    fr, fi = (x * a_re + y * a_im) / den, (y * a_re - x * a_im) / den
    return ar, ai, fr * b_re - fi * b_im, fr * b_im + fi * b_re


_DIMS = {'nn': ((1,), (0,)), 'nt': ((1,), (1,)), 'tn': ((0,), (0,))}


def _raw_bdot(a, b, mode):
    return lax.dot_general(a.astype(bf16), b.astype(bf16), (_DIMS[mode], ((), ())), preferred_element_type=f32)


@functools.partial(jax.custom_vjp, nondiff_argnums=(2,))
def _bdot(a, b, mode):
    return _raw_bdot(a, b, mode)


def _bdot_fwd(a, b, mode):
    return _raw_bdot(a, b, mode), (a, b)


def _bdot_bwd(mode, res, g):
    a, b = res
    if mode == 'nn':
        return _raw_bdot(g, b, 'nt'), _raw_bdot(a, g, 'tn')
    if mode == 'nt':
        return _raw_bdot(g, b, 'nn'), _raw_bdot(g, a, 'tn')
    return _raw_bdot(b, g, 'nt'), _raw_bdot(a, g, 'nn')


_bdot.defvjp(_bdot_fwd, _bdot_bwd)


def _tri_inv_raw(A):
    n = A[0].shape[0]
    eye = (lax.broadcasted_iota(jnp.int32, (n, n), 0) == lax.broadcasted_iota(jnp.int32, (n, n), 1)).astype(f32)
    x = [eye + a for a in A]
    pw, m = A, 1
    while 2 * m < n // 2:
        pw = [_raw_bdot(p, p, 'nn') for p in pw]
        x = [xi + _raw_bdot(xi, p, 'nn') for xi, p in zip(x, pw)]
        m *= 2
    return x


@jax.custom_vjp
def _tri_inv(A):
    return _tri_inv_raw(A)


def _tri_inv_fwd(A):
    x = _tri_inv_raw(A)
    return x, x


def _tri_inv_bwd(x, g):
    return ([_raw_bdot(_raw_bdot(xi, gi, 'tn'), xi, 'nt') for xi, gi in zip(x, g)],)


_tri_inv.defvjp(_tri_inv_fwd, _tri_inv_bwd)


@jax.custom_vjp
def _inv_given(A, X):
    return X


_inv_given.defvjp(lambda A, X: (X, X),
                  lambda x, g: (_tri_inv_bwd(x, g)[0], [jnp.zeros_like(xi) for xi in x]))


def _wkv_chunk(S0, r, lw, k, v, a, b, tri, bd, xinv=None):
    C = r[0].shape[0]
    P = range(len(r))
    lane = lax.broadcasted_iota(jnp.int32, (1, 2 * HEAD), 1)
    m0, m1 = (lane < HEAD).astype(f32), (lane >= HEAD).astype(f32)
    cat = lambda *xs: jnp.concatenate(xs, axis=0)
    stack = lambda x: cat(x * m0, x * m1)
    unstack = lambda x2: m0 * x2[:C] + m1 * x2[C:]
    rid = lax.broadcasted_iota(jnp.int32, (2 * C, 2 * C), 0)
    cid = lax.broadcasted_iota(jnp.int32, (2 * C, 2 * C), 1)
    same = (rid < C) == (cid < C)
    eye2 = (rid == cid).astype(f32)
    tri2 = (same & (rid >= cid)).astype(f32)
    sl2 = tri2 - eye2
    cum = [_dot32(tri, lw[p]) for p in P]
    g = [jnp.exp(cum[p]) for p in P]
    gi = [jnp.exp(-cum[p]) for p in P]
    at = [a[p] * jnp.exp(cum[p] - lw[p]) for p in P]
    rt = [r[p] * g[p] for p in P]
    kb = [k[p] * gi[p] for p in P]
    bb = [b[p] * gi[p] for p in P]
    lhs = [cat(stack(at[p]), stack(rt[p])) for p in P]
    pb = [_bdot(lhs[p], stack(bb[p]), 'nt') for p in P]
    pk = [_bdot(lhs[p], stack(kb[p]), 'nt') for p in P]
    aab = [pb[p][:2 * C] * sl2 for p in P]
    base = [_bdot(cat(at[p], rt[p]), S0[p], 'nt') for p in P]
    t = [_bdot(cat(pk[p][:2 * C] * sl2, pk[p][2 * C:] * tri2), cat(v[p], v[p]), 'nn') for p in P]
    rhs = [cat(base[p][:C], base[p][:C]) + t[p][:2 * C] for p in P]
    x = _tri_inv(aab) if xinv is None else _inv_given(aab, xinv)
    u = [unstack(_bdot(x[p], rhs[p], 'nn')) for p in P]
    w2 = [_bdot(pb[p][2 * C:] * tri2, cat(u[p], u[p]), 'nn') for p in P]
    y = [base[p][C:] + unstack(t[p][2 * C:]) + unstack(w2[p]) for p in P]
    S1 = [g[p][C - 1:C, :] * (S0[p] + bd * _bdot(cat(v[p], u[p]), cat(kb[p], bb[p]), 'tn')) for p in P]
    return y, S1, x


def _pairs(x):
    return [x[:, 2 * HEAD * p:2 * HEAD * (p + 1)] for p in range(HEADS // 2)]


def _wkv_consts():
    tri = jnp.tril(jnp.ones((WKV_C, WKV_C), f32))
    hid = jnp.arange(2 * HEAD) // HEAD
    return tri, (hid[:, None] == hid[None, :]).astype(f32)


def _wkv_step(S0, r, lw, k, v, a, b, tri, bd, xinv=None):
    ys, xs, S = [], [], S0
    for c in range(WKV_SUB):
        sub = lambda t: [x[c * WKV_C:(c + 1) * WKV_C] for x in t]
        y, S, x = _wkv_chunk(S, sub(r), sub(lw), sub(k), sub(v), sub(a), sub(b), tri, bd, None if xinv is None else xinv[c])
        ys.append(y)
        xs.append(x)
    return [jnp.concatenate([y[p] for y in ys], axis=0) for p in range(len(S0))], S, xs


def _wkv7_fwd(r, lw, k, v, a, b):
    L = r.shape[0]
    nc, npair = L // WKV_ROWS, HEADS // 2

    def body(r_ref, lw_ref, k_ref, v_ref, a_ref, b_ref, tri_ref, bd_ref, y_ref, ck_ref, xi_ref, s_ref):
        @pl.when(pl.program_id(0) == 0)
        def _():
            s_ref[...] = jnp.zeros_like(s_ref)

        s0 = [s_ref[p] for p in range(npair)]
        for p in range(npair):
            ck_ref[0, p] = s0[p]
        y, s1, xs = _wkv_step(s0, *(_pairs(x) for x in (r_ref, lw_ref, k_ref, v_ref, a_ref, b_ref)), tri_ref[...], bd_ref[...])
        for p in range(npair):
            y_ref[:, 2 * HEAD * p:2 * HEAD * (p + 1)] = y[p]
            s_ref[p] = s1[p]
            for c in range(WKV_SUB):
                xi_ref[0, c, p] = xs[c][p].astype(xi_ref.dtype)

    row = pl.BlockSpec((WKV_ROWS, RWKV_W), lambda c: (c, 0))
    sspec = pl.BlockSpec((1, npair, 2 * HEAD, 2 * HEAD), lambda c: (c, 0, 0, 0))
    xspec = pl.BlockSpec((1, WKV_SUB, npair, 2 * HEAD, 2 * HEAD), lambda c: (c, 0, 0, 0, 0))
    return pl.pallas_call(
        body, name="wkv7_fwd", grid=(nc,),
        in_specs=[row] * 6 + [pl.BlockSpec((WKV_C, WKV_C), lambda c: (0, 0)), pl.BlockSpec((2 * HEAD, 2 * HEAD), lambda c: (0, 0))],
        out_specs=[row, sspec, xspec],
        out_shape=[jax.ShapeDtypeStruct((L, RWKV_W), f32), jax.ShapeDtypeStruct((nc, npair, 2 * HEAD, 2 * HEAD), f32),
                   jax.ShapeDtypeStruct((nc, WKV_SUB, npair, 2 * HEAD, 2 * HEAD), bf16)],
        scratch_shapes=[pltpu.VMEM((npair, 2 * HEAD, 2 * HEAD), f32)],
        compiler_params=pltpu.CompilerParams(dimension_semantics=("arbitrary",), vmem_limit_bytes=VMEM_LIMIT),
    )(r, lw, k, v, a, b, *_wkv_consts())


def _wkv7_bwd(r, lw, k, v, a, b, ck, xinv, dy):
    L = r.shape[0]
    nc, npair = L // WKV_ROWS, HEADS // 2

    def body(r_ref, lw_ref, k_ref, v_ref, a_ref, b_ref, ck_ref, xi_ref, dy_ref, tri_ref, bd_ref,
             dr_ref, dlw_ref, dk_ref, dv_ref, da_ref, db_ref, ds_ref):
        @pl.when(pl.program_id(0) == 0)
        def _():
            ds_ref[...] = jnp.zeros_like(ds_ref)

        tri, bd = tri_ref[...], bd_ref[...]
        ins = [[ck_ref[0, p] for p in range(npair)]] + [_pairs(x) for x in (r_ref, lw_ref, k_ref, v_ref, a_ref, b_ref)]
        xs = [[xi_ref[0, c, p].astype(f32) for p in range(npair)] for c in range(WKV_SUB)]
        _, vjp = jax.vjp(lambda *t: _wkv_step(*t, tri, bd, xs)[:2], *ins)
        gs = vjp((_pairs(dy_ref), [ds_ref[p] for p in range(npair)]))
        for p in range(npair):
            ds_ref[p] = gs[0][p]
            for ref, gval in zip((dr_ref, dlw_ref, dk_ref, dv_ref, da_ref, db_ref), gs[1:]):
                ref[:, 2 * HEAD * p:2 * HEAD * (p + 1)] = gval[p]

    row = pl.BlockSpec((WKV_ROWS, RWKV_W), lambda c: (nc - 1 - c, 0))
    sspec = pl.BlockSpec((1, npair, 2 * HEAD, 2 * HEAD), lambda c: (nc - 1 - c, 0, 0, 0))
    xspec = pl.BlockSpec((1, WKV_SUB, npair, 2 * HEAD, 2 * HEAD), lambda c: (nc - 1 - c, 0, 0, 0, 0))
    return pl.pallas_call(
        body, name="wkv7_bwd", grid=(nc,),
        in_specs=[row] * 6 + [sspec, xspec, row, pl.BlockSpec((WKV_C, WKV_C), lambda c: (0, 0)),
                              pl.BlockSpec((2 * HEAD, 2 * HEAD), lambda c: (0, 0))],
        out_specs=[row] * 6,
        out_shape=[jax.ShapeDtypeStruct((L, RWKV_W), f32)] * 6,
        scratch_shapes=[pltpu.VMEM((npair, 2 * HEAD, 2 * HEAD), f32)],
        compiler_params=pltpu.CompilerParams(dimension_semantics=("arbitrary",), vmem_limit_bytes=VMEM_LIMIT),
    )(r, lw, k, v, a, b, ck, xinv, dy, *_wkv_consts())


def _cmul(ar, ai, xr, xi):
    return ar * xr - ai * xi, ar * xi + ai * xr


def _scan_init(a_ref, car_ref, pw_ref, reverse):
    car_ref[...] = jnp.zeros_like(car_ref)
    ar = jnp.broadcast_to(a_ref[:, :S5_N], (8, S5_N))
    ai = jnp.broadcast_to(a_ref[:, S5_N:], (8, S5_N))
    if reverse:
        ai = -ai
    row = lax.broadcasted_iota(jnp.int32, (8, S5_N), 0)
    pr, pi = ar, ai
    qr, qi = jnp.zeros((8, S5_N), f32), jnp.zeros((8, S5_N), f32)
    for e in range(1, 9):
        sel = (row == 8 - e) if reverse else (row == e - 1)
        qr, qi = jnp.where(sel, pr, qr), jnp.where(sel, pi, qi)
        if e in (1, 2, 4):
            j = (1, 2, 4).index(e)
            pw_ref[j, :, :S5_N] = pr
            pw_ref[j, :, S5_N:] = pi
        pr, pi = _cmul(pr, pi, ar, ai)
    pw_ref[3, :, :S5_N] = qr
    pw_ref[3, :, S5_N:] = qi


def _scan_tile(x_ref, o_ref, car_ref, pw_ref, reverse):
    ng = x_ref.shape[0] // 8
    row = lax.broadcasted_iota(jnp.int32, (8, S5_N), 0)

    def group(gi, carry):
        g = (ng - 1 - gi) if reverse else gi
        t0 = pl.multiple_of(g * 8, 8)
        xr, xi = x_ref[pl.ds(t0, 8), :S5_N], x_ref[pl.ds(t0, 8), S5_N:]
        for j, d in enumerate((1, 2, 4)):
            if reverse:
                sr = jnp.where(row < 8 - d, pltpu.roll(xr, 8 - d, axis=0), 0.0)
                si = jnp.where(row < 8 - d, pltpu.roll(xi, 8 - d, axis=0), 0.0)
            else:
                sr = jnp.where(row >= d, pltpu.roll(xr, d, axis=0), 0.0)
                si = jnp.where(row >= d, pltpu.roll(xi, d, axis=0), 0.0)
            mr, mi = _cmul(pw_ref[j, :, :S5_N], pw_ref[j, :, S5_N:], sr, si)
            xr, xi = xr + mr, xi + mi
        cr, ci = carry
        mr, mi = _cmul(pw_ref[3, :, :S5_N], pw_ref[3, :, S5_N:], cr, ci)
        xr, xi = xr + mr, xi + mi
        o_ref[pl.ds(t0, 8), :S5_N] = xr
        o_ref[pl.ds(t0, 8), S5_N:] = xi
        e = 0 if reverse else 7
        return (jnp.broadcast_to(xr[e:e + 1, :], (8, S5_N)), jnp.broadcast_to(xi[e:e + 1, :], (8, S5_N)))

    cr, ci = lax.fori_loop(0, ng, group, (car_ref[:, :S5_N], car_ref[:, S5_N:]))
    car_ref[:, :S5_N] = cr
    car_ref[:, S5_N:] = ci


_CB, _SB = 128, 512


def _cblk(k):
    return slice(_CB * k, _CB * (k + 1))


def _sblk(j):
    return slice(_SB * j, _SB * (j + 1))


def _s5_fwd(u, bmat, cmat, abar, late):
    L = u.shape[0]
    nt = L // S5_T
    names = list(late)
    nh = len(names)

    def body(u_ref, b_ref, c_ref, a_ref, *rest):
        h_in, (st_ref, y_ref), h_out = rest[:nh], rest[nh:nh + 2], rest[nh + 2:2 * nh + 2]
        bu_ref, car_ref, pw_ref, ssem, rsem, lsem = rest[2 * nh + 2:]
        i = pl.program_id(0)

        def copies():
            px, py, pc = _mesh_pos()
            me = 2 * px + py
            out = []
            for a, nm in enumerate(names):
                hr = late[nm].shape[0] // 2
                src, dst = h_in[a].at[pl.ds(pl.multiple_of(pc * hr, 16), hr), :], _slab(h_out[a], nm, me, pc)
                out.append(pltpu.make_async_copy(src, dst, lsem.at[a]))
                out += [pltpu.make_async_remote_copy(src, dst, ssem.at[3 * a + k], rsem.at[3 * a + k],
                                                     device_id=(qx, qy, pc), device_id_type=MESH)
                        for k, (qx, qy) in enumerate(_chip_peers(px, py))]
            return out

        @pl.when(i == 0)
        def _():
            _scan_init(a_ref, car_ref, pw_ref, False)
            for cp in copies():
                cp.start()

        for j in range(8):
            bu_ref[:, _sblk(j)] = _raw_bdot(u_ref[:, _cblk(j % 4)], b_ref[j], 'nn')
        _scan_tile(bu_ref, st_ref, car_ref, pw_ref, False)
        for k in range(4):
            y_ref[:, _cblk(k)] = (_raw_bdot(st_ref[:, _sblk(k)], c_ref[k], 'nn')
                                  + _raw_bdot(st_ref[:, _sblk(4 + k)], c_ref[4 + k], 'nn'))

        @pl.when(i == nt - 1)
        def _():
            for cp in copies():
                cp.wait()

    whole = lambda shape: pl.BlockSpec(shape, lambda i: (0,) * len(shape))
    outs = pl.pallas_call(
        body, name="s5_fwd", grid=(nt,),
        in_specs=[pl.BlockSpec((S5_T, S5_W), lambda i: (i, 0)), whole(bmat.shape), whole(cmat.shape), whole(abar.shape)]
        + [ANY] * nh,
        out_specs=[pl.BlockSpec((S5_T, 2 * S5_N), lambda i: (i, 0)), pl.BlockSpec((S5_T, S5_W), lambda i: (i, 0))] + [ANY] * nh,
        out_shape=[jax.ShapeDtypeStruct((L, 2 * S5_N), f32), jax.ShapeDtypeStruct((L, S5_W), f32)]
        + [jax.ShapeDtypeStruct(GATHER[nm][0], late[nm].dtype) for nm in names],
        scratch_shapes=[pltpu.VMEM((S5_T, 2 * S5_N), f32), pltpu.VMEM((8, 2 * S5_N), f32), pltpu.VMEM((4, 8, 2 * S5_N), f32),
                        pltpu.SemaphoreType.DMA((3 * nh,)), pltpu.SemaphoreType.DMA((3 * nh,)), pltpu.SemaphoreType.DMA((nh,))],
        compiler_params=pltpu.CompilerParams(dimension_semantics=("arbitrary",), vmem_limit_bytes=VMEM_LIMIT),
    )(u, bmat, cmat, abar, *[late[nm] for nm in names])
    return outs[0], outs[1], dict(zip(names, outs[2:]))


def _s5_bwd(dy, st, u, du_direct, bmat, cmat, abar, chip_sum):
    L = u.shape[0]
    nt = L // S5_T
    nb8 = S5_T // 8
    names = list(chip_sum)
    nh = len(names)

    def body(dy_ref, st_ref, sp_ref, u_ref, dud_ref, b_ref, c_ref, a_ref, *rest):
        x_in, (du_ref, db_ref, dc_ref, da_ref), x_out = rest[:nh], rest[nh:nh + 4], rest[nh + 4:2 * nh + 4]
        lam_ref, car_ref, pw_ref, ssem, rsem = rest[2 * nh + 4:]
        i = pl.program_id(0)

        @pl.when(i == 0)
        def _():
            _scan_init(a_ref, car_ref, pw_ref, True)
            db_ref[...] = jnp.zeros_like(db_ref)
            dc_ref[...] = jnp.zeros_like(dc_ref)
            da_ref[...] = jnp.zeros_like(da_ref)
            for cp in _exchange_copies(x_in, x_out, ssem, rsem):
                cp.start()

        for j in range(8):
            lam_ref[:, _sblk(j)] = _raw_bdot(dy_ref[:, _cblk(j % 4)], c_ref[j], 'nt')
        _scan_tile(lam_ref, lam_ref, car_ref, pw_ref, True)
        for k in range(4):
            du_ref[:, _cblk(k)] = (dud_ref[:, _cblk(k)] + _raw_bdot(lam_ref[:, _sblk(k)], b_ref[k], 'nt')
                                   + _raw_bdot(lam_ref[:, _sblk(4 + k)], b_ref[4 + k], 'nt')
                                   ).astype(du_ref.dtype)
            sr = _shift_down(st_ref[:, _sblk(k)], sp_ref[:, _sblk(k)], nt - 1 - i, 1)
            si = _shift_down(st_ref[:, _sblk(4 + k)], sp_ref[:, _sblk(4 + k)], nt - 1 - i, 1)
            lr, li = lam_ref[:, _sblk(k)], lam_ref[:, _sblk(4 + k)]
            da_ref[:, _sblk(k)] += _sum0(lr * sr + li * si)
            da_ref[:, _sblk(4 + k)] += _sum0(li * sr - lr * si)
        for j in range(8):
            db_ref[j] += _raw_bdot(u_ref[:, _cblk(j % 4)], lam_ref[:, _sblk(j)], 'tn')
            dc_ref[j] += _raw_bdot(st_ref[:, _sblk(j)], dy_ref[:, _cblk(j % 4)], 'tn')

        @pl.when(i == nt - 1)
        def _():
            for cp in _exchange_copies(x_in, x_out, ssem, rsem):
                cp.wait()

    whole = lambda shape: pl.BlockSpec(shape, lambda i: (0,) * len(shape))
    rev = lambda i: (nt - 1 - i, 0)
    outs = pl.pallas_call(
        body, name="s5_bwd", grid=(nt,),
        in_specs=[pl.BlockSpec((S5_T, S5_W), rev), pl.BlockSpec((S5_T, 2 * S5_N), rev),
                  pl.BlockSpec((8, 2 * S5_N), lambda i: (jnp.maximum((nt - 1 - i) * nb8 - 1, 0), 0)),
                  pl.BlockSpec((S5_T, S5_W), rev), pl.BlockSpec((S5_T, S5_W), rev), whole(bmat.shape), whole(cmat.shape),
                  whole(abar.shape)] + [ANY] * nh,
        out_specs=[pl.BlockSpec((S5_T, S5_W), rev), whole((8, _CB, _SB)), whole((8, _SB, _CB)), whole((1, 2 * S5_N))]
        + [ANY] * nh,
        out_shape=[jax.ShapeDtypeStruct((L, S5_W), bf16), jax.ShapeDtypeStruct((8, _CB, _SB), f32),
                   jax.ShapeDtypeStruct((8, _SB, _CB), f32), jax.ShapeDtypeStruct((1, 2 * S5_N), f32)]
        + [jax.ShapeDtypeStruct(chip_sum[nm].shape, chip_sum[nm].dtype) for nm in names],
        scratch_shapes=[pltpu.VMEM((S5_T, 2 * S5_N), f32), pltpu.VMEM((8, 2 * S5_N), f32), pltpu.VMEM((4, 8, 2 * S5_N), f32),
                        pltpu.SemaphoreType.DMA((3 * nh,)), pltpu.SemaphoreType.DMA((3 * nh,))],
        compiler_params=pltpu.CompilerParams(dimension_semantics=("arbitrary",), vmem_limit_bytes=VMEM_LIMIT),
    )(dy, st, st, u, du_direct, bmat, cmat, abar, *[chip_sum[nm] for nm in names])
    return outs[0], outs[1], outs[2], outs[3], dict(zip(names, outs[4:]))


def _s5_disc_fwd(a_re, a_im, ls, b_re, b_im):
    def body(a_re_ref, a_im_ref, ls_ref, b_re_ref, b_im_ref, ar_ref, ai_ref, br_ref, bi_ref):
        outs = _s5_disc(a_re_ref[...], a_im_ref[...], ls_ref[...], b_re_ref[...], b_im_ref[...])
        for ref, v in zip((ar_ref, ai_ref, br_ref, bi_ref), outs):
            ref[...] = v

    c1, c16 = jax.ShapeDtypeStruct((S5_N, 1), f32), jax.ShapeDtypeStruct((S5_N, S5_C), f32)
    return pl.pallas_call(body, name="s5_disc", out_shape=[c1, c1, c16, c16])(a_re, a_im, ls, b_re, b_im)


def _s5_disc_bwd(a_re, a_im, ls, b_re, b_im, d_ar, d_ai, d_br, d_bi, seg):
    def body(a_re_ref, a_im_ref, ls_ref, b_re_ref, b_im_ref, g1, g2, g3, g4, seg_ref, o1, o2, o3, o4, o5):
        _, vjp = jax.vjp(_s5_disc, a_re_ref[...], a_im_ref[...], ls_ref[...], b_re_ref[...], b_im_ref[...])
        da_re, da_im, dls, db_re, db_im = vjp((g1[...], g2[...], g3[...], g4[...]))
        o1[...] = da_re
        o2[...] = da_im
        o3[...] = _dot32(seg_ref[...], dls)
        o4[...] = db_re
        o5[...] = db_im

    c1, c16 = jax.ShapeDtypeStruct((S5_N, 1), f32), jax.ShapeDtypeStruct((S5_N, S5_C), f32)
    return pl.pallas_call(body, name="s5_disc_bwd", out_shape=[c1, c1, jax.ShapeDtypeStruct((S5_G, 1), f32), c16, c16])(
        a_re, a_im, ls, b_re, b_im, d_ar, d_ai, d_br, d_bi, seg)


ANY = pl.BlockSpec(memory_space=pl.ANY)

GATHER = {'w_in': ((4352, 1024), 0), 'ffn_w_up': ((1024, 5632), 1), 'w_branch_rwkv': ((512, 1024), 1),
          'w_branch_s5': ((512, 1024), 1), 'w_out': ((1024, 1024), 0), 's5_w_glu': ((512, 512), 0),
          'ffn_w_down': ((2816, 1024), 0), 'rwkv_w2': ((64, 512), 1), 'rwkv_a2': ((64, 512), 1),
          'rwkv_g2': ((128, 512), 1), 'ffn_conv_w': ((8, 5632), 1)}
BIG = ['w_in', 'ffn_w_up', 'w_branch_rwkv', 'w_branch_s5', 'w_out', 's5_w_glu', 'ffn_w_down']
TINY = ['rwkv_w2', 'rwkv_a2', 'rwkv_g2', 'ffn_conv_w']
SMALL = [n for n in WEIGHTS if n not in GATHER]
SMALL_ROWS = 320
ADAM_ROWS = 256


def _mo(v, m):
    return v if isinstance(v, int) else pl.multiple_of(v, m)


def _slab(ref, name, j, h=None):
    (R, Cn), axis = GATHER[name]
    if axis == 0:
        rs = R // 4
        if h is None:
            return ref.at[pl.ds(_mo(j * rs, 16), rs), :]
        return ref.at[pl.ds(_mo(j * rs + h * (rs // 2), 8), rs // 2), :]
    cols = pl.ds(_mo(j * (Cn // 4), 128), Cn // 4)
    if h is None:
        return ref.at[:, cols]
    return ref.at[pl.ds(_mo(h * (R // 2), 8), R // 2), cols]


def _half_shape(name):
    (R, Cn), axis = GATHER[name]
    return (R // 8, Cn) if axis == 0 else (R // 2, Cn // 4)


def _chip_peers(px, py):
    return [((1 - px) if (k >> 1) else px, (1 - py) if (k & 1) else py) for k in (1, 2, 3)]


def _run_copies(copies):
    for cp in copies:
        cp.start()
    for cp in copies:
        cp.wait()


def _gather_weights(blocks):
    names = list(blocks)
    n = len(names)

    def body(*refs):
        ins, outs = refs[:n], refs[n:2 * n]
        ssem, rsem, lsem = refs[2 * n:]
        px, py, pc = _mesh_pos()
        me = 2 * px + py
        copies = []
        for i, nm in enumerate(names):
            if nm in BIG:
                hr = blocks[nm].shape[0] // 2
                src, dst = ins[i].at[pl.ds(pl.multiple_of(pc * hr, 16), hr), :], _slab(outs[i], nm, me, pc)
            else:
                src, dst = ins[i], _slab(outs[i], nm, me)
            copies.append(pltpu.make_async_copy(src, dst, lsem.at[i]))
            for k, (qx, qy) in enumerate(_chip_peers(px, py)):
                copies.append(pltpu.make_async_remote_copy(src, dst, ssem.at[3 * i + k], rsem.at[3 * i + k],
                                                           device_id=(qx, qy, pc), device_id_type=MESH))
        _run_copies(copies)

    outs = pl.pallas_call(
        body, name="gather_weights", in_specs=[ANY] * n, out_specs=[ANY] * n,
        out_shape=[jax.ShapeDtypeStruct(GATHER[nm][0], blocks[nm].dtype) for nm in names],
        scratch_shapes=[pltpu.SemaphoreType.DMA((3 * n,)), pltpu.SemaphoreType.DMA((3 * n,)), pltpu.SemaphoreType.DMA((n,))],
    )(*[blocks[nm] for nm in names])
    return dict(zip(names, outs))


def _gather_pair(full, names, call_name):
    n = len(names)

    def body(*refs):
        ins, outs = refs[:n], refs[n:2 * n]
        ssem, rsem = refs[2 * n:]
        px, py, pc = _mesh_pos()
        copies = []
        for i, nm in enumerate(names):
            for j in range(4):
                copies.append(pltpu.make_async_remote_copy(_slab(ins[i], nm, j, pc), _slab(outs[i], nm, j, pc),
                                                           ssem.at[4 * i + j], rsem.at[4 * i + j],
                                                           device_id=(px, py, 1 - pc), device_id_type=MESH))
        _run_copies(copies)

    outs = pl.pallas_call(
        body, name=call_name, in_specs=[ANY] * n, out_specs=[ANY] * n,
        out_shape=[jax.ShapeDtypeStruct(full[nm].shape, full[nm].dtype) for nm in names],
        input_output_aliases={i: i for i in range(n)},
        scratch_shapes=[pltpu.SemaphoreType.DMA((4 * n,)), pltpu.SemaphoreType.DMA((4 * n,))],
    )(*[full[nm] for nm in names])
    return dict(zip(names, outs))


def _grads_to_sibling(G, names, call_name, small=None):
    n = len(names)
    ns = 0 if small is None else 1

    def body(*refs):
        g_refs, o_refs = refs[:n + ns], refs[n + ns:2 * (n + ns)]
        ssem, rsem = refs[2 * (n + ns):]
        px, py, pc = _mesh_pos()
        sib = (px, py, 1 - pc)
        copies = []
        for i, nm in enumerate(names):
            for j in range(4):
                copies.append(pltpu.make_async_remote_copy(_slab(g_refs[i], nm, j, 1 - pc), o_refs[i].at[j],
                                                           ssem.at[4 * i + j], rsem.at[4 * i + j],
                                                           device_id=sib, device_id_type=MESH))
        if ns:
            copies.append(pltpu.make_async_remote_copy(g_refs[n], o_refs[n], ssem.at[4 * n], rsem.at[4 * n],
                                                       device_id=sib, device_id_type=MESH))
        _run_copies(copies)

    outs = pl.pallas_call(
        body, name=call_name, in_specs=[ANY] * (n + ns), out_specs=[ANY] * (n + ns),
        out_shape=[jax.ShapeDtypeStruct((4,) + _half_shape(nm), f32) for nm in names]
        + [jax.ShapeDtypeStruct((SMALL_ROWS, PACK_W), f32)] * ns,
        scratch_shapes=[pltpu.SemaphoreType.DMA((4 * n + ns,)), pltpu.SemaphoreType.DMA((4 * n + ns,))],
    )(*[G[nm] for nm in names], *([small] * ns))
    return dict(zip(names, outs[:n])), (outs[n] if ns else None)


def _pair_add(G, recv, names, call_name, small=None, small_recv=None):
    n = len(names)
    ns = 0 if small is None else 1
    cidx = lax.axis_index("c").astype(jnp.int32).reshape(1)

    def body(c_ref, *refs):
        ins, outs = refs[:2 * (n + ns)], refs[2 * (n + ns):]
        for i in range(n):
            outs[i][...] = (ins[i][...] + ins[n + ns + i][...]).astype(bf16)
        if ns:
            outs[n][...] = ins[n][...] + ins[2 * n + 1][...]

    g_specs, r_specs = [], []
    for nm in names:
        hr, hc = _half_shape(nm)
        if GATHER[nm][1] == 0:
            g_specs.append(pl.BlockSpec((hr // 2, hc), lambda j, i, c: ((2 * j + c[0]) * 2 + i, 0)))
        else:
            g_specs.append(pl.BlockSpec((hr // 2, hc), lambda j, i, c: (2 * c[0] + i, j)))
        r_specs.append(pl.BlockSpec((1, hr // 2, hc), lambda j, i, c: (j, i, 0)))
    sm = [pl.BlockSpec((SMALL_ROWS // 8, PACK_W), lambda j, i, c: (2 * j + i, 0))] * ns
    outs = pl.pallas_call(
        body, name=call_name,
        grid_spec=pltpu.PrefetchScalarGridSpec(num_scalar_prefetch=1, grid=(4, 2), in_specs=g_specs + sm + r_specs + sm,
                                               out_specs=r_specs + sm),
        out_shape=[jax.ShapeDtypeStruct((4,) + _half_shape(nm), bf16) for nm in names]
        + [jax.ShapeDtypeStruct((SMALL_ROWS, PACK_W), f32)] * ns,
        compiler_params=pltpu.CompilerParams(vmem_limit_bytes=VMEM_LIMIT),
    )(cidx, *[G[nm] for nm in names], *([small] * ns), *[recv[nm] for nm in names], *([small_recv] * ns))
    return dict(zip(names, outs[:n])), (outs[n] if ns else None)


def _exchange_copies(ins, outs, ssem, rsem):
    px, py, pc = _mesh_pos()
    me = 2 * px + py
    return [pltpu.make_async_remote_copy(ins[i].at[2 * qx + qy], outs[i].at[me], ssem.at[3 * i + k], rsem.at[3 * i + k],
                                         device_id=(qx, qy, pc), device_id_type=MESH)
            for i in range(len(ins)) for k, (qx, qy) in enumerate(_chip_peers(px, py))]


def _grads_chip_exchange(chip_sum, names, small):
    n = len(names)

    def body(*refs):
        ins, outs = refs[:n + 1], refs[n + 1:2 * n + 2]
        ssem, rsem, ssem_s, rsem_s = refs[2 * n + 2:]
        px, py, pc = _mesh_pos()
        me = 2 * px + py
        copies = _exchange_copies(ins[:n], outs[:n], ssem, rsem)
        hs = SMALL_ROWS // 2
        mine = ins[n].at[pl.ds(pl.multiple_of(pc * hs, 8), hs), :]
        copies += [pltpu.make_async_remote_copy(mine, outs[n].at[me], ssem_s.at[k], rsem_s.at[k],
                                                device_id=(qx, qy, pc), device_id_type=MESH)
                   for k, (qx, qy) in enumerate(_chip_peers(px, py))]
        _run_copies(copies)

    outs = pl.pallas_call(
        body, name="grads_chip_exchange", in_specs=[ANY] * (n + 1), out_specs=[ANY] * (n + 1),
        out_shape=[jax.ShapeDtypeStruct(chip_sum[nm].shape, chip_sum[nm].dtype) for nm in names]
        + [jax.ShapeDtypeStruct((4, SMALL_ROWS // 2, PACK_W), f32)],
        scratch_shapes=[pltpu.SemaphoreType.DMA((3 * n,)), pltpu.SemaphoreType.DMA((3 * n,)),
                        pltpu.SemaphoreType.DMA((3,)), pltpu.SemaphoreType.DMA((3,))],
    )(*[chip_sum[nm] for nm in names], small)
    return dict(zip(names, outs[:n])), outs[n]


def _sum_slots(slots, chip_sum, small4, small_own):
    n = len(BIG)
    me = jnp.stack([2 * lax.axis_index("x") + lax.axis_index("y"), lax.axis_index("c")]).astype(jnp.int32)

    def body(me_ref, *refs):
        for i in range(n + 1):
            own = refs[5 * i + 4][...].astype(f32)
            own = own[0] if i < n else own
            term = [jnp.where(me_ref[0] == k, own, refs[5 * i + k][0].astype(f32)) for k in range(4)]
            refs[5 * (n + 1) + i][...] = ((term[0] + term[1]) + term[2]) + term[3]

    redirect = lambda k: (lambda i, m: (jnp.where(m[0] == k, (k + 1) % 4, k), i, 0))
    in_specs, args, specs_out, shapes = [], [], [], []
    for nm in BIG:
        hr, hc = _half_shape(nm)
        in_specs += [pl.BlockSpec((1, hr // 2, hc), redirect(k)) for k in range(4)]
        in_specs.append(pl.BlockSpec((1, hr // 2, hc), lambda i, m: (m[0], i, 0)))
        args += [slots[nm]] * 4 + [chip_sum[nm]]
        specs_out.append(pl.BlockSpec((hr // 2, hc), lambda i, m: (i, 0)))
        shapes.append(jax.ShapeDtypeStruct((hr, hc), f32))
    in_specs += [pl.BlockSpec((1, SMALL_ROWS // 4, PACK_W), redirect(k)) for k in range(4)]
    in_specs.append(pl.BlockSpec((SMALL_ROWS // 4, PACK_W), lambda i, m: (2 * m[1] + i, 0)))
    args += [small4] * 4 + [small_own]
    specs_out.append(pl.BlockSpec((SMALL_ROWS // 4, PACK_W), lambda i, m: (i, 0)))
    shapes.append(jax.ShapeDtypeStruct((SMALL_ROWS // 2, PACK_W), f32))
    outs = pl.pallas_call(
        body, name="grads_chip_sum",
        grid_spec=pltpu.PrefetchScalarGridSpec(num_scalar_prefetch=1, grid=(2,), in_specs=in_specs, out_specs=specs_out),
        out_shape=shapes, compiler_params=pltpu.CompilerParams(vmem_limit_bytes=VMEM_LIMIT),
    )(me, *args)
    return dict(zip(BIG, outs[:n])), outs[n]


def _halves_to_sibling(half):
    names = list(half)
    n = len(names)

    def body(*refs):
        ins, outs = refs[:n], refs[n:2 * n]
        ssem, rsem = refs[2 * n:]
        px, py, pc = _mesh_pos()
        _run_copies([pltpu.make_async_remote_copy(ins[i], outs[i], ssem.at[i], rsem.at[i],
                                                  device_id=(px, py, 1 - pc), device_id_type=MESH) for i in range(n)])

    outs = pl.pallas_call(
        body, name="grads_halves_to_sibling", in_specs=[ANY] * n, out_specs=[ANY] * n,
        out_shape=[jax.ShapeDtypeStruct(half[nm].shape, f32) for nm in names],
        scratch_shapes=[pltpu.SemaphoreType.DMA((n,)), pltpu.SemaphoreType.DMA((n,))],
    )(*[half[nm] for nm in names])
    return dict(zip(names, outs))


def _join_halves(mine, other, pc):
    hr = mine.shape[0]
    return lax.dynamic_slice_in_dim(jnp.concatenate([other, mine, other], axis=0), (1 - pc) * hr, 2 * hr, axis=0)


def _flat_pad(v):
    v = v.reshape(-1)
    return jnp.pad(v, (0, _ceil_to(v.shape[0], PACK_W) - v.shape[0]))


def _pack_rows(parts, rows):
    flat = jnp.concatenate([_flat_pad(p) for p in parts])
    return jnp.pad(flat, (0, rows * PACK_W - flat.shape[0])).reshape(rows, PACK_W)


def _unpack_rows(buf, shapes):
    flat = buf.reshape(-1)
    out, off = [], 0
    for shp in shapes:
        n = 1
        for d in shp:
            n *= d
        out.append(flat[off:off + n].reshape(shp))
        off += _ceil_to(n, PACK_W)
    return out


def _adamw_math(w_, g_, m_, v_):
    m2 = ADAM_B1 * m_ + (1.0 - ADAM_B1) * g_
    v2 = ADAM_B2 * v_ + (1.0 - ADAM_B2) * (g_ * g_)
    m_hat = m2 / (1.0 - ADAM_B1 ** ADAM_STEP)
    v_hat = v2 / (1.0 - ADAM_B2 ** ADAM_STEP)
    return -ADAM_LR * (m_hat / (jnp.sqrt(v_hat) + ADAM_EPS) + ADAM_WD * w_), m2, v2


def _adamw(groups):
    ng = len(groups)

    def body(*refs):
        ins, outs = refs[:4 * ng], refs[4 * ng:]
        for i in range(ng):
            res = _adamw_math(*(r[...] for r in ins[4 * i:4 * i + 4]))
            for ref, val in zip(outs[3 * i:3 * i + 3], res):
                ref[...] = val

    in_specs, out_specs, out_shape = [], [], []
    for grp in groups:
        R, Cn = grp[0].shape
        spec = pl.BlockSpec((R // 8, Cn), lambda i: (i, 0))
        in_specs += [spec] * 4
        out_specs += [spec] * 3
        out_shape += [jax.ShapeDtypeStruct((R, Cn), f32)] * 3
    outs = pl.pallas_call(
        body, name="adamw", grid=(8,), in_specs=in_specs, out_specs=out_specs, out_shape=out_shape,
        compiler_params=pltpu.CompilerParams(vmem_limit_bytes=VMEM_LIMIT),
    )(*[a for grp in groups for a in grp])
    return [tuple(outs[3 * i:3 * i + 3]) for i in range(ng)]


def _forward_backward(x, tgt, W, S, late):
    L = x.shape[0]
    TM, TMW, TS = 256, 128, 512
    row = lambda c, dt=f32: (c, dt)
    hid = jnp.arange(RWKV_W) // HEAD
    E = (hid[:, None] == hid[None, :]).astype(f32)
    seg = (jnp.arange(S5_N)[None, :] // S5_P == jnp.arange(S5_G)[:, None]).astype(f32)

    w_in_t = W['w_in']
    w_p, w_u, w_g = w_in_t[:N_RWKV], w_in_t[N_RWKV:N_RWKV + S5_W], w_in_t[N_RWKV + S5_W:]
    zpad = jnp.zeros((64, RWKV_W), f32)
    w2p = jnp.concatenate([W['rwkv_w2'], zpad], axis=0)
    a2p = jnp.concatenate([zpad, W['rwkv_a2']], axis=0)
    g2 = W['rwkv_g2']
    prep_consts = [S['rwkv_shift_mu'], S['rwkv_w0'], S['rwkv_a0'], S['rwkv_k_k'], S['rwkv_k_a'], w2p, a2p, g2, E]
    out_consts = [S['rwkv_lnx_w'], S['rwkv_lnx_b'], S['rwkv_r_k'], E]
    cw, cb = W['ffn_conv_w'][:3], S['ffn_conv_b']

    a_re, a_im = S['s5_a_re'].reshape(S5_N, 1), S['s5_a_im'].reshape(S5_N, 1)
    ls = jnp.repeat(S['s5_log_step'].reshape(S5_G, 1), S5_P, axis=0)
    b_re, b_im = S['s5_b_re'].reshape(S5_N, S5_C), S['s5_b_im'].reshape(S5_N, S5_C)
    ar, ai, bbr, bbi = _s5_disc_fwd(a_re, a_im, ls, b_re, b_im)
    abar = jnp.concatenate([ar.reshape(1, S5_N), ai.reshape(1, S5_N)], axis=1)
    eye8 = jnp.eye(8, dtype=f32)

    def blocks_in(bb):
        t = bb.reshape(4, 8, S5_P, S5_C).transpose(0, 1, 3, 2)
        return (t[:, :, :, None, :] * eye8[None, :, None, :, None]).reshape(4, _CB, _SB)

    def blocks_out(cc):
        t = cc.reshape(4, 8, S5_C, S5_P).transpose(0, 1, 3, 2)
        return (t[:, :, :, None, :] * eye8[None, :, None, :, None]).reshape(4, _SB, _CB)

    def undiag_in(blocks):
        t = blocks.reshape(4, 8, S5_C, 8, S5_P)
        t = jnp.sum(t * eye8[None, :, None, :, None], axis=3)
        return t.reshape(S5_G, S5_C, S5_P).transpose(0, 2, 1).reshape(S5_N, S5_C)

    def undiag_out(blocks):
        t = blocks.reshape(4, 8, S5_P, 8, S5_C)
        t = jnp.sum(t * eye8[None, :, None, :, None], axis=3)
        return t.reshape(S5_G, S5_P, S5_C).transpose(0, 2, 1)

    bmat = jnp.concatenate([blocks_in(bbr), blocks_in(bbi)], axis=0).astype(bf16)
    cmat = jnp.concatenate([blocks_out(S['s5_c_re'].reshape(S5_G, S5_C, S5_P)),
                            -blocks_out(S['s5_c_im'].reshape(S5_G, S5_C, S5_P))], axis=0).astype(bf16)

    g1, g2n, g3, g4 = S['norm_mix_pre'], S['norm_mix_post'], S['norm_ffn_pre'], S['norm_ffn_post']
    h1, p, u, gp = _rows_mm("in_proj", lambda R, C: _rms(R[0], C[0]), L, TM, [x], [g1], w_in_t, 'nt',
                            [N_RWKV, S5_W, 2 * D_MODEL])

    def prep_fn(i, n, R, P, X, C):
        q = R[0] + (_shift_down(R[0], P[0], i, 1) - R[0]) * C[0]
        return _prep(q, *C[1:]), ()

    r, lw, k2, v, an, bv, g = _rowcall("rwkv_prep", prep_fn, L, TS, [p], prep_consts,
                                       out_rows=[row(RWKV_W)] * 7, prev=[0])
    y, ck, xinv = _wkv7_fwd(r, lw, k2, v, an, bv)
    o_a, o_r = _rows_mm("rwkv_out", lambda R, C: _rwkv_out(*R, *C), L, TS, [y, r, k2, v, g], out_consts,
                        W['w_branch_rwkv'], 'nn', [D_MODEL])

    st, ysc, got = _s5_fwd(u, bmat, cmat, abar, late)
    W = {**W, **_gather_pair(got, list(got), "gather_weights_pair_late")}
    def mid_fn(R, C):
        yg_ = _s5_mid(*R, *C)
        return yg_, [yg_]

    _, yg, z2 = _rows_mm("s5_mid", mid_fn, L, TS, [ysc, u], [S['s5_d']], W['s5_w_glu'], 'nn', [S5_W], extra=[row(S5_W)])
    o_b, o_s = _rows_mm("s5_glu", lambda R, C: _s5_glu(*R, *C), L, TS, [yg, z2], [S['s5_b_glu']],
                        W['w_branch_s5'], 'nn', [D_MODEL])

    merged, mixed = _rows_mm("merge_out", lambda R, C: _merge(*R, *C), L, TS, [gp, o_r, o_s], [S['b_gate']],
                             W['w_out'], 'nn', [D_MODEL])

    def resid_fn(R, C):
        x1_ = R[0] + _rms(R[1], C[0])
        return _rms(x1_, C[1]), [x1_]

    h2, x1, z = _rows_mm("resid_up", resid_fn, L, TM, [x, mixed], [g2n, g3], W['ffn_w_up'], 'nn', [2 * D_FF],
                         extra=[row(D_MODEL)])

    def conv(zt, zprev, i, cw_, cb_):
        z2s, z1s = _shift_down(zt, zprev, i, 2), _shift_down(zt, zprev, i, 1)
        return cb_ + cw_[0:1] * z2s + cw_[1:2] * z1s + cw_[2:3] * zt, z2s, z1s

    (act,) = _rowcall("conv_act", lambda i, n, R, P, X, C: ((_act(conv(R[0], P[0], i, C[0], C[1])[0]),), ()), L, TMW,
                      [z], [cw, cb], out_rows=[row(D_FF, bf16)], prev=[0])
    f = _mm(act, W['ffn_w_down'], 'nn', "mm_down")

    def final_fn(i, n, R, P, X, C):
        x1_, f_, t_ = R
        fn_, vjp = jax.vjp(_rms, f_, C[0])
        diff = x1_ + fn_ - t_
        loss = jnp.sum(diff * diff) * (0.5 / D_MODEL)
        dx2_ = diff * (1.0 / D_MODEL)
        df_, dg4_ = vjp(dx2_)
        return (df_, dx2_), (jnp.full((1, PACK_W), loss, f32), dg4_)

    df, dx2, loss, dg4 = _rowcall("loss_head", final_fn, L, TS, [x1, f, tgt], [g4],
                                  out_rows=[row(D_MODEL, bf16), row(D_MODEL)], out_accs=[(1, PACK_W), (1, D_MODEL)])
    G = {'norm_ffn_post': dg4}

    dact = _mm(df, W['ffn_w_down'], 'nt', "mm_down_dx")
    G['ffn_w_down'] = _mm(act, df, 'tn', "mm_down_dw")

    def conv_bwd_fn(i, n, R, P, X, C):
        z_, dact_ = R
        cw_, cb_ = C
        zc, z2s, z1s = conv(z_, P[0], i, cw_, cb_)
        _, vjp = jax.vjp(_act, zc)
        (dzc_,) = vjp(dact_)
        last8 = z_[z_.shape[0] - 8:]
        zcn = cb_ + cw_[0:1] * _shift_down(X[0], last8, 1, 2) + cw_[1:2] * _shift_down(X[0], last8, 1, 1) + cw_[2:3] * X[0]
        _, vjpn = jax.vjp(_act, zcn)
        (dzcn,) = vjpn(X[1])
        dz_ = (cw_[2:3] * dzc_ + cw_[1:2] * _shift_up(dzc_, dzcn, i, n, 1) + cw_[0:1] * _shift_up(dzc_, dzcn, i, n, 2))
        return (dz_,), (_sum0(dzc_), _sum0(dzc_ * z2s), _sum0(dzc_ * z1s), _sum0(dzc_ * z_))

    wide = (1, 2 * D_FF)
    dz, dcb, dcw0, dcw1, dcw2 = _rowcall("conv_act_bwd", conv_bwd_fn, L, TMW, [z, dact], [cw, cb],
                                         out_rows=[row(2 * D_FF, bf16)], out_accs=[wide] * 4, prev=[0], nxt=[0, 1])
    G['ffn_conv_b'] = dcb
    G['ffn_conv_w'] = jnp.concatenate([dcw0, dcw1, dcw2], axis=0)
    dh2 = _mm(dz, W['ffn_w_up'], 'nt', "mm_up_dx")
    G['ffn_w_up'] = _mm(h2, dz, 'tn', "mm_up_dw")

    def norm2_bwd_fn(i, n, R, P, X, C):
        x1_, mixed_, dx2_, dh2_ = R
        _, vjp3 = jax.vjp(_rms, x1_, C[1])
        dx1a, dg3_ = vjp3(dh2_)
        dx1_ = dx2_ + dx1a
        _, vjp2 = jax.vjp(_rms, mixed_, C[0])
        dmixed_, dg2_ = vjp2(dx1_)
        return (dx1_, dmixed_), (dg2_, dg3_)

    dx1, dmixed, dg2n, dg3 = _rowcall("norm_mid_bwd", norm2_bwd_fn, L, TS, [x1, mixed, dx2, dh2], [g2n, g3],
                                      out_rows=[row(D_MODEL), row(D_MODEL, bf16)], out_accs=[(1, D_MODEL)] * 2)
    G['norm_mix_post'], G['norm_ffn_pre'] = dg2n, dg3

    dmerged = _mm(dmixed, W['w_out'], 'nt', "mm_out_dx")
    G['w_out'] = _mm(merged, dmixed, 'tn', "mm_out_dw")

    def merge_bwd_fn(i, n, R, P, X, C):
        _, vjp = jax.vjp(_merge, R[0], R[1], R[2], C[0])
        dgp_, do_r_, do_s_, dbg_ = vjp(R[3])
        return (dgp_, do_r_, do_s_), (dbg_,)

    dgp, do_r, do_s, G['b_gate'] = _rowcall("merge_bwd", merge_bwd_fn, L, TS, [gp, o_r, o_s, dmerged], [S['b_gate']],
                                            out_rows=[row(2 * D_MODEL, bf16), row(D_MODEL, bf16), row(D_MODEL, bf16)],
                                            out_accs=[(1, 2 * D_MODEL)])
    do_a = _mm(do_r, W['w_branch_rwkv'], 'nt', "mm_br_dx")
    G['w_branch_rwkv'] = _mm(o_a, do_r, 'tn', "mm_br_dw")
    do_b = _mm(do_s, W['w_branch_s5'], 'nt', "mm_bs_dx")
    G['w_branch_s5'] = _mm(o_b, do_s, 'tn', "mm_bs_dw")

    def glu_bwd_fn(i, n, R, P, X, C):
        _, vjp = jax.vjp(_s5_glu, R[0], R[1], C[0])
        dyg1_, dz2_, dbg_ = vjp(R[2])
        return (dyg1_, dz2_), (dbg_,)

    dyg1, dz2, G['s5_b_glu'] = _rowcall("s5_glu_bwd", glu_bwd_fn, L, TS, [yg, z2, do_b], [S['s5_b_glu']],
                                        out_rows=[row(S5_W), row(S5_W, bf16)], out_accs=[(1, S5_W)])
    dyg2 = _mm(dz2, W['s5_w_glu'], 'nt', "mm_glu_dx")
    G['s5_w_glu'] = _mm(yg, dz2, 'tn', "mm_glu_dw")

    def mid_bwd_fn(i, n, R, P, X, C):
        _, vjp = jax.vjp(_s5_mid, R[0], R[1], C[0])
        dysc_, du_, dd_ = vjp(R[2] + R[3])
        return (dysc_, du_), (dd_,)

    dysc, du1, G['s5_d'] = _rowcall("s5_mid_bwd", mid_bwd_fn, L, TS, [ysc, u, dyg1, dyg2], [S['s5_d']],
                                    out_rows=[row(S5_W, bf16), row(S5_W)], out_accs=[(1, S5_W)])
    early = [n for n in BIG if n != 'w_in']
    recv_e, _ = _grads_to_sibling(G, early, "grads_to_sibling_early")
    chip_e, _ = _pair_add(G, recv_e, early, "grads_pair_sum_early")
    du, dbmat, dcmat, dabar, slots_e = _s5_bwd(dysc, st, u, du1, bmat, cmat, abar, chip_e)
    da_re, da_im, dls, db_re, db_im = _s5_disc_bwd(
        a_re, a_im, ls, b_re, b_im, dabar[:, :S5_N].reshape(S5_N, 1), dabar[:, S5_N:].reshape(S5_N, 1),
        undiag_in(dbmat[:4]), undiag_in(dbmat[4:]), seg)
    G['s5_a_re'], G['s5_a_im'], G['s5_log_step'] = da_re, da_im, dls
    G['s5_b_re'], G['s5_b_im'] = db_re, db_im
    G['s5_c_re'], G['s5_c_im'] = undiag_out(dcmat[:4]), -undiag_out(dcmat[4:])

    def out_bwd_fn(i, n, R, P, X, C):
        _, vjp = jax.vjp(_rwkv_out, *R[:5], *C)
        gs = vjp(R[5])
        return gs[:5], gs[5:8]

    dy, dr1, dk1, dv1, dg, dlw, dlb, drk = _rowcall("rwkv_out_bwd", out_bwd_fn, L, TM, [y, r, k2, v, g, do_a], out_consts,
                                                    out_rows=[row(RWKV_W)] * 5, out_accs=[(1, RWKV_W)] * 3)
    G['rwkv_lnx_w'], G['rwkv_lnx_b'], G['rwkv_r_k'] = dlw, dlb, drk
    dr2, dlwk, dk2b, dv2, dan, dbv = _wkv7_bwd(r, lw, k2, v, an, bv, ck, xinv, dy)

    def prep_bwd_fn(i, n, R, P, X, C):
        p_ = R[0]
        d1 = _shift_down(p_, P[0], i, 1) - p_
        q = p_ + d1 * C[0]
        _, vjp = jax.vjp(_prep, q, *C[1:])
        cots = (R[1] + R[2], R[3], R[4] + R[5], R[6] + R[7], R[8], R[9], R[10])
        gs = vjp(cots)
        return (gs[0],), (_sum0(gs[0] * d1),) + tuple(gs[1:8])

    small, lowr = (1, RWKV_W), (128, RWKV_W)
    dq, dmu, dw0, da0, dkk, dka, dw2p, da2p, dg2 = _rowcall(
        "rwkv_prep_bwd", prep_bwd_fn, L, TM, [p, dr1, dr2, dlwk, dk1, dk2b, dv1, dv2, dan, dbv, dg],
        prep_consts, out_rows=[row(N_RWKV)], out_accs=[(1, N_RWKV)] + [small] * 4 + [lowr] * 3, prev=[0])
    G['rwkv_shift_mu'], G['rwkv_w0'], G['rwkv_a0'], G['rwkv_k_k'], G['rwkv_k_a'] = dmu, dw0, da0, dkk, dka
    G['rwkv_w2'], G['rwkv_a2'], G['rwkv_g2'] = dw2p[:64], da2p[64:], dg2

    def shift_bwd_fn(i, n, R, P, X, C):
        dm = R[0] * C[0]
        return (R[0] - dm + _shift_up(dm, X[0] * C[0], i, n, 1),), ()

    (dp,) = _rowcall("shift_bwd", shift_bwd_fn, L, TS, [dq], [S['rwkv_shift_mu']], out_rows=[row(N_RWKV, bf16)], nxt=[0])

    dproj = jnp.concatenate([dp, du, dgp], axis=1)
    dh1 = _mm(dproj, w_in_t, 'nn', "mm_in_dx")
    G['w_in'] = _mm(dproj, h1, 'tn', "mm_in_dw")

    def norm1_bwd_fn(i, n, R, P, X, C):
        _, vjp = jax.vjp(_rms, R[0], C[0])
        dxa, dg1_ = vjp(R[2])
        return (R[1] + dxa,), (dg1_,)

    dx, G['norm_mix_pre'] = _rowcall("norm_pre_bwd", norm1_bwd_fn, L, TS, [x, dx1, dh1], [g1],
                                     out_rows=[row(D_MODEL)], out_accs=[(1, D_MODEL)])
    return loss, dx, G, chip_e, slots_e


def kernel(x, norm_mix_pre, norm_mix_post, norm_ffn_pre, norm_ffn_post, w_in, b_gate, rwkv_shift_mu, rwkv_w0, rwkv_w2, rwkv_a0, rwkv_a2, rwkv_g2, rwkv_k_k, rwkv_k_a, rwkv_r_k, rwkv_lnx_w, rwkv_lnx_b, s5_a_re, s5_a_im, s5_b_re, s5_b_im, s5_c_re, s5_c_im, s5_d, s5_log_step, s5_w_glu, s5_b_glu, w_branch_rwkv, w_branch_s5, w_out, ffn_w_up, ffn_conv_w, ffn_conv_b, ffn_w_down, loss_target, m_norm_mix_pre, m_norm_mix_post, m_norm_ffn_pre, m_norm_ffn_post, m_w_in, m_b_gate, m_rwkv_shift_mu, m_rwkv_w0, m_rwkv_w2, m_rwkv_a0, m_rwkv_a2, m_rwkv_g2, m_rwkv_k_k, m_rwkv_k_a, m_rwkv_r_k, m_rwkv_lnx_w, m_rwkv_lnx_b, m_s5_a_re, m_s5_a_im, m_s5_b_re, m_s5_b_im, m_s5_c_re, m_s5_c_im, m_s5_d, m_s5_log_step, m_s5_w_glu, m_s5_b_glu, m_w_branch_rwkv, m_w_branch_s5, m_w_out, m_ffn_w_up, m_ffn_conv_w, m_ffn_conv_b, m_ffn_w_down, v_norm_mix_pre, v_norm_mix_post, v_norm_ffn_pre, v_norm_ffn_post, v_w_in, v_b_gate, v_rwkv_shift_mu, v_rwkv_w0, v_rwkv_w2, v_rwkv_a0, v_rwkv_a2, v_rwkv_g2, v_rwkv_k_k, v_rwkv_k_a, v_rwkv_r_k, v_rwkv_lnx_w, v_rwkv_lnx_b, v_s5_a_re, v_s5_a_im, v_s5_b_re, v_s5_b_im, v_s5_c_re, v_s5_c_im, v_s5_d, v_s5_log_step, v_s5_w_glu, v_s5_b_glu, v_w_branch_rwkv, v_w_branch_s5, v_w_out, v_ffn_w_up, v_ffn_conv_w, v_ffn_conv_b, v_ffn_w_down):
    A = dict(locals())
    me = 2 * lax.axis_index("x") + lax.axis_index("y")
    blk = lambda n: A[n][0]

    mine = {n: (blk(n).T if n == 'w_in' else blk(n)).astype(bf16) for n in BIG}
    mine.update({n: blk(n) for n in TINY})
    mine['ffn_conv_w'] = jnp.pad(blk('ffn_conv_w'), ((0, 5), (0, 0)))
    late = ['ffn_w_up', 'ffn_w_down']
    W = _gather_weights({n: blkv for n, blkv in mine.items() if n not in late})
    W.update(_gather_pair(W, [n for n in BIG if n not in late], "gather_weights_pair"))
    S = {n: A[n].reshape(1, -1) for n in SMALL}

    loss, dx, G, chip_e, slots_e = _forward_backward(x[0], loss_target[0], W, S, {n: mine[n] for n in late})

    tiny_shapes = [G[n].shape for n in TINY]
    small_buf = _pack_rows([G[n] for n in SMALL] + [G[n] for n in TINY] + [loss], SMALL_ROWS)
    recv, small_recv = _grads_to_sibling(G, ['w_in'], "grads_to_sibling", small_buf)
    chip_l, small_sum = _pair_add(G, recv, ['w_in'], "grads_pair_sum", small_buf, small_recv)
    slots_l, small4 = _grads_chip_exchange(chip_l, ['w_in'], small_sum)
    half, half['small'] = _sum_slots({**slots_e, **slots_l}, {**chip_e, **chip_l}, small4, small_sum)
    other = _halves_to_sibling(half)
    pc = lax.axis_index("c")
    small_tot = _join_halves(half['small'], other['small'], pc)
    grad = {n: _join_halves(half[n], other[n], pc) for n in BIG}
    grad['w_in'] = grad['w_in'].T
    vals = _unpack_rows(small_tot, [A[n].shape for n in SMALL] + tiny_shapes + [(1, PACK_W)])
    grad.update(zip(SMALL, vals))
    for n, full in zip(TINY, vals[len(SMALL):]):
        cs = A[n].shape[2]
        grad[n] = lax.dynamic_slice_in_dim(full, me * cs, cs, axis=1)
    loss_out = vals[-1][0, 0]

    packed = SMALL + TINY
    groups = [(blk(n), grad[n], blk('m_' + n), blk('v_' + n)) for n in BIG]
    groups.append(tuple(_pack_rows([src(n) for n in packed], ADAM_ROWS)
                        for src in (lambda n: A[n], lambda n: grad[n], lambda n: A['m_' + n], lambda n: A['v_' + n])))
    res = _adamw(groups)
    outs = [dict(), dict(), dict()]
    for n, r3 in zip(BIG, res[:-1]):
        for d, val in zip(outs, r3):
            d[n] = val
    for d, buf in zip(outs, res[-1]):
        d.update(zip(packed, _unpack_rows(buf, [A[n].shape for n in packed])))
    full = lambda d: [d[n].reshape(A[n].shape) for n in WEIGHTS]
    return (loss_out, dx[None], *full(grad), *full(outs[0]), *full(outs[1]), *full(outs[2]))
```
